```python
import math
import jax
import jax.numpy as jnp
from jax import lax
import numpy as np

D_MODEL = 1024
BATCH = 8
SEQ = 2048
DEPTH = 2

S5_WIDTH = D_MODEL
S5_GROUP = 16
S5_GROUPS = S5_WIDTH // S5_GROUP
S5_STATE = 64
SSD_HEAD_DIM = 64
SSD_WIDTH = D_MODEL
SSD_HEADS = SSD_WIDTH // SSD_HEAD_DIM
SSD_GROUPS = 2
SSD_STATE = 128
SSD_CONV = 4
SSD_CHUNK = 128
SSD_CONV_DIM = SSD_WIDTH + 2 * SSD_GROUPS * SSD_STATE
MIX_WIDTH = S5_WIDTH + SSD_WIDTH
IN_PROJ = S5_WIDTH + SSD_WIDTH + SSD_CONV_DIM + SSD_HEADS
IN_SPLITS = (S5_WIDTH, S5_WIDTH + SSD_WIDTH, S5_WIDTH + SSD_WIDTH + SSD_CONV_DIM)
FFN_HIDDEN = ((8 * D_MODEL + 3 * 256 - 1) // (3 * 256)) * 256
EPS = 1e-6

kernel_name = 'hybrid_s5_ssd_parallel_heads'


def rmsnorm(x, g):
    xf = x.astype(jnp.float32)
    y = xf * lax.rsqrt(jnp.mean(xf * xf, axis=-1, keepdims=True) + EPS)
    return (y * g.astype(jnp.float32)).astype(x.dtype)


def s5_mixer(u, lam_re, lam_im, log_step, b_re, b_im, c_re, c_im, d_skip, w_glu, b_glu):
    bsz, L, _ = u.shape
    f32 = jnp.float32
    uf = u.astype(f32).reshape(bsz, L, S5_GROUPS, S5_GROUP)
    step = jnp.exp(log_step.astype(f32))[:, None]
    lr = lam_re.astype(f32)
    li = lam_im.astype(f32)
    mag = jnp.exp(lr * step)
    ang = li * step
    abar_re = mag * jnp.cos(ang)
    abar_im = mag * jnp.sin(ang)
    den = lr * lr + li * li
    nr = abar_re - 1.0
    ni = abar_im
    coef_re = (nr * lr + ni * li) / den
    coef_im = (ni * lr - nr * li) / den
    bre = b_re.astype(f32)
    bim = b_im.astype(f32)
    bbar_re = coef_re[..., None] * bre - coef_im[..., None] * bim
    bbar_im = coef_re[..., None] * bim + coef_im[..., None] * bre
    bu_re = jnp.einsum('blgh,gph->blgp', uf, bbar_re)
    bu_im = jnp.einsum('blgh,gph->blgp', uf, bbar_im)
    a_re = jnp.broadcast_to(abar_re, (1, L, S5_GROUPS, S5_STATE))
    a_im = jnp.broadcast_to(abar_im, (1, L, S5_GROUPS, S5_STATE))

    def combine(e1, e2):
        a1r, a1i, b1r, b1i = e1
        a2r, a2i, b2r, b2i = e2
        return (a2r * a1r - a2i * a1i,
                a2r * a1i + a2i * a1r,
                a2r * b1r - a2i * b1i + b2r,
                a2r * b1i + a2i * b1r + b2i)

    _, _, xr, xi = lax.associative_scan(combine, (a_re, a_im, bu_re, bu_im), axis=1)
    y = (jnp.einsum('blgp,ghp->blgh', xr, c_re.astype(f32))
         - jnp.einsum('blgp,ghp->blgh', xi, c_im.astype(f32)))
    y = y.reshape(bsz, L, S5_WIDTH) + d_skip.astype(f32) * uf.reshape(bsz, L, S5_WIDTH)
    g = jax.nn.gelu(y)
    out = g * jax.nn.sigmoid(g @ w_glu.astype(f32) + b_glu.astype(f32))
    return out.astype(u.dtype)


def segsum(a):
    T = a.shape[-1]
    rep = jnp.broadcast_to(a[..., None], a.shape + (T,))
    strict = jnp.tril(jnp.ones((T, T), dtype=bool), -1)
    cs = jnp.cumsum(jnp.where(strict, rep, 0.0), axis=-2)
    incl = jnp.tril(jnp.ones((T, T), dtype=bool))
    return jnp.where(incl, cs, -jnp.inf)


def ssd_mixer(z, xbc, dt, conv_w, conv_b, dt_bias, a_log, d_skip):
    bsz, L, _ = xbc.shape
    f32 = jnp.float32
    nc = L // SSD_CHUNK
    R = SSD_HEADS // SSD_GROUPS
    xbc = lax.conv_general_dilated(
        xbc, conv_w[:, None, :].astype(xbc.dtype), window_strides=(1,),
        padding=[(SSD_CONV - 1, 0)], dimension_numbers=('NWC', 'WIO', 'NWC'),
        feature_group_count=SSD_CONV_DIM) + conv_b
    xbc = jax.nn.silu(xbc).astype(f32)
    xs, bs, cs = jnp.split(xbc, (SSD_WIDTH, SSD_WIDTH + SSD_GROUPS * SSD_STATE), axis=-1)
    dtp = jax.nn.softplus(dt.astype(f32) + dt_bias.astype(f32))
    A = -jnp.exp(a_log.astype(f32))
    dta = (dtp * A).reshape(bsz, nc, SSD_CHUNK, SSD_GROUPS, R).transpose(0, 3, 4, 1, 2)
    xh = xs.reshape(bsz, L, SSD_HEADS, SSD_HEAD_DIM)
    xdt = (xh * dtp[..., None]).reshape(bsz, nc, SSD_CHUNK, SSD_GROUPS, R, SSD_HEAD_DIM)
    bmat = bs.reshape(bsz, nc, SSD_CHUNK, SSD_GROUPS, SSD_STATE)
    cmat = cs.reshape(bsz, nc, SSD_CHUNK, SSD_GROUPS, SSD_STATE)
    a_cum = jnp.cumsum(dta, axis=-1)
    lmat = jnp.exp(segsum(dta))
    y_diag = jnp.einsum('bclgn,bcsgn,bgrcls,bcsgrp->bclgrp', cmat, bmat, lmat, xdt)
    decay_states = jnp.exp(a_cum[..., -1:] - a_cum)
    states = jnp.einsum('bclgn,bgrcl,bclgrp->bcgrpn', bmat, decay_states, xdt)
    chunk_tot = jnp.pad(a_cum[..., -1], ((0, 0), (0, 0), (0, 0), (1, 0)))
    decay_chunk = jnp.exp(segsum(chunk_tot))
    states0 = jnp.concatenate([jnp.zeros_like(states[:, :1]), states], axis=1)
    new_states = jnp.einsum('bgrzc,bcgrpn->bzgrpn', decay_chunk, states0)
    prev = new_states[:, :-1]
    y_off = jnp.einsum('bclgn,bcgrpn,bgrcl->bclgrp', cmat, prev, jnp.exp(a_cum))
    y = (y_diag + y_off).reshape(bsz, L, SSD_HEADS, SSD_HEAD_DIM) + d_skip.astype(f32)[:, None] * xh
    y = y.reshape(bsz, L, SSD_WIDTH) * jax.nn.silu(z.astype(f32))
    return y.astype(z.dtype)


def _fwd_setup_inputs(seed: int = 0) -> dict:
    key = jax.random.key(seed)
    ks = jax.random.split(key, 32)
    f32 = jnp.float32

    def nrm(k, shape, scale):
        return jax.random.normal(k, shape, f32) * scale

    def gain(k, n):
        return 1.0 + 0.01 * jax.random.normal(k, (DEPTH, n), f32)

    n_idx = jnp.arange(S5_STATE, dtype=f32)
    lam_re = -0.5 + 0.01 * jax.random.normal(ks[3], (DEPTH, S5_GROUPS, S5_STATE), f32)
    lam_im = (jnp.broadcast_to(math.pi * n_idx, (DEPTH, S5_GROUPS, S5_STATE))
              + 0.01 * jax.random.normal(ks[4], (DEPTH, S5_GROUPS, S5_STATE), f32))
    log_step = jax.random.uniform(ks[5], (DEPTH, S5_GROUPS), f32, math.log(1e-3), math.log(1e-1))
    dt0 = jnp.exp(jax.random.uniform(ks[17], (DEPTH, SSD_HEADS), f32, math.log(1e-3), math.log(1e-1)))
    dt_bias = dt0 + jnp.log(-jnp.expm1(-dt0))
    a_log = jnp.log(jax.random.uniform(ks[18], (DEPTH, SSD_HEADS), f32, 1.0, 16.0))
    return {
        'x': jax.random.normal(ks[0], (BATCH, SEQ, D_MODEL), f32),
        'norm_mix': gain(ks[1], D_MODEL),
        'w_in': nrm(ks[2], (DEPTH, D_MODEL, IN_PROJ), D_MODEL ** -0.5),
        's5_lam_re': lam_re,
        's5_lam_im': lam_im,
        's5_log_step': log_step,
        's5_b_re': nrm(ks[6], (DEPTH, S5_GROUPS, S5_STATE, S5_GROUP), (2 * S5_GROUP) ** -0.5),
        's5_b_im': nrm(ks[7], (DEPTH, S5_GROUPS, S5_STATE, S5_GROUP), (2 * S5_GROUP) ** -0.5),
        's5_c_re': nrm(ks[8], (DEPTH, S5_GROUPS, S5_GROUP, S5_STATE), (2 * S5_STATE) ** -0.5),
        's5_c_im': nrm(ks[9], (DEPTH, S5_GROUPS, S5_GROUP, S5_STATE), (2 * S5_STATE) ** -0.5),
        's5_d': nrm(ks[10], (DEPTH, S5_WIDTH), 1.0),
        's5_w_glu': nrm(ks[11], (DEPTH, S5_WIDTH, S5_WIDTH), S5_WIDTH ** -0.5),
        's5_b_glu': nrm(ks[12], (DEPTH, S5_WIDTH), 0.01),
        's5_norm': gain(ks[13], S5_WIDTH),
        'ssd_conv_w': nrm(ks[14], (DEPTH, SSD_CONV, SSD_CONV_DIM), SSD_CONV ** -0.5),
        'ssd_conv_b': nrm(ks[15], (DEPTH, SSD_CONV_DIM), 0.01),
        'ssd_dt_bias': dt_bias,
        'ssd_a_log': a_log,
        'ssd_d': 1.0 + 0.1 * jax.random.normal(ks[19], (DEPTH, SSD_HEADS), f32),
        'ssd_norm': gain(ks[20], SSD_WIDTH),
        'w_out': nrm(ks[21], (DEPTH, MIX_WIDTH, D_MODEL), MIX_WIDTH ** -0.5),
        'norm_ffn': gain(ks[22], D_MODEL),
        'w_gate': nrm(ks[23], (DEPTH, D_MODEL, FFN_HIDDEN), D_MODEL ** -0.5),
        'w_up': nrm(ks[24], (DEPTH, D_MODEL, FFN_HIDDEN), D_MODEL ** -0.5),
        'w_down': nrm(ks[25], (DEPTH, FFN_HIDDEN, D_MODEL), FFN_HIDDEN ** -0.5),
        'norm_final': 1.0 + 0.01 * jax.random.normal(ks[26], (D_MODEL,), f32),
    }


def _fwd_reference(x, norm_mix, w_in, s5_lam_re, s5_lam_im, s5_log_step, s5_b_re, s5_b_im,
              s5_c_re, s5_c_im, s5_d, s5_w_glu, s5_b_glu, s5_norm, ssd_conv_w, ssd_conv_b,
              ssd_dt_bias, ssd_a_log, ssd_d, ssd_norm, w_out, norm_ffn, w_gate, w_up,
              w_down, norm_final):
    for i in range(DEPTH):
        h = rmsnorm(x, norm_mix[i])
        proj = h @ w_in[i]
        u_a, z_b, xbc_b, dt_b = jnp.split(proj, IN_SPLITS, axis=-1)
        y_a = s5_mixer(u_a, s5_lam_re[i], s5_lam_im[i], s5_log_step[i], s5_b_re[i],
                       s5_b_im[i], s5_c_re[i], s5_c_im[i], s5_d[i], s5_w_glu[i], s5_b_glu[i])
        y_a = rmsnorm(y_a, s5_norm[i])
        y_b = ssd_mixer(z_b, xbc_b, dt_b, ssd_conv_w[i], ssd_conv_b[i], ssd_dt_bias[i],
                        ssd_a_log[i], ssd_d[i])
        y_b = rmsnorm(y_b, ssd_norm[i])
        x = x + jnp.concatenate([y_a, y_b], axis=-1) @ w_out[i]
        h = rmsnorm(x, norm_ffn[i])
        x = x + (jax.nn.silu(h @ w_gate[i]) * (h @ w_up[i])) @ w_down[i]
    return rmsnorm(x, norm_final)


import jax as _jax
import jax.numpy as _jnp

TWIN_FORMAT = 'train_step'
FWD_PARAMS = ['x', 'norm_mix', 'w_in', 's5_lam_re', 's5_lam_im', 's5_log_step', 's5_b_re', 's5_b_im', 's5_c_re', 's5_c_im', 's5_d', 's5_w_glu', 's5_b_glu', 's5_norm', 'ssd_conv_w', 'ssd_conv_b', 'ssd_dt_bias', 'ssd_a_log', 'ssd_d', 'ssd_norm', 'w_out', 'norm_ffn', 'w_gate', 'w_up', 'w_down', 'norm_final']
TWIN_WEIGHTS = ['norm_mix', 'w_in', 's5_lam_re', 's5_lam_im', 's5_log_step', 's5_b_re', 's5_b_im', 's5_c_re', 's5_c_im', 's5_d', 's5_w_glu', 's5_b_glu', 's5_norm', 'ssd_conv_w', 'ssd_conv_b', 'ssd_dt_bias', 'ssd_a_log', 'ssd_d', 'ssd_norm', 'w_out', 'norm_ffn', 'w_gate', 'w_up', 'w_down', 'norm_final']
TWIN_DIFF_INPUT = 'x'
TWIN_INPUTS = ['x', 'norm_mix', 'w_in', 's5_lam_re', 's5_lam_im', 's5_log_step', 's5_b_re', 's5_b_im', 's5_c_re', 's5_c_im', 's5_d', 's5_w_glu', 's5_b_glu', 's5_norm', 'ssd_conv_w', 'ssd_conv_b', 'ssd_dt_bias', 'ssd_a_log', 'ssd_d', 'ssd_norm', 'w_out', 'norm_ffn', 'w_gate', 'w_up', 'w_down', 'norm_final', 'loss_target', 'm_norm_mix', 'm_w_in', 'm_s5_lam_re', 'm_s5_lam_im', 'm_s5_log_step', 'm_s5_b_re', 'm_s5_b_im', 'm_s5_c_re', 'm_s5_c_im', 'm_s5_d', 'm_s5_w_glu', 'm_s5_b_glu', 'm_s5_norm', 'm_ssd_conv_w', 'm_ssd_conv_b', 'm_ssd_dt_bias', 'm_ssd_a_log', 'm_ssd_d', 'm_ssd_norm', 'm_w_out', 'm_norm_ffn', 'm_w_gate', 'm_w_up', 'm_w_down', 'm_norm_final', 'v_norm_mix', 'v_w_in', 'v_s5_lam_re', 'v_s5_lam_im', 'v_s5_log_step', 'v_s5_b_re', 'v_s5_b_im', 'v_s5_c_re', 'v_s5_c_im', 'v_s5_d', 'v_s5_w_glu', 'v_s5_b_glu', 'v_s5_norm', 'v_ssd_conv_w', 'v_ssd_conv_b', 'v_ssd_dt_bias', 'v_ssd_a_log', 'v_ssd_d', 'v_ssd_norm', 'v_w_out', 'v_norm_ffn', 'v_w_gate', 'v_w_up', 'v_w_down', 'v_norm_final']
TWIN_OUTPUTS = ['loss', 'grad_x', 'grad_norm_mix', 'grad_w_in', 'grad_s5_lam_re', 'grad_s5_lam_im', 'grad_s5_log_step', 'grad_s5_b_re', 'grad_s5_b_im', 'grad_s5_c_re', 'grad_s5_c_im', 'grad_s5_d', 'grad_s5_w_glu', 'grad_s5_b_glu', 'grad_s5_norm', 'grad_ssd_conv_w', 'grad_ssd_conv_b', 'grad_ssd_dt_bias', 'grad_ssd_a_log', 'grad_ssd_d', 'grad_ssd_norm', 'grad_w_out', 'grad_norm_ffn', 'grad_w_gate', 'grad_w_up', 'grad_w_down', 'grad_norm_final', 'delta_norm_mix', 'delta_w_in', 'delta_s5_lam_re', 'delta_s5_lam_im', 'delta_s5_log_step', 'delta_s5_b_re', 'delta_s5_b_im', 'delta_s5_c_re', 'delta_s5_c_im', 'delta_s5_d', 'delta_s5_w_glu', 'delta_s5_b_glu', 'delta_s5_norm', 'delta_ssd_conv_w', 'delta_ssd_conv_b', 'delta_ssd_dt_bias', 'delta_ssd_a_log', 'delta_ssd_d', 'delta_ssd_norm', 'delta_w_out', 'delta_norm_ffn', 'delta_w_gate', 'delta_w_up', 'delta_w_down', 'delta_norm_final', 'new_m_norm_mix', 'new_m_w_in', 'new_m_s5_lam_re', 'new_m_s5_lam_im', 'new_m_s5_log_step', 'new_m_s5_b_re', 'new_m_s5_b_im', 'new_m_s5_c_re', 'new_m_s5_c_im', 'new_m_s5_d', 'new_m_s5_w_glu', 'new_m_s5_b_glu', 'new_m_s5_norm', 'new_m_ssd_conv_w', 'new_m_ssd_conv_b', 'new_m_ssd_dt_bias', 'new_m_ssd_a_log', 'new_m_ssd_d', 'new_m_ssd_norm', 'new_m_w_out', 'new_m_norm_ffn', 'new_m_w_gate', 'new_m_w_up', 'new_m_w_down', 'new_m_norm_final', 'new_v_norm_mix', 'new_v_w_in', 'new_v_s5_lam_re', 'new_v_s5_lam_im', 'new_v_s5_log_step', 'new_v_s5_b_re', 'new_v_s5_b_im', 'new_v_s5_c_re', 'new_v_s5_c_im', 'new_v_s5_d', 'new_v_s5_w_glu', 'new_v_s5_b_glu', 'new_v_s5_norm', 'new_v_ssd_conv_w', 'new_v_ssd_conv_b', 'new_v_ssd_dt_bias', 'new_v_ssd_a_log', 'new_v_ssd_d', 'new_v_ssd_norm', 'new_v_w_out', 'new_v_norm_ffn', 'new_v_w_gate', 'new_v_w_up', 'new_v_w_down', 'new_v_norm_final']
TWIN_LEAF_KINDS = {'loss': 'loss', 'grad_x': 'grad_x', 'grad_norm_mix': 'grad_w', 'grad_w_in': 'grad_w', 'grad_s5_lam_re': 'grad_w', 'grad_s5_lam_im': 'grad_w', 'grad_s5_log_step': 'grad_w', 'grad_s5_b_re': 'grad_w', 'grad_s5_b_im': 'grad_w', 'grad_s5_c_re': 'grad_w', 'grad_s5_c_im': 'grad_w', 'grad_s5_d': 'grad_w', 'grad_s5_w_glu': 'grad_w', 'grad_s5_b_glu': 'grad_w', 'grad_s5_norm': 'grad_w', 'grad_ssd_conv_w': 'grad_w', 'grad_ssd_conv_b': 'grad_w', 'grad_ssd_dt_bias': 'grad_w', 'grad_ssd_a_log': 'grad_w', 'grad_ssd_d': 'grad_w', 'grad_ssd_norm': 'grad_w', 'grad_w_out': 'grad_w', 'grad_norm_ffn': 'grad_w', 'grad_w_gate': 'grad_w', 'grad_w_up': 'grad_w', 'grad_w_down': 'grad_w', 'grad_norm_final': 'grad_w', 'delta_norm_mix': 'delta_w', 'delta_w_in': 'delta_w', 'delta_s5_lam_re': 'delta_w', 'delta_s5_lam_im': 'delta_w', 'delta_s5_log_step': 'delta_w', 'delta_s5_b_re': 'delta_w', 'delta_s5_b_im': 'delta_w', 'delta_s5_c_re': 'delta_w', 'delta_s5_c_im': 'delta_w', 'delta_s5_d': 'delta_w', 'delta_s5_w_glu': 'delta_w', 'delta_s5_b_glu': 'delta_w', 'delta_s5_norm': 'delta_w', 'delta_ssd_conv_w': 'delta_w', 'delta_ssd_conv_b': 'delta_w', 'delta_ssd_dt_bias': 'delta_w', 'delta_ssd_a_log': 'delta_w', 'delta_ssd_d': 'delta_w', 'delta_ssd_norm': 'delta_w', 'delta_w_out': 'delta_w', 'delta_norm_ffn': 'delta_w', 'delta_w_gate': 'delta_w', 'delta_w_up': 'delta_w', 'delta_w_down': 'delta_w', 'delta_norm_final': 'delta_w', 'new_m_norm_mix': 'new_m', 'new_m_w_in': 'new_m', 'new_m_s5_lam_re': 'new_m', 'new_m_s5_lam_im': 'new_m', 'new_m_s5_log_step': 'new_m', 'new_m_s5_b_re': 'new_m', 'new_m_s5_b_im': 'new_m', 'new_m_s5_c_re': 'new_m', 'new_m_s5_c_im': 'new_m', 'new_m_s5_d': 'new_m', 'new_m_s5_w_glu': 'new_m', 'new_m_s5_b_glu': 'new_m', 'new_m_s5_norm': 'new_m', 'new_m_ssd_conv_w': 'new_m', 'new_m_ssd_conv_b': 'new_m', 'new_m_ssd_dt_bias': 'new_m', 'new_m_ssd_a_log': 'new_m', 'new_m_ssd_d': 'new_m', 'new_m_ssd_norm': 'new_m', 'new_m_w_out': 'new_m', 'new_m_norm_ffn': 'new_m', 'new_m_w_gate': 'new_m', 'new_m_w_up': 'new_m', 'new_m_w_down': 'new_m', 'new_m_norm_final': 'new_m', 'new_v_norm_mix': 'new_v', 'new_v_w_in': 'new_v', 'new_v_s5_lam_re': 'new_v', 'new_v_s5_lam_im': 'new_v', 'new_v_s5_log_step': 'new_v', 'new_v_s5_b_re': 'new_v', 'new_v_s5_b_im': 'new_v', 'new_v_s5_c_re': 'new_v', 'new_v_s5_c_im': 'new_v', 'new_v_s5_d': 'new_v', 'new_v_s5_w_glu': 'new_v', 'new_v_s5_b_glu': 'new_v', 'new_v_s5_norm': 'new_v', 'new_v_ssd_conv_w': 'new_v', 'new_v_ssd_conv_b': 'new_v', 'new_v_ssd_dt_bias': 'new_v', 'new_v_ssd_a_log': 'new_v', 'new_v_ssd_d': 'new_v', 'new_v_ssd_norm': 'new_v', 'new_v_w_out': 'new_v', 'new_v_norm_ffn': 'new_v', 'new_v_w_gate': 'new_v', 'new_v_w_up': 'new_v', 'new_v_w_down': 'new_v', 'new_v_norm_final': 'new_v'}


def _forward(args):
    return _fwd_reference(*[args[k] for k in FWD_PARAMS])


def _output_shape():
    out = _jax.eval_shape(lambda: _forward(_fwd_setup_inputs(0)))
    return out.shape, out.dtype

N_MICROBATCH = 1
ADAM_LR = 0.001
ADAM_B1 = 0.9
ADAM_B2 = 0.999
ADAM_EPS = 1e-08
ADAM_WD = 0.01
ADAM_STEP = 10
PER_EXAMPLE_BATCH_AXIS = {'x': 0, 'loss_target': 0}
SHARED_INPUTS = []
_WEIGHT_DTYPES = {'norm_mix': _jnp.float32, 'w_in': _jnp.float32, 's5_lam_re': _jnp.float32, 's5_lam_im': _jnp.float32, 's5_log_step': _jnp.float32, 's5_b_re': _jnp.float32, 's5_b_im': _jnp.float32, 's5_c_re': _jnp.float32, 's5_c_im': _jnp.float32, 's5_d': _jnp.float32, 's5_w_glu': _jnp.float32, 's5_b_glu': _jnp.float32, 's5_norm': _jnp.float32, 'ssd_conv_w': _jnp.float32, 'ssd_conv_b': _jnp.float32, 'ssd_dt_bias': _jnp.float32, 'ssd_a_log': _jnp.float32, 'ssd_d': _jnp.float32, 'ssd_norm': _jnp.float32, 'w_out': _jnp.float32, 'norm_ffn': _jnp.float32, 'w_gate': _jnp.float32, 'w_up': _jnp.float32, 'w_down': _jnp.float32, 'norm_final': _jnp.float32}
MOMENT_SCALE = {'norm_mix': 1.251457e-01, 'w_in': 6.461683e-02, 's5_lam_re': 3.707317e-03, 's5_lam_im': 3.456360e-03, 's5_log_step': 2.317552e+00, 's5_b_re': 2.374409e-03, 's5_b_im': 2.415804e-03, 's5_c_re': 4.813954e-03, 's5_c_im': 4.800538e-03, 's5_d': 7.210829e-02, 's5_w_glu': 1.932333e-02, 's5_b_glu': 2.950852e-02, 's5_norm': 6.769169e-02, 'ssd_conv_w': 5.685593e-02, 'ssd_conv_b': 7.083296e-02, 'ssd_dt_bias': 2.199808e-01, 'ssd_a_log': 2.020545e-01, 'ssd_d': 3.635838e-01, 'ssd_norm': 6.415509e-02, 'w_out': 9.351001e-02, 'norm_ffn': 6.940777e-02, 'w_gate': 2.999743e-02, 'w_up': 2.911115e-02, 'w_down': 4.823796e-02, 'norm_final': 1.606827e+01}


def _to_microbatches(a, axis):
    t = _jnp.moveaxis(a, axis, 0)
    t = t.reshape((N_MICROBATCH, t.shape[0] // N_MICROBATCH) + t.shape[1:])
    return _jnp.moveaxis(t, 1, axis + 1)


def setup_inputs(seed: int = 0) -> dict:
    inp = _fwd_setup_inputs(seed)
    key = _jax.random.fold_in(_jax.random.key(seed), 7919)
    shape, _ = _output_shape()
    out = dict(inp)
    out["loss_target"] = _jax.random.normal(_jax.random.fold_in(key, 0), shape, _jnp.float32)
    for i, name in enumerate(TWIN_WEIGHTS):
        w = inp[name].astype(_jnp.float32)
        if MOMENT_SCALE is None:
            s = _jnp.sqrt(_jnp.mean(_jnp.square(w)) + 1e-30)
        else:
            s = MOMENT_SCALE[name]
        km, kv = _jax.random.split(_jax.random.fold_in(key, i + 1))
        out[name] = w
        out["m_" + name] = s * _jax.random.normal(km, w.shape, _jnp.float32)
        out["v_" + name] = (s * s) * _jax.random.uniform(kv, w.shape, _jnp.float32, 0.5, 1.5)
    if N_MICROBATCH > 1:
        for name, axis in PER_EXAMPLE_BATCH_AXIS.items():
            out[name] = _to_microbatches(out[name], axis)
    return {'x': out['x'], 'norm_mix': out['norm_mix'], 'w_in': out['w_in'], 's5_lam_re': out['s5_lam_re'], 's5_lam_im': out['s5_lam_im'], 's5_log_step': out['s5_log_step'], 's5_b_re': out['s5_b_re'], 's5_b_im': out['s5_b_im'], 's5_c_re': out['s5_c_re'], 's5_c_im': out['s5_c_im'], 's5_d': out['s5_d'], 's5_w_glu': out['s5_w_glu'], 's5_b_glu': out['s5_b_glu'], 's5_norm': out['s5_norm'], 'ssd_conv_w': out['ssd_conv_w'], 'ssd_conv_b': out['ssd_conv_b'], 'ssd_dt_bias': out['ssd_dt_bias'], 'ssd_a_log': out['ssd_a_log'], 'ssd_d': out['ssd_d'], 'ssd_norm': out['ssd_norm'], 'w_out': out['w_out'], 'norm_ffn': out['norm_ffn'], 'w_gate': out['w_gate'], 'w_up': out['w_up'], 'w_down': out['w_down'], 'norm_final': out['norm_final'], 'loss_target': out['loss_target'], 'm_norm_mix': out['m_norm_mix'], 'm_w_in': out['m_w_in'], 'm_s5_lam_re': out['m_s5_lam_re'], 'm_s5_lam_im': out['m_s5_lam_im'], 'm_s5_log_step': out['m_s5_log_step'], 'm_s5_b_re': out['m_s5_b_re'], 'm_s5_b_im': out['m_s5_b_im'], 'm_s5_c_re': out['m_s5_c_re'], 'm_s5_c_im': out['m_s5_c_im'], 'm_s5_d': out['m_s5_d'], 'm_s5_w_glu': out['m_s5_w_glu'], 'm_s5_b_glu': out['m_s5_b_glu'], 'm_s5_norm': out['m_s5_norm'], 'm_ssd_conv_w': out['m_ssd_conv_w'], 'm_ssd_conv_b': out['m_ssd_conv_b'], 'm_ssd_dt_bias': out['m_ssd_dt_bias'], 'm_ssd_a_log': out['m_ssd_a_log'], 'm_ssd_d': out['m_ssd_d'], 'm_ssd_norm': out['m_ssd_norm'], 'm_w_out': out['m_w_out'], 'm_norm_ffn': out['m_norm_ffn'], 'm_w_gate': out['m_w_gate'], 'm_w_up': out['m_w_up'], 'm_w_down': out['m_w_down'], 'm_norm_final': out['m_norm_final'], 'v_norm_mix': out['v_norm_mix'], 'v_w_in': out['v_w_in'], 'v_s5_lam_re': out['v_s5_lam_re'], 'v_s5_lam_im': out['v_s5_lam_im'], 'v_s5_log_step': out['v_s5_log_step'], 'v_s5_b_re': out['v_s5_b_re'], 'v_s5_b_im': out['v_s5_b_im'], 'v_s5_c_re': out['v_s5_c_re'], 'v_s5_c_im': out['v_s5_c_im'], 'v_s5_d': out['v_s5_d'], 'v_s5_w_glu': out['v_s5_w_glu'], 'v_s5_b_glu': out['v_s5_b_glu'], 'v_s5_norm': out['v_s5_norm'], 'v_ssd_conv_w': out['v_ssd_conv_w'], 'v_ssd_conv_b': out['v_ssd_conv_b'], 'v_ssd_dt_bias': out['v_ssd_dt_bias'], 'v_ssd_a_log': out['v_ssd_a_log'], 'v_ssd_d': out['v_ssd_d'], 'v_ssd_norm': out['v_ssd_norm'], 'v_w_out': out['v_w_out'], 'v_norm_ffn': out['v_norm_ffn'], 'v_w_gate': out['v_w_gate'], 'v_w_up': out['v_w_up'], 'v_w_down': out['v_w_down'], 'v_norm_final': out['v_norm_final']}


def _loss(weights, diff, rest, loss_target):
    with _jax.named_scope("forward"):
        args = {**rest, TWIN_DIFF_INPUT: diff, **{k: w.astype(_WEIGHT_DTYPES[k]) for k, w in weights.items()}}
        y = _forward(args)
    with _jax.named_scope("loss_head"):
        err = _jnp.square(y.astype(_jnp.float32) - loss_target)
        return 0.5 * _jnp.sum(_jnp.mean(err, axis=-1)) if err.ndim else 0.5 * err


def _adamw(w, g, m, v):
    m = ADAM_B1 * m + (1.0 - ADAM_B1) * g
    v = ADAM_B2 * v + (1.0 - ADAM_B2) * _jnp.square(g)
    m_hat = m / (1.0 - ADAM_B1 ** ADAM_STEP)
    v_hat = v / (1.0 - ADAM_B2 ** ADAM_STEP)
    delta = -ADAM_LR * (m_hat / (_jnp.sqrt(v_hat) + ADAM_EPS) + ADAM_WD * w)
    return delta, m, v


def reference(x, norm_mix, w_in, s5_lam_re, s5_lam_im, s5_log_step, s5_b_re, s5_b_im, s5_c_re, s5_c_im, s5_d, s5_w_glu, s5_b_glu, s5_norm, ssd_conv_w, ssd_conv_b, ssd_dt_bias, ssd_a_log, ssd_d, ssd_norm, w_out, norm_ffn, w_gate, w_up, w_down, norm_final, loss_target, m_norm_mix, m_w_in, m_s5_lam_re, m_s5_lam_im, m_s5_log_step, m_s5_b_re, m_s5_b_im, m_s5_c_re, m_s5_c_im, m_s5_d, m_s5_w_glu, m_s5_b_glu, m_s5_norm, m_ssd_conv_w, m_ssd_conv_b, m_ssd_dt_bias, m_ssd_a_log, m_ssd_d, m_ssd_norm, m_w_out, m_norm_ffn, m_w_gate, m_w_up, m_w_down, m_norm_final, v_norm_mix, v_w_in, v_s5_lam_re, v_s5_lam_im, v_s5_log_step, v_s5_b_re, v_s5_b_im, v_s5_c_re, v_s5_c_im, v_s5_d, v_s5_w_glu, v_s5_b_glu, v_s5_norm, v_ssd_conv_w, v_ssd_conv_b, v_ssd_dt_bias, v_ssd_a_log, v_ssd_d, v_ssd_norm, v_w_out, v_norm_ffn, v_w_gate, v_w_up, v_w_down, v_norm_final):
    given = dict(x=x, norm_mix=norm_mix, w_in=w_in, s5_lam_re=s5_lam_re, s5_lam_im=s5_lam_im, s5_log_step=s5_log_step, s5_b_re=s5_b_re, s5_b_im=s5_b_im, s5_c_re=s5_c_re, s5_c_im=s5_c_im, s5_d=s5_d, s5_w_glu=s5_w_glu, s5_b_glu=s5_b_glu, s5_norm=s5_norm, ssd_conv_w=ssd_conv_w, ssd_conv_b=ssd_conv_b, ssd_dt_bias=ssd_dt_bias, ssd_a_log=ssd_a_log, ssd_d=ssd_d, ssd_norm=ssd_norm, w_out=w_out, norm_ffn=norm_ffn, w_gate=w_gate, w_up=w_up, w_down=w_down, norm_final=norm_final, loss_target=loss_target, m_norm_mix=m_norm_mix, m_w_in=m_w_in, m_s5_lam_re=m_s5_lam_re, m_s5_lam_im=m_s5_lam_im, m_s5_log_step=m_s5_log_step, m_s5_b_re=m_s5_b_re, m_s5_b_im=m_s5_b_im, m_s5_c_re=m_s5_c_re, m_s5_c_im=m_s5_c_im, m_s5_d=m_s5_d, m_s5_w_glu=m_s5_w_glu, m_s5_b_glu=m_s5_b_glu, m_s5_norm=m_s5_norm, m_ssd_conv_w=m_ssd_conv_w, m_ssd_conv_b=m_ssd_conv_b, m_ssd_dt_bias=m_ssd_dt_bias, m_ssd_a_log=m_ssd_a_log, m_ssd_d=m_ssd_d, m_ssd_norm=m_ssd_norm, m_w_out=m_w_out, m_norm_ffn=m_norm_ffn, m_w_gate=m_w_gate, m_w_up=m_w_up, m_w_down=m_w_down, m_norm_final=m_norm_final, v_norm_mix=v_norm_mix, v_w_in=v_w_in, v_s5_lam_re=v_s5_lam_re, v_s5_lam_im=v_s5_lam_im, v_s5_log_step=v_s5_log_step, v_s5_b_re=v_s5_b_re, v_s5_b_im=v_s5_b_im, v_s5_c_re=v_s5_c_re, v_s5_c_im=v_s5_c_im, v_s5_d=v_s5_d, v_s5_w_glu=v_s5_w_glu, v_s5_b_glu=v_s5_b_glu, v_s5_norm=v_s5_norm, v_ssd_conv_w=v_ssd_conv_w, v_ssd_conv_b=v_ssd_conv_b, v_ssd_dt_bias=v_ssd_dt_bias, v_ssd_a_log=v_ssd_a_log, v_ssd_d=v_ssd_d, v_ssd_norm=v_ssd_norm, v_w_out=v_w_out, v_norm_ffn=v_norm_ffn, v_w_gate=v_w_gate, v_w_up=v_w_up, v_w_down=v_w_down, v_norm_final=v_norm_final)
    weights = {n: given[n] for n in TWIN_WEIGHTS}
    shared = {n: given[n] for n in SHARED_INPUTS}
    per_example = {n: given[n] for n in ['x']}
    grad_fn = _jax.value_and_grad(_loss, argnums=(0, 1))

    def one_microbatch(ex, loss_target):
        ex = dict(ex)
        diff = ex.pop(TWIN_DIFF_INPUT)
        return grad_fn(weights, diff, {**shared, **ex}, loss_target)

    if N_MICROBATCH == 1:
        loss, (grad_w, grad_x) = one_microbatch(per_example, given["loss_target"])
    else:
        def body(carry, xs):
            loss_sum, grad_sum = carry
            l_k, (gw_k, gx_k) = one_microbatch(xs[0], xs[1])
            with _jax.named_scope("update"):
                return (loss_sum + l_k, _jax.tree.map(_jnp.add, grad_sum, gw_k)), gx_k

        init = (_jnp.zeros((), _jnp.float32), _jax.tree.map(_jnp.zeros_like, weights))
        (loss, grad_w), grad_x = _jax.lax.scan(body, init, (per_example, given["loss_target"]))
    with _jax.named_scope("update"):
        delta_w, new_m, new_v = {}, {}, {}
        for n in TWIN_WEIGHTS:
            delta_w[n], new_m[n], new_v[n] = _adamw(weights[n], grad_w[n], given["m_" + n], given["v_" + n])
    return (loss, grad_x, *[grad_w[n] for n in TWIN_WEIGHTS], *[delta_w[n] for n in TWIN_WEIGHTS],
            *[new_m[n] for n in TWIN_WEIGHTS], *[new_v[n] for n in TWIN_WEIGHTS])
```

```python
import functools

import jax
import jax.numpy as jnp
from jax import lax
from jax.experimental import pallas as pl
from jax.experimental.pallas import tpu as pltpu

F32 = jnp.float32
BF16 = jnp.bfloat16
MESH_ID = pl.DeviceIdType.MESH

N_DEV = 8
DEPTH = 2
D_MODEL = 1024
S5_GROUPS = 64
S5_GROUP = 16
S5_STATE = 64
S5_LANES = S5_GROUPS * S5_STATE
SSD_HEADS = 16
SSD_HEAD_DIM = 64
SSD_STATE = 128
SSD_CHUNK = 128
SSD_CONV = 4
SSD_CONV_DIM = 1536
FFN_HIDDEN = 2816
IN_PROJ = 3600
EPS = 1e-6
LANE = 128
SUBLANE = 8
VMEM_LIMIT = 56 * 1024 * 1024

ADAM_LR = 0.001
ADAM_B1 = 0.9
ADAM_B2 = 0.999
ADAM_EPS = 1e-08
ADAM_WD = 0.01
ADAM_STEP = 10

TOK_TILE = 256
S5_TILE = 128
S5_SEG = S5_TILE // SUBLANE


def _sigmoid(x):
    return jax.nn.sigmoid(x)


def _silu(x):
    return x * _sigmoid(x)


def _gelu(x):
    return 0.5 * x * (1.0 + jnp.tanh(0.7978845608028654 * (x + 0.044715 * (x * x * x))))


def _softplus(x):
    return jnp.maximum(x, 0.0) + jnp.log(1.0 + jnp.exp(-jnp.abs(x)))


def _rms(x, g):
    r = lax.rsqrt(jnp.mean(x * x, axis=-1, keepdims=True) + EPS)
    return x * r * g


def _nn(a, b):
    return lax.dot_general(a.astype(BF16), b.astype(BF16), (((1,), (0,)), ((), ())), preferred_element_type=F32)


def _nt(a, b):
    return lax.dot_general(a.astype(BF16), b.astype(BF16), (((1,), (1,)), ((), ())), preferred_element_type=F32)


def _tn(a, b):
    return lax.dot_general(a.astype(BF16), b.astype(BF16), (((0,), (0,)), ((), ())), preferred_element_type=F32)


def _nn_f32(a, b):
    return lax.dot_general(a, b, (((1,), (0,)), ((), ())), precision=lax.Precision.HIGHEST, preferred_element_type=F32)


def _tn_f32(a, b):
    return lax.dot_general(a, b, (((0,), (0,)), ((), ())), precision=lax.Precision.HIGHEST, preferred_element_type=F32)


@jax.custom_vjp
def _nn_d(a, b):
    return _nn(a, b)


_nn_d.defvjp(lambda a, b: (_nn(a, b), (a, b)), lambda r, g: (_nt(g, r[1]), _tn(r[0], g)))


@jax.custom_vjp
def _nt_d(a, b):
    return _nt(a, b)


_nt_d.defvjp(lambda a, b: (_nt(a, b), (a, b)), lambda r, g: (_nn(g, r[1]), _tn(g, r[0])))


@jax.custom_vjp
def _tn_d(a, b):
    return _tn(a, b)


_tn_d.defvjp(lambda a, b: (_tn(a, b), (a, b)), lambda r, g: (_nt(r[1], g), _nn(r[0], g)))


@jax.custom_vjp
def _cumsum_rows(tri, x):
    return _nn_f32(tri, x)


_cumsum_rows.defvjp(lambda tri, x: (_nn_f32(tri, x), tri), lambda tri, g: (jnp.zeros_like(tri), _tn_f32(tri, g)))


def _full(shape):
    zeros = (0,) * len(shape)
    return pl.BlockSpec(shape, lambda *_: zeros)


def _const(shape):
    zeros = (0,) * len(shape)
    return pl.BlockSpec(shape, lambda *_: zeros, pipeline_mode=pl.Buffered(1))


def _rows(tile, width, n_tiles=None):
    if n_tiles is None:
        return pl.BlockSpec((tile, width), lambda i: (i, 0))
    return pl.BlockSpec((tile, width), lambda i: (n_tiles - 1 - i, 0))


def _call(body, name, grid, in_specs, out_specs, out_shape, scratch=()):
    return pl.pallas_call(
        body, name=name, grid=grid, in_specs=in_specs, out_specs=out_specs, out_shape=out_shape,
        scratch_shapes=list(scratch),
        compiler_params=pltpu.CompilerParams(dimension_semantics=("arbitrary",) * len(grid),
                                             vmem_limit_bytes=VMEM_LIMIT))


def _sds(shape, dtype=F32):
    return jax.ShapeDtypeStruct(shape, dtype)


def _tile_of(n, cap=512):
    if n <= LANE:
        return n
    best = LANE
    for t in range(LANE, cap + 1, LANE):
        if n % t == 0:
            best = t
    return best


def _inproj_fwd(x, nm, wu, wz, wx, wd, name):
    n_tok = x.shape[0]
    tm = TOK_TILE

    def body(x_ref, nm_ref, wu_ref, wz_ref, wx_ref, wd_ref, u_ref, z_ref, xbc_ref, dt_ref):
        h = _rms(x_ref[...], nm_ref[...]).astype(BF16)
        u_ref[...] = _nn(h, wu_ref[...])
        z_ref[...] = _nn(h, wz_ref[...])
        xbc_ref[...] = _nn(h, wx_ref[...])
        dt_ref[...] = _nn(h, wd_ref[...])

    return _call(
        body, name, (n_tok // tm,),
        [_rows(tm, D_MODEL), _const((1, D_MODEL)), _const(wu.shape), _const(wz.shape), _const(wx.shape), _const(wd.shape)],
        [_rows(tm, D_MODEL), _rows(tm, D_MODEL), _rows(tm, SSD_CONV_DIM), _rows(tm, LANE)],
        [_sds((n_tok, D_MODEL)), _sds((n_tok, D_MODEL)), _sds((n_tok, SSD_CONV_DIM)), _sds((n_tok, LANE))],
    )(x, nm, wu, wz, wx, wd)


def _inproj_bwd(x, nm, du, dz, dxbc, ddt, dres, wu, wz, wx, wd, name):
    n_tok = x.shape[0]
    tm = TOK_TILE

    def body(x_ref, nm_ref, du_ref, dz_ref, dxbc_ref, ddt_ref, dres_ref, wu_ref, wz_ref, wx_ref, wd_ref,
             dx_ref, h_ref, dnm_ref):
        dh = (_nt(du_ref[...], wu_ref[...]) + _nt(dz_ref[...], wz_ref[...])
              + _nt(dxbc_ref[...], wx_ref[...]) + _nt(ddt_ref[...], wd_ref[...]))
        h, vjp = jax.vjp(_rms, x_ref[...], nm_ref[...])
        dx, dnm = vjp(dh)
        dx_ref[...] = dres_ref[...] + dx
        h_ref[...] = h.astype(BF16)

        @pl.when(pl.program_id(0) == 0)
        def _():
            dnm_ref[...] = jnp.zeros_like(dnm_ref)

        dnm_ref[...] += dnm

    return _call(
        body, name, (n_tok // tm,),
        [_rows(tm, D_MODEL), _const((1, D_MODEL)), _rows(tm, D_MODEL), _rows(tm, D_MODEL), _rows(tm, SSD_CONV_DIM),
         _rows(tm, LANE), _rows(tm, D_MODEL), _const(wu.shape), _const(wz.shape), _const(wx.shape), _const(wd.shape)],
        [_rows(tm, D_MODEL), _rows(tm, D_MODEL), _full((1, D_MODEL))],
        [_sds((n_tok, D_MODEL)), _sds((n_tok, D_MODEL), BF16), _sds((1, D_MODEL))],
    )(x, nm, du, dz, dxbc, ddt, dres, wu, wz, wx, wd)


def _ffn_act(gt, up):
    return _silu(gt) * up


FFN_BLOCK = FFN_HIDDEN // N_DEV
FFN_BLOCK_PAD = -(-FFN_BLOCK // LANE) * LANE


FFN_PAD = N_DEV * FFN_BLOCK_PAD


def _mix_ffn_fwd(x0, ya, yb, wo, nf, wg, wu, wd, name):
    n_tok = x0.shape[0]
    tm = TOK_TILE

    def body(x0_ref, ya_ref, yb_ref, wo_ref, nf_ref, wg_ref, wu_ref, wd_ref, x1_ref, x2_ref):
        x1 = x0_ref[...] + _nn(ya_ref[...], wo_ref[:D_MODEL, :]) + _nn(yb_ref[...], wo_ref[D_MODEL:, :])
        h = _rms(x1, nf_ref[...]).astype(BF16)
        x1_ref[...] = x1
        x2_ref[...] = x1 + _nn(_ffn_act(_nt(h, wg_ref[...]), _nt(h, wu_ref[...])), wd_ref[...])

    return _call(
        body, name, (n_tok // tm,),
        [_rows(tm, D_MODEL), _rows(tm, D_MODEL), _rows(tm, D_MODEL), _const(wo.shape),
         _const((1, D_MODEL)), _const(wg.shape), _const(wu.shape), _const(wd.shape)],
        [_rows(tm, D_MODEL), _rows(tm, D_MODEL)],
        [_sds((n_tok, D_MODEL)), _sds((n_tok, D_MODEL))],
    )(x0, ya, yb, wo, nf, wg, wu, wd)


def _mix_ffn_bwd(x1, dx2, wo, nf, wg, wu, wd, name):
    n_tok = x1.shape[0]
    tm = TOK_TILE // 2

    def body(x1_ref, dx2_ref, wo_ref, nf_ref, wg_ref, wu_ref, wd_ref,
             dx1_ref, dya_ref, dyb_ref, h_ref, a_ref, dgt_ref, dup_ref, dnf_ref):
        dx2 = dx2_ref[...]
        dx2b = dx2.astype(BF16)
        h, rms_vjp = jax.vjp(_rms, x1_ref[...], nf_ref[...])
        hb = h.astype(BF16)
        a, act_vjp = jax.vjp(_ffn_act, _nt(hb, wg_ref[...]), _nt(hb, wu_ref[...]))
        dgt, dup = act_vjp(_nt(dx2b, wd_ref[...]))
        a_ref[...] = a.astype(BF16)
        dgt_ref[...] = dgt.astype(BF16)
        dup_ref[...] = dup.astype(BF16)
        dx, dnf = rms_vjp(_nn(dgt, wg_ref[...]) + _nn(dup, wu_ref[...]))
        dx1 = dx2 + dx
        dx1_ref[...] = dx1
        dya_ref[...] = _nt(dx1, wo_ref[:D_MODEL, :])
        dyb_ref[...] = _nt(dx1, wo_ref[D_MODEL:, :])
        h_ref[...] = hb

        @pl.when(pl.program_id(0) == 0)
        def _():
            dnf_ref[...] = jnp.zeros_like(dnf_ref)

        dnf_ref[...] += dnf

    hidden = _rows(tm, FFN_PAD)
    return _call(
        body, name, (n_tok // tm,),
        [_rows(tm, D_MODEL), _rows(tm, D_MODEL), _const(wo.shape), _const((1, D_MODEL)),
         _const(wg.shape), _const(wu.shape), _const(wd.shape)],
        [_rows(tm, D_MODEL), _rows(tm, D_MODEL), _rows(tm, D_MODEL), _rows(tm, D_MODEL), hidden, hidden, hidden,
         _full((1, D_MODEL))],
        [_sds((n_tok, D_MODEL)), _sds((n_tok, D_MODEL)), _sds((n_tok, D_MODEL)), _sds((n_tok, D_MODEL), BF16),
         _sds((n_tok, FFN_PAD), BF16), _sds((n_tok, FFN_PAD), BF16), _sds((n_tok, FFN_PAD), BF16), _sds((1, D_MODEL))],
    )(x1, dx2, wo, nf, wg, wu, wd)


def _loss_head(x, nf, target, name):
    n_tok = x.shape[0]
    tm = TOK_TILE

    def loss_of(xv, g, t):
        e = _rms(xv, g) - t
        return 0.5 * jnp.sum(jnp.sum(e * e, axis=-1, keepdims=True) * (1.0 / D_MODEL), axis=0, keepdims=True)

    def body(x_ref, nf_ref, t_ref, loss_ref, dx_ref, dnf_ref):
        loss, vjp = jax.vjp(functools.partial(loss_of, t=t_ref[...]), x_ref[...], nf_ref[...])
        dx, dnf = vjp(jnp.ones_like(loss))
        dx_ref[...] = dx

        @pl.when(pl.program_id(0) == 0)
        def _():
            dnf_ref[...] = jnp.zeros_like(dnf_ref)
            loss_ref[...] = jnp.zeros_like(loss_ref)

        dnf_ref[...] += dnf
        loss_ref[...] += jnp.broadcast_to(loss, loss_ref.shape)

    return _call(
        body, name, (n_tok // tm,),
        [_rows(tm, D_MODEL), _const((1, D_MODEL)), _rows(tm, D_MODEL)],
        [_full((SUBLANE, LANE)), _rows(tm, D_MODEL), _full((1, D_MODEL))],
        [_sds((SUBLANE, LANE)), _sds((n_tok, D_MODEL)), _sds((1, D_MODEL))],
    )(x, nf, target)


def _matmul_tn(a, b, name):
    n_tok, k1 = a.shape
    k2 = b.shape[1]
    t1, t2 = _tile_of(k1), _tile_of(k2)

    def body(a_ref, b_ref, o_ref):
        o_ref[...] = _tn(a_ref[...], b_ref[...])

    return pl.pallas_call(
        body, name=name, grid=(k1 // t1, k2 // t2),
        in_specs=[pl.BlockSpec((n_tok, t1), lambda i, j: (0, i)), pl.BlockSpec((n_tok, t2), lambda i, j: (0, j))],
        out_specs=pl.BlockSpec((t1, t2), lambda i, j: (i, j)),
        out_shape=_sds((k1, k2)),
        compiler_params=pltpu.CompilerParams(dimension_semantics=("arbitrary", "arbitrary"), vmem_limit_bytes=VMEM_LIMIT),
    )(a, b)


def _tn_params():
    return pltpu.CompilerParams(dimension_semantics=("arbitrary", "arbitrary"), vmem_limit_bytes=VMEM_LIMIT)


def _matmul_tn_lhs_blocks(a, b, width, keep, name):
    n_tok, k1 = a.shape
    k2 = b.shape[1]
    t2 = _tile_of(k2)

    def body(a_ref, b_ref, o_ref):
        o_ref[...] = _tn(a_ref[...], b_ref[...])[:keep, :]

    return pl.pallas_call(
        body, name=name, grid=(k1 // width, k2 // t2),
        in_specs=[pl.BlockSpec((n_tok, width), lambda d, j: (0, d)), pl.BlockSpec((n_tok, t2), lambda d, j: (0, j))],
        out_specs=pl.BlockSpec((None, keep, t2), lambda d, j: (d, 0, j)),
        out_shape=_sds((k1 // width, keep, k2)), compiler_params=_tn_params(),
    )(a, b)


def _matmul_tn_pair(a0, a1, b, name):
    n_tok, k1 = a0.shape
    k2 = b.shape[1]
    t1, t2 = _tile_of(k1), _tile_of(k2)

    def body(a0_ref, a1_ref, b_ref, o_ref):
        @pl.when(pl.program_id(0) == 0)
        def _():
            o_ref[...] = _tn(a0_ref[...], b_ref[...])

        @pl.when(pl.program_id(0) == 1)
        def _():
            o_ref[...] = _tn(a1_ref[...], b_ref[...])

    lhs = pl.BlockSpec((n_tok, t1), lambda s, i, j: (0, i))
    return pl.pallas_call(
        body, name=name, grid=(2, k1 // t1, k2 // t2),
        in_specs=[lhs, lhs, pl.BlockSpec((n_tok, t2), lambda s, i, j: (0, j))],
        out_specs=pl.BlockSpec((None, t1, t2), lambda s, i, j: (s, i, j)),
        out_shape=_sds((2, k1, k2)),
        compiler_params=pltpu.CompilerParams(dimension_semantics=("arbitrary",) * 3, vmem_limit_bytes=VMEM_LIMIT),
    )(a0, a1, b)


W_IN_BLOCK = IN_PROJ // N_DEV
W_IN_SPLITS = (D_MODEL, 2 * D_MODEL, 2 * D_MODEL + SSD_CONV_DIM)
RELAYOUT_TILE = 256


def _w_in_split(blocks, name):
    tr = RELAYOUT_TILE

    def body(b_ref, wu_ref, wz_ref, wx_ref, wd_ref):
        full = jnp.concatenate([b_ref[d] for d in range(N_DEV)], axis=1)
        wu_ref[...] = full[:, :W_IN_SPLITS[0]]
        wz_ref[...] = full[:, W_IN_SPLITS[0]:W_IN_SPLITS[1]]
        wx_ref[...] = full[:, W_IN_SPLITS[1]:W_IN_SPLITS[2]]
        wd_ref[...] = jnp.concatenate([full[:, W_IN_SPLITS[2]:], jnp.zeros((tr, LANE - SSD_HEADS), full.dtype)], axis=1)

    return _call(
        body, name, (D_MODEL // tr,), [pl.BlockSpec((N_DEV, tr, W_IN_BLOCK), lambda i: (0, i, 0))],
        [_rows(tr, D_MODEL), _rows(tr, D_MODEL), _rows(tr, SSD_CONV_DIM), _rows(tr, LANE)],
        [_sds((D_MODEL, D_MODEL), BF16), _sds((D_MODEL, D_MODEL), BF16), _sds((D_MODEL, SSD_CONV_DIM), BF16),
         _sds((D_MODEL, LANE), BF16)],
    )(blocks)


def _w_in_grad_blocks(gu, gz, gx, gdt, name):
    tr = RELAYOUT_TILE

    def body(gu_ref, gz_ref, gx_ref, gdt_ref, o_ref):
        full = jnp.concatenate([gu_ref[...], gz_ref[...], gx_ref[...], gdt_ref[...]], axis=1)
        for d in range(N_DEV):
            o_ref[d] = full[:, d * W_IN_BLOCK:(d + 1) * W_IN_BLOCK]

    return _call(
        body, name, (D_MODEL // tr,),
        [_rows(tr, D_MODEL), _rows(tr, D_MODEL), _rows(tr, SSD_CONV_DIM), _rows(tr, LANE)],
        [pl.BlockSpec((N_DEV, tr, W_IN_BLOCK), lambda i: (0, i, 0))], [_sds((N_DEV, D_MODEL, W_IN_BLOCK))],
    )(gu, gz, gx, gdt)[0]


S5_SLICES = D_MODEL // LANE
S5_SLICE_STATES = S5_LANES // S5_SLICES
SCAN_LANES = 512


def _s5_scan(br_ref, bi_ref, a_r, a_i, car_r, car_i, ini_r, ini_i, reverse, xr_ref=None, xi_ref=None,
             acc_r=None, acc_i=None):
    n_rows = br_ref.shape[1]
    seg = n_rows // SUBLANE
    order = range(SUBLANE - 1, -1, -1) if reverse else range(SUBLANE)

    def rows(t):
        return pl.ds((seg - 1 - t) if reverse else t, SUBLANE, stride=seg)

    tiles_per = SCAN_LANES // LANE

    def load(ref, t, lb):
        return jnp.concatenate([ref[lb * tiles_per + j, rows(t), :] for j in range(tiles_per)], axis=1)

    def store(ref, t, lb, val):
        for j in range(tiles_per):
            ref[lb * tiles_per + j, rows(t), :] = val[:, j * LANE:(j + 1) * LANE]

    for lb in range(S5_LANES // SCAN_LANES):
        lanes = pl.ds(lb * SCAN_LANES, SCAN_LANES)
        ar1, ai1 = a_r[:, lb * SCAN_LANES:(lb + 1) * SCAN_LANES], a_i[:, lb * SCAN_LANES:(lb + 1) * SCAN_LANES]
        ar8 = jnp.broadcast_to(ar1, (SUBLANE, SCAN_LANES))
        ai8 = jnp.broadcast_to(ai1, (SUBLANE, SCAN_LANES))

        def local(t, c):
            sr, si = c
            return (ar8 * sr - ai8 * si + load(br_ref, t, lb), ar8 * si + ai8 * sr + load(bi_ref, t, lb))

        zero = jnp.zeros((SUBLANE, SCAN_LANES), F32)
        er, ei = lax.fori_loop(0, seg, local, (zero, zero))
        pr, pi = ar1, ai1
        for _ in range(seg.bit_length() - 1):
            pr, pi = pr * pr - pi * pi, 2.0 * pr * pi
        cr, ci = car_r[:, lanes], car_i[:, lanes]
        for s in order:
            ini_r[s:s + 1, lanes] = cr
            ini_i[s:s + 1, lanes] = ci
            cr, ci = pr * cr - pi * ci + er[s:s + 1, :], pr * ci + pi * cr + ei[s:s + 1, :]
        car_r[:, lanes] = cr
        car_i[:, lanes] = ci

        if xr_ref is None:
            def final(t, c):
                sr, si = c
                nr = ar8 * sr - ai8 * si + load(br_ref, t, lb)
                ni = ar8 * si + ai8 * sr + load(bi_ref, t, lb)
                store(br_ref, t, lb, nr)
                store(bi_ref, t, lb, ni)
                return nr, ni

            lax.fori_loop(0, seg, final, (ini_r[:, lanes], ini_i[:, lanes]))
        else:
            def final_acc(t, c):
                sr, si, gr, gi = c
                xr, xi = load(xr_ref, t, lb), load(xi_ref, t, lb)
                gr = gr + sr * xr + si * xi
                gi = gi + si * xr - sr * xi
                nr = ar8 * sr - ai8 * si + load(br_ref, t, lb)
                ni = ar8 * si + ai8 * sr + load(bi_ref, t, lb)
                store(br_ref, t, lb, nr)
                store(bi_ref, t, lb, ni)
                return nr, ni, gr, gi

            _, _, gr, gi = lax.fori_loop(0, seg, final_acc,
                                         (ini_r[:, lanes], ini_i[:, lanes], acc_r[:, lanes], acc_i[:, lanes]))
            acc_r[:, lanes] = gr
            acc_i[:, lanes] = gi


def _s5_tail(gg, q, sn):
    return _rms(gg * _sigmoid(q), sn)


S5_STATE_TILES = S5_LANES // LANE
TILES_PER_SLICE = S5_SLICE_STATES // LANE


def _put_states(ref, k, val):
    for j in range(TILES_PER_SLICE):
        ref[k * TILES_PER_SLICE + j] = val[:, j * LANE:(j + 1) * LANE]


def _get_states(ref, k):
    return jnp.concatenate([ref[k * TILES_PER_SLICE + j] for j in range(TILES_PER_SLICE)], axis=1)


def _state_rows(tile, n_tiles=None):
    if n_tiles is None:
        return pl.BlockSpec((S5_STATE_TILES, tile, LANE), lambda i: (0, i, 0))
    return pl.BlockSpec((S5_STATE_TILES, tile, LANE), lambda i: (0, n_tiles - 1 - i, 0))


def _s5_fwd(u, a_r, a_i, bdb, bcr, bci, dsk, wglu, bglu, sn, name):
    n_tok = u.shape[0]
    tc = S5_TILE
    sw = S5_SLICE_STATES

    def body(u_ref, ar_ref, ai_ref, bdb_ref, bcr_ref, bci_ref, d_ref, wg_ref, bg_ref, sn_ref,
             ya_ref, xr_ref, xi_ref, v_ref, car_r, car_i, ini_r, ini_i):
        @pl.when(pl.program_id(0) == 0)
        def _():
            car_r[...] = jnp.zeros_like(car_r)
            car_i[...] = jnp.zeros_like(car_i)

        u_t = u_ref[...]
        ub = u_t.astype(BF16)
        for k in range(S5_SLICES):
            bu = _nn(ub[:, k * LANE:(k + 1) * LANE], bdb_ref[k])
            _put_states(xr_ref, k, bu[:, :sw])
            _put_states(xi_ref, k, bu[:, sw:])
        _s5_scan(xr_ref, xi_ref, ar_ref[...], ai_ref[...], car_r, car_i, ini_r, ini_i, reverse=False)
        vs = [_nn(_get_states(xr_ref, k), bcr_ref[k]) - _nn(_get_states(xi_ref, k), bci_ref[k])
              for k in range(S5_SLICES)]
        v = jnp.concatenate(vs, axis=1) + d_ref[...] * u_t
        v_ref[...] = v
        gg = _gelu(v)
        ya_ref[...] = _s5_tail(gg, _nn(gg, wg_ref[...]) + bg_ref[...], sn_ref[...])

    return _call(
        body, name, (n_tok // tc,),
        [_rows(tc, D_MODEL), _const((1, S5_LANES)), _const((1, S5_LANES)), _const(bdb.shape), _const(bcr.shape),
         _const(bci.shape), _const((1, D_MODEL)), _const(wglu.shape), _const((1, D_MODEL)), _const((1, D_MODEL))],
        [_rows(tc, D_MODEL), _state_rows(tc), _state_rows(tc), _rows(tc, D_MODEL)],
        [_sds((n_tok, D_MODEL)), _sds((S5_STATE_TILES, n_tok, LANE)), _sds((S5_STATE_TILES, n_tok, LANE)),
         _sds((n_tok, D_MODEL))],
        scratch=[pltpu.VMEM((1, S5_LANES), F32), pltpu.VMEM((1, S5_LANES), F32),
                 pltpu.VMEM((SUBLANE, S5_LANES), F32), pltpu.VMEM((SUBLANE, S5_LANES), F32)],
    )(u, a_r, a_i, bdb, bcr, bci, dsk, wglu, bglu, sn)


def _s5_bwd(dya, v, u, xr, xi, a_r, a_i, bdb, bcr, bci, dsk, wglu, bglu, sn, name):
    n_tok = u.shape[0]
    tc = S5_TILE
    nt = n_tok // tc
    sw = S5_SLICE_STATES

    def body(dya_ref, v_ref, u_ref, xr_ref, xi_ref, ar_ref, ai_ref, bdb_ref, bcr_ref, bci_ref, d_ref, wg_ref, bg_ref, sn_ref,
             du_ref, gg_ref, dq_ref, gbdb_ref, gbcr_ref, gbci_ref, gar_ref, gai_ref, gd_ref, gbg_ref, gsn_ref,
             gr_ref, gi_ref, car_r, car_i, ini_r, ini_i):
        @pl.when(pl.program_id(0) == 0)
        def _():
            for r in (car_r, car_i, gbdb_ref, gbcr_ref, gbci_ref, gar_ref, gai_ref, gd_ref, gbg_ref, gsn_ref):
                r[...] = jnp.zeros_like(r)

        u_t = u_ref[...]
        gg, gelu_vjp = jax.vjp(_gelu, v_ref[...])
        _, tail_vjp = jax.vjp(_s5_tail, gg, _nn(gg, wg_ref[...]) + bg_ref[...], sn_ref[...])
        dgg, dq, dsn = tail_vjp(dya_ref[...])
        (dv,) = gelu_vjp(dgg + _nt(dq, wg_ref[...]))
        gg_ref[...] = gg.astype(BF16)
        dq_ref[...] = dq.astype(BF16)
        gd_ref[...] += jnp.sum(dv * u_t, axis=0, keepdims=True)
        gbg_ref[...] += jnp.sum(dq, axis=0, keepdims=True)
        gsn_ref[...] += dsn
        dvb = dv.astype(BF16)
        for k in range(S5_SLICES):
            dvk = dvb[:, k * LANE:(k + 1) * LANE]
            _put_states(gr_ref, k, _nt(dvk, bcr_ref[k]))
            _put_states(gi_ref, k, -_nt(dvk, bci_ref[k]))
            gbcr_ref[k] += _tn(_get_states(xr_ref, k), dvk)
            gbci_ref[k] -= _tn(_get_states(xi_ref, k), dvk)
        _s5_scan(gr_ref, gi_ref, ar_ref[...], -ai_ref[...], car_r, car_i, ini_r, ini_i, reverse=True,
                 xr_ref=xr_ref, xi_ref=xi_ref, acc_r=gar_ref, acc_i=gai_ref)
        ub = u_t.astype(BF16)
        dus = []
        for k in range(S5_SLICES):
            gk_r, gk_i = _get_states(gr_ref, k).astype(BF16), _get_states(gi_ref, k).astype(BF16)
            bk = bdb_ref[k]
            dus.append(_nt(gk_r, bk[:, :sw]) + _nt(gk_i, bk[:, sw:]))
            uk = ub[:, k * LANE:(k + 1) * LANE]
            gbdb_ref[k, :, :sw] += _tn(uk, gk_r)
            gbdb_ref[k, :, sw:] += _tn(uk, gk_i)
        du_ref[...] = jnp.concatenate(dus, axis=1) + d_ref[...] * dv

    rev = functools.partial(_rows, n_tiles=nt)
    return _call(
        body, name, (nt,),
        [rev(tc, D_MODEL), rev(tc, D_MODEL), rev(tc, D_MODEL), _state_rows(tc, nt), _state_rows(tc, nt),
         _const((1, S5_LANES)), _const((1, S5_LANES)), _const(bdb.shape), _const(bcr.shape), _const(bci.shape),
         _const((1, D_MODEL)), _const(wglu.shape), _const((1, D_MODEL)), _const((1, D_MODEL))],
        [rev(tc, D_MODEL), rev(tc, D_MODEL), rev(tc, D_MODEL), _full(bdb.shape), _full(bcr.shape), _full(bci.shape),
         _full((SUBLANE, S5_LANES)), _full((SUBLANE, S5_LANES)), _full((1, D_MODEL)), _full((1, D_MODEL)), _full((1, D_MODEL))],
        [_sds((n_tok, D_MODEL)), _sds((n_tok, D_MODEL), BF16), _sds((n_tok, D_MODEL), BF16), _sds(bdb.shape), _sds(bcr.shape),
         _sds(bci.shape), _sds((SUBLANE, S5_LANES)), _sds((SUBLANE, S5_LANES)), _sds((1, D_MODEL)), _sds((1, D_MODEL)),
         _sds((1, D_MODEL))],
        scratch=[pltpu.VMEM((S5_STATE_TILES, tc, LANE), F32), pltpu.VMEM((S5_STATE_TILES, tc, LANE), F32),
                 pltpu.VMEM((1, S5_LANES), F32), pltpu.VMEM((1, S5_LANES), F32),
                 pltpu.VMEM((SUBLANE, S5_LANES), F32), pltpu.VMEM((SUBLANE, S5_LANES), F32)],
    )(dya, v, u, xr, xi, a_r, a_i, bdb, bcr, bci, dsk, wglu, bglu, sn)


SSD_WIDTH = SSD_HEADS * SSD_HEAD_DIM
SSD_GROUPS = 2
HEADS_PER_GROUP = SSD_HEADS // SSD_GROUPS


def _take(x, axis, start, size):
    n = x.shape[axis]

    def sl(v):
        return lax.slice_in_dim(v, start, start + size, axis=axis)

    @jax.custom_vjp
    def f(v):
        return sl(v)

    def bwd(_, g):
        parts = []
        if start:
            parts.append(jnp.zeros(g.shape[:axis] + (start,) + g.shape[axis + 1:], g.dtype))
        parts.append(g)
        if n - start - size:
            parts.append(jnp.zeros(g.shape[:axis] + (n - start - size,) + g.shape[axis + 1:], g.dtype))
        return (jnp.concatenate(parts, axis=axis) if len(parts) > 1 else g,)

    f.defvjp(lambda v: (sl(v), None), bwd)
    return f(x)


def _lane_of(x, h):
    col = lax.broadcasted_iota(jnp.int32, x.shape, 1)
    return jnp.sum(jnp.where(col == h, x, 0.0), axis=1, keepdims=True)


def _ssd_chunk(xc, z, dt, dtb, alog, dvec, gn, st, nn, nt, tn, cumsum, take):
    t_len = xc.shape[0]
    xa = _silu(xc)
    dtp = _softplus(dt + dtb)
    d_a = dtp * (-jnp.exp(alog))
    row = lax.broadcasted_iota(jnp.int32, (t_len, t_len), 0)
    col = lax.broadcasted_iota(jnp.int32, (t_len, t_len), 1)
    causal = row >= col
    cum = cumsum(causal.astype(F32), d_a)
    eye = (row == col).astype(F32)
    ys, sts = [], []
    for g in range(SSD_GROUPS):
        bg = take(xa, 1, SSD_WIDTH + g * SSD_STATE, SSD_STATE)
        cg = take(xa, 1, SSD_WIDTH + (SSD_GROUPS + g) * SSD_STATE, SSD_STATE)
        cb = nt(cg, bg)
        for r in range(HEADS_PER_GROUP):
            h = g * HEADS_PER_GROUP + r
            cc = _lane_of(cum, h)
            cr = jnp.sum(cc * eye, axis=0, keepdims=True)
            decay = jnp.exp(jnp.where(causal, cc - cr, -1e30))
            xh = take(xa, 1, h * SSD_HEAD_DIM, SSD_HEAD_DIM)
            xdt = xh * _lane_of(dtp, h)
            sth = take(st, 0, h * SSD_HEAD_DIM, SSD_HEAD_DIM)
            c_last = jnp.sum(jnp.where(row[:, :1] == t_len - 1, cc, 0.0), axis=0, keepdims=True)
            y = nn(cb * decay, xdt) + jnp.exp(cc) * nt(cg, sth) + _lane_of(dvec, h) * xh
            ys.append(y)
            sts.append(jnp.exp(c_last) * sth + tn(xdt * jnp.exp(c_last - cc), bg))
    y = jnp.concatenate(ys, axis=1) * _silu(z)
    return _rms(y, gn), jnp.concatenate(sts, axis=0)


def _shift_back(cur, prev, j):
    if j == 0:
        return cur
    row = lax.broadcasted_iota(jnp.int32, cur.shape, 0)
    return jnp.where(row < j, pltpu.roll(prev, j, 0), pltpu.roll(cur, j, 0))


def _shift_ahead(cur, nxt, j):
    if j == 0:
        return cur
    n = cur.shape[0]
    row = lax.broadcasted_iota(jnp.int32, cur.shape, 0)
    return jnp.where(row >= n - j, pltpu.roll(nxt, n - j, 0), pltpu.roll(cur, n - j, 0))


def _conv(cur, prev, w, b):
    out = b + w[SSD_CONV - 1:SSD_CONV, :] * cur
    for k in range(SSD_CONV - 1):
        out = out + w[k:k + 1, :] * _shift_back(cur, prev, SSD_CONV - 1 - k)
    return out


def _ssd_fwd(xbc, z, dt, conv_w, conv_b, dtb, alog, dvec, gn, name):
    n_tok = xbc.shape[0]
    tc = SSD_CHUNK
    nc = n_tok // tc
    st_rows = SSD_HEADS * SSD_HEAD_DIM

    def body(cur_ref, prev_ref, z_ref, dt_ref, w_ref, b_ref, dtb_ref, alog_ref, dvec_ref, gn_ref,
             yb_ref, stin_ref, st_ref):
        i = pl.program_id(0)

        @pl.when(i == 0)
        def _():
            st_ref[...] = jnp.zeros_like(st_ref)

        prev = jnp.where(i > 0, prev_ref[...], 0.0)
        xc = _conv(cur_ref[...], prev, w_ref[...], b_ref[...])
        st = st_ref[...]
        stin_ref[0] = st
        yb, st_new = _ssd_chunk(xc, z_ref[...], dt_ref[...], dtb_ref[...], alog_ref[...], dvec_ref[...], gn_ref[...], st,
                                _nn, _nt, _tn, _nn_f32, lambda v, axis, start, size: lax.slice_in_dim(v, start, start + size, axis=axis))
        yb_ref[...] = yb
        st_ref[...] = st_new

    return _call(
        body, name, (nc,),
        [_rows(tc, SSD_CONV_DIM), pl.BlockSpec((tc, SSD_CONV_DIM), lambda i: (jnp.maximum(i - 1, 0), 0)),
         _rows(tc, D_MODEL), _rows(tc, LANE), _const((SSD_CONV, SSD_CONV_DIM)), _const((1, SSD_CONV_DIM)),
         _const((1, LANE)), _const((1, LANE)), _const((1, LANE)), _const((1, D_MODEL))],
        [_rows(tc, D_MODEL), pl.BlockSpec((1, st_rows, SSD_STATE), lambda i: (i, 0, 0))],
        [_sds((n_tok, D_MODEL)), _sds((nc, st_rows, SSD_STATE))],
        scratch=[pltpu.VMEM((st_rows, SSD_STATE), F32)],
    )(xbc, xbc, z, dt, conv_w, conv_b, dtb, alog, dvec, gn)


def _ssd_bwd(dyb, xbc, z, dt, stin, conv_w, conv_b, dtb, alog, dvec, gn, name):
    n_tok = xbc.shape[0]
    tc = SSD_CHUNK
    nc = n_tok // tc
    st_rows = SSD_HEADS * SSD_HEAD_DIM

    def body(dyb_ref, cur_ref, prev_ref, z_ref, dt_ref, stin_ref, w_ref, b_ref, dtb_ref, alog_ref, dvec_ref, gn_ref,
             dxbc_ref, dz_ref, ddt_ref, gw_ref, gb_ref, gdtb_ref, galog_ref, gdvec_ref, ggn_ref,
             dst_ref, dxc_next_ref):
        i = pl.program_id(0)

        @pl.when(i == 0)
        def _():
            for r in (dst_ref, dxc_next_ref, gw_ref, gb_ref, gdtb_ref, galog_ref, gdvec_ref, ggn_ref):
                r[...] = jnp.zeros_like(r)

        cur = cur_ref[...]
        prev = jnp.where(i < nc - 1, prev_ref[...], 0.0)
        w = w_ref[...]
        xc = _conv(cur, prev, w, b_ref[...])
        chunk = functools.partial(_ssd_chunk, nn=_nn_d, nt=_nt_d, tn=_tn_d, cumsum=_cumsum_rows, take=_take)
        _, vjp = jax.vjp(chunk, xc, z_ref[...], dt_ref[...], dtb_ref[...], alog_ref[...], dvec_ref[...], gn_ref[...],
                         stin_ref[0])
        dxc, dz, ddt, gdtb, galog, gdvec, ggn, dst = vjp((dyb_ref[...], dst_ref[...]))
        dst_ref[...] = dst
        dz_ref[...] = dz
        ddt_ref[...] = ddt
        gdtb_ref[...] += gdtb
        galog_ref[...] += galog
        gdvec_ref[...] += gdvec
        ggn_ref[...] += ggn
        dxc_next = dxc_next_ref[...]
        dxbc = w[SSD_CONV - 1:SSD_CONV, :] * dxc
        gws = []
        for k in range(SSD_CONV - 1):
            j = SSD_CONV - 1 - k
            dxbc = dxbc + w[k:k + 1, :] * _shift_ahead(dxc, dxc_next, j)
            gws.append(jnp.sum(dxc * _shift_back(cur, prev, j), axis=0, keepdims=True))
        gws.append(jnp.sum(dxc * cur, axis=0, keepdims=True))
        dxbc_ref[...] = dxbc
        gw_ref[...] += jnp.concatenate(gws, axis=0)
        gb_ref[...] += jnp.sum(dxc, axis=0, keepdims=True)
        dxc_next_ref[...] = dxc

    rev = functools.partial(_rows, n_tiles=nc)
    return _call(
        body, name, (nc,),
        [rev(tc, D_MODEL), rev(tc, SSD_CONV_DIM),
         pl.BlockSpec((tc, SSD_CONV_DIM), lambda i: (jnp.maximum(nc - 2 - i, 0), 0)),
         rev(tc, D_MODEL), rev(tc, LANE), pl.BlockSpec((1, st_rows, SSD_STATE), lambda i: (nc - 1 - i, 0, 0)),
         _const((SSD_CONV, SSD_CONV_DIM)), _const((1, SSD_CONV_DIM)), _const((1, LANE)), _const((1, LANE)),
         _const((1, LANE)), _const((1, D_MODEL))],
        [rev(tc, SSD_CONV_DIM), rev(tc, D_MODEL), rev(tc, LANE), _full((SSD_CONV, SSD_CONV_DIM)), _full((1, SSD_CONV_DIM)),
         _full((1, LANE)), _full((1, LANE)), _full((1, LANE)), _full((1, D_MODEL))],
        [_sds((n_tok, SSD_CONV_DIM)), _sds((n_tok, D_MODEL)), _sds((n_tok, LANE)), _sds((SSD_CONV, SSD_CONV_DIM)),
         _sds((1, SSD_CONV_DIM)), _sds((1, LANE)), _sds((1, LANE)), _sds((1, LANE)), _sds((1, D_MODEL))],
        scratch=[pltpu.VMEM((st_rows, SSD_STATE), F32), pltpu.VMEM((tc, SSD_CONV_DIM), F32)],
    )(dyb, xbc, xbc, z, dt, stin, conv_w, conv_b, dtb, alog, dvec, gn)


@jax.custom_vjp
def _expand_cols(x, e):
    return _nn_f32(x, e)


_expand_cols.defvjp(
    lambda x, e: (_nn_f32(x, e), e),
    lambda e, g: (lax.dot_general(g, e, (((1,), (1,)), ((), ())), precision=lax.Precision.HIGHEST,
                                  preferred_element_type=F32), jnp.zeros_like(e)))


def _s5_discretize(lam_re, lam_im, log_step, b_re, b_im, expand):
    step = jnp.exp(log_step)
    mag = jnp.exp(lam_re * step)
    ang = lam_im * step
    a_r = mag * jnp.cos(ang)
    a_i = mag * jnp.sin(ang)
    den = lam_re * lam_re + lam_im * lam_im
    n_r = a_r - 1.0
    coef_r = _expand_cols((n_r * lam_re + a_i * lam_im) / den, expand)
    coef_i = _expand_cols((a_i * lam_re - n_r * lam_im) / den, expand)
    return a_r, a_i, coef_r * b_re - coef_i * b_im, coef_r * b_im + coef_i * b_re


def _expand_matrix():
    p = lax.broadcasted_iota(jnp.int32, (S5_STATE, S5_STATE * S5_GROUP), 0)
    c = lax.broadcasted_iota(jnp.int32, (S5_STATE, S5_STATE * S5_GROUP), 1)
    return (c // S5_GROUP == p).astype(F32)


def _s5_discretize_fwd(lam_re, lam_im, log_step, b_re, b_im, name):
    def body(lr_ref, li_ref, ls_ref, br_ref, bi_ref, ar_ref, ai_ref, bbr_ref, bbi_ref):
        outs = _s5_discretize(lr_ref[...], li_ref[...], ls_ref[...], br_ref[...], bi_ref[...], _expand_matrix())
        for r, o in zip((ar_ref, ai_ref, bbr_ref, bbi_ref), outs):
            r[...] = o

    sq, wide = (S5_GROUPS, S5_STATE), (S5_GROUPS, S5_STATE * S5_GROUP)
    return _call(body, name, (1,), [_full(sq), _full(sq), _full((S5_GROUPS, 1)), _full(wide), _full(wide)],
                 [_full(sq), _full(sq), _full(wide), _full(wide)], [_sds(sq), _sds(sq), _sds(wide), _sds(wide)],
                 )(lam_re, lam_im, log_step, b_re, b_im)


def _s5_discretize_bwd(lam_re, lam_im, log_step, b_re, b_im, g_ar8, g_ai8, g_bbr, g_bbi, name):
    def body(lr_ref, li_ref, ls_ref, br_ref, bi_ref, gar_ref, gai_ref, gbbr_ref, gbbi_ref,
             glr_ref, gli_ref, gls_ref, gbr_ref, gbi_ref):
        _, vjp = jax.vjp(functools.partial(_s5_discretize, expand=_expand_matrix()),
                         lr_ref[...], li_ref[...], ls_ref[...], br_ref[...], bi_ref[...])
        grads = vjp((jnp.sum(gar_ref[...], axis=0), jnp.sum(gai_ref[...], axis=0), gbbr_ref[...], gbbi_ref[...]))
        for r, g in zip((glr_ref, gli_ref, gls_ref, gbr_ref, gbi_ref), grads):
            r[...] = g

    sq, wide, col = (S5_GROUPS, S5_STATE), (S5_GROUPS, S5_STATE * S5_GROUP), (S5_GROUPS, 1)
    part = (SUBLANE,) + sq
    return _call(body, name, (1,),
                 [_full(sq), _full(sq), _full(col), _full(wide), _full(wide), _full(part), _full(part), _full(wide), _full(wide)],
                 [_full(sq), _full(sq), _full(col), _full(wide), _full(wide)],
                 [_sds(sq), _sds(sq), _sds(col), _sds(wide), _sds(wide)],
                 )(lam_re, lam_im, log_step, b_re, b_im, g_ar8, g_ai8, g_bbr, g_bbi)


GROUPS_PER_SLICE = LANE // S5_GROUP


def _block_diag_b(bb):
    t = bb.reshape(S5_SLICES, GROUPS_PER_SLICE, S5_STATE, S5_GROUP)
    eye = jnp.eye(GROUPS_PER_SLICE, dtype=bb.dtype)
    return jnp.einsum("kgph,gf->kghfp", t, eye).reshape(S5_SLICES, LANE, S5_SLICE_STATES)


def _block_diag_b_inv(m):
    t = m.reshape(S5_SLICES, GROUPS_PER_SLICE, S5_GROUP, GROUPS_PER_SLICE, S5_STATE)
    return jnp.einsum("kghgp->kgph", t).reshape(S5_GROUPS, S5_STATE * S5_GROUP)


def _block_diag_c(c):
    t = c.reshape(S5_SLICES, GROUPS_PER_SLICE, S5_GROUP, S5_STATE)
    eye = jnp.eye(GROUPS_PER_SLICE, dtype=c.dtype)
    return jnp.einsum("kghp,gf->kgpfh", t, eye).reshape(S5_SLICES, S5_SLICE_STATES, LANE)


def _block_diag_c_inv(m):
    t = m.reshape(S5_SLICES, GROUPS_PER_SLICE, S5_STATE, GROUPS_PER_SLICE, S5_GROUP)
    return jnp.einsum("kgpgh->kghp", t).reshape(S5_GROUPS, S5_GROUP, S5_STATE)


def _pad_lanes(v):
    return jnp.pad(v.reshape(1, -1), ((0, 0), (0, LANE - v.shape[0])))


def _prepare_layer(w, blk, i):
    p = {}
    p["wu"], p["wz"], p["wx"], p["wd"] = _w_in_split(blk["w_in"], name=f"w_in_split_{i}")
    p["nm"] = w["norm_mix"][i].reshape(1, D_MODEL)
    p["lam_re"], p["lam_im"] = w["s5_lam_re"][i], w["s5_lam_im"][i]
    p["log_step"] = w["s5_log_step"][i].reshape(S5_GROUPS, 1)
    p["b_re"] = w["s5_b_re"][i].reshape(S5_GROUPS, S5_STATE * S5_GROUP)
    p["b_im"] = w["s5_b_im"][i].reshape(S5_GROUPS, S5_STATE * S5_GROUP)
    a_r, a_i, bb_r, bb_i = _s5_discretize_fwd(p["lam_re"], p["lam_im"], p["log_step"], p["b_re"], p["b_im"],
                                              name=f"s5_discretize_{i}")
    p["a_r"], p["a_i"] = a_r.reshape(1, S5_LANES), a_i.reshape(1, S5_LANES)
    p["bdb"] = jnp.concatenate([_block_diag_b(bb_r), _block_diag_b(bb_i)], axis=2).astype(BF16)
    p["bcr"] = _block_diag_c(w["s5_c_re"][i]).astype(BF16)
    p["bci"] = _block_diag_c(w["s5_c_im"][i]).astype(BF16)
    p["dsk"] = w["s5_d"][i].reshape(1, D_MODEL)
    p["wglu"] = blk["s5_w_glu"].reshape(D_MODEL, D_MODEL)
    p["bglu"] = w["s5_b_glu"][i].reshape(1, D_MODEL)
    p["sn"] = w["s5_norm"][i].reshape(1, D_MODEL)
    p["conv_w"] = blk["ssd_conv_w"]
    p["conv_b"] = w["ssd_conv_b"][i].reshape(1, SSD_CONV_DIM)
    p["dtb"] = _pad_lanes(w["ssd_dt_bias"][i])
    p["alog"] = _pad_lanes(w["ssd_a_log"][i])
    p["dvec"] = _pad_lanes(w["ssd_d"][i])
    p["gn"] = w["ssd_norm"][i].reshape(1, D_MODEL)
    p["wo"] = blk["w_out"].reshape(2 * D_MODEL, D_MODEL)
    p["nf"] = w["norm_ffn"][i].reshape(1, D_MODEL)
    p["wg"], p["wup"], p["wdn"] = (blk[n].reshape(FFN_PAD, D_MODEL) for n in ("w_gate", "w_up", "w_down"))
    return p


def _layer_fwd(x0, p, i):
    u, z, xbc, dt = _inproj_fwd(x0, p["nm"], p["wu"], p["wz"], p["wx"], p["wd"], name=f"inproj_fwd_{i}")
    ya, xr, xi, v = _s5_fwd(u, p["a_r"], p["a_i"], p["bdb"], p["bcr"], p["bci"], p["dsk"], p["wglu"], p["bglu"], p["sn"],
                            name=f"s5_fwd_{i}")
    yb, stin = _ssd_fwd(xbc, z, dt, p["conv_w"], p["conv_b"], p["dtb"], p["alog"], p["dvec"], p["gn"], name=f"ssd_fwd_{i}")
    x1, x2 = _mix_ffn_fwd(x0, ya, yb, p["wo"], p["nf"], p["wg"], p["wup"], p["wdn"], name=f"mix_ffn_fwd_{i}")
    return x2, dict(x0=x0, u=u, z=z, xbc=xbc, dt=dt, xr=xr, xi=xi, v=v, stin=stin, ya=ya, yb=yb, x1=x1)


def _layer_bwd(dx2, s, p, i):
    g = {}
    dx1, dya, dyb, h2, act, dgt, dup, g_nf = _mix_ffn_bwd(s["x1"], dx2, p["wo"], p["nf"], p["wg"], p["wup"], p["wdn"],
                                                          name=f"mix_ffn_bwd_{i}")
    g["norm_ffn"] = g_nf.reshape(D_MODEL)
    g["w_down"] = _matmul_tn_lhs_blocks(act, dx2, FFN_BLOCK_PAD, FFN_BLOCK, name=f"grad_w_down_{i}")
    g["w_gate"] = _matmul_tn_lhs_blocks(dgt, h2, FFN_BLOCK_PAD, FFN_BLOCK, name=f"grad_w_gate_{i}")
    g["w_up"] = _matmul_tn_lhs_blocks(dup, h2, FFN_BLOCK_PAD, FFN_BLOCK, name=f"grad_w_up_{i}")
    g["w_out"] = _matmul_tn_pair(s["ya"], s["yb"], dx1, name=f"grad_w_out_{i}").reshape(N_DEV, 2 * D_MODEL // N_DEV, D_MODEL)

    (du, gg, dq, g_bdb, g_bcr, g_bci, g_ar8, g_ai8, g_d, g_bglu, g_sn) = _s5_bwd(
        dya, s["v"], s["u"], s["xr"], s["xi"], p["a_r"], p["a_i"], p["bdb"], p["bcr"], p["bci"], p["dsk"], p["wglu"],
        p["bglu"], p["sn"], name=f"s5_bwd_{i}")
    g["s5_w_glu"] = _matmul_tn(gg, dq, name=f"grad_w_glu_{i}").reshape(N_DEV, D_MODEL // N_DEV, D_MODEL)
    g["s5_d"], g["s5_b_glu"], g["s5_norm"] = g_d.reshape(D_MODEL), g_bglu.reshape(D_MODEL), g_sn.reshape(D_MODEL)
    g["s5_c_re"], g["s5_c_im"] = _block_diag_c_inv(g_bcr), _block_diag_c_inv(g_bci)
    sq = (SUBLANE, S5_GROUPS, S5_STATE)
    g_lr, g_li, g_ls, g_br, g_bi = _s5_discretize_bwd(
        p["lam_re"], p["lam_im"], p["log_step"], p["b_re"], p["b_im"], g_ar8.reshape(sq), g_ai8.reshape(sq),
        _block_diag_b_inv(g_bdb[:, :, :S5_SLICE_STATES]), _block_diag_b_inv(g_bdb[:, :, S5_SLICE_STATES:]),
        name=f"s5_discretize_bwd_{i}")
    g["s5_lam_re"], g["s5_lam_im"], g["s5_log_step"] = g_lr, g_li, g_ls.reshape(S5_GROUPS)
    b_shape = (S5_GROUPS, S5_STATE, S5_GROUP)
    g["s5_b_re"], g["s5_b_im"] = g_br.reshape(b_shape), g_bi.reshape(b_shape)

    dxbc, dz, ddt, g_cw, g_cb, g_dtb, g_alog, g_dvec, g_gn = _ssd_bwd(
        dyb, s["xbc"], s["z"], s["dt"], s["stin"], p["conv_w"], p["conv_b"], p["dtb"], p["alog"], p["dvec"], p["gn"],
        name=f"ssd_bwd_{i}")
    g["ssd_conv_w"] = jnp.moveaxis(g_cw.reshape(SSD_CONV, N_DEV, SSD_CONV_DIM // N_DEV), 1, 0)
    g["ssd_conv_b"] = g_cb.reshape(SSD_CONV_DIM)
    g["ssd_dt_bias"], g["ssd_a_log"], g["ssd_d"] = g_dtb[0, :SSD_HEADS], g_alog[0, :SSD_HEADS], g_dvec[0, :SSD_HEADS]
    g["ssd_norm"] = g_gn.reshape(D_MODEL)

    dx0, h, g_nm = _inproj_bwd(s["x0"], p["nm"], du, dz, dxbc, ddt, dx1, p["wu"], p["wz"], p["wx"], p["wd"],
                               name=f"inproj_bwd_{i}")
    g["norm_mix"] = g_nm.reshape(D_MODEL)
    g["w_in"] = _w_in_grad_blocks(
        _matmul_tn(h, du, name=f"grad_w_in_u_{i}"), _matmul_tn(h, dz, name=f"grad_w_in_z_{i}"),
        _matmul_tn(h, dxbc, name=f"grad_w_in_xbc_{i}"), _matmul_tn(h, ddt, name=f"grad_w_in_dt_{i}"),
        name=f"grad_w_in_blocks_{i}")
    return dx0, g


def _example_step(x, target, w, blks):
    prepared = [_prepare_layer(w, blks[i], i) for i in range(DEPTH)]
    saved = []
    h = x
    for i in range(DEPTH):
        h, s = _layer_fwd(h, prepared[i], i)
        saved.append(s)
    loss, dh, g_final = _loss_head(h, w["norm_final"].reshape(1, D_MODEL), target, name="loss_head")
    layer_grads = [None] * DEPTH
    for i in reversed(range(DEPTH)):
        dh, layer_grads[i] = _layer_bwd(dh, saved[i], prepared[i], i)
    return loss, dh, layer_grads, g_final.reshape(D_MODEL)


def _mesh_position():
    return lax.axis_index("x"), lax.axis_index("y"), lax.axis_index("c")


def _peer(pos, k):
    x, y, c = pos
    px = 1 - x if k & 4 else x
    py = 1 - y if k & 2 else y
    pc = 1 - c if k & 1 else c
    return (px, py, pc), 4 * px + 2 * py + pc


HBM = pl.BlockSpec(memory_space=pl.ANY)


def _run_copies(local, remote):
    for cp in local + remote:
        cp.start()
    for cp in remote:
        cp.wait_recv()
    for cp in remote:
        cp.wait_send()
    for cp in local:
        cp.wait()


def _comm_scratch(n_units):
    return [pltpu.SemaphoreType.DMA((n_units, N_DEV - 1)), pltpu.SemaphoreType.DMA((n_units, N_DEV - 1)),
            pltpu.SemaphoreType.DMA((n_units,))]


def _gather_blocks(arrays, layered, name):
    units, out_shapes = [], []
    for j, (a, lay) in enumerate(zip(arrays, layered)):
        for layer in (range(a.shape[0]) if lay else (None,)):
            units.append((j, layer, len(out_shapes)))
            out_shapes.append(_sds((N_DEV,) + (a.shape[1:] if lay else a.shape), a.dtype))
    n_in = len(arrays)

    def body(*refs):
        ins, outs = refs[:n_in], refs[n_in:n_in + len(out_shapes)]
        send_sems, recv_sems, local_sems = refs[n_in + len(out_shapes):]
        pos = _mesh_position()
        me = 4 * pos[0] + 2 * pos[1] + pos[2]
        local, remote = [], []
        for u, (j, layer, o) in enumerate(units):
            src = ins[j] if layer is None else ins[j].at[layer]
            local.append(pltpu.make_async_copy(src, outs[o].at[me], local_sems.at[u]))
            for k in range(1, N_DEV):
                peer, _ = _peer(pos, k)
                remote.append(pltpu.make_async_remote_copy(
                    src_ref=src, dst_ref=outs[o].at[me], send_sem=send_sems.at[u, k - 1], recv_sem=recv_sems.at[u, k - 1],
                    device_id=peer, device_id_type=MESH_ID))
        _run_copies(local, remote)

    outs = pl.pallas_call(body, name=name, in_specs=[HBM] * n_in, out_specs=[HBM] * len(out_shapes), out_shape=out_shapes,
                          scratch_shapes=_comm_scratch(len(units)))(*arrays)
    grouped = [[] for _ in arrays]
    for j, _, o in units:
        grouped[j].append(outs[o])
    return [tuple(g) for g in grouped]


def _exchange_blocks(entries, name):
    units, flat_in, out_shapes = [], [], []
    for j, entry in enumerate(entries):
        for layer, a in enumerate(entry):
            units.append((len(flat_in), layer, j))
            flat_in.append(a)
        out_shapes.append(_sds((N_DEV, len(entry)) + entry[0].shape[1:], entry[0].dtype))
    n_in = len(flat_in)

    def body(*refs):
        ins, outs = refs[:n_in], refs[n_in:n_in + len(out_shapes)]
        send_sems, recv_sems, local_sems = refs[n_in + len(out_shapes):]
        pos = _mesh_position()
        me = 4 * pos[0] + 2 * pos[1] + pos[2]
        local, remote = [], []
        for u, (i, layer, o) in enumerate(units):
            local.append(pltpu.make_async_copy(ins[i].at[me], outs[o].at[me, layer], local_sems.at[u]))
            for k in range(1, N_DEV):
                peer, peer_index = _peer(pos, k)
                remote.append(pltpu.make_async_remote_copy(
                    src_ref=ins[i].at[peer_index], dst_ref=outs[o].at[me, layer], send_sem=send_sems.at[u, k - 1],
                    recv_sem=recv_sems.at[u, k - 1], device_id=peer, device_id_type=MESH_ID))
        _run_copies(local, remote)

    return pl.pallas_call(body, name=name, in_specs=[HBM] * n_in, out_specs=[HBM] * len(out_shapes), out_shape=out_shapes,
                          scratch_shapes=_comm_scratch(len(units)))(*flat_in)


SUM_TILE = 512


def _adamw(w, g, m, v):
    m = ADAM_B1 * m + (1.0 - ADAM_B1) * g
    v = ADAM_B2 * v + (1.0 - ADAM_B2) * (g * g)
    m_hat = m / (1.0 - ADAM_B1 ** ADAM_STEP)
    v_hat = v / (1.0 - ADAM_B2 ** ADAM_STEP)
    return -ADAM_LR * (m_hat / (jnp.sqrt(v_hat) + ADAM_EPS) + ADAM_WD * w), m, v


def _sum_adamw(recv, w, m, v, name):
    n_lay, rows, cols = w.shape
    tr = _row_tile(rows, cap=256)

    def body(r_ref, w_ref, m_ref, v_ref, g_ref, d_ref, mo_ref, vo_ref):
        g = r_ref[0]
        for j in range(1, N_DEV):
            g = g + r_ref[j]
        g_ref[...] = g
        d_ref[...], mo_ref[...], vo_ref[...] = _adamw(w_ref[...], g, m_ref[...], v_ref[...])

    blk = pl.BlockSpec((None, tr, cols), lambda l, i: (l, i, 0))
    return pl.pallas_call(
        body, name=name, grid=(n_lay, rows // tr),
        in_specs=[pl.BlockSpec((N_DEV, None, tr, cols), lambda l, i: (0, l, i, 0)), blk, blk, blk],
        out_specs=[blk] * 4, out_shape=[_sds(w.shape)] * 4,
        compiler_params=pltpu.CompilerParams(dimension_semantics=("arbitrary", "arbitrary"), vmem_limit_bytes=VMEM_LIMIT),
    )(recv, w, m, v)


def _sum_senders(recv, name):
    _, n_lay, rows, cols = recv.shape
    tr = _row_tile(rows, cap=256)

    def body(r_ref, g_ref):
        g = r_ref[0]
        for j in range(1, N_DEV):
            g = g + r_ref[j]
        g_ref[...] = g

    return pl.pallas_call(
        body, name=name, grid=(n_lay, rows // tr),
        in_specs=[pl.BlockSpec((N_DEV, None, tr, cols), lambda l, i: (0, l, i, 0))],
        out_specs=pl.BlockSpec((None, tr, cols), lambda l, i: (l, i, 0)), out_shape=_sds((n_lay, rows, cols)),
        compiler_params=pltpu.CompilerParams(dimension_semantics=("arbitrary", "arbitrary"), vmem_limit_bytes=VMEM_LIMIT),
    )(recv)


def _adamw_blocks(g, w, m, v, name):
    n_lay, rows, cols = w.shape
    tr = _row_tile(rows, cap=256)

    def body(g_ref, w_ref, m_ref, v_ref, d_ref, mo_ref, vo_ref):
        d_ref[...], mo_ref[...], vo_ref[...] = _adamw(w_ref[...], g_ref[...], m_ref[...], v_ref[...])

    blk = pl.BlockSpec((None, tr, cols), lambda l, i: (l, i, 0))
    return pl.pallas_call(
        body, name=name, grid=(n_lay, rows // tr), in_specs=[blk] * 4, out_specs=[blk] * 3, out_shape=[_sds(w.shape)] * 3,
        compiler_params=pltpu.CompilerParams(dimension_semantics=("arbitrary", "arbitrary"), vmem_limit_bytes=VMEM_LIMIT),
    )(g, w, m, v)


def _sum_slots(recv, name):
    rows = recv.shape[1]

    def body(r_ref, g_ref):
        g = r_ref[0]
        for j in range(1, N_DEV):
            g = g + r_ref[j]
        g_ref[...] = g

    return _call(body, name, (1,), [_full(recv.shape)], [_full((rows, LANE))], [_sds((rows, LANE))])(recv)[0]


def _adamw_rows(g, w, m, v, name):
    rows = w.shape[0]
    tr = _row_tile(rows)

    def body(g_ref, w_ref, m_ref, v_ref, d_ref, mo_ref, vo_ref):
        d_ref[...], mo_ref[...], vo_ref[...] = _adamw(w_ref[...], g_ref[...], m_ref[...], v_ref[...])

    flat = _rows(tr, LANE)
    return _call(body, name, (rows // tr,), [flat] * 4, [flat] * 3, [_sds((rows, LANE))] * 3)(g, w, m, v)


def _row_tile(rows, cap=1024):
    if rows % SUBLANE:
        return rows
    best = SUBLANE
    for t in range(SUBLANE, cap + 1, SUBLANE):
        if rows % t == 0:
            best = t
    return best


BIG = (("w_in", (DEPTH, D_MODEL, IN_PROJ // N_DEV), 2),
       ("s5_w_glu", (DEPTH, D_MODEL // N_DEV, D_MODEL), 1),
       ("ssd_conv_w", (DEPTH, SSD_CONV, SSD_CONV_DIM // N_DEV), 2),
       ("w_out", (DEPTH, 2 * D_MODEL // N_DEV, D_MODEL), 1),
       ("w_gate", (DEPTH, D_MODEL, FFN_HIDDEN // N_DEV), 2),
       ("w_up", (DEPTH, D_MODEL, FFN_HIDDEN // N_DEV), 2),
       ("w_down", (DEPTH, FFN_HIDDEN // N_DEV, D_MODEL), 1))
SMALL = (("norm_mix", (DEPTH, D_MODEL)), ("s5_lam_re", (DEPTH, S5_GROUPS, S5_STATE)), ("s5_lam_im", (DEPTH, S5_GROUPS, S5_STATE)),
         ("s5_log_step", (DEPTH, S5_GROUPS)), ("s5_b_re", (DEPTH, S5_GROUPS, S5_STATE, S5_GROUP)),
         ("s5_b_im", (DEPTH, S5_GROUPS, S5_STATE, S5_GROUP)), ("s5_c_re", (DEPTH, S5_GROUPS, S5_GROUP, S5_STATE)),
         ("s5_c_im", (DEPTH, S5_GROUPS, S5_GROUP, S5_STATE)), ("s5_d", (DEPTH, D_MODEL)), ("s5_b_glu", (DEPTH, D_MODEL)),
         ("s5_norm", (DEPTH, D_MODEL)), ("ssd_conv_b", (DEPTH, SSD_CONV_DIM)), ("ssd_dt_bias", (DEPTH, SSD_HEADS)),
         ("ssd_a_log", (DEPTH, SSD_HEADS)), ("ssd_d", (DEPTH, SSD_HEADS)), ("ssd_norm", (DEPTH, D_MODEL)),
         ("norm_ffn", (DEPTH, D_MODEL)), ("norm_final", (D_MODEL,)))
WEIGHT_ORDER = ("norm_mix", "w_in", "s5_lam_re", "s5_lam_im", "s5_log_step", "s5_b_re", "s5_b_im", "s5_c_re", "s5_c_im", "s5_d",
                "s5_w_glu", "s5_b_glu", "s5_norm", "ssd_conv_w", "ssd_conv_b", "ssd_dt_bias", "ssd_a_log", "ssd_d", "ssd_norm",
                "w_out", "norm_ffn", "w_gate", "w_up", "w_down", "norm_final")


def _size(shape):
    n = 1
    for s in shape:
        n *= s
    return n


def _round_up(n, m):
    return -(-n // m) * m


SMALL_SIZE = sum(_size(s) for _, s in SMALL)
SMALL_ROWS = _round_up(-(-SMALL_SIZE // (N_DEV * LANE)), SUBLANE)


def _pack(parts, rows, dtype):
    flat = jnp.concatenate([p.reshape(-1).astype(dtype) for p in parts])
    return jnp.pad(flat, (0, rows * LANE - flat.shape[0])).reshape(rows, LANE)


def _unpack(flat, specs):
    out, off = {}, 0
    flat = flat.reshape(-1)
    for name, shape in specs:
        out[name] = flat[off:off + _size(shape)].reshape(shape)
        off += _size(shape)
    return out


def kernel(x, norm_mix, w_in, s5_lam_re, s5_lam_im, s5_log_step, s5_b_re, s5_b_im, s5_c_re, s5_c_im, s5_d, s5_w_glu, s5_b_glu, s5_norm, ssd_conv_w, ssd_conv_b, ssd_dt_bias, ssd_a_log, ssd_d, ssd_norm, w_out, norm_ffn, w_gate, w_up, w_down, norm_final, loss_target, m_norm_mix, m_w_in, m_s5_lam_re, m_s5_lam_im, m_s5_log_step, m_s5_b_re, m_s5_b_im, m_s5_c_re, m_s5_c_im, m_s5_d, m_s5_w_glu, m_s5_b_glu, m_s5_norm, m_ssd_conv_w, m_ssd_conv_b, m_ssd_dt_bias, m_ssd_a_log, m_ssd_d, m_ssd_norm, m_w_out, m_norm_ffn, m_w_gate, m_w_up, m_w_down, m_norm_final, v_norm_mix, v_w_in, v_s5_lam_re, v_s5_lam_im, v_s5_log_step, v_s5_b_re, v_s5_b_im, v_s5_c_re, v_s5_c_im, v_s5_d, v_s5_w_glu, v_s5_b_glu, v_s5_norm, v_ssd_conv_w, v_ssd_conv_b, v_ssd_dt_bias, v_ssd_a_log, v_ssd_d, v_ssd_norm, v_w_out, v_norm_ffn, v_w_gate, v_w_up, v_w_down, v_norm_final):
    given = dict(locals())
    w = {n: given[n] for n in WEIGHT_ORDER}
    m = {n: given["m_" + n] for n in WEIGHT_ORDER}
    v = {n: given["v_" + n] for n in WEIGHT_ORDER}
    big_names = tuple(n for n, _, _ in BIG)
    matmul_names = tuple(n for n in big_names if n != "ssd_conv_w")

    conv_hi = w["ssd_conv_w"].astype(BF16)
    conv_lo = (w["ssd_conv_w"] - conv_hi.astype(F32)).astype(BF16)
    row_pad = ((0, 0), (0, FFN_BLOCK_PAD - FFN_BLOCK), (0, 0))
    as_rows = {"w_gate": jnp.swapaxes(w["w_gate"], 1, 2), "w_up": jnp.swapaxes(w["w_up"], 1, 2), "w_down": w["w_down"]}
    to_send = [jnp.pad(as_rows[n].astype(BF16), row_pad) if n in as_rows else w[n].astype(BF16) for n in matmul_names]
    gathered = _gather_blocks(to_send + [jnp.stack([conv_hi, conv_lo])], [True] * len(matmul_names) + [False],
                              name="gather_weights")
    conv_pair = gathered[-1][0].astype(F32)
    conv_full = jnp.moveaxis(conv_pair[:, 0] + conv_pair[:, 1], 0, 2).reshape(DEPTH, SSD_CONV, SSD_CONV_DIM)
    blks = []
    for i in range(DEPTH):
        blk = {n: gathered[j][i] for j, n in enumerate(matmul_names)}
        blk["ssd_conv_w"] = conv_full[i]
        blks.append(blk)

    loss, grad_x, layer_grads, g_final = _example_step(x[0], loss_target[0], w, blks)

    small = jnp.concatenate([g_final if n == "norm_final" else jnp.stack([layer_grads[i][n] for i in range(DEPTH)]).reshape(-1)
                             for n, _ in SMALL])
    small_slots = jnp.pad(small, (0, N_DEV * SMALL_ROWS * LANE - small.shape[0])).reshape(N_DEV, SMALL_ROWS, LANE)
    received = _exchange_blocks([tuple(layer_grads[i][n] for i in range(DEPTH)) for n in big_names] + [(small_slots,)],
                                name="exchange_gradients")

    results = {}
    for n, recv in zip(big_names, received):
        if n in ("w_gate", "w_up"):
            g = jnp.swapaxes(_sum_senders(recv, name=f"sum_{n}"), 1, 2)
            results[n] = [g, *_adamw_blocks(g, w[n], m[n], v[n], name=f"adamw_{n}")]
        else:
            results[n] = _sum_adamw(recv, w[n], m[n], v[n], name=f"sum_adamw_{n}")
    g_part = _sum_slots(received[-1].reshape(N_DEV, SMALL_ROWS, LANE), name="sum_replicated")
    g_small = _gather_blocks([g_part], [False], name="gather_replicated")[0][0].reshape(N_DEV * SMALL_ROWS, LANE)
    small_rows = N_DEV * SMALL_ROWS
    d_small, m_small, v_small = _adamw_rows(
        g_small, _pack([w[n] for n, _ in SMALL], small_rows, F32), _pack([m[n] for n, _ in SMALL], small_rows, F32),
        _pack([v[n] for n, _ in SMALL], small_rows, F32), name="adamw_replicated")
    for k, packed in enumerate((g_small, d_small, m_small, v_small)):
        for n, arr in _unpack(packed, SMALL).items():
            results.setdefault(n, [None] * 4)[k] = arr

    outs = [results[n][k] for k in range(4) for n in WEIGHT_ORDER]
    total_loss = lax.psum(loss[0, 0], ("x", "y", "c"))
    return (total_loss, grad_x[None], *outs)
```

```python
import functools

import jax
import jax.numpy as jnp
from jax import lax
from jax.experimental import pallas as pl
from jax.experimental.pallas import tpu as pltpu

F32 = jnp.float32
BF16 = jnp.bfloat16
MESH_ID = pl.DeviceIdType.MESH

N_DEV = 8
DEPTH = 2
D_MODEL = 1024
S5_GROUPS = 64
S5_GROUP = 16
S5_STATE = 64
S5_LANES = S5_GROUPS * S5_STATE
SSD_HEADS = 16
SSD_HEAD_DIM = 64
SSD_STATE = 128
SSD_CHUNK = 128
SSD_CONV = 4
SSD_CONV_DIM = 1536
FFN_HIDDEN = 2816
IN_PROJ = 3600
EPS = 1e-6
LANE = 128
SUBLANE = 8
VMEM_LIMIT = 56 * 1024 * 1024

ADAM_LR = 0.001
ADAM_B1 = 0.9
ADAM_B2 = 0.999
ADAM_EPS = 1e-08
ADAM_WD = 0.01
ADAM_STEP = 10

TOK_TILE = 256
S5_TILE = 128
S5_SEG = S5_TILE // SUBLANE


def _sigmoid(x):
    return jax.nn.sigmoid(x)


def _silu(x):
    return x * _sigmoid(x)


def _gelu(x):
    return 0.5 * x * (1.0 + jnp.tanh(0.7978845608028654 * (x + 0.044715 * (x * x * x))))


def _softplus(x):
    return jnp.maximum(x, 0.0) + jnp.log(1.0 + jnp.exp(-jnp.abs(x)))


def _rms(x, g):
    r = lax.rsqrt(jnp.mean(x * x, axis=-1, keepdims=True) + EPS)
    return x * r * g


def _nn(a, b):
    return lax.dot_general(a.astype(BF16), b.astype(BF16), (((1,), (0,)), ((), ())), preferred_element_type=F32)


def _nt(a, b):
    return lax.dot_general(a.astype(BF16), b.astype(BF16), (((1,), (1,)), ((), ())), preferred_element_type=F32)


def _tn(a, b):
    return lax.dot_general(a.astype(BF16), b.astype(BF16), (((0,), (0,)), ((), ())), preferred_element_type=F32)


def _nn_f32(a, b):
    return lax.dot_general(a, b, (((1,), (0,)), ((), ())), precision=lax.Precision.HIGHEST, preferred_element_type=F32)


def _tn_f32(a, b):
    return lax.dot_general(a, b, (((0,), (0,)), ((), ())), precision=lax.Precision.HIGHEST, preferred_element_type=F32)


@jax.custom_vjp
def _nn_d(a, b):
    return _nn(a, b)


_nn_d.defvjp(lambda a, b: (_nn(a, b), (a, b)), lambda r, g: (_nt(g, r[1]), _tn(r[0], g)))


@jax.custom_vjp
def _nt_d(a, b):
    return _nt(a, b)


_nt_d.defvjp(lambda a, b: (_nt(a, b), (a, b)), lambda r, g: (_nn(g, r[1]), _tn(g, r[0])))


@jax.custom_vjp
def _tn_d(a, b):
    return _tn(a, b)


_tn_d.defvjp(lambda a, b: (_tn(a, b), (a, b)), lambda r, g: (_nt(r[1], g), _nn(r[0], g)))


@jax.custom_vjp
def _cumsum_rows(tri, x):
    return _nn_f32(tri, x)


_cumsum_rows.defvjp(lambda tri, x: (_nn_f32(tri, x), tri), lambda tri, g: (jnp.zeros_like(tri), _tn_f32(tri, g)))


def _full(shape):
    zeros = (0,) * len(shape)
    return pl.BlockSpec(shape, lambda *_: zeros)


def _const(shape):
    zeros = (0,) * len(shape)
    return pl.BlockSpec(shape, lambda *_: zeros, pipeline_mode=pl.Buffered(1))


def _rows(tile, width, n_tiles=None):
    if n_tiles is None:
        return pl.BlockSpec((tile, width), lambda i: (i, 0))
    return pl.BlockSpec((tile, width), lambda i: (n_tiles - 1 - i, 0))


def _call(body, name, grid, in_specs, out_specs, out_shape, scratch=()):
    return pl.pallas_call(
        body, name=name, grid=grid, in_specs=in_specs, out_specs=out_specs, out_shape=out_shape,
        scratch_shapes=list(scratch),
        compiler_params=pltpu.CompilerParams(dimension_semantics=("arbitrary",) * len(grid),
                                             vmem_limit_bytes=VMEM_LIMIT))


def _sds(shape, dtype=F32):
    return jax.ShapeDtypeStruct(shape, dtype)


def _tile_of(n, cap=512):
    if n <= LANE:
        return n
    best = LANE
    for t in range(LANE, cap + 1, LANE):
        if n % t == 0:
            best = t
    return best


def _inproj_fwd(x, nm, wu, wz, wx, wd, name):
    n_tok = x.shape[0]
    tm = TOK_TILE

    def body(x_ref, nm_ref, wu_ref, wz_ref, wx_ref, wd_ref, u_ref, z_ref, xbc_ref, dt_ref):
        h = _rms(x_ref[...], nm_ref[...]).astype(BF16)
        u_ref[...] = _nn(h, wu_ref[...])
        z_ref[...] = _nn(h, wz_ref[...])
        xbc_ref[...] = _nn(h, wx_ref[...])
        dt_ref[...] = _nn(h, wd_ref[...])

    return _call(
        body, name, (n_tok // tm,),
        [_rows(tm, D_MODEL), _const((1, D_MODEL)), _const(wu.shape), _const(wz.shape), _const(wx.shape), _const(wd.shape)],
        [_rows(tm, D_MODEL), _rows(tm, D_MODEL), _rows(tm, SSD_CONV_DIM), _rows(tm, LANE)],
        [_sds((n_tok, D_MODEL)), _sds((n_tok, D_MODEL)), _sds((n_tok, SSD_CONV_DIM)), _sds((n_tok, LANE))],
    )(x, nm, wu, wz, wx, wd)


def _inproj_bwd(x, nm, du, dz, dxbc, ddt, dres, wu, wz, wx, wd, name):
    n_tok = x.shape[0]
    tm = TOK_TILE

    def body(x_ref, nm_ref, du_ref, dz_ref, dxbc_ref, ddt_ref, dres_ref, wu_ref, wz_ref, wx_ref, wd_ref,
             dx_ref, h_ref, dnm_ref):
        dh = (_nt(du_ref[...], wu_ref[...]) + _nt(dz_ref[...], wz_ref[...])
              + _nt(dxbc_ref[...], wx_ref[...]) + _nt(ddt_ref[...], wd_ref[...]))
        h, vjp = jax.vjp(_rms, x_ref[...], nm_ref[...])
        dx, dnm = vjp(dh)
        dx_ref[...] = dres_ref[...] + dx
        h_ref[...] = h.astype(BF16)

        @pl.when(pl.program_id(0) == 0)
        def _():
            dnm_ref[...] = jnp.zeros_like(dnm_ref)

        dnm_ref[...] += dnm

    return _call(
        body, name, (n_tok // tm,),
        [_rows(tm, D_MODEL), _const((1, D_MODEL)), _rows(tm, D_MODEL), _rows(tm, D_MODEL), _rows(tm, SSD_CONV_DIM),
         _rows(tm, LANE), _rows(tm, D_MODEL), _const(wu.shape), _const(wz.shape), _const(wx.shape), _const(wd.shape)],
        [_rows(tm, D_MODEL), _rows(tm, D_MODEL), _full((1, D_MODEL))],
        [_sds((n_tok, D_MODEL)), _sds((n_tok, D_MODEL), BF16), _sds((1, D_MODEL))],
    )(x, nm, du, dz, dxbc, ddt, dres, wu, wz, wx, wd)


def _ffn_act(gt, up):
    return _silu(gt) * up


FFN_BLOCK = FFN_HIDDEN // N_DEV
FFN_BLOCK_PAD = -(-FFN_BLOCK // LANE) * LANE


FFN_PAD = N_DEV * FFN_BLOCK_PAD


def _mix_ffn_fwd(x0, ya, yb, wo, nf, wg, wu, wd, name):
    n_tok = x0.shape[0]
    tm = TOK_TILE

    def body(x0_ref, ya_ref, yb_ref, wo_ref, nf_ref, wg_ref, wu_ref, wd_ref, x1_ref, x2_ref):
        x1 = x0_ref[...] + _nn(ya_ref[...], wo_ref[:D_MODEL, :]) + _nn(yb_ref[...], wo_ref[D_MODEL:, :])
        h = _rms(x1, nf_ref[...]).astype(BF16)
        x1_ref[...] = x1
        x2_ref[...] = x1 + _nn(_ffn_act(_nt(h, wg_ref[...]), _nt(h, wu_ref[...])), wd_ref[...])

    return _call(
        body, name, (n_tok // tm,),
        [_rows(tm, D_MODEL), _rows(tm, D_MODEL), _rows(tm, D_MODEL), _const(wo.shape),
         _const((1, D_MODEL)), _const(wg.shape), _const(wu.shape), _const(wd.shape)],
        [_rows(tm, D_MODEL), _rows(tm, D_MODEL)],
        [_sds((n_tok, D_MODEL)), _sds((n_tok, D_MODEL))],
    )(x0, ya, yb, wo, nf, wg, wu, wd)


def _mix_ffn_bwd(x1, dx2, wo, nf, wg, wu, wd, name):
    n_tok = x1.shape[0]
    tm = TOK_TILE
    n_chunks = 3
    hc = FFN_PAD // n_chunks

    def body(x1_ref, dx2_ref, wo_ref, nf_ref, wg_ref, wu_ref, wd_ref,
             dx1_ref, dya_ref, dyb_ref, h_ref, a_ref, dgt_ref, dup_ref, dnf_ref):
        dx2 = dx2_ref[...]
        dx2b = dx2.astype(BF16)
        h, rms_vjp = jax.vjp(_rms, x1_ref[...], nf_ref[...])
        hb = h.astype(BF16)
        dh = jnp.zeros_like(h)
        for c in range(n_chunks):
            rows = pl.ds(c * hc, hc)
            a, act_vjp = jax.vjp(_ffn_act, _nt(hb, wg_ref[rows, :]), _nt(hb, wu_ref[rows, :]))
            dgt, dup = act_vjp(_nt(dx2b, wd_ref[rows, :]))
            a_ref[:, c * hc:(c + 1) * hc] = a.astype(BF16)
            dgt_ref[:, c * hc:(c + 1) * hc] = dgt.astype(BF16)
            dup_ref[:, c * hc:(c + 1) * hc] = dup.astype(BF16)
            dh = dh + _nn(dgt, wg_ref[rows, :]) + _nn(dup, wu_ref[rows, :])
        dx, dnf = rms_vjp(dh)
        dx1 = dx2 + dx
        dx1_ref[...] = dx1
        dya_ref[...] = _nt(dx1, wo_ref[:D_MODEL, :])
        dyb_ref[...] = _nt(dx1, wo_ref[D_MODEL:, :])
        h_ref[...] = hb

        @pl.when(pl.program_id(0) == 0)
        def _():
            dnf_ref[...] = jnp.zeros_like(dnf_ref)

        dnf_ref[...] += dnf

    hidden = _rows(tm, FFN_PAD)
    return _call(
        body, name, (n_tok // tm,),
        [_rows(tm, D_MODEL), _rows(tm, D_MODEL), _const(wo.shape), _const((1, D_MODEL)),
         _const(wg.shape), _const(wu.shape), _const(wd.shape)],
        [_rows(tm, D_MODEL), _rows(tm, D_MODEL), _rows(tm, D_MODEL), _rows(tm, D_MODEL), hidden, hidden, hidden,
         _full((1, D_MODEL))],
        [_sds((n_tok, D_MODEL)), _sds((n_tok, D_MODEL)), _sds((n_tok, D_MODEL)), _sds((n_tok, D_MODEL), BF16),
         _sds((n_tok, FFN_PAD), BF16), _sds((n_tok, FFN_PAD), BF16), _sds((n_tok, FFN_PAD), BF16), _sds((1, D_MODEL))],
    )(x1, dx2, wo, nf, wg, wu, wd)


def _loss_head(x, nf, target, name):
    n_tok = x.shape[0]
    tm = TOK_TILE

    def loss_of(xv, g, t):
        e = _rms(xv, g) - t
        return 0.5 * jnp.sum(jnp.sum(e * e, axis=-1, keepdims=True) * (1.0 / D_MODEL), axis=0, keepdims=True)

    def body(x_ref, nf_ref, t_ref, loss_ref, dx_ref, dnf_ref):
        loss, vjp = jax.vjp(functools.partial(loss_of, t=t_ref[...]), x_ref[...], nf_ref[...])
        dx, dnf = vjp(jnp.ones_like(loss))
        dx_ref[...] = dx

        @pl.when(pl.program_id(0) == 0)
        def _():
            dnf_ref[...] = jnp.zeros_like(dnf_ref)
            loss_ref[...] = jnp.zeros_like(loss_ref)

        dnf_ref[...] += dnf
        loss_ref[...] += jnp.broadcast_to(loss, loss_ref.shape)

    return _call(
        body, name, (n_tok // tm,),
        [_rows(tm, D_MODEL), _const((1, D_MODEL)), _rows(tm, D_MODEL)],
        [_full((SUBLANE, LANE)), _rows(tm, D_MODEL), _full((1, D_MODEL))],
        [_sds((SUBLANE, LANE)), _sds((n_tok, D_MODEL)), _sds((1, D_MODEL))],
    )(x, nf, target)


GRAD_WIRE = BF16


def _matmul_tn(a, b, name):
    n_tok, k1 = a.shape
    k2 = b.shape[1]
    t1, t2 = _tile_of(k1), _tile_of(k2)

    def body(a_ref, b_ref, o_ref):
        o_ref[...] = _tn(a_ref[...], b_ref[...]).astype(GRAD_WIRE)

    return pl.pallas_call(
        body, name=name, grid=(k1 // t1, k2 // t2),
        in_specs=[pl.BlockSpec((n_tok, t1), lambda i, j: (0, i)), pl.BlockSpec((n_tok, t2), lambda i, j: (0, j))],
        out_specs=pl.BlockSpec((t1, t2), lambda i, j: (i, j)),
        out_shape=_sds((k1, k2), GRAD_WIRE),
        compiler_params=pltpu.CompilerParams(dimension_semantics=("arbitrary", "arbitrary"), vmem_limit_bytes=VMEM_LIMIT),
    )(a, b)


def _tn_params():
    return pltpu.CompilerParams(dimension_semantics=("arbitrary", "arbitrary"), vmem_limit_bytes=VMEM_LIMIT)


def _matmul_tn_lhs_blocks(a, b, width, keep, name):
    n_tok, k1 = a.shape
    k2 = b.shape[1]
    t2 = _tile_of(k2)

    def body(a_ref, b_ref, o_ref):
        o_ref[...] = _tn(a_ref[...], b_ref[...])[:keep, :].astype(GRAD_WIRE)

    return pl.pallas_call(
        body, name=name, grid=(k1 // width, k2 // t2),
        in_specs=[pl.BlockSpec((n_tok, width), lambda d, j: (0, d)), pl.BlockSpec((n_tok, t2), lambda d, j: (0, j))],
        out_specs=pl.BlockSpec((None, keep, t2), lambda d, j: (d, 0, j)),
        out_shape=_sds((k1 // width, keep, k2), GRAD_WIRE), compiler_params=_tn_params(),
    )(a, b)


def _matmul_tn_pair(a0, a1, b, name):
    n_tok, k1 = a0.shape
    k2 = b.shape[1]
    t1, t2 = _tile_of(k1), _tile_of(k2)

    def body(a0_ref, a1_ref, b_ref, o_ref):
        @pl.when(pl.program_id(0) == 0)
        def _():
            o_ref[...] = _tn(a0_ref[...], b_ref[...]).astype(GRAD_WIRE)

        @pl.when(pl.program_id(0) == 1)
        def _():
            o_ref[...] = _tn(a1_ref[...], b_ref[...]).astype(GRAD_WIRE)

    lhs = pl.BlockSpec((n_tok, t1), lambda s, i, j: (0, i))
    return pl.pallas_call(
        body, name=name, grid=(2, k1 // t1, k2 // t2),
        in_specs=[lhs, lhs, pl.BlockSpec((n_tok, t2), lambda s, i, j: (0, j))],
        out_specs=pl.BlockSpec((None, t1, t2), lambda s, i, j: (s, i, j)),
        out_shape=_sds((2, k1, k2), GRAD_WIRE),
        compiler_params=pltpu.CompilerParams(dimension_semantics=("arbitrary",) * 3, vmem_limit_bytes=VMEM_LIMIT),
    )(a0, a1, b)


W_IN_BLOCK = IN_PROJ // N_DEV
W_IN_SPLITS = (D_MODEL, 2 * D_MODEL, 2 * D_MODEL + SSD_CONV_DIM)
RELAYOUT_TILE = 256


def _w_in_split(blocks, name):
    tr = RELAYOUT_TILE

    def body(b_ref, wu_ref, wz_ref, wx_ref, wd_ref):
        full = jnp.concatenate([b_ref[d] for d in range(N_DEV)], axis=1)
        wu_ref[...] = full[:, :W_IN_SPLITS[0]]
        wz_ref[...] = full[:, W_IN_SPLITS[0]:W_IN_SPLITS[1]]
        wx_ref[...] = full[:, W_IN_SPLITS[1]:W_IN_SPLITS[2]]
        wd_ref[...] = jnp.concatenate([full[:, W_IN_SPLITS[2]:], jnp.zeros((tr, LANE - SSD_HEADS), full.dtype)], axis=1)

    return _call(
        body, name, (D_MODEL // tr,), [pl.BlockSpec((N_DEV, tr, W_IN_BLOCK), lambda i: (0, i, 0))],
        [_rows(tr, D_MODEL), _rows(tr, D_MODEL), _rows(tr, SSD_CONV_DIM), _rows(tr, LANE)],
        [_sds((D_MODEL, D_MODEL), BF16), _sds((D_MODEL, D_MODEL), BF16), _sds((D_MODEL, SSD_CONV_DIM), BF16),
         _sds((D_MODEL, LANE), BF16)],
    )(blocks)


def _w_in_grad_blocks(gu, gz, gx, gdt, name):
    tr = RELAYOUT_TILE

    def body(gu_ref, gz_ref, gx_ref, gdt_ref, o_ref):
        full = jnp.concatenate([gu_ref[...], gz_ref[...], gx_ref[...], gdt_ref[...]], axis=1)
        for d in range(N_DEV):
            o_ref[d] = full[:, d * W_IN_BLOCK:(d + 1) * W_IN_BLOCK]

    return _call(
        body, name, (D_MODEL // tr,),
        [_rows(tr, D_MODEL), _rows(tr, D_MODEL), _rows(tr, SSD_CONV_DIM), _rows(tr, LANE)],
        [pl.BlockSpec((N_DEV, tr, W_IN_BLOCK), lambda i: (0, i, 0))], [_sds((N_DEV, D_MODEL, W_IN_BLOCK), gu.dtype)],
    )(gu, gz, gx, gdt)[0]


S5_SLICES = D_MODEL // LANE
S5_SLICE_STATES = S5_LANES // S5_SLICES
SCAN_LANES = 512


def _s5_scan(br_ref, bi_ref, a_r, a_i, car_r, car_i, ini_r, ini_i, reverse, xr_ref=None, xi_ref=None,
             acc_r=None, acc_i=None):
    n_rows = br_ref.shape[1]
    seg = n_rows // SUBLANE
    order = range(SUBLANE - 1, -1, -1) if reverse else range(SUBLANE)

    def rows(t):
        return pl.ds((seg - 1 - t) if reverse else t, SUBLANE, stride=seg)

    tiles_per = SCAN_LANES // LANE

    def load(ref, t, lb):
        return jnp.concatenate([ref[lb * tiles_per + j, rows(t), :] for j in range(tiles_per)], axis=1)

    def store(ref, t, lb, val):
        for j in range(tiles_per):
            ref[lb * tiles_per + j, rows(t), :] = val[:, j * LANE:(j + 1) * LANE]

    for lb in range(S5_LANES // SCAN_LANES):
        lanes = pl.ds(lb * SCAN_LANES, SCAN_LANES)
        ar1, ai1 = a_r[:, lb * SCAN_LANES:(lb + 1) * SCAN_LANES], a_i[:, lb * SCAN_LANES:(lb + 1) * SCAN_LANES]
        ar8 = jnp.broadcast_to(ar1, (SUBLANE, SCAN_LANES))
        ai8 = jnp.broadcast_to(ai1, (SUBLANE, SCAN_LANES))

        def local(t, c):
            sr, si = c
            return (ar8 * sr - ai8 * si + load(br_ref, t, lb), ar8 * si + ai8 * sr + load(bi_ref, t, lb))

        zero = jnp.zeros((SUBLANE, SCAN_LANES), F32)
        er, ei = lax.fori_loop(0, seg, local, (zero, zero))
        pr, pi = ar1, ai1
        for _ in range(seg.bit_length() - 1):
            pr, pi = pr * pr - pi * pi, 2.0 * pr * pi
        cr, ci = car_r[:, lanes], car_i[:, lanes]
        for s in order:
            ini_r[s:s + 1, lanes] = cr
            ini_i[s:s + 1, lanes] = ci
            cr, ci = pr * cr - pi * ci + er[s:s + 1, :], pr * ci + pi * cr + ei[s:s + 1, :]
        car_r[:, lanes] = cr
        car_i[:, lanes] = ci

        if xr_ref is None:
            def final(t, c):
                sr, si = c
                nr = ar8 * sr - ai8 * si + load(br_ref, t, lb)
                ni = ar8 * si + ai8 * sr + load(bi_ref, t, lb)
                store(br_ref, t, lb, nr)
                store(bi_ref, t, lb, ni)
                return nr, ni

            lax.fori_loop(0, seg, final, (ini_r[:, lanes], ini_i[:, lanes]))
        else:
            def final_acc(t, c):
                sr, si, gr, gi = c
                xr, xi = load(xr_ref, t, lb), load(xi_ref, t, lb)
                gr = gr + sr * xr + si * xi
                gi = gi + si * xr - sr * xi
                nr = ar8 * sr - ai8 * si + load(br_ref, t, lb)
                ni = ar8 * si + ai8 * sr + load(bi_ref, t, lb)
                store(br_ref, t, lb, nr)
                store(bi_ref, t, lb, ni)
                return nr, ni, gr, gi

            _, _, gr, gi = lax.fori_loop(0, seg, final_acc,
                                         (ini_r[:, lanes], ini_i[:, lanes], acc_r[:, lanes], acc_i[:, lanes]))
            acc_r[:, lanes] = gr
            acc_i[:, lanes] = gi


def _s5_tail(gg, q, sn):
    return _rms(gg * _sigmoid(q), sn)


S5_STATE_TILES = S5_LANES // LANE
TILES_PER_SLICE = S5_SLICE_STATES // LANE


def _put_states(ref, k, val):
    for j in range(TILES_PER_SLICE):
        ref[k * TILES_PER_SLICE + j] = val[:, j * LANE:(j + 1) * LANE]


def _get_states(ref, k):
    return jnp.concatenate([ref[k * TILES_PER_SLICE + j] for j in range(TILES_PER_SLICE)], axis=1)


def _state_rows(tile, n_tiles=None):
    if n_tiles is None:
        return pl.BlockSpec((S5_STATE_TILES, tile, LANE), lambda i: (0, i, 0))
    return pl.BlockSpec((S5_STATE_TILES, tile, LANE), lambda i: (0, n_tiles - 1 - i, 0))


def _s5_fwd(u, a_r, a_i, bdb, bcr, bci, dsk, wglu, bglu, sn, name):
    n_tok = u.shape[0]
    tc = S5_TILE
    sw = S5_SLICE_STATES

    def body(u_ref, ar_ref, ai_ref, bdb_ref, bcr_ref, bci_ref, d_ref, wg_ref, bg_ref, sn_ref,
             ya_ref, xr_ref, xi_ref, v_ref, car_r, car_i, ini_r, ini_i):
        @pl.when(pl.program_id(0) == 0)
        def _():
            car_r[...] = jnp.zeros_like(car_r)
            car_i[...] = jnp.zeros_like(car_i)

        u_t = u_ref[...]
        ub = u_t.astype(BF16)
        for k in range(S5_SLICES):
            bu = _nn(ub[:, k * LANE:(k + 1) * LANE], bdb_ref[k])
            _put_states(xr_ref, k, bu[:, :sw])
            _put_states(xi_ref, k, bu[:, sw:])
        _s5_scan(xr_ref, xi_ref, ar_ref[...], ai_ref[...], car_r, car_i, ini_r, ini_i, reverse=False)
        vs = [_nn(_get_states(xr_ref, k), bcr_ref[k]) - _nn(_get_states(xi_ref, k), bci_ref[k])
              for k in range(S5_SLICES)]
        v = jnp.concatenate(vs, axis=1) + d_ref[...] * u_t
        v_ref[...] = v
        gg = _gelu(v)
        ya_ref[...] = _s5_tail(gg, _nn(gg, wg_ref[...]) + bg_ref[...], sn_ref[...])

    return _call(
        body, name, (n_tok // tc,),
        [_rows(tc, D_MODEL), _const((1, S5_LANES)), _const((1, S5_LANES)), _const(bdb.shape), _const(bcr.shape),
         _const(bci.shape), _const((1, D_MODEL)), _const(wglu.shape), _const((1, D_MODEL)), _const((1, D_MODEL))],
        [_rows(tc, D_MODEL), _state_rows(tc), _state_rows(tc), _rows(tc, D_MODEL)],
        [_sds((n_tok, D_MODEL)), _sds((S5_STATE_TILES, n_tok, LANE)), _sds((S5_STATE_TILES, n_tok, LANE)),
         _sds((n_tok, D_MODEL))],
        scratch=[pltpu.VMEM((1, S5_LANES), F32), pltpu.VMEM((1, S5_LANES), F32),
                 pltpu.VMEM((SUBLANE, S5_LANES), F32), pltpu.VMEM((SUBLANE, S5_LANES), F32)],
    )(u, a_r, a_i, bdb, bcr, bci, dsk, wglu, bglu, sn)


def _s5_bwd(dya, v, u, xr, xi, a_r, a_i, bdb, bcr, bci, dsk, wglu, bglu, sn, name):
    n_tok = u.shape[0]
    tc = S5_TILE
    nt = n_tok // tc
    sw = S5_SLICE_STATES

    def body(dya_ref, v_ref, u_ref, xr_ref, xi_ref, ar_ref, ai_ref, bdb_ref, bcr_ref, bci_ref, d_ref, wg_ref, bg_ref, sn_ref,
             du_ref, gg_ref, dq_ref, gbdb_ref, gbcr_ref, gbci_ref, gar_ref, gai_ref, gd_ref, gbg_ref, gsn_ref,
             gr_ref, gi_ref, car_r, car_i, ini_r, ini_i):
        @pl.when(pl.program_id(0) == 0)
        def _():
            for r in (car_r, car_i, gbdb_ref, gbcr_ref, gbci_ref, gar_ref, gai_ref, gd_ref, gbg_ref, gsn_ref):
                r[...] = jnp.zeros_like(r)

        u_t = u_ref[...]
        gg, gelu_vjp = jax.vjp(_gelu, v_ref[...])
        _, tail_vjp = jax.vjp(_s5_tail, gg, _nn(gg, wg_ref[...]) + bg_ref[...], sn_ref[...])
        dgg, dq, dsn = tail_vjp(dya_ref[...])
        (dv,) = gelu_vjp(dgg + _nt(dq, wg_ref[...]))
        gg_ref[...] = gg.astype(BF16)
        dq_ref[...] = dq.astype(BF16)
        gd_ref[...] += jnp.sum(dv * u_t, axis=0, keepdims=True)
        gbg_ref[...] += jnp.sum(dq, axis=0, keepdims=True)
        gsn_ref[...] += dsn
        dvb = dv.astype(BF16)
        for k in range(S5_SLICES):
            dvk = dvb[:, k * LANE:(k + 1) * LANE]
            _put_states(gr_ref, k, _nt(dvk, bcr_ref[k]))
            _put_states(gi_ref, k, -_nt(dvk, bci_ref[k]))
            gbcr_ref[k] += _tn(_get_states(xr_ref, k), dvk)
            gbci_ref[k] -= _tn(_get_states(xi_ref, k), dvk)
        _s5_scan(gr_ref, gi_ref, ar_ref[...], -ai_ref[...], car_r, car_i, ini_r, ini_i, reverse=True,
                 xr_ref=xr_ref, xi_ref=xi_ref, acc_r=gar_ref, acc_i=gai_ref)
        ub = u_t.astype(BF16)
        dus = []
        for k in range(S5_SLICES):
            gk_r, gk_i = _get_states(gr_ref, k).astype(BF16), _get_states(gi_ref, k).astype(BF16)
            bk = bdb_ref[k]
            dus.append(_nt(gk_r, bk[:, :sw]) + _nt(gk_i, bk[:, sw:]))
            uk = ub[:, k * LANE:(k + 1) * LANE]
            gbdb_ref[k, :, :sw] += _tn(uk, gk_r)
            gbdb_ref[k, :, sw:] += _tn(uk, gk_i)
        du_ref[...] = jnp.concatenate(dus, axis=1) + d_ref[...] * dv

    rev = functools.partial(_rows, n_tiles=nt)
    return _call(
        body, name, (nt,),
        [rev(tc, D_MODEL), rev(tc, D_MODEL), rev(tc, D_MODEL), _state_rows(tc, nt), _state_rows(tc, nt),
         _const((1, S5_LANES)), _const((1, S5_LANES)), _const(bdb.shape), _const(bcr.shape), _const(bci.shape),
         _const((1, D_MODEL)), _const(wglu.shape), _const((1, D_MODEL)), _const((1, D_MODEL))],
        [rev(tc, D_MODEL), rev(tc, D_MODEL), rev(tc, D_MODEL), _full(bdb.shape), _full(bcr.shape), _full(bci.shape),
         _full((SUBLANE, S5_LANES)), _full((SUBLANE, S5_LANES)), _full((1, D_MODEL)), _full((1, D_MODEL)), _full((1, D_MODEL))],
        [_sds((n_tok, D_MODEL)), _sds((n_tok, D_MODEL), BF16), _sds((n_tok, D_MODEL), BF16), _sds(bdb.shape), _sds(bcr.shape),
         _sds(bci.shape), _sds((SUBLANE, S5_LANES)), _sds((SUBLANE, S5_LANES)), _sds((1, D_MODEL)), _sds((1, D_MODEL)),
         _sds((1, D_MODEL))],
        scratch=[pltpu.VMEM((S5_STATE_TILES, tc, LANE), F32), pltpu.VMEM((S5_STATE_TILES, tc, LANE), F32),
                 pltpu.VMEM((1, S5_LANES), F32), pltpu.VMEM((1, S5_LANES), F32),
                 pltpu.VMEM((SUBLANE, S5_LANES), F32), pltpu.VMEM((SUBLANE, S5_LANES), F32)],
    )(dya, v, u, xr, xi, a_r, a_i, bdb, bcr, bci, dsk, wglu, bglu, sn)


SSD_WIDTH = SSD_HEADS * SSD_HEAD_DIM
SSD_GROUPS = 2
HEADS_PER_GROUP = SSD_HEADS // SSD_GROUPS


def _take(x, axis, start, size):
    n = x.shape[axis]

    def sl(v):
        return lax.slice_in_dim(v, start, start + size, axis=axis)

    @jax.custom_vjp
    def f(v):
        return sl(v)

    def bwd(_, g):
        parts = []
        if start:
            parts.append(jnp.zeros(g.shape[:axis] + (start,) + g.shape[axis + 1:], g.dtype))
        parts.append(g)
        if n - start - size:
            parts.append(jnp.zeros(g.shape[:axis] + (n - start - size,) + g.shape[axis + 1:], g.dtype))
        return (jnp.concatenate(parts, axis=axis) if len(parts) > 1 else g,)

    f.defvjp(lambda v: (sl(v), None), bwd)
    return f(x)


def _lane_of(x, h):
    col = lax.broadcasted_iota(jnp.int32, x.shape, 1)
    return jnp.sum(jnp.where(col == h, x, 0.0), axis=1, keepdims=True)


def _ssd_chunk(xc, z, dt, dtb, alog, dvec, gn, st, nn, nt, tn, cumsum, take):
    t_len = xc.shape[0]
    xa = _silu(xc)
    dtp = _softplus(dt + dtb)
    d_a = dtp * (-jnp.exp(alog))
    row = lax.broadcasted_iota(jnp.int32, (t_len, t_len), 0)
    col = lax.broadcasted_iota(jnp.int32, (t_len, t_len), 1)
    causal = row >= col
    cum = cumsum(causal.astype(F32), d_a)
    eye = (row == col).astype(F32)
    ys, sts = [], []
    for g in range(SSD_GROUPS):
        bg = take(xa, 1, SSD_WIDTH + g * SSD_STATE, SSD_STATE)
        cg = take(xa, 1, SSD_WIDTH + (SSD_GROUPS + g) * SSD_STATE, SSD_STATE)
        cb = nt(cg, bg)
        for r in range(HEADS_PER_GROUP):
            h = g * HEADS_PER_GROUP + r
            cc = _lane_of(cum, h)
            cr = jnp.sum(cc * eye, axis=0, keepdims=True)
            decay = jnp.exp(jnp.where(causal, cc - cr, -1e30))
            xh = take(xa, 1, h * SSD_HEAD_DIM, SSD_HEAD_DIM)
            xdt = xh * _lane_of(dtp, h)
            sth = take(st, 0, h * SSD_HEAD_DIM, SSD_HEAD_DIM)
            c_last = jnp.sum(jnp.where(row[:, :1] == t_len - 1, cc, 0.0), axis=0, keepdims=True)
            y = nn(cb * decay, xdt) + jnp.exp(cc) * nt(cg, sth) + _lane_of(dvec, h) * xh
            ys.append(y)
            sts.append(jnp.exp(c_last) * sth + tn(xdt * jnp.exp(c_last - cc), bg))
    y = jnp.concatenate(ys, axis=1) * _silu(z)
    return _rms(y, gn), jnp.concatenate(sts, axis=0)


def _shift_back(cur, prev, j):
    if j == 0:
        return cur
    row = lax.broadcasted_iota(jnp.int32, cur.shape, 0)
    return jnp.where(row < j, pltpu.roll(prev, j, 0), pltpu.roll(cur, j, 0))


def _shift_ahead(cur, nxt, j):
    if j == 0:
        return cur
    n = cur.shape[0]
    row = lax.broadcasted_iota(jnp.int32, cur.shape, 0)
    return jnp.where(row >= n - j, pltpu.roll(nxt, n - j, 0), pltpu.roll(cur, n - j, 0))


def _conv(cur, prev, w, b):
    out = b + w[SSD_CONV - 1:SSD_CONV, :] * cur
    for k in range(SSD_CONV - 1):
        out = out + w[k:k + 1, :] * _shift_back(cur, prev, SSD_CONV - 1 - k)
    return out


def _ssd_fwd(xbc, z, dt, conv_w, conv_b, dtb, alog, dvec, gn, name):
    n_tok = xbc.shape[0]
    tc = SSD_CHUNK
    nc = n_tok // tc
    st_rows = SSD_HEADS * SSD_HEAD_DIM

    def body(cur_ref, prev_ref, z_ref, dt_ref, w_ref, b_ref, dtb_ref, alog_ref, dvec_ref, gn_ref,
             yb_ref, stin_ref, st_ref):
        i = pl.program_id(0)

        @pl.when(i == 0)
        def _():
            st_ref[...] = jnp.zeros_like(st_ref)

        prev = jnp.where(i > 0, prev_ref[...], 0.0)
        xc = _conv(cur_ref[...], prev, w_ref[...], b_ref[...])
        st = st_ref[...]
        stin_ref[0] = st
        yb, st_new = _ssd_chunk(xc, z_ref[...], dt_ref[...], dtb_ref[...], alog_ref[...], dvec_ref[...], gn_ref[...], st,
                                _nn, _nt, _tn, _nn_f32, lambda v, axis, start, size: lax.slice_in_dim(v, start, start + size, axis=axis))
        yb_ref[...] = yb
        st_ref[...] = st_new

    return _call(
        body, name, (nc,),
        [_rows(tc, SSD_CONV_DIM), pl.BlockSpec((tc, SSD_CONV_DIM), lambda i: (jnp.maximum(i - 1, 0), 0)),
         _rows(tc, D_MODEL), _rows(tc, LANE), _const((SSD_CONV, SSD_CONV_DIM)), _const((1, SSD_CONV_DIM)),
         _const((1, LANE)), _const((1, LANE)), _const((1, LANE)), _const((1, D_MODEL))],
        [_rows(tc, D_MODEL), pl.BlockSpec((1, st_rows, SSD_STATE), lambda i: (i, 0, 0))],
        [_sds((n_tok, D_MODEL)), _sds((nc, st_rows, SSD_STATE))],
        scratch=[pltpu.VMEM((st_rows, SSD_STATE), F32)],
    )(xbc, xbc, z, dt, conv_w, conv_b, dtb, alog, dvec, gn)


def _ssd_bwd(dyb, xbc, z, dt, stin, conv_w, conv_b, dtb, alog, dvec, gn, name):
    n_tok = xbc.shape[0]
    tc = SSD_CHUNK
    nc = n_tok // tc
    st_rows = SSD_HEADS * SSD_HEAD_DIM

    def body(dyb_ref, cur_ref, prev_ref, z_ref, dt_ref, stin_ref, w_ref, b_ref, dtb_ref, alog_ref, dvec_ref, gn_ref,
             dxbc_ref, dz_ref, ddt_ref, gw_ref, gb_ref, gdtb_ref, galog_ref, gdvec_ref, ggn_ref,
             dst_ref, dxc_next_ref):
        i = pl.program_id(0)

        @pl.when(i == 0)
        def _():
            for r in (dst_ref, dxc_next_ref, gw_ref, gb_ref, gdtb_ref, galog_ref, gdvec_ref, ggn_ref):
                r[...] = jnp.zeros_like(r)

        cur = cur_ref[...]
        prev = jnp.where(i < nc - 1, prev_ref[...], 0.0)
        w = w_ref[...]
        xc = _conv(cur, prev, w, b_ref[...])
        chunk = functools.partial(_ssd_chunk, nn=_nn_d, nt=_nt_d, tn=_tn_d, cumsum=_cumsum_rows, take=_take)
        _, vjp = jax.vjp(chunk, xc, z_ref[...], dt_ref[...], dtb_ref[...], alog_ref[...], dvec_ref[...], gn_ref[...],
                         stin_ref[0])
        dxc, dz, ddt, gdtb, galog, gdvec, ggn, dst = vjp((dyb_ref[...], dst_ref[...]))
        dst_ref[...] = dst
        dz_ref[...] = dz
        ddt_ref[...] = ddt
        gdtb_ref[...] += gdtb
        galog_ref[...] += galog
        gdvec_ref[...] += gdvec
        ggn_ref[...] += ggn
        dxc_next = dxc_next_ref[...]
        dxbc = w[SSD_CONV - 1:SSD_CONV, :] * dxc
        gws = []
        for k in range(SSD_CONV - 1):
            j = SSD_CONV - 1 - k
            dxbc = dxbc + w[k:k + 1, :] * _shift_ahead(dxc, dxc_next, j)
            gws.append(jnp.sum(dxc * _shift_back(cur, prev, j), axis=0, keepdims=True))
        gws.append(jnp.sum(dxc * cur, axis=0, keepdims=True))
        dxbc_ref[...] = dxbc
        gw_ref[...] += jnp.concatenate(gws, axis=0)
        gb_ref[...] += jnp.sum(dxc, axis=0, keepdims=True)
        dxc_next_ref[...] = dxc

    rev = functools.partial(_rows, n_tiles=nc)
    return _call(
        body, name, (nc,),
        [rev(tc, D_MODEL), rev(tc, SSD_CONV_DIM),
         pl.BlockSpec((tc, SSD_CONV_DIM), lambda i: (jnp.maximum(nc - 2 - i, 0), 0)),
         rev(tc, D_MODEL), rev(tc, LANE), pl.BlockSpec((1, st_rows, SSD_STATE), lambda i: (nc - 1 - i, 0, 0)),
         _const((SSD_CONV, SSD_CONV_DIM)), _const((1, SSD_CONV_DIM)), _const((1, LANE)), _const((1, LANE)),
         _const((1, LANE)), _const((1, D_MODEL))],
        [rev(tc, SSD_CONV_DIM), rev(tc, D_MODEL), rev(tc, LANE), _full((SSD_CONV, SSD_CONV_DIM)), _full((1, SSD_CONV_DIM)),
         _full((1, LANE)), _full((1, LANE)), _full((1, LANE)), _full((1, D_MODEL))],
        [_sds((n_tok, SSD_CONV_DIM)), _sds((n_tok, D_MODEL)), _sds((n_tok, LANE)), _sds((SSD_CONV, SSD_CONV_DIM)),
         _sds((1, SSD_CONV_DIM)), _sds((1, LANE)), _sds((1, LANE)), _sds((1, LANE)), _sds((1, D_MODEL))],
        scratch=[pltpu.VMEM((st_rows, SSD_STATE), F32), pltpu.VMEM((tc, SSD_CONV_DIM), F32)],
    )(dyb, xbc, xbc, z, dt, stin, conv_w, conv_b, dtb, alog, dvec, gn)


@jax.custom_vjp
def _expand_cols(x, e):
    return _nn_f32(x, e)


_expand_cols.defvjp(
    lambda x, e: (_nn_f32(x, e), e),
    lambda e, g: (lax.dot_general(g, e, (((1,), (1,)), ((), ())), precision=lax.Precision.HIGHEST,
                                  preferred_element_type=F32), jnp.zeros_like(e)))


def _s5_discretize(lam_re, lam_im, log_step, b_re, b_im, expand):
    step = jnp.exp(log_step)
    mag = jnp.exp(lam_re * step)
    ang = lam_im * step
    a_r = mag * jnp.cos(ang)
    a_i = mag * jnp.sin(ang)
    den = lam_re * lam_re + lam_im * lam_im
    n_r = a_r - 1.0
    coef_r = _expand_cols((n_r * lam_re + a_i * lam_im) / den, expand)
    coef_i = _expand_cols((a_i * lam_re - n_r * lam_im) / den, expand)
    return a_r, a_i, coef_r * b_re - coef_i * b_im, coef_r * b_im + coef_i * b_re


def _expand_matrix():
    p = lax.broadcasted_iota(jnp.int32, (S5_STATE, S5_STATE * S5_GROUP), 0)
    c = lax.broadcasted_iota(jnp.int32, (S5_STATE, S5_STATE * S5_GROUP), 1)
    return (c // S5_GROUP == p).astype(F32)


def _s5_discretize_fwd(lam_re, lam_im, log_step, b_re, b_im, name):
    def body(lr_ref, li_ref, ls_ref, br_ref, bi_ref, ar_ref, ai_ref, bbr_ref, bbi_ref):
        outs = _s5_discretize(lr_ref[...], li_ref[...], ls_ref[...], br_ref[...], bi_ref[...], _expand_matrix())
        for r, o in zip((ar_ref, ai_ref, bbr_ref, bbi_ref), outs):
            r[...] = o

    sq, wide = (S5_GROUPS, S5_STATE), (S5_GROUPS, S5_STATE * S5_GROUP)
    return _call(body, name, (1,), [_full(sq), _full(sq), _full((S5_GROUPS, 1)), _full(wide), _full(wide)],
                 [_full(sq), _full(sq), _full(wide), _full(wide)], [_sds(sq), _sds(sq), _sds(wide), _sds(wide)],
                 )(lam_re, lam_im, log_step, b_re, b_im)


def _s5_discretize_bwd(lam_re, lam_im, log_step, b_re, b_im, g_ar8, g_ai8, g_bbr, g_bbi, name):
    def body(lr_ref, li_ref, ls_ref, br_ref, bi_ref, gar_ref, gai_ref, gbbr_ref, gbbi_ref,
             glr_ref, gli_ref, gls_ref, gbr_ref, gbi_ref):
        _, vjp = jax.vjp(functools.partial(_s5_discretize, expand=_expand_matrix()),
                         lr_ref[...], li_ref[...], ls_ref[...], br_ref[...], bi_ref[...])
        grads = vjp((jnp.sum(gar_ref[...], axis=0), jnp.sum(gai_ref[...], axis=0), gbbr_ref[...], gbbi_ref[...]))
        for r, g in zip((glr_ref, gli_ref, gls_ref, gbr_ref, gbi_ref), grads):
            r[...] = g

    sq, wide, col = (S5_GROUPS, S5_STATE), (S5_GROUPS, S5_STATE * S5_GROUP), (S5_GROUPS, 1)
    part = (SUBLANE,) + sq
    return _call(body, name, (1,),
                 [_full(sq), _full(sq), _full(col), _full(wide), _full(wide), _full(part), _full(part), _full(wide), _full(wide)],
                 [_full(sq), _full(sq), _full(col), _full(wide), _full(wide)],
                 [_sds(sq), _sds(sq), _sds(col), _sds(wide), _sds(wide)],
                 )(lam_re, lam_im, log_step, b_re, b_im, g_ar8, g_ai8, g_bbr, g_bbi)


GROUPS_PER_SLICE = LANE // S5_GROUP


def _block_diag_b(bb):
    t = bb.reshape(S5_SLICES, GROUPS_PER_SLICE, S5_STATE, S5_GROUP)
    eye = jnp.eye(GROUPS_PER_SLICE, dtype=bb.dtype)
    return jnp.einsum("kgph,gf->kghfp", t, eye).reshape(S5_SLICES, LANE, S5_SLICE_STATES)


def _block_diag_b_inv(m):
    t = m.reshape(S5_SLICES, GROUPS_PER_SLICE, S5_GROUP, GROUPS_PER_SLICE, S5_STATE)
    return jnp.einsum("kghgp->kgph", t).reshape(S5_GROUPS, S5_STATE * S5_GROUP)


def _block_diag_c(c):
    t = c.reshape(S5_SLICES, GROUPS_PER_SLICE, S5_GROUP, S5_STATE)
    eye = jnp.eye(GROUPS_PER_SLICE, dtype=c.dtype)
    return jnp.einsum("kghp,gf->kgpfh", t, eye).reshape(S5_SLICES, S5_SLICE_STATES, LANE)


def _block_diag_c_inv(m):
    t = m.reshape(S5_SLICES, GROUPS_PER_SLICE, S5_STATE, GROUPS_PER_SLICE, S5_GROUP)
    return jnp.einsum("kgpgh->kghp", t).reshape(S5_GROUPS, S5_GROUP, S5_STATE)


def _pad_lanes(v):
    return jnp.pad(v.reshape(1, -1), ((0, 0), (0, LANE - v.shape[0])))


def _prepare_layer(w, blk, i):
    p = {}
    p["wu"], p["wz"], p["wx"], p["wd"] = _w_in_split(blk["w_in"], name=f"w_in_split_{i}")
    p["nm"] = w["norm_mix"][i].reshape(1, D_MODEL)
    p["lam_re"], p["lam_im"] = w["s5_lam_re"][i], w["s5_lam_im"][i]
    p["log_step"] = w["s5_log_step"][i].reshape(S5_GROUPS, 1)
    p["b_re"] = w["s5_b_re"][i].reshape(S5_GROUPS, S5_STATE * S5_GROUP)
    p["b_im"] = w["s5_b_im"][i].reshape(S5_GROUPS, S5_STATE * S5_GROUP)
    a_r, a_i, bb_r, bb_i = _s5_discretize_fwd(p["lam_re"], p["lam_im"], p["log_step"], p["b_re"], p["b_im"],
                                              name=f"s5_discretize_{i}")
    p["a_r"], p["a_i"] = a_r.reshape(1, S5_LANES), a_i.reshape(1, S5_LANES)
    p["bdb"] = jnp.concatenate([_block_diag_b(bb_r), _block_diag_b(bb_i)], axis=2).astype(BF16)
    p["bcr"] = _block_diag_c(w["s5_c_re"][i]).astype(BF16)
    p["bci"] = _block_diag_c(w["s5_c_im"][i]).astype(BF16)
    p["dsk"] = w["s5_d"][i].reshape(1, D_MODEL)
    p["wglu"] = blk["s5_w_glu"].reshape(D_MODEL, D_MODEL)
    p["bglu"] = w["s5_b_glu"][i].reshape(1, D_MODEL)
    p["sn"] = w["s5_norm"][i].reshape(1, D_MODEL)
    p["conv_w"] = blk["ssd_conv_w"]
    p["conv_b"] = w["ssd_conv_b"][i].reshape(1, SSD_CONV_DIM)
    p["dtb"] = _pad_lanes(w["ssd_dt_bias"][i])
    p["alog"] = _pad_lanes(w["ssd_a_log"][i])
    p["dvec"] = _pad_lanes(w["ssd_d"][i])
    p["gn"] = w["ssd_norm"][i].reshape(1, D_MODEL)
    p["wo"] = blk["w_out"].reshape(2 * D_MODEL, D_MODEL)
    p["nf"] = w["norm_ffn"][i].reshape(1, D_MODEL)
    p["wg"], p["wup"], p["wdn"] = (blk[n].reshape(FFN_PAD, D_MODEL) for n in ("w_gate", "w_up", "w_down"))
    return p


def _layer_fwd(x0, p, i):
    u, z, xbc, dt = _inproj_fwd(x0, p["nm"], p["wu"], p["wz"], p["wx"], p["wd"], name=f"inproj_fwd_{i}")
    ya, xr, xi, v = _s5_fwd(u, p["a_r"], p["a_i"], p["bdb"], p["bcr"], p["bci"], p["dsk"], p["wglu"], p["bglu"], p["sn"],
                            name=f"s5_fwd_{i}")
    yb, stin = _ssd_fwd(xbc, z, dt, p["conv_w"], p["conv_b"], p["dtb"], p["alog"], p["dvec"], p["gn"], name=f"ssd_fwd_{i}")
    x1, x2 = _mix_ffn_fwd(x0, ya, yb, p["wo"], p["nf"], p["wg"], p["wup"], p["wdn"], name=f"mix_ffn_fwd_{i}")
    return x2, dict(x0=x0, u=u, z=z, xbc=xbc, dt=dt, xr=xr, xi=xi, v=v, stin=stin, ya=ya, yb=yb, x1=x1)


def _layer_bwd(dx2, s, p, i):
    g = {}
    dx1, dya, dyb, h2, act, dgt, dup, g_nf = _mix_ffn_bwd(s["x1"], dx2, p["wo"], p["nf"], p["wg"], p["wup"], p["wdn"],
                                                          name=f"mix_ffn_bwd_{i}")
    g["norm_ffn"] = g_nf.reshape(D_MODEL)
    g["w_down"] = _matmul_tn_lhs_blocks(act, dx2, FFN_BLOCK_PAD, FFN_BLOCK, name=f"grad_w_down_{i}")
    g["w_gate"] = _matmul_tn_lhs_blocks(dgt, h2, FFN_BLOCK_PAD, FFN_BLOCK, name=f"grad_w_gate_{i}")
    g["w_up"] = _matmul_tn_lhs_blocks(dup, h2, FFN_BLOCK_PAD, FFN_BLOCK, name=f"grad_w_up_{i}")
    g["w_out"] = _matmul_tn_pair(s["ya"], s["yb"], dx1, name=f"grad_w_out_{i}").reshape(N_DEV, 2 * D_MODEL // N_DEV, D_MODEL)

    (du, gg, dq, g_bdb, g_bcr, g_bci, g_ar8, g_ai8, g_d, g_bglu, g_sn) = _s5_bwd(
        dya, s["v"], s["u"], s["xr"], s["xi"], p["a_r"], p["a_i"], p["bdb"], p["bcr"], p["bci"], p["dsk"], p["wglu"],
        p["bglu"], p["sn"], name=f"s5_bwd_{i}")
    g["s5_w_glu"] = _matmul_tn(gg, dq, name=f"grad_w_glu_{i}").reshape(N_DEV, D_MODEL // N_DEV, D_MODEL)
    g["s5_d"], g["s5_b_glu"], g["s5_norm"] = g_d.reshape(D_MODEL), g_bglu.reshape(D_MODEL), g_sn.reshape(D_MODEL)
    g["s5_c_re"], g["s5_c_im"] = _block_diag_c_inv(g_bcr), _block_diag_c_inv(g_bci)
    sq = (SUBLANE, S5_GROUPS, S5_STATE)
    g_lr, g_li, g_ls, g_br, g_bi = _s5_discretize_bwd(
        p["lam_re"], p["lam_im"], p["log_step"], p["b_re"], p["b_im"], g_ar8.reshape(sq), g_ai8.reshape(sq),
        _block_diag_b_inv(g_bdb[:, :, :S5_SLICE_STATES]), _block_diag_b_inv(g_bdb[:, :, S5_SLICE_STATES:]),
        name=f"s5_discretize_bwd_{i}")
    g["s5_lam_re"], g["s5_lam_im"], g["s5_log_step"] = g_lr, g_li, g_ls.reshape(S5_GROUPS)
    b_shape = (S5_GROUPS, S5_STATE, S5_GROUP)
    g["s5_b_re"], g["s5_b_im"] = g_br.reshape(b_shape), g_bi.reshape(b_shape)

    dxbc, dz, ddt, g_cw, g_cb, g_dtb, g_alog, g_dvec, g_gn = _ssd_bwd(
        dyb, s["xbc"], s["z"], s["dt"], s["stin"], p["conv_w"], p["conv_b"], p["dtb"], p["alog"], p["dvec"], p["gn"],
        name=f"ssd_bwd_{i}")
    g["ssd_conv_w"] = jnp.moveaxis(g_cw.reshape(SSD_CONV, N_DEV, SSD_CONV_DIM // N_DEV), 1, 0)
    g["ssd_conv_b"] = g_cb.reshape(SSD_CONV_DIM)
    g["ssd_dt_bias"], g["ssd_a_log"], g["ssd_d"] = g_dtb[0, :SSD_HEADS], g_alog[0, :SSD_HEADS], g_dvec[0, :SSD_HEADS]
    g["ssd_norm"] = g_gn.reshape(D_MODEL)

    dx0, h, g_nm = _inproj_bwd(s["x0"], p["nm"], du, dz, dxbc, ddt, dx1, p["wu"], p["wz"], p["wx"], p["wd"],
                               name=f"inproj_bwd_{i}")
    g["norm_mix"] = g_nm.reshape(D_MODEL)
    g["w_in"] = _w_in_grad_blocks(
        _matmul_tn(h, du, name=f"grad_w_in_u_{i}"), _matmul_tn(h, dz, name=f"grad_w_in_z_{i}"),
        _matmul_tn(h, dxbc, name=f"grad_w_in_xbc_{i}"), _matmul_tn(h, ddt, name=f"grad_w_in_dt_{i}"),
        name=f"grad_w_in_blocks_{i}")
    return dx0, g


def _example_step(x, target, w, blks):
    prepared = [_prepare_layer(w, blks[i], i) for i in range(DEPTH)]
    saved = []
    h = x
    for i in range(DEPTH):
        h, s = _layer_fwd(h, prepared[i], i)
        saved.append(s)
    loss, dh, g_final = _loss_head(h, w["norm_final"].reshape(1, D_MODEL), target, name="loss_head")
    layer_grads = [None] * DEPTH
    for i in reversed(range(DEPTH)):
        dh, layer_grads[i] = _layer_bwd(dh, saved[i], prepared[i], i)
    return loss, dh, layer_grads, g_final.reshape(D_MODEL)


def _mesh_position():
    return lax.axis_index("x"), lax.axis_index("y"), lax.axis_index("c")


def _peer(pos, k):
    x, y, c = pos
    px = 1 - x if k & 4 else x
    py = 1 - y if k & 2 else y
    pc = 1 - c if k & 1 else c
    return (px, py, pc), 4 * px + 2 * py + pc


HBM = pl.BlockSpec(memory_space=pl.ANY)


def _run_copies(local, remote):
    for cp in local + remote:
        cp.start()
    for cp in remote:
        cp.wait_recv()
    for cp in remote:
        cp.wait_send()
    for cp in local:
        cp.wait()


def _comm_scratch(n_units):
    return [pltpu.SemaphoreType.DMA((n_units, N_DEV - 1)), pltpu.SemaphoreType.DMA((n_units, N_DEV - 1)),
            pltpu.SemaphoreType.DMA((n_units,))]


def _gather_blocks(arrays, layered, name):
    units, out_shapes = [], []
    for j, (a, lay) in enumerate(zip(arrays, layered)):
        for layer in (range(a.shape[0]) if lay else (None,)):
            units.append((j, layer, len(out_shapes)))
            out_shapes.append(_sds((N_DEV,) + (a.shape[1:] if lay else a.shape), a.dtype))
    n_in = len(arrays)
    other_chips = (4, 2, 6)

    def body(*refs):
        ins, outs = refs[:n_in], refs[n_in:n_in + len(out_shapes)]
        send_sems, recv_sems, local_sems = refs[n_in + len(out_shapes):]
        pos = _mesh_position()
        me = 4 * pos[0] + 2 * pos[1] + pos[2]
        sibling, _ = _peer(pos, 1)
        local, own, passed = [], [], []
        for u, (j, layer, o) in enumerate(units):
            src = ins[j] if layer is None else ins[j].at[layer]
            local.append(pltpu.make_async_copy(src, outs[o].at[me], local_sems.at[u]))

            def copy(sem, src_ref, slot, to, u=u, o=o):
                return pltpu.make_async_remote_copy(
                    src_ref=src_ref, dst_ref=outs[o].at[slot], send_sem=send_sems.at[u, sem], recv_sem=recv_sems.at[u, sem],
                    device_id=to, device_id_type=MESH_ID)

            own.append([copy(0, src, me, sibling)] + [copy(1 + i, src, me, _peer(pos, k)[0]) for i, k in enumerate(other_chips)])
            passed.append([copy(4 + i, outs[o].at[_peer(pos, k)[1]], _peer(pos, k)[1], sibling) for i, k in enumerate(other_chips)])
        for cp in local + [c for unit in own for c in unit]:
            cp.start()
        for u in range(len(units)):
            for i in range(len(other_chips)):
                own[u][1 + i].wait_recv()
                passed[u][i].start()
        for u in range(len(units)):
            own[u][0].wait_recv()
            for cp in passed[u]:
                cp.wait_recv()
        for cp in [c for unit in own + passed for c in unit]:
            cp.wait_send()
        for cp in local:
            cp.wait()

    outs = pl.pallas_call(body, name=name, in_specs=[HBM] * n_in, out_specs=[HBM] * len(out_shapes), out_shape=out_shapes,
                          scratch_shapes=_comm_scratch(len(units)))(*arrays)
    grouped = [[] for _ in arrays]
    for j, _, o in units:
        grouped[j].append(outs[o])
    return [tuple(g) for g in grouped]


def _exchange_blocks(entries, name):
    units, flat_in, out_shapes = [], [], []
    for j, entry in enumerate(entries):
        for layer, a in enumerate(entry):
            units.append((len(flat_in), layer, j))
            flat_in.append(a)
        out_shapes.append(_sds((N_DEV, len(entry)) + entry[0].shape[1:], entry[0].dtype))
    n_in = len(flat_in)

    def body(*refs):
        ins, outs = refs[:n_in], refs[n_in:n_in + len(out_shapes)]
        send_sems, recv_sems, local_sems = refs[n_in + len(out_shapes):]
        pos = _mesh_position()
        me = 4 * pos[0] + 2 * pos[1] + pos[2]
        local, remote = [], []
        for u, (i, layer, o) in enumerate(units):
            local.append(pltpu.make_async_copy(ins[i].at[me], outs[o].at[me, layer], local_sems.at[u]))
            for k in range(1, N_DEV):
                peer, peer_index = _peer(pos, k)
                remote.append(pltpu.make_async_remote_copy(
                    src_ref=ins[i].at[peer_index], dst_ref=outs[o].at[me, layer], send_sem=send_sems.at[u, k - 1],
                    recv_sem=recv_sems.at[u, k - 1], device_id=peer, device_id_type=MESH_ID))
        _run_copies(local, remote)

    return pl.pallas_call(body, name=name, in_specs=[HBM] * n_in, out_specs=[HBM] * len(out_shapes), out_shape=out_shapes,
                          scratch_shapes=_comm_scratch(len(units)))(*flat_in)


SUM_TILE = 512


def _adamw(w, g, m, v):
    m = ADAM_B1 * m + (1.0 - ADAM_B1) * g
    v = ADAM_B2 * v + (1.0 - ADAM_B2) * (g * g)
    m_hat = m / (1.0 - ADAM_B1 ** ADAM_STEP)
    v_hat = v / (1.0 - ADAM_B2 ** ADAM_STEP)
    return -ADAM_LR * (m_hat / (jnp.sqrt(v_hat) + ADAM_EPS) + ADAM_WD * w), m, v


def _sum_adamw(recv, w, m, v, name):
    n_lay, rows, cols = w.shape
    tr = _row_tile(rows, cap=256)

    def body(r_ref, w_ref, m_ref, v_ref, g_ref, d_ref, mo_ref, vo_ref):
        g = r_ref[0].astype(F32)
        for j in range(1, N_DEV):
            g = g + r_ref[j].astype(F32)
        g_ref[...] = g
        d_ref[...], mo_ref[...], vo_ref[...] = _adamw(w_ref[...], g, m_ref[...], v_ref[...])

    blk = pl.BlockSpec((None, tr, cols), lambda l, i: (l, i, 0))
    return pl.pallas_call(
        body, name=name, grid=(n_lay, rows // tr),
        in_specs=[pl.BlockSpec((N_DEV, None, tr, cols), lambda l, i: (0, l, i, 0)), blk, blk, blk],
        out_specs=[blk] * 4, out_shape=[_sds(w.shape)] * 4,
        compiler_params=pltpu.CompilerParams(dimension_semantics=("arbitrary", "arbitrary"), vmem_limit_bytes=VMEM_LIMIT),
    )(recv, w, m, v)


def _sum_senders(recv, name):
    _, n_lay, rows, cols = recv.shape
    tr = _row_tile(rows, cap=256)

    def body(r_ref, g_ref):
        g = r_ref[0].astype(F32)
        for j in range(1, N_DEV):
            g = g + r_ref[j].astype(F32)
        g_ref[...] = g

    return pl.pallas_call(
        body, name=name, grid=(n_lay, rows // tr),
        in_specs=[pl.BlockSpec((N_DEV, None, tr, cols), lambda l, i: (0, l, i, 0))],
        out_specs=pl.BlockSpec((None, tr, cols), lambda l, i: (l, i, 0)), out_shape=_sds((n_lay, rows, cols)),
        compiler_params=pltpu.CompilerParams(dimension_semantics=("arbitrary", "arbitrary"), vmem_limit_bytes=VMEM_LIMIT),
    )(recv)


def _adamw_blocks(g, w, m, v, name):
    n_lay, rows, cols = w.shape
    tr = _row_tile(rows, cap=256)

    def body(g_ref, w_ref, m_ref, v_ref, d_ref, mo_ref, vo_ref):
        d_ref[...], mo_ref[...], vo_ref[...] = _adamw(w_ref[...], g_ref[...], m_ref[...], v_ref[...])

    blk = pl.BlockSpec((None, tr, cols), lambda l, i: (l, i, 0))
    return pl.pallas_call(
        body, name=name, grid=(n_lay, rows // tr), in_specs=[blk] * 4, out_specs=[blk] * 3, out_shape=[_sds(w.shape)] * 3,
        compiler_params=pltpu.CompilerParams(dimension_semantics=("arbitrary", "arbitrary"), vmem_limit_bytes=VMEM_LIMIT),
    )(g, w, m, v)


def _sum_slots(recv, name):
    rows = recv.shape[1]

    def body(r_ref, g_ref):
        g = r_ref[0].astype(F32)
        for j in range(1, N_DEV):
            g = g + r_ref[j].astype(F32)
        g_ref[...] = g

    return _call(body, name, (1,), [_full(recv.shape)], [_full((rows, LANE))], [_sds((rows, LANE))])(recv)[0]


def _adamw_rows(g, w, m, v, name):
    rows = w.shape[0]
    tr = _row_tile(rows)

    def body(g_ref, w_ref, m_ref, v_ref, d_ref, mo_ref, vo_ref):
        d_ref[...], mo_ref[...], vo_ref[...] = _adamw(w_ref[...], g_ref[...], m_ref[...], v_ref[...])

    flat = _rows(tr, LANE)
    return _call(body, name, (rows // tr,), [flat] * 4, [flat] * 3, [_sds((rows, LANE))] * 3)(g, w, m, v)


def _row_tile(rows, cap=1024):
    if rows % SUBLANE:
        return rows
    best = SUBLANE
    for t in range(SUBLANE, cap + 1, SUBLANE):
        if rows % t == 0:
            best = t
    return best


BIG = (("w_in", (DEPTH, D_MODEL, IN_PROJ // N_DEV), 2),
       ("s5_w_glu", (DEPTH, D_MODEL // N_DEV, D_MODEL), 1),
       ("ssd_conv_w", (DEPTH, SSD_CONV, SSD_CONV_DIM // N_DEV), 2),
       ("w_out", (DEPTH, 2 * D_MODEL // N_DEV, D_MODEL), 1),
       ("w_gate", (DEPTH, D_MODEL, FFN_HIDDEN // N_DEV), 2),
       ("w_up", (DEPTH, D_MODEL, FFN_HIDDEN // N_DEV), 2),
       ("w_down", (DEPTH, FFN_HIDDEN // N_DEV, D_MODEL), 1))
SMALL = (("norm_mix", (DEPTH, D_MODEL)), ("s5_lam_re", (DEPTH, S5_GROUPS, S5_STATE)), ("s5_lam_im", (DEPTH, S5_GROUPS, S5_STATE)),
         ("s5_log_step", (DEPTH, S5_GROUPS)), ("s5_b_re", (DEPTH, S5_GROUPS, S5_STATE, S5_GROUP)),
         ("s5_b_im", (DEPTH, S5_GROUPS, S5_STATE, S5_GROUP)), ("s5_c_re", (DEPTH, S5_GROUPS, S5_GROUP, S5_STATE)),
         ("s5_c_im", (DEPTH, S5_GROUPS, S5_GROUP, S5_STATE)), ("s5_d", (DEPTH, D_MODEL)), ("s5_b_glu", (DEPTH, D_MODEL)),
         ("s5_norm", (DEPTH, D_MODEL)), ("ssd_conv_b", (DEPTH, SSD_CONV_DIM)), ("ssd_dt_bias", (DEPTH, SSD_HEADS)),
         ("ssd_a_log", (DEPTH, SSD_HEADS)), ("ssd_d", (DEPTH, SSD_HEADS)), ("ssd_norm", (DEPTH, D_MODEL)),
         ("norm_ffn", (DEPTH, D_MODEL)), ("norm_final", (D_MODEL,)))
WEIGHT_ORDER = ("norm_mix", "w_in", "s5_lam_re", "s5_lam_im", "s5_log_step", "s5_b_re", "s5_b_im", "s5_c_re", "s5_c_im", "s5_d",
                "s5_w_glu", "s5_b_glu", "s5_norm", "ssd_conv_w", "ssd_conv_b", "ssd_dt_bias", "ssd_a_log", "ssd_d", "ssd_norm",
                "w_out", "norm_ffn", "w_gate", "w_up", "w_down", "norm_final")


def _size(shape):
    n = 1
    for s in shape:
        n *= s
    return n


def _round_up(n, m):
    return -(-n // m) * m


SMALL_SIZE = sum(_size(s) for _, s in SMALL)
SMALL_ROWS = _round_up(-(-SMALL_SIZE // (N_DEV * LANE)), SUBLANE)


def _pack(parts, rows, dtype):
    flat = jnp.concatenate([p.reshape(-1).astype(dtype) for p in parts])
    return jnp.pad(flat, (0, rows * LANE - flat.shape[0])).reshape(rows, LANE)


def _unpack(flat, specs):
    out, off = {}, 0
    flat = flat.reshape(-1)
    for name, shape in specs:
        out[name] = flat[off:off + _size(shape)].reshape(shape)
        off += _size(shape)
    return out


def kernel(x, norm_mix, w_in, s5_lam_re, s5_lam_im, s5_log_step, s5_b_re, s5_b_im, s5_c_re, s5_c_im, s5_d, s5_w_glu, s5_b_glu, s5_norm, ssd_conv_w, ssd_conv_b, ssd_dt_bias, ssd_a_log, ssd_d, ssd_norm, w_out, norm_ffn, w_gate, w_up, w_down, norm_final, loss_target, m_norm_mix, m_w_in, m_s5_lam_re, m_s5_lam_im, m_s5_log_step, m_s5_b_re, m_s5_b_im, m_s5_c_re, m_s5_c_im, m_s5_d, m_s5_w_glu, m_s5_b_glu, m_s5_norm, m_ssd_conv_w, m_ssd_conv_b, m_ssd_dt_bias, m_ssd_a_log, m_ssd_d, m_ssd_norm, m_w_out, m_norm_ffn, m_w_gate, m_w_up, m_w_down, m_norm_final, v_norm_mix, v_w_in, v_s5_lam_re, v_s5_lam_im, v_s5_log_step, v_s5_b_re, v_s5_b_im, v_s5_c_re, v_s5_c_im, v_s5_d, v_s5_w_glu, v_s5_b_glu, v_s5_norm, v_ssd_conv_w, v_ssd_conv_b, v_ssd_dt_bias, v_ssd_a_log, v_ssd_d, v_ssd_norm, v_w_out, v_norm_ffn, v_w_gate, v_w_up, v_w_down, v_norm_final):
    given = dict(locals())
    w = {n: given[n] for n in WEIGHT_ORDER}
    m = {n: given["m_" + n] for n in WEIGHT_ORDER}
    v = {n: given["v_" + n] for n in WEIGHT_ORDER}
    big_names = tuple(n for n, _, _ in BIG)
    matmul_names = tuple(n for n in big_names if n != "ssd_conv_w")

    conv_hi = w["ssd_conv_w"].astype(BF16)
    conv_lo = (w["ssd_conv_w"] - conv_hi.astype(F32)).astype(BF16)
    row_pad = ((0, 0), (0, FFN_BLOCK_PAD - FFN_BLOCK), (0, 0))
    as_rows = {"w_gate": jnp.swapaxes(w["w_gate"], 1, 2), "w_up": jnp.swapaxes(w["w_up"], 1, 2), "w_down": w["w_down"]}
    to_send = [jnp.pad(as_rows[n].astype(BF16), row_pad) if n in as_rows else w[n].astype(BF16) for n in matmul_names]
    gathered = _gather_blocks(to_send + [jnp.stack([conv_hi, conv_lo])], [True] * len(matmul_names) + [False],
                              name="gather_weights")
    conv_pair = gathered[-1][0].astype(F32)
    conv_full = jnp.moveaxis(conv_pair[:, 0] + conv_pair[:, 1], 0, 2).reshape(DEPTH, SSD_CONV, SSD_CONV_DIM)
    blks = []
    for i in range(DEPTH):
        blk = {n: gathered[j][i] for j, n in enumerate(matmul_names)}
        blk["ssd_conv_w"] = conv_full[i]
        blks.append(blk)

    loss, grad_x, layer_grads, g_final = _example_step(x[0], loss_target[0], w, blks)

    small = jnp.concatenate([g_final if n == "norm_final" else jnp.stack([layer_grads[i][n] for i in range(DEPTH)]).reshape(-1)
                             for n, _ in SMALL])
    small_slots = jnp.pad(small, (0, N_DEV * SMALL_ROWS * LANE - small.shape[0])).reshape(N_DEV, SMALL_ROWS, LANE)
    received = _exchange_blocks([tuple(layer_grads[i][n] for i in range(DEPTH)) for n in big_names] + [(small_slots,)],
                                name="exchange_gradients")

    results = {}
    for n, recv in zip(big_names, received):
        if n in ("w_gate", "w_up"):
            g = jnp.swapaxes(_sum_senders(recv, name=f"sum_{n}"), 1, 2)
            results[n] = [g, *_adamw_blocks(g, w[n], m[n], v[n], name=f"adamw_{n}")]
        else:
            results[n] = _sum_adamw(recv, w[n], m[n], v[n], name=f"sum_adamw_{n}")
    g_part = _sum_slots(received[-1].reshape(N_DEV, SMALL_ROWS, LANE), name="sum_replicated")
    g_small = _gather_blocks([g_part], [False], name="gather_replicated")[0][0].reshape(N_DEV * SMALL_ROWS, LANE)
    small_rows = N_DEV * SMALL_ROWS
    d_small, m_small, v_small = _adamw_rows(
        g_small, _pack([w[n] for n, _ in SMALL], small_rows, F32), _pack([m[n] for n, _ in SMALL], small_rows, F32),
        _pack([v[n] for n, _ in SMALL], small_rows, F32), name="adamw_replicated")
    for k, packed in enumerate((g_small, d_small, m_small, v_small)):
        for n, arr in _unpack(packed, SMALL).items():
            results.setdefault(n, [None] * 4)[k] = arr

    outs = [results[n][k] for k in range(4) for n in WEIGHT_ORDER]
    total_loss = lax.psum(loss[0, 0], ("x", "y", "c"))
    return (total_loss, grad_x[None], *outs)
```

```python
import functools

import jax
import jax.numpy as jnp
from jax import lax
from jax.experimental import pallas as pl
from jax.experimental.pallas import tpu as pltpu

F32 = jnp.float32
BF16 = jnp.bfloat16
MESH_ID = pl.DeviceIdType.MESH

N_DEV = 8
DEPTH = 2
D_MODEL = 1024
S5_GROUPS = 64
S5_GROUP = 16
S5_STATE = 64
S5_LANES = S5_GROUPS * S5_STATE
SSD_HEADS = 16
SSD_HEAD_DIM = 64
SSD_STATE = 128
SSD_CHUNK = 128
SSD_CONV = 4
SSD_CONV_DIM = 1536
FFN_HIDDEN = 2816
IN_PROJ = 3600
EPS = 1e-6
LANE = 128
SUBLANE = 8
VMEM_LIMIT = 56 * 1024 * 1024

ADAM_LR = 0.001
ADAM_B1 = 0.9
ADAM_B2 = 0.999
ADAM_EPS = 1e-08
ADAM_WD = 0.01
ADAM_STEP = 10

TOK_TILE = 256
S5_TILE = 128
S5_SEG = S5_TILE // SUBLANE


def _sigmoid(x):
    return jax.nn.sigmoid(x)


def _silu(x):
    return x * _sigmoid(x)


def _gelu(x):
    return 0.5 * x * (1.0 + jnp.tanh(0.7978845608028654 * (x + 0.044715 * (x * x * x))))


def _softplus(x):
    return jnp.maximum(x, 0.0) + jnp.log(1.0 + jnp.exp(-jnp.abs(x)))


def _rms(x, g):
    r = lax.rsqrt(jnp.mean(x * x, axis=-1, keepdims=True) + EPS)
    return x * r * g


def _nn(a, b):
    return lax.dot_general(a.astype(BF16), b.astype(BF16), (((1,), (0,)), ((), ())), preferred_element_type=F32)


def _nt(a, b):
    return lax.dot_general(a.astype(BF16), b.astype(BF16), (((1,), (1,)), ((), ())), preferred_element_type=F32)


def _tn(a, b):
    return lax.dot_general(a.astype(BF16), b.astype(BF16), (((0,), (0,)), ((), ())), preferred_element_type=F32)


def _nn_f32(a, b):
    return lax.dot_general(a, b, (((1,), (0,)), ((), ())), precision=lax.Precision.HIGHEST, preferred_element_type=F32)


def _tn_f32(a, b):
    return lax.dot_general(a, b, (((0,), (0,)), ((), ())), precision=lax.Precision.HIGHEST, preferred_element_type=F32)


@jax.custom_vjp
def _nn_d(a, b):
    return _nn(a, b)


_nn_d.defvjp(lambda a, b: (_nn(a, b), (a, b)), lambda r, g: (_nt(g, r[1]), _tn(r[0], g)))


@jax.custom_vjp
def _nt_d(a, b):
    return _nt(a, b)


_nt_d.defvjp(lambda a, b: (_nt(a, b), (a, b)), lambda r, g: (_nn(g, r[1]), _tn(g, r[0])))


@jax.custom_vjp
def _tn_d(a, b):
    return _tn(a, b)


_tn_d.defvjp(lambda a, b: (_tn(a, b), (a, b)), lambda r, g: (_nt(r[1], g), _nn(r[0], g)))


@jax.custom_vjp
def _cumsum_rows(tri, x):
    return _nn_f32(tri, x)


_cumsum_rows.defvjp(lambda tri, x: (_nn_f32(tri, x), tri), lambda tri, g: (jnp.zeros_like(tri), _tn_f32(tri, g)))


def _full(shape):
    zeros = (0,) * len(shape)
    return pl.BlockSpec(shape, lambda *_: zeros)


def _const(shape):
    zeros = (0,) * len(shape)
    return pl.BlockSpec(shape, lambda *_: zeros, pipeline_mode=pl.Buffered(1))


def _rows(tile, width, n_tiles=None):
    if n_tiles is None:
        return pl.BlockSpec((tile, width), lambda i: (i, 0))
    return pl.BlockSpec((tile, width), lambda i: (n_tiles - 1 - i, 0))


def _call(body, name, grid, in_specs, out_specs, out_shape, scratch=()):
    return pl.pallas_call(
        body, name=name, grid=grid, in_specs=in_specs, out_specs=out_specs, out_shape=out_shape,
        scratch_shapes=list(scratch),
        compiler_params=pltpu.CompilerParams(dimension_semantics=("arbitrary",) * len(grid),
                                             vmem_limit_bytes=VMEM_LIMIT))


def _sds(shape, dtype=F32):
    return jax.ShapeDtypeStruct(shape, dtype)


def _tile_of(n, cap=512):
    if n <= LANE:
        return n
    best = LANE
    for t in range(LANE, cap + 1, LANE):
        if n % t == 0:
            best = t
    return best


def _inproj_fwd(x, nm, wu, wz, wx, wd, name):
    n_tok = x.shape[0]
    tm = TOK_TILE

    def body(x_ref, nm_ref, wu_ref, wz_ref, wx_ref, wd_ref, u_ref, z_ref, xbc_ref, dt_ref):
        h = _rms(x_ref[...], nm_ref[...]).astype(BF16)
        u_ref[...] = _nn(h, wu_ref[...])
        z_ref[...] = _nn(h, wz_ref[...])
        xbc_ref[...] = _nn(h, wx_ref[...])
        dt_ref[...] = _nn(h, wd_ref[...])

    return _call(
        body, name, (n_tok // tm,),
        [_rows(tm, D_MODEL), _const((1, D_MODEL)), _const(wu.shape), _const(wz.shape), _const(wx.shape), _const(wd.shape)],
        [_rows(tm, D_MODEL), _rows(tm, D_MODEL), _rows(tm, SSD_CONV_DIM), _rows(tm, LANE)],
        [_sds((n_tok, D_MODEL)), _sds((n_tok, D_MODEL)), _sds((n_tok, SSD_CONV_DIM)), _sds((n_tok, LANE))],
    )(x, nm, wu, wz, wx, wd)


def _inproj_bwd(x, nm, du, dz, dxbc, ddt, dres, wu, wz, wx, wd, name):
    n_tok = x.shape[0]
    tm = TOK_TILE

    def body(x_ref, nm_ref, du_ref, dz_ref, dxbc_ref, ddt_ref, dres_ref, wu_ref, wz_ref, wx_ref, wd_ref,
             dx_ref, h_ref, dnm_ref):
        dh = (_nt(du_ref[...], wu_ref[...]) + _nt(dz_ref[...], wz_ref[...])
              + _nt(dxbc_ref[...], wx_ref[...]) + _nt(ddt_ref[...], wd_ref[...]))
        h, vjp = jax.vjp(_rms, x_ref[...], nm_ref[...])
        dx, dnm = vjp(dh)
        dx_ref[...] = dres_ref[...] + dx
        h_ref[...] = h.astype(BF16)

        @pl.when(pl.program_id(0) == 0)
        def _():
            dnm_ref[...] = jnp.zeros_like(dnm_ref)

        dnm_ref[...] += dnm

    return _call(
        body, name, (n_tok // tm,),
        [_rows(tm, D_MODEL), _const((1, D_MODEL)), _rows(tm, D_MODEL), _rows(tm, D_MODEL), _rows(tm, SSD_CONV_DIM),
         _rows(tm, LANE), _rows(tm, D_MODEL), _const(wu.shape), _const(wz.shape), _const(wx.shape), _const(wd.shape)],
        [_rows(tm, D_MODEL), _rows(tm, D_MODEL), _full((1, D_MODEL))],
        [_sds((n_tok, D_MODEL)), _sds((n_tok, D_MODEL), BF16), _sds((1, D_MODEL))],
    )(x, nm, du, dz, dxbc, ddt, dres, wu, wz, wx, wd)


def _ffn_act(gt, up):
    return _silu(gt) * up


FFN_BLOCK = FFN_HIDDEN // N_DEV
FFN_BLOCK_PAD = -(-FFN_BLOCK // LANE) * LANE


FFN_PAD = N_DEV * FFN_BLOCK_PAD


def _mix_ffn_fwd(x0, ya, yb, wo, nf, wg, wu, wd, name):
    n_tok = x0.shape[0]
    tm = TOK_TILE

    def body(x0_ref, ya_ref, yb_ref, wo_ref, nf_ref, wg_ref, wu_ref, wd_ref, x1_ref, x2_ref):
        x1 = x0_ref[...] + _nn(ya_ref[...], wo_ref[:D_MODEL, :]) + _nn(yb_ref[...], wo_ref[D_MODEL:, :])
        h = _rms(x1, nf_ref[...]).astype(BF16)
        x1_ref[...] = x1
        x2_ref[...] = x1 + _nn(_ffn_act(_nt(h, wg_ref[...]), _nt(h, wu_ref[...])), wd_ref[...])

    return _call(
        body, name, (n_tok // tm,),
        [_rows(tm, D_MODEL), _rows(tm, D_MODEL), _rows(tm, D_MODEL), _const(wo.shape),
         _const((1, D_MODEL)), _const(wg.shape), _const(wu.shape), _const(wd.shape)],
        [_rows(tm, D_MODEL), _rows(tm, D_MODEL)],
        [_sds((n_tok, D_MODEL)), _sds((n_tok, D_MODEL))],
    )(x0, ya, yb, wo, nf, wg, wu, wd)


def _mix_ffn_bwd(x1, dx2, wo, nf, wg, wu, wd, name):
    n_tok = x1.shape[0]
    tm = TOK_TILE
    n_chunks = 3
    hc = FFN_PAD // n_chunks

    def body(x1_ref, dx2_ref, wo_ref, nf_ref, wg_ref, wu_ref, wd_ref,
             dx1_ref, dya_ref, dyb_ref, h_ref, a_ref, dgt_ref, dup_ref, dnf_ref):
        dx2 = dx2_ref[...]
        dx2b = dx2.astype(BF16)
        h, rms_vjp = jax.vjp(_rms, x1_ref[...], nf_ref[...])
        hb = h.astype(BF16)
        dh = jnp.zeros_like(h)
        for c in range(n_chunks):
            rows = pl.ds(c * hc, hc)
            a, act_vjp = jax.vjp(_ffn_act, _nt(hb, wg_ref[rows, :]), _nt(hb, wu_ref[rows, :]))
            dgt, dup = act_vjp(_nt(dx2b, wd_ref[rows, :]))
            a_ref[:, c * hc:(c + 1) * hc] = a.astype(BF16)
            dgt_ref[:, c * hc:(c + 1) * hc] = dgt.astype(BF16)
            dup_ref[:, c * hc:(c + 1) * hc] = dup.astype(BF16)
            dh = dh + _nn(dgt, wg_ref[rows, :]) + _nn(dup, wu_ref[rows, :])
        dx, dnf = rms_vjp(dh)
        dx1 = dx2 + dx
        dx1_ref[...] = dx1
        dya_ref[...] = _nt(dx1, wo_ref[:D_MODEL, :])
        dyb_ref[...] = _nt(dx1, wo_ref[D_MODEL:, :])
        h_ref[...] = hb

        @pl.when(pl.program_id(0) == 0)
        def _():
            dnf_ref[...] = jnp.zeros_like(dnf_ref)

        dnf_ref[...] += dnf

    hidden = _rows(tm, FFN_PAD)
    return _call(
        body, name, (n_tok // tm,),
        [_rows(tm, D_MODEL), _rows(tm, D_MODEL), _const(wo.shape), _const((1, D_MODEL)),
         _const(wg.shape), _const(wu.shape), _const(wd.shape)],
        [_rows(tm, D_MODEL), _rows(tm, D_MODEL), _rows(tm, D_MODEL), _rows(tm, D_MODEL), hidden, hidden, hidden,
         _full((1, D_MODEL))],
        [_sds((n_tok, D_MODEL)), _sds((n_tok, D_MODEL)), _sds((n_tok, D_MODEL)), _sds((n_tok, D_MODEL), BF16),
         _sds((n_tok, FFN_PAD), BF16), _sds((n_tok, FFN_PAD), BF16), _sds((n_tok, FFN_PAD), BF16), _sds((1, D_MODEL))],
    )(x1, dx2, wo, nf, wg, wu, wd)


def _loss_head(x, nf, target, name):
    n_tok = x.shape[0]
    tm = TOK_TILE

    def loss_of(xv, g, t):
        e = _rms(xv, g) - t
        return 0.5 * jnp.sum(jnp.sum(e * e, axis=-1, keepdims=True) * (1.0 / D_MODEL), axis=0, keepdims=True)

    def body(x_ref, nf_ref, t_ref, loss_ref, dx_ref, dnf_ref):
        loss, vjp = jax.vjp(functools.partial(loss_of, t=t_ref[...]), x_ref[...], nf_ref[...])
        dx, dnf = vjp(jnp.ones_like(loss))
        dx_ref[...] = dx

        @pl.when(pl.program_id(0) == 0)
        def _():
            dnf_ref[...] = jnp.zeros_like(dnf_ref)
            loss_ref[...] = jnp.zeros_like(loss_ref)

        dnf_ref[...] += dnf
        loss_ref[...] += jnp.broadcast_to(loss, loss_ref.shape)

    return _call(
        body, name, (n_tok // tm,),
        [_rows(tm, D_MODEL), _const((1, D_MODEL)), _rows(tm, D_MODEL)],
        [_full((SUBLANE, LANE)), _rows(tm, D_MODEL), _full((1, D_MODEL))],
        [_sds((SUBLANE, LANE)), _sds((n_tok, D_MODEL)), _sds((1, D_MODEL))],
    )(x, nf, target)


GRAD_WIRE = BF16


def _matmul_tn(a, b, name):
    n_tok, k1 = a.shape
    k2 = b.shape[1]
    t1, t2 = _tile_of(k1), _tile_of(k2)

    def body(a_ref, b_ref, o_ref):
        o_ref[...] = _tn(a_ref[...], b_ref[...]).astype(GRAD_WIRE)

    return pl.pallas_call(
        body, name=name, grid=(k1 // t1, k2 // t2),
        in_specs=[pl.BlockSpec((n_tok, t1), lambda i, j: (0, i)), pl.BlockSpec((n_tok, t2), lambda i, j: (0, j))],
        out_specs=pl.BlockSpec((t1, t2), lambda i, j: (i, j)),
        out_shape=_sds((k1, k2), GRAD_WIRE),
        compiler_params=pltpu.CompilerParams(dimension_semantics=("arbitrary", "arbitrary"), vmem_limit_bytes=VMEM_LIMIT),
    )(a, b)


def _tn_params():
    return pltpu.CompilerParams(dimension_semantics=("arbitrary", "arbitrary"), vmem_limit_bytes=VMEM_LIMIT)


def _matmul_tn_lhs_blocks(a, b, width, keep, name):
    n_tok, k1 = a.shape
    k2 = b.shape[1]
    t2 = _tile_of(k2)

    def body(a_ref, b_ref, o_ref):
        o_ref[...] = _tn(a_ref[...], b_ref[...])[:keep, :].astype(GRAD_WIRE)

    return pl.pallas_call(
        body, name=name, grid=(k1 // width, k2 // t2),
        in_specs=[pl.BlockSpec((n_tok, width), lambda d, j: (0, d)), pl.BlockSpec((n_tok, t2), lambda d, j: (0, j))],
        out_specs=pl.BlockSpec((None, keep, t2), lambda d, j: (d, 0, j)),
        out_shape=_sds((k1 // width, keep, k2), GRAD_WIRE), compiler_params=_tn_params(),
    )(a, b)


def _matmul_tn_pair(a0, a1, b, name):
    n_tok, k1 = a0.shape
    k2 = b.shape[1]
    t1, t2 = _tile_of(k1), _tile_of(k2)

    def body(a0_ref, a1_ref, b_ref, o_ref):
        @pl.when(pl.program_id(0) == 0)
        def _():
            o_ref[...] = _tn(a0_ref[...], b_ref[...]).astype(GRAD_WIRE)

        @pl.when(pl.program_id(0) == 1)
        def _():
            o_ref[...] = _tn(a1_ref[...], b_ref[...]).astype(GRAD_WIRE)

    lhs = pl.BlockSpec((n_tok, t1), lambda s, i, j: (0, i))
    return pl.pallas_call(
        body, name=name, grid=(2, k1 // t1, k2 // t2),
        in_specs=[lhs, lhs, pl.BlockSpec((n_tok, t2), lambda s, i, j: (0, j))],
        out_specs=pl.BlockSpec((None, t1, t2), lambda s, i, j: (s, i, j)),
        out_shape=_sds((2, k1, k2), GRAD_WIRE),
        compiler_params=pltpu.CompilerParams(dimension_semantics=("arbitrary",) * 3, vmem_limit_bytes=VMEM_LIMIT),
    )(a0, a1, b)


W_IN_BLOCK = IN_PROJ // N_DEV
W_IN_SPLITS = (D_MODEL, 2 * D_MODEL, 2 * D_MODEL + SSD_CONV_DIM)
RELAYOUT_TILE = 256


def _w_in_split(blocks, name):
    tr = RELAYOUT_TILE

    def body(b_ref, wu_ref, wz_ref, wx_ref, wd_ref):
        full = jnp.concatenate([b_ref[d] for d in range(N_DEV)], axis=1)
        wu_ref[...] = full[:, :W_IN_SPLITS[0]]
        wz_ref[...] = full[:, W_IN_SPLITS[0]:W_IN_SPLITS[1]]
        wx_ref[...] = full[:, W_IN_SPLITS[1]:W_IN_SPLITS[2]]
        wd_ref[...] = jnp.concatenate([full[:, W_IN_SPLITS[2]:], jnp.zeros((tr, LANE - SSD_HEADS), full.dtype)], axis=1)

    return _call(
        body, name, (D_MODEL // tr,), [pl.BlockSpec((N_DEV, tr, W_IN_BLOCK), lambda i: (0, i, 0))],
        [_rows(tr, D_MODEL), _rows(tr, D_MODEL), _rows(tr, SSD_CONV_DIM), _rows(tr, LANE)],
        [_sds((D_MODEL, D_MODEL), BF16), _sds((D_MODEL, D_MODEL), BF16), _sds((D_MODEL, SSD_CONV_DIM), BF16),
         _sds((D_MODEL, LANE), BF16)],
    )(blocks)


def _w_in_grad_blocks(gu, gz, gx, gdt, name):
    tr = RELAYOUT_TILE

    def body(gu_ref, gz_ref, gx_ref, gdt_ref, o_ref):
        full = jnp.concatenate([gu_ref[...], gz_ref[...], gx_ref[...], gdt_ref[...]], axis=1)
        for d in range(N_DEV):
            o_ref[d] = full[:, d * W_IN_BLOCK:(d + 1) * W_IN_BLOCK]

    return _call(
        body, name, (D_MODEL // tr,),
        [_rows(tr, D_MODEL), _rows(tr, D_MODEL), _rows(tr, SSD_CONV_DIM), _rows(tr, LANE)],
        [pl.BlockSpec((N_DEV, tr, W_IN_BLOCK), lambda i: (0, i, 0))], [_sds((N_DEV, D_MODEL, W_IN_BLOCK), gu.dtype)],
    )(gu, gz, gx, gdt)[0]


S5_SLICES = D_MODEL // LANE
S5_SLICE_STATES = S5_LANES // S5_SLICES
SCAN_LANES = 512


def _s5_scan(br_ref, bi_ref, a_r, a_i, car_r, car_i, ini_r, ini_i, reverse, xr_ref=None, xi_ref=None,
             acc_r=None, acc_i=None):
    n_rows = br_ref.shape[1]
    seg = n_rows // SUBLANE
    order = range(SUBLANE - 1, -1, -1) if reverse else range(SUBLANE)

    def rows(t):
        return pl.ds((seg - 1 - t) if reverse else t, SUBLANE, stride=seg)

    tiles_per = SCAN_LANES // LANE

    def load(ref, t, lb):
        return jnp.concatenate([ref[lb * tiles_per + j, rows(t), :] for j in range(tiles_per)], axis=1)

    def store(ref, t, lb, val):
        for j in range(tiles_per):
            ref[lb * tiles_per + j, rows(t), :] = val[:, j * LANE:(j + 1) * LANE]

    for lb in range(S5_LANES // SCAN_LANES):
        lanes = pl.ds(lb * SCAN_LANES, SCAN_LANES)
        ar1, ai1 = a_r[:, lb * SCAN_LANES:(lb + 1) * SCAN_LANES], a_i[:, lb * SCAN_LANES:(lb + 1) * SCAN_LANES]
        ar8 = jnp.broadcast_to(ar1, (SUBLANE, SCAN_LANES))
        ai8 = jnp.broadcast_to(ai1, (SUBLANE, SCAN_LANES))

        def local(t, c):
            sr, si = c
            return (ar8 * sr - ai8 * si + load(br_ref, t, lb), ar8 * si + ai8 * sr + load(bi_ref, t, lb))

        zero = jnp.zeros((SUBLANE, SCAN_LANES), F32)
        er, ei = lax.fori_loop(0, seg, local, (zero, zero))
        pr, pi = ar1, ai1
        for _ in range(seg.bit_length() - 1):
            pr, pi = pr * pr - pi * pi, 2.0 * pr * pi
        cr, ci = car_r[:, lanes], car_i[:, lanes]
        for s in order:
            ini_r[s:s + 1, lanes] = cr
            ini_i[s:s + 1, lanes] = ci
            cr, ci = pr * cr - pi * ci + er[s:s + 1, :], pr * ci + pi * cr + ei[s:s + 1, :]
        car_r[:, lanes] = cr
        car_i[:, lanes] = ci

        if xr_ref is None:
            def final(t, c):
                sr, si = c
                nr = ar8 * sr - ai8 * si + load(br_ref, t, lb)
                ni = ar8 * si + ai8 * sr + load(bi_ref, t, lb)
                store(br_ref, t, lb, nr)
                store(bi_ref, t, lb, ni)
                return nr, ni

            lax.fori_loop(0, seg, final, (ini_r[:, lanes], ini_i[:, lanes]))
        else:
            def final_acc(t, c):
                sr, si, gr, gi = c
                xr, xi = load(xr_ref, t, lb), load(xi_ref, t, lb)
                gr = gr + sr * xr + si * xi
                gi = gi + si * xr - sr * xi
                nr = ar8 * sr - ai8 * si + load(br_ref, t, lb)
                ni = ar8 * si + ai8 * sr + load(bi_ref, t, lb)
                store(br_ref, t, lb, nr)
                store(bi_ref, t, lb, ni)
                return nr, ni, gr, gi

            _, _, gr, gi = lax.fori_loop(0, seg, final_acc,
                                         (ini_r[:, lanes], ini_i[:, lanes], acc_r[:, lanes], acc_i[:, lanes]))
            acc_r[:, lanes] = gr
            acc_i[:, lanes] = gi


def _s5_tail(gg, q, sn):
    return _rms(gg * _sigmoid(q), sn)


S5_STATE_TILES = S5_LANES // LANE
TILES_PER_SLICE = S5_SLICE_STATES // LANE


def _put_states(ref, k, val):
    for j in range(TILES_PER_SLICE):
        ref[k * TILES_PER_SLICE + j] = val[:, j * LANE:(j + 1) * LANE]


def _get_states(ref, k):
    return jnp.concatenate([ref[k * TILES_PER_SLICE + j] for j in range(TILES_PER_SLICE)], axis=1)


def _state_rows(tile, n_tiles=None):
    if n_tiles is None:
        return pl.BlockSpec((S5_STATE_TILES, tile, LANE), lambda i: (0, i, 0))
    return pl.BlockSpec((S5_STATE_TILES, tile, LANE), lambda i: (0, n_tiles - 1 - i, 0))


def _s5_fwd(u, a_r, a_i, bdb, bcr, bci, dsk, wglu, bglu, sn, name):
    n_tok = u.shape[0]
    tc = S5_TILE
    sw = S5_SLICE_STATES

    def body(u_ref, ar_ref, ai_ref, bdb_ref, bcr_ref, bci_ref, d_ref, wg_ref, bg_ref, sn_ref,
             ya_ref, xr_ref, xi_ref, v_ref, car_r, car_i, ini_r, ini_i):
        @pl.when(pl.program_id(0) == 0)
        def _():
            car_r[...] = jnp.zeros_like(car_r)
            car_i[...] = jnp.zeros_like(car_i)

        u_t = u_ref[...]
        ub = u_t.astype(BF16)
        for k in range(S5_SLICES):
            bu = _nn(ub[:, k * LANE:(k + 1) * LANE], bdb_ref[k])
            _put_states(xr_ref, k, bu[:, :sw])
            _put_states(xi_ref, k, bu[:, sw:])
        _s5_scan(xr_ref, xi_ref, ar_ref[...], ai_ref[...], car_r, car_i, ini_r, ini_i, reverse=False)
        vs = [_nn(_get_states(xr_ref, k), bcr_ref[k]) - _nn(_get_states(xi_ref, k), bci_ref[k])
              for k in range(S5_SLICES)]
        v = jnp.concatenate(vs, axis=1) + d_ref[...] * u_t
        v_ref[...] = v
        gg = _gelu(v)
        ya_ref[...] = _s5_tail(gg, _nn(gg, wg_ref[...]) + bg_ref[...], sn_ref[...])

    return _call(
        body, name, (n_tok // tc,),
        [_rows(tc, D_MODEL), _const((1, S5_LANES)), _const((1, S5_LANES)), _const(bdb.shape), _const(bcr.shape),
         _const(bci.shape), _const((1, D_MODEL)), _const(wglu.shape), _const((1, D_MODEL)), _const((1, D_MODEL))],
        [_rows(tc, D_MODEL), _state_rows(tc), _state_rows(tc), _rows(tc, D_MODEL)],
        [_sds((n_tok, D_MODEL)), _sds((S5_STATE_TILES, n_tok, LANE)), _sds((S5_STATE_TILES, n_tok, LANE)),
         _sds((n_tok, D_MODEL))],
        scratch=[pltpu.VMEM((1, S5_LANES), F32), pltpu.VMEM((1, S5_LANES), F32),
                 pltpu.VMEM((SUBLANE, S5_LANES), F32), pltpu.VMEM((SUBLANE, S5_LANES), F32)],
    )(u, a_r, a_i, bdb, bcr, bci, dsk, wglu, bglu, sn)


def _s5_bwd(dya, v, u, xr, xi, a_r, a_i, bdb, bcr, bci, dsk, wglu, bglu, sn, name):
    n_tok = u.shape[0]
    tc = S5_TILE
    nt = n_tok // tc
    sw = S5_SLICE_STATES

    def body(dya_ref, v_ref, u_ref, xr_ref, xi_ref, ar_ref, ai_ref, bdb_ref, bcr_ref, bci_ref, d_ref, wg_ref, bg_ref, sn_ref,
             du_ref, gg_ref, dq_ref, gbdb_ref, gbcr_ref, gbci_ref, gar_ref, gai_ref, gd_ref, gbg_ref, gsn_ref,
             gr_ref, gi_ref, car_r, car_i, ini_r, ini_i):
        @pl.when(pl.program_id(0) == 0)
        def _():
            for r in (car_r, car_i, gbdb_ref, gbcr_ref, gbci_ref, gar_ref, gai_ref, gd_ref, gbg_ref, gsn_ref):
                r[...] = jnp.zeros_like(r)

        u_t = u_ref[...]
        gg, gelu_vjp = jax.vjp(_gelu, v_ref[...])
        _, tail_vjp = jax.vjp(_s5_tail, gg, _nn(gg, wg_ref[...]) + bg_ref[...], sn_ref[...])
        dgg, dq, dsn = tail_vjp(dya_ref[...])
        (dv,) = gelu_vjp(dgg + _nt(dq, wg_ref[...]))
        gg_ref[...] = gg.astype(BF16)
        dq_ref[...] = dq.astype(BF16)
        gd_ref[...] += jnp.sum(dv * u_t, axis=0, keepdims=True)
        gbg_ref[...] += jnp.sum(dq, axis=0, keepdims=True)
        gsn_ref[...] += dsn
        dvb = dv.astype(BF16)
        for k in range(S5_SLICES):
            dvk = dvb[:, k * LANE:(k + 1) * LANE]
            _put_states(gr_ref, k, _nt(dvk, bcr_ref[k]))
            _put_states(gi_ref, k, -_nt(dvk, bci_ref[k]))
            gbcr_ref[k] += _tn(_get_states(xr_ref, k), dvk)
            gbci_ref[k] -= _tn(_get_states(xi_ref, k), dvk)
        _s5_scan(gr_ref, gi_ref, ar_ref[...], -ai_ref[...], car_r, car_i, ini_r, ini_i, reverse=True,
                 xr_ref=xr_ref, xi_ref=xi_ref, acc_r=gar_ref, acc_i=gai_ref)
        ub = u_t.astype(BF16)
        dus = []
        for k in range(S5_SLICES):
            gk_r, gk_i = _get_states(gr_ref, k).astype(BF16), _get_states(gi_ref, k).astype(BF16)
            bk = bdb_ref[k]
            dus.append(_nt(gk_r, bk[:, :sw]) + _nt(gk_i, bk[:, sw:]))
            uk = ub[:, k * LANE:(k + 1) * LANE]
            gbdb_ref[k, :, :sw] += _tn(uk, gk_r)
            gbdb_ref[k, :, sw:] += _tn(uk, gk_i)
        du_ref[...] = jnp.concatenate(dus, axis=1) + d_ref[...] * dv

    rev = functools.partial(_rows, n_tiles=nt)
    return _call(
        body, name, (nt,),
        [rev(tc, D_MODEL), rev(tc, D_MODEL), rev(tc, D_MODEL), _state_rows(tc, nt), _state_rows(tc, nt),
         _const((1, S5_LANES)), _const((1, S5_LANES)), _const(bdb.shape), _const(bcr.shape), _const(bci.shape),
         _const((1, D_MODEL)), _const(wglu.shape), _const((1, D_MODEL)), _const((1, D_MODEL))],
        [rev(tc, D_MODEL), rev(tc, D_MODEL), rev(tc, D_MODEL), _full(bdb.shape), _full(bcr.shape), _full(bci.shape),
         _full((SUBLANE, S5_LANES)), _full((SUBLANE, S5_LANES)), _full((1, D_MODEL)), _full((1, D_MODEL)), _full((1, D_MODEL))],
        [_sds((n_tok, D_MODEL)), _sds((n_tok, D_MODEL), BF16), _sds((n_tok, D_MODEL), BF16), _sds(bdb.shape), _sds(bcr.shape),
         _sds(bci.shape), _sds((SUBLANE, S5_LANES)), _sds((SUBLANE, S5_LANES)), _sds((1, D_MODEL)), _sds((1, D_MODEL)),
         _sds((1, D_MODEL))],
        scratch=[pltpu.VMEM((S5_STATE_TILES, tc, LANE), F32), pltpu.VMEM((S5_STATE_TILES, tc, LANE), F32),
                 pltpu.VMEM((1, S5_LANES), F32), pltpu.VMEM((1, S5_LANES), F32),
                 pltpu.VMEM((SUBLANE, S5_LANES), F32), pltpu.VMEM((SUBLANE, S5_LANES), F32)],
    )(dya, v, u, xr, xi, a_r, a_i, bdb, bcr, bci, dsk, wglu, bglu, sn)


SSD_WIDTH = SSD_HEADS * SSD_HEAD_DIM
SSD_GROUPS = 2
HEADS_PER_GROUP = SSD_HEADS // SSD_GROUPS


def _take(x, axis, start, size):
    n = x.shape[axis]

    def sl(v):
        return lax.slice_in_dim(v, start, start + size, axis=axis)

    @jax.custom_vjp
    def f(v):
        return sl(v)

    def bwd(_, g):
        parts = []
        if start:
            parts.append(jnp.zeros(g.shape[:axis] + (start,) + g.shape[axis + 1:], g.dtype))
        parts.append(g)
        if n - start - size:
            parts.append(jnp.zeros(g.shape[:axis] + (n - start - size,) + g.shape[axis + 1:], g.dtype))
        return (jnp.concatenate(parts, axis=axis) if len(parts) > 1 else g,)

    f.defvjp(lambda v: (sl(v), None), bwd)
    return f(x)


def _lane_of(x, h):
    col = lax.broadcasted_iota(jnp.int32, x.shape, 1)
    return jnp.sum(jnp.where(col == h, x, 0.0), axis=1, keepdims=True)


def _ssd_chunk(xc, z, dt, dtb, alog, dvec, gn, st, nn, nt, tn, cumsum, take):
    t_len = xc.shape[0]
    xa = _silu(xc)
    dtp = _softplus(dt + dtb)
    d_a = dtp * (-jnp.exp(alog))
    row = lax.broadcasted_iota(jnp.int32, (t_len, t_len), 0)
    col = lax.broadcasted_iota(jnp.int32, (t_len, t_len), 1)
    causal = row >= col
    cum = cumsum(causal.astype(F32), d_a)
    eye = (row == col).astype(F32)
    ys, sts = [], []
    for g in range(SSD_GROUPS):
        bg = take(xa, 1, SSD_WIDTH + g * SSD_STATE, SSD_STATE)
        cg = take(xa, 1, SSD_WIDTH + (SSD_GROUPS + g) * SSD_STATE, SSD_STATE)
        cb = nt(cg, bg)
        for r in range(HEADS_PER_GROUP):
            h = g * HEADS_PER_GROUP + r
            cc = _lane_of(cum, h)
            cr = jnp.sum(cc * eye, axis=0, keepdims=True)
            decay = jnp.exp(jnp.where(causal, cc - cr, -1e30))
            xh = take(xa, 1, h * SSD_HEAD_DIM, SSD_HEAD_DIM)
            xdt = xh * _lane_of(dtp, h)
            sth = take(st, 0, h * SSD_HEAD_DIM, SSD_HEAD_DIM)
            c_last = jnp.sum(jnp.where(row[:, :1] == t_len - 1, cc, 0.0), axis=0, keepdims=True)
            y = nn(cb * decay, xdt) + jnp.exp(cc) * nt(cg, sth) + _lane_of(dvec, h) * xh
            ys.append(y)
            sts.append(jnp.exp(c_last) * sth + tn(xdt * jnp.exp(c_last - cc), bg))
    y = jnp.concatenate(ys, axis=1) * _silu(z)
    return _rms(y, gn), jnp.concatenate(sts, axis=0)


def _shift_back(cur, prev, j):
    if j == 0:
        return cur
    row = lax.broadcasted_iota(jnp.int32, cur.shape, 0)
    return jnp.where(row < j, pltpu.roll(prev, j, 0), pltpu.roll(cur, j, 0))


def _shift_ahead(cur, nxt, j):
    if j == 0:
        return cur
    n = cur.shape[0]
    row = lax.broadcasted_iota(jnp.int32, cur.shape, 0)
    return jnp.where(row >= n - j, pltpu.roll(nxt, n - j, 0), pltpu.roll(cur, n - j, 0))


def _conv(cur, prev, w, b):
    out = b + w[SSD_CONV - 1:SSD_CONV, :] * cur
    for k in range(SSD_CONV - 1):
        out = out + w[k:k + 1, :] * _shift_back(cur, prev, SSD_CONV - 1 - k)
    return out


def _ssd_fwd(xbc, z, dt, conv_w, conv_b, dtb, alog, dvec, gn, name):
    n_tok = xbc.shape[0]
    tc = SSD_CHUNK
    nc = n_tok // tc
    st_rows = SSD_HEADS * SSD_HEAD_DIM

    def body(cur_ref, prev_ref, z_ref, dt_ref, w_ref, b_ref, dtb_ref, alog_ref, dvec_ref, gn_ref,
             yb_ref, stin_ref, st_ref):
        i = pl.program_id(0)

        @pl.when(i == 0)
        def _():
            st_ref[...] = jnp.zeros_like(st_ref)

        prev = jnp.where(i > 0, prev_ref[...], 0.0)
        xc = _conv(cur_ref[...], prev, w_ref[...], b_ref[...])
        st = st_ref[...]
        stin_ref[0] = st
        yb, st_new = _ssd_chunk(xc, z_ref[...], dt_ref[...], dtb_ref[...], alog_ref[...], dvec_ref[...], gn_ref[...], st,
                                _nn, _nt, _tn, _nn_f32, lambda v, axis, start, size: lax.slice_in_dim(v, start, start + size, axis=axis))
        yb_ref[...] = yb
        st_ref[...] = st_new

    return _call(
        body, name, (nc,),
        [_rows(tc, SSD_CONV_DIM), pl.BlockSpec((tc, SSD_CONV_DIM), lambda i: (jnp.maximum(i - 1, 0), 0)),
         _rows(tc, D_MODEL), _rows(tc, LANE), _const((SSD_CONV, SSD_CONV_DIM)), _const((1, SSD_CONV_DIM)),
         _const((1, LANE)), _const((1, LANE)), _const((1, LANE)), _const((1, D_MODEL))],
        [_rows(tc, D_MODEL), pl.BlockSpec((1, st_rows, SSD_STATE), lambda i: (i, 0, 0))],
        [_sds((n_tok, D_MODEL)), _sds((nc, st_rows, SSD_STATE))],
        scratch=[pltpu.VMEM((st_rows, SSD_STATE), F32)],
    )(xbc, xbc, z, dt, conv_w, conv_b, dtb, alog, dvec, gn)


def _ssd_bwd(dyb, xbc, z, dt, stin, conv_w, conv_b, dtb, alog, dvec, gn, name):
    n_tok = xbc.shape[0]
    tc = SSD_CHUNK
    nc = n_tok // tc
    st_rows = SSD_HEADS * SSD_HEAD_DIM

    def body(dyb_ref, cur_ref, prev_ref, z_ref, dt_ref, stin_ref, w_ref, b_ref, dtb_ref, alog_ref, dvec_ref, gn_ref,
             dxbc_ref, dz_ref, ddt_ref, gw_ref, gb_ref, gdtb_ref, galog_ref, gdvec_ref, ggn_ref,
             dst_ref, dxc_next_ref):
        i = pl.program_id(0)

        @pl.when(i == 0)
        def _():
            for r in (dst_ref, dxc_next_ref, gw_ref, gb_ref, gdtb_ref, galog_ref, gdvec_ref, ggn_ref):
                r[...] = jnp.zeros_like(r)

        cur = cur_ref[...]
        prev = jnp.where(i < nc - 1, prev_ref[...], 0.0)
        w = w_ref[...]
        xc = _conv(cur, prev, w, b_ref[...])
        chunk = functools.partial(_ssd_chunk, nn=_nn_d, nt=_nt_d, tn=_tn_d, cumsum=_cumsum_rows, take=_take)
        _, vjp = jax.vjp(chunk, xc, z_ref[...], dt_ref[...], dtb_ref[...], alog_ref[...], dvec_ref[...], gn_ref[...],
                         stin_ref[0])
        dxc, dz, ddt, gdtb, galog, gdvec, ggn, dst = vjp((dyb_ref[...], dst_ref[...]))
        dst_ref[...] = dst
        dz_ref[...] = dz
        ddt_ref[...] = ddt
        gdtb_ref[...] += gdtb
        galog_ref[...] += galog
        gdvec_ref[...] += gdvec
        ggn_ref[...] += ggn
        dxc_next = dxc_next_ref[...]
        dxbc = w[SSD_CONV - 1:SSD_CONV, :] * dxc
        gws = []
        for k in range(SSD_CONV - 1):
            j = SSD_CONV - 1 - k
            dxbc = dxbc + w[k:k + 1, :] * _shift_ahead(dxc, dxc_next, j)
            gws.append(jnp.sum(dxc * _shift_back(cur, prev, j), axis=0, keepdims=True))
        gws.append(jnp.sum(dxc * cur, axis=0, keepdims=True))
        dxbc_ref[...] = dxbc
        gw_ref[...] += jnp.concatenate(gws, axis=0)
        gb_ref[...] += jnp.sum(dxc, axis=0, keepdims=True)
        dxc_next_ref[...] = dxc

    rev = functools.partial(_rows, n_tiles=nc)
    return _call(
        body, name, (nc,),
        [rev(tc, D_MODEL), rev(tc, SSD_CONV_DIM),
         pl.BlockSpec((tc, SSD_CONV_DIM), lambda i: (jnp.maximum(nc - 2 - i, 0), 0)),
         rev(tc, D_MODEL), rev(tc, LANE), pl.BlockSpec((1, st_rows, SSD_STATE), lambda i: (nc - 1 - i, 0, 0)),
         _const((SSD_CONV, SSD_CONV_DIM)), _const((1, SSD_CONV_DIM)), _const((1, LANE)), _const((1, LANE)),
         _const((1, LANE)), _const((1, D_MODEL))],
        [rev(tc, SSD_CONV_DIM), rev(tc, D_MODEL), rev(tc, LANE), _full((SSD_CONV, SSD_CONV_DIM)), _full((1, SSD_CONV_DIM)),
         _full((1, LANE)), _full((1, LANE)), _full((1, LANE)), _full((1, D_MODEL))],
        [_sds((n_tok, SSD_CONV_DIM)), _sds((n_tok, D_MODEL)), _sds((n_tok, LANE)), _sds((SSD_CONV, SSD_CONV_DIM)),
         _sds((1, SSD_CONV_DIM)), _sds((1, LANE)), _sds((1, LANE)), _sds((1, LANE)), _sds((1, D_MODEL))],
        scratch=[pltpu.VMEM((st_rows, SSD_STATE), F32), pltpu.VMEM((tc, SSD_CONV_DIM), F32)],
    )(dyb, xbc, xbc, z, dt, stin, conv_w, conv_b, dtb, alog, dvec, gn)


@jax.custom_vjp
def _expand_cols(x, e):
    return _nn_f32(x, e)


_expand_cols.defvjp(
    lambda x, e: (_nn_f32(x, e), e),
    lambda e, g: (lax.dot_general(g, e, (((1,), (1,)), ((), ())), precision=lax.Precision.HIGHEST,
                                  preferred_element_type=F32), jnp.zeros_like(e)))


def _s5_discretize(lam_re, lam_im, log_step, b_re, b_im, expand):
    step = jnp.exp(log_step)
    mag = jnp.exp(lam_re * step)
    ang = lam_im * step
    a_r = mag * jnp.cos(ang)
    a_i = mag * jnp.sin(ang)
    den = lam_re * lam_re + lam_im * lam_im
    n_r = a_r - 1.0
    coef_r = _expand_cols((n_r * lam_re + a_i * lam_im) / den, expand)
    coef_i = _expand_cols((a_i * lam_re - n_r * lam_im) / den, expand)
    return a_r, a_i, coef_r * b_re - coef_i * b_im, coef_r * b_im + coef_i * b_re


def _expand_matrix():
    p = lax.broadcasted_iota(jnp.int32, (S5_STATE, S5_STATE * S5_GROUP), 0)
    c = lax.broadcasted_iota(jnp.int32, (S5_STATE, S5_STATE * S5_GROUP), 1)
    return (c // S5_GROUP == p).astype(F32)


def _s5_discretize_fwd(lam_re, lam_im, log_step, b_re, b_im, name):
    def body(lr_ref, li_ref, ls_ref, br_ref, bi_ref, ar_ref, ai_ref, bbr_ref, bbi_ref):
        outs = _s5_discretize(lr_ref[...], li_ref[...], ls_ref[...], br_ref[...], bi_ref[...], _expand_matrix())
        for r, o in zip((ar_ref, ai_ref, bbr_ref, bbi_ref), outs):
            r[...] = o

    sq, wide = (S5_GROUPS, S5_STATE), (S5_GROUPS, S5_STATE * S5_GROUP)
    return _call(body, name, (1,), [_full(sq), _full(sq), _full((S5_GROUPS, 1)), _full(wide), _full(wide)],
                 [_full(sq), _full(sq), _full(wide), _full(wide)], [_sds(sq), _sds(sq), _sds(wide), _sds(wide)],
                 )(lam_re, lam_im, log_step, b_re, b_im)


def _s5_discretize_bwd(lam_re, lam_im, log_step, b_re, b_im, g_ar8, g_ai8, g_bbr, g_bbi, name):
    def body(lr_ref, li_ref, ls_ref, br_ref, bi_ref, gar_ref, gai_ref, gbbr_ref, gbbi_ref,
             glr_ref, gli_ref, gls_ref, gbr_ref, gbi_ref):
        _, vjp = jax.vjp(functools.partial(_s5_discretize, expand=_expand_matrix()),
                         lr_ref[...], li_ref[...], ls_ref[...], br_ref[...], bi_ref[...])
        grads = vjp((jnp.sum(gar_ref[...], axis=0), jnp.sum(gai_ref[...], axis=0), gbbr_ref[...], gbbi_ref[...]))
        for r, g in zip((glr_ref, gli_ref, gls_ref, gbr_ref, gbi_ref), grads):
            r[...] = g

    sq, wide, col = (S5_GROUPS, S5_STATE), (S5_GROUPS, S5_STATE * S5_GROUP), (S5_GROUPS, 1)
    part = (SUBLANE,) + sq
    return _call(body, name, (1,),
                 [_full(sq), _full(sq), _full(col), _full(wide), _full(wide), _full(part), _full(part), _full(wide), _full(wide)],
                 [_full(sq), _full(sq), _full(col), _full(wide), _full(wide)],
                 [_sds(sq), _sds(sq), _sds(col), _sds(wide), _sds(wide)],
                 )(lam_re, lam_im, log_step, b_re, b_im, g_ar8, g_ai8, g_bbr, g_bbi)


GROUPS_PER_SLICE = LANE // S5_GROUP


def _block_diag_b(bb):
    t = bb.reshape(S5_SLICES, GROUPS_PER_SLICE, S5_STATE, S5_GROUP)
    eye = jnp.eye(GROUPS_PER_SLICE, dtype=bb.dtype)
    return jnp.einsum("kgph,gf->kghfp", t, eye).reshape(S5_SLICES, LANE, S5_SLICE_STATES)


def _block_diag_b_inv(m):
    t = m.reshape(S5_SLICES, GROUPS_PER_SLICE, S5_GROUP, GROUPS_PER_SLICE, S5_STATE)
    return jnp.einsum("kghgp->kgph", t).reshape(S5_GROUPS, S5_STATE * S5_GROUP)


def _block_diag_c(c):
    t = c.reshape(S5_SLICES, GROUPS_PER_SLICE, S5_GROUP, S5_STATE)
    eye = jnp.eye(GROUPS_PER_SLICE, dtype=c.dtype)
    return jnp.einsum("kghp,gf->kgpfh", t, eye).reshape(S5_SLICES, S5_SLICE_STATES, LANE)


def _block_diag_c_inv(m):
    t = m.reshape(S5_SLICES, GROUPS_PER_SLICE, S5_STATE, GROUPS_PER_SLICE, S5_GROUP)
    return jnp.einsum("kgpgh->kghp", t).reshape(S5_GROUPS, S5_GROUP, S5_STATE)


def _pad_lanes(v):
    return jnp.pad(v.reshape(1, -1), ((0, 0), (0, LANE - v.shape[0])))


def _prepare_layer(w, blk, i):
    p = {}
    p["wu"], p["wz"], p["wx"], p["wd"] = _w_in_split(blk["w_in"], name=f"w_in_split_{i}")
    p["nm"] = w["norm_mix"][i].reshape(1, D_MODEL)
    p["lam_re"], p["lam_im"] = w["s5_lam_re"][i], w["s5_lam_im"][i]
    p["log_step"] = w["s5_log_step"][i].reshape(S5_GROUPS, 1)
    p["b_re"] = w["s5_b_re"][i].reshape(S5_GROUPS, S5_STATE * S5_GROUP)
    p["b_im"] = w["s5_b_im"][i].reshape(S5_GROUPS, S5_STATE * S5_GROUP)
    a_r, a_i, bb_r, bb_i = _s5_discretize_fwd(p["lam_re"], p["lam_im"], p["log_step"], p["b_re"], p["b_im"],
                                              name=f"s5_discretize_{i}")
    p["a_r"], p["a_i"] = a_r.reshape(1, S5_LANES), a_i.reshape(1, S5_LANES)
    p["bdb"] = jnp.concatenate([_block_diag_b(bb_r), _block_diag_b(bb_i)], axis=2).astype(BF16)
    p["bcr"] = _block_diag_c(w["s5_c_re"][i]).astype(BF16)
    p["bci"] = _block_diag_c(w["s5_c_im"][i]).astype(BF16)
    p["dsk"] = w["s5_d"][i].reshape(1, D_MODEL)
    p["wglu"] = blk["s5_w_glu"].reshape(D_MODEL, D_MODEL)
    p["bglu"] = w["s5_b_glu"][i].reshape(1, D_MODEL)
    p["sn"] = w["s5_norm"][i].reshape(1, D_MODEL)
    p["conv_w"] = blk["ssd_conv_w"]
    p["conv_b"] = w["ssd_conv_b"][i].reshape(1, SSD_CONV_DIM)
    p["dtb"] = _pad_lanes(w["ssd_dt_bias"][i])
    p["alog"] = _pad_lanes(w["ssd_a_log"][i])
    p["dvec"] = _pad_lanes(w["ssd_d"][i])
    p["gn"] = w["ssd_norm"][i].reshape(1, D_MODEL)
    p["wo"] = blk["w_out"].reshape(2 * D_MODEL, D_MODEL)
    p["nf"] = w["norm_ffn"][i].reshape(1, D_MODEL)
    p["wg"], p["wup"], p["wdn"] = (blk[n].reshape(FFN_PAD, D_MODEL) for n in ("w_gate", "w_up", "w_down"))
    return p


def _layer_fwd(x0, p, i):
    u, z, xbc, dt = _inproj_fwd(x0, p["nm"], p["wu"], p["wz"], p["wx"], p["wd"], name=f"inproj_fwd_{i}")
    ya, xr, xi, v = _s5_fwd(u, p["a_r"], p["a_i"], p["bdb"], p["bcr"], p["bci"], p["dsk"], p["wglu"], p["bglu"], p["sn"],
                            name=f"s5_fwd_{i}")
    yb, stin = _ssd_fwd(xbc, z, dt, p["conv_w"], p["conv_b"], p["dtb"], p["alog"], p["dvec"], p["gn"], name=f"ssd_fwd_{i}")
    x1, x2 = _mix_ffn_fwd(x0, ya, yb, p["wo"], p["nf"], p["wg"], p["wup"], p["wdn"], name=f"mix_ffn_fwd_{i}")
    return x2, dict(x0=x0, u=u, z=z, xbc=xbc, dt=dt, xr=xr, xi=xi, v=v, stin=stin, ya=ya, yb=yb, x1=x1)


def _layer_bwd(dx2, s, p, i):
    g = {}
    dx1, dya, dyb, h2, act, dgt, dup, g_nf = _mix_ffn_bwd(s["x1"], dx2, p["wo"], p["nf"], p["wg"], p["wup"], p["wdn"],
                                                          name=f"mix_ffn_bwd_{i}")
    g["norm_ffn"] = g_nf.reshape(D_MODEL)
    g["w_down"] = _matmul_tn_lhs_blocks(act, dx2, FFN_BLOCK_PAD, FFN_BLOCK, name=f"grad_w_down_{i}")
    g["w_gate"] = _matmul_tn_lhs_blocks(dgt, h2, FFN_BLOCK_PAD, FFN_BLOCK, name=f"grad_w_gate_{i}")
    g["w_up"] = _matmul_tn_lhs_blocks(dup, h2, FFN_BLOCK_PAD, FFN_BLOCK, name=f"grad_w_up_{i}")
    g["w_out"] = _matmul_tn_pair(s["ya"], s["yb"], dx1, name=f"grad_w_out_{i}").reshape(N_DEV, 2 * D_MODEL // N_DEV, D_MODEL)

    (du, gg, dq, g_bdb, g_bcr, g_bci, g_ar8, g_ai8, g_d, g_bglu, g_sn) = _s5_bwd(
        dya, s["v"], s["u"], s["xr"], s["xi"], p["a_r"], p["a_i"], p["bdb"], p["bcr"], p["bci"], p["dsk"], p["wglu"],
        p["bglu"], p["sn"], name=f"s5_bwd_{i}")
    g["s5_w_glu"] = _matmul_tn(gg, dq, name=f"grad_w_glu_{i}").reshape(N_DEV, D_MODEL // N_DEV, D_MODEL)
    g["s5_d"], g["s5_b_glu"], g["s5_norm"] = g_d.reshape(D_MODEL), g_bglu.reshape(D_MODEL), g_sn.reshape(D_MODEL)
    g["s5_c_re"], g["s5_c_im"] = _block_diag_c_inv(g_bcr), _block_diag_c_inv(g_bci)
    sq = (SUBLANE, S5_GROUPS, S5_STATE)
    g_lr, g_li, g_ls, g_br, g_bi = _s5_discretize_bwd(
        p["lam_re"], p["lam_im"], p["log_step"], p["b_re"], p["b_im"], g_ar8.reshape(sq), g_ai8.reshape(sq),
        _block_diag_b_inv(g_bdb[:, :, :S5_SLICE_STATES]), _block_diag_b_inv(g_bdb[:, :, S5_SLICE_STATES:]),
        name=f"s5_discretize_bwd_{i}")
    g["s5_lam_re"], g["s5_lam_im"], g["s5_log_step"] = g_lr, g_li, g_ls.reshape(S5_GROUPS)
    b_shape = (S5_GROUPS, S5_STATE, S5_GROUP)
    g["s5_b_re"], g["s5_b_im"] = g_br.reshape(b_shape), g_bi.reshape(b_shape)

    dxbc, dz, ddt, g_cw, g_cb, g_dtb, g_alog, g_dvec, g_gn = _ssd_bwd(
        dyb, s["xbc"], s["z"], s["dt"], s["stin"], p["conv_w"], p["conv_b"], p["dtb"], p["alog"], p["dvec"], p["gn"],
        name=f"ssd_bwd_{i}")
    g["ssd_conv_w"] = jnp.moveaxis(g_cw.reshape(SSD_CONV, N_DEV, SSD_CONV_DIM // N_DEV), 1, 0)
    g["ssd_conv_b"] = g_cb.reshape(SSD_CONV_DIM)
    g["ssd_dt_bias"], g["ssd_a_log"], g["ssd_d"] = g_dtb[0, :SSD_HEADS], g_alog[0, :SSD_HEADS], g_dvec[0, :SSD_HEADS]
    g["ssd_norm"] = g_gn.reshape(D_MODEL)

    dx0, h, g_nm = _inproj_bwd(s["x0"], p["nm"], du, dz, dxbc, ddt, dx1, p["wu"], p["wz"], p["wx"], p["wd"],
                               name=f"inproj_bwd_{i}")
    g["norm_mix"] = g_nm.reshape(D_MODEL)
    g["w_in"] = _w_in_grad_blocks(
        _matmul_tn(h, du, name=f"grad_w_in_u_{i}"), _matmul_tn(h, dz, name=f"grad_w_in_z_{i}"),
        _matmul_tn(h, dxbc, name=f"grad_w_in_xbc_{i}"), _matmul_tn(h, ddt, name=f"grad_w_in_dt_{i}"),
        name=f"grad_w_in_blocks_{i}")
    return dx0, g


def _example_step(x, target, w, blks):
    prepared = [_prepare_layer(w, blks[i], i) for i in range(DEPTH)]
    saved = []
    h = x
    for i in range(DEPTH):
        h, s = _layer_fwd(h, prepared[i], i)
        saved.append(s)
    loss, dh, g_final = _loss_head(h, w["norm_final"].reshape(1, D_MODEL), target, name="loss_head")
    layer_grads = [None] * DEPTH
    for i in reversed(range(DEPTH)):
        dh, layer_grads[i] = _layer_bwd(dh, saved[i], prepared[i], i)
    return loss, dh, layer_grads, g_final.reshape(D_MODEL)


def _mesh_position():
    return lax.axis_index("x"), lax.axis_index("y"), lax.axis_index("c")


def _peer(pos, k):
    x, y, c = pos
    px = 1 - x if k & 4 else x
    py = 1 - y if k & 2 else y
    pc = 1 - c if k & 1 else c
    return (px, py, pc), 4 * px + 2 * py + pc


HBM = pl.BlockSpec(memory_space=pl.ANY)


def _run_copies(local, remote):
    for cp in local + remote:
        cp.start()
    for cp in remote:
        cp.wait_recv()
    for cp in remote:
        cp.wait_send()
    for cp in local:
        cp.wait()


def _comm_scratch(n_units):
    return [pltpu.SemaphoreType.DMA((n_units, N_DEV - 1)), pltpu.SemaphoreType.DMA((n_units, N_DEV - 1)),
            pltpu.SemaphoreType.DMA((n_units,))]


def _gather_blocks(arrays, layered, name):
    units, out_shapes = [], []
    for j, (a, lay) in enumerate(zip(arrays, layered)):
        for layer in (range(a.shape[0]) if lay else (None,)):
            units.append((j, layer, len(out_shapes)))
            out_shapes.append(_sds((N_DEV,) + (a.shape[1:] if lay else a.shape), a.dtype))
    n_in = len(arrays)
    other_chips = (4, 2, 6)

    def body(*refs):
        ins, outs = refs[:n_in], refs[n_in:n_in + len(out_shapes)]
        send_sems, recv_sems, local_sems = refs[n_in + len(out_shapes):]
        pos = _mesh_position()
        me = 4 * pos[0] + 2 * pos[1] + pos[2]
        sibling, _ = _peer(pos, 1)
        local, own, passed = [], [], []
        for u, (j, layer, o) in enumerate(units):
            src = ins[j] if layer is None else ins[j].at[layer]
            local.append(pltpu.make_async_copy(src, outs[o].at[me], local_sems.at[u]))

            def copy(sem, src_ref, slot, to, u=u, o=o):
                return pltpu.make_async_remote_copy(
                    src_ref=src_ref, dst_ref=outs[o].at[slot], send_sem=send_sems.at[u, sem], recv_sem=recv_sems.at[u, sem],
                    device_id=to, device_id_type=MESH_ID)

            own.append([copy(0, src, me, sibling)] + [copy(1 + i, src, me, _peer(pos, k)[0]) for i, k in enumerate(other_chips)])
            passed.append([copy(4 + i, outs[o].at[_peer(pos, k)[1]], _peer(pos, k)[1], sibling) for i, k in enumerate(other_chips)])
        for cp in local + [c for unit in own for c in unit]:
            cp.start()
        for u in range(len(units)):
            for i in range(len(other_chips)):
                own[u][1 + i].wait_recv()
                passed[u][i].start()
        for u in range(len(units)):
            own[u][0].wait_recv()
            for cp in passed[u]:
                cp.wait_recv()
        for cp in [c for unit in own + passed for c in unit]:
            cp.wait_send()
        for cp in local:
            cp.wait()

    outs = pl.pallas_call(body, name=name, in_specs=[HBM] * n_in, out_specs=[HBM] * len(out_shapes), out_shape=out_shapes,
                          scratch_shapes=_comm_scratch(len(units)))(*arrays)
    grouped = [[] for _ in arrays]
    for j, _, o in units:
        grouped[j].append(outs[o])
    return [tuple(g) for g in grouped]


def _exchange_blocks(entries, name):
    units, flat_in, out_shapes = [], [], []
    for j, entry in enumerate(entries):
        for layer, a in enumerate(entry):
            units.append((len(flat_in), layer, j))
            flat_in.append(a)
        out_shapes.append(_sds((N_DEV, len(entry)) + entry[0].shape[1:], entry[0].dtype))
    n_in = len(flat_in)

    def body(*refs):
        ins, outs = refs[:n_in], refs[n_in:n_in + len(out_shapes)]
        send_sems, recv_sems, local_sems = refs[n_in + len(out_shapes):]
        pos = _mesh_position()
        me = 4 * pos[0] + 2 * pos[1] + pos[2]
        local, remote = [], []
        for u, (i, layer, o) in enumerate(units):
            local.append(pltpu.make_async_copy(ins[i].at[me], outs[o].at[me, layer], local_sems.at[u]))
            for k in range(1, N_DEV):
                peer, peer_index = _peer(pos, k)
                remote.append(pltpu.make_async_remote_copy(
                    src_ref=ins[i].at[peer_index], dst_ref=outs[o].at[me, layer], send_sem=send_sems.at[u, k - 1],
                    recv_sem=recv_sems.at[u, k - 1], device_id=peer, device_id_type=MESH_ID))
        _run_copies(local, remote)

    return pl.pallas_call(body, name=name, in_specs=[HBM] * n_in, out_specs=[HBM] * len(out_shapes), out_shape=out_shapes,
                          scratch_shapes=_comm_scratch(len(units)))(*flat_in)


SEM = pl.BlockSpec(memory_space=pltpu.SEMAPHORE)
SIDE_EFFECT = pltpu.SideEffectType.DATAFLOW_SIDE_EFFECTING


def _own_slots(arrays, indexed, name):
    n = len(arrays)
    shapes = [_sds(a.shape if indexed else (N_DEV,) + a.shape, a.dtype) for a in arrays]

    def body(*refs):
        ins, outs, sems = refs[:n], refs[n:2 * n], refs[2 * n]
        pos = _mesh_position()
        me = 4 * pos[0] + 2 * pos[1] + pos[2]
        copies = [pltpu.make_async_copy(ins[u].at[me] if indexed else ins[u], outs[u].at[me], sems.at[u]) for u in range(n)]
        for cp in copies:
            cp.start()
        for cp in copies:
            cp.wait()

    return pl.pallas_call(body, name=name, in_specs=[HBM] * n, out_specs=[HBM] * n, out_shape=shapes,
                          scratch_shapes=[pltpu.SemaphoreType.DMA((n,))])(*arrays)


def _split_copies(srcs, lands, send_sems, recv_sems, indexed):
    pos = _mesh_position()
    me = 4 * pos[0] + 2 * pos[1] + pos[2]
    copies = []
    for u, (src, land) in enumerate(zip(srcs, lands)):
        for k in range(1, N_DEV):
            peer, peer_index = _peer(pos, k)
            copies.append(pltpu.make_async_remote_copy(
                src_ref=src.at[peer_index] if indexed else src, dst_ref=land.at[me],
                send_sem=send_sems.at[u * (N_DEV - 1) + k - 1], recv_sem=recv_sems.at[u * (N_DEV - 1) + k - 1],
                device_id=peer, device_id_type=MESH_ID))
    return copies


def _exchange_start(arrays, lands, indexed, name):
    n = len(arrays)

    def body(*refs):
        srcs, zones = refs[:n], refs[n:2 * n]
        send_sems, recv_sems = refs[2 * n], refs[2 * n + 1]
        token = refs[-1]
        for cp in _split_copies(srcs, zones, send_sems, recv_sems, indexed):
            cp.start()
        token[...] = jnp.zeros_like(token)

    sem_shape = pltpu.SemaphoreType.DMA((n * (N_DEV - 1),))
    outs = pl.pallas_call(
        body, name=name, in_specs=[HBM] * (2 * n),
        out_specs=[SEM, SEM] + [HBM] * (2 * n) + [pl.BlockSpec(memory_space=pltpu.VMEM)],
        out_shape=[sem_shape, sem_shape] + [pltpu.HBM(a.shape, a.dtype) for a in list(arrays) + list(lands)]
        + [_sds((SUBLANE, LANE))],
        input_output_aliases={i: 2 + i for i in range(2 * n)},
        compiler_params=pltpu.CompilerParams(has_side_effects=SIDE_EFFECT),
    )(*[pltpu.with_memory_space_constraint(a, pltpu.HBM) for a in list(arrays) + list(lands)])
    return outs[0], outs[1], outs[2:2 + n], outs[2 + n:2 + 2 * n], outs[-1]


def _exchange_wait(send_sems, recv_sems, arrays, lands, after, indexed, name):
    n = len(arrays)

    def body(*refs):
        srcs, zones = refs[:n], refs[n:2 * n]
        s_sems, r_sems = refs[2 * n], refs[2 * n + 1]
        for cp in _split_copies(srcs, zones, s_sems, r_sems, indexed):
            cp.wait_send()
            cp.wait_recv()

    outs = pl.pallas_call(
        body, name=name, in_specs=[HBM] * (2 * n) + [SEM, SEM, HBM],
        out_specs=[HBM] * (2 * n), out_shape=[pltpu.HBM(a.shape, a.dtype) for a in list(arrays) + list(lands)],
        input_output_aliases={i: i for i in range(2 * n)},
        compiler_params=pltpu.CompilerParams(has_side_effects=SIDE_EFFECT),
    )(*arrays, *lands, send_sems, recv_sems, after)
    return outs[n:]


SUM_TILE = 512


def _adamw(w, g, m, v):
    m = ADAM_B1 * m + (1.0 - ADAM_B1) * g
    v = ADAM_B2 * v + (1.0 - ADAM_B2) * (g * g)
    m_hat = m / (1.0 - ADAM_B1 ** ADAM_STEP)
    v_hat = v / (1.0 - ADAM_B2 ** ADAM_STEP)
    return -ADAM_LR * (m_hat / (jnp.sqrt(v_hat) + ADAM_EPS) + ADAM_WD * w), m, v


def _sum_adamw(recv, w, m, v, layer, others, name):
    _, rows, cols = w.shape
    tr = _row_tile(rows, cap=256)

    def body(r_ref, w_ref, m_ref, v_ref, *rest):
        g_ref, d_ref, mo_ref, vo_ref = rest[-4:]
        g = r_ref[0].astype(F32)
        for j in range(1, N_DEV):
            g = g + r_ref[j].astype(F32)
        g_ref[...] = g
        d_ref[...], mo_ref[...], vo_ref[...] = _adamw(w_ref[...], g, m_ref[...], v_ref[...])

    blk = pl.BlockSpec((None, tr, cols), lambda i: (layer, i, 0))
    carried = list(others) if others is not None else []
    return pl.pallas_call(
        body, name=name, grid=(rows // tr,),
        in_specs=[pl.BlockSpec((N_DEV, tr, cols), lambda i: (0, i, 0)), blk, blk, blk] + [HBM] * len(carried),
        out_specs=[blk] * 4, out_shape=[_sds(w.shape)] * 4,
        input_output_aliases={4 + k: k for k in range(len(carried))},
        compiler_params=pltpu.CompilerParams(dimension_semantics=("arbitrary",), vmem_limit_bytes=VMEM_LIMIT),
    )(recv, w, m, v, *carried)


def _sum_senders(recv, name):
    _, rows, cols = recv.shape
    tr = _row_tile(rows, cap=256)

    def body(r_ref, g_ref):
        g = r_ref[0].astype(F32)
        for j in range(1, N_DEV):
            g = g + r_ref[j].astype(F32)
        g_ref[...] = g

    return _call(body, name, (rows // tr,), [pl.BlockSpec((N_DEV, tr, cols), lambda i: (0, i, 0))], [_rows(tr, cols)],
                 [_sds((rows, cols))])(recv)[0]


def _adamw_blocks(g, w, m, v, name):
    n_lay, rows, cols = w.shape
    tr = _row_tile(rows, cap=256)

    def body(g_ref, w_ref, m_ref, v_ref, d_ref, mo_ref, vo_ref):
        d_ref[...], mo_ref[...], vo_ref[...] = _adamw(w_ref[...], g_ref[...], m_ref[...], v_ref[...])

    blk = pl.BlockSpec((None, tr, cols), lambda l, i: (l, i, 0))
    return pl.pallas_call(
        body, name=name, grid=(n_lay, rows // tr), in_specs=[blk] * 4, out_specs=[blk] * 3, out_shape=[_sds(w.shape)] * 3,
        compiler_params=pltpu.CompilerParams(dimension_semantics=("arbitrary", "arbitrary"), vmem_limit_bytes=VMEM_LIMIT),
    )(g, w, m, v)


def _sum_slots(recv, name):
    rows = recv.shape[1]

    def body(r_ref, g_ref):
        g = r_ref[0].astype(F32)
        for j in range(1, N_DEV):
            g = g + r_ref[j].astype(F32)
        g_ref[...] = g

    return _call(body, name, (1,), [_full(recv.shape)], [_full((rows, LANE))], [_sds((rows, LANE))])(recv)[0]


def _adamw_rows(g, w, m, v, name):
    rows = w.shape[0]
    tr = _row_tile(rows)

    def body(g_ref, w_ref, m_ref, v_ref, d_ref, mo_ref, vo_ref):
        d_ref[...], mo_ref[...], vo_ref[...] = _adamw(w_ref[...], g_ref[...], m_ref[...], v_ref[...])

    flat = _rows(tr, LANE)
    return _call(body, name, (rows // tr,), [flat] * 4, [flat] * 3, [_sds((rows, LANE))] * 3)(g, w, m, v)


def _row_tile(rows, cap=1024):
    if rows % SUBLANE:
        return rows
    best = SUBLANE
    for t in range(SUBLANE, cap + 1, SUBLANE):
        if rows % t == 0:
            best = t
    return best


BIG = (("w_in", (DEPTH, D_MODEL, IN_PROJ // N_DEV), 2),
       ("s5_w_glu", (DEPTH, D_MODEL // N_DEV, D_MODEL), 1),
       ("ssd_conv_w", (DEPTH, SSD_CONV, SSD_CONV_DIM // N_DEV), 2),
       ("w_out", (DEPTH, 2 * D_MODEL // N_DEV, D_MODEL), 1),
       ("w_gate", (DEPTH, D_MODEL, FFN_HIDDEN // N_DEV), 2),
       ("w_up", (DEPTH, D_MODEL, FFN_HIDDEN // N_DEV), 2),
       ("w_down", (DEPTH, FFN_HIDDEN // N_DEV, D_MODEL), 1))
SMALL = (("norm_mix", (DEPTH, D_MODEL)), ("s5_lam_re", (DEPTH, S5_GROUPS, S5_STATE)), ("s5_lam_im", (DEPTH, S5_GROUPS, S5_STATE)),
         ("s5_log_step", (DEPTH, S5_GROUPS)), ("s5_b_re", (DEPTH, S5_GROUPS, S5_STATE, S5_GROUP)),
         ("s5_b_im", (DEPTH, S5_GROUPS, S5_STATE, S5_GROUP)), ("s5_c_re", (DEPTH, S5_GROUPS, S5_GROUP, S5_STATE)),
         ("s5_c_im", (DEPTH, S5_GROUPS, S5_GROUP, S5_STATE)), ("s5_d", (DEPTH, D_MODEL)), ("s5_b_glu", (DEPTH, D_MODEL)),
         ("s5_norm", (DEPTH, D_MODEL)), ("ssd_conv_b", (DEPTH, SSD_CONV_DIM)), ("ssd_dt_bias", (DEPTH, SSD_HEADS)),
         ("ssd_a_log", (DEPTH, SSD_HEADS)), ("ssd_d", (DEPTH, SSD_HEADS)), ("ssd_norm", (DEPTH, D_MODEL)),
         ("norm_ffn", (DEPTH, D_MODEL)), ("norm_final", (D_MODEL,)))
WEIGHT_ORDER = ("norm_mix", "w_in", "s5_lam_re", "s5_lam_im", "s5_log_step", "s5_b_re", "s5_b_im", "s5_c_re", "s5_c_im", "s5_d",
                "s5_w_glu", "s5_b_glu", "s5_norm", "ssd_conv_w", "ssd_conv_b", "ssd_dt_bias", "ssd_a_log", "ssd_d", "ssd_norm",
                "w_out", "norm_ffn", "w_gate", "w_up", "w_down", "norm_final")


def _size(shape):
    n = 1
    for s in shape:
        n *= s
    return n


def _round_up(n, m):
    return -(-n // m) * m


SMALL_SIZE = sum(_size(s) for _, s in SMALL)
SMALL_ROWS = _round_up(-(-SMALL_SIZE // (N_DEV * LANE)), SUBLANE)


def _pack(parts, rows, dtype):
    flat = jnp.concatenate([p.reshape(-1).astype(dtype) for p in parts])
    return jnp.pad(flat, (0, rows * LANE - flat.shape[0])).reshape(rows, LANE)


def _unpack(flat, specs):
    out, off = {}, 0
    flat = flat.reshape(-1)
    for name, shape in specs:
        out[name] = flat[off:off + _size(shape)].reshape(shape)
        off += _size(shape)
    return out


def kernel(x, norm_mix, w_in, s5_lam_re, s5_lam_im, s5_log_step, s5_b_re, s5_b_im, s5_c_re, s5_c_im, s5_d, s5_w_glu, s5_b_glu, s5_norm, ssd_conv_w, ssd_conv_b, ssd_dt_bias, ssd_a_log, ssd_d, ssd_norm, w_out, norm_ffn, w_gate, w_up, w_down, norm_final, loss_target, m_norm_mix, m_w_in, m_s5_lam_re, m_s5_lam_im, m_s5_log_step, m_s5_b_re, m_s5_b_im, m_s5_c_re, m_s5_c_im, m_s5_d, m_s5_w_glu, m_s5_b_glu, m_s5_norm, m_ssd_conv_w, m_ssd_conv_b, m_ssd_dt_bias, m_ssd_a_log, m_ssd_d, m_ssd_norm, m_w_out, m_norm_ffn, m_w_gate, m_w_up, m_w_down, m_norm_final, v_norm_mix, v_w_in, v_s5_lam_re, v_s5_lam_im, v_s5_log_step, v_s5_b_re, v_s5_b_im, v_s5_c_re, v_s5_c_im, v_s5_d, v_s5_w_glu, v_s5_b_glu, v_s5_norm, v_ssd_conv_w, v_ssd_conv_b, v_ssd_dt_bias, v_ssd_a_log, v_ssd_d, v_ssd_norm, v_w_out, v_norm_ffn, v_w_gate, v_w_up, v_w_down, v_norm_final):
    given = dict(locals())
    w = {n: given[n] for n in WEIGHT_ORDER}
    m = {n: given["m_" + n] for n in WEIGHT_ORDER}
    v = {n: given["v_" + n] for n in WEIGHT_ORDER}
    big_names = tuple(n for n, _, _ in BIG)
    matmul_names = tuple(n for n in big_names if n != "ssd_conv_w")

    conv_hi = w["ssd_conv_w"].astype(BF16)
    conv_lo = (w["ssd_conv_w"] - conv_hi.astype(F32)).astype(BF16)
    row_pad = ((0, 0), (0, FFN_BLOCK_PAD - FFN_BLOCK), (0, 0))
    as_rows = {"w_gate": jnp.swapaxes(w["w_gate"], 1, 2), "w_up": jnp.swapaxes(w["w_up"], 1, 2), "w_down": w["w_down"]}
    to_send = [jnp.pad(as_rows[n].astype(BF16), row_pad) if n in as_rows else w[n].astype(BF16) for n in matmul_names]

    def layer_blocks(i):
        return [a[i] for a in to_send] + [jnp.stack([conv_hi[i], conv_lo[i]])]

    def as_layer_weights(gathered):
        blk = dict(zip(matmul_names, gathered))
        pair = gathered[-1].astype(F32)
        blk["ssd_conv_w"] = jnp.moveaxis(pair[:, 0] + pair[:, 1], 0, 1).reshape(SSD_CONV, SSD_CONV_DIM)
        return blk

    gathered0 = [g[0] for g in _gather_blocks(layer_blocks(0), [False] * (len(matmul_names) + 1), name="gather_weights_0")]
    blocks1 = layer_blocks(1)
    sems1 = _exchange_start(blocks1, _own_slots(blocks1, False, name="gather_own_1"), False, name="gather_start_1")
    h, _ = lax.optimization_barrier((x[0], sems1[-1]))
    prepared = [_prepare_layer(w, as_layer_weights(gathered0), 0), None]
    saved = [None, None]
    h, saved[0] = _layer_fwd(h, prepared[0], 0)
    gathered1 = _exchange_wait(*sems1[:4], h, False, name="gather_wait_1")
    prepared[1] = _prepare_layer(w, as_layer_weights(gathered1), 1)
    h, saved[1] = _layer_fwd(h, prepared[1], 1)
    loss, dh, g_final = _loss_head(h, w["norm_final"].reshape(1, D_MODEL), loss_target[0], name="loss_head")

    layer_grads = [None, None]
    dh, layer_grads[1] = _layer_bwd(dh, saved[1], prepared[1], 1)
    slots1 = [layer_grads[1][n] for n in big_names]
    sems2 = _exchange_start(slots1, _own_slots(slots1, True, name="exchange_own_1"), True, name="exchange_start_1")
    dh, _ = lax.optimization_barrier((dh, sems2[-1]))
    grad_x, layer_grads[0] = _layer_bwd(dh, saved[0], prepared[0], 0)

    small = jnp.concatenate([g_final.reshape(-1) if n == "norm_final"
                             else jnp.stack([layer_grads[i][n] for i in range(DEPTH)]).reshape(-1) for n, _ in SMALL])
    small_slots = jnp.pad(small, (0, N_DEV * SMALL_ROWS * LANE - small.shape[0])).reshape(N_DEV, SMALL_ROWS, LANE)
    received0 = _exchange_blocks([(layer_grads[0][n],) for n in big_names] + [(small_slots,)], name="exchange_gradients_0")
    received1 = _exchange_wait(*sems2[:4], grad_x, True, name="exchange_wait_1")

    results = {}
    for j, n in enumerate(big_names):
        recv = (received0[j].reshape(received0[j].shape[:1] + received0[j].shape[2:]), received1[j])
        if n in ("w_gate", "w_up"):
            g = jnp.stack([jnp.swapaxes(_sum_senders(recv[i], name=f"sum_{n}_{i}"), 0, 1) for i in range(DEPTH)])
            results[n] = [g, *_adamw_blocks(g, w[n], m[n], v[n], name=f"adamw_{n}")]
        else:
            first = _sum_adamw(recv[1], w[n], m[n], v[n], 1, None, name=f"sum_adamw_{n}_1")
            results[n] = _sum_adamw(recv[0], w[n], m[n], v[n], 0, first, name=f"sum_adamw_{n}_0")
    g_part = _sum_slots(received0[-1].reshape(N_DEV, SMALL_ROWS, LANE), name="sum_replicated")
    g_small = _gather_blocks([g_part], [False], name="gather_replicated")[0][0].reshape(N_DEV * SMALL_ROWS, LANE)
    small_rows = N_DEV * SMALL_ROWS
    d_small, m_small, v_small = _adamw_rows(
        g_small, _pack([w[n] for n, _ in SMALL], small_rows, F32), _pack([m[n] for n, _ in SMALL], small_rows, F32),
        _pack([v[n] for n, _ in SMALL], small_rows, F32), name="adamw_replicated")
    for k, packed in enumerate((g_small, d_small, m_small, v_small)):
        for n, arr in _unpack(packed, SMALL).items():
            results.setdefault(n, [None] * 4)[k] = arr

    outs = [results[n][k] for k in range(4) for n in WEIGHT_ORDER]
    total_loss = lax.psum(loss[0, 0], ("x", "y", "c"))
    return (total_loss, grad_x[None], *outs)
```

```python
import functools

import jax
import jax.numpy as jnp
from jax import lax
from jax.experimental import pallas as pl
from jax.experimental.pallas import tpu as pltpu

F32 = jnp.float32
BF16 = jnp.bfloat16
MESH_ID = pl.DeviceIdType.MESH

N_DEV = 8
DEPTH = 2
D_MODEL = 1024
S5_GROUPS = 64
S5_GROUP = 16
S5_STATE = 64
S5_LANES = S5_GROUPS * S5_STATE
SSD_HEADS = 16
SSD_HEAD_DIM = 64
SSD_STATE = 128
SSD_CHUNK = 128
SSD_CONV = 4
SSD_CONV_DIM = 1536
FFN_HIDDEN = 2816
IN_PROJ = 3600
EPS = 1e-6
LANE = 128
SUBLANE = 8
VMEM_LIMIT = 56 * 1024 * 1024

ADAM_LR = 0.001
ADAM_B1 = 0.9
ADAM_B2 = 0.999
ADAM_EPS = 1e-08
ADAM_WD = 0.01
ADAM_STEP = 10

TOK_TILE = 256
S5_TILE = 128
S5_SEG = S5_TILE // SUBLANE


def _sigmoid(x):
    return jax.nn.sigmoid(x)


def _silu(x):
    return x * _sigmoid(x)


def _gelu(x):
    return 0.5 * x * (1.0 + jnp.tanh(0.7978845608028654 * (x + 0.044715 * (x * x * x))))


def _softplus(x):
    return jnp.maximum(x, 0.0) + jnp.log(1.0 + jnp.exp(-jnp.abs(x)))


def _rms(x, g):
    r = lax.rsqrt(jnp.mean(x * x, axis=-1, keepdims=True) + EPS)
    return x * r * g


def _nn(a, b):
    return lax.dot_general(a.astype(BF16), b.astype(BF16), (((1,), (0,)), ((), ())), preferred_element_type=F32)


def _nt(a, b):
    return lax.dot_general(a.astype(BF16), b.astype(BF16), (((1,), (1,)), ((), ())), preferred_element_type=F32)


def _tn(a, b):
    return lax.dot_general(a.astype(BF16), b.astype(BF16), (((0,), (0,)), ((), ())), preferred_element_type=F32)


def _nn_f32(a, b):
    return lax.dot_general(a, b, (((1,), (0,)), ((), ())), precision=lax.Precision.HIGHEST, preferred_element_type=F32)


def _tn_f32(a, b):
    return lax.dot_general(a, b, (((0,), (0,)), ((), ())), precision=lax.Precision.HIGHEST, preferred_element_type=F32)


@jax.custom_vjp
def _nn_d(a, b):
    return _nn(a, b)


_nn_d.defvjp(lambda a, b: (_nn(a, b), (a, b)), lambda r, g: (_nt(g, r[1]), _tn(r[0], g)))


@jax.custom_vjp
def _nt_d(a, b):
    return _nt(a, b)


_nt_d.defvjp(lambda a, b: (_nt(a, b), (a, b)), lambda r, g: (_nn(g, r[1]), _tn(g, r[0])))


@jax.custom_vjp
def _tn_d(a, b):
    return _tn(a, b)


_tn_d.defvjp(lambda a, b: (_tn(a, b), (a, b)), lambda r, g: (_nt(r[1], g), _nn(r[0], g)))


@jax.custom_vjp
def _cumsum_rows(tri, x):
    return _nn_f32(tri, x)


_cumsum_rows.defvjp(lambda tri, x: (_nn_f32(tri, x), tri), lambda tri, g: (jnp.zeros_like(tri), _tn_f32(tri, g)))


def _full(shape):
    zeros = (0,) * len(shape)
    return pl.BlockSpec(shape, lambda *_: zeros)


def _const(shape):
    zeros = (0,) * len(shape)
    return pl.BlockSpec(shape, lambda *_: zeros, pipeline_mode=pl.Buffered(1))


def _rows(tile, width, n_tiles=None):
    if n_tiles is None:
        return pl.BlockSpec((tile, width), lambda i: (i, 0))
    return pl.BlockSpec((tile, width), lambda i: (n_tiles - 1 - i, 0))


def _call(body, name, grid, in_specs, out_specs, out_shape, scratch=()):
    return pl.pallas_call(
        body, name=name, grid=grid, in_specs=in_specs, out_specs=out_specs, out_shape=out_shape,
        scratch_shapes=list(scratch),
        compiler_params=pltpu.CompilerParams(dimension_semantics=("arbitrary",) * len(grid),
                                             vmem_limit_bytes=VMEM_LIMIT))


def _sds(shape, dtype=F32):
    return jax.ShapeDtypeStruct(shape, dtype)


def _tile_of(n, cap=512):
    if n <= LANE:
        return n
    best = LANE
    for t in range(LANE, cap + 1, LANE):
        if n % t == 0:
            best = t
    return best


def _inproj_fwd(x, nm, wu, wz, wx, wd, name):
    n_tok = x.shape[0]
    tm = TOK_TILE

    def body(x_ref, nm_ref, wu_ref, wz_ref, wx_ref, wd_ref, u_ref, z_ref, xbc_ref, dt_ref):
        h = _rms(x_ref[...], nm_ref[...]).astype(BF16)
        u_ref[...] = _nn(h, wu_ref[...])
        z_ref[...] = _nn(h, wz_ref[...])
        xbc_ref[...] = _nn(h, wx_ref[...])
        dt_ref[...] = _nn(h, wd_ref[...])

    return _call(
        body, name, (n_tok // tm,),
        [_rows(tm, D_MODEL), _const((1, D_MODEL)), _const(wu.shape), _const(wz.shape), _const(wx.shape), _const(wd.shape)],
        [_rows(tm, D_MODEL), _rows(tm, D_MODEL), _rows(tm, SSD_CONV_DIM), _rows(tm, LANE)],
        [_sds((n_tok, D_MODEL)), _sds((n_tok, D_MODEL)), _sds((n_tok, SSD_CONV_DIM)), _sds((n_tok, LANE))],
    )(x, nm, wu, wz, wx, wd)


def _inproj_bwd(x, nm, du, dz, dxbc, ddt, dres, wu, wz, wx, wd, name):
    n_tok = x.shape[0]
    tm = TOK_TILE

    def body(x_ref, nm_ref, du_ref, dz_ref, dxbc_ref, ddt_ref, dres_ref, wu_ref, wz_ref, wx_ref, wd_ref,
             dx_ref, h_ref, dnm_ref):
        dh = (_nt(du_ref[...], wu_ref[...]) + _nt(dz_ref[...], wz_ref[...])
              + _nt(dxbc_ref[...], wx_ref[...]) + _nt(ddt_ref[...], wd_ref[...]))
        h, vjp = jax.vjp(_rms, x_ref[...], nm_ref[...])
        dx, dnm = vjp(dh)
        dx_ref[...] = dres_ref[...] + dx
        h_ref[...] = h.astype(BF16)

        @pl.when(pl.program_id(0) == 0)
        def _():
            dnm_ref[...] = jnp.zeros_like(dnm_ref)

        dnm_ref[...] += dnm

    return _call(
        body, name, (n_tok // tm,),
        [_rows(tm, D_MODEL), _const((1, D_MODEL)), _rows(tm, D_MODEL), _rows(tm, D_MODEL), _rows(tm, SSD_CONV_DIM),
         _rows(tm, LANE), _rows(tm, D_MODEL), _const(wu.shape), _const(wz.shape), _const(wx.shape), _const(wd.shape)],
        [_rows(tm, D_MODEL), _rows(tm, D_MODEL), _full((1, D_MODEL))],
        [_sds((n_tok, D_MODEL)), _sds((n_tok, D_MODEL), BF16), _sds((1, D_MODEL))],
    )(x, nm, du, dz, dxbc, ddt, dres, wu, wz, wx, wd)


def _ffn_act(gt, up):
    return _silu(gt) * up


FFN_BLOCK = FFN_HIDDEN // N_DEV
FFN_BLOCK_PAD = -(-FFN_BLOCK // LANE) * LANE


FFN_PAD = N_DEV * FFN_BLOCK_PAD


def _mix_ffn_fwd(x0, ya, yb, wo, nf, wg, wu, wd, name):
    n_tok = x0.shape[0]
    tm = TOK_TILE

    def body(x0_ref, ya_ref, yb_ref, wo_ref, nf_ref, wg_ref, wu_ref, wd_ref, x1_ref, x2_ref):
        x1 = x0_ref[...] + _nn(ya_ref[...], wo_ref[:D_MODEL, :]) + _nn(yb_ref[...], wo_ref[D_MODEL:, :])
        h = _rms(x1, nf_ref[...]).astype(BF16)
        x1_ref[...] = x1
        x2_ref[...] = x1 + _nn(_ffn_act(_nt(h, wg_ref[...]), _nt(h, wu_ref[...])), wd_ref[...])

    return _call(
        body, name, (n_tok // tm,),
        [_rows(tm, D_MODEL), _rows(tm, D_MODEL), _rows(tm, D_MODEL), _const(wo.shape),
         _const((1, D_MODEL)), _const(wg.shape), _const(wu.shape), _const(wd.shape)],
        [_rows(tm, D_MODEL), _rows(tm, D_MODEL)],
        [_sds((n_tok, D_MODEL)), _sds((n_tok, D_MODEL))],
    )(x0, ya, yb, wo, nf, wg, wu, wd)


def _mix_ffn_bwd(x1, dx2, wo, nf, wg, wu, wd, after, name):
    n_tok = x1.shape[0]
    tm = TOK_TILE
    n_chunks = 3
    hc = FFN_PAD // n_chunks

    def body(x1_ref, dx2_ref, wo_ref, nf_ref, wg_ref, wu_ref, wd_ref, after_ref,
             dx1_ref, dya_ref, dyb_ref, h_ref, a_ref, dgt_ref, dup_ref, dnf_ref):
        dx2 = dx2_ref[...]
        dx2b = dx2.astype(BF16)
        h, rms_vjp = jax.vjp(_rms, x1_ref[...], nf_ref[...])
        hb = h.astype(BF16)
        dh = jnp.zeros_like(h)
        for c in range(n_chunks):
            rows = pl.ds(c * hc, hc)
            a, act_vjp = jax.vjp(_ffn_act, _nt(hb, wg_ref[rows, :]), _nt(hb, wu_ref[rows, :]))
            dgt, dup = act_vjp(_nt(dx2b, wd_ref[rows, :]))
            a_ref[:, c * hc:(c + 1) * hc] = a.astype(BF16)
            dgt_ref[:, c * hc:(c + 1) * hc] = dgt.astype(BF16)
            dup_ref[:, c * hc:(c + 1) * hc] = dup.astype(BF16)
            dh = dh + _nn(dgt, wg_ref[rows, :]) + _nn(dup, wu_ref[rows, :])
        dx, dnf = rms_vjp(dh)
        dx1 = dx2 + dx
        dx1_ref[...] = dx1
        dya_ref[...] = _nt(dx1, wo_ref[:D_MODEL, :])
        dyb_ref[...] = _nt(dx1, wo_ref[D_MODEL:, :])
        h_ref[...] = hb

        @pl.when(pl.program_id(0) == 0)
        def _():
            dnf_ref[...] = jnp.zeros_like(dnf_ref)

        dnf_ref[...] += dnf

    hidden = _rows(tm, FFN_PAD)
    return _call(
        body, name, (n_tok // tm,),
        [_rows(tm, D_MODEL), _rows(tm, D_MODEL), _const(wo.shape), _const((1, D_MODEL)),
         _const(wg.shape), _const(wu.shape), _const(wd.shape), HBM],
        [_rows(tm, D_MODEL), _rows(tm, D_MODEL), _rows(tm, D_MODEL), _rows(tm, D_MODEL), hidden, hidden, hidden,
         _full((1, D_MODEL))],
        [_sds((n_tok, D_MODEL)), _sds((n_tok, D_MODEL)), _sds((n_tok, D_MODEL)), _sds((n_tok, D_MODEL), BF16),
         _sds((n_tok, FFN_PAD), BF16), _sds((n_tok, FFN_PAD), BF16), _sds((n_tok, FFN_PAD), BF16), _sds((1, D_MODEL))],
    )(x1, dx2, wo, nf, wg, wu, wd, after)


def _loss_head(x, nf, target, name):
    n_tok = x.shape[0]
    tm = TOK_TILE

    def loss_of(xv, g, t):
        e = _rms(xv, g) - t
        return 0.5 * jnp.sum(jnp.sum(e * e, axis=-1, keepdims=True) * (1.0 / D_MODEL), axis=0, keepdims=True)

    def body(x_ref, nf_ref, t_ref, loss_ref, dx_ref, dnf_ref):
        loss, vjp = jax.vjp(functools.partial(loss_of, t=t_ref[...]), x_ref[...], nf_ref[...])
        dx, dnf = vjp(jnp.ones_like(loss))
        dx_ref[...] = dx

        @pl.when(pl.program_id(0) == 0)
        def _():
            dnf_ref[...] = jnp.zeros_like(dnf_ref)
            loss_ref[...] = jnp.zeros_like(loss_ref)

        dnf_ref[...] += dnf
        loss_ref[...] += jnp.broadcast_to(loss, loss_ref.shape)

    return _call(
        body, name, (n_tok // tm,),
        [_rows(tm, D_MODEL), _const((1, D_MODEL)), _rows(tm, D_MODEL)],
        [_full((SUBLANE, LANE)), _rows(tm, D_MODEL), _full((1, D_MODEL))],
        [_sds((SUBLANE, LANE)), _sds((n_tok, D_MODEL)), _sds((1, D_MODEL))],
    )(x, nf, target)


GRAD_WIRE = BF16


def _matmul_tn(a, b, name):
    n_tok, k1 = a.shape
    k2 = b.shape[1]
    t1, t2 = _tile_of(k1), _tile_of(k2)

    def body(a_ref, b_ref, o_ref):
        o_ref[...] = _tn(a_ref[...], b_ref[...]).astype(GRAD_WIRE)

    return pl.pallas_call(
        body, name=name, grid=(k1 // t1, k2 // t2),
        in_specs=[pl.BlockSpec((n_tok, t1), lambda i, j: (0, i)), pl.BlockSpec((n_tok, t2), lambda i, j: (0, j))],
        out_specs=pl.BlockSpec((t1, t2), lambda i, j: (i, j)),
        out_shape=_sds((k1, k2), GRAD_WIRE),
        compiler_params=pltpu.CompilerParams(dimension_semantics=("arbitrary", "arbitrary"), vmem_limit_bytes=VMEM_LIMIT),
    )(a, b)


def _tn_params():
    return pltpu.CompilerParams(dimension_semantics=("arbitrary", "arbitrary"), vmem_limit_bytes=VMEM_LIMIT)


def _matmul_tn_lhs_blocks(a, b, width, keep, name):
    n_tok, k1 = a.shape
    k2 = b.shape[1]
    t2 = _tile_of(k2)

    def body(a_ref, b_ref, o_ref):
        o_ref[...] = _tn(a_ref[...], b_ref[...])[:keep, :].astype(GRAD_WIRE)

    return pl.pallas_call(
        body, name=name, grid=(k1 // width, k2 // t2),
        in_specs=[pl.BlockSpec((n_tok, width), lambda d, j: (0, d)), pl.BlockSpec((n_tok, t2), lambda d, j: (0, j))],
        out_specs=pl.BlockSpec((None, keep, t2), lambda d, j: (d, 0, j)),
        out_shape=_sds((k1 // width, keep, k2), GRAD_WIRE), compiler_params=_tn_params(),
    )(a, b)


def _matmul_tn_pair(a0, a1, b, name):
    n_tok, k1 = a0.shape
    k2 = b.shape[1]
    t1, t2 = _tile_of(k1), _tile_of(k2)

    def body(a0_ref, a1_ref, b_ref, o_ref):
        @pl.when(pl.program_id(0) == 0)
        def _():
            o_ref[...] = _tn(a0_ref[...], b_ref[...]).astype(GRAD_WIRE)

        @pl.when(pl.program_id(0) == 1)
        def _():
            o_ref[...] = _tn(a1_ref[...], b_ref[...]).astype(GRAD_WIRE)

    lhs = pl.BlockSpec((n_tok, t1), lambda s, i, j: (0, i))
    return pl.pallas_call(
        body, name=name, grid=(2, k1 // t1, k2 // t2),
        in_specs=[lhs, lhs, pl.BlockSpec((n_tok, t2), lambda s, i, j: (0, j))],
        out_specs=pl.BlockSpec((None, t1, t2), lambda s, i, j: (s, i, j)),
        out_shape=_sds((2, k1, k2), GRAD_WIRE),
        compiler_params=pltpu.CompilerParams(dimension_semantics=("arbitrary",) * 3, vmem_limit_bytes=VMEM_LIMIT),
    )(a0, a1, b)


W_IN_BLOCK = IN_PROJ // N_DEV
W_IN_SPLITS = (D_MODEL, 2 * D_MODEL, 2 * D_MODEL + SSD_CONV_DIM)
RELAYOUT_TILE = 256


def _w_in_split(blocks, after, name):
    tr = RELAYOUT_TILE

    def body(b_ref, after_ref, wu_ref, wz_ref, wx_ref, wd_ref):
        full = jnp.concatenate([b_ref[d] for d in range(N_DEV)], axis=1)
        wu_ref[...] = full[:, :W_IN_SPLITS[0]]
        wz_ref[...] = full[:, W_IN_SPLITS[0]:W_IN_SPLITS[1]]
        wx_ref[...] = full[:, W_IN_SPLITS[1]:W_IN_SPLITS[2]]
        wd_ref[...] = jnp.concatenate([full[:, W_IN_SPLITS[2]:], jnp.zeros((tr, LANE - SSD_HEADS), full.dtype)], axis=1)

    return _call(
        body, name, (D_MODEL // tr,), [pl.BlockSpec((N_DEV, tr, W_IN_BLOCK), lambda i: (0, i, 0)), HBM],
        [_rows(tr, D_MODEL), _rows(tr, D_MODEL), _rows(tr, SSD_CONV_DIM), _rows(tr, LANE)],
        [_sds((D_MODEL, D_MODEL), BF16), _sds((D_MODEL, D_MODEL), BF16), _sds((D_MODEL, SSD_CONV_DIM), BF16),
         _sds((D_MODEL, LANE), BF16)],
    )(blocks, after)


def _w_in_grad_blocks(gu, gz, gx, gdt, name):
    tr = RELAYOUT_TILE

    def body(gu_ref, gz_ref, gx_ref, gdt_ref, o_ref):
        full = jnp.concatenate([gu_ref[...], gz_ref[...], gx_ref[...], gdt_ref[...]], axis=1)
        for d in range(N_DEV):
            o_ref[d] = full[:, d * W_IN_BLOCK:(d + 1) * W_IN_BLOCK]

    return _call(
        body, name, (D_MODEL // tr,),
        [_rows(tr, D_MODEL), _rows(tr, D_MODEL), _rows(tr, SSD_CONV_DIM), _rows(tr, LANE)],
        [pl.BlockSpec((N_DEV, tr, W_IN_BLOCK), lambda i: (0, i, 0))], [_sds((N_DEV, D_MODEL, W_IN_BLOCK), gu.dtype)],
    )(gu, gz, gx, gdt)[0]


S5_SLICES = D_MODEL // LANE
S5_SLICE_STATES = S5_LANES // S5_SLICES
SCAN_LANES = 512


def _s5_scan(br_ref, bi_ref, a_r, a_i, car_r, car_i, ini_r, ini_i, reverse, xr_ref=None, xi_ref=None,
             acc_r=None, acc_i=None):
    n_rows = br_ref.shape[1]
    seg = n_rows // SUBLANE
    order = range(SUBLANE - 1, -1, -1) if reverse else range(SUBLANE)

    def rows(t):
        return pl.ds((seg - 1 - t) if reverse else t, SUBLANE, stride=seg)

    tiles_per = SCAN_LANES // LANE

    def load(ref, t, lb):
        return jnp.concatenate([ref[lb * tiles_per + j, rows(t), :] for j in range(tiles_per)], axis=1)

    def store(ref, t, lb, val):
        for j in range(tiles_per):
            ref[lb * tiles_per + j, rows(t), :] = val[:, j * LANE:(j + 1) * LANE]

    for lb in range(S5_LANES // SCAN_LANES):
        lanes = pl.ds(lb * SCAN_LANES, SCAN_LANES)
        ar1, ai1 = a_r[:, lb * SCAN_LANES:(lb + 1) * SCAN_LANES], a_i[:, lb * SCAN_LANES:(lb + 1) * SCAN_LANES]
        ar8 = jnp.broadcast_to(ar1, (SUBLANE, SCAN_LANES))
        ai8 = jnp.broadcast_to(ai1, (SUBLANE, SCAN_LANES))

        def local(t, c):
            sr, si = c
            return (ar8 * sr - ai8 * si + load(br_ref, t, lb), ar8 * si + ai8 * sr + load(bi_ref, t, lb))

        zero = jnp.zeros((SUBLANE, SCAN_LANES), F32)
        er, ei = lax.fori_loop(0, seg, local, (zero, zero))
        pr, pi = ar1, ai1
        for _ in range(seg.bit_length() - 1):
            pr, pi = pr * pr - pi * pi, 2.0 * pr * pi
        cr, ci = car_r[:, lanes], car_i[:, lanes]
        for s in order:
            ini_r[s:s + 1, lanes] = cr
            ini_i[s:s + 1, lanes] = ci
            cr, ci = pr * cr - pi * ci + er[s:s + 1, :], pr * ci + pi * cr + ei[s:s + 1, :]
        car_r[:, lanes] = cr
        car_i[:, lanes] = ci

        if xr_ref is None:
            def final(t, c):
                sr, si = c
                nr = ar8 * sr - ai8 * si + load(br_ref, t, lb)
                ni = ar8 * si + ai8 * sr + load(bi_ref, t, lb)
                store(br_ref, t, lb, nr)
                store(bi_ref, t, lb, ni)
                return nr, ni

            lax.fori_loop(0, seg, final, (ini_r[:, lanes], ini_i[:, lanes]))
        else:
            def final_acc(t, c):
                sr, si, gr, gi = c
                xr, xi = load(xr_ref, t, lb), load(xi_ref, t, lb)
                gr = gr + sr * xr + si * xi
                gi = gi + si * xr - sr * xi
                nr = ar8 * sr - ai8 * si + load(br_ref, t, lb)
                ni = ar8 * si + ai8 * sr + load(bi_ref, t, lb)
                store(br_ref, t, lb, nr)
                store(bi_ref, t, lb, ni)
                return nr, ni, gr, gi

            _, _, gr, gi = lax.fori_loop(0, seg, final_acc,
                                         (ini_r[:, lanes], ini_i[:, lanes], acc_r[:, lanes], acc_i[:, lanes]))
            acc_r[:, lanes] = gr
            acc_i[:, lanes] = gi


def _s5_tail(gg, q, sn):
    return _rms(gg * _sigmoid(q), sn)


S5_STATE_TILES = S5_LANES // LANE
TILES_PER_SLICE = S5_SLICE_STATES // LANE


def _put_states(ref, k, val):
    for j in range(TILES_PER_SLICE):
        ref[k * TILES_PER_SLICE + j] = val[:, j * LANE:(j + 1) * LANE]


def _get_states(ref, k):
    return jnp.concatenate([ref[k * TILES_PER_SLICE + j] for j in range(TILES_PER_SLICE)], axis=1)


def _state_rows(tile, n_tiles=None):
    if n_tiles is None:
        return pl.BlockSpec((S5_STATE_TILES, tile, LANE), lambda i: (0, i, 0))
    return pl.BlockSpec((S5_STATE_TILES, tile, LANE), lambda i: (0, n_tiles - 1 - i, 0))


def _s5_fwd(u, a_r, a_i, bdb, bcr, bci, dsk, wglu, bglu, sn, name):
    n_tok = u.shape[0]
    tc = S5_TILE
    sw = S5_SLICE_STATES

    def body(u_ref, ar_ref, ai_ref, bdb_ref, bcr_ref, bci_ref, d_ref, wg_ref, bg_ref, sn_ref,
             ya_ref, xr_ref, xi_ref, v_ref, car_r, car_i, ini_r, ini_i):
        @pl.when(pl.program_id(0) == 0)
        def _():
            car_r[...] = jnp.zeros_like(car_r)
            car_i[...] = jnp.zeros_like(car_i)

        u_t = u_ref[...]
        ub = u_t.astype(BF16)
        for k in range(S5_SLICES):
            bu = _nn(ub[:, k * LANE:(k + 1) * LANE], bdb_ref[k])
            _put_states(xr_ref, k, bu[:, :sw])
            _put_states(xi_ref, k, bu[:, sw:])
        _s5_scan(xr_ref, xi_ref, ar_ref[...], ai_ref[...], car_r, car_i, ini_r, ini_i, reverse=False)
        vs = [_nn(_get_states(xr_ref, k), bcr_ref[k]) - _nn(_get_states(xi_ref, k), bci_ref[k])
              for k in range(S5_SLICES)]
        v = jnp.concatenate(vs, axis=1) + d_ref[...] * u_t
        v_ref[...] = v
        gg = _gelu(v)
        ya_ref[...] = _s5_tail(gg, _nn(gg, wg_ref[...]) + bg_ref[...], sn_ref[...])

    return _call(
        body, name, (n_tok // tc,),
        [_rows(tc, D_MODEL), _const((1, S5_LANES)), _const((1, S5_LANES)), _const(bdb.shape), _const(bcr.shape),
         _const(bci.shape), _const((1, D_MODEL)), _const(wglu.shape), _const((1, D_MODEL)), _const((1, D_MODEL))],
        [_rows(tc, D_MODEL), _state_rows(tc), _state_rows(tc), _rows(tc, D_MODEL)],
        [_sds((n_tok, D_MODEL)), _sds((S5_STATE_TILES, n_tok, LANE)), _sds((S5_STATE_TILES, n_tok, LANE)),
         _sds((n_tok, D_MODEL))],
        scratch=[pltpu.VMEM((1, S5_LANES), F32), pltpu.VMEM((1, S5_LANES), F32),
                 pltpu.VMEM((SUBLANE, S5_LANES), F32), pltpu.VMEM((SUBLANE, S5_LANES), F32)],
    )(u, a_r, a_i, bdb, bcr, bci, dsk, wglu, bglu, sn)


def _s5_bwd(dya, v, u, xr, xi, a_r, a_i, bdb, bcr, bci, dsk, wglu, bglu, sn, name):
    n_tok = u.shape[0]
    tc = S5_TILE
    nt = n_tok // tc
    sw = S5_SLICE_STATES

    def body(dya_ref, v_ref, u_ref, xr_ref, xi_ref, ar_ref, ai_ref, bdb_ref, bcr_ref, bci_ref, d_ref, wg_ref, bg_ref, sn_ref,
             du_ref, gg_ref, dq_ref, gbdb_ref, gbcr_ref, gbci_ref, gar_ref, gai_ref, gd_ref, gbg_ref, gsn_ref,
             gr_ref, gi_ref, car_r, car_i, ini_r, ini_i):
        @pl.when(pl.program_id(0) == 0)
        def _():
            for r in (car_r, car_i, gbdb_ref, gbcr_ref, gbci_ref, gar_ref, gai_ref, gd_ref, gbg_ref, gsn_ref):
                r[...] = jnp.zeros_like(r)

        u_t = u_ref[...]
        gg, gelu_vjp = jax.vjp(_gelu, v_ref[...])
        _, tail_vjp = jax.vjp(_s5_tail, gg, _nn(gg, wg_ref[...]) + bg_ref[...], sn_ref[...])
        dgg, dq, dsn = tail_vjp(dya_ref[...])
        (dv,) = gelu_vjp(dgg + _nt(dq, wg_ref[...]))
        gg_ref[...] = gg.astype(BF16)
        dq_ref[...] = dq.astype(BF16)
        gd_ref[...] += jnp.sum(dv * u_t, axis=0, keepdims=True)
        gbg_ref[...] += jnp.sum(dq, axis=0, keepdims=True)
        gsn_ref[...] += dsn
        dvb = dv.astype(BF16)
        for k in range(S5_SLICES):
            dvk = dvb[:, k * LANE:(k + 1) * LANE]
            _put_states(gr_ref, k, _nt(dvk, bcr_ref[k]))
            _put_states(gi_ref, k, -_nt(dvk, bci_ref[k]))
            gbcr_ref[k] += _tn(_get_states(xr_ref, k), dvk)
            gbci_ref[k] -= _tn(_get_states(xi_ref, k), dvk)
        _s5_scan(gr_ref, gi_ref, ar_ref[...], -ai_ref[...], car_r, car_i, ini_r, ini_i, reverse=True,
                 xr_ref=xr_ref, xi_ref=xi_ref, acc_r=gar_ref, acc_i=gai_ref)
        ub = u_t.astype(BF16)
        dus = []
        for k in range(S5_SLICES):
            gk_r, gk_i = _get_states(gr_ref, k).astype(BF16), _get_states(gi_ref, k).astype(BF16)
            bk = bdb_ref[k]
            dus.append(_nt(gk_r, bk[:, :sw]) + _nt(gk_i, bk[:, sw:]))
            uk = ub[:, k * LANE:(k + 1) * LANE]
            gbdb_ref[k, :, :sw] += _tn(uk, gk_r)
            gbdb_ref[k, :, sw:] += _tn(uk, gk_i)
        du_ref[...] = jnp.concatenate(dus, axis=1) + d_ref[...] * dv

    rev = functools.partial(_rows, n_tiles=nt)
    return _call(
        body, name, (nt,),
        [rev(tc, D_MODEL), rev(tc, D_MODEL), rev(tc, D_MODEL), _state_rows(tc, nt), _state_rows(tc, nt),
         _const((1, S5_LANES)), _const((1, S5_LANES)), _const(bdb.shape), _const(bcr.shape), _const(bci.shape),
         _const((1, D_MODEL)), _const(wglu.shape), _const((1, D_MODEL)), _const((1, D_MODEL))],
        [rev(tc, D_MODEL), rev(tc, D_MODEL), rev(tc, D_MODEL), _full(bdb.shape), _full(bcr.shape), _full(bci.shape),
         _full((SUBLANE, S5_LANES)), _full((SUBLANE, S5_LANES)), _full((1, D_MODEL)), _full((1, D_MODEL)), _full((1, D_MODEL))],
        [_sds((n_tok, D_MODEL)), _sds((n_tok, D_MODEL), BF16), _sds((n_tok, D_MODEL), BF16), _sds(bdb.shape), _sds(bcr.shape),
         _sds(bci.shape), _sds((SUBLANE, S5_LANES)), _sds((SUBLANE, S5_LANES)), _sds((1, D_MODEL)), _sds((1, D_MODEL)),
         _sds((1, D_MODEL))],
        scratch=[pltpu.VMEM((S5_STATE_TILES, tc, LANE), F32), pltpu.VMEM((S5_STATE_TILES, tc, LANE), F32),
                 pltpu.VMEM((1, S5_LANES), F32), pltpu.VMEM((1, S5_LANES), F32),
                 pltpu.VMEM((SUBLANE, S5_LANES), F32), pltpu.VMEM((SUBLANE, S5_LANES), F32)],
    )(dya, v, u, xr, xi, a_r, a_i, bdb, bcr, bci, dsk, wglu, bglu, sn)


SSD_WIDTH = SSD_HEADS * SSD_HEAD_DIM
SSD_GROUPS = 2
HEADS_PER_GROUP = SSD_HEADS // SSD_GROUPS


def _take(x, axis, start, size):
    n = x.shape[axis]

    def sl(v):
        return lax.slice_in_dim(v, start, start + size, axis=axis)

    @jax.custom_vjp
    def f(v):
        return sl(v)

    def bwd(_, g):
        parts = []
        if start:
            parts.append(jnp.zeros(g.shape[:axis] + (start,) + g.shape[axis + 1:], g.dtype))
        parts.append(g)
        if n - start - size:
            parts.append(jnp.zeros(g.shape[:axis] + (n - start - size,) + g.shape[axis + 1:], g.dtype))
        return (jnp.concatenate(parts, axis=axis) if len(parts) > 1 else g,)

    f.defvjp(lambda v: (sl(v), None), bwd)
    return f(x)


def _lane_of(x, h):
    col = lax.broadcasted_iota(jnp.int32, x.shape, 1)
    return jnp.sum(jnp.where(col == h, x, 0.0), axis=1, keepdims=True)


def _ssd_chunk(xc, z, dt, dtb, alog, dvec, gn, st, nn, nt, tn, cumsum, take):
    t_len = xc.shape[0]
    xa = _silu(xc)
    dtp = _softplus(dt + dtb)
    d_a = dtp * (-jnp.exp(alog))
    row = lax.broadcasted_iota(jnp.int32, (t_len, t_len), 0)
    col = lax.broadcasted_iota(jnp.int32, (t_len, t_len), 1)
    causal = row >= col
    cum = cumsum(causal.astype(F32), d_a)
    eye = (row == col).astype(F32)
    ys, sts = [], []
    for g in range(SSD_GROUPS):
        bg = take(xa, 1, SSD_WIDTH + g * SSD_STATE, SSD_STATE)
        cg = take(xa, 1, SSD_WIDTH + (SSD_GROUPS + g) * SSD_STATE, SSD_STATE)
        cb = nt(cg, bg)
        for r in range(HEADS_PER_GROUP):
            h = g * HEADS_PER_GROUP + r
            cc = _lane_of(cum, h)
            cr = jnp.sum(cc * eye, axis=0, keepdims=True)
            decay = jnp.exp(jnp.where(causal, cc - cr, -1e30))
            xh = take(xa, 1, h * SSD_HEAD_DIM, SSD_HEAD_DIM)
            xdt = xh * _lane_of(dtp, h)
            sth = take(st, 0, h * SSD_HEAD_DIM, SSD_HEAD_DIM)
            c_last = jnp.sum(jnp.where(row[:, :1] == t_len - 1, cc, 0.0), axis=0, keepdims=True)
            y = nn(cb * decay, xdt) + jnp.exp(cc) * nt(cg, sth) + _lane_of(dvec, h) * xh
            ys.append(y)
            sts.append(jnp.exp(c_last) * sth + tn(xdt * jnp.exp(c_last - cc), bg))
    y = jnp.concatenate(ys, axis=1) * _silu(z)
    return _rms(y, gn), jnp.concatenate(sts, axis=0)


def _shift_back(cur, prev, j):
    if j == 0:
        return cur
    row = lax.broadcasted_iota(jnp.int32, cur.shape, 0)
    return jnp.where(row < j, pltpu.roll(prev, j, 0), pltpu.roll(cur, j, 0))


def _shift_ahead(cur, nxt, j):
    if j == 0:
        return cur
    n = cur.shape[0]
    row = lax.broadcasted_iota(jnp.int32, cur.shape, 0)
    return jnp.where(row >= n - j, pltpu.roll(nxt, n - j, 0), pltpu.roll(cur, n - j, 0))


def _conv(cur, prev, w, b):
    out = b + w[SSD_CONV - 1:SSD_CONV, :] * cur
    for k in range(SSD_CONV - 1):
        out = out + w[k:k + 1, :] * _shift_back(cur, prev, SSD_CONV - 1 - k)
    return out


def _ssd_fwd(xbc, z, dt, conv_w, conv_b, dtb, alog, dvec, gn, name):
    n_tok = xbc.shape[0]
    tc = SSD_CHUNK
    nc = n_tok // tc
    st_rows = SSD_HEADS * SSD_HEAD_DIM

    def body(cur_ref, prev_ref, z_ref, dt_ref, w_ref, b_ref, dtb_ref, alog_ref, dvec_ref, gn_ref,
             yb_ref, stin_ref, st_ref):
        i = pl.program_id(0)

        @pl.when(i == 0)
        def _():
            st_ref[...] = jnp.zeros_like(st_ref)

        prev = jnp.where(i > 0, prev_ref[...], 0.0)
        xc = _conv(cur_ref[...], prev, w_ref[...], b_ref[...])
        st = st_ref[...]
        stin_ref[0] = st
        yb, st_new = _ssd_chunk(xc, z_ref[...], dt_ref[...], dtb_ref[...], alog_ref[...], dvec_ref[...], gn_ref[...], st,
                                _nn, _nt, _tn, _nn_f32, lambda v, axis, start, size: lax.slice_in_dim(v, start, start + size, axis=axis))
        yb_ref[...] = yb
        st_ref[...] = st_new

    return _call(
        body, name, (nc,),
        [_rows(tc, SSD_CONV_DIM), pl.BlockSpec((tc, SSD_CONV_DIM), lambda i: (jnp.maximum(i - 1, 0), 0)),
         _rows(tc, D_MODEL), _rows(tc, LANE), _const((SSD_CONV, SSD_CONV_DIM)), _const((1, SSD_CONV_DIM)),
         _const((1, LANE)), _const((1, LANE)), _const((1, LANE)), _const((1, D_MODEL))],
        [_rows(tc, D_MODEL), pl.BlockSpec((1, st_rows, SSD_STATE), lambda i: (i, 0, 0))],
        [_sds((n_tok, D_MODEL)), _sds((nc, st_rows, SSD_STATE))],
        scratch=[pltpu.VMEM((st_rows, SSD_STATE), F32)],
    )(xbc, xbc, z, dt, conv_w, conv_b, dtb, alog, dvec, gn)


def _ssd_bwd(dyb, xbc, z, dt, stin, conv_w, conv_b, dtb, alog, dvec, gn, name):
    n_tok = xbc.shape[0]
    tc = SSD_CHUNK
    nc = n_tok // tc
    st_rows = SSD_HEADS * SSD_HEAD_DIM

    def body(dyb_ref, cur_ref, prev_ref, z_ref, dt_ref, stin_ref, w_ref, b_ref, dtb_ref, alog_ref, dvec_ref, gn_ref,
             dxbc_ref, dz_ref, ddt_ref, gw_ref, gb_ref, gdtb_ref, galog_ref, gdvec_ref, ggn_ref,
             dst_ref, dxc_next_ref):
        i = pl.program_id(0)

        @pl.when(i == 0)
        def _():
            for r in (dst_ref, dxc_next_ref, gw_ref, gb_ref, gdtb_ref, galog_ref, gdvec_ref, ggn_ref):
                r[...] = jnp.zeros_like(r)

        cur = cur_ref[...]
        prev = jnp.where(i < nc - 1, prev_ref[...], 0.0)
        w = w_ref[...]
        xc = _conv(cur, prev, w, b_ref[...])
        chunk = functools.partial(_ssd_chunk, nn=_nn_d, nt=_nt_d, tn=_tn_d, cumsum=_cumsum_rows, take=_take)
        _, vjp = jax.vjp(chunk, xc, z_ref[...], dt_ref[...], dtb_ref[...], alog_ref[...], dvec_ref[...], gn_ref[...],
                         stin_ref[0])
        dxc, dz, ddt, gdtb, galog, gdvec, ggn, dst = vjp((dyb_ref[...], dst_ref[...]))
        dst_ref[...] = dst
        dz_ref[...] = dz
        ddt_ref[...] = ddt
        gdtb_ref[...] += gdtb
        galog_ref[...] += galog
        gdvec_ref[...] += gdvec
        ggn_ref[...] += ggn
        dxc_next = dxc_next_ref[...]
        dxbc = w[SSD_CONV - 1:SSD_CONV, :] * dxc
        gws = []
        for k in range(SSD_CONV - 1):
            j = SSD_CONV - 1 - k
            dxbc = dxbc + w[k:k + 1, :] * _shift_ahead(dxc, dxc_next, j)
            gws.append(jnp.sum(dxc * _shift_back(cur, prev, j), axis=0, keepdims=True))
        gws.append(jnp.sum(dxc * cur, axis=0, keepdims=True))
        dxbc_ref[...] = dxbc
        gw_ref[...] += jnp.concatenate(gws, axis=0)
        gb_ref[...] += jnp.sum(dxc, axis=0, keepdims=True)
        dxc_next_ref[...] = dxc

    rev = functools.partial(_rows, n_tiles=nc)
    return _call(
        body, name, (nc,),
        [rev(tc, D_MODEL), rev(tc, SSD_CONV_DIM),
         pl.BlockSpec((tc, SSD_CONV_DIM), lambda i: (jnp.maximum(nc - 2 - i, 0), 0)),
         rev(tc, D_MODEL), rev(tc, LANE), pl.BlockSpec((1, st_rows, SSD_STATE), lambda i: (nc - 1 - i, 0, 0)),
         _const((SSD_CONV, SSD_CONV_DIM)), _const((1, SSD_CONV_DIM)), _const((1, LANE)), _const((1, LANE)),
         _const((1, LANE)), _const((1, D_MODEL))],
        [rev(tc, SSD_CONV_DIM), rev(tc, D_MODEL), rev(tc, LANE), _full((SSD_CONV, SSD_CONV_DIM)), _full((1, SSD_CONV_DIM)),
         _full((1, LANE)), _full((1, LANE)), _full((1, LANE)), _full((1, D_MODEL))],
        [_sds((n_tok, SSD_CONV_DIM)), _sds((n_tok, D_MODEL)), _sds((n_tok, LANE)), _sds((SSD_CONV, SSD_CONV_DIM)),
         _sds((1, SSD_CONV_DIM)), _sds((1, LANE)), _sds((1, LANE)), _sds((1, LANE)), _sds((1, D_MODEL))],
        scratch=[pltpu.VMEM((st_rows, SSD_STATE), F32), pltpu.VMEM((tc, SSD_CONV_DIM), F32)],
    )(dyb, xbc, xbc, z, dt, stin, conv_w, conv_b, dtb, alog, dvec, gn)


@jax.custom_vjp
def _expand_cols(x, e):
    return _nn_f32(x, e)


_expand_cols.defvjp(
    lambda x, e: (_nn_f32(x, e), e),
    lambda e, g: (lax.dot_general(g, e, (((1,), (1,)), ((), ())), precision=lax.Precision.HIGHEST,
                                  preferred_element_type=F32), jnp.zeros_like(e)))


def _s5_discretize(lam_re, lam_im, log_step, b_re, b_im, expand):
    step = jnp.exp(log_step)
    mag = jnp.exp(lam_re * step)
    ang = lam_im * step
    a_r = mag * jnp.cos(ang)
    a_i = mag * jnp.sin(ang)
    den = lam_re * lam_re + lam_im * lam_im
    n_r = a_r - 1.0
    coef_r = _expand_cols((n_r * lam_re + a_i * lam_im) / den, expand)
    coef_i = _expand_cols((a_i * lam_re - n_r * lam_im) / den, expand)
    return a_r, a_i, coef_r * b_re - coef_i * b_im, coef_r * b_im + coef_i * b_re


def _expand_matrix():
    p = lax.broadcasted_iota(jnp.int32, (S5_STATE, S5_STATE * S5_GROUP), 0)
    c = lax.broadcasted_iota(jnp.int32, (S5_STATE, S5_STATE * S5_GROUP), 1)
    return (c // S5_GROUP == p).astype(F32)


def _s5_discretize_fwd(lam_re, lam_im, log_step, b_re, b_im, name):
    def body(lr_ref, li_ref, ls_ref, br_ref, bi_ref, ar_ref, ai_ref, bbr_ref, bbi_ref):
        outs = _s5_discretize(lr_ref[...], li_ref[...], ls_ref[...], br_ref[...], bi_ref[...], _expand_matrix())
        for r, o in zip((ar_ref, ai_ref, bbr_ref, bbi_ref), outs):
            r[...] = o

    sq, wide = (S5_GROUPS, S5_STATE), (S5_GROUPS, S5_STATE * S5_GROUP)
    return _call(body, name, (1,), [_full(sq), _full(sq), _full((S5_GROUPS, 1)), _full(wide), _full(wide)],
                 [_full(sq), _full(sq), _full(wide), _full(wide)], [_sds(sq), _sds(sq), _sds(wide), _sds(wide)],
                 )(lam_re, lam_im, log_step, b_re, b_im)


def _s5_discretize_bwd(lam_re, lam_im, log_step, b_re, b_im, g_ar8, g_ai8, g_bbr, g_bbi, name):
    def body(lr_ref, li_ref, ls_ref, br_ref, bi_ref, gar_ref, gai_ref, gbbr_ref, gbbi_ref,
             glr_ref, gli_ref, gls_ref, gbr_ref, gbi_ref):
        _, vjp = jax.vjp(functools.partial(_s5_discretize, expand=_expand_matrix()),
                         lr_ref[...], li_ref[...], ls_ref[...], br_ref[...], bi_ref[...])
        grads = vjp((jnp.sum(gar_ref[...], axis=0), jnp.sum(gai_ref[...], axis=0), gbbr_ref[...], gbbi_ref[...]))
        for r, g in zip((glr_ref, gli_ref, gls_ref, gbr_ref, gbi_ref), grads):
            r[...] = g

    sq, wide, col = (S5_GROUPS, S5_STATE), (S5_GROUPS, S5_STATE * S5_GROUP), (S5_GROUPS, 1)
    part = (SUBLANE,) + sq
    return _call(body, name, (1,),
                 [_full(sq), _full(sq), _full(col), _full(wide), _full(wide), _full(part), _full(part), _full(wide), _full(wide)],
                 [_full(sq), _full(sq), _full(col), _full(wide), _full(wide)],
                 [_sds(sq), _sds(sq), _sds(col), _sds(wide), _sds(wide)],
                 )(lam_re, lam_im, log_step, b_re, b_im, g_ar8, g_ai8, g_bbr, g_bbi)


GROUPS_PER_SLICE = LANE // S5_GROUP


def _block_diag_b(bb):
    t = bb.reshape(S5_SLICES, GROUPS_PER_SLICE, S5_STATE, S5_GROUP)
    eye = jnp.eye(GROUPS_PER_SLICE, dtype=bb.dtype)
    return jnp.einsum("kgph,gf->kghfp", t, eye).reshape(S5_SLICES, LANE, S5_SLICE_STATES)


def _block_diag_b_inv(m):
    t = m.reshape(S5_SLICES, GROUPS_PER_SLICE, S5_GROUP, GROUPS_PER_SLICE, S5_STATE)
    return jnp.einsum("kghgp->kgph", t).reshape(S5_GROUPS, S5_STATE * S5_GROUP)


def _block_diag_c(c):
    t = c.reshape(S5_SLICES, GROUPS_PER_SLICE, S5_GROUP, S5_STATE)
    eye = jnp.eye(GROUPS_PER_SLICE, dtype=c.dtype)
    return jnp.einsum("kghp,gf->kgpfh", t, eye).reshape(S5_SLICES, S5_SLICE_STATES, LANE)


def _block_diag_c_inv(m):
    t = m.reshape(S5_SLICES, GROUPS_PER_SLICE, S5_STATE, GROUPS_PER_SLICE, S5_GROUP)
    return jnp.einsum("kgpgh->kghp", t).reshape(S5_GROUPS, S5_GROUP, S5_STATE)


def _pad_lanes(v):
    return jnp.pad(v.reshape(1, -1), ((0, 0), (0, LANE - v.shape[0])))


def _prepare_layer(w, blk, i, after):
    p = {}
    p["wu"], p["wz"], p["wx"], p["wd"] = _w_in_split(blk["w_in"], after, name=f"w_in_split_{i}")
    p["nm"] = w["norm_mix"][i].reshape(1, D_MODEL)
    p["lam_re"], p["lam_im"] = w["s5_lam_re"][i], w["s5_lam_im"][i]
    p["log_step"] = w["s5_log_step"][i].reshape(S5_GROUPS, 1)
    p["b_re"] = w["s5_b_re"][i].reshape(S5_GROUPS, S5_STATE * S5_GROUP)
    p["b_im"] = w["s5_b_im"][i].reshape(S5_GROUPS, S5_STATE * S5_GROUP)
    a_r, a_i, bb_r, bb_i = _s5_discretize_fwd(p["lam_re"], p["lam_im"], p["log_step"], p["b_re"], p["b_im"],
                                              name=f"s5_discretize_{i}")
    p["a_r"], p["a_i"] = a_r.reshape(1, S5_LANES), a_i.reshape(1, S5_LANES)
    p["bdb"] = jnp.concatenate([_block_diag_b(bb_r), _block_diag_b(bb_i)], axis=2).astype(BF16)
    p["bcr"] = _block_diag_c(w["s5_c_re"][i]).astype(BF16)
    p["bci"] = _block_diag_c(w["s5_c_im"][i]).astype(BF16)
    p["dsk"] = w["s5_d"][i].reshape(1, D_MODEL)
    p["wglu"] = blk["s5_w_glu"].reshape(D_MODEL, D_MODEL)
    p["bglu"] = w["s5_b_glu"][i].reshape(1, D_MODEL)
    p["sn"] = w["s5_norm"][i].reshape(1, D_MODEL)
    p["conv_w"] = blk["ssd_conv_w"]
    p["conv_b"] = w["ssd_conv_b"][i].reshape(1, SSD_CONV_DIM)
    p["dtb"] = _pad_lanes(w["ssd_dt_bias"][i])
    p["alog"] = _pad_lanes(w["ssd_a_log"][i])
    p["dvec"] = _pad_lanes(w["ssd_d"][i])
    p["gn"] = w["ssd_norm"][i].reshape(1, D_MODEL)
    p["wo"] = blk["w_out"].reshape(2 * D_MODEL, D_MODEL)
    p["nf"] = w["norm_ffn"][i].reshape(1, D_MODEL)
    p["wg"], p["wup"], p["wdn"] = (blk[n].reshape(FFN_PAD, D_MODEL) for n in ("w_gate", "w_up", "w_down"))
    return p


def _layer_fwd(x0, p, i):
    u, z, xbc, dt = _inproj_fwd(x0, p["nm"], p["wu"], p["wz"], p["wx"], p["wd"], name=f"inproj_fwd_{i}")
    ya, xr, xi, v = _s5_fwd(u, p["a_r"], p["a_i"], p["bdb"], p["bcr"], p["bci"], p["dsk"], p["wglu"], p["bglu"], p["sn"],
                            name=f"s5_fwd_{i}")
    yb, stin = _ssd_fwd(xbc, z, dt, p["conv_w"], p["conv_b"], p["dtb"], p["alog"], p["dvec"], p["gn"], name=f"ssd_fwd_{i}")
    x1, x2 = _mix_ffn_fwd(x0, ya, yb, p["wo"], p["nf"], p["wg"], p["wup"], p["wdn"], name=f"mix_ffn_fwd_{i}")
    return x2, dict(x0=x0, u=u, z=z, xbc=xbc, dt=dt, xr=xr, xi=xi, v=v, stin=stin, ya=ya, yb=yb, x1=x1)


def _layer_bwd(dx2, s, p, i, after):
    g = {}
    dx1, dya, dyb, h2, act, dgt, dup, g_nf = _mix_ffn_bwd(s["x1"], dx2, p["wo"], p["nf"], p["wg"], p["wup"], p["wdn"],
                                                          after, name=f"mix_ffn_bwd_{i}")
    g["norm_ffn"] = g_nf.reshape(D_MODEL)
    g["w_down"] = _matmul_tn_lhs_blocks(act, dx2, FFN_BLOCK_PAD, FFN_BLOCK, name=f"grad_w_down_{i}")
    g["w_gate"] = _matmul_tn_lhs_blocks(dgt, h2, FFN_BLOCK_PAD, FFN_BLOCK, name=f"grad_w_gate_{i}")
    g["w_up"] = _matmul_tn_lhs_blocks(dup, h2, FFN_BLOCK_PAD, FFN_BLOCK, name=f"grad_w_up_{i}")
    g["w_out"] = _matmul_tn_pair(s["ya"], s["yb"], dx1, name=f"grad_w_out_{i}").reshape(N_DEV, 2 * D_MODEL // N_DEV, D_MODEL)

    (du, gg, dq, g_bdb, g_bcr, g_bci, g_ar8, g_ai8, g_d, g_bglu, g_sn) = _s5_bwd(
        dya, s["v"], s["u"], s["xr"], s["xi"], p["a_r"], p["a_i"], p["bdb"], p["bcr"], p["bci"], p["dsk"], p["wglu"],
        p["bglu"], p["sn"], name=f"s5_bwd_{i}")
    g["s5_w_glu"] = _matmul_tn(gg, dq, name=f"grad_w_glu_{i}").reshape(N_DEV, D_MODEL // N_DEV, D_MODEL)
    g["s5_d"], g["s5_b_glu"], g["s5_norm"] = g_d.reshape(D_MODEL), g_bglu.reshape(D_MODEL), g_sn.reshape(D_MODEL)
    g["s5_c_re"], g["s5_c_im"] = _block_diag_c_inv(g_bcr), _block_diag_c_inv(g_bci)
    sq = (SUBLANE, S5_GROUPS, S5_STATE)
    g_lr, g_li, g_ls, g_br, g_bi = _s5_discretize_bwd(
        p["lam_re"], p["lam_im"], p["log_step"], p["b_re"], p["b_im"], g_ar8.reshape(sq), g_ai8.reshape(sq),
        _block_diag_b_inv(g_bdb[:, :, :S5_SLICE_STATES]), _block_diag_b_inv(g_bdb[:, :, S5_SLICE_STATES:]),
        name=f"s5_discretize_bwd_{i}")
    g["s5_lam_re"], g["s5_lam_im"], g["s5_log_step"] = g_lr, g_li, g_ls.reshape(S5_GROUPS)
    b_shape = (S5_GROUPS, S5_STATE, S5_GROUP)
    g["s5_b_re"], g["s5_b_im"] = g_br.reshape(b_shape), g_bi.reshape(b_shape)

    dxbc, dz, ddt, g_cw, g_cb, g_dtb, g_alog, g_dvec, g_gn = _ssd_bwd(
        dyb, s["xbc"], s["z"], s["dt"], s["stin"], p["conv_w"], p["conv_b"], p["dtb"], p["alog"], p["dvec"], p["gn"],
        name=f"ssd_bwd_{i}")
    g["ssd_conv_w"] = jnp.moveaxis(g_cw.reshape(SSD_CONV, N_DEV, SSD_CONV_DIM // N_DEV), 1, 0)
    g["ssd_conv_b"] = g_cb.reshape(SSD_CONV_DIM)
    g["ssd_dt_bias"], g["ssd_a_log"], g["ssd_d"] = g_dtb[0, :SSD_HEADS], g_alog[0, :SSD_HEADS], g_dvec[0, :SSD_HEADS]
    g["ssd_norm"] = g_gn.reshape(D_MODEL)

    dx0, h, g_nm = _inproj_bwd(s["x0"], p["nm"], du, dz, dxbc, ddt, dx1, p["wu"], p["wz"], p["wx"], p["wd"],
                               name=f"inproj_bwd_{i}")
    g["norm_mix"] = g_nm.reshape(D_MODEL)
    g["w_in"] = _w_in_grad_blocks(
        _matmul_tn(h, du, name=f"grad_w_in_u_{i}"), _matmul_tn(h, dz, name=f"grad_w_in_z_{i}"),
        _matmul_tn(h, dxbc, name=f"grad_w_in_xbc_{i}"), _matmul_tn(h, ddt, name=f"grad_w_in_dt_{i}"),
        name=f"grad_w_in_blocks_{i}")
    return dx0, g


def _example_step(x, target, w, blks):
    prepared = [_prepare_layer(w, blks[i], i, x) for i in range(DEPTH)]
    saved = []
    h = x
    for i in range(DEPTH):
        h, s = _layer_fwd(h, prepared[i], i)
        saved.append(s)
    loss, dh, g_final = _loss_head(h, w["norm_final"].reshape(1, D_MODEL), target, name="loss_head")
    layer_grads = [None] * DEPTH
    for i in reversed(range(DEPTH)):
        dh, layer_grads[i] = _layer_bwd(dh, saved[i], prepared[i], i, x)
    return loss, dh, layer_grads, g_final.reshape(D_MODEL)


def _mesh_position():
    return lax.axis_index("x"), lax.axis_index("y"), lax.axis_index("c")


def _peer(pos, k):
    x, y, c = pos
    px = 1 - x if k & 4 else x
    py = 1 - y if k & 2 else y
    pc = 1 - c if k & 1 else c
    return (px, py, pc), 4 * px + 2 * py + pc


HBM = pl.BlockSpec(memory_space=pl.ANY)


def _run_copies(local, remote):
    for cp in local + remote:
        cp.start()
    for cp in remote:
        cp.wait_recv()
    for cp in remote:
        cp.wait_send()
    for cp in local:
        cp.wait()


def _comm_scratch(n_units):
    return [pltpu.SemaphoreType.DMA((n_units, N_DEV - 1)), pltpu.SemaphoreType.DMA((n_units, N_DEV - 1)),
            pltpu.SemaphoreType.DMA((n_units,))]


def _gather_blocks(arrays, layered, name):
    units, out_shapes = [], []
    for j, (a, lay) in enumerate(zip(arrays, layered)):
        for layer in (range(a.shape[0]) if lay else (None,)):
            units.append((j, layer, len(out_shapes)))
            out_shapes.append(_sds((N_DEV,) + (a.shape[1:] if lay else a.shape), a.dtype))
    n_in = len(arrays)
    other_chips = (4, 2, 6)

    def body(*refs):
        ins, outs = refs[:n_in], refs[n_in:n_in + len(out_shapes)]
        send_sems, recv_sems, local_sems = refs[n_in + len(out_shapes):]
        pos = _mesh_position()
        me = 4 * pos[0] + 2 * pos[1] + pos[2]
        sibling, _ = _peer(pos, 1)
        local, own, passed = [], [], []
        for u, (j, layer, o) in enumerate(units):
            src = ins[j] if layer is None else ins[j].at[layer]
            local.append(pltpu.make_async_copy(src, outs[o].at[me], local_sems.at[u]))

            def copy(sem, src_ref, slot, to, u=u, o=o):
                return pltpu.make_async_remote_copy(
                    src_ref=src_ref, dst_ref=outs[o].at[slot], send_sem=send_sems.at[u, sem], recv_sem=recv_sems.at[u, sem],
                    device_id=to, device_id_type=MESH_ID)

            own.append([copy(0, src, me, sibling)] + [copy(1 + i, src, me, _peer(pos, k)[0]) for i, k in enumerate(other_chips)])
            passed.append([copy(4 + i, outs[o].at[_peer(pos, k)[1]], _peer(pos, k)[1], sibling) for i, k in enumerate(other_chips)])
        for cp in local + [c for unit in own for c in unit]:
            cp.start()
        for u in range(len(units)):
            for i in range(len(other_chips)):
                own[u][1 + i].wait_recv()
                passed[u][i].start()
        for u in range(len(units)):
            own[u][0].wait_recv()
            for cp in passed[u]:
                cp.wait_recv()
        for cp in [c for unit in own + passed for c in unit]:
            cp.wait_send()
        for cp in local:
            cp.wait()

    outs = pl.pallas_call(body, name=name, in_specs=[HBM] * n_in, out_specs=[HBM] * len(out_shapes), out_shape=out_shapes,
                          scratch_shapes=_comm_scratch(len(units)))(*arrays)
    grouped = [[] for _ in arrays]
    for j, _, o in units:
        grouped[j].append(outs[o])
    return [tuple(g) for g in grouped]


def _exchange_blocks(entries, name):
    units, flat_in, out_shapes = [], [], []
    for j, entry in enumerate(entries):
        for layer, a in enumerate(entry):
            units.append((len(flat_in), layer, j))
            flat_in.append(a)
        out_shapes.append(_sds((N_DEV, len(entry)) + entry[0].shape[1:], entry[0].dtype))
    n_in = len(flat_in)

    def body(*refs):
        ins, outs = refs[:n_in], refs[n_in:n_in + len(out_shapes)]
        send_sems, recv_sems, local_sems = refs[n_in + len(out_shapes):]
        pos = _mesh_position()
        me = 4 * pos[0] + 2 * pos[1] + pos[2]
        local, remote = [], []
        for u, (i, layer, o) in enumerate(units):
            local.append(pltpu.make_async_copy(ins[i].at[me], outs[o].at[me, layer], local_sems.at[u]))
            for k in range(1, N_DEV):
                peer, peer_index = _peer(pos, k)
                remote.append(pltpu.make_async_remote_copy(
                    src_ref=ins[i].at[peer_index], dst_ref=outs[o].at[me, layer], send_sem=send_sems.at[u, k - 1],
                    recv_sem=recv_sems.at[u, k - 1], device_id=peer, device_id_type=MESH_ID))
        _run_copies(local, remote)

    return pl.pallas_call(body, name=name, in_specs=[HBM] * n_in, out_specs=[HBM] * len(out_shapes), out_shape=out_shapes,
                          scratch_shapes=_comm_scratch(len(units)))(*flat_in)


SEM = pl.BlockSpec(memory_space=pltpu.SEMAPHORE)
SIDE_EFFECT = pltpu.SideEffectType.DATAFLOW_SIDE_EFFECTING


def _own_slots(arrays, indexed, me, name):
    lands = []
    for u, a in enumerate(arrays):
        block = a.shape[1:] if indexed else a.shape
        rows, cols = _size(block[:-1]), block[-1]
        tr = _row_tile(rows, cap=512)

        def body(me_ref, src_ref, out_ref):
            out_ref[...] = src_ref[...]

        src_spec = (pl.BlockSpec((None, tr, cols), lambda i, me_ref: (me_ref[0], i, 0)) if indexed
                    else pl.BlockSpec((tr, cols), lambda i, me_ref: (i, 0)))
        land = pl.pallas_call(
            body, name=f"{name}_{u}", out_shape=_sds((N_DEV, rows, cols), a.dtype),
            grid_spec=pltpu.PrefetchScalarGridSpec(
                num_scalar_prefetch=1, grid=(rows // tr,), in_specs=[src_spec],
                out_specs=pl.BlockSpec((None, tr, cols), lambda i, me_ref: (me_ref[0], i, 0))),
        )(me, a.reshape((N_DEV, rows, cols) if indexed else (rows, cols)))
        lands.append(land.reshape((N_DEV,) + block))
    return lands


def _split_copies(srcs, lands, send_sems, recv_sems, indexed):
    pos = _mesh_position()
    me = 4 * pos[0] + 2 * pos[1] + pos[2]
    copies = []
    for u, (src, land) in enumerate(zip(srcs, lands)):
        for k in range(1, N_DEV):
            peer, peer_index = _peer(pos, k)
            copies.append(pltpu.make_async_remote_copy(
                src_ref=src.at[peer_index] if indexed else src, dst_ref=land.at[me],
                send_sem=send_sems.at[u * (N_DEV - 1) + k - 1], recv_sem=recv_sems.at[u * (N_DEV - 1) + k - 1],
                device_id=peer, device_id_type=MESH_ID))
    return copies


def _exchange_start(arrays, lands, indexed, name):
    n = len(arrays)

    def body(*refs):
        srcs, zones = refs[:n], refs[n:2 * n]
        send_sems, recv_sems = refs[2 * n], refs[2 * n + 1]
        token = refs[-1]
        for cp in _split_copies(srcs, zones, send_sems, recv_sems, indexed):
            cp.start()
        token[...] = jnp.zeros_like(token)

    sem_shape = pltpu.SemaphoreType.DMA((n * (N_DEV - 1),))
    outs = pl.pallas_call(
        body, name=name, in_specs=[HBM] * (2 * n),
        out_specs=[SEM, SEM] + [HBM] * (2 * n) + [pl.BlockSpec(memory_space=pltpu.VMEM)],
        out_shape=[sem_shape, sem_shape] + [pltpu.HBM(a.shape, a.dtype) for a in list(arrays) + list(lands)]
        + [_sds((SUBLANE, LANE))],
        input_output_aliases={i: 2 + i for i in range(2 * n)},
        compiler_params=pltpu.CompilerParams(has_side_effects=SIDE_EFFECT),
    )(*[pltpu.with_memory_space_constraint(a, pltpu.HBM) for a in list(arrays) + list(lands)])
    return outs[0], outs[1], outs[2:2 + n], outs[2 + n:2 + 2 * n], outs[-1]


def _exchange_wait(send_sems, recv_sems, arrays, lands, after, indexed, name):
    n = len(arrays)

    def body(*refs):
        srcs, zones = refs[:n], refs[n:2 * n]
        s_sems, r_sems = refs[2 * n], refs[2 * n + 1]
        for cp in _split_copies(srcs, zones, s_sems, r_sems, indexed):
            cp.wait_send()
            cp.wait_recv()

    outs = pl.pallas_call(
        body, name=name, in_specs=[HBM] * (2 * n) + [SEM, SEM, HBM],
        out_specs=[HBM] * (2 * n), out_shape=[pltpu.HBM(a.shape, a.dtype) for a in list(arrays) + list(lands)],
        input_output_aliases={i: i for i in range(2 * n)},
        compiler_params=pltpu.CompilerParams(has_side_effects=SIDE_EFFECT),
    )(*arrays, *lands, send_sems, recv_sems, after)
    return outs[n:]


SUM_TILE = 512


def _adamw(w, g, m, v):
    m = ADAM_B1 * m + (1.0 - ADAM_B1) * g
    v = ADAM_B2 * v + (1.0 - ADAM_B2) * (g * g)
    m_hat = m / (1.0 - ADAM_B1 ** ADAM_STEP)
    v_hat = v / (1.0 - ADAM_B2 ** ADAM_STEP)
    return -ADAM_LR * (m_hat / (jnp.sqrt(v_hat) + ADAM_EPS) + ADAM_WD * w), m, v


def _sum_adamw(recv, w, m, v, layer, others, name):
    _, rows, cols = w.shape
    tr = _row_tile(rows, cap=256)

    def body(r_ref, w_ref, m_ref, v_ref, *rest):
        g_ref, d_ref, mo_ref, vo_ref = rest[-4:]
        g = r_ref[0].astype(F32)
        for j in range(1, N_DEV):
            g = g + r_ref[j].astype(F32)
        g_ref[...] = g
        d_ref[...], mo_ref[...], vo_ref[...] = _adamw(w_ref[...], g, m_ref[...], v_ref[...])

    blk = pl.BlockSpec((None, tr, cols), lambda i: (layer, i, 0))
    carried = list(others) if others is not None else []
    return pl.pallas_call(
        body, name=name, grid=(rows // tr,),
        in_specs=[pl.BlockSpec((N_DEV, tr, cols), lambda i: (0, i, 0)), blk, blk, blk] + [HBM] * len(carried),
        out_specs=[blk] * 4, out_shape=[_sds(w.shape)] * 4,
        input_output_aliases={4 + k: k for k in range(len(carried))},
        compiler_params=pltpu.CompilerParams(dimension_semantics=("arbitrary",), vmem_limit_bytes=VMEM_LIMIT),
    )(recv, w, m, v, *carried)


def _sum_senders(recv, name):
    _, rows, cols = recv.shape
    tr = _row_tile(rows, cap=256)

    def body(r_ref, g_ref):
        g = r_ref[0].astype(F32)
        for j in range(1, N_DEV):
            g = g + r_ref[j].astype(F32)
        g_ref[...] = g

    return _call(body, name, (rows // tr,), [pl.BlockSpec((N_DEV, tr, cols), lambda i: (0, i, 0))], [_rows(tr, cols)],
                 [_sds((rows, cols))])(recv)[0]


def _adamw_blocks(g, w, m, v, name):
    n_lay, rows, cols = w.shape
    tr = _row_tile(rows, cap=256)

    def body(g_ref, w_ref, m_ref, v_ref, d_ref, mo_ref, vo_ref):
        d_ref[...], mo_ref[...], vo_ref[...] = _adamw(w_ref[...], g_ref[...], m_ref[...], v_ref[...])

    blk = pl.BlockSpec((None, tr, cols), lambda l, i: (l, i, 0))
    return pl.pallas_call(
        body, name=name, grid=(n_lay, rows // tr), in_specs=[blk] * 4, out_specs=[blk] * 3, out_shape=[_sds(w.shape)] * 3,
        compiler_params=pltpu.CompilerParams(dimension_semantics=("arbitrary", "arbitrary"), vmem_limit_bytes=VMEM_LIMIT),
    )(g, w, m, v)


def _sum_slots(recv, name):
    rows = recv.shape[1]

    def body(r_ref, g_ref):
        g = r_ref[0].astype(F32)
        for j in range(1, N_DEV):
            g = g + r_ref[j].astype(F32)
        g_ref[...] = g

    return _call(body, name, (1,), [_full(recv.shape)], [_full((rows, LANE))], [_sds((rows, LANE))])(recv)[0]


def _adamw_rows(g, w, m, v, name):
    rows = w.shape[0]
    tr = _row_tile(rows)

    def body(g_ref, w_ref, m_ref, v_ref, d_ref, mo_ref, vo_ref):
        d_ref[...], mo_ref[...], vo_ref[...] = _adamw(w_ref[...], g_ref[...], m_ref[...], v_ref[...])

    flat = _rows(tr, LANE)
    return _call(body, name, (rows // tr,), [flat] * 4, [flat] * 3, [_sds((rows, LANE))] * 3)(g, w, m, v)


def _row_tile(rows, cap=1024):
    if rows % SUBLANE:
        return rows
    best = SUBLANE
    for t in range(SUBLANE, cap + 1, SUBLANE):
        if rows % t == 0:
            best = t
    return best


BIG = (("w_in", (DEPTH, D_MODEL, IN_PROJ // N_DEV), 2),
       ("s5_w_glu", (DEPTH, D_MODEL // N_DEV, D_MODEL), 1),
       ("ssd_conv_w", (DEPTH, SSD_CONV, SSD_CONV_DIM // N_DEV), 2),
       ("w_out", (DEPTH, 2 * D_MODEL // N_DEV, D_MODEL), 1),
       ("w_gate", (DEPTH, D_MODEL, FFN_HIDDEN // N_DEV), 2),
       ("w_up", (DEPTH, D_MODEL, FFN_HIDDEN // N_DEV), 2),
       ("w_down", (DEPTH, FFN_HIDDEN // N_DEV, D_MODEL), 1))
SMALL = (("norm_mix", (DEPTH, D_MODEL)), ("s5_lam_re", (DEPTH, S5_GROUPS, S5_STATE)), ("s5_lam_im", (DEPTH, S5_GROUPS, S5_STATE)),
         ("s5_log_step", (DEPTH, S5_GROUPS)), ("s5_b_re", (DEPTH, S5_GROUPS, S5_STATE, S5_GROUP)),
         ("s5_b_im", (DEPTH, S5_GROUPS, S5_STATE, S5_GROUP)), ("s5_c_re", (DEPTH, S5_GROUPS, S5_GROUP, S5_STATE)),
         ("s5_c_im", (DEPTH, S5_GROUPS, S5_GROUP, S5_STATE)), ("s5_d", (DEPTH, D_MODEL)), ("s5_b_glu", (DEPTH, D_MODEL)),
         ("s5_norm", (DEPTH, D_MODEL)), ("ssd_conv_b", (DEPTH, SSD_CONV_DIM)), ("ssd_dt_bias", (DEPTH, SSD_HEADS)),
         ("ssd_a_log", (DEPTH, SSD_HEADS)), ("ssd_d", (DEPTH, SSD_HEADS)), ("ssd_norm", (DEPTH, D_MODEL)),
         ("norm_ffn", (DEPTH, D_MODEL)), ("norm_final", (D_MODEL,)))
WEIGHT_ORDER = ("norm_mix", "w_in", "s5_lam_re", "s5_lam_im", "s5_log_step", "s5_b_re", "s5_b_im", "s5_c_re", "s5_c_im", "s5_d",
                "s5_w_glu", "s5_b_glu", "s5_norm", "ssd_conv_w", "ssd_conv_b", "ssd_dt_bias", "ssd_a_log", "ssd_d", "ssd_norm",
                "w_out", "norm_ffn", "w_gate", "w_up", "w_down", "norm_final")


def _size(shape):
    n = 1
    for s in shape:
        n *= s
    return n


def _round_up(n, m):
    return -(-n // m) * m


SMALL_SIZE = sum(_size(s) for _, s in SMALL)
SMALL_ROWS = _round_up(-(-SMALL_SIZE // (N_DEV * LANE)), SUBLANE)


def _pack(parts, rows, dtype):
    flat = jnp.concatenate([p.reshape(-1).astype(dtype) for p in parts])
    return jnp.pad(flat, (0, rows * LANE - flat.shape[0])).reshape(rows, LANE)


def _unpack(flat, specs):
    out, off = {}, 0
    flat = flat.reshape(-1)
    for name, shape in specs:
        out[name] = flat[off:off + _size(shape)].reshape(shape)
        off += _size(shape)
    return out


def kernel(x, norm_mix, w_in, s5_lam_re, s5_lam_im, s5_log_step, s5_b_re, s5_b_im, s5_c_re, s5_c_im, s5_d, s5_w_glu, s5_b_glu, s5_norm, ssd_conv_w, ssd_conv_b, ssd_dt_bias, ssd_a_log, ssd_d, ssd_norm, w_out, norm_ffn, w_gate, w_up, w_down, norm_final, loss_target, m_norm_mix, m_w_in, m_s5_lam_re, m_s5_lam_im, m_s5_log_step, m_s5_b_re, m_s5_b_im, m_s5_c_re, m_s5_c_im, m_s5_d, m_s5_w_glu, m_s5_b_glu, m_s5_norm, m_ssd_conv_w, m_ssd_conv_b, m_ssd_dt_bias, m_ssd_a_log, m_ssd_d, m_ssd_norm, m_w_out, m_norm_ffn, m_w_gate, m_w_up, m_w_down, m_norm_final, v_norm_mix, v_w_in, v_s5_lam_re, v_s5_lam_im, v_s5_log_step, v_s5_b_re, v_s5_b_im, v_s5_c_re, v_s5_c_im, v_s5_d, v_s5_w_glu, v_s5_b_glu, v_s5_norm, v_ssd_conv_w, v_ssd_conv_b, v_ssd_dt_bias, v_ssd_a_log, v_ssd_d, v_ssd_norm, v_w_out, v_norm_ffn, v_w_gate, v_w_up, v_w_down, v_norm_final):
    given = dict(locals())
    w = {n: given[n] for n in WEIGHT_ORDER}
    m = {n: given["m_" + n] for n in WEIGHT_ORDER}
    v = {n: given["v_" + n] for n in WEIGHT_ORDER}
    big_names = tuple(n for n, _, _ in BIG)
    matmul_names = tuple(n for n in big_names if n != "ssd_conv_w")

    conv_hi = w["ssd_conv_w"].astype(BF16)
    conv_lo = (w["ssd_conv_w"] - conv_hi.astype(F32)).astype(BF16)
    row_pad = ((0, 0), (0, FFN_BLOCK_PAD - FFN_BLOCK), (0, 0))
    as_rows = {"w_gate": jnp.swapaxes(w["w_gate"], 1, 2), "w_up": jnp.swapaxes(w["w_up"], 1, 2), "w_down": w["w_down"]}
    to_send = [jnp.pad(as_rows[n].astype(BF16), row_pad) if n in as_rows else w[n].astype(BF16) for n in matmul_names]

    def layer_blocks(i):
        return [a[i] for a in to_send] + [jnp.stack([conv_hi[i], conv_lo[i]])]

    def as_layer_weights(gathered):
        blk = dict(zip(matmul_names, gathered))
        pair = gathered[-1].astype(F32)
        blk["ssd_conv_w"] = jnp.moveaxis(pair[:, 0] + pair[:, 1], 0, 1).reshape(SSD_CONV, SSD_CONV_DIM)
        return blk

    gathered0 = [g[0] for g in _gather_blocks(layer_blocks(0), [False] * (len(matmul_names) + 1), name="gather_weights_0")]
    blocks1 = layer_blocks(1)
    me = (4 * lax.axis_index("x") + 2 * lax.axis_index("y") + lax.axis_index("c")).astype(jnp.int32).reshape(1)
    sems1 = _exchange_start(blocks1, _own_slots(blocks1, False, me, name="gather_own_1"), False, name="gather_start_1")
    prepared = [_prepare_layer(w, as_layer_weights(gathered0), 0, sems1[-1]), None]
    saved = [None, None]
    h, saved[0] = _layer_fwd(x[0], prepared[0], 0)
    gathered1 = _exchange_wait(*sems1[:4], h, False, name="gather_wait_1")
    prepared[1] = _prepare_layer(w, as_layer_weights(gathered1), 1, sems1[-1])
    h, saved[1] = _layer_fwd(h, prepared[1], 1)
    loss, dh, g_final = _loss_head(h, w["norm_final"].reshape(1, D_MODEL), loss_target[0], name="loss_head")

    layer_grads = [None, None]
    dh, layer_grads[1] = _layer_bwd(dh, saved[1], prepared[1], 1, sems1[-1])
    slots1 = [layer_grads[1][n] for n in big_names]
    sems2 = _exchange_start(slots1, _own_slots(slots1, True, me, name="exchange_own_1"), True, name="exchange_start_1")
    grad_x, layer_grads[0] = _layer_bwd(dh, saved[0], prepared[0], 0, sems2[-1])

    small = jnp.concatenate([g_final.reshape(-1) if n == "norm_final"
                             else jnp.stack([layer_grads[i][n] for i in range(DEPTH)]).reshape(-1) for n, _ in SMALL])
    small_slots = jnp.pad(small, (0, N_DEV * SMALL_ROWS * LANE - small.shape[0])).reshape(N_DEV, SMALL_ROWS, LANE)
    received0 = _exchange_blocks([(layer_grads[0][n],) for n in big_names] + [(small_slots,)], name="exchange_gradients_0")
    received1 = _exchange_wait(*sems2[:4], grad_x, True, name="exchange_wait_1")

    results = {}
    for j, n in enumerate(big_names):
        recv = (received0[j].reshape(received0[j].shape[:1] + received0[j].shape[2:]), received1[j])
        if n in ("w_gate", "w_up"):
            g = jnp.stack([jnp.swapaxes(_sum_senders(recv[i], name=f"sum_{n}_{i}"), 0, 1) for i in range(DEPTH)])
            results[n] = [g, *_adamw_blocks(g, w[n], m[n], v[n], name=f"adamw_{n}")]
        else:
            first = _sum_adamw(recv[1], w[n], m[n], v[n], 1, None, name=f"sum_adamw_{n}_1")
            results[n] = _sum_adamw(recv[0], w[n], m[n], v[n], 0, first, name=f"sum_adamw_{n}_0")
    g_part = _sum_slots(received0[-1].reshape(N_DEV, SMALL_ROWS, LANE), name="sum_replicated")
    g_small = _gather_blocks([g_part], [False], name="gather_replicated")[0][0].reshape(N_DEV * SMALL_ROWS, LANE)
    small_rows = N_DEV * SMALL_ROWS
    d_small, m_small, v_small = _adamw_rows(
        g_small, _pack([w[n] for n, _ in SMALL], small_rows, F32), _pack([m[n] for n, _ in SMALL], small_rows, F32),
        _pack([v[n] for n, _ in SMALL], small_rows, F32), name="adamw_replicated")
    for k, packed in enumerate((g_small, d_small, m_small, v_small)):
        for n, arr in _unpack(packed, SMALL).items():
            results.setdefault(n, [None] * 4)[k] = arr

    outs = [results[n][k] for k in range(4) for n in WEIGHT_ORDER]
    total_loss = lax.psum(loss[0, 0], ("x", "y", "c"))
    return (total_loss, grad_x[None], *outs)
```

```python
import functools

import jax
import jax.numpy as jnp
from jax import lax
from jax.experimental import pallas as pl
from jax.experimental.pallas import tpu as pltpu

F32 = jnp.float32
BF16 = jnp.bfloat16
MESH_ID = pl.DeviceIdType.MESH

N_DEV = 8
DEPTH = 2
D_MODEL = 1024
S5_GROUPS = 64
S5_GROUP = 16
S5_STATE = 64
S5_LANES = S5_GROUPS * S5_STATE
SSD_HEADS = 16
SSD_HEAD_DIM = 64
SSD_STATE = 128
SSD_CHUNK = 128
SSD_CONV = 4
SSD_CONV_DIM = 1536
FFN_HIDDEN = 2816
IN_PROJ = 3600
EPS = 1e-6
LANE = 128
SUBLANE = 8
VMEM_LIMIT = 56 * 1024 * 1024

ADAM_LR = 0.001
ADAM_B1 = 0.9
ADAM_B2 = 0.999
ADAM_EPS = 1e-08
ADAM_WD = 0.01
ADAM_STEP = 10

TOK_TILE = 256
S5_TILE = 128
S5_SEG = S5_TILE // SUBLANE


def _sigmoid(x):
    return jax.nn.sigmoid(x)


def _silu(x):
    return x * _sigmoid(x)


def _gelu(x):
    return 0.5 * x * (1.0 + jnp.tanh(0.7978845608028654 * (x + 0.044715 * (x * x * x))))


def _softplus(x):
    return jnp.maximum(x, 0.0) + jnp.log(1.0 + jnp.exp(-jnp.abs(x)))


def _rms(x, g):
    r = lax.rsqrt(jnp.mean(x * x, axis=-1, keepdims=True) + EPS)
    return x * r * g


def _nn(a, b):
    return lax.dot_general(a.astype(BF16), b.astype(BF16), (((1,), (0,)), ((), ())), preferred_element_type=F32)


def _nt(a, b):
    return lax.dot_general(a.astype(BF16), b.astype(BF16), (((1,), (1,)), ((), ())), preferred_element_type=F32)


def _tn(a, b):
    return lax.dot_general(a.astype(BF16), b.astype(BF16), (((0,), (0,)), ((), ())), preferred_element_type=F32)


def _nn_f32(a, b):
    return lax.dot_general(a, b, (((1,), (0,)), ((), ())), precision=lax.Precision.HIGHEST, preferred_element_type=F32)


def _tn_f32(a, b):
    return lax.dot_general(a, b, (((0,), (0,)), ((), ())), precision=lax.Precision.HIGHEST, preferred_element_type=F32)


@jax.custom_vjp
def _nn_d(a, b):
    return _nn(a, b)


_nn_d.defvjp(lambda a, b: (_nn(a, b), (a, b)), lambda r, g: (_nt(g, r[1]), _tn(r[0], g)))


@jax.custom_vjp
def _nt_d(a, b):
    return _nt(a, b)


_nt_d.defvjp(lambda a, b: (_nt(a, b), (a, b)), lambda r, g: (_nn(g, r[1]), _tn(g, r[0])))


@jax.custom_vjp
def _tn_d(a, b):
    return _tn(a, b)


_tn_d.defvjp(lambda a, b: (_tn(a, b), (a, b)), lambda r, g: (_nt(r[1], g), _nn(r[0], g)))


@jax.custom_vjp
def _cumsum_rows(tri, x):
    return _nn_f32(tri, x)


_cumsum_rows.defvjp(lambda tri, x: (_nn_f32(tri, x), tri), lambda tri, g: (jnp.zeros_like(tri), _tn_f32(tri, g)))


def _full(shape):
    zeros = (0,) * len(shape)
    return pl.BlockSpec(shape, lambda *_: zeros)


def _const(shape):
    zeros = (0,) * len(shape)
    return pl.BlockSpec(shape, lambda *_: zeros, pipeline_mode=pl.Buffered(1))


def _rows(tile, width, n_tiles=None):
    if n_tiles is None:
        return pl.BlockSpec((tile, width), lambda i: (i, 0))
    return pl.BlockSpec((tile, width), lambda i: (n_tiles - 1 - i, 0))


def _call(body, name, grid, in_specs, out_specs, out_shape, scratch=()):
    return pl.pallas_call(
        body, name=name, grid=grid, in_specs=in_specs, out_specs=out_specs, out_shape=out_shape,
        scratch_shapes=list(scratch),
        compiler_params=pltpu.CompilerParams(dimension_semantics=("arbitrary",) * len(grid),
                                             vmem_limit_bytes=VMEM_LIMIT))


def _sds(shape, dtype=F32):
    return jax.ShapeDtypeStruct(shape, dtype)


def _tile_of(n, cap=512):
    if n <= LANE:
        return n
    best = LANE
    for t in range(LANE, cap + 1, LANE):
        if n % t == 0:
            best = t
    return best


def _inproj_fwd(x, nm, wu, wz, wx, wd, name):
    n_tok = x.shape[0]
    tm = TOK_TILE

    def body(x_ref, nm_ref, wu_ref, wz_ref, wx_ref, wd_ref, u_ref, z_ref, xbc_ref, dt_ref):
        h = _rms(x_ref[...], nm_ref[...]).astype(BF16)
        u_ref[...] = _nn(h, wu_ref[...])
        z_ref[...] = _nn(h, wz_ref[...])
        xbc_ref[...] = _nn(h, wx_ref[...])
        dt_ref[...] = _nn(h, wd_ref[...])

    return _call(
        body, name, (n_tok // tm,),
        [_rows(tm, D_MODEL), _const((1, D_MODEL)), _const(wu.shape), _const(wz.shape), _const(wx.shape), _const(wd.shape)],
        [_rows(tm, D_MODEL), _rows(tm, D_MODEL), _rows(tm, SSD_CONV_DIM), _rows(tm, LANE)],
        [_sds((n_tok, D_MODEL)), _sds((n_tok, D_MODEL)), _sds((n_tok, SSD_CONV_DIM)), _sds((n_tok, LANE))],
    )(x, nm, wu, wz, wx, wd)


def _inproj_bwd(x, nm, du, dz, dxbc, ddt, dres, wu, wz, wx, wd, name):
    n_tok = x.shape[0]
    tm = TOK_TILE

    def body(x_ref, nm_ref, du_ref, dz_ref, dxbc_ref, ddt_ref, dres_ref, wu_ref, wz_ref, wx_ref, wd_ref,
             dx_ref, h_ref, dnm_ref):
        dh = (_nt(du_ref[...], wu_ref[...]) + _nt(dz_ref[...], wz_ref[...])
              + _nt(dxbc_ref[...], wx_ref[...]) + _nt(ddt_ref[...], wd_ref[...]))
        h, vjp = jax.vjp(_rms, x_ref[...], nm_ref[...])
        dx, dnm = vjp(dh)
        dx_ref[...] = dres_ref[...] + dx
        h_ref[...] = h.astype(BF16)

        @pl.when(pl.program_id(0) == 0)
        def _():
            dnm_ref[...] = jnp.zeros_like(dnm_ref)

        dnm_ref[...] += dnm

    return _call(
        body, name, (n_tok // tm,),
        [_rows(tm, D_MODEL), _const((1, D_MODEL)), _rows(tm, D_MODEL), _rows(tm, D_MODEL), _rows(tm, SSD_CONV_DIM),
         _rows(tm, LANE), _rows(tm, D_MODEL), _const(wu.shape), _const(wz.shape), _const(wx.shape), _const(wd.shape)],
        [_rows(tm, D_MODEL), _rows(tm, D_MODEL), _full((1, D_MODEL))],
        [_sds((n_tok, D_MODEL)), _sds((n_tok, D_MODEL), BF16), _sds((1, D_MODEL))],
    )(x, nm, du, dz, dxbc, ddt, dres, wu, wz, wx, wd)


def _ffn_act(gt, up):
    return _silu(gt) * up


FFN_BLOCK = FFN_HIDDEN // N_DEV
FFN_BLOCK_PAD = -(-FFN_BLOCK // LANE) * LANE


FFN_PAD = N_DEV * FFN_BLOCK_PAD


def _mix_ffn_fwd(x0, ya, yb, wo, nf, wg, wu, wd, name):
    n_tok = x0.shape[0]
    tm = TOK_TILE

    def body(x0_ref, ya_ref, yb_ref, wo_ref, nf_ref, wg_ref, wu_ref, wd_ref, x1_ref, x2_ref):
        x1 = x0_ref[...] + _nn(ya_ref[...], wo_ref[:D_MODEL, :]) + _nn(yb_ref[...], wo_ref[D_MODEL:, :])
        h = _rms(x1, nf_ref[...]).astype(BF16)
        x1_ref[...] = x1
        x2_ref[...] = x1 + _nn(_ffn_act(_nt(h, wg_ref[...]), _nt(h, wu_ref[...])), wd_ref[...])

    return _call(
        body, name, (n_tok // tm,),
        [_rows(tm, D_MODEL), _rows(tm, D_MODEL), _rows(tm, D_MODEL), _const(wo.shape),
         _const((1, D_MODEL)), _const(wg.shape), _const(wu.shape), _const(wd.shape)],
        [_rows(tm, D_MODEL), _rows(tm, D_MODEL)],
        [_sds((n_tok, D_MODEL)), _sds((n_tok, D_MODEL))],
    )(x0, ya, yb, wo, nf, wg, wu, wd)


def _mix_ffn_bwd(x1, dx2, wo, nf, wg, wu, wd, after, name):
    n_tok = x1.shape[0]
    tm = TOK_TILE
    n_chunks = 3
    hc = FFN_PAD // n_chunks

    def body(x1_ref, dx2_ref, wo_ref, nf_ref, wg_ref, wu_ref, wd_ref, after_ref,
             dx1_ref, dya_ref, dyb_ref, h_ref, a_ref, dgt_ref, dup_ref, dnf_ref):
        dx2 = dx2_ref[...]
        dx2b = dx2.astype(BF16)
        h, rms_vjp = jax.vjp(_rms, x1_ref[...], nf_ref[...])
        hb = h.astype(BF16)
        dh = jnp.zeros_like(h)
        for c in range(n_chunks):
            rows = pl.ds(c * hc, hc)
            a, act_vjp = jax.vjp(_ffn_act, _nt(hb, wg_ref[rows, :]), _nt(hb, wu_ref[rows, :]))
            dgt, dup = act_vjp(_nt(dx2b, wd_ref[rows, :]))
            a_ref[:, c * hc:(c + 1) * hc] = a.astype(BF16)
            dgt_ref[:, c * hc:(c + 1) * hc] = dgt.astype(BF16)
            dup_ref[:, c * hc:(c + 1) * hc] = dup.astype(BF16)
            dh = dh + _nn(dgt, wg_ref[rows, :]) + _nn(dup, wu_ref[rows, :])
        dx, dnf = rms_vjp(dh)
        dx1 = dx2 + dx
        dx1_ref[...] = dx1
        dya_ref[...] = _nt(dx1, wo_ref[:D_MODEL, :])
        dyb_ref[...] = _nt(dx1, wo_ref[D_MODEL:, :])
        h_ref[...] = hb

        @pl.when(pl.program_id(0) == 0)
        def _():
            dnf_ref[...] = jnp.zeros_like(dnf_ref)

        dnf_ref[...] += dnf

    hidden = _rows(tm, FFN_PAD)
    return _call(
        body, name, (n_tok // tm,),
        [_rows(tm, D_MODEL), _rows(tm, D_MODEL), _const(wo.shape), _const((1, D_MODEL)),
         _const(wg.shape), _const(wu.shape), _const(wd.shape), HBM],
        [_rows(tm, D_MODEL), _rows(tm, D_MODEL), _rows(tm, D_MODEL), _rows(tm, D_MODEL), hidden, hidden, hidden,
         _full((1, D_MODEL))],
        [_sds((n_tok, D_MODEL)), _sds((n_tok, D_MODEL)), _sds((n_tok, D_MODEL)), _sds((n_tok, D_MODEL), BF16),
         _sds((n_tok, FFN_PAD), BF16), _sds((n_tok, FFN_PAD), BF16), _sds((n_tok, FFN_PAD), BF16), _sds((1, D_MODEL))],
    )(x1, dx2, wo, nf, wg, wu, wd, after)


def _loss_head(x, nf, target, name):
    n_tok = x.shape[0]
    tm = TOK_TILE

    def loss_of(xv, g, t):
        e = _rms(xv, g) - t
        return 0.5 * jnp.sum(jnp.sum(e * e, axis=-1, keepdims=True) * (1.0 / D_MODEL), axis=0, keepdims=True)

    def body(x_ref, nf_ref, t_ref, loss_ref, dx_ref, dnf_ref):
        loss, vjp = jax.vjp(functools.partial(loss_of, t=t_ref[...]), x_ref[...], nf_ref[...])
        dx, dnf = vjp(jnp.ones_like(loss))
        dx_ref[...] = dx

        @pl.when(pl.program_id(0) == 0)
        def _():
            dnf_ref[...] = jnp.zeros_like(dnf_ref)
            loss_ref[...] = jnp.zeros_like(loss_ref)

        dnf_ref[...] += dnf
        loss_ref[...] += jnp.broadcast_to(loss, loss_ref.shape)

    return _call(
        body, name, (n_tok // tm,),
        [_rows(tm, D_MODEL), _const((1, D_MODEL)), _rows(tm, D_MODEL)],
        [_full((SUBLANE, LANE)), _rows(tm, D_MODEL), _full((1, D_MODEL))],
        [_sds((SUBLANE, LANE)), _sds((n_tok, D_MODEL)), _sds((1, D_MODEL))],
    )(x, nf, target)


GRAD_WIRE = BF16


def _matmul_tn(a, b, name):
    n_tok, k1 = a.shape
    k2 = b.shape[1]
    t1, t2 = _tile_of(k1), _tile_of(k2)

    def body(a_ref, b_ref, o_ref):
        o_ref[...] = _tn(a_ref[...], b_ref[...]).astype(GRAD_WIRE)

    return pl.pallas_call(
        body, name=name, grid=(k1 // t1, k2 // t2),
        in_specs=[pl.BlockSpec((n_tok, t1), lambda i, j: (0, i)), pl.BlockSpec((n_tok, t2), lambda i, j: (0, j))],
        out_specs=pl.BlockSpec((t1, t2), lambda i, j: (i, j)),
        out_shape=_sds((k1, k2), GRAD_WIRE),
        compiler_params=pltpu.CompilerParams(dimension_semantics=("arbitrary", "arbitrary"), vmem_limit_bytes=VMEM_LIMIT),
    )(a, b)


def _tn_params():
    return pltpu.CompilerParams(dimension_semantics=("arbitrary", "arbitrary"), vmem_limit_bytes=VMEM_LIMIT)


def _matmul_tn_lhs_blocks(a, b, width, keep, name):
    n_tok, k1 = a.shape
    k2 = b.shape[1]
    t2 = _tile_of(k2)

    def body(a_ref, b_ref, o_ref):
        o_ref[...] = _tn(a_ref[...], b_ref[...])[:keep, :].astype(GRAD_WIRE)

    return pl.pallas_call(
        body, name=name, grid=(k1 // width, k2 // t2),
        in_specs=[pl.BlockSpec((n_tok, width), lambda d, j: (0, d)), pl.BlockSpec((n_tok, t2), lambda d, j: (0, j))],
        out_specs=pl.BlockSpec((None, keep, t2), lambda d, j: (d, 0, j)),
        out_shape=_sds((k1 // width, keep, k2), GRAD_WIRE), compiler_params=_tn_params(),
    )(a, b)


def _matmul_tn_pair(a0, a1, b, name):
    n_tok, k1 = a0.shape
    k2 = b.shape[1]
    t1, t2 = _tile_of(k1), _tile_of(k2)

    def body(a0_ref, a1_ref, b_ref, o_ref):
        @pl.when(pl.program_id(0) == 0)
        def _():
            o_ref[...] = _tn(a0_ref[...], b_ref[...]).astype(GRAD_WIRE)

        @pl.when(pl.program_id(0) == 1)
        def _():
            o_ref[...] = _tn(a1_ref[...], b_ref[...]).astype(GRAD_WIRE)

    lhs = pl.BlockSpec((n_tok, t1), lambda s, i, j: (0, i))
    return pl.pallas_call(
        body, name=name, grid=(2, k1 // t1, k2 // t2),
        in_specs=[lhs, lhs, pl.BlockSpec((n_tok, t2), lambda s, i, j: (0, j))],
        out_specs=pl.BlockSpec((None, t1, t2), lambda s, i, j: (s, i, j)),
        out_shape=_sds((2, k1, k2), GRAD_WIRE),
        compiler_params=pltpu.CompilerParams(dimension_semantics=("arbitrary",) * 3, vmem_limit_bytes=VMEM_LIMIT),
    )(a0, a1, b)


W_IN_BLOCK = IN_PROJ // N_DEV
W_IN_SPLITS = (D_MODEL, 2 * D_MODEL, 2 * D_MODEL + SSD_CONV_DIM)
RELAYOUT_TILE = 256


def _w_in_split(blocks, after, name):
    tr = RELAYOUT_TILE

    def body(b_ref, after_ref, wu_ref, wz_ref, wx_ref, wd_ref):
        full = jnp.concatenate([b_ref[d] for d in range(N_DEV)], axis=1)
        wu_ref[...] = full[:, :W_IN_SPLITS[0]]
        wz_ref[...] = full[:, W_IN_SPLITS[0]:W_IN_SPLITS[1]]
        wx_ref[...] = full[:, W_IN_SPLITS[1]:W_IN_SPLITS[2]]
        wd_ref[...] = jnp.concatenate([full[:, W_IN_SPLITS[2]:], jnp.zeros((tr, LANE - SSD_HEADS), full.dtype)], axis=1)

    return _call(
        body, name, (D_MODEL // tr,), [pl.BlockSpec((N_DEV, tr, W_IN_BLOCK), lambda i: (0, i, 0)), HBM],
        [_rows(tr, D_MODEL), _rows(tr, D_MODEL), _rows(tr, SSD_CONV_DIM), _rows(tr, LANE)],
        [_sds((D_MODEL, D_MODEL), BF16), _sds((D_MODEL, D_MODEL), BF16), _sds((D_MODEL, SSD_CONV_DIM), BF16),
         _sds((D_MODEL, LANE), BF16)],
    )(blocks, after)


def _w_in_grad_blocks(gu, gz, gx, gdt, name):
    tr = RELAYOUT_TILE

    def body(gu_ref, gz_ref, gx_ref, gdt_ref, o_ref):
        full = jnp.concatenate([gu_ref[...], gz_ref[...], gx_ref[...], gdt_ref[...]], axis=1)
        for d in range(N_DEV):
            o_ref[d] = full[:, d * W_IN_BLOCK:(d + 1) * W_IN_BLOCK]

    return _call(
        body, name, (D_MODEL // tr,),
        [_rows(tr, D_MODEL), _rows(tr, D_MODEL), _rows(tr, SSD_CONV_DIM), _rows(tr, LANE)],
        [pl.BlockSpec((N_DEV, tr, W_IN_BLOCK), lambda i: (0, i, 0))], [_sds((N_DEV, D_MODEL, W_IN_BLOCK), gu.dtype)],
    )(gu, gz, gx, gdt)[0]


S5_SLICES = D_MODEL // LANE
S5_SLICE_STATES = S5_LANES // S5_SLICES
SCAN_LANES = 512


def _s5_scan(br_ref, bi_ref, a_r, a_i, car_r, car_i, ini_r, ini_i, reverse, xr_ref=None, xi_ref=None,
             acc_r=None, acc_i=None):
    n_rows = br_ref.shape[1]
    seg = n_rows // SUBLANE
    order = range(SUBLANE - 1, -1, -1) if reverse else range(SUBLANE)

    def rows(t):
        return pl.ds(pl.multiple_of(((seg - 1 - t) if reverse else t) * SUBLANE, SUBLANE), SUBLANE)

    tiles_per = SCAN_LANES // LANE

    def load(ref, t, lb):
        return jnp.concatenate([ref[lb * tiles_per + j, rows(t), :] for j in range(tiles_per)], axis=1)

    def store(ref, t, lb, val):
        for j in range(tiles_per):
            ref[lb * tiles_per + j, rows(t), :] = val[:, j * LANE:(j + 1) * LANE]

    for lb in range(S5_LANES // SCAN_LANES):
        lanes = pl.ds(lb * SCAN_LANES, SCAN_LANES)
        ar1, ai1 = a_r[:, lb * SCAN_LANES:(lb + 1) * SCAN_LANES], a_i[:, lb * SCAN_LANES:(lb + 1) * SCAN_LANES]
        ar8 = jnp.broadcast_to(ar1, (SUBLANE, SCAN_LANES))
        ai8 = jnp.broadcast_to(ai1, (SUBLANE, SCAN_LANES))

        def local(t, c):
            sr, si = c
            return (ar8 * sr - ai8 * si + load(br_ref, t, lb), ar8 * si + ai8 * sr + load(bi_ref, t, lb))

        zero = jnp.zeros((SUBLANE, SCAN_LANES), F32)
        er, ei = lax.fori_loop(0, seg, local, (zero, zero))
        pr, pi = ar1, ai1
        for _ in range(seg.bit_length() - 1):
            pr, pi = pr * pr - pi * pi, 2.0 * pr * pi
        cr, ci = car_r[:, lanes], car_i[:, lanes]
        for s in order:
            ini_r[s:s + 1, lanes] = cr
            ini_i[s:s + 1, lanes] = ci
            cr, ci = pr * cr - pi * ci + er[s:s + 1, :], pr * ci + pi * cr + ei[s:s + 1, :]
        car_r[:, lanes] = cr
        car_i[:, lanes] = ci

        if xr_ref is None:
            def final(t, c):
                sr, si = c
                nr = ar8 * sr - ai8 * si + load(br_ref, t, lb)
                ni = ar8 * si + ai8 * sr + load(bi_ref, t, lb)
                store(br_ref, t, lb, nr)
                store(bi_ref, t, lb, ni)
                return nr, ni

            lax.fori_loop(0, seg, final, (ini_r[:, lanes], ini_i[:, lanes]))
        else:
            def final_acc(t, c):
                sr, si, gr, gi = c
                xr, xi = load(xr_ref, t, lb), load(xi_ref, t, lb)
                gr = gr + sr * xr + si * xi
                gi = gi + si * xr - sr * xi
                nr = ar8 * sr - ai8 * si + load(br_ref, t, lb)
                ni = ar8 * si + ai8 * sr + load(bi_ref, t, lb)
                store(br_ref, t, lb, nr)
                store(bi_ref, t, lb, ni)
                return nr, ni, gr, gi

            _, _, gr, gi = lax.fori_loop(0, seg, final_acc,
                                         (ini_r[:, lanes], ini_i[:, lanes], acc_r[:, lanes], acc_i[:, lanes]))
            acc_r[:, lanes] = gr
            acc_i[:, lanes] = gi


def _s5_tail(gg, q, sn):
    return _rms(gg * _sigmoid(q), sn)


def _scan_order(n_rows):
    seg = n_rows // SUBLANE
    r = lax.broadcasted_iota(jnp.int32, (n_rows, n_rows), 0)
    c = lax.broadcasted_iota(jnp.int32, (n_rows, n_rows), 1)
    return (c == (r % SUBLANE) * seg + r // SUBLANE).astype(F32)


S5_STATE_TILES = S5_LANES // LANE
TILES_PER_SLICE = S5_SLICE_STATES // LANE


def _put_states(ref, k, val):
    for j in range(TILES_PER_SLICE):
        ref[k * TILES_PER_SLICE + j] = val[:, j * LANE:(j + 1) * LANE]


def _get_states(ref, k):
    return jnp.concatenate([ref[k * TILES_PER_SLICE + j] for j in range(TILES_PER_SLICE)], axis=1)


def _state_rows(tile, n_tiles=None):
    if n_tiles is None:
        return pl.BlockSpec((S5_STATE_TILES, tile, LANE), lambda i: (0, i, 0))
    return pl.BlockSpec((S5_STATE_TILES, tile, LANE), lambda i: (0, n_tiles - 1 - i, 0))


def _s5_fwd(u, a_r, a_i, bdb, bcr, bci, dsk, wglu, bglu, sn, name):
    n_tok = u.shape[0]
    tc = S5_TILE
    sw = S5_SLICE_STATES

    def body(u_ref, ar_ref, ai_ref, bdb_ref, bcr_ref, bci_ref, d_ref, wg_ref, bg_ref, sn_ref,
             ya_ref, xr_ref, xi_ref, v_ref, car_r, car_i, ini_r, ini_i):
        @pl.when(pl.program_id(0) == 0)
        def _():
            car_r[...] = jnp.zeros_like(car_r)
            car_i[...] = jnp.zeros_like(car_i)

        order = _scan_order(tc)
        u_t = _nn_f32(order, u_ref[...])
        ub = u_t.astype(BF16)
        for k in range(S5_SLICES):
            bu = _nn(ub[:, k * LANE:(k + 1) * LANE], bdb_ref[k])
            _put_states(xr_ref, k, bu[:, :sw])
            _put_states(xi_ref, k, bu[:, sw:])
        _s5_scan(xr_ref, xi_ref, ar_ref[...], ai_ref[...], car_r, car_i, ini_r, ini_i, reverse=False)
        vs = [_nn(_get_states(xr_ref, k), bcr_ref[k]) - _nn(_get_states(xi_ref, k), bci_ref[k])
              for k in range(S5_SLICES)]
        v = jnp.concatenate(vs, axis=1) + d_ref[...] * u_t
        v_ref[...] = v
        gg = _gelu(v)
        ya_ref[...] = _tn_f32(order, _s5_tail(gg, _nn(gg, wg_ref[...]) + bg_ref[...], sn_ref[...]))

    return _call(
        body, name, (n_tok // tc,),
        [_rows(tc, D_MODEL), _const((1, S5_LANES)), _const((1, S5_LANES)), _const(bdb.shape), _const(bcr.shape),
         _const(bci.shape), _const((1, D_MODEL)), _const(wglu.shape), _const((1, D_MODEL)), _const((1, D_MODEL))],
        [_rows(tc, D_MODEL), _state_rows(tc), _state_rows(tc), _rows(tc, D_MODEL)],
        [_sds((n_tok, D_MODEL)), _sds((S5_STATE_TILES, n_tok, LANE)), _sds((S5_STATE_TILES, n_tok, LANE)),
         _sds((n_tok, D_MODEL))],
        scratch=[pltpu.VMEM((1, S5_LANES), F32), pltpu.VMEM((1, S5_LANES), F32),
                 pltpu.VMEM((SUBLANE, S5_LANES), F32), pltpu.VMEM((SUBLANE, S5_LANES), F32)],
    )(u, a_r, a_i, bdb, bcr, bci, dsk, wglu, bglu, sn)


def _s5_bwd(dya, v, u, xr, xi, a_r, a_i, bdb, bcr, bci, dsk, wglu, bglu, sn, after, name):
    n_tok = u.shape[0]
    tc = S5_TILE
    nt = n_tok // tc
    sw = S5_SLICE_STATES

    def body(dya_ref, v_ref, u_ref, xr_ref, xi_ref, ar_ref, ai_ref, bdb_ref, bcr_ref, bci_ref, d_ref, wg_ref, bg_ref, sn_ref,
             after_ref, du_ref, gg_ref, dq_ref, gbdb_ref, gbcr_ref, gbci_ref, gar_ref, gai_ref, gd_ref, gbg_ref, gsn_ref,
             gr_ref, gi_ref, car_r, car_i, ini_r, ini_i):
        @pl.when(pl.program_id(0) == 0)
        def _():
            for r in (car_r, car_i, gbdb_ref, gbcr_ref, gbci_ref, gar_ref, gai_ref, gd_ref, gbg_ref, gsn_ref):
                r[...] = jnp.zeros_like(r)

        order = _scan_order(tc)
        u_t = _nn_f32(order, u_ref[...])
        gg, gelu_vjp = jax.vjp(_gelu, v_ref[...])
        _, tail_vjp = jax.vjp(_s5_tail, gg, _nn(gg, wg_ref[...]) + bg_ref[...], sn_ref[...])
        dgg, dq, dsn = tail_vjp(_nn_f32(order, dya_ref[...]))
        (dv,) = gelu_vjp(dgg + _nt(dq, wg_ref[...]))
        gg_ref[...] = gg.astype(BF16)
        dq_ref[...] = dq.astype(BF16)
        gd_ref[...] += jnp.sum(dv * u_t, axis=0, keepdims=True)
        gbg_ref[...] += jnp.sum(dq, axis=0, keepdims=True)
        gsn_ref[...] += dsn
        dvb = dv.astype(BF16)
        for k in range(S5_SLICES):
            dvk = dvb[:, k * LANE:(k + 1) * LANE]
            _put_states(gr_ref, k, _nt(dvk, bcr_ref[k]))
            _put_states(gi_ref, k, -_nt(dvk, bci_ref[k]))
            gbcr_ref[k] += _tn(_get_states(xr_ref, k), dvk)
            gbci_ref[k] -= _tn(_get_states(xi_ref, k), dvk)
        _s5_scan(gr_ref, gi_ref, ar_ref[...], -ai_ref[...], car_r, car_i, ini_r, ini_i, reverse=True,
                 xr_ref=xr_ref, xi_ref=xi_ref, acc_r=gar_ref, acc_i=gai_ref)
        ub = u_t.astype(BF16)
        dus = []
        for k in range(S5_SLICES):
            gk_r, gk_i = _get_states(gr_ref, k).astype(BF16), _get_states(gi_ref, k).astype(BF16)
            bk = bdb_ref[k]
            dus.append(_nt(gk_r, bk[:, :sw]) + _nt(gk_i, bk[:, sw:]))
            uk = ub[:, k * LANE:(k + 1) * LANE]
            gbdb_ref[k, :, :sw] += _tn(uk, gk_r)
            gbdb_ref[k, :, sw:] += _tn(uk, gk_i)
        du_ref[...] = _tn_f32(order, jnp.concatenate(dus, axis=1) + d_ref[...] * dv)

    rev = functools.partial(_rows, n_tiles=nt)
    return _call(
        body, name, (nt,),
        [rev(tc, D_MODEL), rev(tc, D_MODEL), rev(tc, D_MODEL), _state_rows(tc, nt), _state_rows(tc, nt),
         _const((1, S5_LANES)), _const((1, S5_LANES)), _const(bdb.shape), _const(bcr.shape), _const(bci.shape),
         _const((1, D_MODEL)), _const(wglu.shape), _const((1, D_MODEL)), _const((1, D_MODEL)), HBM],
        [rev(tc, D_MODEL), rev(tc, D_MODEL), rev(tc, D_MODEL), _full(bdb.shape), _full(bcr.shape), _full(bci.shape),
         _full((SUBLANE, S5_LANES)), _full((SUBLANE, S5_LANES)), _full((1, D_MODEL)), _full((1, D_MODEL)), _full((1, D_MODEL))],
        [_sds((n_tok, D_MODEL)), _sds((n_tok, D_MODEL), BF16), _sds((n_tok, D_MODEL), BF16), _sds(bdb.shape), _sds(bcr.shape),
         _sds(bci.shape), _sds((SUBLANE, S5_LANES)), _sds((SUBLANE, S5_LANES)), _sds((1, D_MODEL)), _sds((1, D_MODEL)),
         _sds((1, D_MODEL))],
        scratch=[pltpu.VMEM((S5_STATE_TILES, tc, LANE), F32), pltpu.VMEM((S5_STATE_TILES, tc, LANE), F32),
                 pltpu.VMEM((1, S5_LANES), F32), pltpu.VMEM((1, S5_LANES), F32),
                 pltpu.VMEM((SUBLANE, S5_LANES), F32), pltpu.VMEM((SUBLANE, S5_LANES), F32)],
    )(dya, v, u, xr, xi, a_r, a_i, bdb, bcr, bci, dsk, wglu, bglu, sn, after)


SSD_WIDTH = SSD_HEADS * SSD_HEAD_DIM
SSD_GROUPS = 2
HEADS_PER_GROUP = SSD_HEADS // SSD_GROUPS


def _take(x, axis, start, size):
    n = x.shape[axis]

    def sl(v):
        return lax.slice_in_dim(v, start, start + size, axis=axis)

    @jax.custom_vjp
    def f(v):
        return sl(v)

    def bwd(_, g):
        parts = []
        if start:
            parts.append(jnp.zeros(g.shape[:axis] + (start,) + g.shape[axis + 1:], g.dtype))
        parts.append(g)
        if n - start - size:
            parts.append(jnp.zeros(g.shape[:axis] + (n - start - size,) + g.shape[axis + 1:], g.dtype))
        return (jnp.concatenate(parts, axis=axis) if len(parts) > 1 else g,)

    f.defvjp(lambda v: (sl(v), None), bwd)
    return f(x)


def _lane_of(x, h):
    col = lax.broadcasted_iota(jnp.int32, x.shape, 1)
    return jnp.sum(jnp.where(col == h, x, 0.0), axis=1, keepdims=True)


def _ssd_chunk(xc, z, dt, dtb, alog, dvec, gn, st, nn, nt, tn, cumsum, take):
    t_len = xc.shape[0]
    xa = _silu(xc)
    dtp = _softplus(dt + dtb)
    d_a = dtp * (-jnp.exp(alog))
    row = lax.broadcasted_iota(jnp.int32, (t_len, t_len), 0)
    col = lax.broadcasted_iota(jnp.int32, (t_len, t_len), 1)
    causal = row >= col
    cum = cumsum(causal.astype(F32), d_a)
    eye = (row == col).astype(F32)
    ys, sts = [], []
    for g in range(SSD_GROUPS):
        bg = take(xa, 1, SSD_WIDTH + g * SSD_STATE, SSD_STATE)
        cg = take(xa, 1, SSD_WIDTH + (SSD_GROUPS + g) * SSD_STATE, SSD_STATE)
        cb = nt(cg, bg)
        for r in range(HEADS_PER_GROUP):
            h = g * HEADS_PER_GROUP + r
            cc = _lane_of(cum, h)
            cr = jnp.sum(cc * eye, axis=0, keepdims=True)
            decay = jnp.exp(jnp.where(causal, cc - cr, -1e30))
            xh = take(xa, 1, h * SSD_HEAD_DIM, SSD_HEAD_DIM)
            xdt = xh * _lane_of(dtp, h)
            sth = take(st, 0, h * SSD_HEAD_DIM, SSD_HEAD_DIM)
            c_last = jnp.sum(jnp.where(row[:, :1] == t_len - 1, cc, 0.0), axis=0, keepdims=True)
            y = nn(cb * decay, xdt) + jnp.exp(cc) * nt(cg, sth) + _lane_of(dvec, h) * xh
            ys.append(y)
            sts.append(jnp.exp(c_last) * sth + tn(xdt * jnp.exp(c_last - cc), bg))
    y = jnp.concatenate(ys, axis=1) * _silu(z)
    return _rms(y, gn), jnp.concatenate(sts, axis=0)


def _shift_back(cur, prev, j):
    if j == 0:
        return cur
    row = lax.broadcasted_iota(jnp.int32, cur.shape, 0)
    return jnp.where(row < j, pltpu.roll(prev, j, 0), pltpu.roll(cur, j, 0))


def _shift_ahead(cur, nxt, j):
    if j == 0:
        return cur
    n = cur.shape[0]
    row = lax.broadcasted_iota(jnp.int32, cur.shape, 0)
    return jnp.where(row >= n - j, pltpu.roll(nxt, n - j, 0), pltpu.roll(cur, n - j, 0))


def _conv(cur, prev, w, b):
    out = b + w[SSD_CONV - 1:SSD_CONV, :] * cur
    for k in range(SSD_CONV - 1):
        out = out + w[k:k + 1, :] * _shift_back(cur, prev, SSD_CONV - 1 - k)
    return out


def _ssd_fwd(xbc, z, dt, conv_w, conv_b, dtb, alog, dvec, gn, name):
    n_tok = xbc.shape[0]
    tc = SSD_CHUNK
    nc = n_tok // tc
    st_rows = SSD_HEADS * SSD_HEAD_DIM

    def body(cur_ref, prev_ref, z_ref, dt_ref, w_ref, b_ref, dtb_ref, alog_ref, dvec_ref, gn_ref,
             yb_ref, stin_ref, st_ref):
        i = pl.program_id(0)

        @pl.when(i == 0)
        def _():
            st_ref[...] = jnp.zeros_like(st_ref)

        prev = jnp.where(i > 0, prev_ref[...], 0.0)
        xc = _conv(cur_ref[...], prev, w_ref[...], b_ref[...])
        st = st_ref[...]
        stin_ref[0] = st
        yb, st_new = _ssd_chunk(xc, z_ref[...], dt_ref[...], dtb_ref[...], alog_ref[...], dvec_ref[...], gn_ref[...], st,
                                _nn, _nt, _tn, _nn_f32, lambda v, axis, start, size: lax.slice_in_dim(v, start, start + size, axis=axis))
        yb_ref[...] = yb
        st_ref[...] = st_new

    return _call(
        body, name, (nc,),
        [_rows(tc, SSD_CONV_DIM), pl.BlockSpec((tc, SSD_CONV_DIM), lambda i: (jnp.maximum(i - 1, 0), 0)),
         _rows(tc, D_MODEL), _rows(tc, LANE), _const((SSD_CONV, SSD_CONV_DIM)), _const((1, SSD_CONV_DIM)),
         _const((1, LANE)), _const((1, LANE)), _const((1, LANE)), _const((1, D_MODEL))],
        [_rows(tc, D_MODEL), pl.BlockSpec((1, st_rows, SSD_STATE), lambda i: (i, 0, 0))],
        [_sds((n_tok, D_MODEL)), _sds((nc, st_rows, SSD_STATE))],
        scratch=[pltpu.VMEM((st_rows, SSD_STATE), F32)],
    )(xbc, xbc, z, dt, conv_w, conv_b, dtb, alog, dvec, gn)


def _ssd_bwd(dyb, xbc, z, dt, stin, conv_w, conv_b, dtb, alog, dvec, gn, name):
    n_tok = xbc.shape[0]
    tc = SSD_CHUNK
    nc = n_tok // tc
    st_rows = SSD_HEADS * SSD_HEAD_DIM

    def body(dyb_ref, cur_ref, prev_ref, z_ref, dt_ref, stin_ref, w_ref, b_ref, dtb_ref, alog_ref, dvec_ref, gn_ref,
             dxbc_ref, dz_ref, ddt_ref, gw_ref, gb_ref, gdtb_ref, galog_ref, gdvec_ref, ggn_ref,
             dst_ref, dxc_next_ref):
        i = pl.program_id(0)

        @pl.when(i == 0)
        def _():
            for r in (dst_ref, dxc_next_ref, gw_ref, gb_ref, gdtb_ref, galog_ref, gdvec_ref, ggn_ref):
                r[...] = jnp.zeros_like(r)

        cur = cur_ref[...]
        prev = jnp.where(i < nc - 1, prev_ref[...], 0.0)
        w = w_ref[...]
        xc = _conv(cur, prev, w, b_ref[...])
        chunk = functools.partial(_ssd_chunk, nn=_nn_d, nt=_nt_d, tn=_tn_d, cumsum=_cumsum_rows, take=_take)
        _, vjp = jax.vjp(chunk, xc, z_ref[...], dt_ref[...], dtb_ref[...], alog_ref[...], dvec_ref[...], gn_ref[...],
                         stin_ref[0])
        dxc, dz, ddt, gdtb, galog, gdvec, ggn, dst = vjp((dyb_ref[...], dst_ref[...]))
        dst_ref[...] = dst
        dz_ref[...] = dz
        ddt_ref[...] = ddt
        gdtb_ref[...] += gdtb
        galog_ref[...] += galog
        gdvec_ref[...] += gdvec
        ggn_ref[...] += ggn
        dxc_next = dxc_next_ref[...]
        dxbc = w[SSD_CONV - 1:SSD_CONV, :] * dxc
        gws = []
        for k in range(SSD_CONV - 1):
            j = SSD_CONV - 1 - k
            dxbc = dxbc + w[k:k + 1, :] * _shift_ahead(dxc, dxc_next, j)
            gws.append(jnp.sum(dxc * _shift_back(cur, prev, j), axis=0, keepdims=True))
        gws.append(jnp.sum(dxc * cur, axis=0, keepdims=True))
        dxbc_ref[...] = dxbc
        gw_ref[...] += jnp.concatenate(gws, axis=0)
        gb_ref[...] += jnp.sum(dxc, axis=0, keepdims=True)
        dxc_next_ref[...] = dxc

    rev = functools.partial(_rows, n_tiles=nc)
    return _call(
        body, name, (nc,),
        [rev(tc, D_MODEL), rev(tc, SSD_CONV_DIM),
         pl.BlockSpec((tc, SSD_CONV_DIM), lambda i: (jnp.maximum(nc - 2 - i, 0), 0)),
         rev(tc, D_MODEL), rev(tc, LANE), pl.BlockSpec((1, st_rows, SSD_STATE), lambda i: (nc - 1 - i, 0, 0)),
         _const((SSD_CONV, SSD_CONV_DIM)), _const((1, SSD_CONV_DIM)), _const((1, LANE)), _const((1, LANE)),
         _const((1, LANE)), _const((1, D_MODEL))],
        [rev(tc, SSD_CONV_DIM), rev(tc, D_MODEL), rev(tc, LANE), _full((SSD_CONV, SSD_CONV_DIM)), _full((1, SSD_CONV_DIM)),
         _full((1, LANE)), _full((1, LANE)), _full((1, LANE)), _full((1, D_MODEL))],
        [_sds((n_tok, SSD_CONV_DIM)), _sds((n_tok, D_MODEL)), _sds((n_tok, LANE)), _sds((SSD_CONV, SSD_CONV_DIM)),
         _sds((1, SSD_CONV_DIM)), _sds((1, LANE)), _sds((1, LANE)), _sds((1, LANE)), _sds((1, D_MODEL))],
        scratch=[pltpu.VMEM((st_rows, SSD_STATE), F32), pltpu.VMEM((tc, SSD_CONV_DIM), F32)],
    )(dyb, xbc, xbc, z, dt, stin, conv_w, conv_b, dtb, alog, dvec, gn)


@jax.custom_vjp
def _expand_cols(x, e):
    return _nn_f32(x, e)


_expand_cols.defvjp(
    lambda x, e: (_nn_f32(x, e), e),
    lambda e, g: (lax.dot_general(g, e, (((1,), (1,)), ((), ())), precision=lax.Precision.HIGHEST,
                                  preferred_element_type=F32), jnp.zeros_like(e)))


def _s5_discretize(lam_re, lam_im, log_step, b_re, b_im, expand):
    step = jnp.exp(log_step)
    mag = jnp.exp(lam_re * step)
    ang = lam_im * step
    a_r = mag * jnp.cos(ang)
    a_i = mag * jnp.sin(ang)
    den = lam_re * lam_re + lam_im * lam_im
    n_r = a_r - 1.0
    coef_r = _expand_cols((n_r * lam_re + a_i * lam_im) / den, expand)
    coef_i = _expand_cols((a_i * lam_re - n_r * lam_im) / den, expand)
    return a_r, a_i, coef_r * b_re - coef_i * b_im, coef_r * b_im + coef_i * b_re


def _expand_matrix():
    p = lax.broadcasted_iota(jnp.int32, (S5_STATE, S5_STATE * S5_GROUP), 0)
    c = lax.broadcasted_iota(jnp.int32, (S5_STATE, S5_STATE * S5_GROUP), 1)
    return (c // S5_GROUP == p).astype(F32)


def _s5_discretize_fwd(lam_re, lam_im, log_step, b_re, b_im, name):
    def body(lr_ref, li_ref, ls_ref, br_ref, bi_ref, ar_ref, ai_ref, bbr_ref, bbi_ref):
        outs = _s5_discretize(lr_ref[...], li_ref[...], ls_ref[...], br_ref[...], bi_ref[...], _expand_matrix())
        for r, o in zip((ar_ref, ai_ref, bbr_ref, bbi_ref), outs):
            r[...] = o

    sq, wide = (S5_GROUPS, S5_STATE), (S5_GROUPS, S5_STATE * S5_GROUP)
    return _call(body, name, (1,), [_full(sq), _full(sq), _full((S5_GROUPS, 1)), _full(wide), _full(wide)],
                 [_full(sq), _full(sq), _full(wide), _full(wide)], [_sds(sq), _sds(sq), _sds(wide), _sds(wide)],
                 )(lam_re, lam_im, log_step, b_re, b_im)


def _s5_discretize_bwd(lam_re, lam_im, log_step, b_re, b_im, g_ar8, g_ai8, g_bbr, g_bbi, name):
    def body(lr_ref, li_ref, ls_ref, br_ref, bi_ref, gar_ref, gai_ref, gbbr_ref, gbbi_ref,
             glr_ref, gli_ref, gls_ref, gbr_ref, gbi_ref):
        _, vjp = jax.vjp(functools.partial(_s5_discretize, expand=_expand_matrix()),
                         lr_ref[...], li_ref[...], ls_ref[...], br_ref[...], bi_ref[...])
        grads = vjp((jnp.sum(gar_ref[...], axis=0), jnp.sum(gai_ref[...], axis=0), gbbr_ref[...], gbbi_ref[...]))
        for r, g in zip((glr_ref, gli_ref, gls_ref, gbr_ref, gbi_ref), grads):
            r[...] = g

    sq, wide, col = (S5_GROUPS, S5_STATE), (S5_GROUPS, S5_STATE * S5_GROUP), (S5_GROUPS, 1)
    part = (SUBLANE,) + sq
    return _call(body, name, (1,),
                 [_full(sq), _full(sq), _full(col), _full(wide), _full(wide), _full(part), _full(part), _full(wide), _full(wide)],
                 [_full(sq), _full(sq), _full(col), _full(wide), _full(wide)],
                 [_sds(sq), _sds(sq), _sds(col), _sds(wide), _sds(wide)],
                 )(lam_re, lam_im, log_step, b_re, b_im, g_ar8, g_ai8, g_bbr, g_bbi)


GROUPS_PER_SLICE = LANE // S5_GROUP


def _block_diag_b(bb):
    t = bb.reshape(S5_SLICES, GROUPS_PER_SLICE, S5_STATE, S5_GROUP)
    eye = jnp.eye(GROUPS_PER_SLICE, dtype=bb.dtype)
    return jnp.einsum("kgph,gf->kghfp", t, eye).reshape(S5_SLICES, LANE, S5_SLICE_STATES)


def _block_diag_b_inv(m):
    t = m.reshape(S5_SLICES, GROUPS_PER_SLICE, S5_GROUP, GROUPS_PER_SLICE, S5_STATE)
    return jnp.einsum("kghgp->kgph", t).reshape(S5_GROUPS, S5_STATE * S5_GROUP)


def _block_diag_c(c):
    t = c.reshape(S5_SLICES, GROUPS_PER_SLICE, S5_GROUP, S5_STATE)
    eye = jnp.eye(GROUPS_PER_SLICE, dtype=c.dtype)
    return jnp.einsum("kghp,gf->kgpfh", t, eye).reshape(S5_SLICES, S5_SLICE_STATES, LANE)


def _block_diag_c_inv(m):
    t = m.reshape(S5_SLICES, GROUPS_PER_SLICE, S5_STATE, GROUPS_PER_SLICE, S5_GROUP)
    return jnp.einsum("kgpgh->kghp", t).reshape(S5_GROUPS, S5_GROUP, S5_STATE)


def _pad_lanes(v):
    return jnp.pad(v.reshape(1, -1), ((0, 0), (0, LANE - v.shape[0])))


def _prepare_layer(w, blk, i, after):
    p = {}
    p["wu"], p["wz"], p["wx"], p["wd"] = _w_in_split(blk["w_in"], after, name=f"w_in_split_{i}")
    p["nm"] = w["norm_mix"][i].reshape(1, D_MODEL)
    p["lam_re"], p["lam_im"] = w["s5_lam_re"][i], w["s5_lam_im"][i]
    p["log_step"] = w["s5_log_step"][i].reshape(S5_GROUPS, 1)
    p["b_re"] = w["s5_b_re"][i].reshape(S5_GROUPS, S5_STATE * S5_GROUP)
    p["b_im"] = w["s5_b_im"][i].reshape(S5_GROUPS, S5_STATE * S5_GROUP)
    a_r, a_i, bb_r, bb_i = _s5_discretize_fwd(p["lam_re"], p["lam_im"], p["log_step"], p["b_re"], p["b_im"],
                                              name=f"s5_discretize_{i}")
    p["a_r"], p["a_i"] = a_r.reshape(1, S5_LANES), a_i.reshape(1, S5_LANES)
    p["bdb"] = jnp.concatenate([_block_diag_b(bb_r), _block_diag_b(bb_i)], axis=2).astype(BF16)
    p["bcr"] = _block_diag_c(w["s5_c_re"][i]).astype(BF16)
    p["bci"] = _block_diag_c(w["s5_c_im"][i]).astype(BF16)
    p["dsk"] = w["s5_d"][i].reshape(1, D_MODEL)
    p["wglu"] = blk["s5_w_glu"].reshape(D_MODEL, D_MODEL)
    p["bglu"] = w["s5_b_glu"][i].reshape(1, D_MODEL)
    p["sn"] = w["s5_norm"][i].reshape(1, D_MODEL)
    p["conv_w"] = blk["ssd_conv_w"]
    p["conv_b"] = w["ssd_conv_b"][i].reshape(1, SSD_CONV_DIM)
    p["dtb"] = _pad_lanes(w["ssd_dt_bias"][i])
    p["alog"] = _pad_lanes(w["ssd_a_log"][i])
    p["dvec"] = _pad_lanes(w["ssd_d"][i])
    p["gn"] = w["ssd_norm"][i].reshape(1, D_MODEL)
    p["wo"] = blk["w_out"].reshape(2 * D_MODEL, D_MODEL)
    p["nf"] = w["norm_ffn"][i].reshape(1, D_MODEL)
    p["wg"], p["wup"], p["wdn"] = (blk[n].reshape(FFN_PAD, D_MODEL) for n in ("w_gate", "w_up", "w_down"))
    return p


def _layer_fwd(x0, p, i):
    u, z, xbc, dt = _inproj_fwd(x0, p["nm"], p["wu"], p["wz"], p["wx"], p["wd"], name=f"inproj_fwd_{i}")
    ya, xr, xi, v = _s5_fwd(u, p["a_r"], p["a_i"], p["bdb"], p["bcr"], p["bci"], p["dsk"], p["wglu"], p["bglu"], p["sn"],
                            name=f"s5_fwd_{i}")
    yb, stin = _ssd_fwd(xbc, z, dt, p["conv_w"], p["conv_b"], p["dtb"], p["alog"], p["dvec"], p["gn"], name=f"ssd_fwd_{i}")
    x1, x2 = _mix_ffn_fwd(x0, ya, yb, p["wo"], p["nf"], p["wg"], p["wup"], p["wdn"], name=f"mix_ffn_fwd_{i}")
    return x2, dict(x0=x0, u=u, z=z, xbc=xbc, dt=dt, xr=xr, xi=xi, v=v, stin=stin, ya=ya, yb=yb, x1=x1)


def _layer_bwd(dx2, s, p, i, after, between=None):
    g = {}
    dx1, dya, dyb, h2, act, dgt, dup, g_nf = _mix_ffn_bwd(s["x1"], dx2, p["wo"], p["nf"], p["wg"], p["wup"], p["wdn"],
                                                          after, name=f"mix_ffn_bwd_{i}")
    g["norm_ffn"] = g_nf.reshape(D_MODEL)
    g["w_down"] = _matmul_tn_lhs_blocks(act, dx2, FFN_BLOCK_PAD, FFN_BLOCK, name=f"grad_w_down_{i}")
    g["w_gate"] = _matmul_tn_lhs_blocks(dgt, h2, FFN_BLOCK_PAD, FFN_BLOCK, name=f"grad_w_gate_{i}")
    g["w_up"] = _matmul_tn_lhs_blocks(dup, h2, FFN_BLOCK_PAD, FFN_BLOCK, name=f"grad_w_up_{i}")
    g["w_out"] = _matmul_tn_pair(s["ya"], s["yb"], dx1, name=f"grad_w_out_{i}").reshape(N_DEV, 2 * D_MODEL // N_DEV, D_MODEL)
    if between is not None:
        after = between(g)

    (du, gg, dq, g_bdb, g_bcr, g_bci, g_ar8, g_ai8, g_d, g_bglu, g_sn) = _s5_bwd(
        dya, s["v"], s["u"], s["xr"], s["xi"], p["a_r"], p["a_i"], p["bdb"], p["bcr"], p["bci"], p["dsk"], p["wglu"],
        p["bglu"], p["sn"], after, name=f"s5_bwd_{i}")
    g["s5_w_glu"] = _matmul_tn(gg, dq, name=f"grad_w_glu_{i}").reshape(N_DEV, D_MODEL // N_DEV, D_MODEL)
    g["s5_d"], g["s5_b_glu"], g["s5_norm"] = g_d.reshape(D_MODEL), g_bglu.reshape(D_MODEL), g_sn.reshape(D_MODEL)
    g["s5_c_re"], g["s5_c_im"] = _block_diag_c_inv(g_bcr), _block_diag_c_inv(g_bci)
    sq = (SUBLANE, S5_GROUPS, S5_STATE)
    g_lr, g_li, g_ls, g_br, g_bi = _s5_discretize_bwd(
        p["lam_re"], p["lam_im"], p["log_step"], p["b_re"], p["b_im"], g_ar8.reshape(sq), g_ai8.reshape(sq),
        _block_diag_b_inv(g_bdb[:, :, :S5_SLICE_STATES]), _block_diag_b_inv(g_bdb[:, :, S5_SLICE_STATES:]),
        name=f"s5_discretize_bwd_{i}")
    g["s5_lam_re"], g["s5_lam_im"], g["s5_log_step"] = g_lr, g_li, g_ls.reshape(S5_GROUPS)
    b_shape = (S5_GROUPS, S5_STATE, S5_GROUP)
    g["s5_b_re"], g["s5_b_im"] = g_br.reshape(b_shape), g_bi.reshape(b_shape)

    dxbc, dz, ddt, g_cw, g_cb, g_dtb, g_alog, g_dvec, g_gn = _ssd_bwd(
        dyb, s["xbc"], s["z"], s["dt"], s["stin"], p["conv_w"], p["conv_b"], p["dtb"], p["alog"], p["dvec"], p["gn"],
        name=f"ssd_bwd_{i}")
    g["ssd_conv_w"] = jnp.moveaxis(g_cw.reshape(SSD_CONV, N_DEV, SSD_CONV_DIM // N_DEV), 1, 0)
    g["ssd_conv_b"] = g_cb.reshape(SSD_CONV_DIM)
    g["ssd_dt_bias"], g["ssd_a_log"], g["ssd_d"] = g_dtb[0, :SSD_HEADS], g_alog[0, :SSD_HEADS], g_dvec[0, :SSD_HEADS]
    g["ssd_norm"] = g_gn.reshape(D_MODEL)

    dx0, h, g_nm = _inproj_bwd(s["x0"], p["nm"], du, dz, dxbc, ddt, dx1, p["wu"], p["wz"], p["wx"], p["wd"],
                               name=f"inproj_bwd_{i}")
    g["norm_mix"] = g_nm.reshape(D_MODEL)
    g["w_in"] = _w_in_grad_blocks(
        _matmul_tn(h, du, name=f"grad_w_in_u_{i}"), _matmul_tn(h, dz, name=f"grad_w_in_z_{i}"),
        _matmul_tn(h, dxbc, name=f"grad_w_in_xbc_{i}"), _matmul_tn(h, ddt, name=f"grad_w_in_dt_{i}"),
        name=f"grad_w_in_blocks_{i}")
    return dx0, g


def _example_step(x, target, w, blks):
    prepared = [_prepare_layer(w, blks[i], i, x) for i in range(DEPTH)]
    saved = []
    h = x
    for i in range(DEPTH):
        h, s = _layer_fwd(h, prepared[i], i)
        saved.append(s)
    loss, dh, g_final = _loss_head(h, w["norm_final"].reshape(1, D_MODEL), target, name="loss_head")
    layer_grads = [None] * DEPTH
    for i in reversed(range(DEPTH)):
        dh, layer_grads[i] = _layer_bwd(dh, saved[i], prepared[i], i, x)
    return loss, dh, layer_grads, g_final.reshape(D_MODEL)


def _mesh_position():
    return lax.axis_index("x"), lax.axis_index("y"), lax.axis_index("c")


def _peer(pos, k):
    x, y, c = pos
    px = 1 - x if k & 4 else x
    py = 1 - y if k & 2 else y
    pc = 1 - c if k & 1 else c
    return (px, py, pc), 4 * px + 2 * py + pc


HBM = pl.BlockSpec(memory_space=pl.ANY)


def _run_copies(local, remote):
    for cp in local + remote:
        cp.start()
    for cp in remote:
        cp.wait_recv()
    for cp in remote:
        cp.wait_send()
    for cp in local:
        cp.wait()


def _comm_scratch(n_units):
    return [pltpu.SemaphoreType.DMA((n_units, N_DEV - 1)), pltpu.SemaphoreType.DMA((n_units, N_DEV - 1)),
            pltpu.SemaphoreType.DMA((n_units,))]


def _gather_blocks(arrays, layered, name):
    units, out_shapes = [], []
    for j, (a, lay) in enumerate(zip(arrays, layered)):
        for layer in (range(a.shape[0]) if lay else (None,)):
            units.append((j, layer, len(out_shapes)))
            out_shapes.append(_sds((N_DEV,) + (a.shape[1:] if lay else a.shape), a.dtype))
    n_in = len(arrays)
    other_chips = (4, 2, 6)

    def body(*refs):
        ins, outs = refs[:n_in], refs[n_in:n_in + len(out_shapes)]
        send_sems, recv_sems, local_sems = refs[n_in + len(out_shapes):]
        pos = _mesh_position()
        me = 4 * pos[0] + 2 * pos[1] + pos[2]
        sibling, _ = _peer(pos, 1)
        local, own, passed = [], [], []
        for u, (j, layer, o) in enumerate(units):
            src = ins[j] if layer is None else ins[j].at[layer]
            local.append(pltpu.make_async_copy(src, outs[o].at[me], local_sems.at[u]))

            def copy(sem, src_ref, slot, to, u=u, o=o):
                return pltpu.make_async_remote_copy(
                    src_ref=src_ref, dst_ref=outs[o].at[slot], send_sem=send_sems.at[u, sem], recv_sem=recv_sems.at[u, sem],
                    device_id=to, device_id_type=MESH_ID)

            own.append([copy(0, src, me, sibling)] + [copy(1 + i, src, me, _peer(pos, k)[0]) for i, k in enumerate(other_chips)])
            passed.append([copy(4 + i, outs[o].at[_peer(pos, k)[1]], _peer(pos, k)[1], sibling) for i, k in enumerate(other_chips)])
        for cp in local + [c for unit in own for c in unit]:
            cp.start()
        for u in range(len(units)):
            for i in range(len(other_chips)):
                own[u][1 + i].wait_recv()
                passed[u][i].start()
        for u in range(len(units)):
            own[u][0].wait_recv()
            for cp in passed[u]:
                cp.wait_recv()
        for cp in [c for unit in own + passed for c in unit]:
            cp.wait_send()
        for cp in local:
            cp.wait()

    outs = pl.pallas_call(body, name=name, in_specs=[HBM] * n_in, out_specs=[HBM] * len(out_shapes), out_shape=out_shapes,
                          scratch_shapes=_comm_scratch(len(units)))(*arrays)
    grouped = [[] for _ in arrays]
    for j, _, o in units:
        grouped[j].append(outs[o])
    return [tuple(g) for g in grouped]


def _exchange_blocks(entries, name):
    units, flat_in, out_shapes = [], [], []
    for j, entry in enumerate(entries):
        for layer, a in enumerate(entry):
            units.append((len(flat_in), layer, j))
            flat_in.append(a)
        out_shapes.append(_sds((N_DEV, len(entry)) + entry[0].shape[1:], entry[0].dtype))
    n_in = len(flat_in)

    def body(*refs):
        ins, outs = refs[:n_in], refs[n_in:n_in + len(out_shapes)]
        send_sems, recv_sems, local_sems = refs[n_in + len(out_shapes):]
        pos = _mesh_position()
        me = 4 * pos[0] + 2 * pos[1] + pos[2]
        local, remote = [], []
        for u, (i, layer, o) in enumerate(units):
            local.append(pltpu.make_async_copy(ins[i].at[me], outs[o].at[me, layer], local_sems.at[u]))
            for k in range(1, N_DEV):
                peer, peer_index = _peer(pos, k)
                remote.append(pltpu.make_async_remote_copy(
                    src_ref=ins[i].at[peer_index], dst_ref=outs[o].at[me, layer], send_sem=send_sems.at[u, k - 1],
                    recv_sem=recv_sems.at[u, k - 1], device_id=peer, device_id_type=MESH_ID))
        _run_copies(local, remote)

    return pl.pallas_call(body, name=name, in_specs=[HBM] * n_in, out_specs=[HBM] * len(out_shapes), out_shape=out_shapes,
                          scratch_shapes=_comm_scratch(len(units)))(*flat_in)


SEM = pl.BlockSpec(memory_space=pltpu.SEMAPHORE)
SIDE_EFFECT = pltpu.SideEffectType.DATAFLOW_SIDE_EFFECTING


def _own_slots(arrays, indexed, me, name):
    lands = []
    for u, a in enumerate(arrays):
        block = a.shape[1:] if indexed else a.shape
        rows, cols = _size(block[:-1]), block[-1]
        tr = _row_tile(rows, cap=512)

        def body(me_ref, src_ref, out_ref):
            out_ref[...] = src_ref[...]

        src_spec = (pl.BlockSpec((None, tr, cols), lambda i, me_ref: (me_ref[0], i, 0)) if indexed
                    else pl.BlockSpec((tr, cols), lambda i, me_ref: (i, 0)))
        land = pl.pallas_call(
            body, name=f"{name}_{u}", out_shape=_sds((N_DEV, rows, cols), a.dtype),
            grid_spec=pltpu.PrefetchScalarGridSpec(
                num_scalar_prefetch=1, grid=(rows // tr,), in_specs=[src_spec],
                out_specs=pl.BlockSpec((None, tr, cols), lambda i, me_ref: (me_ref[0], i, 0))),
        )(me, a.reshape((N_DEV, rows, cols) if indexed else (rows, cols)))
        lands.append(land.reshape((N_DEV,) + block))
    return lands


def _split_copies(srcs, lands, send_sems, recv_sems, indexed):
    pos = _mesh_position()
    me = 4 * pos[0] + 2 * pos[1] + pos[2]
    copies = []
    for u, (src, land) in enumerate(zip(srcs, lands)):
        for k in range(1, N_DEV):
            peer, peer_index = _peer(pos, k)
            copies.append(pltpu.make_async_remote_copy(
                src_ref=src.at[peer_index] if indexed else src, dst_ref=land.at[me],
                send_sem=send_sems.at[u * (N_DEV - 1) + k - 1], recv_sem=recv_sems.at[u * (N_DEV - 1) + k - 1],
                device_id=peer, device_id_type=MESH_ID))
    return copies


def _exchange_start(arrays, lands, indexed, name):
    n = len(arrays)

    def body(*refs):
        srcs, zones = refs[:n], refs[n:2 * n]
        send_sems, recv_sems = refs[2 * n], refs[2 * n + 1]
        token = refs[-1]
        for cp in _split_copies(srcs, zones, send_sems, recv_sems, indexed):
            cp.start()
        token[...] = jnp.zeros_like(token)

    sem_shape = pltpu.SemaphoreType.DMA((n * (N_DEV - 1),))
    outs = pl.pallas_call(
        body, name=name, in_specs=[HBM] * (2 * n),
        out_specs=[SEM, SEM] + [HBM] * (2 * n) + [pl.BlockSpec(memory_space=pltpu.VMEM)],
        out_shape=[sem_shape, sem_shape] + [pltpu.HBM(a.shape, a.dtype) for a in list(arrays) + list(lands)]
        + [_sds((SUBLANE, LANE))],
        input_output_aliases={i: 2 + i for i in range(2 * n)},
        compiler_params=pltpu.CompilerParams(has_side_effects=SIDE_EFFECT),
    )(*[pltpu.with_memory_space_constraint(a, pltpu.HBM) for a in list(arrays) + list(lands)])
    return outs[0], outs[1], outs[2:2 + n], outs[2 + n:2 + 2 * n], outs[-1]


def _exchange_wait(send_sems, recv_sems, arrays, lands, after, indexed, name):
    n = len(arrays)

    def body(*refs):
        srcs, zones = refs[:n], refs[n:2 * n]
        s_sems, r_sems = refs[2 * n], refs[2 * n + 1]
        for cp in _split_copies(srcs, zones, s_sems, r_sems, indexed):
            cp.wait_send()
            cp.wait_recv()

    outs = pl.pallas_call(
        body, name=name, in_specs=[HBM] * (2 * n) + [SEM, SEM, HBM],
        out_specs=[HBM] * (2 * n), out_shape=[pltpu.HBM(a.shape, a.dtype) for a in list(arrays) + list(lands)],
        input_output_aliases={i: i for i in range(2 * n)},
        compiler_params=pltpu.CompilerParams(has_side_effects=SIDE_EFFECT),
    )(*arrays, *lands, send_sems, recv_sems, after)
    return outs[n:]


SUM_TILE = 512


def _adamw(w, g, m, v):
    m = ADAM_B1 * m + (1.0 - ADAM_B1) * g
    v = ADAM_B2 * v + (1.0 - ADAM_B2) * (g * g)
    m_hat = m / (1.0 - ADAM_B1 ** ADAM_STEP)
    v_hat = v / (1.0 - ADAM_B2 ** ADAM_STEP)
    return -ADAM_LR * (m_hat / (jnp.sqrt(v_hat) + ADAM_EPS) + ADAM_WD * w), m, v


def _sum_adamw(recv, w, m, v, layer, others, name):
    _, rows, cols = w.shape
    tr = _row_tile(rows, cap=256)

    def body(r_ref, w_ref, m_ref, v_ref, *rest):
        g_ref, d_ref, mo_ref, vo_ref = rest[-4:]
        g = r_ref[0].astype(F32)
        for j in range(1, N_DEV):
            g = g + r_ref[j].astype(F32)
        g_ref[...] = g
        d_ref[...], mo_ref[...], vo_ref[...] = _adamw(w_ref[...], g, m_ref[...], v_ref[...])

    blk = pl.BlockSpec((None, tr, cols), lambda i: (layer, i, 0))
    carried = list(others) if others is not None else []
    return pl.pallas_call(
        body, name=name, grid=(rows // tr,),
        in_specs=[pl.BlockSpec((N_DEV, tr, cols), lambda i: (0, i, 0)), blk, blk, blk] + [HBM] * len(carried),
        out_specs=[blk] * 4, out_shape=[_sds(w.shape)] * 4,
        input_output_aliases={4 + k: k for k in range(len(carried))},
        compiler_params=pltpu.CompilerParams(dimension_semantics=("arbitrary",), vmem_limit_bytes=VMEM_LIMIT),
    )(recv, w, m, v, *carried)


def _sum_senders(recv, name):
    _, rows, cols = recv.shape
    tr = _row_tile(rows, cap=256)

    def body(r_ref, g_ref):
        g = r_ref[0].astype(F32)
        for j in range(1, N_DEV):
            g = g + r_ref[j].astype(F32)
        g_ref[...] = g

    return _call(body, name, (rows // tr,), [pl.BlockSpec((N_DEV, tr, cols), lambda i: (0, i, 0))], [_rows(tr, cols)],
                 [_sds((rows, cols))])(recv)[0]


def _adamw_blocks(g, w, m, v, name):
    n_lay, rows, cols = w.shape
    tr = _row_tile(rows, cap=256)

    def body(g_ref, w_ref, m_ref, v_ref, d_ref, mo_ref, vo_ref):
        d_ref[...], mo_ref[...], vo_ref[...] = _adamw(w_ref[...], g_ref[...], m_ref[...], v_ref[...])

    blk = pl.BlockSpec((None, tr, cols), lambda l, i: (l, i, 0))
    return pl.pallas_call(
        body, name=name, grid=(n_lay, rows // tr), in_specs=[blk] * 4, out_specs=[blk] * 3, out_shape=[_sds(w.shape)] * 3,
        compiler_params=pltpu.CompilerParams(dimension_semantics=("arbitrary", "arbitrary"), vmem_limit_bytes=VMEM_LIMIT),
    )(g, w, m, v)


def _sum_slots(recv, name):
    rows = recv.shape[1]

    def body(r_ref, g_ref):
        g = r_ref[0].astype(F32)
        for j in range(1, N_DEV):
            g = g + r_ref[j].astype(F32)
        g_ref[...] = g

    return _call(body, name, (1,), [_full(recv.shape)], [_full((rows, LANE))], [_sds((rows, LANE))])(recv)[0]


def _adamw_rows(g, w, m, v, name):
    rows = w.shape[0]
    tr = _row_tile(rows)

    def body(g_ref, w_ref, m_ref, v_ref, d_ref, mo_ref, vo_ref):
        d_ref[...], mo_ref[...], vo_ref[...] = _adamw(w_ref[...], g_ref[...], m_ref[...], v_ref[...])

    flat = _rows(tr, LANE)
    return _call(body, name, (rows // tr,), [flat] * 4, [flat] * 3, [_sds((rows, LANE))] * 3)(g, w, m, v)


def _row_tile(rows, cap=1024):
    if rows % SUBLANE:
        return rows
    best = SUBLANE
    for t in range(SUBLANE, cap + 1, SUBLANE):
        if rows % t == 0:
            best = t
    return best


BIG = (("w_in", (DEPTH, D_MODEL, IN_PROJ // N_DEV), 2),
       ("s5_w_glu", (DEPTH, D_MODEL // N_DEV, D_MODEL), 1),
       ("ssd_conv_w", (DEPTH, SSD_CONV, SSD_CONV_DIM // N_DEV), 2),
       ("w_out", (DEPTH, 2 * D_MODEL // N_DEV, D_MODEL), 1),
       ("w_gate", (DEPTH, D_MODEL, FFN_HIDDEN // N_DEV), 2),
       ("w_up", (DEPTH, D_MODEL, FFN_HIDDEN // N_DEV), 2),
       ("w_down", (DEPTH, FFN_HIDDEN // N_DEV, D_MODEL), 1))
SMALL = (("norm_mix", (DEPTH, D_MODEL)), ("s5_lam_re", (DEPTH, S5_GROUPS, S5_STATE)), ("s5_lam_im", (DEPTH, S5_GROUPS, S5_STATE)),
         ("s5_log_step", (DEPTH, S5_GROUPS)), ("s5_b_re", (DEPTH, S5_GROUPS, S5_STATE, S5_GROUP)),
         ("s5_b_im", (DEPTH, S5_GROUPS, S5_STATE, S5_GROUP)), ("s5_c_re", (DEPTH, S5_GROUPS, S5_GROUP, S5_STATE)),
         ("s5_c_im", (DEPTH, S5_GROUPS, S5_GROUP, S5_STATE)), ("s5_d", (DEPTH, D_MODEL)), ("s5_b_glu", (DEPTH, D_MODEL)),
         ("s5_norm", (DEPTH, D_MODEL)), ("ssd_conv_b", (DEPTH, SSD_CONV_DIM)), ("ssd_dt_bias", (DEPTH, SSD_HEADS)),
         ("ssd_a_log", (DEPTH, SSD_HEADS)), ("ssd_d", (DEPTH, SSD_HEADS)), ("ssd_norm", (DEPTH, D_MODEL)),
         ("norm_ffn", (DEPTH, D_MODEL)), ("norm_final", (D_MODEL,)))
WEIGHT_ORDER = ("norm_mix", "w_in", "s5_lam_re", "s5_lam_im", "s5_log_step", "s5_b_re", "s5_b_im", "s5_c_re", "s5_c_im", "s5_d",
                "s5_w_glu", "s5_b_glu", "s5_norm", "ssd_conv_w", "ssd_conv_b", "ssd_dt_bias", "ssd_a_log", "ssd_d", "ssd_norm",
                "w_out", "norm_ffn", "w_gate", "w_up", "w_down", "norm_final")


def _size(shape):
    n = 1
    for s in shape:
        n *= s
    return n


def _round_up(n, m):
    return -(-n // m) * m


SMALL_SIZE = sum(_size(s) for _, s in SMALL)
SMALL_ROWS = _round_up(-(-SMALL_SIZE // (N_DEV * LANE)), SUBLANE)


def _pack(parts, rows, dtype):
    flat = jnp.concatenate([p.reshape(-1).astype(dtype) for p in parts])
    return jnp.pad(flat, (0, rows * LANE - flat.shape[0])).reshape(rows, LANE)


def _unpack(flat, specs):
    out, off = {}, 0
    flat = flat.reshape(-1)
    for name, shape in specs:
        out[name] = flat[off:off + _size(shape)].reshape(shape)
        off += _size(shape)
    return out


def kernel(x, norm_mix, w_in, s5_lam_re, s5_lam_im, s5_log_step, s5_b_re, s5_b_im, s5_c_re, s5_c_im, s5_d, s5_w_glu, s5_b_glu, s5_norm, ssd_conv_w, ssd_conv_b, ssd_dt_bias, ssd_a_log, ssd_d, ssd_norm, w_out, norm_ffn, w_gate, w_up, w_down, norm_final, loss_target, m_norm_mix, m_w_in, m_s5_lam_re, m_s5_lam_im, m_s5_log_step, m_s5_b_re, m_s5_b_im, m_s5_c_re, m_s5_c_im, m_s5_d, m_s5_w_glu, m_s5_b_glu, m_s5_norm, m_ssd_conv_w, m_ssd_conv_b, m_ssd_dt_bias, m_ssd_a_log, m_ssd_d, m_ssd_norm, m_w_out, m_norm_ffn, m_w_gate, m_w_up, m_w_down, m_norm_final, v_norm_mix, v_w_in, v_s5_lam_re, v_s5_lam_im, v_s5_log_step, v_s5_b_re, v_s5_b_im, v_s5_c_re, v_s5_c_im, v_s5_d, v_s5_w_glu, v_s5_b_glu, v_s5_norm, v_ssd_conv_w, v_ssd_conv_b, v_ssd_dt_bias, v_ssd_a_log, v_ssd_d, v_ssd_norm, v_w_out, v_norm_ffn, v_w_gate, v_w_up, v_w_down, v_norm_final):
    given = dict(locals())
    w = {n: given[n] for n in WEIGHT_ORDER}
    m = {n: given["m_" + n] for n in WEIGHT_ORDER}
    v = {n: given["v_" + n] for n in WEIGHT_ORDER}
    big_names = tuple(n for n, _, _ in BIG)
    matmul_names = tuple(n for n in big_names if n != "ssd_conv_w")

    conv_hi = w["ssd_conv_w"].astype(BF16)
    conv_lo = (w["ssd_conv_w"] - conv_hi.astype(F32)).astype(BF16)
    row_pad = ((0, 0), (0, FFN_BLOCK_PAD - FFN_BLOCK), (0, 0))
    as_rows = {"w_gate": jnp.swapaxes(w["w_gate"], 1, 2), "w_up": jnp.swapaxes(w["w_up"], 1, 2), "w_down": w["w_down"]}
    to_send = [jnp.pad(as_rows[n].astype(BF16), row_pad) if n in as_rows else w[n].astype(BF16) for n in matmul_names]

    def layer_blocks(i):
        return [a[i] for a in to_send] + [jnp.stack([conv_hi[i], conv_lo[i]])]

    def as_layer_weights(gathered):
        blk = dict(zip(matmul_names, gathered))
        pair = gathered[-1].astype(F32)
        blk["ssd_conv_w"] = jnp.moveaxis(pair[:, 0] + pair[:, 1], 0, 1).reshape(SSD_CONV, SSD_CONV_DIM)
        return blk

    gathered0 = [g[0] for g in _gather_blocks(layer_blocks(0), [False] * (len(matmul_names) + 1), name="gather_weights_0")]
    blocks1 = layer_blocks(1)
    me = (4 * lax.axis_index("x") + 2 * lax.axis_index("y") + lax.axis_index("c")).astype(jnp.int32).reshape(1)
    sems1 = _exchange_start(blocks1, _own_slots(blocks1, False, me, name="gather_own_1"), False, name="gather_start_1")
    prepared = [_prepare_layer(w, as_layer_weights(gathered0), 0, sems1[-1]), None]
    saved = [None, None]
    h, saved[0] = _layer_fwd(x[0], prepared[0], 0)
    gathered1 = _exchange_wait(*sems1[:4], h, False, name="gather_wait_1")
    prepared[1] = _prepare_layer(w, as_layer_weights(gathered1), 1, sems1[-1])
    h, saved[1] = _layer_fwd(h, prepared[1], 1)
    loss, dh, g_final = _loss_head(h, w["norm_final"].reshape(1, D_MODEL), loss_target[0], name="loss_head")

    layer_grads = [None, None]
    dh, layer_grads[1] = _layer_bwd(dh, saved[1], prepared[1], 1, sems1[-1])
    slots1 = [layer_grads[1][n] for n in big_names]
    sems2 = _exchange_start(slots1, _own_slots(slots1, True, me, name="exchange_own_1"), True, name="exchange_start_1")
    early_names = ("w_out", "w_gate", "w_up", "w_down")
    late_names = tuple(n for n in big_names if n not in early_names)
    early = {}

    def send_early(g):
        slots = [g[n] for n in early_names]
        early["sems"] = _exchange_start(slots, _own_slots(slots, True, me, name="exchange_own_0"), True, name="exchange_start_0")
        return early["sems"][-1]

    grad_x, layer_grads[0] = _layer_bwd(dh, saved[0], prepared[0], 0, sems2[-1], between=send_early)

    small = jnp.concatenate([g_final.reshape(-1) if n == "norm_final"
                             else jnp.stack([layer_grads[i][n] for i in range(DEPTH)]).reshape(-1) for n, _ in SMALL])
    small_slots = jnp.pad(small, (0, N_DEV * SMALL_ROWS * LANE - small.shape[0])).reshape(N_DEV, SMALL_ROWS, LANE)
    received_late = _exchange_blocks([(layer_grads[0][n],) for n in late_names] + [(small_slots,)], name="exchange_gradients_0")
    received1 = _exchange_wait(*sems2[:4], grad_x, True, name="exchange_wait_1")
    received_early = _exchange_wait(*early["sems"][:4], grad_x, True, name="exchange_wait_0")
    received0 = {n: r.reshape(r.shape[:1] + r.shape[2:]) for n, r in zip(late_names, received_late)}
    received0.update(zip(early_names, received_early))

    results = {}
    for j, n in enumerate(big_names):
        recv = (received0[n], received1[j])
        if n in ("w_gate", "w_up"):
            g = jnp.stack([jnp.swapaxes(_sum_senders(recv[i], name=f"sum_{n}_{i}"), 0, 1) for i in range(DEPTH)])
            results[n] = [g, *_adamw_blocks(g, w[n], m[n], v[n], name=f"adamw_{n}")]
        else:
            first = _sum_adamw(recv[1], w[n], m[n], v[n], 1, None, name=f"sum_adamw_{n}_1")
            results[n] = _sum_adamw(recv[0], w[n], m[n], v[n], 0, first, name=f"sum_adamw_{n}_0")
    g_part = _sum_slots(received_late[-1].reshape(N_DEV, SMALL_ROWS, LANE), name="sum_replicated")
    g_small = _gather_blocks([g_part], [False], name="gather_replicated")[0][0].reshape(N_DEV * SMALL_ROWS, LANE)
    small_rows = N_DEV * SMALL_ROWS
    d_small, m_small, v_small = _adamw_rows(
        g_small, _pack([w[n] for n, _ in SMALL], small_rows, F32), _pack([m[n] for n, _ in SMALL], small_rows, F32),
        _pack([v[n] for n, _ in SMALL], small_rows, F32), name="adamw_replicated")
    for k, packed in enumerate((g_small, d_small, m_small, v_small)):
        for n, arr in _unpack(packed, SMALL).items():
            results.setdefault(n, [None] * 4)[k] = arr

    outs = [results[n][k] for k in range(4) for n in WEIGHT_ORDER]
    total_loss = lax.psum(loss[0, 0], ("x", "y", "c"))
    return (total_loss, grad_x[None], *outs)
```

```python
import functools

import jax
import jax.numpy as jnp
from jax import lax
from jax.experimental import pallas as pl
from jax.experimental.pallas import tpu as pltpu

F32 = jnp.float32
BF16 = jnp.bfloat16
MESH_ID = pl.DeviceIdType.MESH

N_DEV = 8
DEPTH = 2
D_MODEL = 1024
S5_GROUPS = 64
S5_GROUP = 16
S5_STATE = 64
S5_LANES = S5_GROUPS * S5_STATE
SSD_HEADS = 16
SSD_HEAD_DIM = 64
SSD_STATE = 128
SSD_CHUNK = 128
SSD_CONV = 4
SSD_CONV_DIM = 1536
FFN_HIDDEN = 2816
IN_PROJ = 3600
EPS = 1e-6
LANE = 128
SUBLANE = 8
VMEM_LIMIT = 56 * 1024 * 1024

ADAM_LR = 0.001
ADAM_B1 = 0.9
ADAM_B2 = 0.999
ADAM_EPS = 1e-08
ADAM_WD = 0.01
ADAM_STEP = 10

TOK_TILE = 256
S5_TILE = 128
S5_SEG = S5_TILE // SUBLANE


def _sigmoid(x):
    return jax.nn.sigmoid(x)


def _silu(x):
    return x * _sigmoid(x)


def _gelu(x):
    return 0.5 * x * (1.0 + jnp.tanh(0.7978845608028654 * (x + 0.044715 * (x * x * x))))


def _softplus(x):
    return jnp.maximum(x, 0.0) + jnp.log(1.0 + jnp.exp(-jnp.abs(x)))


def _rms(x, g):
    r = lax.rsqrt(jnp.mean(x * x, axis=-1, keepdims=True) + EPS)
    return x * r * g


def _nn(a, b):
    return lax.dot_general(a.astype(BF16), b.astype(BF16), (((1,), (0,)), ((), ())), preferred_element_type=F32)


def _nt(a, b):
    return lax.dot_general(a.astype(BF16), b.astype(BF16), (((1,), (1,)), ((), ())), preferred_element_type=F32)


def _tn(a, b):
    return lax.dot_general(a.astype(BF16), b.astype(BF16), (((0,), (0,)), ((), ())), preferred_element_type=F32)


def _nn_f32(a, b):
    return lax.dot_general(a, b, (((1,), (0,)), ((), ())), precision=lax.Precision.HIGHEST, preferred_element_type=F32)


def _tn_f32(a, b):
    return lax.dot_general(a, b, (((0,), (0,)), ((), ())), precision=lax.Precision.HIGHEST, preferred_element_type=F32)


@jax.custom_vjp
def _nn_d(a, b):
    return _nn(a, b)


_nn_d.defvjp(lambda a, b: (_nn(a, b), (a, b)), lambda r, g: (_nt(g, r[1]), _tn(r[0], g)))


@jax.custom_vjp
def _nt_d(a, b):
    return _nt(a, b)


_nt_d.defvjp(lambda a, b: (_nt(a, b), (a, b)), lambda r, g: (_nn(g, r[1]), _tn(g, r[0])))


@jax.custom_vjp
def _tn_d(a, b):
    return _tn(a, b)


_tn_d.defvjp(lambda a, b: (_tn(a, b), (a, b)), lambda r, g: (_nt(r[1], g), _nn(r[0], g)))


@jax.custom_vjp
def _cumsum_rows(tri, x):
    return _nn_f32(tri, x)


_cumsum_rows.defvjp(lambda tri, x: (_nn_f32(tri, x), tri), lambda tri, g: (jnp.zeros_like(tri), _tn_f32(tri, g)))


def _full(shape):
    zeros = (0,) * len(shape)
    return pl.BlockSpec(shape, lambda *_: zeros)


def _const(shape):
    zeros = (0,) * len(shape)
    return pl.BlockSpec(shape, lambda *_: zeros, pipeline_mode=pl.Buffered(1))


def _rows(tile, width, n_tiles=None):
    if n_tiles is None:
        return pl.BlockSpec((tile, width), lambda i: (i, 0))
    return pl.BlockSpec((tile, width), lambda i: (n_tiles - 1 - i, 0))


def _call(body, name, grid, in_specs, out_specs, out_shape, scratch=()):
    return pl.pallas_call(
        body, name=name, grid=grid, in_specs=in_specs, out_specs=out_specs, out_shape=out_shape,
        scratch_shapes=list(scratch),
        compiler_params=pltpu.CompilerParams(dimension_semantics=("arbitrary",) * len(grid),
                                             vmem_limit_bytes=VMEM_LIMIT))


def _sds(shape, dtype=F32):
    return jax.ShapeDtypeStruct(shape, dtype)


def _tile_of(n, cap=512):
    if n <= LANE:
        return n
    best = LANE
    for t in range(LANE, cap + 1, LANE):
        if n % t == 0:
            best = t
    return best


def _inproj_fwd(x, nm, wu, wz, wx, wd, name):
    n_tok = x.shape[0]
    tm = TOK_TILE

    def body(x_ref, nm_ref, wu_ref, wz_ref, wx_ref, wd_ref, u_ref, z_ref, xbc_ref, dt_ref):
        h = _rms(x_ref[...], nm_ref[...]).astype(BF16)
        u_ref[...] = _nn(h, wu_ref[...])
        z_ref[...] = _nn(h, wz_ref[...])
        xbc_ref[...] = _nn(h, wx_ref[...])
        dt_ref[...] = _nn(h, wd_ref[...])

    return _call(
        body, name, (n_tok // tm,),
        [_rows(tm, D_MODEL), _const((1, D_MODEL)), _const(wu.shape), _const(wz.shape), _const(wx.shape), _const(wd.shape)],
        [_rows(tm, D_MODEL), _rows(tm, D_MODEL), _rows(tm, SSD_CONV_DIM), _rows(tm, LANE)],
        [_sds((n_tok, D_MODEL)), _sds((n_tok, D_MODEL)), _sds((n_tok, SSD_CONV_DIM)), _sds((n_tok, LANE))],
    )(x, nm, wu, wz, wx, wd)


def _inproj_bwd(x, nm, du, dz, dxbc, ddt, dres, wu, wz, wx, wd, name):
    n_tok = x.shape[0]
    tm = TOK_TILE

    def body(x_ref, nm_ref, du_ref, dz_ref, dxbc_ref, ddt_ref, dres_ref, wu_ref, wz_ref, wx_ref, wd_ref,
             dx_ref, h_ref, dnm_ref):
        dh = (_nt(du_ref[...], wu_ref[...]) + _nt(dz_ref[...], wz_ref[...])
              + _nt(dxbc_ref[...], wx_ref[...]) + _nt(ddt_ref[...], wd_ref[...]))
        h, vjp = jax.vjp(_rms, x_ref[...], nm_ref[...])
        dx, dnm = vjp(dh)
        dx_ref[...] = dres_ref[...] + dx
        h_ref[...] = h.astype(BF16)

        @pl.when(pl.program_id(0) == 0)
        def _():
            dnm_ref[...] = jnp.zeros_like(dnm_ref)

        dnm_ref[...] += dnm

    return _call(
        body, name, (n_tok // tm,),
        [_rows(tm, D_MODEL), _const((1, D_MODEL)), _rows(tm, D_MODEL), _rows(tm, D_MODEL), _rows(tm, SSD_CONV_DIM),
         _rows(tm, LANE), _rows(tm, D_MODEL), _const(wu.shape), _const(wz.shape), _const(wx.shape), _const(wd.shape)],
        [_rows(tm, D_MODEL), _rows(tm, D_MODEL), _full((1, D_MODEL))],
        [_sds((n_tok, D_MODEL)), _sds((n_tok, D_MODEL), BF16), _sds((1, D_MODEL))],
    )(x, nm, du, dz, dxbc, ddt, dres, wu, wz, wx, wd)


def _ffn_act(gt, up):
    return _silu(gt) * up


FFN_BLOCK = FFN_HIDDEN // N_DEV
FFN_BLOCK_PAD = -(-FFN_BLOCK // LANE) * LANE


FFN_PAD = N_DEV * FFN_BLOCK_PAD


def _mix_ffn_fwd(x0, ya, yb, wo, nf, wg, wu, wd, name):
    n_tok = x0.shape[0]
    tm = TOK_TILE

    def body(x0_ref, ya_ref, yb_ref, wo_ref, nf_ref, wg_ref, wu_ref, wd_ref, x1_ref, x2_ref):
        x1 = x0_ref[...] + _nn(ya_ref[...], wo_ref[:D_MODEL, :]) + _nn(yb_ref[...], wo_ref[D_MODEL:, :])
        h = _rms(x1, nf_ref[...]).astype(BF16)
        x1_ref[...] = x1
        x2_ref[...] = x1 + _nn(_ffn_act(_nt(h, wg_ref[...]), _nt(h, wu_ref[...])), wd_ref[...])

    return _call(
        body, name, (n_tok // tm,),
        [_rows(tm, D_MODEL), _rows(tm, D_MODEL), _rows(tm, D_MODEL), _const(wo.shape),
         _const((1, D_MODEL)), _const(wg.shape), _const(wu.shape), _const(wd.shape)],
        [_rows(tm, D_MODEL), _rows(tm, D_MODEL)],
        [_sds((n_tok, D_MODEL)), _sds((n_tok, D_MODEL))],
    )(x0, ya, yb, wo, nf, wg, wu, wd)


def _mix_ffn_bwd(x1, dx2, wo, nf, wg, wu, wd, after, name):
    n_tok = x1.shape[0]
    tm = TOK_TILE
    n_chunks = 3
    hc = FFN_PAD // n_chunks

    def body(x1_ref, dx2_ref, wo_ref, nf_ref, wg_ref, wu_ref, wd_ref, after_ref,
             dx1_ref, dya_ref, dyb_ref, h_ref, a_ref, dgt_ref, dup_ref, dx2b_ref, dx1b_ref, dnf_ref):
        dx2 = dx2_ref[...]
        dx2b = dx2.astype(BF16)
        dx2b_ref[...] = dx2b
        h, rms_vjp = jax.vjp(_rms, x1_ref[...], nf_ref[...])
        hb = h.astype(BF16)
        dh = jnp.zeros_like(h)
        for c in range(n_chunks):
            rows = pl.ds(c * hc, hc)
            a, act_vjp = jax.vjp(_ffn_act, _nt(hb, wg_ref[rows, :]), _nt(hb, wu_ref[rows, :]))
            dgt, dup = act_vjp(_nt(dx2b, wd_ref[rows, :]))
            a_ref[:, c * hc:(c + 1) * hc] = a.astype(BF16)
            dgt_ref[:, c * hc:(c + 1) * hc] = dgt.astype(BF16)
            dup_ref[:, c * hc:(c + 1) * hc] = dup.astype(BF16)
            dh = dh + _nn(dgt, wg_ref[rows, :]) + _nn(dup, wu_ref[rows, :])
        dx, dnf = rms_vjp(dh)
        dx1 = dx2 + dx
        dx1b = dx1.astype(BF16)
        dx1_ref[...] = dx1
        dx1b_ref[...] = dx1b
        dya_ref[...] = _nt(dx1b, wo_ref[:D_MODEL, :])
        dyb_ref[...] = _nt(dx1b, wo_ref[D_MODEL:, :])
        h_ref[...] = hb

        @pl.when(pl.program_id(0) == 0)
        def _():
            dnf_ref[...] = jnp.zeros_like(dnf_ref)

        dnf_ref[...] += dnf

    hidden = _rows(tm, FFN_PAD)
    return _call(
        body, name, (n_tok // tm,),
        [_rows(tm, D_MODEL), _rows(tm, D_MODEL), _const(wo.shape), _const((1, D_MODEL)),
         _const(wg.shape), _const(wu.shape), _const(wd.shape), HBM],
        [_rows(tm, D_MODEL), _rows(tm, D_MODEL), _rows(tm, D_MODEL), _rows(tm, D_MODEL), hidden, hidden, hidden,
         _rows(tm, D_MODEL), _rows(tm, D_MODEL), _full((1, D_MODEL))],
        [_sds((n_tok, D_MODEL)), _sds((n_tok, D_MODEL)), _sds((n_tok, D_MODEL)), _sds((n_tok, D_MODEL), BF16),
         _sds((n_tok, FFN_PAD), BF16), _sds((n_tok, FFN_PAD), BF16), _sds((n_tok, FFN_PAD), BF16),
         _sds((n_tok, D_MODEL), BF16), _sds((n_tok, D_MODEL), BF16), _sds((1, D_MODEL))],
    )(x1, dx2, wo, nf, wg, wu, wd, after)


def _loss_head(x, nf, target, name):
    n_tok = x.shape[0]
    tm = TOK_TILE

    def loss_of(xv, g, t):
        e = _rms(xv, g) - t
        return 0.5 * jnp.sum(jnp.sum(e * e, axis=-1, keepdims=True) * (1.0 / D_MODEL), axis=0, keepdims=True)

    def body(x_ref, nf_ref, t_ref, loss_ref, dx_ref, dnf_ref):
        loss, vjp = jax.vjp(functools.partial(loss_of, t=t_ref[...]), x_ref[...], nf_ref[...])
        dx, dnf = vjp(jnp.ones_like(loss))
        dx_ref[...] = dx

        @pl.when(pl.program_id(0) == 0)
        def _():
            dnf_ref[...] = jnp.zeros_like(dnf_ref)
            loss_ref[...] = jnp.zeros_like(loss_ref)

        dnf_ref[...] += dnf
        loss_ref[...] += jnp.broadcast_to(loss, loss_ref.shape)

    return _call(
        body, name, (n_tok // tm,),
        [_rows(tm, D_MODEL), _const((1, D_MODEL)), _rows(tm, D_MODEL)],
        [_full((SUBLANE, LANE)), _rows(tm, D_MODEL), _full((1, D_MODEL))],
        [_sds((SUBLANE, LANE)), _sds((n_tok, D_MODEL)), _sds((1, D_MODEL))],
    )(x, nf, target)


GRAD_WIRE = BF16


def _matmul_tn(a, b, name):
    n_tok, k1 = a.shape
    k2 = b.shape[1]
    t1 = _tile_of(k1)

    def body(a_ref, b_ref, o_ref):
        o_ref[...] = _tn(a_ref[...], b_ref[...]).astype(GRAD_WIRE)

    return _call(body, name, (k1 // t1,), [pl.BlockSpec((n_tok, t1), lambda i: (0, i)), _const((n_tok, k2))],
                 [pl.BlockSpec((t1, k2), lambda i: (i, 0))], [_sds((k1, k2), GRAD_WIRE)])(a, b)[0]


def _matmul_tn_lhs_blocks(a, b, width, keep, name):
    n_tok, k1 = a.shape
    k2 = b.shape[1]

    def body(a_ref, b_ref, o_ref):
        o_ref[...] = _tn(a_ref[...], b_ref[...])[:keep, :].astype(GRAD_WIRE)

    return _call(body, name, (k1 // width,), [pl.BlockSpec((n_tok, width), lambda d: (0, d)), _const((n_tok, k2))],
                 [pl.BlockSpec((None, keep, k2), lambda d: (d, 0, 0))], [_sds((k1 // width, keep, k2), GRAD_WIRE)])(a, b)[0]


def _matmul_tn_pair(a0, a1, b, name):
    n_tok, k1 = a0.shape
    k2 = b.shape[1]
    t1 = _tile_of(k1)
    n1 = k1 // t1

    def body(a0_ref, a1_ref, b_ref, o_ref):
        @pl.when(pl.program_id(0) < n1)
        def _():
            o_ref[...] = _tn(a0_ref[...], b_ref[...]).astype(GRAD_WIRE)

        @pl.when(pl.program_id(0) >= n1)
        def _():
            o_ref[...] = _tn(a1_ref[...], b_ref[...]).astype(GRAD_WIRE)

    return _call(
        body, name, (2 * n1,),
        [pl.BlockSpec((n_tok, t1), lambda i: (0, jnp.minimum(i, n1 - 1))),
         pl.BlockSpec((n_tok, t1), lambda i: (0, jnp.maximum(i - n1, 0))), _const((n_tok, k2))],
        [pl.BlockSpec((None, t1, k2), lambda i: (i // n1, i % n1, 0))], [_sds((2, k1, k2), GRAD_WIRE)])(a0, a1, b)[0]


W_IN_BLOCK = IN_PROJ // N_DEV
W_IN_SPLITS = (D_MODEL, 2 * D_MODEL, 2 * D_MODEL + SSD_CONV_DIM)
RELAYOUT_TILE = 256


def _w_in_split(blocks, after, name):
    tr = RELAYOUT_TILE

    def body(b_ref, after_ref, wu_ref, wz_ref, wx_ref, wd_ref):
        full = jnp.concatenate([b_ref[d] for d in range(N_DEV)], axis=1)
        wu_ref[...] = full[:, :W_IN_SPLITS[0]]
        wz_ref[...] = full[:, W_IN_SPLITS[0]:W_IN_SPLITS[1]]
        wx_ref[...] = full[:, W_IN_SPLITS[1]:W_IN_SPLITS[2]]
        wd_ref[...] = jnp.concatenate([full[:, W_IN_SPLITS[2]:], jnp.zeros((tr, LANE - SSD_HEADS), full.dtype)], axis=1)

    return _call(
        body, name, (D_MODEL // tr,), [pl.BlockSpec((N_DEV, tr, W_IN_BLOCK), lambda i: (0, i, 0)), HBM],
        [_rows(tr, D_MODEL), _rows(tr, D_MODEL), _rows(tr, SSD_CONV_DIM), _rows(tr, LANE)],
        [_sds((D_MODEL, D_MODEL), BF16), _sds((D_MODEL, D_MODEL), BF16), _sds((D_MODEL, SSD_CONV_DIM), BF16),
         _sds((D_MODEL, LANE), BF16)],
    )(blocks, after)


def _w_in_grad_blocks(gu, gz, gx, gdt, name):
    tr = RELAYOUT_TILE

    def body(gu_ref, gz_ref, gx_ref, gdt_ref, o_ref):
        full = jnp.concatenate([gu_ref[...], gz_ref[...], gx_ref[...], gdt_ref[...]], axis=1)
        for d in range(N_DEV):
            o_ref[d] = full[:, d * W_IN_BLOCK:(d + 1) * W_IN_BLOCK]

    return _call(
        body, name, (D_MODEL // tr,),
        [_rows(tr, D_MODEL), _rows(tr, D_MODEL), _rows(tr, SSD_CONV_DIM), _rows(tr, LANE)],
        [pl.BlockSpec((N_DEV, tr, W_IN_BLOCK), lambda i: (0, i, 0))], [_sds((N_DEV, D_MODEL, W_IN_BLOCK), gu.dtype)],
    )(gu, gz, gx, gdt)[0]


S5_SLICES = D_MODEL // LANE
S5_SLICE_STATES = S5_LANES // S5_SLICES
SCAN_LANES = 512


def _s5_scan(br_ref, bi_ref, a_r, a_i, car_r, car_i, ini_r, ini_i, reverse, xr_ref=None, xi_ref=None,
             acc_r=None, acc_i=None):
    n_rows = br_ref.shape[1]
    seg = n_rows // SUBLANE
    order = range(SUBLANE - 1, -1, -1) if reverse else range(SUBLANE)

    def rows(t):
        return pl.ds(pl.multiple_of(((seg - 1 - t) if reverse else t) * SUBLANE, SUBLANE), SUBLANE)

    tiles_per = SCAN_LANES // LANE

    def load(ref, t, lb):
        return jnp.concatenate([ref[lb * tiles_per + j, rows(t), :] for j in range(tiles_per)], axis=1)

    def store(ref, t, lb, val):
        for j in range(tiles_per):
            ref[lb * tiles_per + j, rows(t), :] = val[:, j * LANE:(j + 1) * LANE]

    for lb in range(S5_LANES // SCAN_LANES):
        lanes = pl.ds(lb * SCAN_LANES, SCAN_LANES)
        ar1, ai1 = a_r[:, lb * SCAN_LANES:(lb + 1) * SCAN_LANES], a_i[:, lb * SCAN_LANES:(lb + 1) * SCAN_LANES]
        ar8 = jnp.broadcast_to(ar1, (SUBLANE, SCAN_LANES))
        ai8 = jnp.broadcast_to(ai1, (SUBLANE, SCAN_LANES))

        def local(t, c):
            sr, si = c
            return (ar8 * sr - ai8 * si + load(br_ref, t, lb), ar8 * si + ai8 * sr + load(bi_ref, t, lb))

        zero = jnp.zeros((SUBLANE, SCAN_LANES), F32)
        er, ei = lax.fori_loop(0, seg, local, (zero, zero))
        pr, pi = ar1, ai1
        for _ in range(seg.bit_length() - 1):
            pr, pi = pr * pr - pi * pi, 2.0 * pr * pi
        cr, ci = car_r[:, lanes], car_i[:, lanes]
        for s in order:
            ini_r[s:s + 1, lanes] = cr
            ini_i[s:s + 1, lanes] = ci
            cr, ci = pr * cr - pi * ci + er[s:s + 1, :], pr * ci + pi * cr + ei[s:s + 1, :]
        car_r[:, lanes] = cr
        car_i[:, lanes] = ci

        if xr_ref is None:
            def final(t, c):
                sr, si = c
                nr = ar8 * sr - ai8 * si + load(br_ref, t, lb)
                ni = ar8 * si + ai8 * sr + load(bi_ref, t, lb)
                store(br_ref, t, lb, nr)
                store(bi_ref, t, lb, ni)
                return nr, ni

            lax.fori_loop(0, seg, final, (ini_r[:, lanes], ini_i[:, lanes]))
        else:
            def final_acc(t, c):
                sr, si, gr, gi = c
                xr, xi = load(xr_ref, t, lb), load(xi_ref, t, lb)
                gr = gr + sr * xr + si * xi
                gi = gi + si * xr - sr * xi
                nr = ar8 * sr - ai8 * si + load(br_ref, t, lb)
                ni = ar8 * si + ai8 * sr + load(bi_ref, t, lb)
                store(br_ref, t, lb, nr)
                store(bi_ref, t, lb, ni)
                return nr, ni, gr, gi

            _, _, gr, gi = lax.fori_loop(0, seg, final_acc,
                                         (ini_r[:, lanes], ini_i[:, lanes], acc_r[:, lanes], acc_i[:, lanes]))
            acc_r[:, lanes] = gr
            acc_i[:, lanes] = gi


def _s5_tail(gg, q, sn):
    return _rms(gg * _sigmoid(q), sn)


def _scan_order(n_rows):
    seg = n_rows // SUBLANE
    r = lax.broadcasted_iota(jnp.int32, (n_rows, n_rows), 0)
    c = lax.broadcasted_iota(jnp.int32, (n_rows, n_rows), 1)
    return (c == (r % SUBLANE) * seg + r // SUBLANE).astype(F32)


S5_STATE_TILES = S5_LANES // LANE
TILES_PER_SLICE = S5_SLICE_STATES // LANE


def _put_states(ref, k, val):
    for j in range(TILES_PER_SLICE):
        ref[k * TILES_PER_SLICE + j] = val[:, j * LANE:(j + 1) * LANE]


def _get_states(ref, k):
    return jnp.concatenate([ref[k * TILES_PER_SLICE + j] for j in range(TILES_PER_SLICE)], axis=1)


def _state_rows(tile, n_tiles=None):
    if n_tiles is None:
        return pl.BlockSpec((S5_STATE_TILES, tile, LANE), lambda i: (0, i, 0))
    return pl.BlockSpec((S5_STATE_TILES, tile, LANE), lambda i: (0, n_tiles - 1 - i, 0))


def _s5_fwd(u, a_r, a_i, bdb, bcr, bci, dsk, wglu, bglu, sn, name):
    n_tok = u.shape[0]
    tc = S5_TILE
    sw = S5_SLICE_STATES

    def body(u_ref, ar_ref, ai_ref, bdb_ref, bcr_ref, bci_ref, d_ref, wg_ref, bg_ref, sn_ref,
             ya_ref, xr_ref, xi_ref, v_ref, car_r, car_i, ini_r, ini_i):
        @pl.when(pl.program_id(0) == 0)
        def _():
            car_r[...] = jnp.zeros_like(car_r)
            car_i[...] = jnp.zeros_like(car_i)

        order = _scan_order(tc)
        u_t = _nn_f32(order, u_ref[...])
        ub = u_t.astype(BF16)
        for k in range(S5_SLICES):
            bu = _nn(ub[:, k * LANE:(k + 1) * LANE], bdb_ref[k])
            _put_states(xr_ref, k, bu[:, :sw])
            _put_states(xi_ref, k, bu[:, sw:])
        _s5_scan(xr_ref, xi_ref, ar_ref[...], ai_ref[...], car_r, car_i, ini_r, ini_i, reverse=False)
        vs = [_nn(_get_states(xr_ref, k), bcr_ref[k]) - _nn(_get_states(xi_ref, k), bci_ref[k])
              for k in range(S5_SLICES)]
        v = jnp.concatenate(vs, axis=1) + d_ref[...] * u_t
        v_ref[...] = v
        gg = _gelu(v)
        ya_ref[...] = _tn_f32(order, _s5_tail(gg, _nn(gg, wg_ref[...]) + bg_ref[...], sn_ref[...])).astype(BF16)

    return _call(
        body, name, (n_tok // tc,),
        [_rows(tc, D_MODEL), _const((1, S5_LANES)), _const((1, S5_LANES)), _const(bdb.shape), _const(bcr.shape),
         _const(bci.shape), _const((1, D_MODEL)), _const(wglu.shape), _const((1, D_MODEL)), _const((1, D_MODEL))],
        [_rows(tc, D_MODEL), _state_rows(tc), _state_rows(tc), _rows(tc, D_MODEL)],
        [_sds((n_tok, D_MODEL), BF16), _sds((S5_STATE_TILES, n_tok, LANE)), _sds((S5_STATE_TILES, n_tok, LANE)),
         _sds((n_tok, D_MODEL))],
        scratch=[pltpu.VMEM((1, S5_LANES), F32), pltpu.VMEM((1, S5_LANES), F32),
                 pltpu.VMEM((SUBLANE, S5_LANES), F32), pltpu.VMEM((SUBLANE, S5_LANES), F32)],
    )(u, a_r, a_i, bdb, bcr, bci, dsk, wglu, bglu, sn)


def _s5_bwd(dya, v, u, xr, xi, a_r, a_i, bdb, bcr, bci, dsk, wglu, bglu, sn, after, name):
    n_tok = u.shape[0]
    tc = S5_TILE
    nt = n_tok // tc
    sw = S5_SLICE_STATES

    def body(dya_ref, v_ref, u_ref, xr_ref, xi_ref, ar_ref, ai_ref, bdb_ref, bcr_ref, bci_ref, d_ref, wg_ref, bg_ref, sn_ref,
             after_ref, du_ref, gg_ref, dq_ref, gbdb_ref, gbcr_ref, gbci_ref, gar_ref, gai_ref, gd_ref, gbg_ref, gsn_ref,
             gr_ref, gi_ref, car_r, car_i, ini_r, ini_i):
        @pl.when(pl.program_id(0) == 0)
        def _():
            for r in (car_r, car_i, gbdb_ref, gbcr_ref, gbci_ref, gar_ref, gai_ref, gd_ref, gbg_ref, gsn_ref):
                r[...] = jnp.zeros_like(r)

        order = _scan_order(tc)
        u_t = _nn_f32(order, u_ref[...])
        gg, gelu_vjp = jax.vjp(_gelu, v_ref[...])
        _, tail_vjp = jax.vjp(_s5_tail, gg, _nn(gg, wg_ref[...]) + bg_ref[...], sn_ref[...])
        dgg, dq, dsn = tail_vjp(_nn_f32(order, dya_ref[...]))
        (dv,) = gelu_vjp(dgg + _nt(dq, wg_ref[...]))
        gg_ref[...] = gg.astype(BF16)
        dq_ref[...] = dq.astype(BF16)
        gd_ref[...] += jnp.sum(dv * u_t, axis=0, keepdims=True)
        gbg_ref[...] += jnp.sum(dq, axis=0, keepdims=True)
        gsn_ref[...] += dsn
        dvb = dv.astype(BF16)
        for k in range(S5_SLICES):
            dvk = dvb[:, k * LANE:(k + 1) * LANE]
            _put_states(gr_ref, k, _nt(dvk, bcr_ref[k]))
            _put_states(gi_ref, k, -_nt(dvk, bci_ref[k]))
            gbcr_ref[k] += _tn(_get_states(xr_ref, k), dvk)
            gbci_ref[k] -= _tn(_get_states(xi_ref, k), dvk)
        _s5_scan(gr_ref, gi_ref, ar_ref[...], -ai_ref[...], car_r, car_i, ini_r, ini_i, reverse=True,
                 xr_ref=xr_ref, xi_ref=xi_ref, acc_r=gar_ref, acc_i=gai_ref)
        ub = u_t.astype(BF16)
        dus = []
        for k in range(S5_SLICES):
            gk_r, gk_i = _get_states(gr_ref, k).astype(BF16), _get_states(gi_ref, k).astype(BF16)
            bk = bdb_ref[k]
            dus.append(_nt(gk_r, bk[:, :sw]) + _nt(gk_i, bk[:, sw:]))
            uk = ub[:, k * LANE:(k + 1) * LANE]
            gbdb_ref[k, :, :sw] += _tn(uk, gk_r)
            gbdb_ref[k, :, sw:] += _tn(uk, gk_i)
        du_ref[...] = _tn_f32(order, jnp.concatenate(dus, axis=1) + d_ref[...] * dv).astype(BF16)

    rev = functools.partial(_rows, n_tiles=nt)
    return _call(
        body, name, (nt,),
        [rev(tc, D_MODEL), rev(tc, D_MODEL), rev(tc, D_MODEL), _state_rows(tc, nt), _state_rows(tc, nt),
         _const((1, S5_LANES)), _const((1, S5_LANES)), _const(bdb.shape), _const(bcr.shape), _const(bci.shape),
         _const((1, D_MODEL)), _const(wglu.shape), _const((1, D_MODEL)), _const((1, D_MODEL)), HBM],
        [rev(tc, D_MODEL), rev(tc, D_MODEL), rev(tc, D_MODEL), _full(bdb.shape), _full(bcr.shape), _full(bci.shape),
         _full((SUBLANE, S5_LANES)), _full((SUBLANE, S5_LANES)), _full((1, D_MODEL)), _full((1, D_MODEL)), _full((1, D_MODEL))],
        [_sds((n_tok, D_MODEL), BF16), _sds((n_tok, D_MODEL), BF16), _sds((n_tok, D_MODEL), BF16), _sds(bdb.shape), _sds(bcr.shape),
         _sds(bci.shape), _sds((SUBLANE, S5_LANES)), _sds((SUBLANE, S5_LANES)), _sds((1, D_MODEL)), _sds((1, D_MODEL)),
         _sds((1, D_MODEL))],
        scratch=[pltpu.VMEM((S5_STATE_TILES, tc, LANE), F32), pltpu.VMEM((S5_STATE_TILES, tc, LANE), F32),
                 pltpu.VMEM((1, S5_LANES), F32), pltpu.VMEM((1, S5_LANES), F32),
                 pltpu.VMEM((SUBLANE, S5_LANES), F32), pltpu.VMEM((SUBLANE, S5_LANES), F32)],
    )(dya, v, u, xr, xi, a_r, a_i, bdb, bcr, bci, dsk, wglu, bglu, sn, after)


SSD_WIDTH = SSD_HEADS * SSD_HEAD_DIM
SSD_GROUPS = 2
HEADS_PER_GROUP = SSD_HEADS // SSD_GROUPS


def _take(x, axis, start, size):
    n = x.shape[axis]

    def sl(v):
        return lax.slice_in_dim(v, start, start + size, axis=axis)

    @jax.custom_vjp
    def f(v):
        return sl(v)

    def bwd(_, g):
        parts = []
        if start:
            parts.append(jnp.zeros(g.shape[:axis] + (start,) + g.shape[axis + 1:], g.dtype))
        parts.append(g)
        if n - start - size:
            parts.append(jnp.zeros(g.shape[:axis] + (n - start - size,) + g.shape[axis + 1:], g.dtype))
        return (jnp.concatenate(parts, axis=axis) if len(parts) > 1 else g,)

    f.defvjp(lambda v: (sl(v), None), bwd)
    return f(x)


def _lane_of(x, h):
    col = lax.broadcasted_iota(jnp.int32, x.shape, 1)
    return jnp.sum(jnp.where(col == h, x, 0.0), axis=1, keepdims=True)


def _ssd_chunk(xc, z, dt, dtb, alog, dvec, gn, st, nn, nt, tn, cumsum, take):
    t_len = xc.shape[0]
    xa = _silu(xc)
    dtp = _softplus(dt + dtb)
    d_a = dtp * (-jnp.exp(alog))
    row = lax.broadcasted_iota(jnp.int32, (t_len, t_len), 0)
    col = lax.broadcasted_iota(jnp.int32, (t_len, t_len), 1)
    causal = row >= col
    cum = cumsum(causal.astype(F32), d_a)
    eye = (row == col).astype(F32)
    ys, sts = [], []
    for g in range(SSD_GROUPS):
        bg = take(xa, 1, SSD_WIDTH + g * SSD_STATE, SSD_STATE)
        cg = take(xa, 1, SSD_WIDTH + (SSD_GROUPS + g) * SSD_STATE, SSD_STATE)
        cb = nt(cg, bg)
        for r in range(HEADS_PER_GROUP):
            h = g * HEADS_PER_GROUP + r
            cc = _lane_of(cum, h)
            cr = jnp.sum(cc * eye, axis=0, keepdims=True)
            decay = jnp.exp(jnp.where(causal, cc - cr, -1e30))
            xh = take(xa, 1, h * SSD_HEAD_DIM, SSD_HEAD_DIM)
            xdt = xh * _lane_of(dtp, h)
            sth = take(st, 0, h * SSD_HEAD_DIM, SSD_HEAD_DIM)
            c_last = jnp.sum(jnp.where(row[:, :1] == t_len - 1, cc, 0.0), axis=0, keepdims=True)
            y = nn(cb * decay, xdt) + jnp.exp(cc) * nt(cg, sth) + _lane_of(dvec, h) * xh
            ys.append(y)
            sts.append(jnp.exp(c_last) * sth + tn(xdt * jnp.exp(c_last - cc), bg))
    y = jnp.concatenate(ys, axis=1) * _silu(z)
    return _rms(y, gn), jnp.concatenate(sts, axis=0)


def _shift_back(cur, prev, j):
    if j == 0:
        return cur
    row = lax.broadcasted_iota(jnp.int32, cur.shape, 0)
    return jnp.where(row < j, pltpu.roll(prev, j, 0), pltpu.roll(cur, j, 0))


def _shift_ahead(cur, nxt, j):
    if j == 0:
        return cur
    n = cur.shape[0]
    row = lax.broadcasted_iota(jnp.int32, cur.shape, 0)
    return jnp.where(row >= n - j, pltpu.roll(nxt, n - j, 0), pltpu.roll(cur, n - j, 0))


def _conv(cur, prev, w, b):
    out = b + w[SSD_CONV - 1:SSD_CONV, :] * cur
    for k in range(SSD_CONV - 1):
        out = out + w[k:k + 1, :] * _shift_back(cur, prev, SSD_CONV - 1 - k)
    return out


def _ssd_fwd(xbc, z, dt, conv_w, conv_b, dtb, alog, dvec, gn, name):
    n_tok = xbc.shape[0]
    tc = SSD_CHUNK
    nc = n_tok // tc
    st_rows = SSD_HEADS * SSD_HEAD_DIM

    def body(cur_ref, prev_ref, z_ref, dt_ref, w_ref, b_ref, dtb_ref, alog_ref, dvec_ref, gn_ref,
             yb_ref, stin_ref, st_ref):
        i = pl.program_id(0)

        @pl.when(i == 0)
        def _():
            st_ref[...] = jnp.zeros_like(st_ref)

        prev = jnp.where(i > 0, prev_ref[...], 0.0)
        xc = _conv(cur_ref[...], prev, w_ref[...], b_ref[...])
        st = st_ref[...]
        stin_ref[0] = st
        yb, st_new = _ssd_chunk(xc, z_ref[...], dt_ref[...], dtb_ref[...], alog_ref[...], dvec_ref[...], gn_ref[...], st,
                                _nn, _nt, _tn, _nn_f32, lambda v, axis, start, size: lax.slice_in_dim(v, start, start + size, axis=axis))
        yb_ref[...] = yb.astype(BF16)
        st_ref[...] = st_new

    return _call(
        body, name, (nc,),
        [_rows(tc, SSD_CONV_DIM), pl.BlockSpec((tc, SSD_CONV_DIM), lambda i: (jnp.maximum(i - 1, 0), 0)),
         _rows(tc, D_MODEL), _rows(tc, LANE), _const((SSD_CONV, SSD_CONV_DIM)), _const((1, SSD_CONV_DIM)),
         _const((1, LANE)), _const((1, LANE)), _const((1, LANE)), _const((1, D_MODEL))],
        [_rows(tc, D_MODEL), pl.BlockSpec((1, st_rows, SSD_STATE), lambda i: (i, 0, 0))],
        [_sds((n_tok, D_MODEL), BF16), _sds((nc, st_rows, SSD_STATE))],
        scratch=[pltpu.VMEM((st_rows, SSD_STATE), F32)],
    )(xbc, xbc, z, dt, conv_w, conv_b, dtb, alog, dvec, gn)


def _ssd_bwd(dyb, xbc, z, dt, stin, conv_w, conv_b, dtb, alog, dvec, gn, name):
    n_tok = xbc.shape[0]
    tc = SSD_CHUNK
    nc = n_tok // tc
    st_rows = SSD_HEADS * SSD_HEAD_DIM

    def body(dyb_ref, cur_ref, prev_ref, z_ref, dt_ref, stin_ref, w_ref, b_ref, dtb_ref, alog_ref, dvec_ref, gn_ref,
             dxbc_ref, dz_ref, ddt_ref, gw_ref, gb_ref, gdtb_ref, galog_ref, gdvec_ref, ggn_ref,
             dst_ref, dxc_next_ref):
        i = pl.program_id(0)

        @pl.when(i == 0)
        def _():
            for r in (dst_ref, dxc_next_ref, gw_ref, gb_ref, gdtb_ref, galog_ref, gdvec_ref, ggn_ref):
                r[...] = jnp.zeros_like(r)

        cur = cur_ref[...]
        prev = jnp.where(i < nc - 1, prev_ref[...], 0.0)
        w = w_ref[...]
        xc = _conv(cur, prev, w, b_ref[...])
        chunk = functools.partial(_ssd_chunk, nn=_nn_d, nt=_nt_d, tn=_tn_d, cumsum=_cumsum_rows, take=_take)
        _, vjp = jax.vjp(chunk, xc, z_ref[...], dt_ref[...], dtb_ref[...], alog_ref[...], dvec_ref[...], gn_ref[...],
                         stin_ref[0])
        dxc, dz, ddt, gdtb, galog, gdvec, ggn, dst = vjp((dyb_ref[...], dst_ref[...]))
        dst_ref[...] = dst
        dz_ref[...] = dz.astype(BF16)
        ddt_ref[...] = ddt.astype(BF16)
        gdtb_ref[...] += gdtb
        galog_ref[...] += galog
        gdvec_ref[...] += gdvec
        ggn_ref[...] += ggn
        dxc_next = dxc_next_ref[...]
        dxbc = w[SSD_CONV - 1:SSD_CONV, :] * dxc
        gws = []
        for k in range(SSD_CONV - 1):
            j = SSD_CONV - 1 - k
            dxbc = dxbc + w[k:k + 1, :] * _shift_ahead(dxc, dxc_next, j)
            gws.append(jnp.sum(dxc * _shift_back(cur, prev, j), axis=0, keepdims=True))
        gws.append(jnp.sum(dxc * cur, axis=0, keepdims=True))
        dxbc_ref[...] = dxbc.astype(BF16)
        gw_ref[...] += jnp.concatenate(gws, axis=0)
        gb_ref[...] += jnp.sum(dxc, axis=0, keepdims=True)
        dxc_next_ref[...] = dxc

    rev = functools.partial(_rows, n_tiles=nc)
    return _call(
        body, name, (nc,),
        [rev(tc, D_MODEL), rev(tc, SSD_CONV_DIM),
         pl.BlockSpec((tc, SSD_CONV_DIM), lambda i: (jnp.maximum(nc - 2 - i, 0), 0)),
         rev(tc, D_MODEL), rev(tc, LANE), pl.BlockSpec((1, st_rows, SSD_STATE), lambda i: (nc - 1 - i, 0, 0)),
         _const((SSD_CONV, SSD_CONV_DIM)), _const((1, SSD_CONV_DIM)), _const((1, LANE)), _const((1, LANE)),
         _const((1, LANE)), _const((1, D_MODEL))],
        [rev(tc, SSD_CONV_DIM), rev(tc, D_MODEL), rev(tc, LANE), _full((SSD_CONV, SSD_CONV_DIM)), _full((1, SSD_CONV_DIM)),
         _full((1, LANE)), _full((1, LANE)), _full((1, LANE)), _full((1, D_MODEL))],
        [_sds((n_tok, SSD_CONV_DIM), BF16), _sds((n_tok, D_MODEL), BF16), _sds((n_tok, LANE), BF16), _sds((SSD_CONV, SSD_CONV_DIM)),
         _sds((1, SSD_CONV_DIM)), _sds((1, LANE)), _sds((1, LANE)), _sds((1, LANE)), _sds((1, D_MODEL))],
        scratch=[pltpu.VMEM((st_rows, SSD_STATE), F32), pltpu.VMEM((tc, SSD_CONV_DIM), F32)],
    )(dyb, xbc, xbc, z, dt, stin, conv_w, conv_b, dtb, alog, dvec, gn)


@jax.custom_vjp
def _expand_cols(x, e):
    return _nn_f32(x, e)


_expand_cols.defvjp(
    lambda x, e: (_nn_f32(x, e), e),
    lambda e, g: (lax.dot_general(g, e, (((1,), (1,)), ((), ())), precision=lax.Precision.HIGHEST,
                                  preferred_element_type=F32), jnp.zeros_like(e)))


def _s5_discretize(lam_re, lam_im, log_step, b_re, b_im, expand):
    step = jnp.exp(log_step)
    mag = jnp.exp(lam_re * step)
    ang = lam_im * step
    a_r = mag * jnp.cos(ang)
    a_i = mag * jnp.sin(ang)
    den = lam_re * lam_re + lam_im * lam_im
    n_r = a_r - 1.0
    coef_r = _expand_cols((n_r * lam_re + a_i * lam_im) / den, expand)
    coef_i = _expand_cols((a_i * lam_re - n_r * lam_im) / den, expand)
    return a_r, a_i, coef_r * b_re - coef_i * b_im, coef_r * b_im + coef_i * b_re


def _expand_matrix():
    p = lax.broadcasted_iota(jnp.int32, (S5_STATE, S5_STATE * S5_GROUP), 0)
    c = lax.broadcasted_iota(jnp.int32, (S5_STATE, S5_STATE * S5_GROUP), 1)
    return (c // S5_GROUP == p).astype(F32)


def _s5_discretize_fwd(lam_re, lam_im, log_step, b_re, b_im, name):
    def body(lr_ref, li_ref, ls_ref, br_ref, bi_ref, ar_ref, ai_ref, bbr_ref, bbi_ref):
        outs = _s5_discretize(lr_ref[...], li_ref[...], ls_ref[...], br_ref[...], bi_ref[...], _expand_matrix())
        for r, o in zip((ar_ref, ai_ref, bbr_ref, bbi_ref), outs):
            r[...] = o

    sq, wide = (S5_GROUPS, S5_STATE), (S5_GROUPS, S5_STATE * S5_GROUP)
    return _call(body, name, (1,), [_full(sq), _full(sq), _full((S5_GROUPS, 1)), _full(wide), _full(wide)],
                 [_full(sq), _full(sq), _full(wide), _full(wide)], [_sds(sq), _sds(sq), _sds(wide), _sds(wide)],
                 )(lam_re, lam_im, log_step, b_re, b_im)


def _s5_discretize_bwd(lam_re, lam_im, log_step, b_re, b_im, g_ar8, g_ai8, g_bbr, g_bbi, name):
    def body(lr_ref, li_ref, ls_ref, br_ref, bi_ref, gar_ref, gai_ref, gbbr_ref, gbbi_ref,
             glr_ref, gli_ref, gls_ref, gbr_ref, gbi_ref):
        _, vjp = jax.vjp(functools.partial(_s5_discretize, expand=_expand_matrix()),
                         lr_ref[...], li_ref[...], ls_ref[...], br_ref[...], bi_ref[...])
        grads = vjp((jnp.sum(gar_ref[...], axis=0), jnp.sum(gai_ref[...], axis=0), gbbr_ref[...], gbbi_ref[...]))
        for r, g in zip((glr_ref, gli_ref, gls_ref, gbr_ref, gbi_ref), grads):
            r[...] = g

    sq, wide, col = (S5_GROUPS, S5_STATE), (S5_GROUPS, S5_STATE * S5_GROUP), (S5_GROUPS, 1)
    part = (SUBLANE,) + sq
    return _call(body, name, (1,),
                 [_full(sq), _full(sq), _full(col), _full(wide), _full(wide), _full(part), _full(part), _full(wide), _full(wide)],
                 [_full(sq), _full(sq), _full(col), _full(wide), _full(wide)],
                 [_sds(sq), _sds(sq), _sds(col), _sds(wide), _sds(wide)],
                 )(lam_re, lam_im, log_step, b_re, b_im, g_ar8, g_ai8, g_bbr, g_bbi)


GROUPS_PER_SLICE = LANE // S5_GROUP


def _block_diag_b(bb):
    t = bb.reshape(S5_SLICES, GROUPS_PER_SLICE, S5_STATE, S5_GROUP)
    eye = jnp.eye(GROUPS_PER_SLICE, dtype=bb.dtype)
    return jnp.einsum("kgph,gf->kghfp", t, eye).reshape(S5_SLICES, LANE, S5_SLICE_STATES)


def _block_diag_b_inv(m):
    t = m.reshape(S5_SLICES, GROUPS_PER_SLICE, S5_GROUP, GROUPS_PER_SLICE, S5_STATE)
    return jnp.einsum("kghgp->kgph", t).reshape(S5_GROUPS, S5_STATE * S5_GROUP)


def _block_diag_c(c):
    t = c.reshape(S5_SLICES, GROUPS_PER_SLICE, S5_GROUP, S5_STATE)
    eye = jnp.eye(GROUPS_PER_SLICE, dtype=c.dtype)
    return jnp.einsum("kghp,gf->kgpfh", t, eye).reshape(S5_SLICES, S5_SLICE_STATES, LANE)


def _block_diag_c_inv(m):
    t = m.reshape(S5_SLICES, GROUPS_PER_SLICE, S5_STATE, GROUPS_PER_SLICE, S5_GROUP)
    return jnp.einsum("kgpgh->kghp", t).reshape(S5_GROUPS, S5_GROUP, S5_STATE)


def _pad_lanes(v):
    return jnp.pad(v.reshape(1, -1), ((0, 0), (0, LANE - v.shape[0])))


def _prepare_layer(w, blk, i, after):
    p = {}
    p["wu"], p["wz"], p["wx"], p["wd"] = _w_in_split(blk["w_in"], after, name=f"w_in_split_{i}")
    p["nm"] = w["norm_mix"][i].reshape(1, D_MODEL)
    p["lam_re"], p["lam_im"] = w["s5_lam_re"][i], w["s5_lam_im"][i]
    p["log_step"] = w["s5_log_step"][i].reshape(S5_GROUPS, 1)
    p["b_re"] = w["s5_b_re"][i].reshape(S5_GROUPS, S5_STATE * S5_GROUP)
    p["b_im"] = w["s5_b_im"][i].reshape(S5_GROUPS, S5_STATE * S5_GROUP)
    a_r, a_i, bb_r, bb_i = _s5_discretize_fwd(p["lam_re"], p["lam_im"], p["log_step"], p["b_re"], p["b_im"],
                                              name=f"s5_discretize_{i}")
    p["a_r"], p["a_i"] = a_r.reshape(1, S5_LANES), a_i.reshape(1, S5_LANES)
    p["bdb"] = jnp.concatenate([_block_diag_b(bb_r), _block_diag_b(bb_i)], axis=2).astype(BF16)
    p["bcr"] = _block_diag_c(w["s5_c_re"][i]).astype(BF16)
    p["bci"] = _block_diag_c(w["s5_c_im"][i]).astype(BF16)
    p["dsk"] = w["s5_d"][i].reshape(1, D_MODEL)
    p["wglu"] = blk["s5_w_glu"].reshape(D_MODEL, D_MODEL)
    p["bglu"] = w["s5_b_glu"][i].reshape(1, D_MODEL)
    p["sn"] = w["s5_norm"][i].reshape(1, D_MODEL)
    p["conv_w"] = blk["ssd_conv_w"]
    p["conv_b"] = w["ssd_conv_b"][i].reshape(1, SSD_CONV_DIM)
    p["dtb"] = _pad_lanes(w["ssd_dt_bias"][i])
    p["alog"] = _pad_lanes(w["ssd_a_log"][i])
    p["dvec"] = _pad_lanes(w["ssd_d"][i])
    p["gn"] = w["ssd_norm"][i].reshape(1, D_MODEL)
    p["wo"] = blk["w_out"].reshape(2 * D_MODEL, D_MODEL)
    p["nf"] = w["norm_ffn"][i].reshape(1, D_MODEL)
    p["wg"], p["wup"], p["wdn"] = (blk[n].reshape(FFN_PAD, D_MODEL) for n in ("w_gate", "w_up", "w_down"))
    return p


def _layer_fwd(x0, p, i):
    u, z, xbc, dt = _inproj_fwd(x0, p["nm"], p["wu"], p["wz"], p["wx"], p["wd"], name=f"inproj_fwd_{i}")
    ya, xr, xi, v = _s5_fwd(u, p["a_r"], p["a_i"], p["bdb"], p["bcr"], p["bci"], p["dsk"], p["wglu"], p["bglu"], p["sn"],
                            name=f"s5_fwd_{i}")
    yb, stin = _ssd_fwd(xbc, z, dt, p["conv_w"], p["conv_b"], p["dtb"], p["alog"], p["dvec"], p["gn"], name=f"ssd_fwd_{i}")
    x1, x2 = _mix_ffn_fwd(x0, ya, yb, p["wo"], p["nf"], p["wg"], p["wup"], p["wdn"], name=f"mix_ffn_fwd_{i}")
    return x2, dict(x0=x0, u=u, z=z, xbc=xbc, dt=dt, xr=xr, xi=xi, v=v, stin=stin, ya=ya, yb=yb, x1=x1)


def _layer_bwd(dx2, s, p, i, after, between=None):
    g = {}
    dx1, dya, dyb, h2, act, dgt, dup, dx2b, dx1b, g_nf = _mix_ffn_bwd(
        s["x1"], dx2, p["wo"], p["nf"], p["wg"], p["wup"], p["wdn"], after, name=f"mix_ffn_bwd_{i}")
    g["norm_ffn"] = g_nf.reshape(D_MODEL)
    g["w_down"] = _matmul_tn_lhs_blocks(act, dx2b, FFN_BLOCK_PAD, FFN_BLOCK, name=f"grad_w_down_{i}")
    g["w_gate"] = _matmul_tn_lhs_blocks(dgt, h2, FFN_BLOCK_PAD, FFN_BLOCK, name=f"grad_w_gate_{i}")
    g["w_up"] = _matmul_tn_lhs_blocks(dup, h2, FFN_BLOCK_PAD, FFN_BLOCK, name=f"grad_w_up_{i}")
    g["w_out"] = _matmul_tn_pair(s["ya"], s["yb"], dx1b, name=f"grad_w_out_{i}").reshape(N_DEV, 2 * D_MODEL // N_DEV, D_MODEL)
    if between is not None:
        after = between(g)

    (du, gg, dq, g_bdb, g_bcr, g_bci, g_ar8, g_ai8, g_d, g_bglu, g_sn) = _s5_bwd(
        dya, s["v"], s["u"], s["xr"], s["xi"], p["a_r"], p["a_i"], p["bdb"], p["bcr"], p["bci"], p["dsk"], p["wglu"],
        p["bglu"], p["sn"], after, name=f"s5_bwd_{i}")
    g["s5_w_glu"] = _matmul_tn(gg, dq, name=f"grad_w_glu_{i}").reshape(N_DEV, D_MODEL // N_DEV, D_MODEL)
    g["s5_d"], g["s5_b_glu"], g["s5_norm"] = g_d.reshape(D_MODEL), g_bglu.reshape(D_MODEL), g_sn.reshape(D_MODEL)
    g["s5_c_re"], g["s5_c_im"] = _block_diag_c_inv(g_bcr), _block_diag_c_inv(g_bci)
    sq = (SUBLANE, S5_GROUPS, S5_STATE)
    g_lr, g_li, g_ls, g_br, g_bi = _s5_discretize_bwd(
        p["lam_re"], p["lam_im"], p["log_step"], p["b_re"], p["b_im"], g_ar8.reshape(sq), g_ai8.reshape(sq),
        _block_diag_b_inv(g_bdb[:, :, :S5_SLICE_STATES]), _block_diag_b_inv(g_bdb[:, :, S5_SLICE_STATES:]),
        name=f"s5_discretize_bwd_{i}")
    g["s5_lam_re"], g["s5_lam_im"], g["s5_log_step"] = g_lr, g_li, g_ls.reshape(S5_GROUPS)
    b_shape = (S5_GROUPS, S5_STATE, S5_GROUP)
    g["s5_b_re"], g["s5_b_im"] = g_br.reshape(b_shape), g_bi.reshape(b_shape)

    dxbc, dz, ddt, g_cw, g_cb, g_dtb, g_alog, g_dvec, g_gn = _ssd_bwd(
        dyb, s["xbc"], s["z"], s["dt"], s["stin"], p["conv_w"], p["conv_b"], p["dtb"], p["alog"], p["dvec"], p["gn"],
        name=f"ssd_bwd_{i}")
    g["ssd_conv_w"] = jnp.moveaxis(g_cw.reshape(SSD_CONV, N_DEV, SSD_CONV_DIM // N_DEV), 1, 0)
    g["ssd_conv_b"] = g_cb.reshape(SSD_CONV_DIM)
    g["ssd_dt_bias"], g["ssd_a_log"], g["ssd_d"] = g_dtb[0, :SSD_HEADS], g_alog[0, :SSD_HEADS], g_dvec[0, :SSD_HEADS]
    g["ssd_norm"] = g_gn.reshape(D_MODEL)

    dx0, h, g_nm = _inproj_bwd(s["x0"], p["nm"], du, dz, dxbc, ddt, dx1, p["wu"], p["wz"], p["wx"], p["wd"],
                               name=f"inproj_bwd_{i}")
    g["norm_mix"] = g_nm.reshape(D_MODEL)
    g["w_in"] = _w_in_grad_blocks(
        _matmul_tn(h, du, name=f"grad_w_in_u_{i}"), _matmul_tn(h, dz, name=f"grad_w_in_z_{i}"),
        _matmul_tn(h, dxbc, name=f"grad_w_in_xbc_{i}"), _matmul_tn(h, ddt, name=f"grad_w_in_dt_{i}"),
        name=f"grad_w_in_blocks_{i}")
    return dx0, g


def _example_step(x, target, w, blks):
    prepared = [_prepare_layer(w, blks[i], i, x) for i in range(DEPTH)]
    saved = []
    h = x
    for i in range(DEPTH):
        h, s = _layer_fwd(h, prepared[i], i)
        saved.append(s)
    loss, dh, g_final = _loss_head(h, w["norm_final"].reshape(1, D_MODEL), target, name="loss_head")
    layer_grads = [None] * DEPTH
    for i in reversed(range(DEPTH)):
        dh, layer_grads[i] = _layer_bwd(dh, saved[i], prepared[i], i, x)
    return loss, dh, layer_grads, g_final.reshape(D_MODEL)


def _mesh_position():
    return lax.axis_index("x"), lax.axis_index("y"), lax.axis_index("c")


def _peer(pos, k):
    x, y, c = pos
    px = 1 - x if k & 4 else x
    py = 1 - y if k & 2 else y
    pc = 1 - c if k & 1 else c
    return (px, py, pc), 4 * px + 2 * py + pc


HBM = pl.BlockSpec(memory_space=pl.ANY)


def _run_copies(local, remote):
    for cp in local + remote:
        cp.start()
    for cp in remote:
        cp.wait_recv()
    for cp in remote:
        cp.wait_send()
    for cp in local:
        cp.wait()


def _comm_scratch(n_units):
    return [pltpu.SemaphoreType.DMA((n_units, N_DEV - 1)), pltpu.SemaphoreType.DMA((n_units, N_DEV - 1)),
            pltpu.SemaphoreType.DMA((n_units,))]


def _gather_blocks(arrays, layered, name):
    units, out_shapes = [], []
    for j, (a, lay) in enumerate(zip(arrays, layered)):
        for layer in (range(a.shape[0]) if lay else (None,)):
            units.append((j, layer, len(out_shapes)))
            out_shapes.append(_sds((N_DEV,) + (a.shape[1:] if lay else a.shape), a.dtype))
    n_in = len(arrays)
    other_chips = (4, 2, 6)

    def body(*refs):
        ins, outs = refs[:n_in], refs[n_in:n_in + len(out_shapes)]
        send_sems, recv_sems, local_sems = refs[n_in + len(out_shapes):]
        pos = _mesh_position()
        me = 4 * pos[0] + 2 * pos[1] + pos[2]
        sibling, _ = _peer(pos, 1)
        local, own, passed = [], [], []
        for u, (j, layer, o) in enumerate(units):
            src = ins[j] if layer is None else ins[j].at[layer]
            local.append(pltpu.make_async_copy(src, outs[o].at[me], local_sems.at[u]))

            def copy(sem, src_ref, slot, to, u=u, o=o):
                return pltpu.make_async_remote_copy(
                    src_ref=src_ref, dst_ref=outs[o].at[slot], send_sem=send_sems.at[u, sem], recv_sem=recv_sems.at[u, sem],
                    device_id=to, device_id_type=MESH_ID)

            own.append([copy(0, src, me, sibling)] + [copy(1 + i, src, me, _peer(pos, k)[0]) for i, k in enumerate(other_chips)])
            passed.append([copy(4 + i, outs[o].at[_peer(pos, k)[1]], _peer(pos, k)[1], sibling) for i, k in enumerate(other_chips)])
        for cp in local + [c for unit in own for c in unit]:
            cp.start()
        for u in range(len(units)):
            for i in range(len(other_chips)):
                own[u][1 + i].wait_recv()
                passed[u][i].start()
        for u in range(len(units)):
            own[u][0].wait_recv()
            for cp in passed[u]:
                cp.wait_recv()
        for cp in [c for unit in own + passed for c in unit]:
            cp.wait_send()
        for cp in local:
            cp.wait()

    outs = pl.pallas_call(body, name=name, in_specs=[HBM] * n_in, out_specs=[HBM] * len(out_shapes), out_shape=out_shapes,
                          scratch_shapes=_comm_scratch(len(units)))(*arrays)
    grouped = [[] for _ in arrays]
    for j, _, o in units:
        grouped[j].append(outs[o])
    return [tuple(g) for g in grouped]


def _exchange_blocks(entries, name):
    units, flat_in, out_shapes = [], [], []
    for j, entry in enumerate(entries):
        for layer, a in enumerate(entry):
            units.append((len(flat_in), layer, j))
            flat_in.append(a)
        out_shapes.append(_sds((N_DEV, len(entry)) + entry[0].shape[1:], entry[0].dtype))
    n_in = len(flat_in)

    def body(*refs):
        ins, outs = refs[:n_in], refs[n_in:n_in + len(out_shapes)]
        send_sems, recv_sems, local_sems = refs[n_in + len(out_shapes):]
        pos = _mesh_position()
        me = 4 * pos[0] + 2 * pos[1] + pos[2]
        local, remote = [], []
        for u, (i, layer, o) in enumerate(units):
            local.append(pltpu.make_async_copy(ins[i].at[me], outs[o].at[me, layer], local_sems.at[u]))
            for k in range(1, N_DEV):
                peer, peer_index = _peer(pos, k)
                remote.append(pltpu.make_async_remote_copy(
                    src_ref=ins[i].at[peer_index], dst_ref=outs[o].at[me, layer], send_sem=send_sems.at[u, k - 1],
                    recv_sem=recv_sems.at[u, k - 1], device_id=peer, device_id_type=MESH_ID))
        _run_copies(local, remote)

    return pl.pallas_call(body, name=name, in_specs=[HBM] * n_in, out_specs=[HBM] * len(out_shapes), out_shape=out_shapes,
                          scratch_shapes=_comm_scratch(len(units)))(*flat_in)


SEM = pl.BlockSpec(memory_space=pltpu.SEMAPHORE)
SIDE_EFFECT = pltpu.SideEffectType.DATAFLOW_SIDE_EFFECTING


def _own_slots(arrays, indexed, me, name):
    lands = []
    for u, a in enumerate(arrays):
        block = a.shape[1:] if indexed else a.shape
        rows, cols = _size(block[:-1]), block[-1]
        tr = _row_tile(rows, cap=512)

        def body(me_ref, src_ref, out_ref):
            out_ref[...] = src_ref[...]

        src_spec = (pl.BlockSpec((None, tr, cols), lambda i, me_ref: (me_ref[0], i, 0)) if indexed
                    else pl.BlockSpec((tr, cols), lambda i, me_ref: (i, 0)))
        land = pl.pallas_call(
            body, name=f"{name}_{u}", out_shape=_sds((N_DEV, rows, cols), a.dtype),
            grid_spec=pltpu.PrefetchScalarGridSpec(
                num_scalar_prefetch=1, grid=(rows // tr,), in_specs=[src_spec],
                out_specs=pl.BlockSpec((None, tr, cols), lambda i, me_ref: (me_ref[0], i, 0))),
        )(me, a.reshape((N_DEV, rows, cols) if indexed else (rows, cols)))
        lands.append(land.reshape((N_DEV,) + block))
    return lands


def _split_copies(srcs, lands, send_sems, recv_sems, indexed):
    pos = _mesh_position()
    me = 4 * pos[0] + 2 * pos[1] + pos[2]
    copies = []
    for u, (src, land) in enumerate(zip(srcs, lands)):
        for k in range(1, N_DEV):
            peer, peer_index = _peer(pos, k)
            copies.append(pltpu.make_async_remote_copy(
                src_ref=src.at[peer_index] if indexed else src, dst_ref=land.at[me],
                send_sem=send_sems.at[u * (N_DEV - 1) + k - 1], recv_sem=recv_sems.at[u * (N_DEV - 1) + k - 1],
                device_id=peer, device_id_type=MESH_ID))
    return copies


def _exchange_start(arrays, lands, indexed, name):
    n = len(arrays)

    def body(*refs):
        srcs, zones = refs[:n], refs[n:2 * n]
        send_sems, recv_sems = refs[2 * n], refs[2 * n + 1]
        token = refs[-1]
        for cp in _split_copies(srcs, zones, send_sems, recv_sems, indexed):
            cp.start()
        token[...] = jnp.zeros_like(token)

    sem_shape = pltpu.SemaphoreType.DMA((n * (N_DEV - 1),))
    outs = pl.pallas_call(
        body, name=name, in_specs=[HBM] * (2 * n),
        out_specs=[SEM, SEM] + [HBM] * (2 * n) + [pl.BlockSpec(memory_space=pltpu.VMEM)],
        out_shape=[sem_shape, sem_shape] + [pltpu.HBM(a.shape, a.dtype) for a in list(arrays) + list(lands)]
        + [_sds((SUBLANE, LANE))],
        input_output_aliases={i: 2 + i for i in range(2 * n)},
        compiler_params=pltpu.CompilerParams(has_side_effects=SIDE_EFFECT),
    )(*[pltpu.with_memory_space_constraint(a, pltpu.HBM) for a in list(arrays) + list(lands)])
    return outs[0], outs[1], outs[2:2 + n], outs[2 + n:2 + 2 * n], outs[-1]


def _exchange_wait(send_sems, recv_sems, arrays, lands, after, indexed, name):
    n = len(arrays)

    def body(*refs):
        srcs, zones = refs[:n], refs[n:2 * n]
        s_sems, r_sems = refs[2 * n], refs[2 * n + 1]
        for cp in _split_copies(srcs, zones, s_sems, r_sems, indexed):
            cp.wait_send()
            cp.wait_recv()

    outs = pl.pallas_call(
        body, name=name, in_specs=[HBM] * (2 * n) + [SEM, SEM, HBM],
        out_specs=[HBM] * (2 * n), out_shape=[pltpu.HBM(a.shape, a.dtype) for a in list(arrays) + list(lands)],
        input_output_aliases={i: i for i in range(2 * n)},
        compiler_params=pltpu.CompilerParams(has_side_effects=SIDE_EFFECT),
    )(*arrays, *lands, send_sems, recv_sems, after)
    return outs[n:]


SUM_TILE = 512


def _adamw(w, g, m, v):
    m = ADAM_B1 * m + (1.0 - ADAM_B1) * g
    v = ADAM_B2 * v + (1.0 - ADAM_B2) * (g * g)
    m_hat = m / (1.0 - ADAM_B1 ** ADAM_STEP)
    v_hat = v / (1.0 - ADAM_B2 ** ADAM_STEP)
    return -ADAM_LR * (m_hat / (jnp.sqrt(v_hat) + ADAM_EPS) + ADAM_WD * w), m, v


def _sum_adamw(recv, w, m, v, layer, others, name):
    _, rows, cols = w.shape
    tr = _row_tile(rows, cap=256)

    def body(r_ref, w_ref, m_ref, v_ref, *rest):
        g_ref, d_ref, mo_ref, vo_ref = rest[-4:]
        g = r_ref[0].astype(F32)
        for j in range(1, N_DEV):
            g = g + r_ref[j].astype(F32)
        g_ref[...] = g
        d_ref[...], mo_ref[...], vo_ref[...] = _adamw(w_ref[...], g, m_ref[...], v_ref[...])

    blk = pl.BlockSpec((None, tr, cols), lambda i: (layer, i, 0))
    carried = list(others) if others is not None else []
    return pl.pallas_call(
        body, name=name, grid=(rows // tr,),
        in_specs=[pl.BlockSpec((N_DEV, tr, cols), lambda i: (0, i, 0)), blk, blk, blk] + [HBM] * len(carried),
        out_specs=[blk] * 4, out_shape=[_sds(w.shape)] * 4,
        input_output_aliases={4 + k: k for k in range(len(carried))},
        compiler_params=pltpu.CompilerParams(dimension_semantics=("arbitrary",), vmem_limit_bytes=VMEM_LIMIT),
    )(recv, w, m, v, *carried)


def _sum_senders(recv, name):
    _, rows, cols = recv.shape
    tr = _row_tile(rows, cap=256)

    def body(r_ref, g_ref):
        g = r_ref[0].astype(F32)
        for j in range(1, N_DEV):
            g = g + r_ref[j].astype(F32)
        g_ref[...] = g

    return _call(body, name, (rows // tr,), [pl.BlockSpec((N_DEV, tr, cols), lambda i: (0, i, 0))], [_rows(tr, cols)],
                 [_sds((rows, cols))])(recv)[0]


def _adamw_blocks(g, w, m, v, name):
    n_lay, rows, cols = w.shape
    tr = _row_tile(rows, cap=256)

    def body(g_ref, w_ref, m_ref, v_ref, d_ref, mo_ref, vo_ref):
        d_ref[...], mo_ref[...], vo_ref[...] = _adamw(w_ref[...], g_ref[...], m_ref[...], v_ref[...])

    blk = pl.BlockSpec((None, tr, cols), lambda l, i: (l, i, 0))
    return pl.pallas_call(
        body, name=name, grid=(n_lay, rows // tr), in_specs=[blk] * 4, out_specs=[blk] * 3, out_shape=[_sds(w.shape)] * 3,
        compiler_params=pltpu.CompilerParams(dimension_semantics=("arbitrary", "arbitrary"), vmem_limit_bytes=VMEM_LIMIT),
    )(g, w, m, v)


def _sum_slots(recv, name):
    rows = recv.shape[1]

    def body(r_ref, g_ref):
        g = r_ref[0].astype(F32)
        for j in range(1, N_DEV):
            g = g + r_ref[j].astype(F32)
        g_ref[...] = g

    return _call(body, name, (1,), [_full(recv.shape)], [_full((rows, LANE))], [_sds((rows, LANE))])(recv)[0]


def _adamw_rows(g, w, m, v, name):
    rows = w.shape[0]
    tr = _row_tile(rows)

    def body(g_ref, w_ref, m_ref, v_ref, d_ref, mo_ref, vo_ref):
        d_ref[...], mo_ref[...], vo_ref[...] = _adamw(w_ref[...], g_ref[...], m_ref[...], v_ref[...])

    flat = _rows(tr, LANE)
    return _call(body, name, (rows // tr,), [flat] * 4, [flat] * 3, [_sds((rows, LANE))] * 3)(g, w, m, v)


def _row_tile(rows, cap=1024):
    if rows % SUBLANE:
        return rows
    best = SUBLANE
    for t in range(SUBLANE, cap + 1, SUBLANE):
        if rows % t == 0:
            best = t
    return best


BIG = (("w_in", (DEPTH, D_MODEL, IN_PROJ // N_DEV), 2),
       ("s5_w_glu", (DEPTH, D_MODEL // N_DEV, D_MODEL), 1),
       ("ssd_conv_w", (DEPTH, SSD_CONV, SSD_CONV_DIM // N_DEV), 2),
       ("w_out", (DEPTH, 2 * D_MODEL // N_DEV, D_MODEL), 1),
       ("w_gate", (DEPTH, D_MODEL, FFN_HIDDEN // N_DEV), 2),
       ("w_up", (DEPTH, D_MODEL, FFN_HIDDEN // N_DEV), 2),
       ("w_down", (DEPTH, FFN_HIDDEN // N_DEV, D_MODEL), 1))
SMALL = (("norm_mix", (DEPTH, D_MODEL)), ("s5_lam_re", (DEPTH, S5_GROUPS, S5_STATE)), ("s5_lam_im", (DEPTH, S5_GROUPS, S5_STATE)),
         ("s5_log_step", (DEPTH, S5_GROUPS)), ("s5_b_re", (DEPTH, S5_GROUPS, S5_STATE, S5_GROUP)),
         ("s5_b_im", (DEPTH, S5_GROUPS, S5_STATE, S5_GROUP)), ("s5_c_re", (DEPTH, S5_GROUPS, S5_GROUP, S5_STATE)),
         ("s5_c_im", (DEPTH, S5_GROUPS, S5_GROUP, S5_STATE)), ("s5_d", (DEPTH, D_MODEL)), ("s5_b_glu", (DEPTH, D_MODEL)),
         ("s5_norm", (DEPTH, D_MODEL)), ("ssd_conv_b", (DEPTH, SSD_CONV_DIM)), ("ssd_dt_bias", (DEPTH, SSD_HEADS)),
         ("ssd_a_log", (DEPTH, SSD_HEADS)), ("ssd_d", (DEPTH, SSD_HEADS)), ("ssd_norm", (DEPTH, D_MODEL)),
         ("norm_ffn", (DEPTH, D_MODEL)), ("norm_final", (D_MODEL,)))
WEIGHT_ORDER = ("norm_mix", "w_in", "s5_lam_re", "s5_lam_im", "s5_log_step", "s5_b_re", "s5_b_im", "s5_c_re", "s5_c_im", "s5_d",
                "s5_w_glu", "s5_b_glu", "s5_norm", "ssd_conv_w", "ssd_conv_b", "ssd_dt_bias", "ssd_a_log", "ssd_d", "ssd_norm",
                "w_out", "norm_ffn", "w_gate", "w_up", "w_down", "norm_final")


def _size(shape):
    n = 1
    for s in shape:
        n *= s
    return n


def _round_up(n, m):
    return -(-n // m) * m


SMALL_SIZE = sum(_size(s) for _, s in SMALL)
SMALL_ROWS = _round_up(-(-SMALL_SIZE // (N_DEV * LANE)), SUBLANE)


def _pack(parts, rows, dtype):
    flat = jnp.concatenate([p.reshape(-1).astype(dtype) for p in parts])
    return jnp.pad(flat, (0, rows * LANE - flat.shape[0])).reshape(rows, LANE)


def _unpack(flat, specs):
    out, off = {}, 0
    flat = flat.reshape(-1)
    for name, shape in specs:
        out[name] = flat[off:off + _size(shape)].reshape(shape)
        off += _size(shape)
    return out


def kernel(x, norm_mix, w_in, s5_lam_re, s5_lam_im, s5_log_step, s5_b_re, s5_b_im, s5_c_re, s5_c_im, s5_d, s5_w_glu, s5_b_glu, s5_norm, ssd_conv_w, ssd_conv_b, ssd_dt_bias, ssd_a_log, ssd_d, ssd_norm, w_out, norm_ffn, w_gate, w_up, w_down, norm_final, loss_target, m_norm_mix, m_w_in, m_s5_lam_re, m_s5_lam_im, m_s5_log_step, m_s5_b_re, m_s5_b_im, m_s5_c_re, m_s5_c_im, m_s5_d, m_s5_w_glu, m_s5_b_glu, m_s5_norm, m_ssd_conv_w, m_ssd_conv_b, m_ssd_dt_bias, m_ssd_a_log, m_ssd_d, m_ssd_norm, m_w_out, m_norm_ffn, m_w_gate, m_w_up, m_w_down, m_norm_final, v_norm_mix, v_w_in, v_s5_lam_re, v_s5_lam_im, v_s5_log_step, v_s5_b_re, v_s5_b_im, v_s5_c_re, v_s5_c_im, v_s5_d, v_s5_w_glu, v_s5_b_glu, v_s5_norm, v_ssd_conv_w, v_ssd_conv_b, v_ssd_dt_bias, v_ssd_a_log, v_ssd_d, v_ssd_norm, v_w_out, v_norm_ffn, v_w_gate, v_w_up, v_w_down, v_norm_final):
    given = dict(locals())
    w = {n: given[n] for n in WEIGHT_ORDER}
    m = {n: given["m_" + n] for n in WEIGHT_ORDER}
    v = {n: given["v_" + n] for n in WEIGHT_ORDER}
    big_names = tuple(n for n, _, _ in BIG)
    matmul_names = tuple(n for n in big_names if n != "ssd_conv_w")

    conv_hi = w["ssd_conv_w"].astype(BF16)
    conv_lo = (w["ssd_conv_w"] - conv_hi.astype(F32)).astype(BF16)
    row_pad = ((0, 0), (0, FFN_BLOCK_PAD - FFN_BLOCK), (0, 0))
    as_rows = {"w_gate": jnp.swapaxes(w["w_gate"], 1, 2), "w_up": jnp.swapaxes(w["w_up"], 1, 2), "w_down": w["w_down"]}
    to_send = [jnp.pad(as_rows[n].astype(BF16), row_pad) if n in as_rows else w[n].astype(BF16) for n in matmul_names]

    def layer_blocks(i):
        return [a[i] for a in to_send] + [jnp.stack([conv_hi[i], conv_lo[i]])]

    def as_layer_weights(gathered):
        blk = dict(zip(matmul_names, gathered))
        pair = gathered[-1].astype(F32)
        blk["ssd_conv_w"] = jnp.moveaxis(pair[:, 0] + pair[:, 1], 0, 1).reshape(SSD_CONV, SSD_CONV_DIM)
        return blk

    gathered0 = [g[0] for g in _gather_blocks(layer_blocks(0), [False] * (len(matmul_names) + 1), name="gather_weights_0")]
    blocks1 = layer_blocks(1)
    me = (4 * lax.axis_index("x") + 2 * lax.axis_index("y") + lax.axis_index("c")).astype(jnp.int32).reshape(1)
    sems1 = _exchange_start(blocks1, _own_slots(blocks1, False, me, name="gather_own_1"), False, name="gather_start_1")
    prepared = [_prepare_layer(w, as_layer_weights(gathered0), 0, sems1[-1]), None]
    saved = [None, None]
    h, saved[0] = _layer_fwd(x[0], prepared[0], 0)
    gathered1 = _exchange_wait(*sems1[:4], h, False, name="gather_wait_1")
    prepared[1] = _prepare_layer(w, as_layer_weights(gathered1), 1, sems1[-1])
    h, saved[1] = _layer_fwd(h, prepared[1], 1)
    loss, dh, g_final = _loss_head(h, w["norm_final"].reshape(1, D_MODEL), loss_target[0], name="loss_head")

    layer_grads = [None, None]
    dh, layer_grads[1] = _layer_bwd(dh, saved[1], prepared[1], 1, sems1[-1])
    slots1 = [layer_grads[1][n] for n in big_names]
    sems2 = _exchange_start(slots1, _own_slots(slots1, True, me, name="exchange_own_1"), True, name="exchange_start_1")
    early_names = ("w_out", "w_gate", "w_up", "w_down")
    late_names = tuple(n for n in big_names if n not in early_names)
    early = {}

    def send_early(g):
        slots = [g[n] for n in early_names]
        early["sems"] = _exchange_start(slots, _own_slots(slots, True, me, name="exchange_own_0"), True, name="exchange_start_0")
        return early["sems"][-1]

    grad_x, layer_grads[0] = _layer_bwd(dh, saved[0], prepared[0], 0, sems2[-1], between=send_early)

    small = jnp.concatenate([g_final.reshape(-1) if n == "norm_final"
                             else jnp.stack([layer_grads[i][n] for i in range(DEPTH)]).reshape(-1) for n, _ in SMALL])
    small_slots = jnp.pad(small, (0, N_DEV * SMALL_ROWS * LANE - small.shape[0])).reshape(N_DEV, SMALL_ROWS, LANE)
    received_late = _exchange_blocks([(layer_grads[0][n],) for n in late_names] + [(small_slots,)], name="exchange_gradients_0")
    received1 = _exchange_wait(*sems2[:4], grad_x, True, name="exchange_wait_1")
    received_early = _exchange_wait(*early["sems"][:4], grad_x, True, name="exchange_wait_0")
    received0 = {n: r.reshape(r.shape[:1] + r.shape[2:]) for n, r in zip(late_names, received_late)}
    received0.update(zip(early_names, received_early))

    results = {}
    for j, n in enumerate(big_names):
        recv = (received0[n], received1[j])
        if n in ("w_gate", "w_up"):
            g = jnp.stack([jnp.swapaxes(_sum_senders(recv[i], name=f"sum_{n}_{i}"), 0, 1) for i in range(DEPTH)])
            results[n] = [g, *_adamw_blocks(g, w[n], m[n], v[n], name=f"adamw_{n}")]
        else:
            first = _sum_adamw(recv[1], w[n], m[n], v[n], 1, None, name=f"sum_adamw_{n}_1")
            results[n] = _sum_adamw(recv[0], w[n], m[n], v[n], 0, first, name=f"sum_adamw_{n}_0")
    g_part = _sum_slots(received_late[-1].reshape(N_DEV, SMALL_ROWS, LANE), name="sum_replicated")
    g_small = _gather_blocks([g_part], [False], name="gather_replicated")[0][0].reshape(N_DEV * SMALL_ROWS, LANE)
    small_rows = N_DEV * SMALL_ROWS
    d_small, m_small, v_small = _adamw_rows(
        g_small, _pack([w[n] for n, _ in SMALL], small_rows, F32), _pack([m[n] for n, _ in SMALL], small_rows, F32),
        _pack([v[n] for n, _ in SMALL], small_rows, F32), name="adamw_replicated")
    for k, packed in enumerate((g_small, d_small, m_small, v_small)):
        for n, arr in _unpack(packed, SMALL).items():
            results.setdefault(n, [None] * 4)[k] = arr

    outs = [results[n][k] for k in range(4) for n in WEIGHT_ORDER]
    total_loss = lax.psum(loss[0, 0], ("x", "y", "c"))
    return (total_loss, grad_x[None], *outs)
```

```python
import functools

import jax
import jax.numpy as jnp
from jax import lax
from jax.experimental import pallas as pl
from jax.experimental.pallas import tpu as pltpu

F32 = jnp.float32
BF16 = jnp.bfloat16
MESH_ID = pl.DeviceIdType.MESH

N_DEV = 8
DEPTH = 2
D_MODEL = 1024
S5_GROUPS = 64
S5_GROUP = 16
S5_STATE = 64
S5_LANES = S5_GROUPS * S5_STATE
SSD_HEADS = 16
SSD_HEAD_DIM = 64
SSD_STATE = 128
SSD_CHUNK = 128
SSD_CONV = 4
SSD_CONV_DIM = 1536
FFN_HIDDEN = 2816
IN_PROJ = 3600
EPS = 1e-6
LANE = 128
SUBLANE = 8
VMEM_LIMIT = 56 * 1024 * 1024

ADAM_LR = 0.001
ADAM_B1 = 0.9
ADAM_B2 = 0.999
ADAM_EPS = 1e-08
ADAM_WD = 0.01
ADAM_STEP = 10

TOK_TILE = 256
S5_TILE = 128


def _sigmoid(x):
    return jax.nn.sigmoid(x)


def _silu(x):
    return x * _sigmoid(x)


def _gelu(x):
    return 0.5 * x * (1.0 + jnp.tanh(0.7978845608028654 * (x + 0.044715 * (x * x * x))))


def _softplus(x):
    return jnp.maximum(x, 0.0) + jnp.log(1.0 + jnp.exp(-jnp.abs(x)))


def _rms(x, g):
    r = lax.rsqrt(jnp.mean(x * x, axis=-1, keepdims=True) + EPS)
    return x * r * g


def _nn(a, b):
    return lax.dot_general(a.astype(BF16), b.astype(BF16), (((1,), (0,)), ((), ())), preferred_element_type=F32)


def _nt(a, b):
    return lax.dot_general(a.astype(BF16), b.astype(BF16), (((1,), (1,)), ((), ())), preferred_element_type=F32)


def _tn(a, b):
    return lax.dot_general(a.astype(BF16), b.astype(BF16), (((0,), (0,)), ((), ())), preferred_element_type=F32)


def _nn_f32(a, b):
    return lax.dot_general(a, b, (((1,), (0,)), ((), ())), precision=lax.Precision.HIGHEST, preferred_element_type=F32)


def _tn_f32(a, b):
    return lax.dot_general(a, b, (((0,), (0,)), ((), ())), precision=lax.Precision.HIGHEST, preferred_element_type=F32)


@jax.custom_vjp
def _nn_d(a, b):
    return _nn(a, b)


_nn_d.defvjp(lambda a, b: (_nn(a, b), (a, b)), lambda r, g: (_nt(g, r[1]), _tn(r[0], g)))


@jax.custom_vjp
def _nt_d(a, b):
    return _nt(a, b)


_nt_d.defvjp(lambda a, b: (_nt(a, b), (a, b)), lambda r, g: (_nn(g, r[1]), _tn(g, r[0])))


@jax.custom_vjp
def _tn_d(a, b):
    return _tn(a, b)


_tn_d.defvjp(lambda a, b: (_tn(a, b), (a, b)), lambda r, g: (_nt(r[1], g), _nn(r[0], g)))


@jax.custom_vjp
def _cumsum_rows(tri, x):
    return _nn_f32(tri, x)


_cumsum_rows.defvjp(lambda tri, x: (_nn_f32(tri, x), tri), lambda tri, g: (jnp.zeros_like(tri), _tn_f32(tri, g)))


def _full(shape):
    zeros = (0,) * len(shape)
    return pl.BlockSpec(shape, lambda *_: zeros)


def _const(shape):
    zeros = (0,) * len(shape)
    return pl.BlockSpec(shape, lambda *_: zeros, pipeline_mode=pl.Buffered(1))


def _rows(tile, width, n_tiles=None):
    if n_tiles is None:
        return pl.BlockSpec((tile, width), lambda i: (i, 0))
    return pl.BlockSpec((tile, width), lambda i: (n_tiles - 1 - i, 0))


def _call(body, name, grid, in_specs, out_specs, out_shape, scratch=()):
    return pl.pallas_call(
        body, name=name, grid=grid, in_specs=in_specs, out_specs=out_specs, out_shape=out_shape,
        scratch_shapes=list(scratch),
        compiler_params=pltpu.CompilerParams(dimension_semantics=("arbitrary",) * len(grid),
                                             vmem_limit_bytes=VMEM_LIMIT))


def _sds(shape, dtype=F32):
    return jax.ShapeDtypeStruct(shape, dtype)


def _tile_of(n, cap=512):
    if n <= LANE:
        return n
    best = LANE
    for t in range(LANE, cap + 1, LANE):
        if n % t == 0:
            best = t
    return best


def _inproj_fwd(x, nm, wu, wz, wx, wd, name):
    n_tok = x.shape[0]
    tm = TOK_TILE

    def body(x_ref, nm_ref, wu_ref, wz_ref, wx_ref, wd_ref, u_ref, z_ref, xbc_ref, dt_ref):
        h = _rms(x_ref[...], nm_ref[...]).astype(BF16)
        u_ref[...] = _nn(h, wu_ref[...])
        z_ref[...] = _nn(h, wz_ref[...])
        xbc_ref[...] = _nn(h, wx_ref[...])
        dt_ref[...] = _nn(h, wd_ref[...])

    return _call(
        body, name, (n_tok // tm,),
        [_rows(tm, D_MODEL), _const((1, D_MODEL)), _const(wu.shape), _const(wz.shape), _const(wx.shape), _const(wd.shape)],
        [_rows(tm, D_MODEL), _rows(tm, D_MODEL), _rows(tm, SSD_CONV_DIM), _rows(tm, LANE)],
        [_sds((n_tok, D_MODEL)), _sds((n_tok, D_MODEL)), _sds((n_tok, SSD_CONV_DIM)), _sds((n_tok, LANE))],
    )(x, nm, wu, wz, wx, wd)


def _inproj_bwd(x, nm, du, dz, dxbc, ddt, dres, wu, wz, wx, wd, name):
    n_tok = x.shape[0]
    tm = TOK_TILE

    def body(x_ref, nm_ref, du_ref, dz_ref, dxbc_ref, ddt_ref, dres_ref, wu_ref, wz_ref, wx_ref, wd_ref,
             dx_ref, h_ref, dnm_ref):
        dh = (_nt(du_ref[...], wu_ref[...]) + _nt(dz_ref[...], wz_ref[...])
              + _nt(dxbc_ref[...], wx_ref[...]) + _nt(ddt_ref[...], wd_ref[...]))
        h, vjp = jax.vjp(_rms, x_ref[...], nm_ref[...])
        dx, dnm = vjp(dh)
        dx_ref[...] = dres_ref[...] + dx
        h_ref[...] = h.astype(BF16)

        @pl.when(pl.program_id(0) == 0)
        def _():
            dnm_ref[...] = jnp.zeros_like(dnm_ref)

        dnm_ref[...] += dnm

    return _call(
        body, name, (n_tok // tm,),
        [_rows(tm, D_MODEL), _const((1, D_MODEL)), _rows(tm, D_MODEL), _rows(tm, D_MODEL), _rows(tm, SSD_CONV_DIM),
         _rows(tm, LANE), _rows(tm, D_MODEL), _const(wu.shape), _const(wz.shape), _const(wx.shape), _const(wd.shape)],
        [_rows(tm, D_MODEL), _rows(tm, D_MODEL), _full((1, D_MODEL))],
        [_sds((n_tok, D_MODEL)), _sds((n_tok, D_MODEL), BF16), _sds((1, D_MODEL))],
    )(x, nm, du, dz, dxbc, ddt, dres, wu, wz, wx, wd)


def _ffn_act(gt, up):
    return _silu(gt) * up


FFN_BLOCK = FFN_HIDDEN // N_DEV
FFN_BLOCK_PAD = -(-FFN_BLOCK // LANE) * LANE


FFN_PAD = N_DEV * FFN_BLOCK_PAD


def _mix_ffn_fwd(x0, ya, yb, wo, nf, wg, wu, wd, name):
    n_tok = x0.shape[0]
    tm = TOK_TILE

    def body(x0_ref, ya_ref, yb_ref, wo_ref, nf_ref, wg_ref, wu_ref, wd_ref, x1_ref, x2_ref):
        x1 = x0_ref[...] + _nn(ya_ref[...], wo_ref[:D_MODEL, :]) + _nn(yb_ref[...], wo_ref[D_MODEL:, :])
        h = _rms(x1, nf_ref[...]).astype(BF16)
        x1_ref[...] = x1
        x2_ref[...] = x1 + _nn(_ffn_act(_nt(h, wg_ref[...]), _nt(h, wu_ref[...])), wd_ref[...])

    return _call(
        body, name, (n_tok // tm,),
        [_rows(tm, D_MODEL), _rows(tm, D_MODEL), _rows(tm, D_MODEL), _const(wo.shape),
         _const((1, D_MODEL)), _const(wg.shape), _const(wu.shape), _const(wd.shape)],
        [_rows(tm, D_MODEL), _rows(tm, D_MODEL)],
        [_sds((n_tok, D_MODEL)), _sds((n_tok, D_MODEL))],
    )(x0, ya, yb, wo, nf, wg, wu, wd)


def _mix_ffn_bwd(x1, dx2, wo, nf, wg, wu, wd, after, name):
    n_tok = x1.shape[0]
    tm = TOK_TILE
    n_chunks = 3
    hc = FFN_PAD // n_chunks

    def body(x1_ref, dx2_ref, wo_ref, nf_ref, wg_ref, wu_ref, wd_ref, after_ref,
             dx1_ref, dya_ref, dyb_ref, h_ref, a_ref, dgt_ref, dup_ref, dx2b_ref, dx1b_ref, dnf_ref):
        dx2 = dx2_ref[...]
        dx2b = dx2.astype(BF16)
        dx2b_ref[...] = dx2b
        h, rms_vjp = jax.vjp(_rms, x1_ref[...], nf_ref[...])
        hb = h.astype(BF16)
        dh = jnp.zeros_like(h)
        for c in range(n_chunks):
            rows = pl.ds(c * hc, hc)
            a, act_vjp = jax.vjp(_ffn_act, _nt(hb, wg_ref[rows, :]), _nt(hb, wu_ref[rows, :]))
            dgt, dup = act_vjp(_nt(dx2b, wd_ref[rows, :]))
            a_ref[:, c * hc:(c + 1) * hc] = a.astype(BF16)
            dgt_ref[:, c * hc:(c + 1) * hc] = dgt.astype(BF16)
            dup_ref[:, c * hc:(c + 1) * hc] = dup.astype(BF16)
            dh = dh + _nn(dgt, wg_ref[rows, :]) + _nn(dup, wu_ref[rows, :])
        dx, dnf = rms_vjp(dh)
        dx1 = dx2 + dx
        dx1b = dx1.astype(BF16)
        dx1_ref[...] = dx1
        dx1b_ref[...] = dx1b
        dya_ref[...] = _nt(dx1b, wo_ref[:D_MODEL, :])
        dyb_ref[...] = _nt(dx1b, wo_ref[D_MODEL:, :])
        h_ref[...] = hb

        @pl.when(pl.program_id(0) == 0)
        def _():
            dnf_ref[...] = jnp.zeros_like(dnf_ref)

        dnf_ref[...] += dnf

    hidden = _rows(tm, FFN_PAD)
    return _call(
        body, name, (n_tok // tm,),
        [_rows(tm, D_MODEL), _rows(tm, D_MODEL), _const(wo.shape), _const((1, D_MODEL)),
         _const(wg.shape), _const(wu.shape), _const(wd.shape), HBM],
        [_rows(tm, D_MODEL), _rows(tm, D_MODEL), _rows(tm, D_MODEL), _rows(tm, D_MODEL), hidden, hidden, hidden,
         _rows(tm, D_MODEL), _rows(tm, D_MODEL), _full((1, D_MODEL))],
        [_sds((n_tok, D_MODEL)), _sds((n_tok, D_MODEL)), _sds((n_tok, D_MODEL)), _sds((n_tok, D_MODEL), BF16),
         _sds((n_tok, FFN_PAD), BF16), _sds((n_tok, FFN_PAD), BF16), _sds((n_tok, FFN_PAD), BF16),
         _sds((n_tok, D_MODEL), BF16), _sds((n_tok, D_MODEL), BF16), _sds((1, D_MODEL))],
    )(x1, dx2, wo, nf, wg, wu, wd, after)


def _loss_head(x, nf, target, name):
    n_tok = x.shape[0]
    tm = TOK_TILE

    def loss_of(xv, g, t):
        e = _rms(xv, g) - t
        return 0.5 * jnp.sum(jnp.sum(e * e, axis=-1, keepdims=True) * (1.0 / D_MODEL), axis=0, keepdims=True)

    def body(x_ref, nf_ref, t_ref, loss_ref, dx_ref, dnf_ref):
        loss, vjp = jax.vjp(functools.partial(loss_of, t=t_ref[...]), x_ref[...], nf_ref[...])
        dx, dnf = vjp(jnp.ones_like(loss))
        dx_ref[...] = dx

        @pl.when(pl.program_id(0) == 0)
        def _():
            dnf_ref[...] = jnp.zeros_like(dnf_ref)
            loss_ref[...] = jnp.zeros_like(loss_ref)

        dnf_ref[...] += dnf
        loss_ref[...] += jnp.broadcast_to(loss, loss_ref.shape)

    return _call(
        body, name, (n_tok // tm,),
        [_rows(tm, D_MODEL), _const((1, D_MODEL)), _rows(tm, D_MODEL)],
        [_full((SUBLANE, LANE)), _rows(tm, D_MODEL), _full((1, D_MODEL))],
        [_sds((SUBLANE, LANE)), _sds((n_tok, D_MODEL)), _sds((1, D_MODEL))],
    )(x, nf, target)


GRAD_WIRE = BF16


def _matmul_tn(a, b, name):
    n_tok, k1 = a.shape
    k2 = b.shape[1]
    t1 = _tile_of(k1)

    def body(a_ref, b_ref, o_ref):
        o_ref[...] = _tn(a_ref[...], b_ref[...]).astype(GRAD_WIRE)

    return _call(body, name, (k1 // t1,), [pl.BlockSpec((n_tok, t1), lambda i: (0, i)), _const((n_tok, k2))],
                 [pl.BlockSpec((t1, k2), lambda i: (i, 0))], [_sds((k1, k2), GRAD_WIRE)])(a, b)[0]


def _matmul_tn_lhs_blocks(a, b, width, keep, name):
    n_tok, k1 = a.shape
    k2 = b.shape[1]

    def body(a_ref, b_ref, o_ref):
        o_ref[...] = _tn(a_ref[...], b_ref[...])[:keep, :].astype(GRAD_WIRE)

    return _call(body, name, (k1 // width,), [pl.BlockSpec((n_tok, width), lambda d: (0, d)), _const((n_tok, k2))],
                 [pl.BlockSpec((None, keep, k2), lambda d: (d, 0, 0))], [_sds((k1 // width, keep, k2), GRAD_WIRE)])(a, b)[0]


def _matmul_tn_pair(a0, a1, b, name):
    n_tok, k1 = a0.shape
    k2 = b.shape[1]
    t1 = _tile_of(k1)
    n1 = k1 // t1

    def body(a0_ref, a1_ref, b_ref, o_ref):
        @pl.when(pl.program_id(0) < n1)
        def _():
            o_ref[...] = _tn(a0_ref[...], b_ref[...]).astype(GRAD_WIRE)

        @pl.when(pl.program_id(0) >= n1)
        def _():
            o_ref[...] = _tn(a1_ref[...], b_ref[...]).astype(GRAD_WIRE)

    return _call(
        body, name, (2 * n1,),
        [pl.BlockSpec((n_tok, t1), lambda i: (0, jnp.minimum(i, n1 - 1))),
         pl.BlockSpec((n_tok, t1), lambda i: (0, jnp.maximum(i - n1, 0))), _const((n_tok, k2))],
        [pl.BlockSpec((None, t1, k2), lambda i: (i // n1, i % n1, 0))], [_sds((2, k1, k2), GRAD_WIRE)])(a0, a1, b)[0]


W_IN_BLOCK = IN_PROJ // N_DEV
W_IN_SPLITS = (D_MODEL, 2 * D_MODEL, 2 * D_MODEL + SSD_CONV_DIM)
RELAYOUT_TILE = 256


def _w_in_split(blocks, after, name):
    tr = RELAYOUT_TILE

    def body(b_ref, after_ref, wu_ref, wz_ref, wx_ref, wd_ref):
        full = jnp.concatenate([b_ref[d] for d in range(N_DEV)], axis=1)
        wu_ref[...] = full[:, :W_IN_SPLITS[0]]
        wz_ref[...] = full[:, W_IN_SPLITS[0]:W_IN_SPLITS[1]]
        wx_ref[...] = full[:, W_IN_SPLITS[1]:W_IN_SPLITS[2]]
        wd_ref[...] = jnp.concatenate([full[:, W_IN_SPLITS[2]:], jnp.zeros((tr, LANE - SSD_HEADS), full.dtype)], axis=1)

    return _call(
        body, name, (D_MODEL // tr,), [pl.BlockSpec((N_DEV, tr, W_IN_BLOCK), lambda i: (0, i, 0)), HBM],
        [_rows(tr, D_MODEL), _rows(tr, D_MODEL), _rows(tr, SSD_CONV_DIM), _rows(tr, LANE)],
        [_sds((D_MODEL, D_MODEL), BF16), _sds((D_MODEL, D_MODEL), BF16), _sds((D_MODEL, SSD_CONV_DIM), BF16),
         _sds((D_MODEL, LANE), BF16)],
    )(blocks, after)


def _w_in_grad_blocks(gu, gz, gx, gdt, name):
    tr = RELAYOUT_TILE

    def body(gu_ref, gz_ref, gx_ref, gdt_ref, o_ref):
        full = jnp.concatenate([gu_ref[...], gz_ref[...], gx_ref[...], gdt_ref[...]], axis=1)
        for d in range(N_DEV):
            o_ref[d] = full[:, d * W_IN_BLOCK:(d + 1) * W_IN_BLOCK]

    return _call(
        body, name, (D_MODEL // tr,),
        [_rows(tr, D_MODEL), _rows(tr, D_MODEL), _rows(tr, SSD_CONV_DIM), _rows(tr, LANE)],
        [pl.BlockSpec((N_DEV, tr, W_IN_BLOCK), lambda i: (0, i, 0))], [_sds((N_DEV, D_MODEL, W_IN_BLOCK), gu.dtype)],
    )(gu, gz, gx, gdt)[0]


S5_SLICES = D_MODEL // LANE
S5_SLICE_STATES = S5_LANES // S5_SLICES
SCAN_LANES = 512


def _s5_scan(br_ref, bi_ref, a_r, a_i, car_r, car_i, ini_r, ini_i, reverse, xr_ref=None, xi_ref=None,
             acc_r=None, acc_i=None):
    n_rows = br_ref.shape[1]
    seg = n_rows // SUBLANE
    order = range(SUBLANE - 1, -1, -1) if reverse else range(SUBLANE)

    def rows(t):
        return pl.ds(pl.multiple_of(((seg - 1 - t) if reverse else t) * SUBLANE, SUBLANE), SUBLANE)

    tiles_per = SCAN_LANES // LANE

    def load(ref, t, lb):
        return jnp.concatenate([ref[lb * tiles_per + j, rows(t), :] for j in range(tiles_per)], axis=1)

    def store(ref, t, lb, val):
        for j in range(tiles_per):
            ref[lb * tiles_per + j, rows(t), :] = val[:, j * LANE:(j + 1) * LANE]

    for lb in range(S5_LANES // SCAN_LANES):
        lanes = pl.ds(lb * SCAN_LANES, SCAN_LANES)
        ar1, ai1 = a_r[:, lb * SCAN_LANES:(lb + 1) * SCAN_LANES], a_i[:, lb * SCAN_LANES:(lb + 1) * SCAN_LANES]
        ar8 = jnp.broadcast_to(ar1, (SUBLANE, SCAN_LANES))
        ai8 = jnp.broadcast_to(ai1, (SUBLANE, SCAN_LANES))

        def local(t, c):
            sr, si = c
            return (ar8 * sr - ai8 * si + load(br_ref, t, lb), ar8 * si + ai8 * sr + load(bi_ref, t, lb))

        zero = jnp.zeros((SUBLANE, SCAN_LANES), F32)
        er, ei = lax.fori_loop(0, seg, local, (zero, zero))
        pr, pi = ar1, ai1
        for _ in range(seg.bit_length() - 1):
            pr, pi = pr * pr - pi * pi, 2.0 * pr * pi
        cr, ci = car_r[:, lanes], car_i[:, lanes]
        for s in order:
            ini_r[s:s + 1, lanes] = cr
            ini_i[s:s + 1, lanes] = ci
            cr, ci = pr * cr - pi * ci + er[s:s + 1, :], pr * ci + pi * cr + ei[s:s + 1, :]
        car_r[:, lanes] = cr
        car_i[:, lanes] = ci

        if xr_ref is None:
            def final(t, c):
                sr, si = c
                nr = ar8 * sr - ai8 * si + load(br_ref, t, lb)
                ni = ar8 * si + ai8 * sr + load(bi_ref, t, lb)
                store(br_ref, t, lb, nr)
                store(bi_ref, t, lb, ni)
                return nr, ni

            lax.fori_loop(0, seg, final, (ini_r[:, lanes], ini_i[:, lanes]))
        else:
            def final_acc(t, c):
                sr, si, gr, gi = c
                xr, xi = load(xr_ref, t, lb), load(xi_ref, t, lb)
                gr = gr + sr * xr + si * xi
                gi = gi + si * xr - sr * xi
                nr = ar8 * sr - ai8 * si + load(br_ref, t, lb)
                ni = ar8 * si + ai8 * sr + load(bi_ref, t, lb)
                store(br_ref, t, lb, nr)
                store(bi_ref, t, lb, ni)
                return nr, ni, gr, gi

            _, _, gr, gi = lax.fori_loop(0, seg, final_acc,
                                         (ini_r[:, lanes], ini_i[:, lanes], acc_r[:, lanes], acc_i[:, lanes]))
            acc_r[:, lanes] = gr
            acc_i[:, lanes] = gi


def _s5_tail(gg, q, sn):
    return _rms(gg * _sigmoid(q), sn)


def _scan_order(n_rows):
    seg = n_rows // SUBLANE
    r = lax.broadcasted_iota(jnp.int32, (n_rows, n_rows), 0)
    c = lax.broadcasted_iota(jnp.int32, (n_rows, n_rows), 1)
    return (c == (r % SUBLANE) * seg + r // SUBLANE).astype(F32)


S5_STATE_TILES = S5_LANES // LANE
TILES_PER_SLICE = S5_SLICE_STATES // LANE


def _put_states(ref, k, val):
    for j in range(TILES_PER_SLICE):
        ref[k * TILES_PER_SLICE + j] = val[:, j * LANE:(j + 1) * LANE]


def _get_states(ref, k):
    return jnp.concatenate([ref[k * TILES_PER_SLICE + j] for j in range(TILES_PER_SLICE)], axis=1)


def _state_rows(tile, n_tiles=None):
    if n_tiles is None:
        return pl.BlockSpec((S5_STATE_TILES, tile, LANE), lambda i: (0, i, 0))
    return pl.BlockSpec((S5_STATE_TILES, tile, LANE), lambda i: (0, n_tiles - 1 - i, 0))


def _s5_fwd(u, a_r, a_i, bdb, bcr, bci, dsk, wglu, bglu, sn, name):
    n_tok = u.shape[0]
    tc = S5_TILE
    sw = S5_SLICE_STATES

    def body(u_ref, ar_ref, ai_ref, bdb_ref, bcr_ref, bci_ref, d_ref, wg_ref, bg_ref, sn_ref,
             ya_ref, xr_ref, xi_ref, v_ref, car_r, car_i, ini_r, ini_i):
        @pl.when(pl.program_id(0) == 0)
        def _():
            car_r[...] = jnp.zeros_like(car_r)
            car_i[...] = jnp.zeros_like(car_i)

        order = _scan_order(tc)
        u_t = _nn_f32(order, u_ref[...])
        ub = u_t.astype(BF16)
        for k in range(S5_SLICES):
            bu = _nn(ub[:, k * LANE:(k + 1) * LANE], bdb_ref[k])
            _put_states(xr_ref, k, bu[:, :sw])
            _put_states(xi_ref, k, bu[:, sw:])
        _s5_scan(xr_ref, xi_ref, ar_ref[...], ai_ref[...], car_r, car_i, ini_r, ini_i, reverse=False)
        vs = [_nn(_get_states(xr_ref, k), bcr_ref[k]) - _nn(_get_states(xi_ref, k), bci_ref[k])
              for k in range(S5_SLICES)]
        v = jnp.concatenate(vs, axis=1) + d_ref[...] * u_t
        v_ref[...] = v
        gg = _gelu(v)
        ya_ref[...] = _tn_f32(order, _s5_tail(gg, _nn(gg, wg_ref[...]) + bg_ref[...], sn_ref[...])).astype(BF16)

    return _call(
        body, name, (n_tok // tc,),
        [_rows(tc, D_MODEL), _const((1, S5_LANES)), _const((1, S5_LANES)), _const(bdb.shape), _const(bcr.shape),
         _const(bci.shape), _const((1, D_MODEL)), _const(wglu.shape), _const((1, D_MODEL)), _const((1, D_MODEL))],
        [_rows(tc, D_MODEL), _state_rows(tc), _state_rows(tc), _rows(tc, D_MODEL)],
        [_sds((n_tok, D_MODEL), BF16), _sds((S5_STATE_TILES, n_tok, LANE)), _sds((S5_STATE_TILES, n_tok, LANE)),
         _sds((n_tok, D_MODEL))],
        scratch=[pltpu.VMEM((1, S5_LANES), F32), pltpu.VMEM((1, S5_LANES), F32),
                 pltpu.VMEM((SUBLANE, S5_LANES), F32), pltpu.VMEM((SUBLANE, S5_LANES), F32)],
    )(u, a_r, a_i, bdb, bcr, bci, dsk, wglu, bglu, sn)


def _s5_bwd(dya, v, u, xr, xi, a_r, a_i, bdb, bcr, bci, dsk, wglu, bglu, sn, after, name):
    n_tok = u.shape[0]
    tc = S5_TILE
    nt = n_tok // tc
    sw = S5_SLICE_STATES

    def body(dya_ref, v_ref, u_ref, xr_ref, xi_ref, ar_ref, ai_ref, bdb_ref, bcr_ref, bci_ref, d_ref, wg_ref, bg_ref, sn_ref,
             after_ref, du_ref, gg_ref, dq_ref, gbdb_ref, gbcr_ref, gbci_ref, gar_ref, gai_ref, gd_ref, gbg_ref, gsn_ref,
             gr_ref, gi_ref, car_r, car_i, ini_r, ini_i):
        @pl.when(pl.program_id(0) == 0)
        def _():
            for r in (car_r, car_i, gbdb_ref, gbcr_ref, gbci_ref, gar_ref, gai_ref, gd_ref, gbg_ref, gsn_ref):
                r[...] = jnp.zeros_like(r)

        order = _scan_order(tc)
        u_t = _nn_f32(order, u_ref[...])
        gg, gelu_vjp = jax.vjp(_gelu, v_ref[...])
        _, tail_vjp = jax.vjp(_s5_tail, gg, _nn(gg, wg_ref[...]) + bg_ref[...], sn_ref[...])
        dgg, dq, dsn = tail_vjp(_nn_f32(order, dya_ref[...]))
        (dv,) = gelu_vjp(dgg + _nt(dq, wg_ref[...]))
        gg_ref[...] = gg.astype(BF16)
        dq_ref[...] = dq.astype(BF16)
        gd_ref[...] += jnp.sum(dv * u_t, axis=0, keepdims=True)
        gbg_ref[...] += jnp.sum(dq, axis=0, keepdims=True)
        gsn_ref[...] += dsn
        dvb = dv.astype(BF16)
        for k in range(S5_SLICES):
            dvk = dvb[:, k * LANE:(k + 1) * LANE]
            _put_states(gr_ref, k, _nt(dvk, bcr_ref[k]))
            _put_states(gi_ref, k, -_nt(dvk, bci_ref[k]))
            gbcr_ref[k] += _tn(_get_states(xr_ref, k), dvk)
            gbci_ref[k] -= _tn(_get_states(xi_ref, k), dvk)
        _s5_scan(gr_ref, gi_ref, ar_ref[...], -ai_ref[...], car_r, car_i, ini_r, ini_i, reverse=True,
                 xr_ref=xr_ref, xi_ref=xi_ref, acc_r=gar_ref, acc_i=gai_ref)
        ub = u_t.astype(BF16)
        dus = []
        for k in range(S5_SLICES):
            gk_r, gk_i = _get_states(gr_ref, k).astype(BF16), _get_states(gi_ref, k).astype(BF16)
            bk = bdb_ref[k]
            dus.append(_nt(gk_r, bk[:, :sw]) + _nt(gk_i, bk[:, sw:]))
            uk = ub[:, k * LANE:(k + 1) * LANE]
            gbdb_ref[k, :, :sw] += _tn(uk, gk_r)
            gbdb_ref[k, :, sw:] += _tn(uk, gk_i)
        du_ref[...] = _tn_f32(order, jnp.concatenate(dus, axis=1) + d_ref[...] * dv).astype(BF16)

    rev = functools.partial(_rows, n_tiles=nt)
    return _call(
        body, name, (nt,),
        [rev(tc, D_MODEL), rev(tc, D_MODEL), rev(tc, D_MODEL), _state_rows(tc, nt), _state_rows(tc, nt),
         _const((1, S5_LANES)), _const((1, S5_LANES)), _const(bdb.shape), _const(bcr.shape), _const(bci.shape),
         _const((1, D_MODEL)), _const(wglu.shape), _const((1, D_MODEL)), _const((1, D_MODEL)), HBM],
        [rev(tc, D_MODEL), rev(tc, D_MODEL), rev(tc, D_MODEL), _full(bdb.shape), _full(bcr.shape), _full(bci.shape),
         _full((SUBLANE, S5_LANES)), _full((SUBLANE, S5_LANES)), _full((1, D_MODEL)), _full((1, D_MODEL)), _full((1, D_MODEL))],
        [_sds((n_tok, D_MODEL), BF16), _sds((n_tok, D_MODEL), BF16), _sds((n_tok, D_MODEL), BF16), _sds(bdb.shape), _sds(bcr.shape),
         _sds(bci.shape), _sds((SUBLANE, S5_LANES)), _sds((SUBLANE, S5_LANES)), _sds((1, D_MODEL)), _sds((1, D_MODEL)),
         _sds((1, D_MODEL))],
        scratch=[pltpu.VMEM((S5_STATE_TILES, tc, LANE), F32), pltpu.VMEM((S5_STATE_TILES, tc, LANE), F32),
                 pltpu.VMEM((1, S5_LANES), F32), pltpu.VMEM((1, S5_LANES), F32),
                 pltpu.VMEM((SUBLANE, S5_LANES), F32), pltpu.VMEM((SUBLANE, S5_LANES), F32)],
    )(dya, v, u, xr, xi, a_r, a_i, bdb, bcr, bci, dsk, wglu, bglu, sn, after)


SSD_WIDTH = SSD_HEADS * SSD_HEAD_DIM
SSD_GROUPS = 2
HEADS_PER_GROUP = SSD_HEADS // SSD_GROUPS


def _take(x, axis, start, size):
    n = x.shape[axis]

    def sl(v):
        return lax.slice_in_dim(v, start, start + size, axis=axis)

    @jax.custom_vjp
    def f(v):
        return sl(v)

    def bwd(_, g):
        parts = []
        if start:
            parts.append(jnp.zeros(g.shape[:axis] + (start,) + g.shape[axis + 1:], g.dtype))
        parts.append(g)
        if n - start - size:
            parts.append(jnp.zeros(g.shape[:axis] + (n - start - size,) + g.shape[axis + 1:], g.dtype))
        return (jnp.concatenate(parts, axis=axis) if len(parts) > 1 else g,)

    f.defvjp(lambda v: (sl(v), None), bwd)
    return f(x)


def _lane_of(x, h):
    col = lax.broadcasted_iota(jnp.int32, x.shape, 1)
    return jnp.sum(jnp.where(col == h, x, 0.0), axis=1, keepdims=True)


def _ssd_chunk(xc, z, dt, dtb, alog, dvec, gn, st, nn, nt, tn, cumsum, take):
    t_len = xc.shape[0]
    xa = _silu(xc)
    dtp = _softplus(dt + dtb)
    d_a = dtp * (-jnp.exp(alog))
    row = lax.broadcasted_iota(jnp.int32, (t_len, t_len), 0)
    col = lax.broadcasted_iota(jnp.int32, (t_len, t_len), 1)
    causal = row >= col
    cum = cumsum(causal.astype(F32), d_a)
    eye = (row == col).astype(F32)
    ys, sts = [], []
    for g in range(SSD_GROUPS):
        bg = take(xa, 1, SSD_WIDTH + g * SSD_STATE, SSD_STATE)
        cg = take(xa, 1, SSD_WIDTH + (SSD_GROUPS + g) * SSD_STATE, SSD_STATE)
        cb = nt(cg, bg)
        for r in range(HEADS_PER_GROUP):
            h = g * HEADS_PER_GROUP + r
            cc = _lane_of(cum, h)
            cr = jnp.sum(cc * eye, axis=0, keepdims=True)
            decay = jnp.exp(jnp.where(causal, cc - cr, -1e30))
            xh = take(xa, 1, h * SSD_HEAD_DIM, SSD_HEAD_DIM)
            xdt = xh * _lane_of(dtp, h)
            sth = take(st, 0, h * SSD_HEAD_DIM, SSD_HEAD_DIM)
            c_last = jnp.sum(jnp.where(row[:, :1] == t_len - 1, cc, 0.0), axis=0, keepdims=True)
            y = nn(cb * decay, xdt) + jnp.exp(cc) * nt(cg, sth) + _lane_of(dvec, h) * xh
            ys.append(y)
            sts.append(jnp.exp(c_last) * sth + tn(xdt * jnp.exp(c_last - cc), bg))
    y = jnp.concatenate(ys, axis=1) * _silu(z)
    return _rms(y, gn), jnp.concatenate(sts, axis=0)


def _shift_back(cur, prev, j):
    if j == 0:
        return cur
    row = lax.broadcasted_iota(jnp.int32, cur.shape, 0)
    return jnp.where(row < j, pltpu.roll(prev, j, 0), pltpu.roll(cur, j, 0))


def _shift_ahead(cur, nxt, j):
    if j == 0:
        return cur
    n = cur.shape[0]
    row = lax.broadcasted_iota(jnp.int32, cur.shape, 0)
    return jnp.where(row >= n - j, pltpu.roll(nxt, n - j, 0), pltpu.roll(cur, n - j, 0))


def _conv(cur, prev, w, b):
    out = b + w[SSD_CONV - 1:SSD_CONV, :] * cur
    for k in range(SSD_CONV - 1):
        out = out + w[k:k + 1, :] * _shift_back(cur, prev, SSD_CONV - 1 - k)
    return out


def _ssd_fwd(xbc, z, dt, conv_w, conv_b, dtb, alog, dvec, gn, name):
    n_tok = xbc.shape[0]
    tc = SSD_CHUNK
    nc = n_tok // tc
    st_rows = SSD_HEADS * SSD_HEAD_DIM

    def body(cur_ref, prev_ref, z_ref, dt_ref, w_ref, b_ref, dtb_ref, alog_ref, dvec_ref, gn_ref,
             yb_ref, stin_ref, st_ref):
        i = pl.program_id(0)

        @pl.when(i == 0)
        def _():
            st_ref[...] = jnp.zeros_like(st_ref)

        prev = jnp.where(i > 0, prev_ref[...], 0.0)
        xc = _conv(cur_ref[...], prev, w_ref[...], b_ref[...])
        st = st_ref[...]
        stin_ref[0] = st
        yb, st_new = _ssd_chunk(xc, z_ref[...], dt_ref[...], dtb_ref[...], alog_ref[...], dvec_ref[...], gn_ref[...], st,
                                _nn, _nt, _tn, _nn_f32, lambda v, axis, start, size: lax.slice_in_dim(v, start, start + size, axis=axis))
        yb_ref[...] = yb.astype(BF16)
        st_ref[...] = st_new

    return _call(
        body, name, (nc,),
        [_rows(tc, SSD_CONV_DIM), pl.BlockSpec((tc, SSD_CONV_DIM), lambda i: (jnp.maximum(i - 1, 0), 0)),
         _rows(tc, D_MODEL), _rows(tc, LANE), _const((SSD_CONV, SSD_CONV_DIM)), _const((1, SSD_CONV_DIM)),
         _const((1, LANE)), _const((1, LANE)), _const((1, LANE)), _const((1, D_MODEL))],
        [_rows(tc, D_MODEL), pl.BlockSpec((1, st_rows, SSD_STATE), lambda i: (i, 0, 0))],
        [_sds((n_tok, D_MODEL), BF16), _sds((nc, st_rows, SSD_STATE))],
        scratch=[pltpu.VMEM((st_rows, SSD_STATE), F32)],
    )(xbc, xbc, z, dt, conv_w, conv_b, dtb, alog, dvec, gn)


def _ssd_bwd(dyb, xbc, z, dt, stin, conv_w, conv_b, dtb, alog, dvec, gn, name):
    n_tok = xbc.shape[0]
    tc = SSD_CHUNK
    nc = n_tok // tc
    st_rows = SSD_HEADS * SSD_HEAD_DIM

    def body(dyb_ref, cur_ref, prev_ref, z_ref, dt_ref, stin_ref, w_ref, b_ref, dtb_ref, alog_ref, dvec_ref, gn_ref,
             dxbc_ref, dz_ref, ddt_ref, gw_ref, gb_ref, gdtb_ref, galog_ref, gdvec_ref, ggn_ref,
             dst_ref, dxc_next_ref):
        i = pl.program_id(0)

        @pl.when(i == 0)
        def _():
            for r in (dst_ref, dxc_next_ref, gw_ref, gb_ref, gdtb_ref, galog_ref, gdvec_ref, ggn_ref):
                r[...] = jnp.zeros_like(r)

        cur = cur_ref[...]
        prev = jnp.where(i < nc - 1, prev_ref[...], 0.0)
        w = w_ref[...]
        xc = _conv(cur, prev, w, b_ref[...])
        chunk = functools.partial(_ssd_chunk, nn=_nn_d, nt=_nt_d, tn=_tn_d, cumsum=_cumsum_rows, take=_take)
        _, vjp = jax.vjp(chunk, xc, z_ref[...], dt_ref[...], dtb_ref[...], alog_ref[...], dvec_ref[...], gn_ref[...],
                         stin_ref[0])
        dxc, dz, ddt, gdtb, galog, gdvec, ggn, dst = vjp((dyb_ref[...], dst_ref[...]))
        dst_ref[...] = dst
        dz_ref[...] = dz.astype(BF16)
        ddt_ref[...] = ddt.astype(BF16)
        gdtb_ref[...] += gdtb
        galog_ref[...] += galog
        gdvec_ref[...] += gdvec
        ggn_ref[...] += ggn
        dxc_next = dxc_next_ref[...]
        dxbc = w[SSD_CONV - 1:SSD_CONV, :] * dxc
        gws = []
        for k in range(SSD_CONV - 1):
            j = SSD_CONV - 1 - k
            dxbc = dxbc + w[k:k + 1, :] * _shift_ahead(dxc, dxc_next, j)
            gws.append(jnp.sum(dxc * _shift_back(cur, prev, j), axis=0, keepdims=True))
        gws.append(jnp.sum(dxc * cur, axis=0, keepdims=True))
        dxbc_ref[...] = dxbc.astype(BF16)
        gw_ref[...] += jnp.concatenate(gws, axis=0)
        gb_ref[...] += jnp.sum(dxc, axis=0, keepdims=True)
        dxc_next_ref[...] = dxc

    rev = functools.partial(_rows, n_tiles=nc)
    return _call(
        body, name, (nc,),
        [rev(tc, D_MODEL), rev(tc, SSD_CONV_DIM),
         pl.BlockSpec((tc, SSD_CONV_DIM), lambda i: (jnp.maximum(nc - 2 - i, 0), 0)),
         rev(tc, D_MODEL), rev(tc, LANE), pl.BlockSpec((1, st_rows, SSD_STATE), lambda i: (nc - 1 - i, 0, 0)),
         _const((SSD_CONV, SSD_CONV_DIM)), _const((1, SSD_CONV_DIM)), _const((1, LANE)), _const((1, LANE)),
         _const((1, LANE)), _const((1, D_MODEL))],
        [rev(tc, SSD_CONV_DIM), rev(tc, D_MODEL), rev(tc, LANE), _full((SSD_CONV, SSD_CONV_DIM)), _full((1, SSD_CONV_DIM)),
         _full((1, LANE)), _full((1, LANE)), _full((1, LANE)), _full((1, D_MODEL))],
        [_sds((n_tok, SSD_CONV_DIM), BF16), _sds((n_tok, D_MODEL), BF16), _sds((n_tok, LANE), BF16), _sds((SSD_CONV, SSD_CONV_DIM)),
         _sds((1, SSD_CONV_DIM)), _sds((1, LANE)), _sds((1, LANE)), _sds((1, LANE)), _sds((1, D_MODEL))],
        scratch=[pltpu.VMEM((st_rows, SSD_STATE), F32), pltpu.VMEM((tc, SSD_CONV_DIM), F32)],
    )(dyb, xbc, xbc, z, dt, stin, conv_w, conv_b, dtb, alog, dvec, gn)


@jax.custom_vjp
def _expand_cols(x, e):
    return _nn_f32(x, e)


_expand_cols.defvjp(
    lambda x, e: (_nn_f32(x, e), e),
    lambda e, g: (lax.dot_general(g, e, (((1,), (1,)), ((), ())), precision=lax.Precision.HIGHEST,
                                  preferred_element_type=F32), jnp.zeros_like(e)))


def _s5_discretize(lam_re, lam_im, log_step, b_re, b_im, expand):
    step = jnp.exp(log_step)
    mag = jnp.exp(lam_re * step)
    ang = lam_im * step
    a_r = mag * jnp.cos(ang)
    a_i = mag * jnp.sin(ang)
    den = lam_re * lam_re + lam_im * lam_im
    n_r = a_r - 1.0
    coef_r = _expand_cols((n_r * lam_re + a_i * lam_im) / den, expand)
    coef_i = _expand_cols((a_i * lam_re - n_r * lam_im) / den, expand)
    return a_r, a_i, coef_r * b_re - coef_i * b_im, coef_r * b_im + coef_i * b_re


def _expand_matrix():
    p = lax.broadcasted_iota(jnp.int32, (S5_STATE, S5_STATE * S5_GROUP), 0)
    c = lax.broadcasted_iota(jnp.int32, (S5_STATE, S5_STATE * S5_GROUP), 1)
    return (c // S5_GROUP == p).astype(F32)


def _s5_discretize_fwd(lam_re, lam_im, log_step, b_re, b_im, name):
    def body(lr_ref, li_ref, ls_ref, br_ref, bi_ref, ar_ref, ai_ref, bbr_ref, bbi_ref):
        outs = _s5_discretize(lr_ref[...], li_ref[...], ls_ref[...], br_ref[...], bi_ref[...], _expand_matrix())
        for r, o in zip((ar_ref, ai_ref, bbr_ref, bbi_ref), outs):
            r[...] = o

    sq, wide = (S5_GROUPS, S5_STATE), (S5_GROUPS, S5_STATE * S5_GROUP)
    return _call(body, name, (1,), [_full(sq), _full(sq), _full((S5_GROUPS, 1)), _full(wide), _full(wide)],
                 [_full(sq), _full(sq), _full(wide), _full(wide)], [_sds(sq), _sds(sq), _sds(wide), _sds(wide)],
                 )(lam_re, lam_im, log_step, b_re, b_im)


def _s5_discretize_bwd(lam_re, lam_im, log_step, b_re, b_im, g_ar8, g_ai8, g_bbr, g_bbi, name):
    def body(lr_ref, li_ref, ls_ref, br_ref, bi_ref, gar_ref, gai_ref, gbbr_ref, gbbi_ref,
             glr_ref, gli_ref, gls_ref, gbr_ref, gbi_ref):
        _, vjp = jax.vjp(functools.partial(_s5_discretize, expand=_expand_matrix()),
                         lr_ref[...], li_ref[...], ls_ref[...], br_ref[...], bi_ref[...])
        grads = vjp((jnp.sum(gar_ref[...], axis=0), jnp.sum(gai_ref[...], axis=0), gbbr_ref[...], gbbi_ref[...]))
        for r, g in zip((glr_ref, gli_ref, gls_ref, gbr_ref, gbi_ref), grads):
            r[...] = g

    sq, wide, col = (S5_GROUPS, S5_STATE), (S5_GROUPS, S5_STATE * S5_GROUP), (S5_GROUPS, 1)
    part = (SUBLANE,) + sq
    return _call(body, name, (1,),
                 [_full(sq), _full(sq), _full(col), _full(wide), _full(wide), _full(part), _full(part), _full(wide), _full(wide)],
                 [_full(sq), _full(sq), _full(col), _full(wide), _full(wide)],
                 [_sds(sq), _sds(sq), _sds(col), _sds(wide), _sds(wide)],
                 )(lam_re, lam_im, log_step, b_re, b_im, g_ar8, g_ai8, g_bbr, g_bbi)


GROUPS_PER_SLICE = LANE // S5_GROUP


def _block_diag_b(bb):
    t = bb.reshape(S5_SLICES, GROUPS_PER_SLICE, S5_STATE, S5_GROUP)
    eye = jnp.eye(GROUPS_PER_SLICE, dtype=bb.dtype)
    return jnp.einsum("kgph,gf->kghfp", t, eye).reshape(S5_SLICES, LANE, S5_SLICE_STATES)


def _block_diag_b_inv(m):
    t = m.reshape(S5_SLICES, GROUPS_PER_SLICE, S5_GROUP, GROUPS_PER_SLICE, S5_STATE)
    return jnp.einsum("kghgp->kgph", t).reshape(S5_GROUPS, S5_STATE * S5_GROUP)


def _block_diag_c(c):
    t = c.reshape(S5_SLICES, GROUPS_PER_SLICE, S5_GROUP, S5_STATE)
    eye = jnp.eye(GROUPS_PER_SLICE, dtype=c.dtype)
    return jnp.einsum("kghp,gf->kgpfh", t, eye).reshape(S5_SLICES, S5_SLICE_STATES, LANE)


def _block_diag_c_inv(m):
    t = m.reshape(S5_SLICES, GROUPS_PER_SLICE, S5_STATE, GROUPS_PER_SLICE, S5_GROUP)
    return jnp.einsum("kgpgh->kghp", t).reshape(S5_GROUPS, S5_GROUP, S5_STATE)


def _pad_lanes(v):
    return jnp.pad(v.reshape(1, -1), ((0, 0), (0, LANE - v.shape[0])))


def _prepare_layer(w, blk, i, after):
    p = {}
    p["wu"], p["wz"], p["wx"], p["wd"] = _w_in_split(blk["w_in"], after, name=f"w_in_split_{i}")
    p["nm"] = w["norm_mix"][i].reshape(1, D_MODEL)
    p["lam_re"], p["lam_im"] = w["s5_lam_re"][i], w["s5_lam_im"][i]
    p["log_step"] = w["s5_log_step"][i].reshape(S5_GROUPS, 1)
    p["b_re"] = w["s5_b_re"][i].reshape(S5_GROUPS, S5_STATE * S5_GROUP)
    p["b_im"] = w["s5_b_im"][i].reshape(S5_GROUPS, S5_STATE * S5_GROUP)
    a_r, a_i, bb_r, bb_i = _s5_discretize_fwd(p["lam_re"], p["lam_im"], p["log_step"], p["b_re"], p["b_im"],
                                              name=f"s5_discretize_{i}")
    p["a_r"], p["a_i"] = a_r.reshape(1, S5_LANES), a_i.reshape(1, S5_LANES)
    p["bdb"] = jnp.concatenate([_block_diag_b(bb_r), _block_diag_b(bb_i)], axis=2).astype(BF16)
    p["bcr"] = _block_diag_c(w["s5_c_re"][i]).astype(BF16)
    p["bci"] = _block_diag_c(w["s5_c_im"][i]).astype(BF16)
    p["dsk"] = w["s5_d"][i].reshape(1, D_MODEL)
    p["wglu"] = blk["s5_w_glu"].reshape(D_MODEL, D_MODEL)
    p["bglu"] = w["s5_b_glu"][i].reshape(1, D_MODEL)
    p["sn"] = w["s5_norm"][i].reshape(1, D_MODEL)
    p["conv_w"] = blk["ssd_conv_w"]
    p["conv_b"] = w["ssd_conv_b"][i].reshape(1, SSD_CONV_DIM)
    p["dtb"] = _pad_lanes(w["ssd_dt_bias"][i])
    p["alog"] = _pad_lanes(w["ssd_a_log"][i])
    p["dvec"] = _pad_lanes(w["ssd_d"][i])
    p["gn"] = w["ssd_norm"][i].reshape(1, D_MODEL)
    p["wo"] = blk["w_out"].reshape(2 * D_MODEL, D_MODEL)
    p["nf"] = w["norm_ffn"][i].reshape(1, D_MODEL)
    p["wg"], p["wup"], p["wdn"] = (blk[n].reshape(FFN_PAD, D_MODEL) for n in ("w_gate", "w_up", "w_down"))
    return p


def _layer_fwd(x0, p, i):
    u, z, xbc, dt = _inproj_fwd(x0, p["nm"], p["wu"], p["wz"], p["wx"], p["wd"], name=f"inproj_fwd_{i}")
    ya, xr, xi, v = _s5_fwd(u, p["a_r"], p["a_i"], p["bdb"], p["bcr"], p["bci"], p["dsk"], p["wglu"], p["bglu"], p["sn"],
                            name=f"s5_fwd_{i}")
    yb, stin = _ssd_fwd(xbc, z, dt, p["conv_w"], p["conv_b"], p["dtb"], p["alog"], p["dvec"], p["gn"], name=f"ssd_fwd_{i}")
    x1, x2 = _mix_ffn_fwd(x0, ya, yb, p["wo"], p["nf"], p["wg"], p["wup"], p["wdn"], name=f"mix_ffn_fwd_{i}")
    return x2, dict(x0=x0, u=u, z=z, xbc=xbc, dt=dt, xr=xr, xi=xi, v=v, stin=stin, ya=ya, yb=yb, x1=x1)


def _layer_bwd(dx2, s, p, i, after, between=None):
    g = {}
    dx1, dya, dyb, h2, act, dgt, dup, dx2b, dx1b, g_nf = _mix_ffn_bwd(
        s["x1"], dx2, p["wo"], p["nf"], p["wg"], p["wup"], p["wdn"], after, name=f"mix_ffn_bwd_{i}")
    g["norm_ffn"] = g_nf.reshape(D_MODEL)
    g["w_down"] = _matmul_tn_lhs_blocks(act, dx2b, FFN_BLOCK_PAD, FFN_BLOCK, name=f"grad_w_down_{i}")
    g["w_gate"] = _matmul_tn_lhs_blocks(dgt, h2, FFN_BLOCK_PAD, FFN_BLOCK, name=f"grad_w_gate_{i}")
    g["w_up"] = _matmul_tn_lhs_blocks(dup, h2, FFN_BLOCK_PAD, FFN_BLOCK, name=f"grad_w_up_{i}")
    g["w_out"] = _matmul_tn_pair(s["ya"], s["yb"], dx1b, name=f"grad_w_out_{i}").reshape(N_DEV, 2 * D_MODEL // N_DEV, D_MODEL)
    if between is not None:
        after = between(g)

    (du, gg, dq, g_bdb, g_bcr, g_bci, g_ar8, g_ai8, g_d, g_bglu, g_sn) = _s5_bwd(
        dya, s["v"], s["u"], s["xr"], s["xi"], p["a_r"], p["a_i"], p["bdb"], p["bcr"], p["bci"], p["dsk"], p["wglu"],
        p["bglu"], p["sn"], after, name=f"s5_bwd_{i}")
    g["s5_w_glu"] = _matmul_tn(gg, dq, name=f"grad_w_glu_{i}").reshape(N_DEV, D_MODEL // N_DEV, D_MODEL)
    g["s5_d"], g["s5_b_glu"], g["s5_norm"] = g_d.reshape(D_MODEL), g_bglu.reshape(D_MODEL), g_sn.reshape(D_MODEL)
    g["s5_c_re"], g["s5_c_im"] = _block_diag_c_inv(g_bcr), _block_diag_c_inv(g_bci)
    sq = (SUBLANE, S5_GROUPS, S5_STATE)
    g_lr, g_li, g_ls, g_br, g_bi = _s5_discretize_bwd(
        p["lam_re"], p["lam_im"], p["log_step"], p["b_re"], p["b_im"], g_ar8.reshape(sq), g_ai8.reshape(sq),
        _block_diag_b_inv(g_bdb[:, :, :S5_SLICE_STATES]), _block_diag_b_inv(g_bdb[:, :, S5_SLICE_STATES:]),
        name=f"s5_discretize_bwd_{i}")
    g["s5_lam_re"], g["s5_lam_im"], g["s5_log_step"] = g_lr, g_li, g_ls.reshape(S5_GROUPS)
    b_shape = (S5_GROUPS, S5_STATE, S5_GROUP)
    g["s5_b_re"], g["s5_b_im"] = g_br.reshape(b_shape), g_bi.reshape(b_shape)

    dxbc, dz, ddt, g_cw, g_cb, g_dtb, g_alog, g_dvec, g_gn = _ssd_bwd(
        dyb, s["xbc"], s["z"], s["dt"], s["stin"], p["conv_w"], p["conv_b"], p["dtb"], p["alog"], p["dvec"], p["gn"],
        name=f"ssd_bwd_{i}")
    g["ssd_conv_w"] = jnp.moveaxis(g_cw.reshape(SSD_CONV, N_DEV, SSD_CONV_DIM // N_DEV), 1, 0)
    g["ssd_conv_b"] = g_cb.reshape(SSD_CONV_DIM)
    g["ssd_dt_bias"], g["ssd_a_log"], g["ssd_d"] = g_dtb[0, :SSD_HEADS], g_alog[0, :SSD_HEADS], g_dvec[0, :SSD_HEADS]
    g["ssd_norm"] = g_gn.reshape(D_MODEL)

    dx0, h, g_nm = _inproj_bwd(s["x0"], p["nm"], du, dz, dxbc, ddt, dx1, p["wu"], p["wz"], p["wx"], p["wd"],
                               name=f"inproj_bwd_{i}")
    g["norm_mix"] = g_nm.reshape(D_MODEL)
    g["w_in"] = _w_in_grad_blocks(
        _matmul_tn(h, du, name=f"grad_w_in_u_{i}"), _matmul_tn(h, dz, name=f"grad_w_in_z_{i}"),
        _matmul_tn(h, dxbc, name=f"grad_w_in_xbc_{i}"), _matmul_tn(h, ddt, name=f"grad_w_in_dt_{i}"),
        name=f"grad_w_in_blocks_{i}")
    return dx0, g


def _example_step(x, target, w, blks):
    prepared = [_prepare_layer(w, blks[i], i, x) for i in range(DEPTH)]
    saved = []
    h = x
    for i in range(DEPTH):
        h, s = _layer_fwd(h, prepared[i], i)
        saved.append(s)
    loss, dh, g_final = _loss_head(h, w["norm_final"].reshape(1, D_MODEL), target, name="loss_head")
    layer_grads = [None] * DEPTH
    for i in reversed(range(DEPTH)):
        dh, layer_grads[i] = _layer_bwd(dh, saved[i], prepared[i], i, x)
    return loss, dh, layer_grads, g_final.reshape(D_MODEL)


def _mesh_position():
    return lax.axis_index("x"), lax.axis_index("y"), lax.axis_index("c")


def _peer(pos, k):
    x, y, c = pos
    px = 1 - x if k & 4 else x
    py = 1 - y if k & 2 else y
    pc = 1 - c if k & 1 else c
    return (px, py, pc), 4 * px + 2 * py + pc


HBM = pl.BlockSpec(memory_space=pl.ANY)


def _run_copies(local, remote):
    for cp in local + remote:
        cp.start()
    for cp in remote:
        cp.wait_recv()
    for cp in remote:
        cp.wait_send()
    for cp in local:
        cp.wait()


def _comm_scratch(n_units):
    return [pltpu.SemaphoreType.DMA((n_units, N_DEV - 1)), pltpu.SemaphoreType.DMA((n_units, N_DEV - 1)),
            pltpu.SemaphoreType.DMA((n_units,))]


def _gather_blocks(arrays, layered, name):
    units, out_shapes = [], []
    for j, (a, lay) in enumerate(zip(arrays, layered)):
        for layer in (range(a.shape[0]) if lay else (None,)):
            units.append((j, layer, len(out_shapes)))
            out_shapes.append(_sds((N_DEV,) + (a.shape[1:] if lay else a.shape), a.dtype))
    n_in = len(arrays)
    other_chips = (4, 2, 6)

    def body(*refs):
        ins, outs = refs[:n_in], refs[n_in:n_in + len(out_shapes)]
        send_sems, recv_sems, local_sems = refs[n_in + len(out_shapes):]
        pos = _mesh_position()
        me = 4 * pos[0] + 2 * pos[1] + pos[2]
        sibling, _ = _peer(pos, 1)
        local, own, passed = [], [], []
        for u, (j, layer, o) in enumerate(units):
            src = ins[j] if layer is None else ins[j].at[layer]
            local.append(pltpu.make_async_copy(src, outs[o].at[me], local_sems.at[u]))

            def copy(sem, src_ref, slot, to, u=u, o=o):
                return pltpu.make_async_remote_copy(
                    src_ref=src_ref, dst_ref=outs[o].at[slot], send_sem=send_sems.at[u, sem], recv_sem=recv_sems.at[u, sem],
                    device_id=to, device_id_type=MESH_ID)

            own.append([copy(0, src, me, sibling)] + [copy(1 + i, src, me, _peer(pos, k)[0]) for i, k in enumerate(other_chips)])
            passed.append([copy(4 + i, outs[o].at[_peer(pos, k)[1]], _peer(pos, k)[1], sibling) for i, k in enumerate(other_chips)])
        for cp in local + [c for unit in own for c in unit]:
            cp.start()
        for u in range(len(units)):
            for i in range(len(other_chips)):
                own[u][1 + i].wait_recv()
                passed[u][i].start()
        for u in range(len(units)):
            own[u][0].wait_recv()
            for cp in passed[u]:
                cp.wait_recv()
        for cp in [c for unit in own + passed for c in unit]:
            cp.wait_send()
        for cp in local:
            cp.wait()

    outs = pl.pallas_call(body, name=name, in_specs=[HBM] * n_in, out_specs=[HBM] * len(out_shapes), out_shape=out_shapes,
                          scratch_shapes=_comm_scratch(len(units)))(*arrays)
    grouped = [[] for _ in arrays]
    for j, _, o in units:
        grouped[j].append(outs[o])
    return [tuple(g) for g in grouped]


def _exchange_blocks(entries, name):
    units, flat_in, out_shapes = [], [], []
    for j, entry in enumerate(entries):
        for layer, a in enumerate(entry):
            units.append((len(flat_in), layer, j))
            flat_in.append(a)
        out_shapes.append(_sds((N_DEV, len(entry)) + entry[0].shape[1:], entry[0].dtype))
    n_in = len(flat_in)

    def body(*refs):
        ins, outs = refs[:n_in], refs[n_in:n_in + len(out_shapes)]
        send_sems, recv_sems, local_sems = refs[n_in + len(out_shapes):]
        pos = _mesh_position()
        me = 4 * pos[0] + 2 * pos[1] + pos[2]
        local, remote = [], []
        for u, (i, layer, o) in enumerate(units):
            local.append(pltpu.make_async_copy(ins[i].at[me], outs[o].at[me, layer], local_sems.at[u]))
            for k in range(1, N_DEV):
                peer, peer_index = _peer(pos, k)
                remote.append(pltpu.make_async_remote_copy(
                    src_ref=ins[i].at[peer_index], dst_ref=outs[o].at[me, layer], send_sem=send_sems.at[u, k - 1],
                    recv_sem=recv_sems.at[u, k - 1], device_id=peer, device_id_type=MESH_ID))
        _run_copies(local, remote)

    return pl.pallas_call(body, name=name, in_specs=[HBM] * n_in, out_specs=[HBM] * len(out_shapes), out_shape=out_shapes,
                          scratch_shapes=_comm_scratch(len(units)))(*flat_in)


SEM = pl.BlockSpec(memory_space=pltpu.SEMAPHORE)
SIDE_EFFECT = pltpu.SideEffectType.DATAFLOW_SIDE_EFFECTING


def _own_slots(arrays, indexed, me, name):
    lands = []
    for u, a in enumerate(arrays):
        block = a.shape[1:] if indexed else a.shape
        rows, cols = _size(block[:-1]), block[-1]
        tr = _row_tile(rows, cap=512)

        def body(me_ref, src_ref, out_ref):
            out_ref[...] = src_ref[...]

        src_spec = (pl.BlockSpec((None, tr, cols), lambda i, me_ref: (me_ref[0], i, 0)) if indexed
                    else pl.BlockSpec((tr, cols), lambda i, me_ref: (i, 0)))
        land = pl.pallas_call(
            body, name=f"{name}_{u}", out_shape=_sds((N_DEV, rows, cols), a.dtype),
            grid_spec=pltpu.PrefetchScalarGridSpec(
                num_scalar_prefetch=1, grid=(rows // tr,), in_specs=[src_spec],
                out_specs=pl.BlockSpec((None, tr, cols), lambda i, me_ref: (me_ref[0], i, 0))),
        )(me, a.reshape((N_DEV, rows, cols) if indexed else (rows, cols)))
        lands.append(land.reshape((N_DEV,) + block))
    return lands


def _split_copies(srcs, lands, send_sems, recv_sems, indexed):
    pos = _mesh_position()
    me = 4 * pos[0] + 2 * pos[1] + pos[2]
    copies = []
    for u, (src, land) in enumerate(zip(srcs, lands)):
        for k in range(1, N_DEV):
            peer, peer_index = _peer(pos, k)
            copies.append(pltpu.make_async_remote_copy(
                src_ref=src.at[peer_index] if indexed else src, dst_ref=land.at[me],
                send_sem=send_sems.at[u * (N_DEV - 1) + k - 1], recv_sem=recv_sems.at[u * (N_DEV - 1) + k - 1],
                device_id=peer, device_id_type=MESH_ID))
    return copies


def _exchange_start(arrays, lands, indexed, name):
    n = len(arrays)

    def body(*refs):
        srcs, zones = refs[:n], refs[n:2 * n]
        send_sems, recv_sems = refs[2 * n], refs[2 * n + 1]
        token = refs[-1]
        for cp in _split_copies(srcs, zones, send_sems, recv_sems, indexed):
            cp.start()
        token[...] = jnp.zeros_like(token)

    sem_shape = pltpu.SemaphoreType.DMA((n * (N_DEV - 1),))
    outs = pl.pallas_call(
        body, name=name, in_specs=[HBM] * (2 * n),
        out_specs=[SEM, SEM] + [HBM] * (2 * n) + [pl.BlockSpec(memory_space=pltpu.VMEM)],
        out_shape=[sem_shape, sem_shape] + [pltpu.HBM(a.shape, a.dtype) for a in list(arrays) + list(lands)]
        + [_sds((SUBLANE, LANE))],
        input_output_aliases={i: 2 + i for i in range(2 * n)},
        compiler_params=pltpu.CompilerParams(has_side_effects=SIDE_EFFECT),
    )(*[pltpu.with_memory_space_constraint(a, pltpu.HBM) for a in list(arrays) + list(lands)])
    return outs[0], outs[1], outs[2:2 + n], outs[2 + n:2 + 2 * n], outs[-1]


def _exchange_wait(send_sems, recv_sems, arrays, lands, after, indexed, name):
    n = len(arrays)

    def body(*refs):
        srcs, zones = refs[:n], refs[n:2 * n]
        s_sems, r_sems = refs[2 * n], refs[2 * n + 1]
        for cp in _split_copies(srcs, zones, s_sems, r_sems, indexed):
            cp.wait_send()
            cp.wait_recv()

    outs = pl.pallas_call(
        body, name=name, in_specs=[HBM] * (2 * n) + [SEM, SEM, HBM],
        out_specs=[HBM] * (2 * n), out_shape=[pltpu.HBM(a.shape, a.dtype) for a in list(arrays) + list(lands)],
        input_output_aliases={i: i for i in range(2 * n)},
        compiler_params=pltpu.CompilerParams(has_side_effects=SIDE_EFFECT),
    )(*arrays, *lands, send_sems, recv_sems, after)
    return outs[n:]


SUM_TILE = 512


def _adamw(w, g, m, v):
    m = ADAM_B1 * m + (1.0 - ADAM_B1) * g
    v = ADAM_B2 * v + (1.0 - ADAM_B2) * (g * g)
    m_hat = m / (1.0 - ADAM_B1 ** ADAM_STEP)
    v_hat = v / (1.0 - ADAM_B2 ** ADAM_STEP)
    return -ADAM_LR * (m_hat / (jnp.sqrt(v_hat) + ADAM_EPS) + ADAM_WD * w), m, v


def _sum_adamw(recv, w, m, v, layer, others, name):
    _, rows, cols = w.shape
    tr = _row_tile(rows, cap=256)

    def body(r_ref, w_ref, m_ref, v_ref, *rest):
        g_ref, d_ref, mo_ref, vo_ref = rest[-4:]
        g = r_ref[0].astype(F32)
        for j in range(1, N_DEV):
            g = g + r_ref[j].astype(F32)
        g_ref[...] = g
        d_ref[...], mo_ref[...], vo_ref[...] = _adamw(w_ref[...], g, m_ref[...], v_ref[...])

    blk = pl.BlockSpec((None, tr, cols), lambda i: (layer, i, 0))
    carried = list(others) if others is not None else []
    return pl.pallas_call(
        body, name=name, grid=(rows // tr,),
        in_specs=[pl.BlockSpec((N_DEV, tr, cols), lambda i: (0, i, 0)), blk, blk, blk] + [HBM] * len(carried),
        out_specs=[blk] * 4, out_shape=[_sds(w.shape)] * 4,
        input_output_aliases={4 + k: k for k in range(len(carried))},
        compiler_params=pltpu.CompilerParams(dimension_semantics=("arbitrary",), vmem_limit_bytes=VMEM_LIMIT),
    )(recv, w, m, v, *carried)


def _sum_senders(recv, name):
    _, rows, cols = recv.shape
    tr = _row_tile(rows, cap=256)

    def body(r_ref, g_ref):
        g = r_ref[0].astype(F32)
        for j in range(1, N_DEV):
            g = g + r_ref[j].astype(F32)
        g_ref[...] = g

    return _call(body, name, (rows // tr,), [pl.BlockSpec((N_DEV, tr, cols), lambda i: (0, i, 0))], [_rows(tr, cols)],
                 [_sds((rows, cols))])(recv)[0]


def _adamw_blocks(g, w, m, v, name):
    n_lay, rows, cols = w.shape
    tr = _row_tile(rows, cap=256)

    def body(g_ref, w_ref, m_ref, v_ref, d_ref, mo_ref, vo_ref):
        d_ref[...], mo_ref[...], vo_ref[...] = _adamw(w_ref[...], g_ref[...], m_ref[...], v_ref[...])

    blk = pl.BlockSpec((None, tr, cols), lambda l, i: (l, i, 0))
    return pl.pallas_call(
        body, name=name, grid=(n_lay, rows // tr), in_specs=[blk] * 4, out_specs=[blk] * 3, out_shape=[_sds(w.shape)] * 3,
        compiler_params=pltpu.CompilerParams(dimension_semantics=("arbitrary", "arbitrary"), vmem_limit_bytes=VMEM_LIMIT),
    )(g, w, m, v)


def _sum_slots(recv, name):
    rows = recv.shape[1]

    def body(r_ref, g_ref):
        g = r_ref[0].astype(F32)
        for j in range(1, N_DEV):
            g = g + r_ref[j].astype(F32)
        g_ref[...] = g

    return _call(body, name, (1,), [_full(recv.shape)], [_full((rows, LANE))], [_sds((rows, LANE))])(recv)[0]


def _adamw_rows(g, w, m, v, name):
    rows = w.shape[0]
    tr = _row_tile(rows)

    def body(g_ref, w_ref, m_ref, v_ref, d_ref, mo_ref, vo_ref):
        d_ref[...], mo_ref[...], vo_ref[...] = _adamw(w_ref[...], g_ref[...], m_ref[...], v_ref[...])

    flat = _rows(tr, LANE)
    return _call(body, name, (rows // tr,), [flat] * 4, [flat] * 3, [_sds((rows, LANE))] * 3)(g, w, m, v)


def _row_tile(rows, cap=1024):
    if rows % SUBLANE:
        return rows
    best = SUBLANE
    for t in range(SUBLANE, cap + 1, SUBLANE):
        if rows % t == 0:
            best = t
    return best


BIG = (("w_in", (DEPTH, D_MODEL, IN_PROJ // N_DEV), 2),
       ("s5_w_glu", (DEPTH, D_MODEL // N_DEV, D_MODEL), 1),
       ("ssd_conv_w", (DEPTH, SSD_CONV, SSD_CONV_DIM // N_DEV), 2),
       ("w_out", (DEPTH, 2 * D_MODEL // N_DEV, D_MODEL), 1),
       ("w_gate", (DEPTH, D_MODEL, FFN_HIDDEN // N_DEV), 2),
       ("w_up", (DEPTH, D_MODEL, FFN_HIDDEN // N_DEV), 2),
       ("w_down", (DEPTH, FFN_HIDDEN // N_DEV, D_MODEL), 1))
SMALL = (("norm_mix", (DEPTH, D_MODEL)), ("s5_lam_re", (DEPTH, S5_GROUPS, S5_STATE)), ("s5_lam_im", (DEPTH, S5_GROUPS, S5_STATE)),
         ("s5_log_step", (DEPTH, S5_GROUPS)), ("s5_b_re", (DEPTH, S5_GROUPS, S5_STATE, S5_GROUP)),
         ("s5_b_im", (DEPTH, S5_GROUPS, S5_STATE, S5_GROUP)), ("s5_c_re", (DEPTH, S5_GROUPS, S5_GROUP, S5_STATE)),
         ("s5_c_im", (DEPTH, S5_GROUPS, S5_GROUP, S5_STATE)), ("s5_d", (DEPTH, D_MODEL)), ("s5_b_glu", (DEPTH, D_MODEL)),
         ("s5_norm", (DEPTH, D_MODEL)), ("ssd_conv_b", (DEPTH, SSD_CONV_DIM)), ("ssd_dt_bias", (DEPTH, SSD_HEADS)),
         ("ssd_a_log", (DEPTH, SSD_HEADS)), ("ssd_d", (DEPTH, SSD_HEADS)), ("ssd_norm", (DEPTH, D_MODEL)),
         ("norm_ffn", (DEPTH, D_MODEL)), ("norm_final", (D_MODEL,)))
WEIGHT_ORDER = ("norm_mix", "w_in", "s5_lam_re", "s5_lam_im", "s5_log_step", "s5_b_re", "s5_b_im", "s5_c_re", "s5_c_im", "s5_d",
                "s5_w_glu", "s5_b_glu", "s5_norm", "ssd_conv_w", "ssd_conv_b", "ssd_dt_bias", "ssd_a_log", "ssd_d", "ssd_norm",
                "w_out", "norm_ffn", "w_gate", "w_up", "w_down", "norm_final")


def _size(shape):
    n = 1
    for s in shape:
        n *= s
    return n


def _round_up(n, m):
    return -(-n // m) * m


SMALL_SIZE = sum(_size(s) for _, s in SMALL)
SMALL_ROWS = _round_up(-(-SMALL_SIZE // (N_DEV * LANE)), SUBLANE)


def _pack(parts, rows, dtype):
    flat = jnp.concatenate([p.reshape(-1).astype(dtype) for p in parts])
    return jnp.pad(flat, (0, rows * LANE - flat.shape[0])).reshape(rows, LANE)


def _unpack(flat, specs):
    out, off = {}, 0
    flat = flat.reshape(-1)
    for name, shape in specs:
        out[name] = flat[off:off + _size(shape)].reshape(shape)
        off += _size(shape)
    return out


def kernel(x, norm_mix, w_in, s5_lam_re, s5_lam_im, s5_log_step, s5_b_re, s5_b_im, s5_c_re, s5_c_im, s5_d, s5_w_glu, s5_b_glu, s5_norm, ssd_conv_w, ssd_conv_b, ssd_dt_bias, ssd_a_log, ssd_d, ssd_norm, w_out, norm_ffn, w_gate, w_up, w_down, norm_final, loss_target, m_norm_mix, m_w_in, m_s5_lam_re, m_s5_lam_im, m_s5_log_step, m_s5_b_re, m_s5_b_im, m_s5_c_re, m_s5_c_im, m_s5_d, m_s5_w_glu, m_s5_b_glu, m_s5_norm, m_ssd_conv_w, m_ssd_conv_b, m_ssd_dt_bias, m_ssd_a_log, m_ssd_d, m_ssd_norm, m_w_out, m_norm_ffn, m_w_gate, m_w_up, m_w_down, m_norm_final, v_norm_mix, v_w_in, v_s5_lam_re, v_s5_lam_im, v_s5_log_step, v_s5_b_re, v_s5_b_im, v_s5_c_re, v_s5_c_im, v_s5_d, v_s5_w_glu, v_s5_b_glu, v_s5_norm, v_ssd_conv_w, v_ssd_conv_b, v_ssd_dt_bias, v_ssd_a_log, v_ssd_d, v_ssd_norm, v_w_out, v_norm_ffn, v_w_gate, v_w_up, v_w_down, v_norm_final):
    given = dict(locals())
    w = {n: given[n] for n in WEIGHT_ORDER}
    m = {n: given["m_" + n] for n in WEIGHT_ORDER}
    v = {n: given["v_" + n] for n in WEIGHT_ORDER}
    big_names = tuple(n for n, _, _ in BIG)
    matmul_names = tuple(n for n in big_names if n != "ssd_conv_w")

    conv_hi = w["ssd_conv_w"].astype(BF16)
    conv_lo = (w["ssd_conv_w"] - conv_hi.astype(F32)).astype(BF16)
    row_pad = ((0, 0), (0, FFN_BLOCK_PAD - FFN_BLOCK), (0, 0))
    as_rows = {"w_gate": jnp.swapaxes(w["w_gate"], 1, 2), "w_up": jnp.swapaxes(w["w_up"], 1, 2), "w_down": w["w_down"]}
    to_send = [jnp.pad(as_rows[n].astype(BF16), row_pad) if n in as_rows else w[n].astype(BF16) for n in matmul_names]

    def layer_blocks(i):
        return [a[i] for a in to_send] + [jnp.stack([conv_hi[i], conv_lo[i]])]

    def as_layer_weights(gathered):
        blk = dict(zip(matmul_names, gathered))
        pair = gathered[-1].astype(F32)
        blk["ssd_conv_w"] = jnp.moveaxis(pair[:, 0] + pair[:, 1], 0, 1).reshape(SSD_CONV, SSD_CONV_DIM)
        return blk

    gathered0 = [g[0] for g in _gather_blocks(layer_blocks(0), [False] * (len(matmul_names) + 1), name="gather_weights_0")]
    blocks1 = layer_blocks(1)
    me = (4 * lax.axis_index("x") + 2 * lax.axis_index("y") + lax.axis_index("c")).astype(jnp.int32).reshape(1)
    sems1 = _exchange_start(blocks1, _own_slots(blocks1, False, me, name="gather_own_1"), False, name="gather_start_1")
    prepared = [_prepare_layer(w, as_layer_weights(gathered0), 0, sems1[-1]), None]
    saved = [None, None]
    h, saved[0] = _layer_fwd(x[0], prepared[0], 0)
    gathered1 = _exchange_wait(*sems1[:4], h, False, name="gather_wait_1")
    prepared[1] = _prepare_layer(w, as_layer_weights(gathered1), 1, sems1[-1])
    h, saved[1] = _layer_fwd(h, prepared[1], 1)
    loss, dh, g_final = _loss_head(h, w["norm_final"].reshape(1, D_MODEL), loss_target[0], name="loss_head")

    layer_grads = [None, None]
    dh, layer_grads[1] = _layer_bwd(dh, saved[1], prepared[1], 1, sems1[-1])
    slots1 = [layer_grads[1][n] for n in big_names]
    sems2 = _exchange_start(slots1, _own_slots(slots1, True, me, name="exchange_own_1"), True, name="exchange_start_1")
    early_names = ("w_out", "w_gate", "w_up", "w_down")
    late_names = tuple(n for n in big_names if n not in early_names)
    early = {}

    def send_early(g):
        slots = [g[n] for n in early_names]
        early["sems"] = _exchange_start(slots, _own_slots(slots, True, me, name="exchange_own_0"), True, name="exchange_start_0")
        return early["sems"][-1]

    grad_x, layer_grads[0] = _layer_bwd(dh, saved[0], prepared[0], 0, sems2[-1], between=send_early)

    small = jnp.concatenate([g_final.reshape(-1) if n == "norm_final"
                             else jnp.stack([layer_grads[i][n] for i in range(DEPTH)]).reshape(-1) for n, _ in SMALL])
    small_slots = jnp.pad(small, (0, N_DEV * SMALL_ROWS * LANE - small.shape[0])).reshape(N_DEV, SMALL_ROWS, LANE)
    late = [layer_grads[0][n] for n in late_names] + [small_slots]
    sems3 = _exchange_start(late, _own_slots(late, True, me, name="exchange_own_late"), True, name="exchange_start_late")
    received1 = _exchange_wait(*sems2[:4], sems3[-1], True, name="exchange_wait_1")
    received_early = _exchange_wait(*early["sems"][:4], sems3[-1], True, name="exchange_wait_0")
    received0 = dict(zip(early_names, received_early))

    transposed = ("w_gate", "w_up")
    layer1 = {n: _sum_adamw(received1[j], w[n], m[n], v[n], 1, None, name=f"sum_adamw_{n}_1")
              for j, n in enumerate(big_names) if n not in transposed}
    results = {}
    for n in transposed:
        recv = (received0[n], received1[big_names.index(n)])
        g = jnp.stack([jnp.swapaxes(_sum_senders(recv[i], name=f"sum_{n}_{i}"), 0, 1) for i in range(DEPTH)])
        results[n] = [g, *_adamw_blocks(g, w[n], m[n], v[n], name=f"adamw_{n}")]
    for n in early_names:
        if n not in transposed:
            results[n] = _sum_adamw(received0[n], w[n], m[n], v[n], 0, layer1[n], name=f"sum_adamw_{n}_0")
    received_late = _exchange_wait(*sems3[:4], results["w_down"][0], True, name="exchange_wait_late")
    for n, recv in zip(late_names, received_late):
        results[n] = _sum_adamw(recv, w[n], m[n], v[n], 0, layer1[n], name=f"sum_adamw_{n}_0")
    g_part = _sum_slots(received_late[-1], name="sum_replicated")
    g_small = _gather_blocks([g_part], [False], name="gather_replicated")[0][0].reshape(N_DEV * SMALL_ROWS, LANE)
    small_rows = N_DEV * SMALL_ROWS
    d_small, m_small, v_small = _adamw_rows(
        g_small, _pack([w[n] for n, _ in SMALL], small_rows, F32), _pack([m[n] for n, _ in SMALL], small_rows, F32),
        _pack([v[n] for n, _ in SMALL], small_rows, F32), name="adamw_replicated")
    for k, packed in enumerate((g_small, d_small, m_small, v_small)):
        for n, arr in _unpack(packed, SMALL).items():
            results.setdefault(n, [None] * 4)[k] = arr

    outs = [results[n][k] for k in range(4) for n in WEIGHT_ORDER]
    total_loss = lax.psum(loss[0, 0], ("x", "y", "c"))
    return (total_loss, grad_x[None], *outs)
```

```python
import functools

import jax
import jax.numpy as jnp
from jax import lax
from jax.experimental import pallas as pl
from jax.experimental.pallas import tpu as pltpu

F32 = jnp.float32
BF16 = jnp.bfloat16
MESH_ID = pl.DeviceIdType.MESH

N_DEV = 8
DEPTH = 2
D_MODEL = 1024
S5_GROUPS = 64
S5_GROUP = 16
S5_STATE = 64
S5_LANES = S5_GROUPS * S5_STATE
SSD_HEADS = 16
SSD_HEAD_DIM = 64
SSD_STATE = 128
SSD_CHUNK = 128
SSD_CONV = 4
SSD_CONV_DIM = 1536
FFN_HIDDEN = 2816
IN_PROJ = 3600
EPS = 1e-6
LANE = 128
SUBLANE = 8
VMEM_LIMIT = 56 * 1024 * 1024

ADAM_LR = 0.001
ADAM_B1 = 0.9
ADAM_B2 = 0.999
ADAM_EPS = 1e-08
ADAM_WD = 0.01
ADAM_STEP = 10

TOK_TILE = 256
S5_TILE = 128


def _sigmoid(x):
    return jax.nn.sigmoid(x)


def _silu(x):
    return x * _sigmoid(x)


def _gelu(x):
    return 0.5 * x * (1.0 + jnp.tanh(0.7978845608028654 * (x + 0.044715 * (x * x * x))))


def _softplus(x):
    return jnp.maximum(x, 0.0) + jnp.log(1.0 + jnp.exp(-jnp.abs(x)))


def _rms(x, g):
    r = lax.rsqrt(jnp.mean(x * x, axis=-1, keepdims=True) + EPS)
    return x * r * g


def _nn(a, b):
    return lax.dot_general(a.astype(BF16), b.astype(BF16), (((1,), (0,)), ((), ())), preferred_element_type=F32)


def _nt(a, b):
    return lax.dot_general(a.astype(BF16), b.astype(BF16), (((1,), (1,)), ((), ())), preferred_element_type=F32)


def _tn(a, b):
    return lax.dot_general(a.astype(BF16), b.astype(BF16), (((0,), (0,)), ((), ())), preferred_element_type=F32)


def _nn_f32(a, b):
    return lax.dot_general(a, b, (((1,), (0,)), ((), ())), precision=lax.Precision.HIGHEST, preferred_element_type=F32)


def _tn_f32(a, b):
    return lax.dot_general(a, b, (((0,), (0,)), ((), ())), precision=lax.Precision.HIGHEST, preferred_element_type=F32)


@jax.custom_vjp
def _nn_d(a, b):
    return _nn(a, b)


_nn_d.defvjp(lambda a, b: (_nn(a, b), (a, b)), lambda r, g: (_nt(g, r[1]), _tn(r[0], g)))


@jax.custom_vjp
def _nt_d(a, b):
    return _nt(a, b)


_nt_d.defvjp(lambda a, b: (_nt(a, b), (a, b)), lambda r, g: (_nn(g, r[1]), _tn(g, r[0])))


@jax.custom_vjp
def _tn_d(a, b):
    return _tn(a, b)


_tn_d.defvjp(lambda a, b: (_tn(a, b), (a, b)), lambda r, g: (_nt(r[1], g), _nn(r[0], g)))


@jax.custom_vjp
def _cumsum_rows(tri, x):
    return _nn_f32(tri, x)


_cumsum_rows.defvjp(lambda tri, x: (_nn_f32(tri, x), tri), lambda tri, g: (jnp.zeros_like(tri), _tn_f32(tri, g)))


def _full(shape):
    zeros = (0,) * len(shape)
    return pl.BlockSpec(shape, lambda *_: zeros)


def _const(shape):
    zeros = (0,) * len(shape)
    return pl.BlockSpec(shape, lambda *_: zeros, pipeline_mode=pl.Buffered(1))


def _rows(tile, width, n_tiles=None):
    if n_tiles is None:
        return pl.BlockSpec((tile, width), lambda i: (i, 0))
    return pl.BlockSpec((tile, width), lambda i: (n_tiles - 1 - i, 0))


def _call(body, name, grid, in_specs, out_specs, out_shape, scratch=()):
    return pl.pallas_call(
        body, name=name, grid=grid, in_specs=in_specs, out_specs=out_specs, out_shape=out_shape,
        scratch_shapes=list(scratch),
        compiler_params=pltpu.CompilerParams(dimension_semantics=("arbitrary",) * len(grid),
                                             vmem_limit_bytes=VMEM_LIMIT))


def _sds(shape, dtype=F32):
    return jax.ShapeDtypeStruct(shape, dtype)


def _tile_of(n, cap=512):
    if n <= LANE:
        return n
    best = LANE
    for t in range(LANE, cap + 1, LANE):
        if n % t == 0:
            best = t
    return best


def _inproj_fwd(x, nm, wu, wz, wx, wd, name):
    n_tok = x.shape[0]
    tm = TOK_TILE

    def body(x_ref, nm_ref, wu_ref, wz_ref, wx_ref, wd_ref, u_ref, z_ref, xbc_ref, dt_ref):
        h = _rms(x_ref[...], nm_ref[...]).astype(BF16)
        u_ref[...] = _nn(h, wu_ref[...])
        z_ref[...] = _nn(h, wz_ref[...])
        xbc_ref[...] = _nn(h, wx_ref[...])
        dt_ref[...] = _nn(h, wd_ref[...])

    return _call(
        body, name, (n_tok // tm,),
        [_rows(tm, D_MODEL), _const((1, D_MODEL)), _const(wu.shape), _const(wz.shape), _const(wx.shape), _const(wd.shape)],
        [_rows(tm, D_MODEL), _rows(tm, D_MODEL), _rows(tm, SSD_CONV_DIM), _rows(tm, LANE)],
        [_sds((n_tok, D_MODEL)), _sds((n_tok, D_MODEL)), _sds((n_tok, SSD_CONV_DIM)), _sds((n_tok, LANE))],
    )(x, nm, wu, wz, wx, wd)


def _inproj_bwd(x, nm, du, dz, dxbc, ddt, dres, wu, wz, wx, wd, name):
    n_tok = x.shape[0]
    tm = TOK_TILE

    def body(x_ref, nm_ref, du_ref, dz_ref, dxbc_ref, ddt_ref, dres_ref, wu_ref, wz_ref, wx_ref, wd_ref,
             dx_ref, h_ref, dnm_ref):
        dh = (_nt(du_ref[...], wu_ref[...]) + _nt(dz_ref[...], wz_ref[...])
              + _nt(dxbc_ref[...], wx_ref[...]) + _nt(ddt_ref[...], wd_ref[...]))
        h, vjp = jax.vjp(_rms, x_ref[...], nm_ref[...])
        dx, dnm = vjp(dh)
        dx_ref[...] = dres_ref[...] + dx
        h_ref[...] = h.astype(BF16)

        @pl.when(pl.program_id(0) == 0)
        def _():
            dnm_ref[...] = jnp.zeros_like(dnm_ref)

        dnm_ref[...] += dnm

    return _call(
        body, name, (n_tok // tm,),
        [_rows(tm, D_MODEL), _const((1, D_MODEL)), _rows(tm, D_MODEL), _rows(tm, D_MODEL), _rows(tm, SSD_CONV_DIM),
         _rows(tm, LANE), _rows(tm, D_MODEL), _const(wu.shape), _const(wz.shape), _const(wx.shape), _const(wd.shape)],
        [_rows(tm, D_MODEL), _rows(tm, D_MODEL), _full((1, D_MODEL))],
        [_sds((n_tok, D_MODEL)), _sds((n_tok, D_MODEL), BF16), _sds((1, D_MODEL))],
    )(x, nm, du, dz, dxbc, ddt, dres, wu, wz, wx, wd)


def _ffn_act(gt, up):
    return _silu(gt) * up


FFN_BLOCK = FFN_HIDDEN // N_DEV
FFN_BLOCK_PAD = -(-FFN_BLOCK // LANE) * LANE


FFN_PAD = N_DEV * FFN_BLOCK_PAD


def _mix_ffn_fwd(x0, ya, yb, wo, nf, wg, wu, wd, name):
    n_tok = x0.shape[0]
    tm = TOK_TILE

    def body(x0_ref, ya_ref, yb_ref, wo_ref, nf_ref, wg_ref, wu_ref, wd_ref, x1_ref, x2_ref):
        x1 = x0_ref[...] + _nn(ya_ref[...], wo_ref[:D_MODEL, :]) + _nn(yb_ref[...], wo_ref[D_MODEL:, :])
        h = _rms(x1, nf_ref[...]).astype(BF16)
        x1_ref[...] = x1
        x2_ref[...] = x1 + _nn(_ffn_act(_nt(h, wg_ref[...]), _nt(h, wu_ref[...])), wd_ref[...])

    return _call(
        body, name, (n_tok // tm,),
        [_rows(tm, D_MODEL), _rows(tm, D_MODEL), _rows(tm, D_MODEL), _const(wo.shape),
         _const((1, D_MODEL)), _const(wg.shape), _const(wu.shape), _const(wd.shape)],
        [_rows(tm, D_MODEL), _rows(tm, D_MODEL)],
        [_sds((n_tok, D_MODEL)), _sds((n_tok, D_MODEL))],
    )(x0, ya, yb, wo, nf, wg, wu, wd)


def _mix_ffn_bwd(x1, dx2, wo, nf, wg, wu, wd, after, name):
    n_tok = x1.shape[0]
    tm = TOK_TILE
    n_chunks = 3
    hc = FFN_PAD // n_chunks

    def body(x1_ref, dx2_ref, wo_ref, nf_ref, wg_ref, wu_ref, wd_ref, after_ref,
             dx1_ref, dya_ref, dyb_ref, h_ref, a_ref, dgt_ref, dup_ref, dx2b_ref, dx1b_ref, dnf_ref):
        dx2 = dx2_ref[...]
        dx2b = dx2.astype(BF16)
        dx2b_ref[...] = dx2b
        h, rms_vjp = jax.vjp(_rms, x1_ref[...], nf_ref[...])
        hb = h.astype(BF16)
        dh = jnp.zeros_like(h)
        for c in range(n_chunks):
            rows = pl.ds(c * hc, hc)
            a, act_vjp = jax.vjp(_ffn_act, _nt(hb, wg_ref[rows, :]), _nt(hb, wu_ref[rows, :]))
            dgt, dup = act_vjp(_nt(dx2b, wd_ref[rows, :]))
            a_ref[:, c * hc:(c + 1) * hc] = a.astype(BF16)
            dgt_ref[:, c * hc:(c + 1) * hc] = dgt.astype(BF16)
            dup_ref[:, c * hc:(c + 1) * hc] = dup.astype(BF16)
            dh = dh + _nn(dgt, wg_ref[rows, :]) + _nn(dup, wu_ref[rows, :])
        dx, dnf = rms_vjp(dh)
        dx1 = dx2 + dx
        dx1b = dx1.astype(BF16)
        dx1_ref[...] = dx1
        dx1b_ref[...] = dx1b
        dya_ref[...] = _nt(dx1b, wo_ref[:D_MODEL, :])
        dyb_ref[...] = _nt(dx1b, wo_ref[D_MODEL:, :])
        h_ref[...] = hb

        @pl.when(pl.program_id(0) == 0)
        def _():
            dnf_ref[...] = jnp.zeros_like(dnf_ref)

        dnf_ref[...] += dnf

    hidden = _rows(tm, FFN_PAD)
    return _call(
        body, name, (n_tok // tm,),
        [_rows(tm, D_MODEL), _rows(tm, D_MODEL), _const(wo.shape), _const((1, D_MODEL)),
         _const(wg.shape), _const(wu.shape), _const(wd.shape), HBM],
        [_rows(tm, D_MODEL), _rows(tm, D_MODEL), _rows(tm, D_MODEL), _rows(tm, D_MODEL), hidden, hidden, hidden,
         _rows(tm, D_MODEL), _rows(tm, D_MODEL), _full((1, D_MODEL))],
        [_sds((n_tok, D_MODEL)), _sds((n_tok, D_MODEL)), _sds((n_tok, D_MODEL)), _sds((n_tok, D_MODEL), BF16),
         _sds((n_tok, FFN_PAD), BF16), _sds((n_tok, FFN_PAD), BF16), _sds((n_tok, FFN_PAD), BF16),
         _sds((n_tok, D_MODEL), BF16), _sds((n_tok, D_MODEL), BF16), _sds((1, D_MODEL))],
    )(x1, dx2, wo, nf, wg, wu, wd, after)


def _loss_head(x, nf, target, name):
    n_tok = x.shape[0]
    tm = TOK_TILE

    def loss_of(xv, g, t):
        e = _rms(xv, g) - t
        return 0.5 * jnp.sum(jnp.sum(e * e, axis=-1, keepdims=True) * (1.0 / D_MODEL), axis=0, keepdims=True)

    def body(x_ref, nf_ref, t_ref, loss_ref, dx_ref, dnf_ref):
        loss, vjp = jax.vjp(functools.partial(loss_of, t=t_ref[...]), x_ref[...], nf_ref[...])
        dx, dnf = vjp(jnp.ones_like(loss))
        dx_ref[...] = dx

        @pl.when(pl.program_id(0) == 0)
        def _():
            dnf_ref[...] = jnp.zeros_like(dnf_ref)
            loss_ref[...] = jnp.zeros_like(loss_ref)

        dnf_ref[...] += dnf
        loss_ref[...] += jnp.broadcast_to(loss, loss_ref.shape)

    return _call(
        body, name, (n_tok // tm,),
        [_rows(tm, D_MODEL), _const((1, D_MODEL)), _rows(tm, D_MODEL)],
        [_full((SUBLANE, LANE)), _rows(tm, D_MODEL), _full((1, D_MODEL))],
        [_sds((SUBLANE, LANE)), _sds((n_tok, D_MODEL)), _sds((1, D_MODEL))],
    )(x, nf, target)


GRAD_WIRE = BF16


def _matmul_tn(a, b, name):
    n_tok, k1 = a.shape
    k2 = b.shape[1]
    t1 = _tile_of(k1)

    def body(a_ref, b_ref, o_ref):
        o_ref[...] = _tn(a_ref[...], b_ref[...]).astype(GRAD_WIRE)

    return _call(body, name, (k1 // t1,), [pl.BlockSpec((n_tok, t1), lambda i: (0, i)), _const((n_tok, k2))],
                 [pl.BlockSpec((t1, k2), lambda i: (i, 0))], [_sds((k1, k2), GRAD_WIRE)])(a, b)[0]


def _matmul_tn_lhs_blocks(a, b, width, keep, name):
    n_tok, k1 = a.shape
    k2 = b.shape[1]

    def body(a_ref, b_ref, o_ref):
        o_ref[...] = _tn(a_ref[...], b_ref[...])[:keep, :].astype(GRAD_WIRE)

    return _call(body, name, (k1 // width,), [pl.BlockSpec((n_tok, width), lambda d: (0, d)), _const((n_tok, k2))],
                 [pl.BlockSpec((None, keep, k2), lambda d: (d, 0, 0))], [_sds((k1 // width, keep, k2), GRAD_WIRE)])(a, b)[0]


def _matmul_tn_pair(a0, a1, b, name):
    n_tok, k1 = a0.shape
    k2 = b.shape[1]
    t1 = _tile_of(k1)
    n1 = k1 // t1

    def body(a0_ref, a1_ref, b_ref, o_ref):
        @pl.when(pl.program_id(0) < n1)
        def _():
            o_ref[...] = _tn(a0_ref[...], b_ref[...]).astype(GRAD_WIRE)

        @pl.when(pl.program_id(0) >= n1)
        def _():
            o_ref[...] = _tn(a1_ref[...], b_ref[...]).astype(GRAD_WIRE)

    return _call(
        body, name, (2 * n1,),
        [pl.BlockSpec((n_tok, t1), lambda i: (0, jnp.minimum(i, n1 - 1))),
         pl.BlockSpec((n_tok, t1), lambda i: (0, jnp.maximum(i - n1, 0))), _const((n_tok, k2))],
        [pl.BlockSpec((None, t1, k2), lambda i: (i // n1, i % n1, 0))], [_sds((2, k1, k2), GRAD_WIRE)])(a0, a1, b)[0]


W_IN_BLOCK = IN_PROJ // N_DEV
W_IN_SPLITS = (D_MODEL, 2 * D_MODEL, 2 * D_MODEL + SSD_CONV_DIM)
RELAYOUT_TILE = 256


def _w_in_split(blocks, after, name):
    tr = RELAYOUT_TILE

    def body(b_ref, after_ref, wu_ref, wz_ref, wx_ref, wd_ref):
        full = jnp.concatenate([b_ref[d] for d in range(N_DEV)], axis=1)
        wu_ref[...] = full[:, :W_IN_SPLITS[0]]
        wz_ref[...] = full[:, W_IN_SPLITS[0]:W_IN_SPLITS[1]]
        wx_ref[...] = full[:, W_IN_SPLITS[1]:W_IN_SPLITS[2]]
        wd_ref[...] = jnp.concatenate([full[:, W_IN_SPLITS[2]:], jnp.zeros((tr, LANE - SSD_HEADS), full.dtype)], axis=1)

    return _call(
        body, name, (D_MODEL // tr,), [pl.BlockSpec((N_DEV, tr, W_IN_BLOCK), lambda i: (0, i, 0)), HBM],
        [_rows(tr, D_MODEL), _rows(tr, D_MODEL), _rows(tr, SSD_CONV_DIM), _rows(tr, LANE)],
        [_sds((D_MODEL, D_MODEL), BF16), _sds((D_MODEL, D_MODEL), BF16), _sds((D_MODEL, SSD_CONV_DIM), BF16),
         _sds((D_MODEL, LANE), BF16)],
    )(blocks, after)


def _w_in_grad_blocks(gu, gz, gx, gdt, name):
    tr = RELAYOUT_TILE

    def body(gu_ref, gz_ref, gx_ref, gdt_ref, o_ref):
        full = jnp.concatenate([gu_ref[...], gz_ref[...], gx_ref[...], gdt_ref[...]], axis=1)
        for d in range(N_DEV):
            o_ref[d] = full[:, d * W_IN_BLOCK:(d + 1) * W_IN_BLOCK]

    return _call(
        body, name, (D_MODEL // tr,),
        [_rows(tr, D_MODEL), _rows(tr, D_MODEL), _rows(tr, SSD_CONV_DIM), _rows(tr, LANE)],
        [pl.BlockSpec((N_DEV, tr, W_IN_BLOCK), lambda i: (0, i, 0))], [_sds((N_DEV, D_MODEL, W_IN_BLOCK), gu.dtype)],
    )(gu, gz, gx, gdt)[0]


S5_SLICES = D_MODEL // LANE
S5_SLICE_STATES = S5_LANES // S5_SLICES
SCAN_LANES = 512


def _s5_scan(br_ref, bi_ref, a_r, a_i, car_r, car_i, ini_r, ini_i, reverse, xr_ref=None, xi_ref=None,
             acc_r=None, acc_i=None):
    n_rows = br_ref.shape[1]
    seg = n_rows // SUBLANE
    order = range(SUBLANE - 1, -1, -1) if reverse else range(SUBLANE)

    def rows(t):
        return pl.ds(pl.multiple_of(((seg - 1 - t) if reverse else t) * SUBLANE, SUBLANE), SUBLANE)

    tiles_per = SCAN_LANES // LANE

    def load(ref, t, lb):
        return jnp.concatenate([ref[lb * tiles_per + j, rows(t), :] for j in range(tiles_per)], axis=1)

    def store(ref, t, lb, val):
        for j in range(tiles_per):
            ref[lb * tiles_per + j, rows(t), :] = val[:, j * LANE:(j + 1) * LANE]

    for lb in range(S5_LANES // SCAN_LANES):
        lanes = pl.ds(lb * SCAN_LANES, SCAN_LANES)
        ar1, ai1 = a_r[:, lb * SCAN_LANES:(lb + 1) * SCAN_LANES], a_i[:, lb * SCAN_LANES:(lb + 1) * SCAN_LANES]
        ar8 = jnp.broadcast_to(ar1, (SUBLANE, SCAN_LANES))
        ai8 = jnp.broadcast_to(ai1, (SUBLANE, SCAN_LANES))

        def local(t, c):
            sr, si = c
            return (ar8 * sr - ai8 * si + load(br_ref, t, lb), ar8 * si + ai8 * sr + load(bi_ref, t, lb))

        zero = jnp.zeros((SUBLANE, SCAN_LANES), F32)
        er, ei = lax.fori_loop(0, seg, local, (zero, zero))
        pr, pi = ar1, ai1
        for _ in range(seg.bit_length() - 1):
            pr, pi = pr * pr - pi * pi, 2.0 * pr * pi
        cr, ci = car_r[:, lanes], car_i[:, lanes]
        for s in order:
            ini_r[s:s + 1, lanes] = cr
            ini_i[s:s + 1, lanes] = ci
            cr, ci = pr * cr - pi * ci + er[s:s + 1, :], pr * ci + pi * cr + ei[s:s + 1, :]
        car_r[:, lanes] = cr
        car_i[:, lanes] = ci

        if xr_ref is None:
            def final(t, c):
                sr, si = c
                nr = ar8 * sr - ai8 * si + load(br_ref, t, lb)
                ni = ar8 * si + ai8 * sr + load(bi_ref, t, lb)
                store(br_ref, t, lb, nr)
                store(bi_ref, t, lb, ni)
                return nr, ni

            lax.fori_loop(0, seg, final, (ini_r[:, lanes], ini_i[:, lanes]))
        else:
            def final_acc(t, c):
                sr, si, gr, gi = c
                xr, xi = load(xr_ref, t, lb), load(xi_ref, t, lb)
                gr = gr + sr * xr + si * xi
                gi = gi + si * xr - sr * xi
                nr = ar8 * sr - ai8 * si + load(br_ref, t, lb)
                ni = ar8 * si + ai8 * sr + load(bi_ref, t, lb)
                store(br_ref, t, lb, nr)
                store(bi_ref, t, lb, ni)
                return nr, ni, gr, gi

            _, _, gr, gi = lax.fori_loop(0, seg, final_acc,
                                         (ini_r[:, lanes], ini_i[:, lanes], acc_r[:, lanes], acc_i[:, lanes]))
            acc_r[:, lanes] = gr
            acc_i[:, lanes] = gi


def _s5_tail(gg, q, sn):
    return _rms(gg * _sigmoid(q), sn)


def _scan_order(n_rows):
    seg = n_rows // SUBLANE
    r = lax.broadcasted_iota(jnp.int32, (n_rows, n_rows), 0)
    c = lax.broadcasted_iota(jnp.int32, (n_rows, n_rows), 1)
    return (c == (r % SUBLANE) * seg + r // SUBLANE).astype(F32)


S5_STATE_TILES = S5_LANES // LANE
TILES_PER_SLICE = S5_SLICE_STATES // LANE


def _put_states(ref, k, val):
    for j in range(TILES_PER_SLICE):
        ref[k * TILES_PER_SLICE + j] = val[:, j * LANE:(j + 1) * LANE]


def _get_states(ref, k):
    return jnp.concatenate([ref[k * TILES_PER_SLICE + j] for j in range(TILES_PER_SLICE)], axis=1)


def _state_rows(tile, n_tiles=None):
    if n_tiles is None:
        return pl.BlockSpec((S5_STATE_TILES, tile, LANE), lambda i: (0, i, 0))
    return pl.BlockSpec((S5_STATE_TILES, tile, LANE), lambda i: (0, n_tiles - 1 - i, 0))


def _s5_fwd(u, a_r, a_i, bdb, bcr, bci, dsk, wglu, bglu, sn, name):
    n_tok = u.shape[0]
    tc = S5_TILE
    sw = S5_SLICE_STATES

    def body(u_ref, ar_ref, ai_ref, bdb_ref, bcr_ref, bci_ref, d_ref, wg_ref, bg_ref, sn_ref,
             ya_ref, xr_ref, xi_ref, v_ref, car_r, car_i, ini_r, ini_i):
        @pl.when(pl.program_id(0) == 0)
        def _():
            car_r[...] = jnp.zeros_like(car_r)
            car_i[...] = jnp.zeros_like(car_i)

        order = _scan_order(tc)
        u_t = _nn_f32(order, u_ref[...])
        ub = u_t.astype(BF16)
        for k in range(S5_SLICES):
            bu = _nn(ub[:, k * LANE:(k + 1) * LANE], bdb_ref[k])
            _put_states(xr_ref, k, bu[:, :sw])
            _put_states(xi_ref, k, bu[:, sw:])
        _s5_scan(xr_ref, xi_ref, ar_ref[...], ai_ref[...], car_r, car_i, ini_r, ini_i, reverse=False)
        vs = [_nn(_get_states(xr_ref, k), bcr_ref[k]) - _nn(_get_states(xi_ref, k), bci_ref[k])
              for k in range(S5_SLICES)]
        v = jnp.concatenate(vs, axis=1) + d_ref[...] * u_t
        v_ref[...] = v
        gg = _gelu(v)
        ya_ref[...] = _tn_f32(order, _s5_tail(gg, _nn(gg, wg_ref[...]) + bg_ref[...], sn_ref[...])).astype(BF16)

    return _call(
        body, name, (n_tok // tc,),
        [_rows(tc, D_MODEL), _const((1, S5_LANES)), _const((1, S5_LANES)), _const(bdb.shape), _const(bcr.shape),
         _const(bci.shape), _const((1, D_MODEL)), _const(wglu.shape), _const((1, D_MODEL)), _const((1, D_MODEL))],
        [_rows(tc, D_MODEL), _state_rows(tc), _state_rows(tc), _rows(tc, D_MODEL)],
        [_sds((n_tok, D_MODEL), BF16), _sds((S5_STATE_TILES, n_tok, LANE)), _sds((S5_STATE_TILES, n_tok, LANE)),
         _sds((n_tok, D_MODEL))],
        scratch=[pltpu.VMEM((1, S5_LANES), F32), pltpu.VMEM((1, S5_LANES), F32),
                 pltpu.VMEM((SUBLANE, S5_LANES), F32), pltpu.VMEM((SUBLANE, S5_LANES), F32)],
    )(u, a_r, a_i, bdb, bcr, bci, dsk, wglu, bglu, sn)


def _s5_bwd(dya, v, u, xr, xi, a_r, a_i, bdb, bcr, bci, dsk, wglu, bglu, sn, after, name):
    n_tok = u.shape[0]
    tc = S5_TILE
    nt = n_tok // tc
    sw = S5_SLICE_STATES

    def body(dya_ref, v_ref, u_ref, xr_ref, xi_ref, ar_ref, ai_ref, bdb_ref, bcr_ref, bci_ref, d_ref, wg_ref, bg_ref, sn_ref,
             after_ref, du_ref, gg_ref, dq_ref, gbdb_ref, gbcr_ref, gbci_ref, gar_ref, gai_ref, gd_ref, gbg_ref, gsn_ref,
             gr_ref, gi_ref, car_r, car_i, ini_r, ini_i):
        @pl.when(pl.program_id(0) == 0)
        def _():
            for r in (car_r, car_i, gbdb_ref, gbcr_ref, gbci_ref, gar_ref, gai_ref, gd_ref, gbg_ref, gsn_ref):
                r[...] = jnp.zeros_like(r)

        order = _scan_order(tc)
        u_t = _nn_f32(order, u_ref[...])
        gg, gelu_vjp = jax.vjp(_gelu, v_ref[...])
        _, tail_vjp = jax.vjp(_s5_tail, gg, _nn(gg, wg_ref[...]) + bg_ref[...], sn_ref[...])
        dgg, dq, dsn = tail_vjp(_nn_f32(order, dya_ref[...]))
        (dv,) = gelu_vjp(dgg + _nt(dq, wg_ref[...]))
        gg_ref[...] = gg.astype(BF16)
        dq_ref[...] = dq.astype(BF16)
        gd_ref[...] += jnp.sum(dv * u_t, axis=0, keepdims=True)
        gbg_ref[...] += jnp.sum(dq, axis=0, keepdims=True)
        gsn_ref[...] += dsn
        dvb = dv.astype(BF16)
        for k in range(S5_SLICES):
            dvk = dvb[:, k * LANE:(k + 1) * LANE]
            _put_states(gr_ref, k, _nt(dvk, bcr_ref[k]))
            _put_states(gi_ref, k, -_nt(dvk, bci_ref[k]))
            gbcr_ref[k] += _tn(_get_states(xr_ref, k), dvk)
            gbci_ref[k] -= _tn(_get_states(xi_ref, k), dvk)
        _s5_scan(gr_ref, gi_ref, ar_ref[...], -ai_ref[...], car_r, car_i, ini_r, ini_i, reverse=True,
                 xr_ref=xr_ref, xi_ref=xi_ref, acc_r=gar_ref, acc_i=gai_ref)
        ub = u_t.astype(BF16)
        dus = []
        for k in range(S5_SLICES):
            gk_r, gk_i = _get_states(gr_ref, k).astype(BF16), _get_states(gi_ref, k).astype(BF16)
            bk = bdb_ref[k]
            dus.append(_nt(gk_r, bk[:, :sw]) + _nt(gk_i, bk[:, sw:]))
            uk = ub[:, k * LANE:(k + 1) * LANE]
            gbdb_ref[k, :, :sw] += _tn(uk, gk_r)
            gbdb_ref[k, :, sw:] += _tn(uk, gk_i)
        du_ref[...] = _tn_f32(order, jnp.concatenate(dus, axis=1) + d_ref[...] * dv).astype(BF16)

    rev = functools.partial(_rows, n_tiles=nt)
    return _call(
        body, name, (nt,),
        [rev(tc, D_MODEL), rev(tc, D_MODEL), rev(tc, D_MODEL), _state_rows(tc, nt), _state_rows(tc, nt),
         _const((1, S5_LANES)), _const((1, S5_LANES)), _const(bdb.shape), _const(bcr.shape), _const(bci.shape),
         _const((1, D_MODEL)), _const(wglu.shape), _const((1, D_MODEL)), _const((1, D_MODEL)), HBM],
        [rev(tc, D_MODEL), rev(tc, D_MODEL), rev(tc, D_MODEL), _full(bdb.shape), _full(bcr.shape), _full(bci.shape),
         _full((SUBLANE, S5_LANES)), _full((SUBLANE, S5_LANES)), _full((1, D_MODEL)), _full((1, D_MODEL)), _full((1, D_MODEL))],
        [_sds((n_tok, D_MODEL), BF16), _sds((n_tok, D_MODEL), BF16), _sds((n_tok, D_MODEL), BF16), _sds(bdb.shape), _sds(bcr.shape),
         _sds(bci.shape), _sds((SUBLANE, S5_LANES)), _sds((SUBLANE, S5_LANES)), _sds((1, D_MODEL)), _sds((1, D_MODEL)),
         _sds((1, D_MODEL))],
        scratch=[pltpu.VMEM((S5_STATE_TILES, tc, LANE), F32), pltpu.VMEM((S5_STATE_TILES, tc, LANE), F32),
                 pltpu.VMEM((1, S5_LANES), F32), pltpu.VMEM((1, S5_LANES), F32),
                 pltpu.VMEM((SUBLANE, S5_LANES), F32), pltpu.VMEM((SUBLANE, S5_LANES), F32)],
    )(dya, v, u, xr, xi, a_r, a_i, bdb, bcr, bci, dsk, wglu, bglu, sn, after)


SSD_WIDTH = SSD_HEADS * SSD_HEAD_DIM
SSD_GROUPS = 2
HEADS_PER_GROUP = SSD_HEADS // SSD_GROUPS


def _take(x, axis, start, size):
    n = x.shape[axis]

    def sl(v):
        return lax.slice_in_dim(v, start, start + size, axis=axis)

    @jax.custom_vjp
    def f(v):
        return sl(v)

    def bwd(_, g):
        parts = []
        if start:
            parts.append(jnp.zeros(g.shape[:axis] + (start,) + g.shape[axis + 1:], g.dtype))
        parts.append(g)
        if n - start - size:
            parts.append(jnp.zeros(g.shape[:axis] + (n - start - size,) + g.shape[axis + 1:], g.dtype))
        return (jnp.concatenate(parts, axis=axis) if len(parts) > 1 else g,)

    f.defvjp(lambda v: (sl(v), None), bwd)
    return f(x)


def _lane_of(x, h):
    col = lax.broadcasted_iota(jnp.int32, x.shape, 1)
    return jnp.sum(jnp.where(col == h, x, 0.0), axis=1, keepdims=True)


def _ssd_chunk(xc, z, dt, dtb, alog, dvec, gn, st, nn, nt, tn, cumsum, take):
    t_len = xc.shape[0]
    xa = _silu(xc)
    dtp = _softplus(dt + dtb)
    d_a = dtp * (-jnp.exp(alog))
    row = lax.broadcasted_iota(jnp.int32, (t_len, t_len), 0)
    col = lax.broadcasted_iota(jnp.int32, (t_len, t_len), 1)
    causal = row >= col
    cum = cumsum(causal.astype(F32), d_a)
    eye = (row == col).astype(F32)
    group_width = HEADS_PER_GROUP * SSD_HEAD_DIM
    ys, sts = [], []
    for g in range(SSD_GROUPS):
        bg = take(xa, 1, SSD_WIDTH + g * SSD_STATE, SSD_STATE)
        cg = take(xa, 1, SSD_WIDTH + (SSD_GROUPS + g) * SSD_STATE, SSD_STATE)
        cb = nt(cg, bg)
        x_g = take(xa, 1, g * group_width, group_width)
        st_g = take(st, 0, g * group_width, group_width)
        diag, dt_l, d_l, grow_l, keep_l, last_r = [], [], [], [], [], []
        for r in range(HEADS_PER_GROUP):
            h = g * HEADS_PER_GROUP + r
            cc = _lane_of(cum, h)
            cr = jnp.sum(cc * eye, axis=0, keepdims=True)
            decay = jnp.exp(jnp.where(causal, cc - cr, -1e30))
            dt_h = _lane_of(dtp, h)
            c_last = jnp.sum(jnp.where(row[:, :1] == t_len - 1, cc, 0.0), axis=0, keepdims=True)
            lanes = (t_len, SSD_HEAD_DIM)
            diag.append(nn(cb * decay, take(x_g, 1, r * SSD_HEAD_DIM, SSD_HEAD_DIM) * dt_h))
            dt_l.append(jnp.broadcast_to(dt_h, lanes))
            d_l.append(jnp.broadcast_to(_lane_of(dvec, h), lanes))
            grow_l.append(jnp.broadcast_to(jnp.exp(cc), lanes))
            keep_l.append(jnp.broadcast_to(jnp.exp(c_last - cc), lanes))
            last_r.append(jnp.broadcast_to(jnp.exp(c_last), (SSD_HEAD_DIM, SSD_STATE)))
        side = functools.partial(jnp.concatenate, axis=1)
        xdt_g = x_g * side(dt_l)
        ys.append(side(diag) + side(grow_l) * nt(cg, st_g) + side(d_l) * x_g)
        sts.append(jnp.concatenate(last_r, axis=0) * st_g + tn(xdt_g * side(keep_l), bg))
    y = jnp.concatenate(ys, axis=1) * _silu(z)
    return _rms(y, gn), jnp.concatenate(sts, axis=0)


def _shift_back(cur, prev, j):
    if j == 0:
        return cur
    row = lax.broadcasted_iota(jnp.int32, cur.shape, 0)
    return jnp.where(row < j, pltpu.roll(prev, j, 0), pltpu.roll(cur, j, 0))


def _shift_ahead(cur, nxt, j):
    if j == 0:
        return cur
    n = cur.shape[0]
    row = lax.broadcasted_iota(jnp.int32, cur.shape, 0)
    return jnp.where(row >= n - j, pltpu.roll(nxt, n - j, 0), pltpu.roll(cur, n - j, 0))


def _conv(cur, prev, w, b):
    out = b + w[SSD_CONV - 1:SSD_CONV, :] * cur
    for k in range(SSD_CONV - 1):
        out = out + w[k:k + 1, :] * _shift_back(cur, prev, SSD_CONV - 1 - k)
    return out


def _ssd_fwd(xbc, z, dt, conv_w, conv_b, dtb, alog, dvec, gn, name):
    n_tok = xbc.shape[0]
    tc = SSD_CHUNK
    nc = n_tok // tc
    st_rows = SSD_HEADS * SSD_HEAD_DIM

    def body(cur_ref, prev_ref, z_ref, dt_ref, w_ref, b_ref, dtb_ref, alog_ref, dvec_ref, gn_ref,
             yb_ref, stin_ref, st_ref):
        i = pl.program_id(0)

        @pl.when(i == 0)
        def _():
            st_ref[...] = jnp.zeros_like(st_ref)

        prev = jnp.where(i > 0, prev_ref[...], 0.0)
        xc = _conv(cur_ref[...], prev, w_ref[...], b_ref[...])
        st = st_ref[...]
        stin_ref[0] = st
        yb, st_new = _ssd_chunk(xc, z_ref[...], dt_ref[...], dtb_ref[...], alog_ref[...], dvec_ref[...], gn_ref[...], st,
                                _nn, _nt, _tn, _nn_f32, lambda v, axis, start, size: lax.slice_in_dim(v, start, start + size, axis=axis))
        yb_ref[...] = yb.astype(BF16)
        st_ref[...] = st_new

    return _call(
        body, name, (nc,),
        [_rows(tc, SSD_CONV_DIM), pl.BlockSpec((tc, SSD_CONV_DIM), lambda i: (jnp.maximum(i - 1, 0), 0)),
         _rows(tc, D_MODEL), _rows(tc, LANE), _const((SSD_CONV, SSD_CONV_DIM)), _const((1, SSD_CONV_DIM)),
         _const((1, LANE)), _const((1, LANE)), _const((1, LANE)), _const((1, D_MODEL))],
        [_rows(tc, D_MODEL), pl.BlockSpec((1, st_rows, SSD_STATE), lambda i: (i, 0, 0))],
        [_sds((n_tok, D_MODEL), BF16), _sds((nc, st_rows, SSD_STATE))],
        scratch=[pltpu.VMEM((st_rows, SSD_STATE), F32)],
    )(xbc, xbc, z, dt, conv_w, conv_b, dtb, alog, dvec, gn)


def _ssd_bwd(dyb, xbc, z, dt, stin, conv_w, conv_b, dtb, alog, dvec, gn, name):
    n_tok = xbc.shape[0]
    tc = SSD_CHUNK
    nc = n_tok // tc
    st_rows = SSD_HEADS * SSD_HEAD_DIM

    def body(dyb_ref, cur_ref, prev_ref, z_ref, dt_ref, stin_ref, w_ref, b_ref, dtb_ref, alog_ref, dvec_ref, gn_ref,
             dxbc_ref, dz_ref, ddt_ref, gw_ref, gb_ref, gdtb_ref, galog_ref, gdvec_ref, ggn_ref,
             dst_ref, dxc_next_ref):
        i = pl.program_id(0)

        @pl.when(i == 0)
        def _():
            for r in (dst_ref, dxc_next_ref, gw_ref, gb_ref, gdtb_ref, galog_ref, gdvec_ref, ggn_ref):
                r[...] = jnp.zeros_like(r)

        cur = cur_ref[...]
        prev = jnp.where(i < nc - 1, prev_ref[...], 0.0)
        w = w_ref[...]
        xc = _conv(cur, prev, w, b_ref[...])
        chunk = functools.partial(_ssd_chunk, nn=_nn_d, nt=_nt_d, tn=_tn_d, cumsum=_cumsum_rows, take=_take)
        _, vjp = jax.vjp(chunk, xc, z_ref[...], dt_ref[...], dtb_ref[...], alog_ref[...], dvec_ref[...], gn_ref[...],
                         stin_ref[0])
        dxc, dz, ddt, gdtb, galog, gdvec, ggn, dst = vjp((dyb_ref[...], dst_ref[...]))
        dst_ref[...] = dst
        dz_ref[...] = dz.astype(BF16)
        ddt_ref[...] = ddt.astype(BF16)
        gdtb_ref[...] += gdtb
        galog_ref[...] += galog
        gdvec_ref[...] += gdvec
        ggn_ref[...] += ggn
        dxc_next = dxc_next_ref[...]
        dxbc = w[SSD_CONV - 1:SSD_CONV, :] * dxc
        gws = []
        for k in range(SSD_CONV - 1):
            j = SSD_CONV - 1 - k
            dxbc = dxbc + w[k:k + 1, :] * _shift_ahead(dxc, dxc_next, j)
            gws.append(jnp.sum(dxc * _shift_back(cur, prev, j), axis=0, keepdims=True))
        gws.append(jnp.sum(dxc * cur, axis=0, keepdims=True))
        dxbc_ref[...] = dxbc.astype(BF16)
        gw_ref[...] += jnp.concatenate(gws, axis=0)
        gb_ref[...] += jnp.sum(dxc, axis=0, keepdims=True)
        dxc_next_ref[...] = dxc

    rev = functools.partial(_rows, n_tiles=nc)
    return _call(
        body, name, (nc,),
        [rev(tc, D_MODEL), rev(tc, SSD_CONV_DIM),
         pl.BlockSpec((tc, SSD_CONV_DIM), lambda i: (jnp.maximum(nc - 2 - i, 0), 0)),
         rev(tc, D_MODEL), rev(tc, LANE), pl.BlockSpec((1, st_rows, SSD_STATE), lambda i: (nc - 1 - i, 0, 0)),
         _const((SSD_CONV, SSD_CONV_DIM)), _const((1, SSD_CONV_DIM)), _const((1, LANE)), _const((1, LANE)),
         _const((1, LANE)), _const((1, D_MODEL))],
        [rev(tc, SSD_CONV_DIM), rev(tc, D_MODEL), rev(tc, LANE), _full((SSD_CONV, SSD_CONV_DIM)), _full((1, SSD_CONV_DIM)),
         _full((1, LANE)), _full((1, LANE)), _full((1, LANE)), _full((1, D_MODEL))],
        [_sds((n_tok, SSD_CONV_DIM), BF16), _sds((n_tok, D_MODEL), BF16), _sds((n_tok, LANE), BF16), _sds((SSD_CONV, SSD_CONV_DIM)),
         _sds((1, SSD_CONV_DIM)), _sds((1, LANE)), _sds((1, LANE)), _sds((1, LANE)), _sds((1, D_MODEL))],
        scratch=[pltpu.VMEM((st_rows, SSD_STATE), F32), pltpu.VMEM((tc, SSD_CONV_DIM), F32)],
    )(dyb, xbc, xbc, z, dt, stin, conv_w, conv_b, dtb, alog, dvec, gn)


@jax.custom_vjp
def _expand_cols(x, e):
    return _nn_f32(x, e)


_expand_cols.defvjp(
    lambda x, e: (_nn_f32(x, e), e),
    lambda e, g: (lax.dot_general(g, e, (((1,), (1,)), ((), ())), precision=lax.Precision.HIGHEST,
                                  preferred_element_type=F32), jnp.zeros_like(e)))


def _s5_discretize(lam_re, lam_im, log_step, b_re, b_im, expand):
    step = jnp.exp(log_step)
    mag = jnp.exp(lam_re * step)
    ang = lam_im * step
    a_r = mag * jnp.cos(ang)
    a_i = mag * jnp.sin(ang)
    den = lam_re * lam_re + lam_im * lam_im
    n_r = a_r - 1.0
    coef_r = _expand_cols((n_r * lam_re + a_i * lam_im) / den, expand)
    coef_i = _expand_cols((a_i * lam_re - n_r * lam_im) / den, expand)
    return a_r, a_i, coef_r * b_re - coef_i * b_im, coef_r * b_im + coef_i * b_re


def _expand_matrix():
    p = lax.broadcasted_iota(jnp.int32, (S5_STATE, S5_STATE * S5_GROUP), 0)
    c = lax.broadcasted_iota(jnp.int32, (S5_STATE, S5_STATE * S5_GROUP), 1)
    return (c // S5_GROUP == p).astype(F32)


def _s5_discretize_fwd(lam_re, lam_im, log_step, b_re, b_im, name):
    def body(lr_ref, li_ref, ls_ref, br_ref, bi_ref, ar_ref, ai_ref, bbr_ref, bbi_ref):
        outs = _s5_discretize(lr_ref[...], li_ref[...], ls_ref[...], br_ref[...], bi_ref[...], _expand_matrix())
        for r, o in zip((ar_ref, ai_ref, bbr_ref, bbi_ref), outs):
            r[...] = o

    sq, wide = (S5_GROUPS, S5_STATE), (S5_GROUPS, S5_STATE * S5_GROUP)
    return _call(body, name, (1,), [_full(sq), _full(sq), _full((S5_GROUPS, 1)), _full(wide), _full(wide)],
                 [_full(sq), _full(sq), _full(wide), _full(wide)], [_sds(sq), _sds(sq), _sds(wide), _sds(wide)],
                 )(lam_re, lam_im, log_step, b_re, b_im)


def _s5_discretize_bwd(lam_re, lam_im, log_step, b_re, b_im, g_ar8, g_ai8, g_bbr, g_bbi, name):
    def body(lr_ref, li_ref, ls_ref, br_ref, bi_ref, gar_ref, gai_ref, gbbr_ref, gbbi_ref,
             glr_ref, gli_ref, gls_ref, gbr_ref, gbi_ref):
        _, vjp = jax.vjp(functools.partial(_s5_discretize, expand=_expand_matrix()),
                         lr_ref[...], li_ref[...], ls_ref[...], br_ref[...], bi_ref[...])
        grads = vjp((jnp.sum(gar_ref[...], axis=0), jnp.sum(gai_ref[...], axis=0), gbbr_ref[...], gbbi_ref[...]))
        for r, g in zip((glr_ref, gli_ref, gls_ref, gbr_ref, gbi_ref), grads):
            r[...] = g

    sq, wide, col = (S5_GROUPS, S5_STATE), (S5_GROUPS, S5_STATE * S5_GROUP), (S5_GROUPS, 1)
    part = (SUBLANE,) + sq
    return _call(body, name, (1,),
                 [_full(sq), _full(sq), _full(col), _full(wide), _full(wide), _full(part), _full(part), _full(wide), _full(wide)],
                 [_full(sq), _full(sq), _full(col), _full(wide), _full(wide)],
                 [_sds(sq), _sds(sq), _sds(col), _sds(wide), _sds(wide)],
                 )(lam_re, lam_im, log_step, b_re, b_im, g_ar8, g_ai8, g_bbr, g_bbi)


GROUPS_PER_SLICE = LANE // S5_GROUP


def _block_diag_b(bb):
    t = bb.reshape(S5_SLICES, GROUPS_PER_SLICE, S5_STATE, S5_GROUP)
    eye = jnp.eye(GROUPS_PER_SLICE, dtype=bb.dtype)
    return jnp.einsum("kgph,gf->kghfp", t, eye).reshape(S5_SLICES, LANE, S5_SLICE_STATES)


def _block_diag_b_inv(m):
    t = m.reshape(S5_SLICES, GROUPS_PER_SLICE, S5_GROUP, GROUPS_PER_SLICE, S5_STATE)
    return jnp.einsum("kghgp->kgph", t).reshape(S5_GROUPS, S5_STATE * S5_GROUP)


def _block_diag_c(c):
    t = c.reshape(S5_SLICES, GROUPS_PER_SLICE, S5_GROUP, S5_STATE)
    eye = jnp.eye(GROUPS_PER_SLICE, dtype=c.dtype)
    return jnp.einsum("kghp,gf->kgpfh", t, eye).reshape(S5_SLICES, S5_SLICE_STATES, LANE)


def _block_diag_c_inv(m):
    t = m.reshape(S5_SLICES, GROUPS_PER_SLICE, S5_STATE, GROUPS_PER_SLICE, S5_GROUP)
    return jnp.einsum("kgpgh->kghp", t).reshape(S5_GROUPS, S5_GROUP, S5_STATE)


def _pad_lanes(v):
    return jnp.pad(v.reshape(1, -1), ((0, 0), (0, LANE - v.shape[0])))


def _prepare_layer(w, blk, i, after):
    p = {}
    p["wu"], p["wz"], p["wx"], p["wd"] = _w_in_split(blk["w_in"], after, name=f"w_in_split_{i}")
    p["nm"] = w["norm_mix"][i].reshape(1, D_MODEL)
    p["lam_re"], p["lam_im"] = w["s5_lam_re"][i], w["s5_lam_im"][i]
    p["log_step"] = w["s5_log_step"][i].reshape(S5_GROUPS, 1)
    p["b_re"] = w["s5_b_re"][i].reshape(S5_GROUPS, S5_STATE * S5_GROUP)
    p["b_im"] = w["s5_b_im"][i].reshape(S5_GROUPS, S5_STATE * S5_GROUP)
    a_r, a_i, bb_r, bb_i = _s5_discretize_fwd(p["lam_re"], p["lam_im"], p["log_step"], p["b_re"], p["b_im"],
                                              name=f"s5_discretize_{i}")
    p["a_r"], p["a_i"] = a_r.reshape(1, S5_LANES), a_i.reshape(1, S5_LANES)
    p["bdb"] = jnp.concatenate([_block_diag_b(bb_r), _block_diag_b(bb_i)], axis=2).astype(BF16)
    p["bcr"] = _block_diag_c(w["s5_c_re"][i]).astype(BF16)
    p["bci"] = _block_diag_c(w["s5_c_im"][i]).astype(BF16)
    p["dsk"] = w["s5_d"][i].reshape(1, D_MODEL)
    p["wglu"] = blk["s5_w_glu"].reshape(D_MODEL, D_MODEL)
    p["bglu"] = w["s5_b_glu"][i].reshape(1, D_MODEL)
    p["sn"] = w["s5_norm"][i].reshape(1, D_MODEL)
    p["conv_w"] = blk["ssd_conv_w"]
    p["conv_b"] = w["ssd_conv_b"][i].reshape(1, SSD_CONV_DIM)
    p["dtb"] = _pad_lanes(w["ssd_dt_bias"][i])
    p["alog"] = _pad_lanes(w["ssd_a_log"][i])
    p["dvec"] = _pad_lanes(w["ssd_d"][i])
    p["gn"] = w["ssd_norm"][i].reshape(1, D_MODEL)
    p["wo"] = blk["w_out"].reshape(2 * D_MODEL, D_MODEL)
    p["nf"] = w["norm_ffn"][i].reshape(1, D_MODEL)
    p["wg"], p["wup"], p["wdn"] = (blk[n].reshape(FFN_PAD, D_MODEL) for n in ("w_gate", "w_up", "w_down"))
    return p


def _layer_fwd(x0, p, i):
    u, z, xbc, dt = _inproj_fwd(x0, p["nm"], p["wu"], p["wz"], p["wx"], p["wd"], name=f"inproj_fwd_{i}")
    ya, xr, xi, v = _s5_fwd(u, p["a_r"], p["a_i"], p["bdb"], p["bcr"], p["bci"], p["dsk"], p["wglu"], p["bglu"], p["sn"],
                            name=f"s5_fwd_{i}")
    yb, stin = _ssd_fwd(xbc, z, dt, p["conv_w"], p["conv_b"], p["dtb"], p["alog"], p["dvec"], p["gn"], name=f"ssd_fwd_{i}")
    x1, x2 = _mix_ffn_fwd(x0, ya, yb, p["wo"], p["nf"], p["wg"], p["wup"], p["wdn"], name=f"mix_ffn_fwd_{i}")
    return x2, dict(x0=x0, u=u, z=z, xbc=xbc, dt=dt, xr=xr, xi=xi, v=v, stin=stin, ya=ya, yb=yb, x1=x1)


def _layer_bwd(dx2, s, p, i, after, between=None):
    g = {}
    dx1, dya, dyb, h2, act, dgt, dup, dx2b, dx1b, g_nf = _mix_ffn_bwd(
        s["x1"], dx2, p["wo"], p["nf"], p["wg"], p["wup"], p["wdn"], after, name=f"mix_ffn_bwd_{i}")
    g["norm_ffn"] = g_nf.reshape(D_MODEL)
    g["w_down"] = _matmul_tn_lhs_blocks(act, dx2b, FFN_BLOCK_PAD, FFN_BLOCK, name=f"grad_w_down_{i}")
    g["w_gate"] = _matmul_tn_lhs_blocks(dgt, h2, FFN_BLOCK_PAD, FFN_BLOCK, name=f"grad_w_gate_{i}")
    g["w_up"] = _matmul_tn_lhs_blocks(dup, h2, FFN_BLOCK_PAD, FFN_BLOCK, name=f"grad_w_up_{i}")
    g["w_out"] = _matmul_tn_pair(s["ya"], s["yb"], dx1b, name=f"grad_w_out_{i}").reshape(N_DEV, 2 * D_MODEL // N_DEV, D_MODEL)
    if between is not None:
        after = between(g)

    (du, gg, dq, g_bdb, g_bcr, g_bci, g_ar8, g_ai8, g_d, g_bglu, g_sn) = _s5_bwd(
        dya, s["v"], s["u"], s["xr"], s["xi"], p["a_r"], p["a_i"], p["bdb"], p["bcr"], p["bci"], p["dsk"], p["wglu"],
        p["bglu"], p["sn"], after, name=f"s5_bwd_{i}")
    g["s5_w_glu"] = _matmul_tn(gg, dq, name=f"grad_w_glu_{i}").reshape(N_DEV, D_MODEL // N_DEV, D_MODEL)
    g["s5_d"], g["s5_b_glu"], g["s5_norm"] = g_d.reshape(D_MODEL), g_bglu.reshape(D_MODEL), g_sn.reshape(D_MODEL)
    g["s5_c_re"], g["s5_c_im"] = _block_diag_c_inv(g_bcr), _block_diag_c_inv(g_bci)
    sq = (SUBLANE, S5_GROUPS, S5_STATE)
    g_lr, g_li, g_ls, g_br, g_bi = _s5_discretize_bwd(
        p["lam_re"], p["lam_im"], p["log_step"], p["b_re"], p["b_im"], g_ar8.reshape(sq), g_ai8.reshape(sq),
        _block_diag_b_inv(g_bdb[:, :, :S5_SLICE_STATES]), _block_diag_b_inv(g_bdb[:, :, S5_SLICE_STATES:]),
        name=f"s5_discretize_bwd_{i}")
    g["s5_lam_re"], g["s5_lam_im"], g["s5_log_step"] = g_lr, g_li, g_ls.reshape(S5_GROUPS)
    b_shape = (S5_GROUPS, S5_STATE, S5_GROUP)
    g["s5_b_re"], g["s5_b_im"] = g_br.reshape(b_shape), g_bi.reshape(b_shape)

    dxbc, dz, ddt, g_cw, g_cb, g_dtb, g_alog, g_dvec, g_gn = _ssd_bwd(
        dyb, s["xbc"], s["z"], s["dt"], s["stin"], p["conv_w"], p["conv_b"], p["dtb"], p["alog"], p["dvec"], p["gn"],
        name=f"ssd_bwd_{i}")
    g["ssd_conv_w"] = jnp.moveaxis(g_cw.reshape(SSD_CONV, N_DEV, SSD_CONV_DIM // N_DEV), 1, 0)
    g["ssd_conv_b"] = g_cb.reshape(SSD_CONV_DIM)
    g["ssd_dt_bias"], g["ssd_a_log"], g["ssd_d"] = g_dtb[0, :SSD_HEADS], g_alog[0, :SSD_HEADS], g_dvec[0, :SSD_HEADS]
    g["ssd_norm"] = g_gn.reshape(D_MODEL)

    dx0, h, g_nm = _inproj_bwd(s["x0"], p["nm"], du, dz, dxbc, ddt, dx1, p["wu"], p["wz"], p["wx"], p["wd"],
                               name=f"inproj_bwd_{i}")
    g["norm_mix"] = g_nm.reshape(D_MODEL)
    g["w_in"] = _w_in_grad_blocks(
        _matmul_tn(h, du, name=f"grad_w_in_u_{i}"), _matmul_tn(h, dz, name=f"grad_w_in_z_{i}"),
        _matmul_tn(h, dxbc, name=f"grad_w_in_xbc_{i}"), _matmul_tn(h, ddt, name=f"grad_w_in_dt_{i}"),
        name=f"grad_w_in_blocks_{i}")
    return dx0, g


def _example_step(x, target, w, blks):
    prepared = [_prepare_layer(w, blks[i], i, x) for i in range(DEPTH)]
    saved = []
    h = x
    for i in range(DEPTH):
        h, s = _layer_fwd(h, prepared[i], i)
        saved.append(s)
    loss, dh, g_final = _loss_head(h, w["norm_final"].reshape(1, D_MODEL), target, name="loss_head")
    layer_grads = [None] * DEPTH
    for i in reversed(range(DEPTH)):
        dh, layer_grads[i] = _layer_bwd(dh, saved[i], prepared[i], i, x)
    return loss, dh, layer_grads, g_final.reshape(D_MODEL)


def _mesh_position():
    return lax.axis_index("x"), lax.axis_index("y"), lax.axis_index("c")


def _peer(pos, k):
    x, y, c = pos
    px = 1 - x if k & 4 else x
    py = 1 - y if k & 2 else y
    pc = 1 - c if k & 1 else c
    return (px, py, pc), 4 * px + 2 * py + pc


HBM = pl.BlockSpec(memory_space=pl.ANY)


def _run_copies(local, remote):
    for cp in local + remote:
        cp.start()
    for cp in remote:
        cp.wait_recv()
    for cp in remote:
        cp.wait_send()
    for cp in local:
        cp.wait()


def _comm_scratch(n_units):
    return [pltpu.SemaphoreType.DMA((n_units, N_DEV - 1)), pltpu.SemaphoreType.DMA((n_units, N_DEV - 1)),
            pltpu.SemaphoreType.DMA((n_units,))]


def _gather_blocks(arrays, layered, name):
    units, out_shapes = [], []
    for j, (a, lay) in enumerate(zip(arrays, layered)):
        for layer in (range(a.shape[0]) if lay else (None,)):
            units.append((j, layer, len(out_shapes)))
            out_shapes.append(_sds((N_DEV,) + (a.shape[1:] if lay else a.shape), a.dtype))
    n_in = len(arrays)
    other_chips = (4, 2, 6)

    def body(*refs):
        ins, outs = refs[:n_in], refs[n_in:n_in + len(out_shapes)]
        send_sems, recv_sems, local_sems = refs[n_in + len(out_shapes):]
        pos = _mesh_position()
        me = 4 * pos[0] + 2 * pos[1] + pos[2]
        sibling, _ = _peer(pos, 1)
        local, own, passed = [], [], []
        for u, (j, layer, o) in enumerate(units):
            src = ins[j] if layer is None else ins[j].at[layer]
            local.append(pltpu.make_async_copy(src, outs[o].at[me], local_sems.at[u]))

            def copy(sem, src_ref, slot, to, u=u, o=o):
                return pltpu.make_async_remote_copy(
                    src_ref=src_ref, dst_ref=outs[o].at[slot], send_sem=send_sems.at[u, sem], recv_sem=recv_sems.at[u, sem],
                    device_id=to, device_id_type=MESH_ID)

            own.append([copy(0, src, me, sibling)] + [copy(1 + i, src, me, _peer(pos, k)[0]) for i, k in enumerate(other_chips)])
            passed.append([copy(4 + i, outs[o].at[_peer(pos, k)[1]], _peer(pos, k)[1], sibling) for i, k in enumerate(other_chips)])
        for cp in local + [c for unit in own for c in unit]:
            cp.start()
        for u in range(len(units)):
            for i in range(len(other_chips)):
                own[u][1 + i].wait_recv()
                passed[u][i].start()
        for u in range(len(units)):
            own[u][0].wait_recv()
            for cp in passed[u]:
                cp.wait_recv()
        for cp in [c for unit in own + passed for c in unit]:
            cp.wait_send()
        for cp in local:
            cp.wait()

    outs = pl.pallas_call(body, name=name, in_specs=[HBM] * n_in, out_specs=[HBM] * len(out_shapes), out_shape=out_shapes,
                          scratch_shapes=_comm_scratch(len(units)))(*arrays)
    grouped = [[] for _ in arrays]
    for j, _, o in units:
        grouped[j].append(outs[o])
    return [tuple(g) for g in grouped]


def _exchange_blocks(entries, name):
    units, flat_in, out_shapes = [], [], []
    for j, entry in enumerate(entries):
        for layer, a in enumerate(entry):
            units.append((len(flat_in), layer, j))
            flat_in.append(a)
        out_shapes.append(_sds((N_DEV, len(entry)) + entry[0].shape[1:], entry[0].dtype))
    n_in = len(flat_in)

    def body(*refs):
        ins, outs = refs[:n_in], refs[n_in:n_in + len(out_shapes)]
        send_sems, recv_sems, local_sems = refs[n_in + len(out_shapes):]
        pos = _mesh_position()
        me = 4 * pos[0] + 2 * pos[1] + pos[2]
        local, remote = [], []
        for u, (i, layer, o) in enumerate(units):
            local.append(pltpu.make_async_copy(ins[i].at[me], outs[o].at[me, layer], local_sems.at[u]))
            for k in range(1, N_DEV):
                peer, peer_index = _peer(pos, k)
                remote.append(pltpu.make_async_remote_copy(
                    src_ref=ins[i].at[peer_index], dst_ref=outs[o].at[me, layer], send_sem=send_sems.at[u, k - 1],
                    recv_sem=recv_sems.at[u, k - 1], device_id=peer, device_id_type=MESH_ID))
        _run_copies(local, remote)

    return pl.pallas_call(body, name=name, in_specs=[HBM] * n_in, out_specs=[HBM] * len(out_shapes), out_shape=out_shapes,
                          scratch_shapes=_comm_scratch(len(units)))(*flat_in)


SEM = pl.BlockSpec(memory_space=pltpu.SEMAPHORE)
SIDE_EFFECT = pltpu.SideEffectType.DATAFLOW_SIDE_EFFECTING


def _own_slots(arrays, indexed, me, name):
    lands = []
    for u, a in enumerate(arrays):
        block = a.shape[1:] if indexed else a.shape
        rows, cols = _size(block[:-1]), block[-1]
        tr = _row_tile(rows, cap=512)

        def body(me_ref, src_ref, out_ref):
            out_ref[...] = src_ref[...]

        src_spec = (pl.BlockSpec((None, tr, cols), lambda i, me_ref: (me_ref[0], i, 0)) if indexed
                    else pl.BlockSpec((tr, cols), lambda i, me_ref: (i, 0)))
        land = pl.pallas_call(
            body, name=f"{name}_{u}", out_shape=_sds((N_DEV, rows, cols), a.dtype),
            grid_spec=pltpu.PrefetchScalarGridSpec(
                num_scalar_prefetch=1, grid=(rows // tr,), in_specs=[src_spec],
                out_specs=pl.BlockSpec((None, tr, cols), lambda i, me_ref: (me_ref[0], i, 0))),
        )(me, a.reshape((N_DEV, rows, cols) if indexed else (rows, cols)))
        lands.append(land.reshape((N_DEV,) + block))
    return lands


def _split_copies(srcs, lands, send_sems, recv_sems, indexed):
    pos = _mesh_position()
    me = 4 * pos[0] + 2 * pos[1] + pos[2]
    copies = []
    for u, (src, land) in enumerate(zip(srcs, lands)):
        for k in range(1, N_DEV):
            peer, peer_index = _peer(pos, k)
            copies.append(pltpu.make_async_remote_copy(
                src_ref=src.at[peer_index] if indexed else src, dst_ref=land.at[me],
                send_sem=send_sems.at[u * (N_DEV - 1) + k - 1], recv_sem=recv_sems.at[u * (N_DEV - 1) + k - 1],
                device_id=peer, device_id_type=MESH_ID))
    return copies


def _exchange_start(arrays, lands, indexed, name):
    n = len(arrays)

    def body(*refs):
        srcs, zones = refs[:n], refs[n:2 * n]
        send_sems, recv_sems = refs[2 * n], refs[2 * n + 1]
        token = refs[-1]
        for cp in _split_copies(srcs, zones, send_sems, recv_sems, indexed):
            cp.start()
        token[...] = jnp.zeros_like(token)

    sem_shape = pltpu.SemaphoreType.DMA((n * (N_DEV - 1),))
    outs = pl.pallas_call(
        body, name=name, in_specs=[HBM] * (2 * n),
        out_specs=[SEM, SEM] + [HBM] * (2 * n) + [pl.BlockSpec(memory_space=pltpu.VMEM)],
        out_shape=[sem_shape, sem_shape] + [pltpu.HBM(a.shape, a.dtype) for a in list(arrays) + list(lands)]
        + [_sds((SUBLANE, LANE))],
        input_output_aliases={i: 2 + i for i in range(2 * n)},
        compiler_params=pltpu.CompilerParams(has_side_effects=SIDE_EFFECT),
    )(*[pltpu.with_memory_space_constraint(a, pltpu.HBM) for a in list(arrays) + list(lands)])
    return outs[0], outs[1], outs[2:2 + n], outs[2 + n:2 + 2 * n], outs[-1]


def _exchange_wait(send_sems, recv_sems, arrays, lands, after, indexed, name):
    n = len(arrays)

    def body(*refs):
        srcs, zones = refs[:n], refs[n:2 * n]
        s_sems, r_sems = refs[2 * n], refs[2 * n + 1]
        for cp in _split_copies(srcs, zones, s_sems, r_sems, indexed):
            cp.wait_send()
            cp.wait_recv()

    outs = pl.pallas_call(
        body, name=name, in_specs=[HBM] * (2 * n) + [SEM, SEM, HBM],
        out_specs=[HBM] * (2 * n), out_shape=[pltpu.HBM(a.shape, a.dtype) for a in list(arrays) + list(lands)],
        input_output_aliases={i: i for i in range(2 * n)},
        compiler_params=pltpu.CompilerParams(has_side_effects=SIDE_EFFECT),
    )(*arrays, *lands, send_sems, recv_sems, after)
    return outs[n:]


SUM_TILE = 512


def _adamw(w, g, m, v):
    m = ADAM_B1 * m + (1.0 - ADAM_B1) * g
    v = ADAM_B2 * v + (1.0 - ADAM_B2) * (g * g)
    m_hat = m / (1.0 - ADAM_B1 ** ADAM_STEP)
    v_hat = v / (1.0 - ADAM_B2 ** ADAM_STEP)
    return -ADAM_LR * (m_hat / (jnp.sqrt(v_hat) + ADAM_EPS) + ADAM_WD * w), m, v


def _sum_adamw(recv, w, m, v, layer, others, name):
    _, rows, cols = w.shape
    tr = _row_tile(rows, cap=256)

    def body(r_ref, w_ref, m_ref, v_ref, *rest):
        g_ref, d_ref, mo_ref, vo_ref = rest[-4:]
        g = r_ref[0].astype(F32)
        for j in range(1, N_DEV):
            g = g + r_ref[j].astype(F32)
        g_ref[...] = g
        d_ref[...], mo_ref[...], vo_ref[...] = _adamw(w_ref[...], g, m_ref[...], v_ref[...])

    blk = pl.BlockSpec((None, tr, cols), lambda i: (layer, i, 0))
    carried = list(others) if others is not None else []
    return pl.pallas_call(
        body, name=name, grid=(rows // tr,),
        in_specs=[pl.BlockSpec((N_DEV, tr, cols), lambda i: (0, i, 0)), blk, blk, blk] + [HBM] * len(carried),
        out_specs=[blk] * 4, out_shape=[_sds(w.shape)] * 4,
        input_output_aliases={4 + k: k for k in range(len(carried))},
        compiler_params=pltpu.CompilerParams(dimension_semantics=("arbitrary",), vmem_limit_bytes=VMEM_LIMIT),
    )(recv, w, m, v, *carried)


def _sum_senders(recv, name):
    _, rows, cols = recv.shape
    tr = _row_tile(rows, cap=256)

    def body(r_ref, g_ref):
        g = r_ref[0].astype(F32)
        for j in range(1, N_DEV):
            g = g + r_ref[j].astype(F32)
        g_ref[...] = g

    return _call(body, name, (rows // tr,), [pl.BlockSpec((N_DEV, tr, cols), lambda i: (0, i, 0))], [_rows(tr, cols)],
                 [_sds((rows, cols))])(recv)[0]


def _adamw_blocks(g, w, m, v, name):
    n_lay, rows, cols = w.shape
    tr = _row_tile(rows, cap=256)

    def body(g_ref, w_ref, m_ref, v_ref, d_ref, mo_ref, vo_ref):
        d_ref[...], mo_ref[...], vo_ref[...] = _adamw(w_ref[...], g_ref[...], m_ref[...], v_ref[...])

    blk = pl.BlockSpec((None, tr, cols), lambda l, i: (l, i, 0))
    return pl.pallas_call(
        body, name=name, grid=(n_lay, rows // tr), in_specs=[blk] * 4, out_specs=[blk] * 3, out_shape=[_sds(w.shape)] * 3,
        compiler_params=pltpu.CompilerParams(dimension_semantics=("arbitrary", "arbitrary"), vmem_limit_bytes=VMEM_LIMIT),
    )(g, w, m, v)


def _sum_slots(recv, name):
    rows = recv.shape[1]

    def body(r_ref, g_ref):
        g = r_ref[0].astype(F32)
        for j in range(1, N_DEV):
            g = g + r_ref[j].astype(F32)
        g_ref[...] = g

    return _call(body, name, (1,), [_full(recv.shape)], [_full((rows, LANE))], [_sds((rows, LANE))])(recv)[0]


def _adamw_rows(g, w, m, v, name):
    rows, cols = w.shape
    tr = _row_tile(rows)

    def body(g_ref, w_ref, m_ref, v_ref, d_ref, mo_ref, vo_ref):
        d_ref[...], mo_ref[...], vo_ref[...] = _adamw(w_ref[...], g_ref[...], m_ref[...], v_ref[...])

    blk = _rows(tr, cols)
    return _call(body, name, (rows // tr,), [blk] * 4, [blk] * 3, [_sds((rows, cols))] * 3)(g, w, m, v)


def _row_tile(rows, cap=1024):
    if rows % SUBLANE:
        return rows
    best = SUBLANE
    for t in range(SUBLANE, cap + 1, SUBLANE):
        if rows % t == 0:
            best = t
    return best


BIG = (("w_in", (DEPTH, D_MODEL, IN_PROJ // N_DEV), 2),
       ("s5_w_glu", (DEPTH, D_MODEL // N_DEV, D_MODEL), 1),
       ("ssd_conv_w", (DEPTH, SSD_CONV, SSD_CONV_DIM // N_DEV), 2),
       ("w_out", (DEPTH, 2 * D_MODEL // N_DEV, D_MODEL), 1),
       ("w_gate", (DEPTH, D_MODEL, FFN_HIDDEN // N_DEV), 2),
       ("w_up", (DEPTH, D_MODEL, FFN_HIDDEN // N_DEV), 2),
       ("w_down", (DEPTH, FFN_HIDDEN // N_DEV, D_MODEL), 1))
SMALL = (("norm_mix", (DEPTH, D_MODEL)), ("s5_lam_re", (DEPTH, S5_GROUPS, S5_STATE)), ("s5_lam_im", (DEPTH, S5_GROUPS, S5_STATE)),
         ("s5_log_step", (DEPTH, S5_GROUPS)), ("s5_b_re", (DEPTH, S5_GROUPS, S5_STATE, S5_GROUP)),
         ("s5_b_im", (DEPTH, S5_GROUPS, S5_STATE, S5_GROUP)), ("s5_c_re", (DEPTH, S5_GROUPS, S5_GROUP, S5_STATE)),
         ("s5_c_im", (DEPTH, S5_GROUPS, S5_GROUP, S5_STATE)), ("s5_d", (DEPTH, D_MODEL)), ("s5_b_glu", (DEPTH, D_MODEL)),
         ("s5_norm", (DEPTH, D_MODEL)), ("ssd_conv_b", (DEPTH, SSD_CONV_DIM)), ("ssd_dt_bias", (DEPTH, SSD_HEADS)),
         ("ssd_a_log", (DEPTH, SSD_HEADS)), ("ssd_d", (DEPTH, SSD_HEADS)), ("ssd_norm", (DEPTH, D_MODEL)),
         ("norm_ffn", (DEPTH, D_MODEL)), ("norm_final", (D_MODEL,)))
WEIGHT_ORDER = ("norm_mix", "w_in", "s5_lam_re", "s5_lam_im", "s5_log_step", "s5_b_re", "s5_b_im", "s5_c_re", "s5_c_im", "s5_d",
                "s5_w_glu", "s5_b_glu", "s5_norm", "ssd_conv_w", "ssd_conv_b", "ssd_dt_bias", "ssd_a_log", "ssd_d", "ssd_norm",
                "w_out", "norm_ffn", "w_gate", "w_up", "w_down", "norm_final")


def _size(shape):
    n = 1
    for s in shape:
        n *= s
    return n


def _round_up(n, m):
    return -(-n // m) * m


SMALL_SIZE = sum(_size(s) for _, s in SMALL)
SMALL_ROWS = _round_up(-(-SMALL_SIZE // (N_DEV * LANE)), SUBLANE)


def _pack(parts, rows, dtype):
    flat = jnp.concatenate([p.reshape(-1).astype(dtype) for p in parts])
    return jnp.pad(flat, (0, rows * LANE - flat.shape[0])).reshape(rows, LANE)


def _unpack(flat, specs):
    out, off = {}, 0
    flat = flat.reshape(-1)
    for name, shape in specs:
        out[name] = flat[off:off + _size(shape)].reshape(shape)
        off += _size(shape)
    return out


def kernel(x, norm_mix, w_in, s5_lam_re, s5_lam_im, s5_log_step, s5_b_re, s5_b_im, s5_c_re, s5_c_im, s5_d, s5_w_glu, s5_b_glu, s5_norm, ssd_conv_w, ssd_conv_b, ssd_dt_bias, ssd_a_log, ssd_d, ssd_norm, w_out, norm_ffn, w_gate, w_up, w_down, norm_final, loss_target, m_norm_mix, m_w_in, m_s5_lam_re, m_s5_lam_im, m_s5_log_step, m_s5_b_re, m_s5_b_im, m_s5_c_re, m_s5_c_im, m_s5_d, m_s5_w_glu, m_s5_b_glu, m_s5_norm, m_ssd_conv_w, m_ssd_conv_b, m_ssd_dt_bias, m_ssd_a_log, m_ssd_d, m_ssd_norm, m_w_out, m_norm_ffn, m_w_gate, m_w_up, m_w_down, m_norm_final, v_norm_mix, v_w_in, v_s5_lam_re, v_s5_lam_im, v_s5_log_step, v_s5_b_re, v_s5_b_im, v_s5_c_re, v_s5_c_im, v_s5_d, v_s5_w_glu, v_s5_b_glu, v_s5_norm, v_ssd_conv_w, v_ssd_conv_b, v_ssd_dt_bias, v_ssd_a_log, v_ssd_d, v_ssd_norm, v_w_out, v_norm_ffn, v_w_gate, v_w_up, v_w_down, v_norm_final):
    given = dict(locals())
    w = {n: given[n] for n in WEIGHT_ORDER}
    m = {n: given["m_" + n] for n in WEIGHT_ORDER}
    v = {n: given["v_" + n] for n in WEIGHT_ORDER}
    big_names = tuple(n for n, _, _ in BIG)
    matmul_names = tuple(n for n in big_names if n != "ssd_conv_w")

    conv_hi = w["ssd_conv_w"].astype(BF16)
    conv_lo = (w["ssd_conv_w"] - conv_hi.astype(F32)).astype(BF16)
    row_pad = ((0, 0), (0, FFN_BLOCK_PAD - FFN_BLOCK), (0, 0))
    as_rows = {"w_gate": jnp.swapaxes(w["w_gate"], 1, 2), "w_up": jnp.swapaxes(w["w_up"], 1, 2), "w_down": w["w_down"]}
    to_send = [jnp.pad(as_rows[n].astype(BF16), row_pad) if n in as_rows else w[n].astype(BF16) for n in matmul_names]

    def layer_blocks(i):
        return [a[i] for a in to_send] + [jnp.stack([conv_hi[i], conv_lo[i]])]

    def as_layer_weights(gathered):
        blk = dict(zip(matmul_names, gathered))
        pair = gathered[-1].astype(F32)
        blk["ssd_conv_w"] = jnp.moveaxis(pair[:, 0] + pair[:, 1], 0, 1).reshape(SSD_CONV, SSD_CONV_DIM)
        return blk

    gathered0 = [g[0] for g in _gather_blocks(layer_blocks(0), [False] * (len(matmul_names) + 1), name="gather_weights_0")]
    blocks1 = layer_blocks(1)
    me = (4 * lax.axis_index("x") + 2 * lax.axis_index("y") + lax.axis_index("c")).astype(jnp.int32).reshape(1)
    sems1 = _exchange_start(blocks1, _own_slots(blocks1, False, me, name="gather_own_1"), False, name="gather_start_1")
    prepared = [_prepare_layer(w, as_layer_weights(gathered0), 0, sems1[-1]), None]
    saved = [None, None]
    h, saved[0] = _layer_fwd(x[0], prepared[0], 0)
    gathered1 = _exchange_wait(*sems1[:4], h, False, name="gather_wait_1")
    prepared[1] = _prepare_layer(w, as_layer_weights(gathered1), 1, sems1[-1])
    h, saved[1] = _layer_fwd(h, prepared[1], 1)
    loss, dh, g_final = _loss_head(h, w["norm_final"].reshape(1, D_MODEL), loss_target[0], name="loss_head")

    layer_grads = [None, None]
    dh, layer_grads[1] = _layer_bwd(dh, saved[1], prepared[1], 1, sems1[-1])
    slots1 = [layer_grads[1][n] for n in big_names]
    sems2 = _exchange_start(slots1, _own_slots(slots1, True, me, name="exchange_own_1"), True, name="exchange_start_1")
    early_names = ("w_out", "w_gate", "w_up", "w_down")
    late_names = tuple(n for n in big_names if n not in early_names)
    early = {}

    def send_early(g):
        slots = [g[n] for n in early_names]
        early["sems"] = _exchange_start(slots, _own_slots(slots, True, me, name="exchange_own_0"), True, name="exchange_start_0")
        return early["sems"][-1]

    grad_x, layer_grads[0] = _layer_bwd(dh, saved[0], prepared[0], 0, sems2[-1], between=send_early)

    small = jnp.concatenate([g_final.reshape(-1) if n == "norm_final"
                             else jnp.stack([layer_grads[i][n] for i in range(DEPTH)]).reshape(-1) for n, _ in SMALL])
    small_slots = jnp.pad(small, (0, N_DEV * SMALL_ROWS * LANE - small.shape[0])).reshape(N_DEV, SMALL_ROWS, LANE)
    late = [layer_grads[0][n] for n in late_names] + [small_slots]
    sems3 = _exchange_start(late, _own_slots(late, True, me, name="exchange_own_late"), True, name="exchange_start_late")
    received1 = _exchange_wait(*sems2[:4], sems3[-1], True, name="exchange_wait_1")
    received_early = _exchange_wait(*early["sems"][:4], sems3[-1], True, name="exchange_wait_0")
    received0 = dict(zip(early_names, received_early))

    transposed = ("w_gate", "w_up")
    layer1 = {n: _sum_adamw(received1[j], w[n], m[n], v[n], 1, None, name=f"sum_adamw_{n}_1")
              for j, n in enumerate(big_names) if n not in transposed}
    results = {}
    for n in transposed:
        recv = (received0[n], received1[big_names.index(n)])
        g = jnp.stack([jnp.swapaxes(_sum_senders(recv[i], name=f"sum_{n}_{i}"), 0, 1) for i in range(DEPTH)])
        results[n] = [g, *_adamw_blocks(g, w[n], m[n], v[n], name=f"adamw_{n}")]
    for n in early_names:
        if n not in transposed:
            results[n] = _sum_adamw(received0[n], w[n], m[n], v[n], 0, layer1[n], name=f"sum_adamw_{n}_0")
    received_late = _exchange_wait(*sems3[:4], results["w_down"][0], True, name="exchange_wait_late")
    for n, recv in zip(late_names, received_late):
        results[n] = _sum_adamw(recv, w[n], m[n], v[n], 0, layer1[n], name=f"sum_adamw_{n}_0")
    g_part = _sum_slots(received_late[-1], name="sum_replicated")
    g_small = _gather_blocks([g_part], [False], name="gather_replicated")[0][0]
    for n, g in _unpack(g_small, SMALL).items():
        as_rows = (-1, g.shape[-1])
        d_n, m_n, v_n = _adamw_rows(g.reshape(as_rows), w[n].reshape(as_rows), m[n].reshape(as_rows), v[n].reshape(as_rows),
                                    name=f"adamw_{n}")
        results[n] = [g, d_n.reshape(g.shape), m_n.reshape(g.shape), v_n.reshape(g.shape)]

    outs = [results[n][k] for k in range(4) for n in WEIGHT_ORDER]
    total_loss = lax.psum(loss[0, 0], ("x", "y", "c"))
    return (total_loss, grad_x[None], *outs)
```

```python
import functools

import jax
import jax.numpy as jnp
from jax import lax
from jax.experimental import pallas as pl
from jax.experimental.pallas import tpu as pltpu

F32 = jnp.float32
BF16 = jnp.bfloat16
MESH_ID = pl.DeviceIdType.MESH

N_DEV = 8
DEPTH = 2
D_MODEL = 1024
S5_GROUPS = 64
S5_GROUP = 16
S5_STATE = 64
S5_LANES = S5_GROUPS * S5_STATE
SSD_HEADS = 16
SSD_HEAD_DIM = 64
SSD_STATE = 128
SSD_CHUNK = 128
SSD_CONV = 4
SSD_CONV_DIM = 1536
FFN_HIDDEN = 2816
IN_PROJ = 3600
EPS = 1e-6
LANE = 128
SUBLANE = 8
VMEM_LIMIT = 56 * 1024 * 1024

ADAM_LR = 0.001
ADAM_B1 = 0.9
ADAM_B2 = 0.999
ADAM_EPS = 1e-08
ADAM_WD = 0.01
ADAM_STEP = 10

TOK_TILE = 256
S5_TILE = 128


def _sigmoid(x):
    return jax.nn.sigmoid(x)


def _silu(x):
    return x * _sigmoid(x)


def _gelu(x):
    return 0.5 * x * (1.0 + jnp.tanh(0.7978845608028654 * (x + 0.044715 * (x * x * x))))


def _softplus(x):
    return jnp.maximum(x, 0.0) + jnp.log(1.0 + jnp.exp(-jnp.abs(x)))


def _rms(x, g):
    r = lax.rsqrt(jnp.mean(x * x, axis=-1, keepdims=True) + EPS)
    return x * r * g


def _nn(a, b):
    return lax.dot_general(a.astype(BF16), b.astype(BF16), (((1,), (0,)), ((), ())), preferred_element_type=F32)


def _nt(a, b):
    return lax.dot_general(a.astype(BF16), b.astype(BF16), (((1,), (1,)), ((), ())), preferred_element_type=F32)


def _tn(a, b):
    return lax.dot_general(a.astype(BF16), b.astype(BF16), (((0,), (0,)), ((), ())), preferred_element_type=F32)


def _nn_f32(a, b):
    return lax.dot_general(a, b, (((1,), (0,)), ((), ())), precision=lax.Precision.HIGHEST, preferred_element_type=F32)


def _tn_f32(a, b):
    return lax.dot_general(a, b, (((0,), (0,)), ((), ())), precision=lax.Precision.HIGHEST, preferred_element_type=F32)


@jax.custom_vjp
def _nn_d(a, b):
    return _nn(a, b)


_nn_d.defvjp(lambda a, b: (_nn(a, b), (a, b)), lambda r, g: (_nt(g, r[1]), _tn(r[0], g)))


@jax.custom_vjp
def _nt_d(a, b):
    return _nt(a, b)


_nt_d.defvjp(lambda a, b: (_nt(a, b), (a, b)), lambda r, g: (_nn(g, r[1]), _tn(g, r[0])))


@jax.custom_vjp
def _tn_d(a, b):
    return _tn(a, b)


_tn_d.defvjp(lambda a, b: (_tn(a, b), (a, b)), lambda r, g: (_nt(r[1], g), _nn(r[0], g)))


@jax.custom_vjp
def _cumsum_rows(tri, x):
    return _nn_f32(tri, x)


_cumsum_rows.defvjp(lambda tri, x: (_nn_f32(tri, x), tri), lambda tri, g: (jnp.zeros_like(tri), _tn_f32(tri, g)))


def _full(shape):
    zeros = (0,) * len(shape)
    return pl.BlockSpec(shape, lambda *_: zeros)


def _const(shape):
    zeros = (0,) * len(shape)
    return pl.BlockSpec(shape, lambda *_: zeros, pipeline_mode=pl.Buffered(1))


def _rows(tile, width, n_tiles=None):
    if n_tiles is None:
        return pl.BlockSpec((tile, width), lambda i: (i, 0))
    return pl.BlockSpec((tile, width), lambda i: (n_tiles - 1 - i, 0))


def _call(body, name, grid, in_specs, out_specs, out_shape, scratch=()):
    return pl.pallas_call(
        body, name=name, grid=grid, in_specs=in_specs, out_specs=out_specs, out_shape=out_shape,
        scratch_shapes=list(scratch),
        compiler_params=pltpu.CompilerParams(dimension_semantics=("arbitrary",) * len(grid),
                                             vmem_limit_bytes=VMEM_LIMIT))


def _sds(shape, dtype=F32):
    return jax.ShapeDtypeStruct(shape, dtype)


def _tile_of(n, cap=512):
    if n <= LANE:
        return n
    best = LANE
    for t in range(LANE, cap + 1, LANE):
        if n % t == 0:
            best = t
    return best


BIG_TILE = 512


def _big_tile(n_tok):
    return BIG_TILE if n_tok % BIG_TILE == 0 else TOK_TILE


def _inproj_fwd(x, nm, wu, wz, wx, wd, name):
    n_tok = x.shape[0]
    tm = _big_tile(n_tok)

    def body(x_ref, nm_ref, wu_ref, wz_ref, wx_ref, wd_ref, u_ref, z_ref, xbc_ref, dt_ref):
        h = _rms(x_ref[...], nm_ref[...]).astype(BF16)
        u_ref[...] = _nn(h, wu_ref[...])
        z_ref[...] = _nn(h, wz_ref[...])
        xbc_ref[...] = _nn(h, wx_ref[...])
        dt_ref[...] = _nn(h, wd_ref[...])

    return _call(
        body, name, (n_tok // tm,),
        [_rows(tm, D_MODEL), _const((1, D_MODEL)), _const(wu.shape), _const(wz.shape), _const(wx.shape), _const(wd.shape)],
        [_rows(tm, D_MODEL), _rows(tm, D_MODEL), _rows(tm, SSD_CONV_DIM), _rows(tm, LANE)],
        [_sds((n_tok, D_MODEL)), _sds((n_tok, D_MODEL)), _sds((n_tok, SSD_CONV_DIM)), _sds((n_tok, LANE))],
    )(x, nm, wu, wz, wx, wd)


def _inproj_bwd(x, nm, du, dz, dxbc, ddt, dres, wu, wz, wx, wd, name):
    n_tok = x.shape[0]
    tm = _big_tile(n_tok)

    def body(x_ref, nm_ref, du_ref, dz_ref, dxbc_ref, ddt_ref, dres_ref, wu_ref, wz_ref, wx_ref, wd_ref,
             dx_ref, h_ref, dnm_ref):
        dh = (_nt(du_ref[...], wu_ref[...]) + _nt(dz_ref[...], wz_ref[...])
              + _nt(dxbc_ref[...], wx_ref[...]) + _nt(ddt_ref[...], wd_ref[...]))
        h, vjp = jax.vjp(_rms, x_ref[...], nm_ref[...])
        dx, dnm = vjp(dh)
        dx_ref[...] = dres_ref[...] + dx
        h_ref[...] = h.astype(BF16)

        @pl.when(pl.program_id(0) == 0)
        def _():
            dnm_ref[...] = jnp.zeros_like(dnm_ref)

        dnm_ref[...] += dnm

    return _call(
        body, name, (n_tok // tm,),
        [_rows(tm, D_MODEL), _const((1, D_MODEL)), _rows(tm, D_MODEL), _rows(tm, D_MODEL), _rows(tm, SSD_CONV_DIM),
         _rows(tm, LANE), _rows(tm, D_MODEL), _const(wu.shape), _const(wz.shape), _const(wx.shape), _const(wd.shape)],
        [_rows(tm, D_MODEL), _rows(tm, D_MODEL), _full((1, D_MODEL))],
        [_sds((n_tok, D_MODEL)), _sds((n_tok, D_MODEL), BF16), _sds((1, D_MODEL))],
    )(x, nm, du, dz, dxbc, ddt, dres, wu, wz, wx, wd)


def _ffn_act(gt, up):
    return _silu(gt) * up


FFN_BLOCK = FFN_HIDDEN // N_DEV
FFN_BLOCK_PAD = -(-FFN_BLOCK // LANE) * LANE


FFN_PAD = N_DEV * FFN_BLOCK_PAD


def _mix_ffn_fwd(x0, ya, yb, wo, nf, wg, wu, wd, name):
    n_tok = x0.shape[0]
    tm = TOK_TILE

    def body(x0_ref, ya_ref, yb_ref, wo_ref, nf_ref, wg_ref, wu_ref, wd_ref, x1_ref, x2_ref):
        x1 = x0_ref[...] + _nn(ya_ref[...], wo_ref[:D_MODEL, :]) + _nn(yb_ref[...], wo_ref[D_MODEL:, :])
        h = _rms(x1, nf_ref[...]).astype(BF16)
        x1_ref[...] = x1
        x2_ref[...] = x1 + _nn(_ffn_act(_nt(h, wg_ref[...]), _nt(h, wu_ref[...])), wd_ref[...])

    return _call(
        body, name, (n_tok // tm,),
        [_rows(tm, D_MODEL), _rows(tm, D_MODEL), _rows(tm, D_MODEL), _const(wo.shape),
         _const((1, D_MODEL)), _const(wg.shape), _const(wu.shape), _const(wd.shape)],
        [_rows(tm, D_MODEL), _rows(tm, D_MODEL)],
        [_sds((n_tok, D_MODEL)), _sds((n_tok, D_MODEL))],
    )(x0, ya, yb, wo, nf, wg, wu, wd)


def _mix_ffn_bwd(x1, dx2, wo, nf, wg, wu, wd, after, name):
    n_tok = x1.shape[0]
    tm = TOK_TILE
    n_chunks = 3
    hc = FFN_PAD // n_chunks

    def body(x1_ref, dx2_ref, wo_ref, nf_ref, wg_ref, wu_ref, wd_ref, after_ref,
             dx1_ref, dya_ref, dyb_ref, h_ref, a_ref, dgt_ref, dup_ref, dx2b_ref, dx1b_ref, dnf_ref):
        dx2 = dx2_ref[...]
        dx2b = dx2.astype(BF16)
        dx2b_ref[...] = dx2b
        h, rms_vjp = jax.vjp(_rms, x1_ref[...], nf_ref[...])
        hb = h.astype(BF16)
        dh = jnp.zeros_like(h)
        for c in range(n_chunks):
            rows = pl.ds(c * hc, hc)
            a, act_vjp = jax.vjp(_ffn_act, _nt(hb, wg_ref[rows, :]), _nt(hb, wu_ref[rows, :]))
            dgt, dup = act_vjp(_nt(dx2b, wd_ref[rows, :]))
            a_ref[:, c * hc:(c + 1) * hc] = a.astype(BF16)
            dgt_ref[:, c * hc:(c + 1) * hc] = dgt.astype(BF16)
            dup_ref[:, c * hc:(c + 1) * hc] = dup.astype(BF16)
            dh = dh + _nn(dgt, wg_ref[rows, :]) + _nn(dup, wu_ref[rows, :])
        dx, dnf = rms_vjp(dh)
        dx1 = dx2 + dx
        dx1b = dx1.astype(BF16)
        dx1_ref[...] = dx1
        dx1b_ref[...] = dx1b
        dya_ref[...] = _nt(dx1b, wo_ref[:D_MODEL, :])
        dyb_ref[...] = _nt(dx1b, wo_ref[D_MODEL:, :])
        h_ref[...] = hb

        @pl.when(pl.program_id(0) == 0)
        def _():
            dnf_ref[...] = jnp.zeros_like(dnf_ref)

        dnf_ref[...] += dnf

    hidden = _rows(tm, FFN_PAD)
    return _call(
        body, name, (n_tok // tm,),
        [_rows(tm, D_MODEL), _rows(tm, D_MODEL), _const(wo.shape), _const((1, D_MODEL)),
         _const(wg.shape), _const(wu.shape), _const(wd.shape), HBM],
        [_rows(tm, D_MODEL), _rows(tm, D_MODEL), _rows(tm, D_MODEL), _rows(tm, D_MODEL), hidden, hidden, hidden,
         _rows(tm, D_MODEL), _rows(tm, D_MODEL), _full((1, D_MODEL))],
        [_sds((n_tok, D_MODEL)), _sds((n_tok, D_MODEL)), _sds((n_tok, D_MODEL)), _sds((n_tok, D_MODEL), BF16),
         _sds((n_tok, FFN_PAD), BF16), _sds((n_tok, FFN_PAD), BF16), _sds((n_tok, FFN_PAD), BF16),
         _sds((n_tok, D_MODEL), BF16), _sds((n_tok, D_MODEL), BF16), _sds((1, D_MODEL))],
    )(x1, dx2, wo, nf, wg, wu, wd, after)


def _loss_head(x, nf, target, name):
    n_tok = x.shape[0]
    tm = TOK_TILE

    def loss_of(xv, g, t):
        e = _rms(xv, g) - t
        return 0.5 * jnp.sum(jnp.sum(e * e, axis=-1, keepdims=True) * (1.0 / D_MODEL), axis=0, keepdims=True)

    def body(x_ref, nf_ref, t_ref, loss_ref, dx_ref, dnf_ref):
        loss, vjp = jax.vjp(functools.partial(loss_of, t=t_ref[...]), x_ref[...], nf_ref[...])
        dx, dnf = vjp(jnp.ones_like(loss))
        dx_ref[...] = dx

        @pl.when(pl.program_id(0) == 0)
        def _():
            dnf_ref[...] = jnp.zeros_like(dnf_ref)
            loss_ref[...] = jnp.zeros_like(loss_ref)

        dnf_ref[...] += dnf
        loss_ref[...] += jnp.broadcast_to(loss, loss_ref.shape)

    return _call(
        body, name, (n_tok // tm,),
        [_rows(tm, D_MODEL), _const((1, D_MODEL)), _rows(tm, D_MODEL)],
        [_full((SUBLANE, LANE)), _rows(tm, D_MODEL), _full((1, D_MODEL))],
        [_sds((SUBLANE, LANE)), _sds((n_tok, D_MODEL)), _sds((1, D_MODEL))],
    )(x, nf, target)


GRAD_WIRE = BF16


def _matmul_tn(a, b, name):
    n_tok, k1 = a.shape
    k2 = b.shape[1]
    t1 = _tile_of(k1)

    def body(a_ref, b_ref, o_ref):
        o_ref[...] = _tn(a_ref[...], b_ref[...]).astype(GRAD_WIRE)

    return _call(body, name, (k1 // t1,), [pl.BlockSpec((n_tok, t1), lambda i: (0, i)), _const((n_tok, k2))],
                 [pl.BlockSpec((t1, k2), lambda i: (i, 0))], [_sds((k1, k2), GRAD_WIRE)])(a, b)[0]


def _matmul_tn_lhs_blocks(a, b, width, keep, name):
    n_tok, k1 = a.shape
    k2 = b.shape[1]

    def body(a_ref, b_ref, o_ref):
        o_ref[...] = _tn(a_ref[...], b_ref[...])[:keep, :].astype(GRAD_WIRE)

    return _call(body, name, (k1 // width,), [pl.BlockSpec((n_tok, width), lambda d: (0, d)), _const((n_tok, k2))],
                 [pl.BlockSpec((None, keep, k2), lambda d: (d, 0, 0))], [_sds((k1 // width, keep, k2), GRAD_WIRE)])(a, b)[0]


def _matmul_tn_pair(a0, a1, b, name):
    n_tok, k1 = a0.shape
    k2 = b.shape[1]
    t1 = _tile_of(k1)
    n1 = k1 // t1

    def body(a0_ref, a1_ref, b_ref, o_ref):
        @pl.when(pl.program_id(0) < n1)
        def _():
            o_ref[...] = _tn(a0_ref[...], b_ref[...]).astype(GRAD_WIRE)

        @pl.when(pl.program_id(0) >= n1)
        def _():
            o_ref[...] = _tn(a1_ref[...], b_ref[...]).astype(GRAD_WIRE)

    return _call(
        body, name, (2 * n1,),
        [pl.BlockSpec((n_tok, t1), lambda i: (0, jnp.minimum(i, n1 - 1))),
         pl.BlockSpec((n_tok, t1), lambda i: (0, jnp.maximum(i - n1, 0))), _const((n_tok, k2))],
        [pl.BlockSpec((None, t1, k2), lambda i: (i // n1, i % n1, 0))], [_sds((2, k1, k2), GRAD_WIRE)])(a0, a1, b)[0]


W_IN_BLOCK = IN_PROJ // N_DEV
W_IN_SPLITS = (D_MODEL, 2 * D_MODEL, 2 * D_MODEL + SSD_CONV_DIM)
RELAYOUT_TILE = 256


def _w_in_split(blocks, after, name):
    tr = RELAYOUT_TILE

    def body(b_ref, after_ref, wu_ref, wz_ref, wx_ref, wd_ref):
        full = jnp.concatenate([b_ref[d] for d in range(N_DEV)], axis=1)
        wu_ref[...] = full[:, :W_IN_SPLITS[0]]
        wz_ref[...] = full[:, W_IN_SPLITS[0]:W_IN_SPLITS[1]]
        wx_ref[...] = full[:, W_IN_SPLITS[1]:W_IN_SPLITS[2]]
        wd_ref[...] = jnp.concatenate([full[:, W_IN_SPLITS[2]:], jnp.zeros((tr, LANE - SSD_HEADS), full.dtype)], axis=1)

    return _call(
        body, name, (D_MODEL // tr,), [pl.BlockSpec((N_DEV, tr, W_IN_BLOCK), lambda i: (0, i, 0)), HBM],
        [_rows(tr, D_MODEL), _rows(tr, D_MODEL), _rows(tr, SSD_CONV_DIM), _rows(tr, LANE)],
        [_sds((D_MODEL, D_MODEL), BF16), _sds((D_MODEL, D_MODEL), BF16), _sds((D_MODEL, SSD_CONV_DIM), BF16),
         _sds((D_MODEL, LANE), BF16)],
    )(blocks, after)


def _w_in_grad_blocks(gu, gz, gx, gdt, name):
    tr = RELAYOUT_TILE

    def body(gu_ref, gz_ref, gx_ref, gdt_ref, o_ref):
        full = jnp.concatenate([gu_ref[...], gz_ref[...], gx_ref[...], gdt_ref[...]], axis=1)
        for d in range(N_DEV):
            o_ref[d] = full[:, d * W_IN_BLOCK:(d + 1) * W_IN_BLOCK]

    return _call(
        body, name, (D_MODEL // tr,),
        [_rows(tr, D_MODEL), _rows(tr, D_MODEL), _rows(tr, SSD_CONV_DIM), _rows(tr, LANE)],
        [pl.BlockSpec((N_DEV, tr, W_IN_BLOCK), lambda i: (0, i, 0))], [_sds((N_DEV, D_MODEL, W_IN_BLOCK), gu.dtype)],
    )(gu, gz, gx, gdt)[0]


S5_SLICES = D_MODEL // LANE
S5_SLICE_STATES = S5_LANES // S5_SLICES
SCAN_LANES = 512


def _s5_scan(br_ref, bi_ref, a_r, a_i, car_r, car_i, ini_r, ini_i, reverse, xr_ref=None, xi_ref=None,
             acc_r=None, acc_i=None, row0=0):
    n_rows = S5_TILE
    seg = n_rows // SUBLANE
    order = range(SUBLANE - 1, -1, -1) if reverse else range(SUBLANE)

    def rows(t):
        return pl.ds(pl.multiple_of(row0 + ((seg - 1 - t) if reverse else t) * SUBLANE, SUBLANE), SUBLANE)

    tiles_per = SCAN_LANES // LANE

    def load(ref, t, lb):
        return jnp.concatenate([ref[lb * tiles_per + j, rows(t), :] for j in range(tiles_per)], axis=1)

    def store(ref, t, lb, val):
        for j in range(tiles_per):
            ref[lb * tiles_per + j, rows(t), :] = val[:, j * LANE:(j + 1) * LANE]

    for lb in range(S5_LANES // SCAN_LANES):
        lanes = pl.ds(lb * SCAN_LANES, SCAN_LANES)
        ar1, ai1 = a_r[:, lb * SCAN_LANES:(lb + 1) * SCAN_LANES], a_i[:, lb * SCAN_LANES:(lb + 1) * SCAN_LANES]
        ar8 = jnp.broadcast_to(ar1, (SUBLANE, SCAN_LANES))
        ai8 = jnp.broadcast_to(ai1, (SUBLANE, SCAN_LANES))

        def local(t, c):
            sr, si = c
            return (ar8 * sr - ai8 * si + load(br_ref, t, lb), ar8 * si + ai8 * sr + load(bi_ref, t, lb))

        zero = jnp.zeros((SUBLANE, SCAN_LANES), F32)
        er, ei = lax.fori_loop(0, seg, local, (zero, zero))
        pr, pi = ar1, ai1
        for _ in range(seg.bit_length() - 1):
            pr, pi = pr * pr - pi * pi, 2.0 * pr * pi
        cr, ci = car_r[:, lanes], car_i[:, lanes]
        for s in order:
            ini_r[s:s + 1, lanes] = cr
            ini_i[s:s + 1, lanes] = ci
            cr, ci = pr * cr - pi * ci + er[s:s + 1, :], pr * ci + pi * cr + ei[s:s + 1, :]
        car_r[:, lanes] = cr
        car_i[:, lanes] = ci

        if xr_ref is None:
            def final(t, c):
                sr, si = c
                nr = ar8 * sr - ai8 * si + load(br_ref, t, lb)
                ni = ar8 * si + ai8 * sr + load(bi_ref, t, lb)
                store(br_ref, t, lb, nr)
                store(bi_ref, t, lb, ni)
                return nr, ni

            lax.fori_loop(0, seg, final, (ini_r[:, lanes], ini_i[:, lanes]))
        else:
            def final_acc(t, c):
                sr, si, gr, gi = c
                xr, xi = load(xr_ref, t, lb), load(xi_ref, t, lb)
                gr = gr + sr * xr + si * xi
                gi = gi + si * xr - sr * xi
                nr = ar8 * sr - ai8 * si + load(br_ref, t, lb)
                ni = ar8 * si + ai8 * sr + load(bi_ref, t, lb)
                store(br_ref, t, lb, nr)
                store(bi_ref, t, lb, ni)
                return nr, ni, gr, gi

            _, _, gr, gi = lax.fori_loop(0, seg, final_acc,
                                         (ini_r[:, lanes], ini_i[:, lanes], acc_r[:, lanes], acc_i[:, lanes]))
            acc_r[:, lanes] = gr
            acc_i[:, lanes] = gi


def _s5_tail(gg, q, sn):
    return _rms(gg * _sigmoid(q), sn)


def _scan_order(n_rows):
    seg = n_rows // SUBLANE
    r = lax.broadcasted_iota(jnp.int32, (n_rows, n_rows), 0)
    c = lax.broadcasted_iota(jnp.int32, (n_rows, n_rows), 1)
    return (c == (r % SUBLANE) * seg + r // SUBLANE).astype(F32)


S5_STATE_TILES = S5_LANES // LANE
TILES_PER_SLICE = S5_SLICE_STATES // LANE


def _put_states(ref, k, val):
    for j in range(TILES_PER_SLICE):
        ref[k * TILES_PER_SLICE + j] = val[:, j * LANE:(j + 1) * LANE]


def _get_states(ref, k):
    return jnp.concatenate([ref[k * TILES_PER_SLICE + j] for j in range(TILES_PER_SLICE)], axis=1)


def _state_rows(tile, n_tiles=None):
    if n_tiles is None:
        return pl.BlockSpec((S5_STATE_TILES, tile, LANE), lambda i: (0, i, 0))
    return pl.BlockSpec((S5_STATE_TILES, tile, LANE), lambda i: (0, n_tiles - 1 - i, 0))


def _s5_fwd(u, a_r, a_i, bdb, bcr, bci, dsk, wglu, bglu, sn, name):
    n_tok = u.shape[0]
    n_sub = 2
    tc = n_sub * S5_TILE
    sw = S5_SLICE_STATES

    def body(u_ref, ar_ref, ai_ref, bdb_ref, bcr_ref, bci_ref, d_ref, wg_ref, bg_ref, sn_ref,
             ya_ref, xr_ref, xi_ref, v_ref, car_r, car_i, ini_r, ini_i):
        @pl.when(pl.program_id(0) == 0)
        def _():
            car_r[...] = jnp.zeros_like(car_r)
            car_i[...] = jnp.zeros_like(car_i)

        order = _scan_order(S5_TILE)
        u_t = jnp.concatenate([_nn_f32(order, u_ref[s * S5_TILE:(s + 1) * S5_TILE, :]) for s in range(n_sub)], axis=0)
        ub = u_t.astype(BF16)
        for k in range(S5_SLICES):
            bu = _nn(ub[:, k * LANE:(k + 1) * LANE], bdb_ref[k])
            _put_states(xr_ref, k, bu[:, :sw])
            _put_states(xi_ref, k, bu[:, sw:])
        for s in range(n_sub):
            _s5_scan(xr_ref, xi_ref, ar_ref[...], ai_ref[...], car_r, car_i, ini_r, ini_i, reverse=False, row0=s * S5_TILE)
        vs = [_nn(_get_states(xr_ref, k), bcr_ref[k]) - _nn(_get_states(xi_ref, k), bci_ref[k])
              for k in range(S5_SLICES)]
        v = jnp.concatenate(vs, axis=1) + d_ref[...] * u_t
        v_ref[...] = v
        gg = _gelu(v)
        ya = _s5_tail(gg, _nn(gg, wg_ref[...]) + bg_ref[...], sn_ref[...])
        for s in range(n_sub):
            rows = slice(s * S5_TILE, (s + 1) * S5_TILE)
            ya_ref[rows, :] = _tn_f32(order, ya[rows, :]).astype(BF16)

    return _call(
        body, name, (n_tok // tc,),
        [_rows(tc, D_MODEL), _const((1, S5_LANES)), _const((1, S5_LANES)), _const(bdb.shape), _const(bcr.shape),
         _const(bci.shape), _const((1, D_MODEL)), _const(wglu.shape), _const((1, D_MODEL)), _const((1, D_MODEL))],
        [_rows(tc, D_MODEL), _state_rows(tc), _state_rows(tc), _rows(tc, D_MODEL)],
        [_sds((n_tok, D_MODEL), BF16), _sds((S5_STATE_TILES, n_tok, LANE)), _sds((S5_STATE_TILES, n_tok, LANE)),
         _sds((n_tok, D_MODEL))],
        scratch=[pltpu.VMEM((1, S5_LANES), F32), pltpu.VMEM((1, S5_LANES), F32),
                 pltpu.VMEM((SUBLANE, S5_LANES), F32), pltpu.VMEM((SUBLANE, S5_LANES), F32)],
    )(u, a_r, a_i, bdb, bcr, bci, dsk, wglu, bglu, sn)


def _s5_bwd(dya, v, u, xr, xi, a_r, a_i, bdb, bcr, bci, dsk, wglu, bglu, sn, after, name):
    n_tok = u.shape[0]
    tc = S5_TILE
    nt = n_tok // tc
    sw = S5_SLICE_STATES

    def body(dya_ref, v_ref, u_ref, xr_ref, xi_ref, ar_ref, ai_ref, bdb_ref, bcr_ref, bci_ref, d_ref, wg_ref, bg_ref, sn_ref,
             after_ref, du_ref, gg_ref, dq_ref, gbdb_ref, gbcr_ref, gbci_ref, gar_ref, gai_ref, gd_ref, gbg_ref, gsn_ref,
             gr_ref, gi_ref, car_r, car_i, ini_r, ini_i):
        @pl.when(pl.program_id(0) == 0)
        def _():
            for r in (car_r, car_i, gbdb_ref, gbcr_ref, gbci_ref, gar_ref, gai_ref, gd_ref, gbg_ref, gsn_ref):
                r[...] = jnp.zeros_like(r)

        order = _scan_order(tc)
        u_t = _nn_f32(order, u_ref[...])
        gg, gelu_vjp = jax.vjp(_gelu, v_ref[...])
        _, tail_vjp = jax.vjp(_s5_tail, gg, _nn(gg, wg_ref[...]) + bg_ref[...], sn_ref[...])
        dgg, dq, dsn = tail_vjp(_nn_f32(order, dya_ref[...]))
        (dv,) = gelu_vjp(dgg + _nt(dq, wg_ref[...]))
        gg_ref[...] = gg.astype(BF16)
        dq_ref[...] = dq.astype(BF16)
        gd_ref[...] += jnp.sum(dv * u_t, axis=0, keepdims=True)
        gbg_ref[...] += jnp.sum(dq, axis=0, keepdims=True)
        gsn_ref[...] += dsn
        dvb = dv.astype(BF16)
        for k in range(S5_SLICES):
            dvk = dvb[:, k * LANE:(k + 1) * LANE]
            _put_states(gr_ref, k, _nt(dvk, bcr_ref[k]))
            _put_states(gi_ref, k, -_nt(dvk, bci_ref[k]))
            gbcr_ref[k] += _tn(_get_states(xr_ref, k), dvk)
            gbci_ref[k] -= _tn(_get_states(xi_ref, k), dvk)
        _s5_scan(gr_ref, gi_ref, ar_ref[...], -ai_ref[...], car_r, car_i, ini_r, ini_i, reverse=True,
                 xr_ref=xr_ref, xi_ref=xi_ref, acc_r=gar_ref, acc_i=gai_ref)
        ub = u_t.astype(BF16)
        dus = []
        for k in range(S5_SLICES):
            gk_r, gk_i = _get_states(gr_ref, k).astype(BF16), _get_states(gi_ref, k).astype(BF16)
            bk = bdb_ref[k]
            dus.append(_nt(gk_r, bk[:, :sw]) + _nt(gk_i, bk[:, sw:]))
            uk = ub[:, k * LANE:(k + 1) * LANE]
            gbdb_ref[k, :, :sw] += _tn(uk, gk_r)
            gbdb_ref[k, :, sw:] += _tn(uk, gk_i)
        du_ref[...] = _tn_f32(order, jnp.concatenate(dus, axis=1) + d_ref[...] * dv).astype(BF16)

    rev = functools.partial(_rows, n_tiles=nt)
    return _call(
        body, name, (nt,),
        [rev(tc, D_MODEL), rev(tc, D_MODEL), rev(tc, D_MODEL), _state_rows(tc, nt), _state_rows(tc, nt),
         _const((1, S5_LANES)), _const((1, S5_LANES)), _const(bdb.shape), _const(bcr.shape), _const(bci.shape),
         _const((1, D_MODEL)), _const(wglu.shape), _const((1, D_MODEL)), _const((1, D_MODEL)), HBM],
        [rev(tc, D_MODEL), rev(tc, D_MODEL), rev(tc, D_MODEL), _full(bdb.shape), _full(bcr.shape), _full(bci.shape),
         _full((SUBLANE, S5_LANES)), _full((SUBLANE, S5_LANES)), _full((1, D_MODEL)), _full((1, D_MODEL)), _full((1, D_MODEL))],
        [_sds((n_tok, D_MODEL), BF16), _sds((n_tok, D_MODEL), BF16), _sds((n_tok, D_MODEL), BF16), _sds(bdb.shape), _sds(bcr.shape),
         _sds(bci.shape), _sds((SUBLANE, S5_LANES)), _sds((SUBLANE, S5_LANES)), _sds((1, D_MODEL)), _sds((1, D_MODEL)),
         _sds((1, D_MODEL))],
        scratch=[pltpu.VMEM((S5_STATE_TILES, tc, LANE), F32), pltpu.VMEM((S5_STATE_TILES, tc, LANE), F32),
                 pltpu.VMEM((1, S5_LANES), F32), pltpu.VMEM((1, S5_LANES), F32),
                 pltpu.VMEM((SUBLANE, S5_LANES), F32), pltpu.VMEM((SUBLANE, S5_LANES), F32)],
    )(dya, v, u, xr, xi, a_r, a_i, bdb, bcr, bci, dsk, wglu, bglu, sn, after)


SSD_WIDTH = SSD_HEADS * SSD_HEAD_DIM
SSD_GROUPS = 2
HEADS_PER_GROUP = SSD_HEADS // SSD_GROUPS


def _take(x, axis, start, size):
    n = x.shape[axis]

    def sl(v):
        return lax.slice_in_dim(v, start, start + size, axis=axis)

    @jax.custom_vjp
    def f(v):
        return sl(v)

    def bwd(_, g):
        parts = []
        if start:
            parts.append(jnp.zeros(g.shape[:axis] + (start,) + g.shape[axis + 1:], g.dtype))
        parts.append(g)
        if n - start - size:
            parts.append(jnp.zeros(g.shape[:axis] + (n - start - size,) + g.shape[axis + 1:], g.dtype))
        return (jnp.concatenate(parts, axis=axis) if len(parts) > 1 else g,)

    f.defvjp(lambda v: (sl(v), None), bwd)
    return f(x)


def _lane_of(x, h):
    col = lax.broadcasted_iota(jnp.int32, x.shape, 1)
    return jnp.sum(jnp.where(col == h, x, 0.0), axis=1, keepdims=True)


def _ssd_chunk(xc, z, dt, dtb, alog, dvec, gn, st, nn, nt, tn, cumsum, take):
    t_len = xc.shape[0]
    xa = _silu(xc)
    dtp = _softplus(dt + dtb)
    d_a = dtp * (-jnp.exp(alog))
    row = lax.broadcasted_iota(jnp.int32, (t_len, t_len), 0)
    col = lax.broadcasted_iota(jnp.int32, (t_len, t_len), 1)
    causal = row >= col
    cum = cumsum(causal.astype(F32), d_a)
    eye = (row == col).astype(F32)
    group_width = HEADS_PER_GROUP * SSD_HEAD_DIM
    ys, sts = [], []
    for g in range(SSD_GROUPS):
        bg = take(xa, 1, SSD_WIDTH + g * SSD_STATE, SSD_STATE)
        cg = take(xa, 1, SSD_WIDTH + (SSD_GROUPS + g) * SSD_STATE, SSD_STATE)
        cb = nt(cg, bg)
        x_g = take(xa, 1, g * group_width, group_width)
        st_g = take(st, 0, g * group_width, group_width)
        diag, dt_l, d_l, grow_l, keep_l, last_r = [], [], [], [], [], []
        for r in range(HEADS_PER_GROUP):
            h = g * HEADS_PER_GROUP + r
            cc = _lane_of(cum, h)
            cr = jnp.sum(cc * eye, axis=0, keepdims=True)
            decay = jnp.exp(jnp.where(causal, cc - cr, -1e30))
            dt_h = _lane_of(dtp, h)
            c_last = jnp.sum(jnp.where(row[:, :1] == t_len - 1, cc, 0.0), axis=0, keepdims=True)
            lanes = (t_len, SSD_HEAD_DIM)
            diag.append(nn(cb * decay, take(x_g, 1, r * SSD_HEAD_DIM, SSD_HEAD_DIM) * dt_h))
            dt_l.append(jnp.broadcast_to(dt_h, lanes))
            d_l.append(jnp.broadcast_to(_lane_of(dvec, h), lanes))
            grow_l.append(jnp.broadcast_to(jnp.exp(cc), lanes))
            keep_l.append(jnp.broadcast_to(jnp.exp(c_last - cc), lanes))
            last_r.append(jnp.broadcast_to(jnp.exp(c_last), (SSD_HEAD_DIM, SSD_STATE)))
        side = functools.partial(jnp.concatenate, axis=1)
        xdt_g = x_g * side(dt_l)
        ys.append(side(diag) + side(grow_l) * nt(cg, st_g) + side(d_l) * x_g)
        sts.append(jnp.concatenate(last_r, axis=0) * st_g + tn(xdt_g * side(keep_l), bg))
    y = jnp.concatenate(ys, axis=1) * _silu(z)
    return _rms(y, gn), jnp.concatenate(sts, axis=0)


SSD_TILE = SSD_CHUNK


def _ssd_tile(xc, z, dt, dtb, alog, dvec, gn, st, nn, nt, tn, cumsum, take):
    ys = []
    for c in range(xc.shape[0] // SSD_CHUNK):
        rows = (c * SSD_CHUNK, SSD_CHUNK)
        y, st = _ssd_chunk(take(xc, 0, *rows), take(z, 0, *rows), take(dt, 0, *rows), dtb, alog, dvec, gn, st,
                           nn, nt, tn, cumsum, take)
        ys.append(y)
    return jnp.concatenate(ys, axis=0), st


def _shift_back(cur, prev, j):
    if j == 0:
        return cur
    row = lax.broadcasted_iota(jnp.int32, cur.shape, 0)
    return jnp.where(row < j, pltpu.roll(prev, j, 0), pltpu.roll(cur, j, 0))


def _shift_ahead(cur, nxt, j):
    if j == 0:
        return cur
    n = cur.shape[0]
    row = lax.broadcasted_iota(jnp.int32, cur.shape, 0)
    return jnp.where(row >= n - j, pltpu.roll(nxt, n - j, 0), pltpu.roll(cur, n - j, 0))


def _conv(cur, prev, w, b):
    out = b + w[SSD_CONV - 1:SSD_CONV, :] * cur
    for k in range(SSD_CONV - 1):
        out = out + w[k:k + 1, :] * _shift_back(cur, prev, SSD_CONV - 1 - k)
    return out


def _ssd_fwd(xbc, z, dt, conv_w, conv_b, dtb, alog, dvec, gn, name):
    n_tok = xbc.shape[0]
    tc = SSD_TILE
    nc = n_tok // tc
    st_rows = SSD_HEADS * SSD_HEAD_DIM

    def body(cur_ref, prev_ref, z_ref, dt_ref, w_ref, b_ref, dtb_ref, alog_ref, dvec_ref, gn_ref,
             yb_ref, stin_ref, st_ref):
        i = pl.program_id(0)

        @pl.when(i == 0)
        def _():
            st_ref[...] = jnp.zeros_like(st_ref)

        prev = jnp.where(i > 0, prev_ref[...], 0.0)
        xc = _conv(cur_ref[...], prev, w_ref[...], b_ref[...])
        st = st_ref[...]
        stin_ref[0] = st
        yb, st_new = _ssd_tile(xc, z_ref[...], dt_ref[...], dtb_ref[...], alog_ref[...], dvec_ref[...], gn_ref[...], st,
                               _nn, _nt, _tn, _nn_f32, lambda v, axis, start, size: lax.slice_in_dim(v, start, start + size, axis=axis))
        yb_ref[...] = yb.astype(BF16)
        st_ref[...] = st_new

    return _call(
        body, name, (nc,),
        [_rows(tc, SSD_CONV_DIM), pl.BlockSpec((tc, SSD_CONV_DIM), lambda i: (jnp.maximum(i - 1, 0), 0)),
         _rows(tc, D_MODEL), _rows(tc, LANE), _const((SSD_CONV, SSD_CONV_DIM)), _const((1, SSD_CONV_DIM)),
         _const((1, LANE)), _const((1, LANE)), _const((1, LANE)), _const((1, D_MODEL))],
        [_rows(tc, D_MODEL), pl.BlockSpec((1, st_rows, SSD_STATE), lambda i: (i, 0, 0))],
        [_sds((n_tok, D_MODEL), BF16), _sds((nc, st_rows, SSD_STATE))],
        scratch=[pltpu.VMEM((st_rows, SSD_STATE), F32)],
    )(xbc, xbc, z, dt, conv_w, conv_b, dtb, alog, dvec, gn)


def _ssd_bwd(dyb, xbc, z, dt, stin, conv_w, conv_b, dtb, alog, dvec, gn, name):
    n_tok = xbc.shape[0]
    tc = SSD_TILE
    nc = n_tok // tc
    st_rows = SSD_HEADS * SSD_HEAD_DIM

    def body(dyb_ref, cur_ref, prev_ref, z_ref, dt_ref, stin_ref, w_ref, b_ref, dtb_ref, alog_ref, dvec_ref, gn_ref,
             dxbc_ref, dz_ref, ddt_ref, gw_ref, gb_ref, gdtb_ref, galog_ref, gdvec_ref, ggn_ref,
             dst_ref, dxc_next_ref):
        i = pl.program_id(0)

        @pl.when(i == 0)
        def _():
            for r in (dst_ref, dxc_next_ref, gw_ref, gb_ref, gdtb_ref, galog_ref, gdvec_ref, ggn_ref):
                r[...] = jnp.zeros_like(r)

        cur = cur_ref[...]
        prev = jnp.where(i < nc - 1, prev_ref[...], 0.0)
        w = w_ref[...]
        xc = _conv(cur, prev, w, b_ref[...])
        chunk = functools.partial(_ssd_tile, nn=_nn_d, nt=_nt_d, tn=_tn_d, cumsum=_cumsum_rows, take=_take)
        _, vjp = jax.vjp(chunk, xc, z_ref[...], dt_ref[...], dtb_ref[...], alog_ref[...], dvec_ref[...], gn_ref[...],
                         stin_ref[0])
        dxc, dz, ddt, gdtb, galog, gdvec, ggn, dst = vjp((dyb_ref[...], dst_ref[...]))
        dst_ref[...] = dst
        dz_ref[...] = dz.astype(BF16)
        ddt_ref[...] = ddt.astype(BF16)
        gdtb_ref[...] += gdtb
        galog_ref[...] += galog
        gdvec_ref[...] += gdvec
        ggn_ref[...] += ggn
        dxc_next = dxc_next_ref[...]
        dxbc = w[SSD_CONV - 1:SSD_CONV, :] * dxc
        gws = []
        for k in range(SSD_CONV - 1):
            j = SSD_CONV - 1 - k
            dxbc = dxbc + w[k:k + 1, :] * _shift_ahead(dxc, dxc_next, j)
            gws.append(jnp.sum(dxc * _shift_back(cur, prev, j), axis=0, keepdims=True))
        gws.append(jnp.sum(dxc * cur, axis=0, keepdims=True))
        dxbc_ref[...] = dxbc.astype(BF16)
        gw_ref[...] += jnp.concatenate(gws, axis=0)
        gb_ref[...] += jnp.sum(dxc, axis=0, keepdims=True)
        dxc_next_ref[...] = dxc

    rev = functools.partial(_rows, n_tiles=nc)
    return _call(
        body, name, (nc,),
        [rev(tc, D_MODEL), rev(tc, SSD_CONV_DIM),
         pl.BlockSpec((tc, SSD_CONV_DIM), lambda i: (jnp.maximum(nc - 2 - i, 0), 0)),
         rev(tc, D_MODEL), rev(tc, LANE), pl.BlockSpec((1, st_rows, SSD_STATE), lambda i: (nc - 1 - i, 0, 0)),
         _const((SSD_CONV, SSD_CONV_DIM)), _const((1, SSD_CONV_DIM)), _const((1, LANE)), _const((1, LANE)),
         _const((1, LANE)), _const((1, D_MODEL))],
        [rev(tc, SSD_CONV_DIM), rev(tc, D_MODEL), rev(tc, LANE), _full((SSD_CONV, SSD_CONV_DIM)), _full((1, SSD_CONV_DIM)),
         _full((1, LANE)), _full((1, LANE)), _full((1, LANE)), _full((1, D_MODEL))],
        [_sds((n_tok, SSD_CONV_DIM), BF16), _sds((n_tok, D_MODEL), BF16), _sds((n_tok, LANE), BF16), _sds((SSD_CONV, SSD_CONV_DIM)),
         _sds((1, SSD_CONV_DIM)), _sds((1, LANE)), _sds((1, LANE)), _sds((1, LANE)), _sds((1, D_MODEL))],
        scratch=[pltpu.VMEM((st_rows, SSD_STATE), F32), pltpu.VMEM((tc, SSD_CONV_DIM), F32)],
    )(dyb, xbc, xbc, z, dt, stin, conv_w, conv_b, dtb, alog, dvec, gn)


@jax.custom_vjp
def _expand_cols(x, e):
    return _nn_f32(x, e)


_expand_cols.defvjp(
    lambda x, e: (_nn_f32(x, e), e),
    lambda e, g: (lax.dot_general(g, e, (((1,), (1,)), ((), ())), precision=lax.Precision.HIGHEST,
                                  preferred_element_type=F32), jnp.zeros_like(e)))


def _s5_discretize(lam_re, lam_im, log_step, b_re, b_im, expand):
    step = jnp.exp(log_step)
    mag = jnp.exp(lam_re * step)
    ang = lam_im * step
    a_r = mag * jnp.cos(ang)
    a_i = mag * jnp.sin(ang)
    den = lam_re * lam_re + lam_im * lam_im
    n_r = a_r - 1.0
    coef_r = _expand_cols((n_r * lam_re + a_i * lam_im) / den, expand)
    coef_i = _expand_cols((a_i * lam_re - n_r * lam_im) / den, expand)
    return a_r, a_i, coef_r * b_re - coef_i * b_im, coef_r * b_im + coef_i * b_re


def _expand_matrix():
    p = lax.broadcasted_iota(jnp.int32, (S5_STATE, S5_STATE * S5_GROUP), 0)
    c = lax.broadcasted_iota(jnp.int32, (S5_STATE, S5_STATE * S5_GROUP), 1)
    return (c // S5_GROUP == p).astype(F32)


def _s5_discretize_fwd(lam_re, lam_im, log_step, b_re, b_im, name):
    def body(lr_ref, li_ref, ls_ref, br_ref, bi_ref, ar_ref, ai_ref, bbr_ref, bbi_ref):
        outs = _s5_discretize(lr_ref[...], li_ref[...], ls_ref[...], br_ref[...], bi_ref[...], _expand_matrix())
        for r, o in zip((ar_ref, ai_ref, bbr_ref, bbi_ref), outs):
            r[...] = o

    sq, wide = (S5_GROUPS, S5_STATE), (S5_GROUPS, S5_STATE * S5_GROUP)
    return _call(body, name, (1,), [_full(sq), _full(sq), _full((S5_GROUPS, 1)), _full(wide), _full(wide)],
                 [_full(sq), _full(sq), _full(wide), _full(wide)], [_sds(sq), _sds(sq), _sds(wide), _sds(wide)],
                 )(lam_re, lam_im, log_step, b_re, b_im)


def _s5_discretize_bwd(lam_re, lam_im, log_step, b_re, b_im, g_ar8, g_ai8, g_bbr, g_bbi, name):
    def body(lr_ref, li_ref, ls_ref, br_ref, bi_ref, gar_ref, gai_ref, gbbr_ref, gbbi_ref,
             glr_ref, gli_ref, gls_ref, gbr_ref, gbi_ref):
        _, vjp = jax.vjp(functools.partial(_s5_discretize, expand=_expand_matrix()),
                         lr_ref[...], li_ref[...], ls_ref[...], br_ref[...], bi_ref[...])
        grads = vjp((jnp.sum(gar_ref[...], axis=0), jnp.sum(gai_ref[...], axis=0), gbbr_ref[...], gbbi_ref[...]))
        for r, g in zip((glr_ref, gli_ref, gls_ref, gbr_ref, gbi_ref), grads):
            r[...] = g

    sq, wide, col = (S5_GROUPS, S5_STATE), (S5_GROUPS, S5_STATE * S5_GROUP), (S5_GROUPS, 1)
    part = (SUBLANE,) + sq
    return _call(body, name, (1,),
                 [_full(sq), _full(sq), _full(col), _full(wide), _full(wide), _full(part), _full(part), _full(wide), _full(wide)],
                 [_full(sq), _full(sq), _full(col), _full(wide), _full(wide)],
                 [_sds(sq), _sds(sq), _sds(col), _sds(wide), _sds(wide)],
                 )(lam_re, lam_im, log_step, b_re, b_im, g_ar8, g_ai8, g_bbr, g_bbi)


GROUPS_PER_SLICE = LANE // S5_GROUP


def _block_diag_b(bb):
    t = bb.reshape(S5_SLICES, GROUPS_PER_SLICE, S5_STATE, S5_GROUP)
    eye = jnp.eye(GROUPS_PER_SLICE, dtype=bb.dtype)
    return jnp.einsum("kgph,gf->kghfp", t, eye).reshape(S5_SLICES, LANE, S5_SLICE_STATES)


def _block_diag_b_inv(m):
    t = m.reshape(S5_SLICES, GROUPS_PER_SLICE, S5_GROUP, GROUPS_PER_SLICE, S5_STATE)
    return jnp.einsum("kghgp->kgph", t).reshape(S5_GROUPS, S5_STATE * S5_GROUP)


def _block_diag_c(c):
    t = c.reshape(S5_SLICES, GROUPS_PER_SLICE, S5_GROUP, S5_STATE)
    eye = jnp.eye(GROUPS_PER_SLICE, dtype=c.dtype)
    return jnp.einsum("kghp,gf->kgpfh", t, eye).reshape(S5_SLICES, S5_SLICE_STATES, LANE)


def _block_diag_c_inv(m):
    t = m.reshape(S5_SLICES, GROUPS_PER_SLICE, S5_STATE, GROUPS_PER_SLICE, S5_GROUP)
    return jnp.einsum("kgpgh->kghp", t).reshape(S5_GROUPS, S5_GROUP, S5_STATE)


def _pad_lanes(v):
    return jnp.pad(v.reshape(1, -1), ((0, 0), (0, LANE - v.shape[0])))


def _prepare_layer(w, blk, i, after):
    p = {}
    p["wu"], p["wz"], p["wx"], p["wd"] = _w_in_split(blk["w_in"], after, name=f"w_in_split_{i}")
    p["nm"] = w["norm_mix"][i].reshape(1, D_MODEL)
    p["lam_re"], p["lam_im"] = w["s5_lam_re"][i], w["s5_lam_im"][i]
    p["log_step"] = w["s5_log_step"][i].reshape(S5_GROUPS, 1)
    p["b_re"] = w["s5_b_re"][i].reshape(S5_GROUPS, S5_STATE * S5_GROUP)
    p["b_im"] = w["s5_b_im"][i].reshape(S5_GROUPS, S5_STATE * S5_GROUP)
    a_r, a_i, bb_r, bb_i = _s5_discretize_fwd(p["lam_re"], p["lam_im"], p["log_step"], p["b_re"], p["b_im"],
                                              name=f"s5_discretize_{i}")
    p["a_r"], p["a_i"] = a_r.reshape(1, S5_LANES), a_i.reshape(1, S5_LANES)
    p["bdb"] = jnp.concatenate([_block_diag_b(bb_r), _block_diag_b(bb_i)], axis=2).astype(BF16)
    p["bcr"] = _block_diag_c(w["s5_c_re"][i]).astype(BF16)
    p["bci"] = _block_diag_c(w["s5_c_im"][i]).astype(BF16)
    p["dsk"] = w["s5_d"][i].reshape(1, D_MODEL)
    p["wglu"] = blk["s5_w_glu"].reshape(D_MODEL, D_MODEL)
    p["bglu"] = w["s5_b_glu"][i].reshape(1, D_MODEL)
    p["sn"] = w["s5_norm"][i].reshape(1, D_MODEL)
    p["conv_w"] = blk["ssd_conv_w"]
    p["conv_b"] = w["ssd_conv_b"][i].reshape(1, SSD_CONV_DIM)
    p["dtb"] = _pad_lanes(w["ssd_dt_bias"][i])
    p["alog"] = _pad_lanes(w["ssd_a_log"][i])
    p["dvec"] = _pad_lanes(w["ssd_d"][i])
    p["gn"] = w["ssd_norm"][i].reshape(1, D_MODEL)
    p["wo"] = blk["w_out"].reshape(2 * D_MODEL, D_MODEL)
    p["nf"] = w["norm_ffn"][i].reshape(1, D_MODEL)
    p["wg"], p["wup"], p["wdn"] = (blk[n].reshape(FFN_PAD, D_MODEL) for n in ("w_gate", "w_up", "w_down"))
    return p


def _layer_fwd(x0, p, i):
    u, z, xbc, dt = _inproj_fwd(x0, p["nm"], p["wu"], p["wz"], p["wx"], p["wd"], name=f"inproj_fwd_{i}")
    ya, xr, xi, v = _s5_fwd(u, p["a_r"], p["a_i"], p["bdb"], p["bcr"], p["bci"], p["dsk"], p["wglu"], p["bglu"], p["sn"],
                            name=f"s5_fwd_{i}")
    yb, stin = _ssd_fwd(xbc, z, dt, p["conv_w"], p["conv_b"], p["dtb"], p["alog"], p["dvec"], p["gn"], name=f"ssd_fwd_{i}")
    x1, x2 = _mix_ffn_fwd(x0, ya, yb, p["wo"], p["nf"], p["wg"], p["wup"], p["wdn"], name=f"mix_ffn_fwd_{i}")
    return x2, dict(x0=x0, u=u, z=z, xbc=xbc, dt=dt, xr=xr, xi=xi, v=v, stin=stin, ya=ya, yb=yb, x1=x1)


def _layer_bwd(dx2, s, p, i, after, between=None):
    g = {}
    dx1, dya, dyb, h2, act, dgt, dup, dx2b, dx1b, g_nf = _mix_ffn_bwd(
        s["x1"], dx2, p["wo"], p["nf"], p["wg"], p["wup"], p["wdn"], after, name=f"mix_ffn_bwd_{i}")
    g["norm_ffn"] = g_nf.reshape(D_MODEL)
    g["w_down"] = _matmul_tn_lhs_blocks(act, dx2b, FFN_BLOCK_PAD, FFN_BLOCK, name=f"grad_w_down_{i}")
    g["w_gate"] = _matmul_tn_lhs_blocks(dgt, h2, FFN_BLOCK_PAD, FFN_BLOCK, name=f"grad_w_gate_{i}")
    g["w_up"] = _matmul_tn_lhs_blocks(dup, h2, FFN_BLOCK_PAD, FFN_BLOCK, name=f"grad_w_up_{i}")
    g["w_out"] = _matmul_tn_pair(s["ya"], s["yb"], dx1b, name=f"grad_w_out_{i}").reshape(N_DEV, 2 * D_MODEL // N_DEV, D_MODEL)
    if between is not None:
        after = between(g)

    (du, gg, dq, g_bdb, g_bcr, g_bci, g_ar8, g_ai8, g_d, g_bglu, g_sn) = _s5_bwd(
        dya, s["v"], s["u"], s["xr"], s["xi"], p["a_r"], p["a_i"], p["bdb"], p["bcr"], p["bci"], p["dsk"], p["wglu"],
        p["bglu"], p["sn"], after, name=f"s5_bwd_{i}")
    g["s5_w_glu"] = _matmul_tn(gg, dq, name=f"grad_w_glu_{i}").reshape(N_DEV, D_MODEL // N_DEV, D_MODEL)
    g["s5_d"], g["s5_b_glu"], g["s5_norm"] = g_d.reshape(D_MODEL), g_bglu.reshape(D_MODEL), g_sn.reshape(D_MODEL)
    g["s5_c_re"], g["s5_c_im"] = _block_diag_c_inv(g_bcr), _block_diag_c_inv(g_bci)
    sq = (SUBLANE, S5_GROUPS, S5_STATE)
    g_lr, g_li, g_ls, g_br, g_bi = _s5_discretize_bwd(
        p["lam_re"], p["lam_im"], p["log_step"], p["b_re"], p["b_im"], g_ar8.reshape(sq), g_ai8.reshape(sq),
        _block_diag_b_inv(g_bdb[:, :, :S5_SLICE_STATES]), _block_diag_b_inv(g_bdb[:, :, S5_SLICE_STATES:]),
        name=f"s5_discretize_bwd_{i}")
    g["s5_lam_re"], g["s5_lam_im"], g["s5_log_step"] = g_lr, g_li, g_ls.reshape(S5_GROUPS)
    b_shape = (S5_GROUPS, S5_STATE, S5_GROUP)
    g["s5_b_re"], g["s5_b_im"] = g_br.reshape(b_shape), g_bi.reshape(b_shape)

    dxbc, dz, ddt, g_cw, g_cb, g_dtb, g_alog, g_dvec, g_gn = _ssd_bwd(
        dyb, s["xbc"], s["z"], s["dt"], s["stin"], p["conv_w"], p["conv_b"], p["dtb"], p["alog"], p["dvec"], p["gn"],
        name=f"ssd_bwd_{i}")
    g["ssd_conv_w"] = jnp.moveaxis(g_cw.reshape(SSD_CONV, N_DEV, SSD_CONV_DIM // N_DEV), 1, 0)
    g["ssd_conv_b"] = g_cb.reshape(SSD_CONV_DIM)
    g["ssd_dt_bias"], g["ssd_a_log"], g["ssd_d"] = g_dtb[0, :SSD_HEADS], g_alog[0, :SSD_HEADS], g_dvec[0, :SSD_HEADS]
    g["ssd_norm"] = g_gn.reshape(D_MODEL)

    dx0, h, g_nm = _inproj_bwd(s["x0"], p["nm"], du, dz, dxbc, ddt, dx1, p["wu"], p["wz"], p["wx"], p["wd"],
                               name=f"inproj_bwd_{i}")
    g["norm_mix"] = g_nm.reshape(D_MODEL)
    g["w_in"] = _w_in_grad_blocks(
        _matmul_tn(h, du, name=f"grad_w_in_u_{i}"), _matmul_tn(h, dz, name=f"grad_w_in_z_{i}"),
        _matmul_tn(h, dxbc, name=f"grad_w_in_xbc_{i}"), _matmul_tn(h, ddt, name=f"grad_w_in_dt_{i}"),
        name=f"grad_w_in_blocks_{i}")
    return dx0, g


def _example_step(x, target, w, blks):
    prepared = [_prepare_layer(w, blks[i], i, x) for i in range(DEPTH)]
    saved = []
    h = x
    for i in range(DEPTH):
        h, s = _layer_fwd(h, prepared[i], i)
        saved.append(s)
    loss, dh, g_final = _loss_head(h, w["norm_final"].reshape(1, D_MODEL), target, name="loss_head")
    layer_grads = [None] * DEPTH
    for i in reversed(range(DEPTH)):
        dh, layer_grads[i] = _layer_bwd(dh, saved[i], prepared[i], i, x)
    return loss, dh, layer_grads, g_final.reshape(D_MODEL)


def _mesh_position():
    return lax.axis_index("x"), lax.axis_index("y"), lax.axis_index("c")


def _peer(pos, k):
    x, y, c = pos
    px = 1 - x if k & 4 else x
    py = 1 - y if k & 2 else y
    pc = 1 - c if k & 1 else c
    return (px, py, pc), 4 * px + 2 * py + pc


HBM = pl.BlockSpec(memory_space=pl.ANY)


def _run_copies(local, remote):
    for cp in local + remote:
        cp.start()
    for cp in remote:
        cp.wait_recv()
    for cp in remote:
        cp.wait_send()
    for cp in local:
        cp.wait()


def _comm_scratch(n_units):
    return [pltpu.SemaphoreType.DMA((n_units, N_DEV - 1)), pltpu.SemaphoreType.DMA((n_units, N_DEV - 1)),
            pltpu.SemaphoreType.DMA((n_units,))]


def _gather_blocks(arrays, layered, name):
    units, out_shapes = [], []
    for j, (a, lay) in enumerate(zip(arrays, layered)):
        for layer in (range(a.shape[0]) if lay else (None,)):
            units.append((j, layer, len(out_shapes)))
            out_shapes.append(_sds((N_DEV,) + (a.shape[1:] if lay else a.shape), a.dtype))
    n_in = len(arrays)
    other_chips = (4, 2, 6)

    def body(*refs):
        ins, outs = refs[:n_in], refs[n_in:n_in + len(out_shapes)]
        send_sems, recv_sems, local_sems = refs[n_in + len(out_shapes):]
        pos = _mesh_position()
        me = 4 * pos[0] + 2 * pos[1] + pos[2]
        sibling, _ = _peer(pos, 1)
        local, own, passed = [], [], []
        for u, (j, layer, o) in enumerate(units):
            src = ins[j] if layer is None else ins[j].at[layer]
            local.append(pltpu.make_async_copy(src, outs[o].at[me], local_sems.at[u]))

            def copy(sem, src_ref, slot, to, u=u, o=o):
                return pltpu.make_async_remote_copy(
                    src_ref=src_ref, dst_ref=outs[o].at[slot], send_sem=send_sems.at[u, sem], recv_sem=recv_sems.at[u, sem],
                    device_id=to, device_id_type=MESH_ID)

            own.append([copy(0, src, me, sibling)] + [copy(1 + i, src, me, _peer(pos, k)[0]) for i, k in enumerate(other_chips)])
            passed.append([copy(4 + i, outs[o].at[_peer(pos, k)[1]], _peer(pos, k)[1], sibling) for i, k in enumerate(other_chips)])
        for cp in local + [c for unit in own for c in unit]:
            cp.start()
        for u in range(len(units)):
            for i in range(len(other_chips)):
                own[u][1 + i].wait_recv()
                passed[u][i].start()
        for u in range(len(units)):
            own[u][0].wait_recv()
            for cp in passed[u]:
                cp.wait_recv()
        for cp in [c for unit in own + passed for c in unit]:
            cp.wait_send()
        for cp in local:
            cp.wait()

    outs = pl.pallas_call(body, name=name, in_specs=[HBM] * n_in, out_specs=[HBM] * len(out_shapes), out_shape=out_shapes,
                          scratch_shapes=_comm_scratch(len(units)))(*arrays)
    grouped = [[] for _ in arrays]
    for j, _, o in units:
        grouped[j].append(outs[o])
    return [tuple(g) for g in grouped]


def _exchange_blocks(entries, name):
    units, flat_in, out_shapes = [], [], []
    for j, entry in enumerate(entries):
        for layer, a in enumerate(entry):
            units.append((len(flat_in), layer, j))
            flat_in.append(a)
        out_shapes.append(_sds((N_DEV, len(entry)) + entry[0].shape[1:], entry[0].dtype))
    n_in = len(flat_in)

    def body(*refs):
        ins, outs = refs[:n_in], refs[n_in:n_in + len(out_shapes)]
        send_sems, recv_sems, local_sems = refs[n_in + len(out_shapes):]
        pos = _mesh_position()
        me = 4 * pos[0] + 2 * pos[1] + pos[2]
        local, remote = [], []
        for u, (i, layer, o) in enumerate(units):
            local.append(pltpu.make_async_copy(ins[i].at[me], outs[o].at[me, layer], local_sems.at[u]))
            for k in range(1, N_DEV):
                peer, peer_index = _peer(pos, k)
                remote.append(pltpu.make_async_remote_copy(
                    src_ref=ins[i].at[peer_index], dst_ref=outs[o].at[me, layer], send_sem=send_sems.at[u, k - 1],
                    recv_sem=recv_sems.at[u, k - 1], device_id=peer, device_id_type=MESH_ID))
        _run_copies(local, remote)

    return pl.pallas_call(body, name=name, in_specs=[HBM] * n_in, out_specs=[HBM] * len(out_shapes), out_shape=out_shapes,
                          scratch_shapes=_comm_scratch(len(units)))(*flat_in)


SEM = pl.BlockSpec(memory_space=pltpu.SEMAPHORE)
SIDE_EFFECT = pltpu.SideEffectType.DATAFLOW_SIDE_EFFECTING


def _own_slots(arrays, indexed, me, name):
    lands = []
    for u, a in enumerate(arrays):
        block = a.shape[1:] if indexed else a.shape
        rows, cols = _size(block[:-1]), block[-1]
        tr = _row_tile(rows, cap=512)

        def body(me_ref, src_ref, out_ref):
            out_ref[...] = src_ref[...]

        src_spec = (pl.BlockSpec((None, tr, cols), lambda i, me_ref: (me_ref[0], i, 0)) if indexed
                    else pl.BlockSpec((tr, cols), lambda i, me_ref: (i, 0)))
        land = pl.pallas_call(
            body, name=f"{name}_{u}", out_shape=_sds((N_DEV, rows, cols), a.dtype),
            grid_spec=pltpu.PrefetchScalarGridSpec(
                num_scalar_prefetch=1, grid=(rows // tr,), in_specs=[src_spec],
                out_specs=pl.BlockSpec((None, tr, cols), lambda i, me_ref: (me_ref[0], i, 0))),
        )(me, a.reshape((N_DEV, rows, cols) if indexed else (rows, cols)))
        lands.append(land.reshape((N_DEV,) + block))
    return lands


def _split_copies(srcs, lands, send_sems, recv_sems, indexed):
    pos = _mesh_position()
    me = 4 * pos[0] + 2 * pos[1] + pos[2]
    copies = []
    for u, (src, land) in enumerate(zip(srcs, lands)):
        for k in range(1, N_DEV):
            peer, peer_index = _peer(pos, k)
            copies.append(pltpu.make_async_remote_copy(
                src_ref=src.at[peer_index] if indexed else src, dst_ref=land.at[me],
                send_sem=send_sems.at[u * (N_DEV - 1) + k - 1], recv_sem=recv_sems.at[u * (N_DEV - 1) + k - 1],
                device_id=peer, device_id_type=MESH_ID))
    return copies


def _exchange_start(arrays, lands, indexed, name):
    n = len(arrays)

    def body(*refs):
        srcs, zones = refs[:n], refs[n:2 * n]
        send_sems, recv_sems = refs[2 * n], refs[2 * n + 1]
        token = refs[-1]
        for cp in _split_copies(srcs, zones, send_sems, recv_sems, indexed):
            cp.start()
        token[...] = jnp.zeros_like(token)

    sem_shape = pltpu.SemaphoreType.DMA((n * (N_DEV - 1),))
    outs = pl.pallas_call(
        body, name=name, in_specs=[HBM] * (2 * n),
        out_specs=[SEM, SEM] + [HBM] * (2 * n) + [pl.BlockSpec(memory_space=pltpu.VMEM)],
        out_shape=[sem_shape, sem_shape] + [pltpu.HBM(a.shape, a.dtype) for a in list(arrays) + list(lands)]
        + [_sds((SUBLANE, LANE))],
        input_output_aliases={i: 2 + i for i in range(2 * n)},
        compiler_params=pltpu.CompilerParams(has_side_effects=SIDE_EFFECT),
    )(*[pltpu.with_memory_space_constraint(a, pltpu.HBM) for a in list(arrays) + list(lands)])
    return outs[0], outs[1], outs[2:2 + n], outs[2 + n:2 + 2 * n], outs[-1]


def _exchange_wait(send_sems, recv_sems, arrays, lands, after, indexed, name):
    n = len(arrays)

    def body(*refs):
        srcs, zones = refs[:n], refs[n:2 * n]
        s_sems, r_sems = refs[2 * n], refs[2 * n + 1]
        for cp in _split_copies(srcs, zones, s_sems, r_sems, indexed):
            cp.wait_send()
            cp.wait_recv()

    outs = pl.pallas_call(
        body, name=name, in_specs=[HBM] * (2 * n) + [SEM, SEM, HBM],
        out_specs=[HBM] * (2 * n), out_shape=[pltpu.HBM(a.shape, a.dtype) for a in list(arrays) + list(lands)],
        input_output_aliases={i: i for i in range(2 * n)},
        compiler_params=pltpu.CompilerParams(has_side_effects=SIDE_EFFECT),
    )(*arrays, *lands, send_sems, recv_sems, after)
    return outs[n:]


SUM_TILE = 512


def _adamw(w, g, m, v):
    m = ADAM_B1 * m + (1.0 - ADAM_B1) * g
    v = ADAM_B2 * v + (1.0 - ADAM_B2) * (g * g)
    m_hat = m / (1.0 - ADAM_B1 ** ADAM_STEP)
    v_hat = v / (1.0 - ADAM_B2 ** ADAM_STEP)
    return -ADAM_LR * (m_hat / (jnp.sqrt(v_hat) + ADAM_EPS) + ADAM_WD * w), m, v


def _sum_adamw(recv, w, m, v, layer, others, name):
    _, rows, cols = w.shape
    tr = _row_tile(rows, cap=256)

    def body(r_ref, w_ref, m_ref, v_ref, *rest):
        g_ref, d_ref, mo_ref, vo_ref = rest[-4:]
        g = r_ref[0].astype(F32)
        for j in range(1, N_DEV):
            g = g + r_ref[j].astype(F32)
        g_ref[...] = g
        d_ref[...], mo_ref[...], vo_ref[...] = _adamw(w_ref[...], g, m_ref[...], v_ref[...])

    blk = pl.BlockSpec((None, tr, cols), lambda i: (layer, i, 0))
    carried = list(others) if others is not None else []
    return pl.pallas_call(
        body, name=name, grid=(rows // tr,),
        in_specs=[pl.BlockSpec((N_DEV, tr, cols), lambda i: (0, i, 0)), blk, blk, blk] + [HBM] * len(carried),
        out_specs=[blk] * 4, out_shape=[_sds(w.shape)] * 4,
        input_output_aliases={4 + k: k for k in range(len(carried))},
        compiler_params=pltpu.CompilerParams(dimension_semantics=("arbitrary",), vmem_limit_bytes=VMEM_LIMIT),
    )(recv, w, m, v, *carried)


def _sum_senders(recv, name):
    _, rows, cols = recv.shape
    tr = _row_tile(rows, cap=256)

    def body(r_ref, g_ref):
        g = r_ref[0].astype(F32)
        for j in range(1, N_DEV):
            g = g + r_ref[j].astype(F32)
        g_ref[...] = g

    return _call(body, name, (rows // tr,), [pl.BlockSpec((N_DEV, tr, cols), lambda i: (0, i, 0))], [_rows(tr, cols)],
                 [_sds((rows, cols))])(recv)[0]


def _adamw_blocks(g, w, m, v, name):
    n_lay, rows, cols = w.shape
    tr = _row_tile(rows, cap=256)

    def body(g_ref, w_ref, m_ref, v_ref, d_ref, mo_ref, vo_ref):
        d_ref[...], mo_ref[...], vo_ref[...] = _adamw(w_ref[...], g_ref[...], m_ref[...], v_ref[...])

    blk = pl.BlockSpec((None, tr, cols), lambda l, i: (l, i, 0))
    return pl.pallas_call(
        body, name=name, grid=(n_lay, rows // tr), in_specs=[blk] * 4, out_specs=[blk] * 3, out_shape=[_sds(w.shape)] * 3,
        compiler_params=pltpu.CompilerParams(dimension_semantics=("arbitrary", "arbitrary"), vmem_limit_bytes=VMEM_LIMIT),
    )(g, w, m, v)


def _sum_slots(recv, name):
    rows = recv.shape[1]

    def body(r_ref, g_ref):
        g = r_ref[0].astype(F32)
        for j in range(1, N_DEV):
            g = g + r_ref[j].astype(F32)
        g_ref[...] = g

    return _call(body, name, (1,), [_full(recv.shape)], [_full((rows, LANE))], [_sds((rows, LANE))])(recv)[0]


def _adamw_rows(g, w, m, v, name):
    rows, cols = w.shape
    tr = _row_tile(rows)

    def body(g_ref, w_ref, m_ref, v_ref, d_ref, mo_ref, vo_ref):
        d_ref[...], mo_ref[...], vo_ref[...] = _adamw(w_ref[...], g_ref[...], m_ref[...], v_ref[...])

    blk = _rows(tr, cols)
    return _call(body, name, (rows // tr,), [blk] * 4, [blk] * 3, [_sds((rows, cols))] * 3)(g, w, m, v)


def _row_tile(rows, cap=1024):
    if rows % SUBLANE:
        return rows
    best = SUBLANE
    for t in range(SUBLANE, cap + 1, SUBLANE):
        if rows % t == 0:
            best = t
    return best


BIG = (("w_in", (DEPTH, D_MODEL, IN_PROJ // N_DEV), 2),
       ("s5_w_glu", (DEPTH, D_MODEL // N_DEV, D_MODEL), 1),
       ("ssd_conv_w", (DEPTH, SSD_CONV, SSD_CONV_DIM // N_DEV), 2),
       ("w_out", (DEPTH, 2 * D_MODEL // N_DEV, D_MODEL), 1),
       ("w_gate", (DEPTH, D_MODEL, FFN_HIDDEN // N_DEV), 2),
       ("w_up", (DEPTH, D_MODEL, FFN_HIDDEN // N_DEV), 2),
       ("w_down", (DEPTH, FFN_HIDDEN // N_DEV, D_MODEL), 1))
SMALL = (("norm_mix", (DEPTH, D_MODEL)), ("s5_lam_re", (DEPTH, S5_GROUPS, S5_STATE)), ("s5_lam_im", (DEPTH, S5_GROUPS, S5_STATE)),
         ("s5_log_step", (DEPTH, S5_GROUPS)), ("s5_b_re", (DEPTH, S5_GROUPS, S5_STATE, S5_GROUP)),
         ("s5_b_im", (DEPTH, S5_GROUPS, S5_STATE, S5_GROUP)), ("s5_c_re", (DEPTH, S5_GROUPS, S5_GROUP, S5_STATE)),
         ("s5_c_im", (DEPTH, S5_GROUPS, S5_GROUP, S5_STATE)), ("s5_d", (DEPTH, D_MODEL)), ("s5_b_glu", (DEPTH, D_MODEL)),
         ("s5_norm", (DEPTH, D_MODEL)), ("ssd_conv_b", (DEPTH, SSD_CONV_DIM)), ("ssd_dt_bias", (DEPTH, SSD_HEADS)),
         ("ssd_a_log", (DEPTH, SSD_HEADS)), ("ssd_d", (DEPTH, SSD_HEADS)), ("ssd_norm", (DEPTH, D_MODEL)),
         ("norm_ffn", (DEPTH, D_MODEL)), ("norm_final", (D_MODEL,)))
WEIGHT_ORDER = ("norm_mix", "w_in", "s5_lam_re", "s5_lam_im", "s5_log_step", "s5_b_re", "s5_b_im", "s5_c_re", "s5_c_im", "s5_d",
                "s5_w_glu", "s5_b_glu", "s5_norm", "ssd_conv_w", "ssd_conv_b", "ssd_dt_bias", "ssd_a_log", "ssd_d", "ssd_norm",
                "w_out", "norm_ffn", "w_gate", "w_up", "w_down", "norm_final")


def _size(shape):
    n = 1
    for s in shape:
        n *= s
    return n


def _round_up(n, m):
    return -(-n // m) * m


SMALL_SIZE = sum(_size(s) for _, s in SMALL)
SMALL_ROWS = _round_up(-(-SMALL_SIZE // (N_DEV * LANE)), SUBLANE)


def _pack(parts, rows, dtype):
    flat = jnp.concatenate([p.reshape(-1).astype(dtype) for p in parts])
    return jnp.pad(flat, (0, rows * LANE - flat.shape[0])).reshape(rows, LANE)


def _unpack(flat, specs):
    out, off = {}, 0
    flat = flat.reshape(-1)
    for name, shape in specs:
        out[name] = flat[off:off + _size(shape)].reshape(shape)
        off += _size(shape)
    return out


def kernel(x, norm_mix, w_in, s5_lam_re, s5_lam_im, s5_log_step, s5_b_re, s5_b_im, s5_c_re, s5_c_im, s5_d, s5_w_glu, s5_b_glu, s5_norm, ssd_conv_w, ssd_conv_b, ssd_dt_bias, ssd_a_log, ssd_d, ssd_norm, w_out, norm_ffn, w_gate, w_up, w_down, norm_final, loss_target, m_norm_mix, m_w_in, m_s5_lam_re, m_s5_lam_im, m_s5_log_step, m_s5_b_re, m_s5_b_im, m_s5_c_re, m_s5_c_im, m_s5_d, m_s5_w_glu, m_s5_b_glu, m_s5_norm, m_ssd_conv_w, m_ssd_conv_b, m_ssd_dt_bias, m_ssd_a_log, m_ssd_d, m_ssd_norm, m_w_out, m_norm_ffn, m_w_gate, m_w_up, m_w_down, m_norm_final, v_norm_mix, v_w_in, v_s5_lam_re, v_s5_lam_im, v_s5_log_step, v_s5_b_re, v_s5_b_im, v_s5_c_re, v_s5_c_im, v_s5_d, v_s5_w_glu, v_s5_b_glu, v_s5_norm, v_ssd_conv_w, v_ssd_conv_b, v_ssd_dt_bias, v_ssd_a_log, v_ssd_d, v_ssd_norm, v_w_out, v_norm_ffn, v_w_gate, v_w_up, v_w_down, v_norm_final):
    given = dict(locals())
    w = {n: given[n] for n in WEIGHT_ORDER}
    m = {n: given["m_" + n] for n in WEIGHT_ORDER}
    v = {n: given["v_" + n] for n in WEIGHT_ORDER}
    big_names = tuple(n for n, _, _ in BIG)
    matmul_names = tuple(n for n in big_names if n != "ssd_conv_w")

    conv_hi = w["ssd_conv_w"].astype(BF16)
    conv_lo = (w["ssd_conv_w"] - conv_hi.astype(F32)).astype(BF16)
    row_pad = ((0, 0), (0, FFN_BLOCK_PAD - FFN_BLOCK), (0, 0))
    as_rows = {"w_gate": jnp.swapaxes(w["w_gate"], 1, 2), "w_up": jnp.swapaxes(w["w_up"], 1, 2), "w_down": w["w_down"]}
    to_send = [jnp.pad(as_rows[n].astype(BF16), row_pad) if n in as_rows else w[n].astype(BF16) for n in matmul_names]

    def layer_blocks(i):
        return [a[i] for a in to_send] + [jnp.stack([conv_hi[i], conv_lo[i]])]

    def as_layer_weights(gathered):
        blk = dict(zip(matmul_names, gathered))
        pair = gathered[-1].astype(F32)
        blk["ssd_conv_w"] = jnp.moveaxis(pair[:, 0] + pair[:, 1], 0, 1).reshape(SSD_CONV, SSD_CONV_DIM)
        return blk

    gathered0 = [g[0] for g in _gather_blocks(layer_blocks(0), [False] * (len(matmul_names) + 1), name="gather_weights_0")]
    blocks1 = layer_blocks(1)
    me = (4 * lax.axis_index("x") + 2 * lax.axis_index("y") + lax.axis_index("c")).astype(jnp.int32).reshape(1)
    sems1 = _exchange_start(blocks1, _own_slots(blocks1, False, me, name="gather_own_1"), False, name="gather_start_1")
    prepared = [_prepare_layer(w, as_layer_weights(gathered0), 0, sems1[-1]), None]
    saved = [None, None]
    h, saved[0] = _layer_fwd(x[0], prepared[0], 0)
    gathered1 = _exchange_wait(*sems1[:4], h, False, name="gather_wait_1")
    prepared[1] = _prepare_layer(w, as_layer_weights(gathered1), 1, sems1[-1])
    h, saved[1] = _layer_fwd(h, prepared[1], 1)
    loss, dh, g_final = _loss_head(h, w["norm_final"].reshape(1, D_MODEL), loss_target[0], name="loss_head")

    layer_grads = [None, None]
    dh, layer_grads[1] = _layer_bwd(dh, saved[1], prepared[1], 1, sems1[-1])
    slots1 = [layer_grads[1][n] for n in big_names]
    sems2 = _exchange_start(slots1, _own_slots(slots1, True, me, name="exchange_own_1"), True, name="exchange_start_1")
    early_names = ("w_out", "w_gate", "w_up", "w_down")
    late_names = tuple(n for n in big_names if n not in early_names)
    early = {}

    def send_early(g):
        slots = [g[n] for n in early_names]
        early["sems"] = _exchange_start(slots, _own_slots(slots, True, me, name="exchange_own_0"), True, name="exchange_start_0")
        return early["sems"][-1]

    grad_x, layer_grads[0] = _layer_bwd(dh, saved[0], prepared[0], 0, sems2[-1], between=send_early)

    small = jnp.concatenate([g_final.reshape(-1) if n == "norm_final"
                             else jnp.stack([layer_grads[i][n] for i in range(DEPTH)]).reshape(-1) for n, _ in SMALL])
    small_slots = jnp.pad(small, (0, N_DEV * SMALL_ROWS * LANE - small.shape[0])).reshape(N_DEV, SMALL_ROWS, LANE)
    late = [layer_grads[0][n] for n in late_names] + [small_slots]
    sems3 = _exchange_start(late, _own_slots(late, True, me, name="exchange_own_late"), True, name="exchange_start_late")
    received1 = _exchange_wait(*sems2[:4], sems3[-1], True, name="exchange_wait_1")
    received_early = _exchange_wait(*early["sems"][:4], sems3[-1], True, name="exchange_wait_0")
    received0 = dict(zip(early_names, received_early))

    transposed = ("w_gate", "w_up")
    layer1 = {n: _sum_adamw(received1[j], w[n], m[n], v[n], 1, None, name=f"sum_adamw_{n}_1")
              for j, n in enumerate(big_names) if n not in transposed}
    results = {}
    for n in transposed:
        recv = (received0[n], received1[big_names.index(n)])
        g = jnp.stack([jnp.swapaxes(_sum_senders(recv[i], name=f"sum_{n}_{i}"), 0, 1) for i in range(DEPTH)])
        results[n] = [g, *_adamw_blocks(g, w[n], m[n], v[n], name=f"adamw_{n}")]
    for n in early_names:
        if n not in transposed:
            results[n] = _sum_adamw(received0[n], w[n], m[n], v[n], 0, layer1[n], name=f"sum_adamw_{n}_0")
    received_late = _exchange_wait(*sems3[:4], results["w_down"][0], True, name="exchange_wait_late")
    for n, recv in zip(late_names, received_late):
        results[n] = _sum_adamw(recv, w[n], m[n], v[n], 0, layer1[n], name=f"sum_adamw_{n}_0")
    g_part = _sum_slots(received_late[-1], name="sum_replicated")
    g_small = _gather_blocks([g_part], [False], name="gather_replicated")[0][0]
    for n, g in _unpack(g_small, SMALL).items():
        as_rows = (-1, g.shape[-1])
        d_n, m_n, v_n = _adamw_rows(g.reshape(as_rows), w[n].reshape(as_rows), m[n].reshape(as_rows), v[n].reshape(as_rows),
                                    name=f"adamw_{n}")
        results[n] = [g, d_n.reshape(g.shape), m_n.reshape(g.shape), v_n.reshape(g.shape)]

    outs = [results[n][k] for k in range(4) for n in WEIGHT_ORDER]
    total_loss = lax.psum(loss[0, 0], ("x", "y", "c"))
    return (total_loss, grad_x[None], *outs)
```

```python
import functools

import jax
import jax.numpy as jnp
from jax import lax
from jax.experimental import pallas as pl
from jax.experimental.pallas import tpu as pltpu

F32 = jnp.float32
BF16 = jnp.bfloat16
MESH_ID = pl.DeviceIdType.MESH

N_DEV = 8
DEPTH = 2
D_MODEL = 1024
S5_GROUPS = 64
S5_GROUP = 16
S5_STATE = 64
S5_LANES = S5_GROUPS * S5_STATE
SSD_HEADS = 16
SSD_HEAD_DIM = 64
SSD_STATE = 128
SSD_CHUNK = 128
SSD_CONV = 4
SSD_CONV_DIM = 1536
FFN_HIDDEN = 2816
IN_PROJ = 3600
EPS = 1e-6
LANE = 128
SUBLANE = 8
VMEM_LIMIT = 56 * 1024 * 1024

ADAM_LR = 0.001
ADAM_B1 = 0.9
ADAM_B2 = 0.999
ADAM_EPS = 1e-08
ADAM_WD = 0.01
ADAM_STEP = 10

TOK_TILE = 256
S5_TILE = 128


def _sigmoid(x):
    return jax.nn.sigmoid(x)


def _silu(x):
    return x * _sigmoid(x)


def _gelu(x):
    return 0.5 * x * (1.0 + jnp.tanh(0.7978845608028654 * (x + 0.044715 * (x * x * x))))


def _softplus(x):
    return jnp.maximum(x, 0.0) + jnp.log(1.0 + jnp.exp(-jnp.abs(x)))


def _rms(x, g):
    r = lax.rsqrt(jnp.mean(x * x, axis=-1, keepdims=True) + EPS)
    return x * r * g


def _nn(a, b):
    return lax.dot_general(a.astype(BF16), b.astype(BF16), (((1,), (0,)), ((), ())), preferred_element_type=F32)


def _nt(a, b):
    return lax.dot_general(a.astype(BF16), b.astype(BF16), (((1,), (1,)), ((), ())), preferred_element_type=F32)


def _tn(a, b):
    return lax.dot_general(a.astype(BF16), b.astype(BF16), (((0,), (0,)), ((), ())), preferred_element_type=F32)


def _nn_f32(a, b):
    return lax.dot_general(a, b, (((1,), (0,)), ((), ())), precision=lax.Precision.HIGHEST, preferred_element_type=F32)


def _tn_f32(a, b):
    return lax.dot_general(a, b, (((0,), (0,)), ((), ())), precision=lax.Precision.HIGHEST, preferred_element_type=F32)


@jax.custom_vjp
def _nn_d(a, b):
    return _nn(a, b)


_nn_d.defvjp(lambda a, b: (_nn(a, b), (a, b)), lambda r, g: (_nt(g, r[1]), _tn(r[0], g)))


@jax.custom_vjp
def _nt_d(a, b):
    return _nt(a, b)


_nt_d.defvjp(lambda a, b: (_nt(a, b), (a, b)), lambda r, g: (_nn(g, r[1]), _tn(g, r[0])))


@jax.custom_vjp
def _tn_d(a, b):
    return _tn(a, b)


_tn_d.defvjp(lambda a, b: (_tn(a, b), (a, b)), lambda r, g: (_nt(r[1], g), _nn(r[0], g)))


@jax.custom_vjp
def _cumsum_rows(tri, x):
    return _nn_f32(tri, x)


_cumsum_rows.defvjp(lambda tri, x: (_nn_f32(tri, x), tri), lambda tri, g: (jnp.zeros_like(tri), _tn_f32(tri, g)))


def _full(shape):
    zeros = (0,) * len(shape)
    return pl.BlockSpec(shape, lambda *_: zeros)


def _const(shape):
    zeros = (0,) * len(shape)
    return pl.BlockSpec(shape, lambda *_: zeros, pipeline_mode=pl.Buffered(1))


def _rows(tile, width, n_tiles=None):
    if n_tiles is None:
        return pl.BlockSpec((tile, width), lambda i: (i, 0))
    return pl.BlockSpec((tile, width), lambda i: (n_tiles - 1 - i, 0))


def _call(body, name, grid, in_specs, out_specs, out_shape, scratch=()):
    return pl.pallas_call(
        body, name=name, grid=grid, in_specs=in_specs, out_specs=out_specs, out_shape=out_shape,
        scratch_shapes=list(scratch),
        compiler_params=pltpu.CompilerParams(dimension_semantics=("arbitrary",) * len(grid),
                                             vmem_limit_bytes=VMEM_LIMIT))


def _sds(shape, dtype=F32):
    return jax.ShapeDtypeStruct(shape, dtype)


def _tile_of(n, cap=512):
    if n <= LANE:
        return n
    best = LANE
    for t in range(LANE, cap + 1, LANE):
        if n % t == 0:
            best = t
    return best


BIG_TILE = 512


def _big_tile(n_tok):
    return TOK_TILE


def _inproj_fwd(x, nm, wu, wz, wx, wd, name):
    n_tok = x.shape[0]
    tm = _big_tile(n_tok)

    def body(x_ref, nm_ref, wu_ref, wz_ref, wx_ref, wd_ref, u_ref, z_ref, xbc_ref, dt_ref):
        h = _rms(x_ref[...], nm_ref[...]).astype(BF16)
        u_ref[...] = _nn(h, wu_ref[...])
        z_ref[...] = _nn(h, wz_ref[...])
        xbc_ref[...] = _nn(h, wx_ref[...])
        dt_ref[...] = _nn(h, wd_ref[...])

    return _call(
        body, name, (n_tok // tm,),
        [_rows(tm, D_MODEL), _const((1, D_MODEL)), _const(wu.shape), _const(wz.shape), _const(wx.shape), _const(wd.shape)],
        [_rows(tm, D_MODEL), _rows(tm, D_MODEL), _rows(tm, SSD_CONV_DIM), _rows(tm, LANE)],
        [_sds((n_tok, D_MODEL)), _sds((n_tok, D_MODEL)), _sds((n_tok, SSD_CONV_DIM)), _sds((n_tok, LANE))],
    )(x, nm, wu, wz, wx, wd)


def _inproj_bwd(x, nm, du, dz, dxbc, ddt, dres, wu, wz, wx, wd, name):
    n_tok = x.shape[0]
    tm = _big_tile(n_tok)

    def body(x_ref, nm_ref, du_ref, dz_ref, dxbc_ref, ddt_ref, dres_ref, wu_ref, wz_ref, wx_ref, wd_ref,
             dx_ref, h_ref, dnm_ref):
        dh = (_nt(du_ref[...], wu_ref[...]) + _nt(dz_ref[...], wz_ref[...])
              + _nt(dxbc_ref[...], wx_ref[...]) + _nt(ddt_ref[...], wd_ref[...]))
        h, vjp = jax.vjp(_rms, x_ref[...], nm_ref[...])
        dx, dnm = vjp(dh)
        dx_ref[...] = dres_ref[...] + dx
        h_ref[...] = h.astype(BF16)

        @pl.when(pl.program_id(0) == 0)
        def _():
            dnm_ref[...] = jnp.zeros_like(dnm_ref)

        dnm_ref[...] += dnm

    return _call(
        body, name, (n_tok // tm,),
        [_rows(tm, D_MODEL), _const((1, D_MODEL)), _rows(tm, D_MODEL), _rows(tm, D_MODEL), _rows(tm, SSD_CONV_DIM),
         _rows(tm, LANE), _rows(tm, D_MODEL), _const(wu.shape), _const(wz.shape), _const(wx.shape), _const(wd.shape)],
        [_rows(tm, D_MODEL), _rows(tm, D_MODEL), _full((1, D_MODEL))],
        [_sds((n_tok, D_MODEL)), _sds((n_tok, D_MODEL), BF16), _sds((1, D_MODEL))],
    )(x, nm, du, dz, dxbc, ddt, dres, wu, wz, wx, wd)


def _ffn_act(gt, up):
    return _silu(gt) * up


FFN_BLOCK = FFN_HIDDEN // N_DEV
FFN_BLOCK_PAD = -(-FFN_BLOCK // LANE) * LANE


FFN_PAD = N_DEV * FFN_BLOCK_PAD


def _mix_ffn_fwd(x0, ya, yb, wo, nf, wg, wu, wd, name):
    n_tok = x0.shape[0]
    tm = TOK_TILE

    def body(x0_ref, ya_ref, yb_ref, wo_ref, nf_ref, wg_ref, wu_ref, wd_ref, x1_ref, x2_ref):
        x1 = x0_ref[...] + _nn(ya_ref[...], wo_ref[:D_MODEL, :]) + _nn(yb_ref[...], wo_ref[D_MODEL:, :])
        h = _rms(x1, nf_ref[...]).astype(BF16)
        x1_ref[...] = x1
        x2_ref[...] = x1 + _nn(_ffn_act(_nt(h, wg_ref[...]), _nt(h, wu_ref[...])), wd_ref[...])

    return _call(
        body, name, (n_tok // tm,),
        [_rows(tm, D_MODEL), _rows(tm, D_MODEL), _rows(tm, D_MODEL), _const(wo.shape),
         _const((1, D_MODEL)), _const(wg.shape), _const(wu.shape), _const(wd.shape)],
        [_rows(tm, D_MODEL), _rows(tm, D_MODEL)],
        [_sds((n_tok, D_MODEL)), _sds((n_tok, D_MODEL))],
    )(x0, ya, yb, wo, nf, wg, wu, wd)


def _mix_ffn_bwd(x1, dx2, wo, nf, wg, wu, wd, after, name):
    n_tok = x1.shape[0]
    tm = TOK_TILE
    n_chunks = 3
    hc = FFN_PAD // n_chunks

    def body(x1_ref, dx2_ref, wo_ref, nf_ref, wg_ref, wu_ref, wd_ref, after_ref,
             dx1_ref, dya_ref, dyb_ref, h_ref, a_ref, dgt_ref, dup_ref, dx2b_ref, dx1b_ref, dnf_ref):
        dx2 = dx2_ref[...]
        dx2b = dx2.astype(BF16)
        dx2b_ref[...] = dx2b
        h, rms_vjp = jax.vjp(_rms, x1_ref[...], nf_ref[...])
        hb = h.astype(BF16)
        dh = jnp.zeros_like(h)
        for c in range(n_chunks):
            rows = pl.ds(c * hc, hc)
            a, act_vjp = jax.vjp(_ffn_act, _nt(hb, wg_ref[rows, :]), _nt(hb, wu_ref[rows, :]))
            dgt, dup = act_vjp(_nt(dx2b, wd_ref[rows, :]))
            a_ref[:, c * hc:(c + 1) * hc] = a.astype(BF16)
            dgt_ref[:, c * hc:(c + 1) * hc] = dgt.astype(BF16)
            dup_ref[:, c * hc:(c + 1) * hc] = dup.astype(BF16)
            dh = dh + _nn(dgt, wg_ref[rows, :]) + _nn(dup, wu_ref[rows, :])
        dx, dnf = rms_vjp(dh)
        dx1 = dx2 + dx
        dx1b = dx1.astype(BF16)
        dx1_ref[...] = dx1
        dx1b_ref[...] = dx1b
        dya_ref[...] = _nt(dx1b, wo_ref[:D_MODEL, :])
        dyb_ref[...] = _nt(dx1b, wo_ref[D_MODEL:, :])
        h_ref[...] = hb

        @pl.when(pl.program_id(0) == 0)
        def _():
            dnf_ref[...] = jnp.zeros_like(dnf_ref)

        dnf_ref[...] += dnf

    hidden = _rows(tm, FFN_PAD)
    return _call(
        body, name, (n_tok // tm,),
        [_rows(tm, D_MODEL), _rows(tm, D_MODEL), _const(wo.shape), _const((1, D_MODEL)),
         _const(wg.shape), _const(wu.shape), _const(wd.shape), HBM],
        [_rows(tm, D_MODEL), _rows(tm, D_MODEL), _rows(tm, D_MODEL), _rows(tm, D_MODEL), hidden, hidden, hidden,
         _rows(tm, D_MODEL), _rows(tm, D_MODEL), _full((1, D_MODEL))],
        [_sds((n_tok, D_MODEL)), _sds((n_tok, D_MODEL)), _sds((n_tok, D_MODEL)), _sds((n_tok, D_MODEL), BF16),
         _sds((n_tok, FFN_PAD), BF16), _sds((n_tok, FFN_PAD), BF16), _sds((n_tok, FFN_PAD), BF16),
         _sds((n_tok, D_MODEL), BF16), _sds((n_tok, D_MODEL), BF16), _sds((1, D_MODEL))],
    )(x1, dx2, wo, nf, wg, wu, wd, after)


def _loss_head(x, nf, target, name):
    n_tok = x.shape[0]
    tm = TOK_TILE

    def loss_of(xv, g, t):
        e = _rms(xv, g) - t
        return 0.5 * jnp.sum(jnp.sum(e * e, axis=-1, keepdims=True) * (1.0 / D_MODEL), axis=0, keepdims=True)

    def body(x_ref, nf_ref, t_ref, loss_ref, dx_ref, dnf_ref):
        loss, vjp = jax.vjp(functools.partial(loss_of, t=t_ref[...]), x_ref[...], nf_ref[...])
        dx, dnf = vjp(jnp.ones_like(loss))
        dx_ref[...] = dx

        @pl.when(pl.program_id(0) == 0)
        def _():
            dnf_ref[...] = jnp.zeros_like(dnf_ref)
            loss_ref[...] = jnp.zeros_like(loss_ref)

        dnf_ref[...] += dnf
        loss_ref[...] += jnp.broadcast_to(loss, loss_ref.shape)

    return _call(
        body, name, (n_tok // tm,),
        [_rows(tm, D_MODEL), _const((1, D_MODEL)), _rows(tm, D_MODEL)],
        [_full((SUBLANE, LANE)), _rows(tm, D_MODEL), _full((1, D_MODEL))],
        [_sds((SUBLANE, LANE)), _sds((n_tok, D_MODEL)), _sds((1, D_MODEL))],
    )(x, nf, target)


GRAD_WIRE = BF16


def _matmul_tn(a, b, name):
    n_tok, k1 = a.shape
    k2 = b.shape[1]
    t1 = _tile_of(k1)

    def body(a_ref, b_ref, o_ref):
        o_ref[...] = _tn(a_ref[...], b_ref[...]).astype(GRAD_WIRE)

    return _call(body, name, (k1 // t1,), [pl.BlockSpec((n_tok, t1), lambda i: (0, i)), _const((n_tok, k2))],
                 [pl.BlockSpec((t1, k2), lambda i: (i, 0))], [_sds((k1, k2), GRAD_WIRE)])(a, b)[0]


def _matmul_tn_lhs_blocks(a, b, width, keep, name):
    n_tok, k1 = a.shape
    k2 = b.shape[1]

    def body(a_ref, b_ref, o_ref):
        o_ref[...] = _tn(a_ref[...], b_ref[...])[:keep, :].astype(GRAD_WIRE)

    return _call(body, name, (k1 // width,), [pl.BlockSpec((n_tok, width), lambda d: (0, d)), _const((n_tok, k2))],
                 [pl.BlockSpec((None, keep, k2), lambda d: (d, 0, 0))], [_sds((k1 // width, keep, k2), GRAD_WIRE)])(a, b)[0]


def _matmul_tn_pair(a0, a1, b, name):
    n_tok, k1 = a0.shape
    k2 = b.shape[1]
    t1 = _tile_of(k1)
    n1 = k1 // t1

    def body(a0_ref, a1_ref, b_ref, o_ref):
        @pl.when(pl.program_id(0) < n1)
        def _():
            o_ref[...] = _tn(a0_ref[...], b_ref[...]).astype(GRAD_WIRE)

        @pl.when(pl.program_id(0) >= n1)
        def _():
            o_ref[...] = _tn(a1_ref[...], b_ref[...]).astype(GRAD_WIRE)

    return _call(
        body, name, (2 * n1,),
        [pl.BlockSpec((n_tok, t1), lambda i: (0, jnp.minimum(i, n1 - 1))),
         pl.BlockSpec((n_tok, t1), lambda i: (0, jnp.maximum(i - n1, 0))), _const((n_tok, k2))],
        [pl.BlockSpec((None, t1, k2), lambda i: (i // n1, i % n1, 0))], [_sds((2, k1, k2), GRAD_WIRE)])(a0, a1, b)[0]


W_IN_BLOCK = IN_PROJ // N_DEV
W_IN_SPLITS = (D_MODEL, 2 * D_MODEL, 2 * D_MODEL + SSD_CONV_DIM)
RELAYOUT_TILE = 256


def _w_in_split(blocks, after, name):
    tr = RELAYOUT_TILE

    def body(b_ref, after_ref, wu_ref, wz_ref, wx_ref, wd_ref):
        full = jnp.concatenate([b_ref[d] for d in range(N_DEV)], axis=1)
        wu_ref[...] = full[:, :W_IN_SPLITS[0]]
        wz_ref[...] = full[:, W_IN_SPLITS[0]:W_IN_SPLITS[1]]
        wx_ref[...] = full[:, W_IN_SPLITS[1]:W_IN_SPLITS[2]]
        wd_ref[...] = jnp.concatenate([full[:, W_IN_SPLITS[2]:], jnp.zeros((tr, LANE - SSD_HEADS), full.dtype)], axis=1)

    return _call(
        body, name, (D_MODEL // tr,), [pl.BlockSpec((N_DEV, tr, W_IN_BLOCK), lambda i: (0, i, 0)), HBM],
        [_rows(tr, D_MODEL), _rows(tr, D_MODEL), _rows(tr, SSD_CONV_DIM), _rows(tr, LANE)],
        [_sds((D_MODEL, D_MODEL), BF16), _sds((D_MODEL, D_MODEL), BF16), _sds((D_MODEL, SSD_CONV_DIM), BF16),
         _sds((D_MODEL, LANE), BF16)],
    )(blocks, after)


def _w_in_grad_blocks(gu, gz, gx, gdt, name):
    tr = RELAYOUT_TILE

    def body(gu_ref, gz_ref, gx_ref, gdt_ref, o_ref):
        full = jnp.concatenate([gu_ref[...], gz_ref[...], gx_ref[...], gdt_ref[...]], axis=1)
        for d in range(N_DEV):
            o_ref[d] = full[:, d * W_IN_BLOCK:(d + 1) * W_IN_BLOCK]

    return _call(
        body, name, (D_MODEL // tr,),
        [_rows(tr, D_MODEL), _rows(tr, D_MODEL), _rows(tr, SSD_CONV_DIM), _rows(tr, LANE)],
        [pl.BlockSpec((N_DEV, tr, W_IN_BLOCK), lambda i: (0, i, 0))], [_sds((N_DEV, D_MODEL, W_IN_BLOCK), gu.dtype)],
    )(gu, gz, gx, gdt)[0]


S5_SLICES = D_MODEL // LANE
S5_SLICE_STATES = S5_LANES // S5_SLICES
SCAN_LANES = 512


def _s5_scan(br_ref, bi_ref, a_r, a_i, car_r, car_i, ini_r, ini_i, reverse, xr_ref=None, xi_ref=None,
             acc_r=None, acc_i=None, row0=0):
    n_rows = S5_TILE
    seg = n_rows // SUBLANE
    order = range(SUBLANE - 1, -1, -1) if reverse else range(SUBLANE)

    def rows(t):
        return pl.ds(pl.multiple_of(row0 + ((seg - 1 - t) if reverse else t) * SUBLANE, SUBLANE), SUBLANE)

    tiles_per = SCAN_LANES // LANE

    def load(ref, t, lb):
        return jnp.concatenate([ref[lb * tiles_per + j, rows(t), :] for j in range(tiles_per)], axis=1)

    def store(ref, t, lb, val):
        for j in range(tiles_per):
            ref[lb * tiles_per + j, rows(t), :] = val[:, j * LANE:(j + 1) * LANE]

    for lb in range(S5_LANES // SCAN_LANES):
        lanes = pl.ds(lb * SCAN_LANES, SCAN_LANES)
        ar1, ai1 = a_r[:, lb * SCAN_LANES:(lb + 1) * SCAN_LANES], a_i[:, lb * SCAN_LANES:(lb + 1) * SCAN_LANES]
        ar8 = jnp.broadcast_to(ar1, (SUBLANE, SCAN_LANES))
        ai8 = jnp.broadcast_to(ai1, (SUBLANE, SCAN_LANES))

        def local(t, c):
            sr, si = c
            return (ar8 * sr - ai8 * si + load(br_ref, t, lb), ar8 * si + ai8 * sr + load(bi_ref, t, lb))

        zero = jnp.zeros((SUBLANE, SCAN_LANES), F32)
        er, ei = lax.fori_loop(0, seg, local, (zero, zero))
        pr, pi = ar1, ai1
        for _ in range(seg.bit_length() - 1):
            pr, pi = pr * pr - pi * pi, 2.0 * pr * pi
        cr, ci = car_r[:, lanes], car_i[:, lanes]
        for s in order:
            ini_r[s:s + 1, lanes] = cr
            ini_i[s:s + 1, lanes] = ci
            cr, ci = pr * cr - pi * ci + er[s:s + 1, :], pr * ci + pi * cr + ei[s:s + 1, :]
        car_r[:, lanes] = cr
        car_i[:, lanes] = ci

        if xr_ref is None:
            def final(t, c):
                sr, si = c
                nr = ar8 * sr - ai8 * si + load(br_ref, t, lb)
                ni = ar8 * si + ai8 * sr + load(bi_ref, t, lb)
                store(br_ref, t, lb, nr)
                store(bi_ref, t, lb, ni)
                return nr, ni

            lax.fori_loop(0, seg, final, (ini_r[:, lanes], ini_i[:, lanes]))
        else:
            def final_acc(t, c):
                sr, si, gr, gi = c
                xr, xi = load(xr_ref, t, lb), load(xi_ref, t, lb)
                gr = gr + sr * xr + si * xi
                gi = gi + si * xr - sr * xi
                nr = ar8 * sr - ai8 * si + load(br_ref, t, lb)
                ni = ar8 * si + ai8 * sr + load(bi_ref, t, lb)
                store(br_ref, t, lb, nr)
                store(bi_ref, t, lb, ni)
                return nr, ni, gr, gi

            _, _, gr, gi = lax.fori_loop(0, seg, final_acc,
                                         (ini_r[:, lanes], ini_i[:, lanes], acc_r[:, lanes], acc_i[:, lanes]))
            acc_r[:, lanes] = gr
            acc_i[:, lanes] = gi


def _s5_tail(gg, q, sn):
    return _rms(gg * _sigmoid(q), sn)


def _scan_order(n_rows):
    seg = n_rows // SUBLANE
    r = lax.broadcasted_iota(jnp.int32, (n_rows, n_rows), 0)
    c = lax.broadcasted_iota(jnp.int32, (n_rows, n_rows), 1)
    return (c == (r % SUBLANE) * seg + r // SUBLANE).astype(F32)


S5_STATE_TILES = S5_LANES // LANE
TILES_PER_SLICE = S5_SLICE_STATES // LANE


def _put_states(ref, k, val):
    for j in range(TILES_PER_SLICE):
        ref[k * TILES_PER_SLICE + j] = val[:, j * LANE:(j + 1) * LANE]


def _get_states(ref, k):
    return jnp.concatenate([ref[k * TILES_PER_SLICE + j] for j in range(TILES_PER_SLICE)], axis=1)


def _state_rows(tile, n_tiles=None):
    if n_tiles is None:
        return pl.BlockSpec((S5_STATE_TILES, tile, LANE), lambda i: (0, i, 0))
    return pl.BlockSpec((S5_STATE_TILES, tile, LANE), lambda i: (0, n_tiles - 1 - i, 0))


def _s5_fwd(u, a_r, a_i, bdb, bcr, bci, dsk, wglu, bglu, sn, name):
    n_tok = u.shape[0]
    n_sub = 2
    tc = n_sub * S5_TILE
    sw = S5_SLICE_STATES

    def body(u_ref, ar_ref, ai_ref, bdb_ref, bcr_ref, bci_ref, d_ref, wg_ref, bg_ref, sn_ref,
             ya_ref, xr_ref, xi_ref, v_ref, car_r, car_i, ini_r, ini_i):
        @pl.when(pl.program_id(0) == 0)
        def _():
            car_r[...] = jnp.zeros_like(car_r)
            car_i[...] = jnp.zeros_like(car_i)

        order = _scan_order(S5_TILE)
        u_t = jnp.concatenate([_nn_f32(order, u_ref[s * S5_TILE:(s + 1) * S5_TILE, :]) for s in range(n_sub)], axis=0)
        ub = u_t.astype(BF16)
        for k in range(S5_SLICES):
            bu = _nn(ub[:, k * LANE:(k + 1) * LANE], bdb_ref[k])
            _put_states(xr_ref, k, bu[:, :sw])
            _put_states(xi_ref, k, bu[:, sw:])
        for s in range(n_sub):
            _s5_scan(xr_ref, xi_ref, ar_ref[...], ai_ref[...], car_r, car_i, ini_r, ini_i, reverse=False, row0=s * S5_TILE)
        vs = [_nn(_get_states(xr_ref, k), bcr_ref[k]) - _nn(_get_states(xi_ref, k), bci_ref[k])
              for k in range(S5_SLICES)]
        v = jnp.concatenate(vs, axis=1) + d_ref[...] * u_t
        v_ref[...] = v
        gg = _gelu(v)
        ya = _s5_tail(gg, _nn(gg, wg_ref[...]) + bg_ref[...], sn_ref[...])
        for s in range(n_sub):
            rows = slice(s * S5_TILE, (s + 1) * S5_TILE)
            ya_ref[rows, :] = _tn_f32(order, ya[rows, :]).astype(BF16)

    return _call(
        body, name, (n_tok // tc,),
        [_rows(tc, D_MODEL), _const((1, S5_LANES)), _const((1, S5_LANES)), _const(bdb.shape), _const(bcr.shape),
         _const(bci.shape), _const((1, D_MODEL)), _const(wglu.shape), _const((1, D_MODEL)), _const((1, D_MODEL))],
        [_rows(tc, D_MODEL), _state_rows(tc), _state_rows(tc), _rows(tc, D_MODEL)],
        [_sds((n_tok, D_MODEL), BF16), _sds((S5_STATE_TILES, n_tok, LANE)), _sds((S5_STATE_TILES, n_tok, LANE)),
         _sds((n_tok, D_MODEL))],
        scratch=[pltpu.VMEM((1, S5_LANES), F32), pltpu.VMEM((1, S5_LANES), F32),
                 pltpu.VMEM((SUBLANE, S5_LANES), F32), pltpu.VMEM((SUBLANE, S5_LANES), F32)],
    )(u, a_r, a_i, bdb, bcr, bci, dsk, wglu, bglu, sn)


def _s5_bwd(dya, v, u, xr, xi, a_r, a_i, bdb, bcr, bci, dsk, wglu, bglu, sn, after, name):
    n_tok = u.shape[0]
    tc = S5_TILE
    nt = n_tok // tc
    sw = S5_SLICE_STATES

    def body(dya_ref, v_ref, u_ref, xr_ref, xi_ref, ar_ref, ai_ref, bdb_ref, bcr_ref, bci_ref, d_ref, wg_ref, bg_ref, sn_ref,
             after_ref, du_ref, gg_ref, dq_ref, gbdb_ref, gbcr_ref, gbci_ref, gar_ref, gai_ref, gd_ref, gbg_ref, gsn_ref,
             gr_ref, gi_ref, car_r, car_i, ini_r, ini_i):
        @pl.when(pl.program_id(0) == 0)
        def _():
            for r in (car_r, car_i, gbdb_ref, gbcr_ref, gbci_ref, gar_ref, gai_ref, gd_ref, gbg_ref, gsn_ref):
                r[...] = jnp.zeros_like(r)

        order = _scan_order(tc)
        u_t = _nn_f32(order, u_ref[...])
        gg, gelu_vjp = jax.vjp(_gelu, v_ref[...])
        _, tail_vjp = jax.vjp(_s5_tail, gg, _nn(gg, wg_ref[...]) + bg_ref[...], sn_ref[...])
        dgg, dq, dsn = tail_vjp(_nn_f32(order, dya_ref[...]))
        (dv,) = gelu_vjp(dgg + _nt(dq, wg_ref[...]))
        gg_ref[...] = gg.astype(BF16)
        dq_ref[...] = dq.astype(BF16)
        gd_ref[...] += jnp.sum(dv * u_t, axis=0, keepdims=True)
        gbg_ref[...] += jnp.sum(dq, axis=0, keepdims=True)
        gsn_ref[...] += dsn
        dvb = dv.astype(BF16)
        for k in range(S5_SLICES):
            dvk = dvb[:, k * LANE:(k + 1) * LANE]
            _put_states(gr_ref, k, _nt(dvk, bcr_ref[k]))
            _put_states(gi_ref, k, -_nt(dvk, bci_ref[k]))
            gbcr_ref[k] += _tn(_get_states(xr_ref, k), dvk)
            gbci_ref[k] -= _tn(_get_states(xi_ref, k), dvk)
        _s5_scan(gr_ref, gi_ref, ar_ref[...], -ai_ref[...], car_r, car_i, ini_r, ini_i, reverse=True,
                 xr_ref=xr_ref, xi_ref=xi_ref, acc_r=gar_ref, acc_i=gai_ref)
        ub = u_t.astype(BF16)
        dus = []
        for k in range(S5_SLICES):
            gk_r, gk_i = _get_states(gr_ref, k).astype(BF16), _get_states(gi_ref, k).astype(BF16)
            bk = bdb_ref[k]
            dus.append(_nt(gk_r, bk[:, :sw]) + _nt(gk_i, bk[:, sw:]))
            uk = ub[:, k * LANE:(k + 1) * LANE]
            gbdb_ref[k, :, :sw] += _tn(uk, gk_r)
            gbdb_ref[k, :, sw:] += _tn(uk, gk_i)
        du_ref[...] = _tn_f32(order, jnp.concatenate(dus, axis=1) + d_ref[...] * dv).astype(BF16)

    rev = functools.partial(_rows, n_tiles=nt)
    return _call(
        body, name, (nt,),
        [rev(tc, D_MODEL), rev(tc, D_MODEL), rev(tc, D_MODEL), _state_rows(tc, nt), _state_rows(tc, nt),
         _const((1, S5_LANES)), _const((1, S5_LANES)), _const(bdb.shape), _const(bcr.shape), _const(bci.shape),
         _const((1, D_MODEL)), _const(wglu.shape), _const((1, D_MODEL)), _const((1, D_MODEL)), HBM],
        [rev(tc, D_MODEL), rev(tc, D_MODEL), rev(tc, D_MODEL), _full(bdb.shape), _full(bcr.shape), _full(bci.shape),
         _full((SUBLANE, S5_LANES)), _full((SUBLANE, S5_LANES)), _full((1, D_MODEL)), _full((1, D_MODEL)), _full((1, D_MODEL))],
        [_sds((n_tok, D_MODEL), BF16), _sds((n_tok, D_MODEL), BF16), _sds((n_tok, D_MODEL), BF16), _sds(bdb.shape), _sds(bcr.shape),
         _sds(bci.shape), _sds((SUBLANE, S5_LANES)), _sds((SUBLANE, S5_LANES)), _sds((1, D_MODEL)), _sds((1, D_MODEL)),
         _sds((1, D_MODEL))],
        scratch=[pltpu.VMEM((S5_STATE_TILES, tc, LANE), F32), pltpu.VMEM((S5_STATE_TILES, tc, LANE), F32),
                 pltpu.VMEM((1, S5_LANES), F32), pltpu.VMEM((1, S5_LANES), F32),
                 pltpu.VMEM((SUBLANE, S5_LANES), F32), pltpu.VMEM((SUBLANE, S5_LANES), F32)],
    )(dya, v, u, xr, xi, a_r, a_i, bdb, bcr, bci, dsk, wglu, bglu, sn, after)


SSD_WIDTH = SSD_HEADS * SSD_HEAD_DIM
SSD_GROUPS = 2
HEADS_PER_GROUP = SSD_HEADS // SSD_GROUPS


def _take(x, axis, start, size):
    n = x.shape[axis]

    def sl(v):
        return lax.slice_in_dim(v, start, start + size, axis=axis)

    @jax.custom_vjp
    def f(v):
        return sl(v)

    def bwd(_, g):
        parts = []
        if start:
            parts.append(jnp.zeros(g.shape[:axis] + (start,) + g.shape[axis + 1:], g.dtype))
        parts.append(g)
        if n - start - size:
            parts.append(jnp.zeros(g.shape[:axis] + (n - start - size,) + g.shape[axis + 1:], g.dtype))
        return (jnp.concatenate(parts, axis=axis) if len(parts) > 1 else g,)

    f.defvjp(lambda v: (sl(v), None), bwd)
    return f(x)


def _lane_of(x, h):
    col = lax.broadcasted_iota(jnp.int32, x.shape, 1)
    return jnp.sum(jnp.where(col == h, x, 0.0), axis=1, keepdims=True)


def _ssd_chunk(xc, z, dt, dtb, alog, dvec, gn, st, nn, nt, tn, cumsum, take):
    t_len = xc.shape[0]
    xa = _silu(xc)
    dtp = _softplus(dt + dtb)
    d_a = dtp * (-jnp.exp(alog))
    row = lax.broadcasted_iota(jnp.int32, (t_len, t_len), 0)
    col = lax.broadcasted_iota(jnp.int32, (t_len, t_len), 1)
    causal = row >= col
    cum = cumsum(causal.astype(F32), d_a)
    eye = (row == col).astype(F32)
    group_width = HEADS_PER_GROUP * SSD_HEAD_DIM
    ys, sts = [], []
    for g in range(SSD_GROUPS):
        bg = take(xa, 1, SSD_WIDTH + g * SSD_STATE, SSD_STATE)
        cg = take(xa, 1, SSD_WIDTH + (SSD_GROUPS + g) * SSD_STATE, SSD_STATE)
        cb = nt(cg, bg)
        x_g = take(xa, 1, g * group_width, group_width)
        st_g = take(st, 0, g * group_width, group_width)
        diag, dt_l, d_l, grow_l, keep_l, last_r = [], [], [], [], [], []
        for r in range(HEADS_PER_GROUP):
            h = g * HEADS_PER_GROUP + r
            cc = _lane_of(cum, h)
            cr = jnp.sum(cc * eye, axis=0, keepdims=True)
            decay = jnp.exp(jnp.where(causal, cc - cr, -1e30))
            dt_h = _lane_of(dtp, h)
            c_last = jnp.sum(jnp.where(row[:, :1] == t_len - 1, cc, 0.0), axis=0, keepdims=True)
            lanes = (t_len, SSD_HEAD_DIM)
            diag.append(nn(cb * decay, take(x_g, 1, r * SSD_HEAD_DIM, SSD_HEAD_DIM) * dt_h))
            dt_l.append(jnp.broadcast_to(dt_h, lanes))
            d_l.append(jnp.broadcast_to(_lane_of(dvec, h), lanes))
            grow_l.append(jnp.broadcast_to(jnp.exp(cc), lanes))
            keep_l.append(jnp.broadcast_to(jnp.exp(c_last - cc), lanes))
            last_r.append(jnp.broadcast_to(jnp.exp(c_last), (SSD_HEAD_DIM, SSD_STATE)))
        side = functools.partial(jnp.concatenate, axis=1)
        xdt_g = x_g * side(dt_l)
        ys.append(side(diag) + side(grow_l) * nt(cg, st_g) + side(d_l) * x_g)
        sts.append(jnp.concatenate(last_r, axis=0) * st_g + tn(xdt_g * side(keep_l), bg))
    y = jnp.concatenate(ys, axis=1) * _silu(z)
    return _rms(y, gn), jnp.concatenate(sts, axis=0)


SSD_TILE = SSD_CHUNK


def _ssd_tile(xc, z, dt, dtb, alog, dvec, gn, st, nn, nt, tn, cumsum, take):
    ys = []
    for c in range(xc.shape[0] // SSD_CHUNK):
        rows = (c * SSD_CHUNK, SSD_CHUNK)
        y, st = _ssd_chunk(take(xc, 0, *rows), take(z, 0, *rows), take(dt, 0, *rows), dtb, alog, dvec, gn, st,
                           nn, nt, tn, cumsum, take)
        ys.append(y)
    return jnp.concatenate(ys, axis=0), st


def _shift_back(cur, prev, j):
    if j == 0:
        return cur
    row = lax.broadcasted_iota(jnp.int32, cur.shape, 0)
    return jnp.where(row < j, pltpu.roll(prev, j, 0), pltpu.roll(cur, j, 0))


def _shift_ahead(cur, nxt, j):
    if j == 0:
        return cur
    n = cur.shape[0]
    row = lax.broadcasted_iota(jnp.int32, cur.shape, 0)
    return jnp.where(row >= n - j, pltpu.roll(nxt, n - j, 0), pltpu.roll(cur, n - j, 0))


def _conv(cur, prev, w, b):
    out = b + w[SSD_CONV - 1:SSD_CONV, :] * cur
    for k in range(SSD_CONV - 1):
        out = out + w[k:k + 1, :] * _shift_back(cur, prev, SSD_CONV - 1 - k)
    return out


def _ssd_fwd(xbc, z, dt, conv_w, conv_b, dtb, alog, dvec, gn, name):
    n_tok = xbc.shape[0]
    tc = SSD_TILE
    nc = n_tok // tc
    st_rows = SSD_HEADS * SSD_HEAD_DIM

    def body(cur_ref, prev_ref, z_ref, dt_ref, w_ref, b_ref, dtb_ref, alog_ref, dvec_ref, gn_ref,
             yb_ref, stin_ref, st_ref):
        i = pl.program_id(0)

        @pl.when(i == 0)
        def _():
            st_ref[...] = jnp.zeros_like(st_ref)

        prev = jnp.where(i > 0, prev_ref[...], 0.0)
        xc = _conv(cur_ref[...], prev, w_ref[...], b_ref[...])
        st = st_ref[...]
        stin_ref[0] = st
        yb, st_new = _ssd_tile(xc, z_ref[...], dt_ref[...], dtb_ref[...], alog_ref[...], dvec_ref[...], gn_ref[...], st,
                               _nn, _nt, _tn, _nn_f32, lambda v, axis, start, size: lax.slice_in_dim(v, start, start + size, axis=axis))
        yb_ref[...] = yb.astype(BF16)
        st_ref[...] = st_new

    return _call(
        body, name, (nc,),
        [_rows(tc, SSD_CONV_DIM), pl.BlockSpec((tc, SSD_CONV_DIM), lambda i: (jnp.maximum(i - 1, 0), 0)),
         _rows(tc, D_MODEL), _rows(tc, LANE), _const((SSD_CONV, SSD_CONV_DIM)), _const((1, SSD_CONV_DIM)),
         _const((1, LANE)), _const((1, LANE)), _const((1, LANE)), _const((1, D_MODEL))],
        [_rows(tc, D_MODEL), pl.BlockSpec((1, st_rows, SSD_STATE), lambda i: (i, 0, 0))],
        [_sds((n_tok, D_MODEL), BF16), _sds((nc, st_rows, SSD_STATE))],
        scratch=[pltpu.VMEM((st_rows, SSD_STATE), F32)],
    )(xbc, xbc, z, dt, conv_w, conv_b, dtb, alog, dvec, gn)


def _ssd_bwd(dyb, xbc, z, dt, stin, conv_w, conv_b, dtb, alog, dvec, gn, name):
    n_tok = xbc.shape[0]
    tc = SSD_TILE
    nc = n_tok // tc
    st_rows = SSD_HEADS * SSD_HEAD_DIM

    def body(dyb_ref, cur_ref, prev_ref, z_ref, dt_ref, stin_ref, w_ref, b_ref, dtb_ref, alog_ref, dvec_ref, gn_ref,
             dxbc_ref, dz_ref, ddt_ref, gw_ref, gb_ref, gdtb_ref, galog_ref, gdvec_ref, ggn_ref,
             dst_ref, dxc_next_ref):
        i = pl.program_id(0)

        @pl.when(i == 0)
        def _():
            for r in (dst_ref, dxc_next_ref, gw_ref, gb_ref, gdtb_ref, galog_ref, gdvec_ref, ggn_ref):
                r[...] = jnp.zeros_like(r)

        cur = cur_ref[...]
        prev = jnp.where(i < nc - 1, prev_ref[...], 0.0)
        w = w_ref[...]
        xc = _conv(cur, prev, w, b_ref[...])
        chunk = functools.partial(_ssd_tile, nn=_nn_d, nt=_nt_d, tn=_tn_d, cumsum=_cumsum_rows, take=_take)
        _, vjp = jax.vjp(chunk, xc, z_ref[...], dt_ref[...], dtb_ref[...], alog_ref[...], dvec_ref[...], gn_ref[...],
                         stin_ref[0])
        dxc, dz, ddt, gdtb, galog, gdvec, ggn, dst = vjp((dyb_ref[...], dst_ref[...]))
        dst_ref[...] = dst
        dz_ref[...] = dz.astype(BF16)
        ddt_ref[...] = ddt.astype(BF16)
        gdtb_ref[...] += gdtb
        galog_ref[...] += galog
        gdvec_ref[...] += gdvec
        ggn_ref[...] += ggn
        dxc_next = dxc_next_ref[...]
        dxbc = w[SSD_CONV - 1:SSD_CONV, :] * dxc
        gws = []
        for k in range(SSD_CONV - 1):
            j = SSD_CONV - 1 - k
            dxbc = dxbc + w[k:k + 1, :] * _shift_ahead(dxc, dxc_next, j)
            gws.append(jnp.sum(dxc * _shift_back(cur, prev, j), axis=0, keepdims=True))
        gws.append(jnp.sum(dxc * cur, axis=0, keepdims=True))
        dxbc_ref[...] = dxbc.astype(BF16)
        gw_ref[...] += jnp.concatenate(gws, axis=0)
        gb_ref[...] += jnp.sum(dxc, axis=0, keepdims=True)
        dxc_next_ref[...] = dxc

    rev = functools.partial(_rows, n_tiles=nc)
    return _call(
        body, name, (nc,),
        [rev(tc, D_MODEL), rev(tc, SSD_CONV_DIM),
         pl.BlockSpec((tc, SSD_CONV_DIM), lambda i: (jnp.maximum(nc - 2 - i, 0), 0)),
         rev(tc, D_MODEL), rev(tc, LANE), pl.BlockSpec((1, st_rows, SSD_STATE), lambda i: (nc - 1 - i, 0, 0)),
         _const((SSD_CONV, SSD_CONV_DIM)), _const((1, SSD_CONV_DIM)), _const((1, LANE)), _const((1, LANE)),
         _const((1, LANE)), _const((1, D_MODEL))],
        [rev(tc, SSD_CONV_DIM), rev(tc, D_MODEL), rev(tc, LANE), _full((SSD_CONV, SSD_CONV_DIM)), _full((1, SSD_CONV_DIM)),
         _full((1, LANE)), _full((1, LANE)), _full((1, LANE)), _full((1, D_MODEL))],
        [_sds((n_tok, SSD_CONV_DIM), BF16), _sds((n_tok, D_MODEL), BF16), _sds((n_tok, LANE), BF16), _sds((SSD_CONV, SSD_CONV_DIM)),
         _sds((1, SSD_CONV_DIM)), _sds((1, LANE)), _sds((1, LANE)), _sds((1, LANE)), _sds((1, D_MODEL))],
        scratch=[pltpu.VMEM((st_rows, SSD_STATE), F32), pltpu.VMEM((tc, SSD_CONV_DIM), F32)],
    )(dyb, xbc, xbc, z, dt, stin, conv_w, conv_b, dtb, alog, dvec, gn)


@jax.custom_vjp
def _expand_cols(x, e):
    return _nn_f32(x, e)


_expand_cols.defvjp(
    lambda x, e: (_nn_f32(x, e), e),
    lambda e, g: (lax.dot_general(g, e, (((1,), (1,)), ((), ())), precision=lax.Precision.HIGHEST,
                                  preferred_element_type=F32), jnp.zeros_like(e)))


def _s5_discretize(lam_re, lam_im, log_step, b_re, b_im, expand):
    step = jnp.exp(log_step)
    mag = jnp.exp(lam_re * step)
    ang = lam_im * step
    a_r = mag * jnp.cos(ang)
    a_i = mag * jnp.sin(ang)
    den = lam_re * lam_re + lam_im * lam_im
    n_r = a_r - 1.0
    coef_r = _expand_cols((n_r * lam_re + a_i * lam_im) / den, expand)
    coef_i = _expand_cols((a_i * lam_re - n_r * lam_im) / den, expand)
    return a_r, a_i, coef_r * b_re - coef_i * b_im, coef_r * b_im + coef_i * b_re


def _expand_matrix():
    p = lax.broadcasted_iota(jnp.int32, (S5_STATE, S5_STATE * S5_GROUP), 0)
    c = lax.broadcasted_iota(jnp.int32, (S5_STATE, S5_STATE * S5_GROUP), 1)
    return (c // S5_GROUP == p).astype(F32)


def _s5_discretize_fwd(lam_re, lam_im, log_step, b_re, b_im, name):
    def body(lr_ref, li_ref, ls_ref, br_ref, bi_ref, ar_ref, ai_ref, bbr_ref, bbi_ref):
        outs = _s5_discretize(lr_ref[...], li_ref[...], ls_ref[...], br_ref[...], bi_ref[...], _expand_matrix())
        for r, o in zip((ar_ref, ai_ref, bbr_ref, bbi_ref), outs):
            r[...] = o

    sq, wide = (S5_GROUPS, S5_STATE), (S5_GROUPS, S5_STATE * S5_GROUP)
    return _call(body, name, (1,), [_full(sq), _full(sq), _full((S5_GROUPS, 1)), _full(wide), _full(wide)],
                 [_full(sq), _full(sq), _full(wide), _full(wide)], [_sds(sq), _sds(sq), _sds(wide), _sds(wide)],
                 )(lam_re, lam_im, log_step, b_re, b_im)


def _s5_discretize_bwd(lam_re, lam_im, log_step, b_re, b_im, g_ar8, g_ai8, g_bbr, g_bbi, name):
    def body(lr_ref, li_ref, ls_ref, br_ref, bi_ref, gar_ref, gai_ref, gbbr_ref, gbbi_ref,
             glr_ref, gli_ref, gls_ref, gbr_ref, gbi_ref):
        _, vjp = jax.vjp(functools.partial(_s5_discretize, expand=_expand_matrix()),
                         lr_ref[...], li_ref[...], ls_ref[...], br_ref[...], bi_ref[...])
        grads = vjp((jnp.sum(gar_ref[...], axis=0), jnp.sum(gai_ref[...], axis=0), gbbr_ref[...], gbbi_ref[...]))
        for r, g in zip((glr_ref, gli_ref, gls_ref, gbr_ref, gbi_ref), grads):
            r[...] = g

    sq, wide, col = (S5_GROUPS, S5_STATE), (S5_GROUPS, S5_STATE * S5_GROUP), (S5_GROUPS, 1)
    part = (SUBLANE,) + sq
    return _call(body, name, (1,),
                 [_full(sq), _full(sq), _full(col), _full(wide), _full(wide), _full(part), _full(part), _full(wide), _full(wide)],
                 [_full(sq), _full(sq), _full(col), _full(wide), _full(wide)],
                 [_sds(sq), _sds(sq), _sds(col), _sds(wide), _sds(wide)],
                 )(lam_re, lam_im, log_step, b_re, b_im, g_ar8, g_ai8, g_bbr, g_bbi)


GROUPS_PER_SLICE = LANE // S5_GROUP


def _block_diag_b(bb):
    t = bb.reshape(S5_SLICES, GROUPS_PER_SLICE, S5_STATE, S5_GROUP)
    eye = jnp.eye(GROUPS_PER_SLICE, dtype=bb.dtype)
    return jnp.einsum("kgph,gf->kghfp", t, eye).reshape(S5_SLICES, LANE, S5_SLICE_STATES)


def _block_diag_b_inv(m):
    t = m.reshape(S5_SLICES, GROUPS_PER_SLICE, S5_GROUP, GROUPS_PER_SLICE, S5_STATE)
    return jnp.einsum("kghgp->kgph", t).reshape(S5_GROUPS, S5_STATE * S5_GROUP)


def _block_diag_c(c):
    t = c.reshape(S5_SLICES, GROUPS_PER_SLICE, S5_GROUP, S5_STATE)
    eye = jnp.eye(GROUPS_PER_SLICE, dtype=c.dtype)
    return jnp.einsum("kghp,gf->kgpfh", t, eye).reshape(S5_SLICES, S5_SLICE_STATES, LANE)


def _block_diag_c_inv(m):
    t = m.reshape(S5_SLICES, GROUPS_PER_SLICE, S5_STATE, GROUPS_PER_SLICE, S5_GROUP)
    return jnp.einsum("kgpgh->kghp", t).reshape(S5_GROUPS, S5_GROUP, S5_STATE)


def _pad_lanes(v):
    return jnp.pad(v.reshape(1, -1), ((0, 0), (0, LANE - v.shape[0])))


def _prepare_layer(w, blk, i, after):
    p = {}
    p["wu"], p["wz"], p["wx"], p["wd"] = _w_in_split(blk["w_in"], after, name=f"w_in_split_{i}")
    p["nm"] = w["norm_mix"][i].reshape(1, D_MODEL)
    p["lam_re"], p["lam_im"] = w["s5_lam_re"][i], w["s5_lam_im"][i]
    p["log_step"] = w["s5_log_step"][i].reshape(S5_GROUPS, 1)
    p["b_re"] = w["s5_b_re"][i].reshape(S5_GROUPS, S5_STATE * S5_GROUP)
    p["b_im"] = w["s5_b_im"][i].reshape(S5_GROUPS, S5_STATE * S5_GROUP)
    a_r, a_i, bb_r, bb_i = _s5_discretize_fwd(p["lam_re"], p["lam_im"], p["log_step"], p["b_re"], p["b_im"],
                                              name=f"s5_discretize_{i}")
    p["a_r"], p["a_i"] = a_r.reshape(1, S5_LANES), a_i.reshape(1, S5_LANES)
    p["bdb"] = jnp.concatenate([_block_diag_b(bb_r), _block_diag_b(bb_i)], axis=2).astype(BF16)
    p["bcr"] = _block_diag_c(w["s5_c_re"][i]).astype(BF16)
    p["bci"] = _block_diag_c(w["s5_c_im"][i]).astype(BF16)
    p["dsk"] = w["s5_d"][i].reshape(1, D_MODEL)
    p["wglu"] = blk["s5_w_glu"].reshape(D_MODEL, D_MODEL)
    p["bglu"] = w["s5_b_glu"][i].reshape(1, D_MODEL)
    p["sn"] = w["s5_norm"][i].reshape(1, D_MODEL)
    p["conv_w"] = blk["ssd_conv_w"]
    p["conv_b"] = w["ssd_conv_b"][i].reshape(1, SSD_CONV_DIM)
    p["dtb"] = _pad_lanes(w["ssd_dt_bias"][i])
    p["alog"] = _pad_lanes(w["ssd_a_log"][i])
    p["dvec"] = _pad_lanes(w["ssd_d"][i])
    p["gn"] = w["ssd_norm"][i].reshape(1, D_MODEL)
    if "w_out" in blk:
        p.update(_late_weights(blk))
    p["nf"] = w["norm_ffn"][i].reshape(1, D_MODEL)
    return p


def _late_weights(blk):
    p = {}
    p["wo"] = blk["w_out"].reshape(2 * D_MODEL, D_MODEL)
    p["wg"], p["wup"], p["wdn"] = (blk[n].reshape(FFN_PAD, D_MODEL) for n in ("w_gate", "w_up", "w_down"))
    return p


def _layer_fwd(x0, p, i, late=None):
    u, z, xbc, dt = _inproj_fwd(x0, p["nm"], p["wu"], p["wz"], p["wx"], p["wd"], name=f"inproj_fwd_{i}")
    ya, xr, xi, v = _s5_fwd(u, p["a_r"], p["a_i"], p["bdb"], p["bcr"], p["bci"], p["dsk"], p["wglu"], p["bglu"], p["sn"],
                            name=f"s5_fwd_{i}")
    yb, stin = _ssd_fwd(xbc, z, dt, p["conv_w"], p["conv_b"], p["dtb"], p["alog"], p["dvec"], p["gn"], name=f"ssd_fwd_{i}")
    if late is not None:
        p.update(_late_weights(late(yb)))
    x1, x2 = _mix_ffn_fwd(x0, ya, yb, p["wo"], p["nf"], p["wg"], p["wup"], p["wdn"], name=f"mix_ffn_fwd_{i}")
    return x2, dict(x0=x0, u=u, z=z, xbc=xbc, dt=dt, xr=xr, xi=xi, v=v, stin=stin, ya=ya, yb=yb, x1=x1)


def _layer_bwd(dx2, s, p, i, after, between=None):
    g = {}
    dx1, dya, dyb, h2, act, dgt, dup, dx2b, dx1b, g_nf = _mix_ffn_bwd(
        s["x1"], dx2, p["wo"], p["nf"], p["wg"], p["wup"], p["wdn"], after, name=f"mix_ffn_bwd_{i}")
    g["norm_ffn"] = g_nf.reshape(D_MODEL)
    g["w_down"] = _matmul_tn_lhs_blocks(act, dx2b, FFN_BLOCK_PAD, FFN_BLOCK, name=f"grad_w_down_{i}")
    g["w_gate"] = _matmul_tn_lhs_blocks(dgt, h2, FFN_BLOCK_PAD, FFN_BLOCK, name=f"grad_w_gate_{i}")
    g["w_up"] = _matmul_tn_lhs_blocks(dup, h2, FFN_BLOCK_PAD, FFN_BLOCK, name=f"grad_w_up_{i}")
    g["w_out"] = _matmul_tn_pair(s["ya"], s["yb"], dx1b, name=f"grad_w_out_{i}").reshape(N_DEV, 2 * D_MODEL // N_DEV, D_MODEL)
    if between is not None:
        after = between(g)

    (du, gg, dq, g_bdb, g_bcr, g_bci, g_ar8, g_ai8, g_d, g_bglu, g_sn) = _s5_bwd(
        dya, s["v"], s["u"], s["xr"], s["xi"], p["a_r"], p["a_i"], p["bdb"], p["bcr"], p["bci"], p["dsk"], p["wglu"],
        p["bglu"], p["sn"], after, name=f"s5_bwd_{i}")
    g["s5_w_glu"] = _matmul_tn(gg, dq, name=f"grad_w_glu_{i}").reshape(N_DEV, D_MODEL // N_DEV, D_MODEL)
    g["s5_d"], g["s5_b_glu"], g["s5_norm"] = g_d.reshape(D_MODEL), g_bglu.reshape(D_MODEL), g_sn.reshape(D_MODEL)
    g["s5_c_re"], g["s5_c_im"] = _block_diag_c_inv(g_bcr), _block_diag_c_inv(g_bci)
    sq = (SUBLANE, S5_GROUPS, S5_STATE)
    g_lr, g_li, g_ls, g_br, g_bi = _s5_discretize_bwd(
        p["lam_re"], p["lam_im"], p["log_step"], p["b_re"], p["b_im"], g_ar8.reshape(sq), g_ai8.reshape(sq),
        _block_diag_b_inv(g_bdb[:, :, :S5_SLICE_STATES]), _block_diag_b_inv(g_bdb[:, :, S5_SLICE_STATES:]),
        name=f"s5_discretize_bwd_{i}")
    g["s5_lam_re"], g["s5_lam_im"], g["s5_log_step"] = g_lr, g_li, g_ls.reshape(S5_GROUPS)
    b_shape = (S5_GROUPS, S5_STATE, S5_GROUP)
    g["s5_b_re"], g["s5_b_im"] = g_br.reshape(b_shape), g_bi.reshape(b_shape)

    dxbc, dz, ddt, g_cw, g_cb, g_dtb, g_alog, g_dvec, g_gn = _ssd_bwd(
        dyb, s["xbc"], s["z"], s["dt"], s["stin"], p["conv_w"], p["conv_b"], p["dtb"], p["alog"], p["dvec"], p["gn"],
        name=f"ssd_bwd_{i}")
    g["ssd_conv_w"] = jnp.moveaxis(g_cw.reshape(SSD_CONV, N_DEV, SSD_CONV_DIM // N_DEV), 1, 0)
    g["ssd_conv_b"] = g_cb.reshape(SSD_CONV_DIM)
    g["ssd_dt_bias"], g["ssd_a_log"], g["ssd_d"] = g_dtb[0, :SSD_HEADS], g_alog[0, :SSD_HEADS], g_dvec[0, :SSD_HEADS]
    g["ssd_norm"] = g_gn.reshape(D_MODEL)

    dx0, h, g_nm = _inproj_bwd(s["x0"], p["nm"], du, dz, dxbc, ddt, dx1, p["wu"], p["wz"], p["wx"], p["wd"],
                               name=f"inproj_bwd_{i}")
    g["norm_mix"] = g_nm.reshape(D_MODEL)
    g["w_in"] = _w_in_grad_blocks(
        _matmul_tn(h, du, name=f"grad_w_in_u_{i}"), _matmul_tn(h, dz, name=f"grad_w_in_z_{i}"),
        _matmul_tn(h, dxbc, name=f"grad_w_in_xbc_{i}"), _matmul_tn(h, ddt, name=f"grad_w_in_dt_{i}"),
        name=f"grad_w_in_blocks_{i}")
    return dx0, g


def _example_step(x, target, w, blks):
    prepared = [_prepare_layer(w, blks[i], i, x) for i in range(DEPTH)]
    saved = []
    h = x
    for i in range(DEPTH):
        h, s = _layer_fwd(h, prepared[i], i)
        saved.append(s)
    loss, dh, g_final = _loss_head(h, w["norm_final"].reshape(1, D_MODEL), target, name="loss_head")
    layer_grads = [None] * DEPTH
    for i in reversed(range(DEPTH)):
        dh, layer_grads[i] = _layer_bwd(dh, saved[i], prepared[i], i, x)
    return loss, dh, layer_grads, g_final.reshape(D_MODEL)


def _mesh_position():
    return lax.axis_index("x"), lax.axis_index("y"), lax.axis_index("c")


def _peer(pos, k):
    x, y, c = pos
    px = 1 - x if k & 4 else x
    py = 1 - y if k & 2 else y
    pc = 1 - c if k & 1 else c
    return (px, py, pc), 4 * px + 2 * py + pc


HBM = pl.BlockSpec(memory_space=pl.ANY)


def _run_copies(local, remote):
    for cp in local + remote:
        cp.start()
    for cp in remote:
        cp.wait_recv()
    for cp in remote:
        cp.wait_send()
    for cp in local:
        cp.wait()


def _comm_scratch(n_units):
    return [pltpu.SemaphoreType.DMA((n_units, N_DEV - 1)), pltpu.SemaphoreType.DMA((n_units, N_DEV - 1)),
            pltpu.SemaphoreType.DMA((n_units,))]


def _gather_blocks(arrays, layered, name):
    units, out_shapes = [], []
    for j, (a, lay) in enumerate(zip(arrays, layered)):
        for layer in (range(a.shape[0]) if lay else (None,)):
            units.append((j, layer, len(out_shapes)))
            out_shapes.append(_sds((N_DEV,) + (a.shape[1:] if lay else a.shape), a.dtype))
    n_in = len(arrays)
    other_chips = (4, 2, 6)

    def body(*refs):
        ins, outs = refs[:n_in], refs[n_in:n_in + len(out_shapes)]
        send_sems, recv_sems, local_sems = refs[n_in + len(out_shapes):]
        pos = _mesh_position()
        me = 4 * pos[0] + 2 * pos[1] + pos[2]
        sibling, _ = _peer(pos, 1)
        local, own, passed = [], [], []
        for u, (j, layer, o) in enumerate(units):
            src = ins[j] if layer is None else ins[j].at[layer]
            local.append(pltpu.make_async_copy(src, outs[o].at[me], local_sems.at[u]))

            def copy(sem, src_ref, slot, to, u=u, o=o):
                return pltpu.make_async_remote_copy(
                    src_ref=src_ref, dst_ref=outs[o].at[slot], send_sem=send_sems.at[u, sem], recv_sem=recv_sems.at[u, sem],
                    device_id=to, device_id_type=MESH_ID)

            own.append([copy(0, src, me, sibling)] + [copy(1 + i, src, me, _peer(pos, k)[0]) for i, k in enumerate(other_chips)])
            passed.append([copy(4 + i, outs[o].at[_peer(pos, k)[1]], _peer(pos, k)[1], sibling) for i, k in enumerate(other_chips)])
        for cp in local + [c for unit in own for c in unit]:
            cp.start()
        for u in range(len(units)):
            for i in range(len(other_chips)):
                own[u][1 + i].wait_recv()
                passed[u][i].start()
        for u in range(len(units)):
            own[u][0].wait_recv()
            for cp in passed[u]:
                cp.wait_recv()
        for cp in [c for unit in own + passed for c in unit]:
            cp.wait_send()
        for cp in local:
            cp.wait()

    outs = pl.pallas_call(body, name=name, in_specs=[HBM] * n_in, out_specs=[HBM] * len(out_shapes), out_shape=out_shapes,
                          scratch_shapes=_comm_scratch(len(units)))(*arrays)
    grouped = [[] for _ in arrays]
    for j, _, o in units:
        grouped[j].append(outs[o])
    return [tuple(g) for g in grouped]


def _exchange_blocks(entries, name):
    units, flat_in, out_shapes = [], [], []
    for j, entry in enumerate(entries):
        for layer, a in enumerate(entry):
            units.append((len(flat_in), layer, j))
            flat_in.append(a)
        out_shapes.append(_sds((N_DEV, len(entry)) + entry[0].shape[1:], entry[0].dtype))
    n_in = len(flat_in)

    def body(*refs):
        ins, outs = refs[:n_in], refs[n_in:n_in + len(out_shapes)]
        send_sems, recv_sems, local_sems = refs[n_in + len(out_shapes):]
        pos = _mesh_position()
        me = 4 * pos[0] + 2 * pos[1] + pos[2]
        local, remote = [], []
        for u, (i, layer, o) in enumerate(units):
            local.append(pltpu.make_async_copy(ins[i].at[me], outs[o].at[me, layer], local_sems.at[u]))
            for k in range(1, N_DEV):
                peer, peer_index = _peer(pos, k)
                remote.append(pltpu.make_async_remote_copy(
                    src_ref=ins[i].at[peer_index], dst_ref=outs[o].at[me, layer], send_sem=send_sems.at[u, k - 1],
                    recv_sem=recv_sems.at[u, k - 1], device_id=peer, device_id_type=MESH_ID))
        _run_copies(local, remote)

    return pl.pallas_call(body, name=name, in_specs=[HBM] * n_in, out_specs=[HBM] * len(out_shapes), out_shape=out_shapes,
                          scratch_shapes=_comm_scratch(len(units)))(*flat_in)


SEM = pl.BlockSpec(memory_space=pltpu.SEMAPHORE)
SIDE_EFFECT = pltpu.SideEffectType.DATAFLOW_SIDE_EFFECTING


def _own_slots(arrays, indexed, me, name):
    lands = []
    for u, a in enumerate(arrays):
        block = a.shape[1:] if indexed else a.shape
        rows, cols = _size(block[:-1]), block[-1]
        tr = _row_tile(rows, cap=512)

        def body(me_ref, src_ref, out_ref):
            out_ref[...] = src_ref[...]

        src_spec = (pl.BlockSpec((None, tr, cols), lambda i, me_ref: (me_ref[0], i, 0)) if indexed
                    else pl.BlockSpec((tr, cols), lambda i, me_ref: (i, 0)))
        land = pl.pallas_call(
            body, name=f"{name}_{u}", out_shape=_sds((N_DEV, rows, cols), a.dtype),
            grid_spec=pltpu.PrefetchScalarGridSpec(
                num_scalar_prefetch=1, grid=(rows // tr,), in_specs=[src_spec],
                out_specs=pl.BlockSpec((None, tr, cols), lambda i, me_ref: (me_ref[0], i, 0))),
        )(me, a.reshape((N_DEV, rows, cols) if indexed else (rows, cols)))
        lands.append(land.reshape((N_DEV,) + block))
    return lands


def _split_copies(srcs, lands, send_sems, recv_sems, indexed):
    pos = _mesh_position()
    me = 4 * pos[0] + 2 * pos[1] + pos[2]
    copies = []
    for u, (src, land) in enumerate(zip(srcs, lands)):
        for k in range(1, N_DEV):
            peer, peer_index = _peer(pos, k)
            copies.append(pltpu.make_async_remote_copy(
                src_ref=src.at[peer_index] if indexed else src, dst_ref=land.at[me],
                send_sem=send_sems.at[u * (N_DEV - 1) + k - 1], recv_sem=recv_sems.at[u * (N_DEV - 1) + k - 1],
                device_id=peer, device_id_type=MESH_ID))
    return copies


def _exchange_start(arrays, lands, indexed, name):
    n = len(arrays)

    def body(*refs):
        srcs, zones = refs[:n], refs[n:2 * n]
        send_sems, recv_sems = refs[2 * n], refs[2 * n + 1]
        token = refs[-1]
        for cp in _split_copies(srcs, zones, send_sems, recv_sems, indexed):
            cp.start()
        token[...] = jnp.zeros_like(token)

    sem_shape = pltpu.SemaphoreType.DMA((n * (N_DEV - 1),))
    outs = pl.pallas_call(
        body, name=name, in_specs=[HBM] * (2 * n),
        out_specs=[SEM, SEM] + [HBM] * (2 * n) + [pl.BlockSpec(memory_space=pltpu.VMEM)],
        out_shape=[sem_shape, sem_shape] + [pltpu.HBM(a.shape, a.dtype) for a in list(arrays) + list(lands)]
        + [_sds((SUBLANE, LANE))],
        input_output_aliases={i: 2 + i for i in range(2 * n)},
        compiler_params=pltpu.CompilerParams(has_side_effects=SIDE_EFFECT),
    )(*[pltpu.with_memory_space_constraint(a, pltpu.HBM) for a in list(arrays) + list(lands)])
    return outs[0], outs[1], outs[2:2 + n], outs[2 + n:2 + 2 * n], outs[-1]


def _exchange_wait(send_sems, recv_sems, arrays, lands, after, indexed, name):
    n = len(arrays)

    def body(*refs):
        srcs, zones = refs[:n], refs[n:2 * n]
        s_sems, r_sems = refs[2 * n], refs[2 * n + 1]
        for cp in _split_copies(srcs, zones, s_sems, r_sems, indexed):
            cp.wait_send()
            cp.wait_recv()

    outs = pl.pallas_call(
        body, name=name, in_specs=[HBM] * (2 * n) + [SEM, SEM, HBM],
        out_specs=[HBM] * (2 * n), out_shape=[pltpu.HBM(a.shape, a.dtype) for a in list(arrays) + list(lands)],
        input_output_aliases={i: i for i in range(2 * n)},
        compiler_params=pltpu.CompilerParams(has_side_effects=SIDE_EFFECT),
    )(*arrays, *lands, send_sems, recv_sems, after)
    return outs[n:]


SUM_TILE = 512


def _adamw(w, g, m, v):
    m = ADAM_B1 * m + (1.0 - ADAM_B1) * g
    v = ADAM_B2 * v + (1.0 - ADAM_B2) * (g * g)
    m_hat = m / (1.0 - ADAM_B1 ** ADAM_STEP)
    v_hat = v / (1.0 - ADAM_B2 ** ADAM_STEP)
    return -ADAM_LR * (m_hat / (jnp.sqrt(v_hat) + ADAM_EPS) + ADAM_WD * w), m, v


def _sum_adamw(recv, w, m, v, layer, others, name):
    _, rows, cols = w.shape
    tr = _row_tile(rows, cap=256)

    def body(r_ref, w_ref, m_ref, v_ref, *rest):
        g_ref, d_ref, mo_ref, vo_ref = rest[-4:]
        g = r_ref[0].astype(F32)
        for j in range(1, N_DEV):
            g = g + r_ref[j].astype(F32)
        g_ref[...] = g
        d_ref[...], mo_ref[...], vo_ref[...] = _adamw(w_ref[...], g, m_ref[...], v_ref[...])

    blk = pl.BlockSpec((None, tr, cols), lambda i: (layer, i, 0))
    carried = list(others) if others is not None else []
    return pl.pallas_call(
        body, name=name, grid=(rows // tr,),
        in_specs=[pl.BlockSpec((N_DEV, tr, cols), lambda i: (0, i, 0)), blk, blk, blk] + [HBM] * len(carried),
        out_specs=[blk] * 4, out_shape=[_sds(w.shape)] * 4,
        input_output_aliases={4 + k: k for k in range(len(carried))},
        compiler_params=pltpu.CompilerParams(dimension_semantics=("arbitrary",), vmem_limit_bytes=VMEM_LIMIT),
    )(recv, w, m, v, *carried)


def _sum_senders(recv, name):
    _, rows, cols = recv.shape
    tr = _row_tile(rows, cap=256)

    def body(r_ref, g_ref):
        g = r_ref[0].astype(F32)
        for j in range(1, N_DEV):
            g = g + r_ref[j].astype(F32)
        g_ref[...] = g

    return _call(body, name, (rows // tr,), [pl.BlockSpec((N_DEV, tr, cols), lambda i: (0, i, 0))], [_rows(tr, cols)],
                 [_sds((rows, cols))])(recv)[0]


def _adamw_blocks(g, w, m, v, name):
    n_lay, rows, cols = w.shape
    tr = _row_tile(rows, cap=256)

    def body(g_ref, w_ref, m_ref, v_ref, d_ref, mo_ref, vo_ref):
        d_ref[...], mo_ref[...], vo_ref[...] = _adamw(w_ref[...], g_ref[...], m_ref[...], v_ref[...])

    blk = pl.BlockSpec((None, tr, cols), lambda l, i: (l, i, 0))
    return pl.pallas_call(
        body, name=name, grid=(n_lay, rows // tr), in_specs=[blk] * 4, out_specs=[blk] * 3, out_shape=[_sds(w.shape)] * 3,
        compiler_params=pltpu.CompilerParams(dimension_semantics=("arbitrary", "arbitrary"), vmem_limit_bytes=VMEM_LIMIT),
    )(g, w, m, v)


def _sum_slots(recv, name):
    rows = recv.shape[1]

    def body(r_ref, g_ref):
        g = r_ref[0].astype(F32)
        for j in range(1, N_DEV):
            g = g + r_ref[j].astype(F32)
        g_ref[...] = g

    return _call(body, name, (1,), [_full(recv.shape)], [_full((rows, LANE))], [_sds((rows, LANE))])(recv)[0]


def _adamw_rows(g, w, m, v, name):
    rows, cols = w.shape
    tr = _row_tile(rows)

    def body(g_ref, w_ref, m_ref, v_ref, d_ref, mo_ref, vo_ref):
        d_ref[...], mo_ref[...], vo_ref[...] = _adamw(w_ref[...], g_ref[...], m_ref[...], v_ref[...])

    blk = _rows(tr, cols)
    return _call(body, name, (rows // tr,), [blk] * 4, [blk] * 3, [_sds((rows, cols))] * 3)(g, w, m, v)


def _row_tile(rows, cap=1024):
    if rows % SUBLANE:
        return rows
    best = SUBLANE
    for t in range(SUBLANE, cap + 1, SUBLANE):
        if rows % t == 0:
            best = t
    return best


BIG = (("w_in", (DEPTH, D_MODEL, IN_PROJ // N_DEV), 2),
       ("s5_w_glu", (DEPTH, D_MODEL // N_DEV, D_MODEL), 1),
       ("ssd_conv_w", (DEPTH, SSD_CONV, SSD_CONV_DIM // N_DEV), 2),
       ("w_out", (DEPTH, 2 * D_MODEL // N_DEV, D_MODEL), 1),
       ("w_gate", (DEPTH, D_MODEL, FFN_HIDDEN // N_DEV), 2),
       ("w_up", (DEPTH, D_MODEL, FFN_HIDDEN // N_DEV), 2),
       ("w_down", (DEPTH, FFN_HIDDEN // N_DEV, D_MODEL), 1))
SMALL = (("norm_mix", (DEPTH, D_MODEL)), ("s5_lam_re", (DEPTH, S5_GROUPS, S5_STATE)), ("s5_lam_im", (DEPTH, S5_GROUPS, S5_STATE)),
         ("s5_log_step", (DEPTH, S5_GROUPS)), ("s5_b_re", (DEPTH, S5_GROUPS, S5_STATE, S5_GROUP)),
         ("s5_b_im", (DEPTH, S5_GROUPS, S5_STATE, S5_GROUP)), ("s5_c_re", (DEPTH, S5_GROUPS, S5_GROUP, S5_STATE)),
         ("s5_c_im", (DEPTH, S5_GROUPS, S5_GROUP, S5_STATE)), ("s5_d", (DEPTH, D_MODEL)), ("s5_b_glu", (DEPTH, D_MODEL)),
         ("s5_norm", (DEPTH, D_MODEL)), ("ssd_conv_b", (DEPTH, SSD_CONV_DIM)), ("ssd_dt_bias", (DEPTH, SSD_HEADS)),
         ("ssd_a_log", (DEPTH, SSD_HEADS)), ("ssd_d", (DEPTH, SSD_HEADS)), ("ssd_norm", (DEPTH, D_MODEL)),
         ("norm_ffn", (DEPTH, D_MODEL)), ("norm_final", (D_MODEL,)))
WEIGHT_ORDER = ("norm_mix", "w_in", "s5_lam_re", "s5_lam_im", "s5_log_step", "s5_b_re", "s5_b_im", "s5_c_re", "s5_c_im", "s5_d",
                "s5_w_glu", "s5_b_glu", "s5_norm", "ssd_conv_w", "ssd_conv_b", "ssd_dt_bias", "ssd_a_log", "ssd_d", "ssd_norm",
                "w_out", "norm_ffn", "w_gate", "w_up", "w_down", "norm_final")


def _size(shape):
    n = 1
    for s in shape:
        n *= s
    return n


def _round_up(n, m):
    return -(-n // m) * m


SMALL_SIZE = sum(_size(s) for _, s in SMALL)
SMALL_ROWS = _round_up(-(-SMALL_SIZE // (N_DEV * LANE)), SUBLANE)


def _pack(parts, rows, dtype):
    flat = jnp.concatenate([p.reshape(-1).astype(dtype) for p in parts])
    return jnp.pad(flat, (0, rows * LANE - flat.shape[0])).reshape(rows, LANE)


def _unpack(flat, specs):
    out, off = {}, 0
    flat = flat.reshape(-1)
    for name, shape in specs:
        out[name] = flat[off:off + _size(shape)].reshape(shape)
        off += _size(shape)
    return out


def kernel(x, norm_mix, w_in, s5_lam_re, s5_lam_im, s5_log_step, s5_b_re, s5_b_im, s5_c_re, s5_c_im, s5_d, s5_w_glu, s5_b_glu, s5_norm, ssd_conv_w, ssd_conv_b, ssd_dt_bias, ssd_a_log, ssd_d, ssd_norm, w_out, norm_ffn, w_gate, w_up, w_down, norm_final, loss_target, m_norm_mix, m_w_in, m_s5_lam_re, m_s5_lam_im, m_s5_log_step, m_s5_b_re, m_s5_b_im, m_s5_c_re, m_s5_c_im, m_s5_d, m_s5_w_glu, m_s5_b_glu, m_s5_norm, m_ssd_conv_w, m_ssd_conv_b, m_ssd_dt_bias, m_ssd_a_log, m_ssd_d, m_ssd_norm, m_w_out, m_norm_ffn, m_w_gate, m_w_up, m_w_down, m_norm_final, v_norm_mix, v_w_in, v_s5_lam_re, v_s5_lam_im, v_s5_log_step, v_s5_b_re, v_s5_b_im, v_s5_c_re, v_s5_c_im, v_s5_d, v_s5_w_glu, v_s5_b_glu, v_s5_norm, v_ssd_conv_w, v_ssd_conv_b, v_ssd_dt_bias, v_ssd_a_log, v_ssd_d, v_ssd_norm, v_w_out, v_norm_ffn, v_w_gate, v_w_up, v_w_down, v_norm_final):
    given = dict(locals())
    w = {n: given[n] for n in WEIGHT_ORDER}
    m = {n: given["m_" + n] for n in WEIGHT_ORDER}
    v = {n: given["v_" + n] for n in WEIGHT_ORDER}
    big_names = tuple(n for n, _, _ in BIG)
    matmul_names = tuple(n for n in big_names if n != "ssd_conv_w")

    conv_hi = w["ssd_conv_w"].astype(BF16)
    conv_lo = (w["ssd_conv_w"] - conv_hi.astype(F32)).astype(BF16)
    row_pad = ((0, 0), (0, FFN_BLOCK_PAD - FFN_BLOCK), (0, 0))
    as_rows = {"w_gate": jnp.swapaxes(w["w_gate"], 1, 2), "w_up": jnp.swapaxes(w["w_up"], 1, 2), "w_down": w["w_down"]}
    to_send = [jnp.pad(as_rows[n].astype(BF16), row_pad) if n in as_rows else w[n].astype(BF16) for n in matmul_names]

    def layer_blocks(i):
        return [a[i] for a in to_send] + [jnp.stack([conv_hi[i], conv_lo[i]])]

    def conv_taps(pair):
        pair = pair.astype(F32)
        return jnp.moveaxis(pair[:, 0] + pair[:, 1], 0, 1).reshape(SSD_CONV, SSD_CONV_DIM)

    def as_layer_weights(gathered):
        blk = dict(zip(matmul_names, gathered))
        blk["ssd_conv_w"] = conv_taps(gathered[-1])
        return blk

    gathered0 = [g[0] for g in _gather_blocks(layer_blocks(0), [False] * (len(matmul_names) + 1), name="gather_weights_0")]
    blocks1 = layer_blocks(1)
    me = (4 * lax.axis_index("x") + 2 * lax.axis_index("y") + lax.axis_index("c")).astype(jnp.int32).reshape(1)
    n_first = 2
    first1, then1 = blocks1[:n_first] + blocks1[-1:], blocks1[n_first:-1]
    sems1a = _exchange_start(first1, _own_slots(first1, False, me, name="gather_own_1a"), False, name="gather_start_1a")
    sems1b = _exchange_start(then1, _own_slots(then1, False, me, name="gather_own_1b"), False, name="gather_start_1b")
    started = sems1a[-1] + sems1b[-1]
    prepared = [_prepare_layer(w, as_layer_weights(gathered0), 0, started), None]
    saved = [None, None]
    h, saved[0] = _layer_fwd(x[0], prepared[0], 0)
    arrived = _exchange_wait(*sems1a[:4], h, False, name="gather_wait_1a")
    blk1 = dict(zip(matmul_names[:n_first], arrived))
    blk1["ssd_conv_w"] = conv_taps(arrived[-1])
    prepared[1] = _prepare_layer(w, blk1, 1, started)

    def rest_of_layer1(after):
        return dict(zip(matmul_names[n_first:], _exchange_wait(*sems1b[:4], after, False, name="gather_wait_1b")))

    h, saved[1] = _layer_fwd(h, prepared[1], 1, late=rest_of_layer1)
    loss, dh, g_final = _loss_head(h, w["norm_final"].reshape(1, D_MODEL), loss_target[0], name="loss_head")

    layer_grads = [None, None]
    dh, layer_grads[1] = _layer_bwd(dh, saved[1], prepared[1], 1, started)
    slots1 = [layer_grads[1][n] for n in big_names]
    sems2 = _exchange_start(slots1, _own_slots(slots1, True, me, name="exchange_own_1"), True, name="exchange_start_1")
    early_names = ("w_out", "w_gate", "w_up", "w_down")
    late_names = tuple(n for n in big_names if n not in early_names)
    early = {}

    def send_early(g):
        slots = [g[n] for n in early_names]
        early["sems"] = _exchange_start(slots, _own_slots(slots, True, me, name="exchange_own_0"), True, name="exchange_start_0")
        return early["sems"][-1]

    grad_x, layer_grads[0] = _layer_bwd(dh, saved[0], prepared[0], 0, sems2[-1], between=send_early)

    small = jnp.concatenate([g_final.reshape(-1) if n == "norm_final"
                             else jnp.stack([layer_grads[i][n] for i in range(DEPTH)]).reshape(-1) for n, _ in SMALL])
    small_slots = jnp.pad(small, (0, N_DEV * SMALL_ROWS * LANE - small.shape[0])).reshape(N_DEV, SMALL_ROWS, LANE)
    late = [layer_grads[0][n] for n in late_names] + [small_slots]
    sems3 = _exchange_start(late, _own_slots(late, True, me, name="exchange_own_late"), True, name="exchange_start_late")
    received1 = _exchange_wait(*sems2[:4], sems3[-1], True, name="exchange_wait_1")
    received_early = _exchange_wait(*early["sems"][:4], sems3[-1], True, name="exchange_wait_0")
    received0 = dict(zip(early_names, received_early))

    transposed = ("w_gate", "w_up")
    layer1 = {n: _sum_adamw(received1[j], w[n], m[n], v[n], 1, None, name=f"sum_adamw_{n}_1")
              for j, n in enumerate(big_names) if n not in transposed}
    results = {}
    for n in transposed:
        recv = (received0[n], received1[big_names.index(n)])
        g = jnp.stack([jnp.swapaxes(_sum_senders(recv[i], name=f"sum_{n}_{i}"), 0, 1) for i in range(DEPTH)])
        results[n] = [g, *_adamw_blocks(g, w[n], m[n], v[n], name=f"adamw_{n}")]
    for n in early_names:
        if n not in transposed:
            results[n] = _sum_adamw(received0[n], w[n], m[n], v[n], 0, layer1[n], name=f"sum_adamw_{n}_0")
    received_late = _exchange_wait(*sems3[:4], results["w_down"][0], True, name="exchange_wait_late")
    for n, recv in zip(late_names, received_late):
        results[n] = _sum_adamw(recv, w[n], m[n], v[n], 0, layer1[n], name=f"sum_adamw_{n}_0")
    g_part = _sum_slots(received_late[-1], name="sum_replicated")
    g_small = _gather_blocks([g_part], [False], name="gather_replicated")[0][0]
    for n, g in _unpack(g_small, SMALL).items():
        as_rows = (-1, g.shape[-1])
        d_n, m_n, v_n = _adamw_rows(g.reshape(as_rows), w[n].reshape(as_rows), m[n].reshape(as_rows), v[n].reshape(as_rows),
                                    name=f"adamw_{n}")
        results[n] = [g, d_n.reshape(g.shape), m_n.reshape(g.shape), v_n.reshape(g.shape)]

    outs = [results[n][k] for k in range(4) for n in WEIGHT_ORDER]
    total_loss = lax.psum(loss[0, 0], ("x", "y", "c"))
    return (total_loss, grad_x[None], *outs)
```

```python
import functools

import jax
import jax.numpy as jnp
from jax import lax
from jax.experimental import pallas as pl
from jax.experimental.pallas import tpu as pltpu

F32 = jnp.float32
BF16 = jnp.bfloat16
MESH_ID = pl.DeviceIdType.MESH

N_DEV = 8
DEPTH = 2
D_MODEL = 1024
S5_GROUPS = 64
S5_GROUP = 16
S5_STATE = 64
S5_LANES = S5_GROUPS * S5_STATE
SSD_HEADS = 16
SSD_HEAD_DIM = 64
SSD_STATE = 128
SSD_CHUNK = 128
SSD_CONV = 4
SSD_CONV_DIM = 1536
FFN_HIDDEN = 2816
IN_PROJ = 3600
EPS = 1e-6
LANE = 128
SUBLANE = 8
VMEM_LIMIT = 56 * 1024 * 1024

ADAM_LR = 0.001
ADAM_B1 = 0.9
ADAM_B2 = 0.999
ADAM_EPS = 1e-08
ADAM_WD = 0.01
ADAM_STEP = 10

TOK_TILE = 256
S5_TILE = 128


def _sigmoid(x):
    return jax.nn.sigmoid(x)


def _silu(x):
    return x * _sigmoid(x)


def _gelu(x):
    return 0.5 * x * (1.0 + jnp.tanh(0.7978845608028654 * (x + 0.044715 * (x * x * x))))


def _softplus(x):
    return jnp.maximum(x, 0.0) + jnp.log(1.0 + jnp.exp(-jnp.abs(x)))


def _rms(x, g):
    r = lax.rsqrt(jnp.mean(x * x, axis=-1, keepdims=True) + EPS)
    return x * r * g


def _nn(a, b):
    return lax.dot_general(a.astype(BF16), b.astype(BF16), (((1,), (0,)), ((), ())), preferred_element_type=F32)


def _nt(a, b):
    return lax.dot_general(a.astype(BF16), b.astype(BF16), (((1,), (1,)), ((), ())), preferred_element_type=F32)


def _tn(a, b):
    return lax.dot_general(a.astype(BF16), b.astype(BF16), (((0,), (0,)), ((), ())), preferred_element_type=F32)


def _nn_f32(a, b):
    return lax.dot_general(a, b, (((1,), (0,)), ((), ())), precision=lax.Precision.HIGHEST, preferred_element_type=F32)


def _tn_f32(a, b):
    return lax.dot_general(a, b, (((0,), (0,)), ((), ())), precision=lax.Precision.HIGHEST, preferred_element_type=F32)


@jax.custom_vjp
def _nn_d(a, b):
    return _nn(a, b)


_nn_d.defvjp(lambda a, b: (_nn(a, b), (a, b)), lambda r, g: (_nt(g, r[1]), _tn(r[0], g)))


@jax.custom_vjp
def _nt_d(a, b):
    return _nt(a, b)


_nt_d.defvjp(lambda a, b: (_nt(a, b), (a, b)), lambda r, g: (_nn(g, r[1]), _tn(g, r[0])))


@jax.custom_vjp
def _tn_d(a, b):
    return _tn(a, b)


_tn_d.defvjp(lambda a, b: (_tn(a, b), (a, b)), lambda r, g: (_nt(r[1], g), _nn(r[0], g)))


@jax.custom_vjp
def _cumsum_rows(tri, x):
    return _nn_f32(tri, x)


_cumsum_rows.defvjp(lambda tri, x: (_nn_f32(tri, x), tri), lambda tri, g: (jnp.zeros_like(tri), _tn_f32(tri, g)))


def _full(shape):
    zeros = (0,) * len(shape)
    return pl.BlockSpec(shape, lambda *_: zeros)


def _const(shape):
    zeros = (0,) * len(shape)
    return pl.BlockSpec(shape, lambda *_: zeros, pipeline_mode=pl.Buffered(1))


def _rows(tile, width, n_tiles=None):
    if n_tiles is None:
        return pl.BlockSpec((tile, width), lambda i: (i, 0))
    return pl.BlockSpec((tile, width), lambda i: (n_tiles - 1 - i, 0))


def _call(body, name, grid, in_specs, out_specs, out_shape, scratch=()):
    return pl.pallas_call(
        body, name=name, grid=grid, in_specs=in_specs, out_specs=out_specs, out_shape=out_shape,
        scratch_shapes=list(scratch),
        compiler_params=pltpu.CompilerParams(dimension_semantics=("arbitrary",) * len(grid),
                                             vmem_limit_bytes=VMEM_LIMIT))


def _sds(shape, dtype=F32):
    return jax.ShapeDtypeStruct(shape, dtype)


def _tile_of(n, cap=512):
    if n <= LANE:
        return n
    best = LANE
    for t in range(LANE, cap + 1, LANE):
        if n % t == 0:
            best = t
    return best


BIG_TILE = 512


def _big_tile(n_tok):
    return TOK_TILE


def _inproj_fwd(x, nm, wu, wz, wx, wd, name):
    n_tok = x.shape[0]
    tm = _big_tile(n_tok)

    def body(x_ref, nm_ref, wu_ref, wz_ref, wx_ref, wd_ref, u_ref, z_ref, xbc_ref, dt_ref):
        h = _rms(x_ref[...], nm_ref[...]).astype(BF16)
        u_ref[...] = _nn(h, wu_ref[...])
        z_ref[...] = _nn(h, wz_ref[...])
        xbc_ref[...] = _nn(h, wx_ref[...])
        dt_ref[...] = _nn(h, wd_ref[...])

    return _call(
        body, name, (n_tok // tm,),
        [_rows(tm, D_MODEL), _const((1, D_MODEL)), _const(wu.shape), _const(wz.shape), _const(wx.shape), _const(wd.shape)],
        [_rows(tm, D_MODEL), _rows(tm, D_MODEL), _rows(tm, SSD_CONV_DIM), _rows(tm, LANE)],
        [_sds((n_tok, D_MODEL)), _sds((n_tok, D_MODEL)), _sds((n_tok, SSD_CONV_DIM)), _sds((n_tok, LANE))],
    )(x, nm, wu, wz, wx, wd)


def _inproj_bwd(x, nm, du, dz, dxbc, ddt, dres, wu, wz, wx, wd, name):
    n_tok = x.shape[0]
    tm = _big_tile(n_tok)

    def body(x_ref, nm_ref, du_ref, dz_ref, dxbc_ref, ddt_ref, dres_ref, wu_ref, wz_ref, wx_ref, wd_ref,
             dx_ref, h_ref, dnm_ref):
        dh = (_nt(du_ref[...], wu_ref[...]) + _nt(dz_ref[...], wz_ref[...])
              + _nt(dxbc_ref[...], wx_ref[...]) + _nt(ddt_ref[...], wd_ref[...]))
        h, vjp = jax.vjp(_rms, x_ref[...], nm_ref[...])
        dx, dnm = vjp(dh)
        dx_ref[...] = dres_ref[...] + dx
        h_ref[...] = h.astype(BF16)

        @pl.when(pl.program_id(0) == 0)
        def _():
            dnm_ref[...] = jnp.zeros_like(dnm_ref)

        dnm_ref[...] += dnm

    return _call(
        body, name, (n_tok // tm,),
        [_rows(tm, D_MODEL), _const((1, D_MODEL)), _rows(tm, D_MODEL), _rows(tm, D_MODEL), _rows(tm, SSD_CONV_DIM),
         _rows(tm, LANE), _rows(tm, D_MODEL), _const(wu.shape), _const(wz.shape), _const(wx.shape), _const(wd.shape)],
        [_rows(tm, D_MODEL), _rows(tm, D_MODEL), _full((1, D_MODEL))],
        [_sds((n_tok, D_MODEL)), _sds((n_tok, D_MODEL), BF16), _sds((1, D_MODEL))],
    )(x, nm, du, dz, dxbc, ddt, dres, wu, wz, wx, wd)


def _ffn_act(gt, up):
    return _silu(gt) * up


FFN_BLOCK = FFN_HIDDEN // N_DEV
FFN_BLOCK_PAD = -(-FFN_BLOCK // LANE) * LANE


FFN_PAD = N_DEV * FFN_BLOCK_PAD


def _mix_ffn_fwd(x0, ya, yb, wo, nf, wg, wu, wd, name):
    n_tok = x0.shape[0]
    tm = TOK_TILE

    def body(x0_ref, ya_ref, yb_ref, wo_ref, nf_ref, wg_ref, wu_ref, wd_ref, x1_ref, x2_ref):
        x1 = x0_ref[...] + _nn(ya_ref[...], wo_ref[:D_MODEL, :]) + _nn(yb_ref[...], wo_ref[D_MODEL:, :])
        h = _rms(x1, nf_ref[...]).astype(BF16)
        x1_ref[...] = x1
        x2_ref[...] = x1 + _nn(_ffn_act(_nt(h, wg_ref[...]), _nt(h, wu_ref[...])), wd_ref[...])

    return _call(
        body, name, (n_tok // tm,),
        [_rows(tm, D_MODEL), _rows(tm, D_MODEL), _rows(tm, D_MODEL), _const(wo.shape),
         _const((1, D_MODEL)), _const(wg.shape), _const(wu.shape), _const(wd.shape)],
        [_rows(tm, D_MODEL), _rows(tm, D_MODEL)],
        [_sds((n_tok, D_MODEL)), _sds((n_tok, D_MODEL))],
    )(x0, ya, yb, wo, nf, wg, wu, wd)


def _mix_ffn_bwd(x1, dx2, wo, nf, wg, wu, wd, after, name):
    n_tok = x1.shape[0]
    tm = TOK_TILE
    n_chunks = 3
    hc = FFN_PAD // n_chunks

    def body(x1_ref, dx2_ref, wo_ref, nf_ref, wg_ref, wu_ref, wd_ref, after_ref,
             dx1_ref, dya_ref, dyb_ref, h_ref, a_ref, dgt_ref, dup_ref, dx2b_ref, dx1b_ref, dnf_ref):
        dx2 = dx2_ref[...]
        dx2b = dx2.astype(BF16)
        dx2b_ref[...] = dx2b
        h, rms_vjp = jax.vjp(_rms, x1_ref[...], nf_ref[...])
        hb = h.astype(BF16)
        dh = jnp.zeros_like(h)
        for c in range(n_chunks):
            rows = pl.ds(c * hc, hc)
            a, act_vjp = jax.vjp(_ffn_act, _nt(hb, wg_ref[rows, :]), _nt(hb, wu_ref[rows, :]))
            dgt, dup = act_vjp(_nt(dx2b, wd_ref[rows, :]))
            a_ref[:, c * hc:(c + 1) * hc] = a.astype(BF16)
            dgt_ref[:, c * hc:(c + 1) * hc] = dgt.astype(BF16)
            dup_ref[:, c * hc:(c + 1) * hc] = dup.astype(BF16)
            dh = dh + _nn(dgt, wg_ref[rows, :]) + _nn(dup, wu_ref[rows, :])
        dx, dnf = rms_vjp(dh)
        dx1 = dx2 + dx
        dx1b = dx1.astype(BF16)
        dx1_ref[...] = dx1
        dx1b_ref[...] = dx1b
        dya_ref[...] = _nt(dx1b, wo_ref[:D_MODEL, :])
        dyb_ref[...] = _nt(dx1b, wo_ref[D_MODEL:, :])
        h_ref[...] = hb

        @pl.when(pl.program_id(0) == 0)
        def _():
            dnf_ref[...] = jnp.zeros_like(dnf_ref)

        dnf_ref[...] += dnf

    hidden = _rows(tm, FFN_PAD)
    return _call(
        body, name, (n_tok // tm,),
        [_rows(tm, D_MODEL), _rows(tm, D_MODEL), _const(wo.shape), _const((1, D_MODEL)),
         _const(wg.shape), _const(wu.shape), _const(wd.shape), HBM],
        [_rows(tm, D_MODEL), _rows(tm, D_MODEL), _rows(tm, D_MODEL), _rows(tm, D_MODEL), hidden, hidden, hidden,
         _rows(tm, D_MODEL), _rows(tm, D_MODEL), _full((1, D_MODEL))],
        [_sds((n_tok, D_MODEL)), _sds((n_tok, D_MODEL)), _sds((n_tok, D_MODEL)), _sds((n_tok, D_MODEL), BF16),
         _sds((n_tok, FFN_PAD), BF16), _sds((n_tok, FFN_PAD), BF16), _sds((n_tok, FFN_PAD), BF16),
         _sds((n_tok, D_MODEL), BF16), _sds((n_tok, D_MODEL), BF16), _sds((1, D_MODEL))],
    )(x1, dx2, wo, nf, wg, wu, wd, after)


def _loss_head(x, nf, target, name):
    n_tok = x.shape[0]
    tm = TOK_TILE

    def loss_of(xv, g, t):
        e = _rms(xv, g) - t
        return 0.5 * jnp.sum(jnp.sum(e * e, axis=-1, keepdims=True) * (1.0 / D_MODEL), axis=0, keepdims=True)

    def body(x_ref, nf_ref, t_ref, loss_ref, dx_ref, dnf_ref):
        loss, vjp = jax.vjp(functools.partial(loss_of, t=t_ref[...]), x_ref[...], nf_ref[...])
        dx, dnf = vjp(jnp.ones_like(loss))
        dx_ref[...] = dx

        @pl.when(pl.program_id(0) == 0)
        def _():
            dnf_ref[...] = jnp.zeros_like(dnf_ref)
            loss_ref[...] = jnp.zeros_like(loss_ref)

        dnf_ref[...] += dnf
        loss_ref[...] += jnp.broadcast_to(loss, loss_ref.shape)

    return _call(
        body, name, (n_tok // tm,),
        [_rows(tm, D_MODEL), _const((1, D_MODEL)), _rows(tm, D_MODEL)],
        [_full((SUBLANE, LANE)), _rows(tm, D_MODEL), _full((1, D_MODEL))],
        [_sds((SUBLANE, LANE)), _sds((n_tok, D_MODEL)), _sds((1, D_MODEL))],
    )(x, nf, target)


GRAD_WIRE = BF16


def _matmul_tn(a, b, name):
    n_tok, k1 = a.shape
    k2 = b.shape[1]
    t1 = _tile_of(k1)

    def body(a_ref, b_ref, o_ref):
        o_ref[...] = _tn(a_ref[...], b_ref[...]).astype(GRAD_WIRE)

    return _call(body, name, (k1 // t1,), [pl.BlockSpec((n_tok, t1), lambda i: (0, i)), _const((n_tok, k2))],
                 [pl.BlockSpec((t1, k2), lambda i: (i, 0))], [_sds((k1, k2), GRAD_WIRE)])(a, b)[0]


def _matmul_tn_lhs_blocks(a, b, width, keep, name):
    n_tok, k1 = a.shape
    k2 = b.shape[1]

    def body(a_ref, b_ref, o_ref):
        o_ref[...] = _tn(a_ref[...], b_ref[...])[:keep, :].astype(GRAD_WIRE)

    return _call(body, name, (k1 // width,), [pl.BlockSpec((n_tok, width), lambda d: (0, d)), _const((n_tok, k2))],
                 [pl.BlockSpec((None, keep, k2), lambda d: (d, 0, 0))], [_sds((k1 // width, keep, k2), GRAD_WIRE)])(a, b)[0]


def _matmul_tn_pair(a0, a1, b, name):
    n_tok, k1 = a0.shape
    k2 = b.shape[1]
    t1 = _tile_of(k1)
    n1 = k1 // t1

    def body(a0_ref, a1_ref, b_ref, o_ref):
        @pl.when(pl.program_id(0) < n1)
        def _():
            o_ref[...] = _tn(a0_ref[...], b_ref[...]).astype(GRAD_WIRE)

        @pl.when(pl.program_id(0) >= n1)
        def _():
            o_ref[...] = _tn(a1_ref[...], b_ref[...]).astype(GRAD_WIRE)

    return _call(
        body, name, (2 * n1,),
        [pl.BlockSpec((n_tok, t1), lambda i: (0, jnp.minimum(i, n1 - 1))),
         pl.BlockSpec((n_tok, t1), lambda i: (0, jnp.maximum(i - n1, 0))), _const((n_tok, k2))],
        [pl.BlockSpec((None, t1, k2), lambda i: (i // n1, i % n1, 0))], [_sds((2, k1, k2), GRAD_WIRE)])(a0, a1, b)[0]


W_IN_BLOCK = IN_PROJ // N_DEV
W_IN_SPLITS = (D_MODEL, 2 * D_MODEL, 2 * D_MODEL + SSD_CONV_DIM)
RELAYOUT_TILE = 256


def _w_in_split(blocks, after, name):
    tr = RELAYOUT_TILE

    def body(b_ref, after_ref, wu_ref, wz_ref, wx_ref, wd_ref):
        full = jnp.concatenate([b_ref[d] for d in range(N_DEV)], axis=1)
        wu_ref[...] = full[:, :W_IN_SPLITS[0]]
        wz_ref[...] = full[:, W_IN_SPLITS[0]:W_IN_SPLITS[1]]
        wx_ref[...] = full[:, W_IN_SPLITS[1]:W_IN_SPLITS[2]]
        wd_ref[...] = jnp.concatenate([full[:, W_IN_SPLITS[2]:], jnp.zeros((tr, LANE - SSD_HEADS), full.dtype)], axis=1)

    return _call(
        body, name, (D_MODEL // tr,), [pl.BlockSpec((N_DEV, tr, W_IN_BLOCK), lambda i: (0, i, 0)), HBM],
        [_rows(tr, D_MODEL), _rows(tr, D_MODEL), _rows(tr, SSD_CONV_DIM), _rows(tr, LANE)],
        [_sds((D_MODEL, D_MODEL), BF16), _sds((D_MODEL, D_MODEL), BF16), _sds((D_MODEL, SSD_CONV_DIM), BF16),
         _sds((D_MODEL, LANE), BF16)],
    )(blocks, after)


def _w_in_grad_blocks(gu, gz, gx, gdt, name):
    tr = RELAYOUT_TILE

    def body(gu_ref, gz_ref, gx_ref, gdt_ref, o_ref):
        full = jnp.concatenate([gu_ref[...], gz_ref[...], gx_ref[...], gdt_ref[...]], axis=1)
        for d in range(N_DEV):
            o_ref[d] = full[:, d * W_IN_BLOCK:(d + 1) * W_IN_BLOCK]

    return _call(
        body, name, (D_MODEL // tr,),
        [_rows(tr, D_MODEL), _rows(tr, D_MODEL), _rows(tr, SSD_CONV_DIM), _rows(tr, LANE)],
        [pl.BlockSpec((N_DEV, tr, W_IN_BLOCK), lambda i: (0, i, 0))], [_sds((N_DEV, D_MODEL, W_IN_BLOCK), gu.dtype)],
    )(gu, gz, gx, gdt)[0]


S5_SLICES = D_MODEL // LANE
S5_SLICE_STATES = S5_LANES // S5_SLICES
SCAN_LANES = 512


def _s5_scan(br_ref, bi_ref, a_r, a_i, car_r, car_i, ini_r, ini_i, reverse, xr_ref=None, xi_ref=None,
             acc_r=None, acc_i=None, row0=0):
    n_rows = S5_TILE
    seg = n_rows // SUBLANE
    order = range(SUBLANE - 1, -1, -1) if reverse else range(SUBLANE)

    def rows(t):
        return pl.ds(pl.multiple_of(row0 + ((seg - 1 - t) if reverse else t) * SUBLANE, SUBLANE), SUBLANE)

    tiles_per = SCAN_LANES // LANE

    def load(ref, t, lb):
        return jnp.concatenate([ref[lb * tiles_per + j, rows(t), :] for j in range(tiles_per)], axis=1)

    def store(ref, t, lb, val):
        for j in range(tiles_per):
            ref[lb * tiles_per + j, rows(t), :] = val[:, j * LANE:(j + 1) * LANE]

    for lb in range(S5_LANES // SCAN_LANES):
        lanes = pl.ds(lb * SCAN_LANES, SCAN_LANES)
        ar1, ai1 = a_r[:, lb * SCAN_LANES:(lb + 1) * SCAN_LANES], a_i[:, lb * SCAN_LANES:(lb + 1) * SCAN_LANES]
        ar8 = jnp.broadcast_to(ar1, (SUBLANE, SCAN_LANES))
        ai8 = jnp.broadcast_to(ai1, (SUBLANE, SCAN_LANES))

        def local(t, c):
            sr, si = c
            return (ar8 * sr - ai8 * si + load(br_ref, t, lb), ar8 * si + ai8 * sr + load(bi_ref, t, lb))

        zero = jnp.zeros((SUBLANE, SCAN_LANES), F32)
        er, ei = lax.fori_loop(0, seg, local, (zero, zero))
        pr, pi = ar1, ai1
        for _ in range(seg.bit_length() - 1):
            pr, pi = pr * pr - pi * pi, 2.0 * pr * pi
        cr, ci = car_r[:, lanes], car_i[:, lanes]
        for s in order:
            ini_r[s:s + 1, lanes] = cr
            ini_i[s:s + 1, lanes] = ci
            cr, ci = pr * cr - pi * ci + er[s:s + 1, :], pr * ci + pi * cr + ei[s:s + 1, :]
        car_r[:, lanes] = cr
        car_i[:, lanes] = ci

        if xr_ref is None:
            def final(t, c):
                sr, si = c
                nr = ar8 * sr - ai8 * si + load(br_ref, t, lb)
                ni = ar8 * si + ai8 * sr + load(bi_ref, t, lb)
                store(br_ref, t, lb, nr)
                store(bi_ref, t, lb, ni)
                return nr, ni

            lax.fori_loop(0, seg, final, (ini_r[:, lanes], ini_i[:, lanes]))
        else:
            def final_acc(t, c):
                sr, si, gr, gi = c
                xr, xi = load(xr_ref, t, lb), load(xi_ref, t, lb)
                gr = gr + sr * xr + si * xi
                gi = gi + si * xr - sr * xi
                nr = ar8 * sr - ai8 * si + load(br_ref, t, lb)
                ni = ar8 * si + ai8 * sr + load(bi_ref, t, lb)
                store(br_ref, t, lb, nr)
                store(bi_ref, t, lb, ni)
                return nr, ni, gr, gi

            _, _, gr, gi = lax.fori_loop(0, seg, final_acc,
                                         (ini_r[:, lanes], ini_i[:, lanes], acc_r[:, lanes], acc_i[:, lanes]))
            acc_r[:, lanes] = gr
            acc_i[:, lanes] = gi


def _s5_tail(gg, q, sn):
    return _rms(gg * _sigmoid(q), sn)


def _scan_order(n_rows):
    seg = n_rows // SUBLANE
    r = lax.broadcasted_iota(jnp.int32, (n_rows, n_rows), 0)
    c = lax.broadcasted_iota(jnp.int32, (n_rows, n_rows), 1)
    return (c == (r % SUBLANE) * seg + r // SUBLANE).astype(F32)


S5_STATE_TILES = S5_LANES // LANE
TILES_PER_SLICE = S5_SLICE_STATES // LANE


def _put_states(ref, k, val):
    for j in range(TILES_PER_SLICE):
        ref[k * TILES_PER_SLICE + j] = val[:, j * LANE:(j + 1) * LANE]


def _get_states(ref, k):
    return jnp.concatenate([ref[k * TILES_PER_SLICE + j] for j in range(TILES_PER_SLICE)], axis=1)


def _state_rows(tile, n_tiles=None):
    if n_tiles is None:
        return pl.BlockSpec((S5_STATE_TILES, tile, LANE), lambda i: (0, i, 0))
    return pl.BlockSpec((S5_STATE_TILES, tile, LANE), lambda i: (0, n_tiles - 1 - i, 0))


def _s5_fwd(u, a_r, a_i, bdb, bcr, bci, dsk, wglu, bglu, sn, name):
    n_tok = u.shape[0]
    n_sub = 2
    tc = n_sub * S5_TILE
    sw = S5_SLICE_STATES

    def body(u_ref, ar_ref, ai_ref, bdb_ref, bcr_ref, bci_ref, d_ref, wg_ref, bg_ref, sn_ref,
             ya_ref, xr_ref, xi_ref, v_ref, car_r, car_i, ini_r, ini_i):
        @pl.when(pl.program_id(0) == 0)
        def _():
            car_r[...] = jnp.zeros_like(car_r)
            car_i[...] = jnp.zeros_like(car_i)

        order = _scan_order(S5_TILE)
        u_t = jnp.concatenate([_nn_f32(order, u_ref[s * S5_TILE:(s + 1) * S5_TILE, :]) for s in range(n_sub)], axis=0)
        ub = u_t.astype(BF16)
        for k in range(S5_SLICES):
            bu = _nn(ub[:, k * LANE:(k + 1) * LANE], bdb_ref[k])
            _put_states(xr_ref, k, bu[:, :sw])
            _put_states(xi_ref, k, bu[:, sw:])
        for s in range(n_sub):
            _s5_scan(xr_ref, xi_ref, ar_ref[...], ai_ref[...], car_r, car_i, ini_r, ini_i, reverse=False, row0=s * S5_TILE)
        vs = [_nn(_get_states(xr_ref, k), bcr_ref[k]) - _nn(_get_states(xi_ref, k), bci_ref[k])
              for k in range(S5_SLICES)]
        v = jnp.concatenate(vs, axis=1) + d_ref[...] * u_t
        v_ref[...] = v
        gg = _gelu(v)
        ya = _s5_tail(gg, _nn(gg, wg_ref[...]) + bg_ref[...], sn_ref[...])
        for s in range(n_sub):
            rows = slice(s * S5_TILE, (s + 1) * S5_TILE)
            ya_ref[rows, :] = _tn_f32(order, ya[rows, :]).astype(BF16)

    return _call(
        body, name, (n_tok // tc,),
        [_rows(tc, D_MODEL), _const((1, S5_LANES)), _const((1, S5_LANES)), _const(bdb.shape), _const(bcr.shape),
         _const(bci.shape), _const((1, D_MODEL)), _const(wglu.shape), _const((1, D_MODEL)), _const((1, D_MODEL))],
        [_rows(tc, D_MODEL), _state_rows(tc), _state_rows(tc), _rows(tc, D_MODEL)],
        [_sds((n_tok, D_MODEL), BF16), _sds((S5_STATE_TILES, n_tok, LANE)), _sds((S5_STATE_TILES, n_tok, LANE)),
         _sds((n_tok, D_MODEL))],
        scratch=[pltpu.VMEM((1, S5_LANES), F32), pltpu.VMEM((1, S5_LANES), F32),
                 pltpu.VMEM((SUBLANE, S5_LANES), F32), pltpu.VMEM((SUBLANE, S5_LANES), F32)],
    )(u, a_r, a_i, bdb, bcr, bci, dsk, wglu, bglu, sn)


def _s5_bwd(dya, v, u, xr, xi, a_r, a_i, bdb, bcr, bci, dsk, wglu, bglu, sn, after, name):
    n_tok = u.shape[0]
    tc = S5_TILE
    nt = n_tok // tc
    sw = S5_SLICE_STATES

    def body(dya_ref, v_ref, u_ref, xr_ref, xi_ref, ar_ref, ai_ref, bdb_ref, bcr_ref, bci_ref, d_ref, wg_ref, bg_ref, sn_ref,
             after_ref, du_ref, gg_ref, dq_ref, gbdb_ref, gbcr_ref, gbci_ref, gar_ref, gai_ref, gd_ref, gbg_ref, gsn_ref,
             gr_ref, gi_ref, car_r, car_i, ini_r, ini_i):
        @pl.when(pl.program_id(0) == 0)
        def _():
            for r in (car_r, car_i, gbdb_ref, gbcr_ref, gbci_ref, gar_ref, gai_ref, gd_ref, gbg_ref, gsn_ref):
                r[...] = jnp.zeros_like(r)

        order = _scan_order(tc)
        u_t = _nn_f32(order, u_ref[...])
        gg, gelu_vjp = jax.vjp(_gelu, v_ref[...])
        _, tail_vjp = jax.vjp(_s5_tail, gg, _nn(gg, wg_ref[...]) + bg_ref[...], sn_ref[...])
        dgg, dq, dsn = tail_vjp(_nn_f32(order, dya_ref[...]))
        (dv,) = gelu_vjp(dgg + _nt(dq, wg_ref[...]))
        gg_ref[...] = gg.astype(BF16)
        dq_ref[...] = dq.astype(BF16)
        gd_ref[...] += jnp.sum(dv * u_t, axis=0, keepdims=True)
        gbg_ref[...] += jnp.sum(dq, axis=0, keepdims=True)
        gsn_ref[...] += dsn
        dvb = dv.astype(BF16)
        for k in range(S5_SLICES):
            dvk = dvb[:, k * LANE:(k + 1) * LANE]
            _put_states(gr_ref, k, _nt(dvk, bcr_ref[k]))
            _put_states(gi_ref, k, -_nt(dvk, bci_ref[k]))
            gbcr_ref[k] += _tn(_get_states(xr_ref, k), dvk)
            gbci_ref[k] -= _tn(_get_states(xi_ref, k), dvk)
        _s5_scan(gr_ref, gi_ref, ar_ref[...], -ai_ref[...], car_r, car_i, ini_r, ini_i, reverse=True,
                 xr_ref=xr_ref, xi_ref=xi_ref, acc_r=gar_ref, acc_i=gai_ref)
        ub = u_t.astype(BF16)
        dus = []
        for k in range(S5_SLICES):
            gk_r, gk_i = _get_states(gr_ref, k).astype(BF16), _get_states(gi_ref, k).astype(BF16)
            bk = bdb_ref[k]
            dus.append(_nt(gk_r, bk[:, :sw]) + _nt(gk_i, bk[:, sw:]))
            uk = ub[:, k * LANE:(k + 1) * LANE]
            gbdb_ref[k, :, :sw] += _tn(uk, gk_r)
            gbdb_ref[k, :, sw:] += _tn(uk, gk_i)
        du_ref[...] = _tn_f32(order, jnp.concatenate(dus, axis=1) + d_ref[...] * dv).astype(BF16)

    rev = functools.partial(_rows, n_tiles=nt)
    return _call(
        body, name, (nt,),
        [rev(tc, D_MODEL), rev(tc, D_MODEL), rev(tc, D_MODEL), _state_rows(tc, nt), _state_rows(tc, nt),
         _const((1, S5_LANES)), _const((1, S5_LANES)), _const(bdb.shape), _const(bcr.shape), _const(bci.shape),
         _const((1, D_MODEL)), _const(wglu.shape), _const((1, D_MODEL)), _const((1, D_MODEL)), HBM],
        [rev(tc, D_MODEL), rev(tc, D_MODEL), rev(tc, D_MODEL), _full(bdb.shape), _full(bcr.shape), _full(bci.shape),
         _full((SUBLANE, S5_LANES)), _full((SUBLANE, S5_LANES)), _full((1, D_MODEL)), _full((1, D_MODEL)), _full((1, D_MODEL))],
        [_sds((n_tok, D_MODEL), BF16), _sds((n_tok, D_MODEL), BF16), _sds((n_tok, D_MODEL), BF16), _sds(bdb.shape), _sds(bcr.shape),
         _sds(bci.shape), _sds((SUBLANE, S5_LANES)), _sds((SUBLANE, S5_LANES)), _sds((1, D_MODEL)), _sds((1, D_MODEL)),
         _sds((1, D_MODEL))],
        scratch=[pltpu.VMEM((S5_STATE_TILES, tc, LANE), F32), pltpu.VMEM((S5_STATE_TILES, tc, LANE), F32),
                 pltpu.VMEM((1, S5_LANES), F32), pltpu.VMEM((1, S5_LANES), F32),
                 pltpu.VMEM((SUBLANE, S5_LANES), F32), pltpu.VMEM((SUBLANE, S5_LANES), F32)],
    )(dya, v, u, xr, xi, a_r, a_i, bdb, bcr, bci, dsk, wglu, bglu, sn, after)


SSD_WIDTH = SSD_HEADS * SSD_HEAD_DIM
SSD_GROUPS = 2
HEADS_PER_GROUP = SSD_HEADS // SSD_GROUPS


def _take(x, axis, start, size):
    n = x.shape[axis]

    def sl(v):
        return lax.slice_in_dim(v, start, start + size, axis=axis)

    @jax.custom_vjp
    def f(v):
        return sl(v)

    def bwd(_, g):
        parts = []
        if start:
            parts.append(jnp.zeros(g.shape[:axis] + (start,) + g.shape[axis + 1:], g.dtype))
        parts.append(g)
        if n - start - size:
            parts.append(jnp.zeros(g.shape[:axis] + (n - start - size,) + g.shape[axis + 1:], g.dtype))
        return (jnp.concatenate(parts, axis=axis) if len(parts) > 1 else g,)

    f.defvjp(lambda v: (sl(v), None), bwd)
    return f(x)


def _lane_of(x, h):
    col = lax.broadcasted_iota(jnp.int32, x.shape, 1)
    return jnp.sum(jnp.where(col == h, x, 0.0), axis=1, keepdims=True)


def _ssd_chunk(xc, z, dt, dtb, alog, dvec, gn, st, nn, nt, tn, cumsum, take):
    t_len = xc.shape[0]
    xa = _silu(xc)
    dtp = _softplus(dt + dtb)
    d_a = dtp * (-jnp.exp(alog))
    row = lax.broadcasted_iota(jnp.int32, (t_len, t_len), 0)
    col = lax.broadcasted_iota(jnp.int32, (t_len, t_len), 1)
    causal = row >= col
    cum = cumsum(causal.astype(F32), d_a)
    eye = (row == col).astype(F32)
    group_width = HEADS_PER_GROUP * SSD_HEAD_DIM
    ys, sts = [], []
    for g in range(SSD_GROUPS):
        bg = take(xa, 1, SSD_WIDTH + g * SSD_STATE, SSD_STATE)
        cg = take(xa, 1, SSD_WIDTH + (SSD_GROUPS + g) * SSD_STATE, SSD_STATE)
        cb = nt(cg, bg)
        x_g = take(xa, 1, g * group_width, group_width)
        st_g = take(st, 0, g * group_width, group_width)
        diag, dt_l, d_l, grow_l, keep_l, last_r = [], [], [], [], [], []
        for r in range(HEADS_PER_GROUP):
            h = g * HEADS_PER_GROUP + r
            cc = _lane_of(cum, h)
            cr = jnp.sum(cc * eye, axis=0, keepdims=True)
            decay = jnp.exp(jnp.where(causal, cc - cr, -1e30))
            dt_h = _lane_of(dtp, h)
            c_last = jnp.sum(jnp.where(row[:, :1] == t_len - 1, cc, 0.0), axis=0, keepdims=True)
            lanes = (t_len, SSD_HEAD_DIM)
            diag.append(nn(cb * decay, take(x_g, 1, r * SSD_HEAD_DIM, SSD_HEAD_DIM) * dt_h))
            dt_l.append(jnp.broadcast_to(dt_h, lanes))
            d_l.append(jnp.broadcast_to(_lane_of(dvec, h), lanes))
            grow_l.append(jnp.broadcast_to(jnp.exp(cc), lanes))
            keep_l.append(jnp.broadcast_to(jnp.exp(c_last - cc), lanes))
            last_r.append(jnp.broadcast_to(jnp.exp(c_last), (SSD_HEAD_DIM, SSD_STATE)))
        side = functools.partial(jnp.concatenate, axis=1)
        xdt_g = x_g * side(dt_l)
        ys.append(side(diag) + side(grow_l) * nt(cg, st_g) + side(d_l) * x_g)
        sts.append(jnp.concatenate(last_r, axis=0) * st_g + tn(xdt_g * side(keep_l), bg))
    y = jnp.concatenate(ys, axis=1) * _silu(z)
    return _rms(y, gn), jnp.concatenate(sts, axis=0)


SSD_TILE = SSD_CHUNK


def _ssd_tile(xc, z, dt, dtb, alog, dvec, gn, st, nn, nt, tn, cumsum, take):
    ys = []
    for c in range(xc.shape[0] // SSD_CHUNK):
        rows = (c * SSD_CHUNK, SSD_CHUNK)
        y, st = _ssd_chunk(take(xc, 0, *rows), take(z, 0, *rows), take(dt, 0, *rows), dtb, alog, dvec, gn, st,
                           nn, nt, tn, cumsum, take)
        ys.append(y)
    return jnp.concatenate(ys, axis=0), st


def _shift_back(cur, prev, j):
    if j == 0:
        return cur
    row = lax.broadcasted_iota(jnp.int32, cur.shape, 0)
    return jnp.where(row < j, pltpu.roll(prev, j, 0), pltpu.roll(cur, j, 0))


def _shift_ahead(cur, nxt, j):
    if j == 0:
        return cur
    n = cur.shape[0]
    row = lax.broadcasted_iota(jnp.int32, cur.shape, 0)
    return jnp.where(row >= n - j, pltpu.roll(nxt, n - j, 0), pltpu.roll(cur, n - j, 0))


def _conv(cur, prev, w, b):
    out = b + w[SSD_CONV - 1:SSD_CONV, :] * cur
    for k in range(SSD_CONV - 1):
        out = out + w[k:k + 1, :] * _shift_back(cur, prev, SSD_CONV - 1 - k)
    return out


def _ssd_fwd(xbc, z, dt, conv_w, conv_b, dtb, alog, dvec, gn, name):
    n_tok = xbc.shape[0]
    tc = SSD_TILE
    nc = n_tok // tc
    st_rows = SSD_HEADS * SSD_HEAD_DIM

    def body(cur_ref, prev_ref, z_ref, dt_ref, w_ref, b_ref, dtb_ref, alog_ref, dvec_ref, gn_ref,
             yb_ref, stin_ref, st_ref):
        i = pl.program_id(0)

        @pl.when(i == 0)
        def _():
            st_ref[...] = jnp.zeros_like(st_ref)

        prev = jnp.where(i > 0, prev_ref[...], 0.0)
        xc = _conv(cur_ref[...], prev, w_ref[...], b_ref[...])
        st = st_ref[...]
        stin_ref[0] = st
        yb, st_new = _ssd_tile(xc, z_ref[...], dt_ref[...], dtb_ref[...], alog_ref[...], dvec_ref[...], gn_ref[...], st,
                               _nn, _nt, _tn, _nn_f32, lambda v, axis, start, size: lax.slice_in_dim(v, start, start + size, axis=axis))
        yb_ref[...] = yb.astype(BF16)
        st_ref[...] = st_new

    return _call(
        body, name, (nc,),
        [_rows(tc, SSD_CONV_DIM), pl.BlockSpec((tc, SSD_CONV_DIM), lambda i: (jnp.maximum(i - 1, 0), 0)),
         _rows(tc, D_MODEL), _rows(tc, LANE), _const((SSD_CONV, SSD_CONV_DIM)), _const((1, SSD_CONV_DIM)),
         _const((1, LANE)), _const((1, LANE)), _const((1, LANE)), _const((1, D_MODEL))],
        [_rows(tc, D_MODEL), pl.BlockSpec((1, st_rows, SSD_STATE), lambda i: (i, 0, 0))],
        [_sds((n_tok, D_MODEL), BF16), _sds((nc, st_rows, SSD_STATE))],
        scratch=[pltpu.VMEM((st_rows, SSD_STATE), F32)],
    )(xbc, xbc, z, dt, conv_w, conv_b, dtb, alog, dvec, gn)


def _ssd_bwd(dyb, xbc, z, dt, stin, conv_w, conv_b, dtb, alog, dvec, gn, name):
    n_tok = xbc.shape[0]
    tc = SSD_TILE
    nc = n_tok // tc
    st_rows = SSD_HEADS * SSD_HEAD_DIM

    def body(dyb_ref, cur_ref, prev_ref, z_ref, dt_ref, stin_ref, w_ref, b_ref, dtb_ref, alog_ref, dvec_ref, gn_ref,
             dxbc_ref, dz_ref, ddt_ref, gw_ref, gb_ref, gdtb_ref, galog_ref, gdvec_ref, ggn_ref,
             dst_ref, dxc_next_ref):
        i = pl.program_id(0)

        @pl.when(i == 0)
        def _():
            for r in (dst_ref, dxc_next_ref, gw_ref, gb_ref, gdtb_ref, galog_ref, gdvec_ref, ggn_ref):
                r[...] = jnp.zeros_like(r)

        cur = cur_ref[...]
        prev = jnp.where(i < nc - 1, prev_ref[...], 0.0)
        w = w_ref[...]
        xc = _conv(cur, prev, w, b_ref[...])
        chunk = functools.partial(_ssd_tile, nn=_nn_d, nt=_nt_d, tn=_tn_d, cumsum=_cumsum_rows, take=_take)
        _, vjp = jax.vjp(chunk, xc, z_ref[...], dt_ref[...], dtb_ref[...], alog_ref[...], dvec_ref[...], gn_ref[...],
                         stin_ref[0])
        dxc, dz, ddt, gdtb, galog, gdvec, ggn, dst = vjp((dyb_ref[...], dst_ref[...]))
        dst_ref[...] = dst
        dz_ref[...] = dz.astype(BF16)
        ddt_ref[...] = ddt.astype(BF16)
        gdtb_ref[...] += gdtb
        galog_ref[...] += galog
        gdvec_ref[...] += gdvec
        ggn_ref[...] += ggn
        dxc_next = dxc_next_ref[...]
        dxbc = w[SSD_CONV - 1:SSD_CONV, :] * dxc
        gws = []
        for k in range(SSD_CONV - 1):
            j = SSD_CONV - 1 - k
            dxbc = dxbc + w[k:k + 1, :] * _shift_ahead(dxc, dxc_next, j)
            gws.append(jnp.sum(dxc * _shift_back(cur, prev, j), axis=0, keepdims=True))
        gws.append(jnp.sum(dxc * cur, axis=0, keepdims=True))
        dxbc_ref[...] = dxbc.astype(BF16)
        gw_ref[...] += jnp.concatenate(gws, axis=0)
        gb_ref[...] += jnp.sum(dxc, axis=0, keepdims=True)
        dxc_next_ref[...] = dxc

    rev = functools.partial(_rows, n_tiles=nc)
    return _call(
        body, name, (nc,),
        [rev(tc, D_MODEL), rev(tc, SSD_CONV_DIM),
         pl.BlockSpec((tc, SSD_CONV_DIM), lambda i: (jnp.maximum(nc - 2 - i, 0), 0)),
         rev(tc, D_MODEL), rev(tc, LANE), pl.BlockSpec((1, st_rows, SSD_STATE), lambda i: (nc - 1 - i, 0, 0)),
         _const((SSD_CONV, SSD_CONV_DIM)), _const((1, SSD_CONV_DIM)), _const((1, LANE)), _const((1, LANE)),
         _const((1, LANE)), _const((1, D_MODEL))],
        [rev(tc, SSD_CONV_DIM), rev(tc, D_MODEL), rev(tc, LANE), _full((SSD_CONV, SSD_CONV_DIM)), _full((1, SSD_CONV_DIM)),
         _full((1, LANE)), _full((1, LANE)), _full((1, LANE)), _full((1, D_MODEL))],
        [_sds((n_tok, SSD_CONV_DIM), BF16), _sds((n_tok, D_MODEL), BF16), _sds((n_tok, LANE), BF16), _sds((SSD_CONV, SSD_CONV_DIM)),
         _sds((1, SSD_CONV_DIM)), _sds((1, LANE)), _sds((1, LANE)), _sds((1, LANE)), _sds((1, D_MODEL))],
        scratch=[pltpu.VMEM((st_rows, SSD_STATE), F32), pltpu.VMEM((tc, SSD_CONV_DIM), F32)],
    )(dyb, xbc, xbc, z, dt, stin, conv_w, conv_b, dtb, alog, dvec, gn)


@jax.custom_vjp
def _expand_cols(x, e):
    return _nn_f32(x, e)


_expand_cols.defvjp(
    lambda x, e: (_nn_f32(x, e), e),
    lambda e, g: (lax.dot_general(g, e, (((1,), (1,)), ((), ())), precision=lax.Precision.HIGHEST,
                                  preferred_element_type=F32), jnp.zeros_like(e)))


def _s5_discretize(lam_re, lam_im, log_step, b_re, b_im, expand):
    step = jnp.exp(log_step)
    mag = jnp.exp(lam_re * step)
    ang = lam_im * step
    a_r = mag * jnp.cos(ang)
    a_i = mag * jnp.sin(ang)
    den = lam_re * lam_re + lam_im * lam_im
    n_r = a_r - 1.0
    coef_r = _expand_cols((n_r * lam_re + a_i * lam_im) / den, expand)
    coef_i = _expand_cols((a_i * lam_re - n_r * lam_im) / den, expand)
    return a_r, a_i, coef_r * b_re - coef_i * b_im, coef_r * b_im + coef_i * b_re


def _expand_matrix():
    p = lax.broadcasted_iota(jnp.int32, (S5_STATE, S5_STATE * S5_GROUP), 0)
    c = lax.broadcasted_iota(jnp.int32, (S5_STATE, S5_STATE * S5_GROUP), 1)
    return (c // S5_GROUP == p).astype(F32)


def _s5_discretize_fwd(lam_re, lam_im, log_step, b_re, b_im, name):
    def body(lr_ref, li_ref, ls_ref, br_ref, bi_ref, ar_ref, ai_ref, bbr_ref, bbi_ref):
        outs = _s5_discretize(lr_ref[...], li_ref[...], ls_ref[...], br_ref[...], bi_ref[...], _expand_matrix())
        for r, o in zip((ar_ref, ai_ref, bbr_ref, bbi_ref), outs):
            r[...] = o

    sq, wide = (S5_GROUPS, S5_STATE), (S5_GROUPS, S5_STATE * S5_GROUP)
    return _call(body, name, (1,), [_full(sq), _full(sq), _full((S5_GROUPS, 1)), _full(wide), _full(wide)],
                 [_full(sq), _full(sq), _full(wide), _full(wide)], [_sds(sq), _sds(sq), _sds(wide), _sds(wide)],
                 )(lam_re, lam_im, log_step, b_re, b_im)


def _s5_discretize_bwd(lam_re, lam_im, log_step, b_re, b_im, g_ar8, g_ai8, g_bbr, g_bbi, after, name):
    def body(lr_ref, li_ref, ls_ref, br_ref, bi_ref, gar_ref, gai_ref, gbbr_ref, gbbi_ref, after_ref,
             glr_ref, gli_ref, gls_ref, gbr_ref, gbi_ref):
        _, vjp = jax.vjp(functools.partial(_s5_discretize, expand=_expand_matrix()),
                         lr_ref[...], li_ref[...], ls_ref[...], br_ref[...], bi_ref[...])
        grads = vjp((jnp.sum(gar_ref[...], axis=0), jnp.sum(gai_ref[...], axis=0), gbbr_ref[...], gbbi_ref[...]))
        for r, g in zip((glr_ref, gli_ref, gls_ref, gbr_ref, gbi_ref), grads):
            r[...] = g

    sq, wide, col = (S5_GROUPS, S5_STATE), (S5_GROUPS, S5_STATE * S5_GROUP), (S5_GROUPS, 1)
    part = (SUBLANE,) + sq
    return _call(body, name, (1,),
                 [_full(sq), _full(sq), _full(col), _full(wide), _full(wide), _full(part), _full(part), _full(wide), _full(wide),
                  HBM],
                 [_full(sq), _full(sq), _full(col), _full(wide), _full(wide)],
                 [_sds(sq), _sds(sq), _sds(col), _sds(wide), _sds(wide)],
                 )(lam_re, lam_im, log_step, b_re, b_im, g_ar8, g_ai8, g_bbr, g_bbi, after)


GROUPS_PER_SLICE = LANE // S5_GROUP


def _block_diag_b(bb):
    t = bb.reshape(S5_SLICES, GROUPS_PER_SLICE, S5_STATE, S5_GROUP)
    eye = jnp.eye(GROUPS_PER_SLICE, dtype=bb.dtype)
    return jnp.einsum("kgph,gf->kghfp", t, eye).reshape(S5_SLICES, LANE, S5_SLICE_STATES)


def _block_diag_b_inv(m):
    t = m.reshape(S5_SLICES, GROUPS_PER_SLICE, S5_GROUP, GROUPS_PER_SLICE, S5_STATE)
    return jnp.einsum("kghgp->kgph", t).reshape(S5_GROUPS, S5_STATE * S5_GROUP)


def _block_diag_c(c):
    t = c.reshape(S5_SLICES, GROUPS_PER_SLICE, S5_GROUP, S5_STATE)
    eye = jnp.eye(GROUPS_PER_SLICE, dtype=c.dtype)
    return jnp.einsum("kghp,gf->kgpfh", t, eye).reshape(S5_SLICES, S5_SLICE_STATES, LANE)


def _block_diag_c_inv(m):
    t = m.reshape(S5_SLICES, GROUPS_PER_SLICE, S5_STATE, GROUPS_PER_SLICE, S5_GROUP)
    return jnp.einsum("kgpgh->kghp", t).reshape(S5_GROUPS, S5_GROUP, S5_STATE)


def _pad_lanes(v):
    return jnp.pad(v.reshape(1, -1), ((0, 0), (0, LANE - v.shape[0])))


def _prepare_layer(w, blk, i, after):
    p = {}
    p["wu"], p["wz"], p["wx"], p["wd"] = _w_in_split(blk["w_in"], after, name=f"w_in_split_{i}")
    p["nm"] = w["norm_mix"][i].reshape(1, D_MODEL)
    p["lam_re"], p["lam_im"] = w["s5_lam_re"][i], w["s5_lam_im"][i]
    p["log_step"] = w["s5_log_step"][i].reshape(S5_GROUPS, 1)
    p["b_re"] = w["s5_b_re"][i].reshape(S5_GROUPS, S5_STATE * S5_GROUP)
    p["b_im"] = w["s5_b_im"][i].reshape(S5_GROUPS, S5_STATE * S5_GROUP)
    a_r, a_i, bb_r, bb_i = _s5_discretize_fwd(p["lam_re"], p["lam_im"], p["log_step"], p["b_re"], p["b_im"],
                                              name=f"s5_discretize_{i}")
    p["a_r"], p["a_i"] = a_r.reshape(1, S5_LANES), a_i.reshape(1, S5_LANES)
    p["bdb"] = jnp.concatenate([_block_diag_b(bb_r), _block_diag_b(bb_i)], axis=2).astype(BF16)
    p["bcr"] = _block_diag_c(w["s5_c_re"][i]).astype(BF16)
    p["bci"] = _block_diag_c(w["s5_c_im"][i]).astype(BF16)
    p["dsk"] = w["s5_d"][i].reshape(1, D_MODEL)
    p["wglu"] = blk["s5_w_glu"].reshape(D_MODEL, D_MODEL)
    p["bglu"] = w["s5_b_glu"][i].reshape(1, D_MODEL)
    p["sn"] = w["s5_norm"][i].reshape(1, D_MODEL)
    p["conv_w"] = blk["ssd_conv_w"]
    p["conv_b"] = w["ssd_conv_b"][i].reshape(1, SSD_CONV_DIM)
    p["dtb"] = _pad_lanes(w["ssd_dt_bias"][i])
    p["alog"] = _pad_lanes(w["ssd_a_log"][i])
    p["dvec"] = _pad_lanes(w["ssd_d"][i])
    p["gn"] = w["ssd_norm"][i].reshape(1, D_MODEL)
    if "w_out" in blk:
        p.update(_late_weights(blk))
    p["nf"] = w["norm_ffn"][i].reshape(1, D_MODEL)
    return p


def _late_weights(blk):
    p = {}
    p["wo"] = blk["w_out"].reshape(2 * D_MODEL, D_MODEL)
    p["wg"], p["wup"], p["wdn"] = (blk[n].reshape(FFN_PAD, D_MODEL) for n in ("w_gate", "w_up", "w_down"))
    return p


def _layer_fwd(x0, p, i, late=None):
    u, z, xbc, dt = _inproj_fwd(x0, p["nm"], p["wu"], p["wz"], p["wx"], p["wd"], name=f"inproj_fwd_{i}")
    ya, xr, xi, v = _s5_fwd(u, p["a_r"], p["a_i"], p["bdb"], p["bcr"], p["bci"], p["dsk"], p["wglu"], p["bglu"], p["sn"],
                            name=f"s5_fwd_{i}")
    yb, stin = _ssd_fwd(xbc, z, dt, p["conv_w"], p["conv_b"], p["dtb"], p["alog"], p["dvec"], p["gn"], name=f"ssd_fwd_{i}")
    if late is not None:
        p.update(_late_weights(late(yb)))
    x1, x2 = _mix_ffn_fwd(x0, ya, yb, p["wo"], p["nf"], p["wg"], p["wup"], p["wdn"], name=f"mix_ffn_fwd_{i}")
    return x2, dict(x0=x0, u=u, z=z, xbc=xbc, dt=dt, xr=xr, xi=xi, v=v, stin=stin, ya=ya, yb=yb, x1=x1)


def _layer_bwd(dx2, s, p, i, after, between=None):
    g = {}
    dx1, dya, dyb, h2, act, dgt, dup, dx2b, dx1b, g_nf = _mix_ffn_bwd(
        s["x1"], dx2, p["wo"], p["nf"], p["wg"], p["wup"], p["wdn"], after, name=f"mix_ffn_bwd_{i}")
    g["norm_ffn"] = g_nf.reshape(D_MODEL)
    g["w_down"] = _matmul_tn_lhs_blocks(act, dx2b, FFN_BLOCK_PAD, FFN_BLOCK, name=f"grad_w_down_{i}")
    g["w_gate"] = _matmul_tn_lhs_blocks(dgt, h2, FFN_BLOCK_PAD, FFN_BLOCK, name=f"grad_w_gate_{i}")
    g["w_up"] = _matmul_tn_lhs_blocks(dup, h2, FFN_BLOCK_PAD, FFN_BLOCK, name=f"grad_w_up_{i}")
    g["w_out"] = _matmul_tn_pair(s["ya"], s["yb"], dx1b, name=f"grad_w_out_{i}").reshape(N_DEV, 2 * D_MODEL // N_DEV, D_MODEL)
    if between is not None:
        after = between(g)

    (du, gg, dq, g_bdb, g_bcr, g_bci, g_ar8, g_ai8, g_d, g_bglu, g_sn) = _s5_bwd(
        dya, s["v"], s["u"], s["xr"], s["xi"], p["a_r"], p["a_i"], p["bdb"], p["bcr"], p["bci"], p["dsk"], p["wglu"],
        p["bglu"], p["sn"], after, name=f"s5_bwd_{i}")
    g["s5_w_glu"] = _matmul_tn(gg, dq, name=f"grad_w_glu_{i}").reshape(N_DEV, D_MODEL // N_DEV, D_MODEL)
    g["s5_d"], g["s5_b_glu"], g["s5_norm"] = g_d.reshape(D_MODEL), g_bglu.reshape(D_MODEL), g_sn.reshape(D_MODEL)
    g["s5_raw"] = (g_bcr, g_bci, g_ar8, g_ai8, g_bdb)

    dxbc, dz, ddt, g_cw, g_cb, g_dtb, g_alog, g_dvec, g_gn = _ssd_bwd(
        dyb, s["xbc"], s["z"], s["dt"], s["stin"], p["conv_w"], p["conv_b"], p["dtb"], p["alog"], p["dvec"], p["gn"],
        name=f"ssd_bwd_{i}")
    g["ssd_conv_w"] = jnp.moveaxis(g_cw.reshape(SSD_CONV, N_DEV, SSD_CONV_DIM // N_DEV), 1, 0)
    g["ssd_conv_b"] = g_cb.reshape(SSD_CONV_DIM)
    g["ssd_dt_bias"], g["ssd_a_log"], g["ssd_d"] = g_dtb[0, :SSD_HEADS], g_alog[0, :SSD_HEADS], g_dvec[0, :SSD_HEADS]
    g["ssd_norm"] = g_gn.reshape(D_MODEL)

    dx0, h, g_nm = _inproj_bwd(s["x0"], p["nm"], du, dz, dxbc, ddt, dx1, p["wu"], p["wz"], p["wx"], p["wd"],
                               name=f"inproj_bwd_{i}")
    g["norm_mix"] = g_nm.reshape(D_MODEL)
    g["w_in"] = _w_in_grad_blocks(
        _matmul_tn(h, du, name=f"grad_w_in_u_{i}"), _matmul_tn(h, dz, name=f"grad_w_in_z_{i}"),
        _matmul_tn(h, dxbc, name=f"grad_w_in_xbc_{i}"), _matmul_tn(h, ddt, name=f"grad_w_in_dt_{i}"),
        name=f"grad_w_in_blocks_{i}")
    return dx0, g


def _s5_param_grads(g, p, i, after):
    g_bcr, g_bci, g_ar8, g_ai8, g_bdb = g.pop("s5_raw")
    g["s5_c_re"], g["s5_c_im"] = _block_diag_c_inv(g_bcr), _block_diag_c_inv(g_bci)
    sq = (SUBLANE, S5_GROUPS, S5_STATE)
    g_lr, g_li, g_ls, g_br, g_bi = _s5_discretize_bwd(
        p["lam_re"], p["lam_im"], p["log_step"], p["b_re"], p["b_im"], g_ar8.reshape(sq), g_ai8.reshape(sq),
        _block_diag_b_inv(g_bdb[:, :, :S5_SLICE_STATES]), _block_diag_b_inv(g_bdb[:, :, S5_SLICE_STATES:]), after,
        name=f"s5_discretize_bwd_{i}")
    g["s5_lam_re"], g["s5_lam_im"], g["s5_log_step"] = g_lr, g_li, g_ls.reshape(S5_GROUPS)
    b_shape = (S5_GROUPS, S5_STATE, S5_GROUP)
    g["s5_b_re"], g["s5_b_im"] = g_br.reshape(b_shape), g_bi.reshape(b_shape)


def _example_step(x, target, w, blks):
    prepared = [_prepare_layer(w, blks[i], i, x) for i in range(DEPTH)]
    saved = []
    h = x
    for i in range(DEPTH):
        h, s = _layer_fwd(h, prepared[i], i)
        saved.append(s)
    loss, dh, g_final = _loss_head(h, w["norm_final"].reshape(1, D_MODEL), target, name="loss_head")
    layer_grads = [None] * DEPTH
    for i in reversed(range(DEPTH)):
        dh, layer_grads[i] = _layer_bwd(dh, saved[i], prepared[i], i, x)
        _s5_param_grads(layer_grads[i], prepared[i], i, x)
    return loss, dh, layer_grads, g_final.reshape(D_MODEL)


def _mesh_position():
    return lax.axis_index("x"), lax.axis_index("y"), lax.axis_index("c")


def _peer(pos, k):
    x, y, c = pos
    px = 1 - x if k & 4 else x
    py = 1 - y if k & 2 else y
    pc = 1 - c if k & 1 else c
    return (px, py, pc), 4 * px + 2 * py + pc


HBM = pl.BlockSpec(memory_space=pl.ANY)


def _run_copies(local, remote):
    for cp in local + remote:
        cp.start()
    for cp in remote:
        cp.wait_recv()
    for cp in remote:
        cp.wait_send()
    for cp in local:
        cp.wait()


def _comm_scratch(n_units):
    return [pltpu.SemaphoreType.DMA((n_units, N_DEV - 1)), pltpu.SemaphoreType.DMA((n_units, N_DEV - 1)),
            pltpu.SemaphoreType.DMA((n_units,))]


def _gather_blocks(arrays, layered, name):
    units, out_shapes = [], []
    for j, (a, lay) in enumerate(zip(arrays, layered)):
        for layer in (range(a.shape[0]) if lay else (None,)):
            units.append((j, layer, len(out_shapes)))
            out_shapes.append(_sds((N_DEV,) + (a.shape[1:] if lay else a.shape), a.dtype))
    n_in = len(arrays)
    other_chips = (4, 2, 6)

    def body(*refs):
        ins, outs = refs[:n_in], refs[n_in:n_in + len(out_shapes)]
        send_sems, recv_sems, local_sems = refs[n_in + len(out_shapes):]
        pos = _mesh_position()
        me = 4 * pos[0] + 2 * pos[1] + pos[2]
        sibling, _ = _peer(pos, 1)
        local, own, passed = [], [], []
        for u, (j, layer, o) in enumerate(units):
            src = ins[j] if layer is None else ins[j].at[layer]
            local.append(pltpu.make_async_copy(src, outs[o].at[me], local_sems.at[u]))

            def copy(sem, src_ref, slot, to, u=u, o=o):
                return pltpu.make_async_remote_copy(
                    src_ref=src_ref, dst_ref=outs[o].at[slot], send_sem=send_sems.at[u, sem], recv_sem=recv_sems.at[u, sem],
                    device_id=to, device_id_type=MESH_ID)

            own.append([copy(0, src, me, sibling)] + [copy(1 + i, src, me, _peer(pos, k)[0]) for i, k in enumerate(other_chips)])
            passed.append([copy(4 + i, outs[o].at[_peer(pos, k)[1]], _peer(pos, k)[1], sibling) for i, k in enumerate(other_chips)])
        for cp in local + [c for unit in own for c in unit]:
            cp.start()
        for u in range(len(units)):
            for i in range(len(other_chips)):
                own[u][1 + i].wait_recv()
                passed[u][i].start()
        for u in range(len(units)):
            own[u][0].wait_recv()
            for cp in passed[u]:
                cp.wait_recv()
        for cp in [c for unit in own + passed for c in unit]:
            cp.wait_send()
        for cp in local:
            cp.wait()

    outs = pl.pallas_call(body, name=name, in_specs=[HBM] * n_in, out_specs=[HBM] * len(out_shapes), out_shape=out_shapes,
                          scratch_shapes=_comm_scratch(len(units)))(*arrays)
    grouped = [[] for _ in arrays]
    for j, _, o in units:
        grouped[j].append(outs[o])
    return [tuple(g) for g in grouped]


def _exchange_blocks(entries, name):
    units, flat_in, out_shapes = [], [], []
    for j, entry in enumerate(entries):
        for layer, a in enumerate(entry):
            units.append((len(flat_in), layer, j))
            flat_in.append(a)
        out_shapes.append(_sds((N_DEV, len(entry)) + entry[0].shape[1:], entry[0].dtype))
    n_in = len(flat_in)

    def body(*refs):
        ins, outs = refs[:n_in], refs[n_in:n_in + len(out_shapes)]
        send_sems, recv_sems, local_sems = refs[n_in + len(out_shapes):]
        pos = _mesh_position()
        me = 4 * pos[0] + 2 * pos[1] + pos[2]
        local, remote = [], []
        for u, (i, layer, o) in enumerate(units):
            local.append(pltpu.make_async_copy(ins[i].at[me], outs[o].at[me, layer], local_sems.at[u]))
            for k in range(1, N_DEV):
                peer, peer_index = _peer(pos, k)
                remote.append(pltpu.make_async_remote_copy(
                    src_ref=ins[i].at[peer_index], dst_ref=outs[o].at[me, layer], send_sem=send_sems.at[u, k - 1],
                    recv_sem=recv_sems.at[u, k - 1], device_id=peer, device_id_type=MESH_ID))
        _run_copies(local, remote)

    return pl.pallas_call(body, name=name, in_specs=[HBM] * n_in, out_specs=[HBM] * len(out_shapes), out_shape=out_shapes,
                          scratch_shapes=_comm_scratch(len(units)))(*flat_in)


SEM = pl.BlockSpec(memory_space=pltpu.SEMAPHORE)
SIDE_EFFECT = pltpu.SideEffectType.DATAFLOW_SIDE_EFFECTING


def _own_slots(arrays, indexed, me, name):
    lands = []
    for u, a in enumerate(arrays):
        block = a.shape[1:] if indexed else a.shape
        rows, cols = _size(block[:-1]), block[-1]
        tr = _row_tile(rows, cap=512)

        def body(me_ref, src_ref, out_ref):
            out_ref[...] = src_ref[...]

        src_spec = (pl.BlockSpec((None, tr, cols), lambda i, me_ref: (me_ref[0], i, 0)) if indexed
                    else pl.BlockSpec((tr, cols), lambda i, me_ref: (i, 0)))
        land = pl.pallas_call(
            body, name=f"{name}_{u}", out_shape=_sds((N_DEV, rows, cols), a.dtype),
            grid_spec=pltpu.PrefetchScalarGridSpec(
                num_scalar_prefetch=1, grid=(rows // tr,), in_specs=[src_spec],
                out_specs=pl.BlockSpec((None, tr, cols), lambda i, me_ref: (me_ref[0], i, 0))),
        )(me, a.reshape((N_DEV, rows, cols) if indexed else (rows, cols)))
        lands.append(land.reshape((N_DEV,) + block))
    return lands


def _split_copies(srcs, lands, send_sems, recv_sems, indexed):
    pos = _mesh_position()
    me = 4 * pos[0] + 2 * pos[1] + pos[2]
    copies = []
    for u, (src, land) in enumerate(zip(srcs, lands)):
        for k in range(1, N_DEV):
            peer, peer_index = _peer(pos, k)
            copies.append(pltpu.make_async_remote_copy(
                src_ref=src.at[peer_index] if indexed else src, dst_ref=land.at[me],
                send_sem=send_sems.at[u * (N_DEV - 1) + k - 1], recv_sem=recv_sems.at[u * (N_DEV - 1) + k - 1],
                device_id=peer, device_id_type=MESH_ID))
    return copies


def _exchange_start(arrays, lands, indexed, name):
    n = len(arrays)

    def body(*refs):
        srcs, zones = refs[:n], refs[n:2 * n]
        send_sems, recv_sems = refs[2 * n], refs[2 * n + 1]
        token = refs[-1]
        for cp in _split_copies(srcs, zones, send_sems, recv_sems, indexed):
            cp.start()
        token[...] = jnp.zeros_like(token)

    sem_shape = pltpu.SemaphoreType.DMA((n * (N_DEV - 1),))
    outs = pl.pallas_call(
        body, name=name, in_specs=[HBM] * (2 * n),
        out_specs=[SEM, SEM] + [HBM] * (2 * n) + [pl.BlockSpec(memory_space=pltpu.VMEM)],
        out_shape=[sem_shape, sem_shape] + [pltpu.HBM(a.shape, a.dtype) for a in list(arrays) + list(lands)]
        + [_sds((SUBLANE, LANE))],
        input_output_aliases={i: 2 + i for i in range(2 * n)},
        compiler_params=pltpu.CompilerParams(has_side_effects=SIDE_EFFECT),
    )(*[pltpu.with_memory_space_constraint(a, pltpu.HBM) for a in list(arrays) + list(lands)])
    return outs[0], outs[1], outs[2:2 + n], outs[2 + n:2 + 2 * n], outs[-1]


def _exchange_wait(send_sems, recv_sems, arrays, lands, after, indexed, name):
    n = len(arrays)

    def body(*refs):
        srcs, zones = refs[:n], refs[n:2 * n]
        s_sems, r_sems = refs[2 * n], refs[2 * n + 1]
        for cp in _split_copies(srcs, zones, s_sems, r_sems, indexed):
            cp.wait_send()
            cp.wait_recv()

    outs = pl.pallas_call(
        body, name=name, in_specs=[HBM] * (2 * n) + [SEM, SEM, HBM],
        out_specs=[HBM] * (2 * n), out_shape=[pltpu.HBM(a.shape, a.dtype) for a in list(arrays) + list(lands)],
        input_output_aliases={i: i for i in range(2 * n)},
        compiler_params=pltpu.CompilerParams(has_side_effects=SIDE_EFFECT),
    )(*arrays, *lands, send_sems, recv_sems, after)
    return outs[n:]


SUM_TILE = 512


def _adamw(w, g, m, v):
    m = ADAM_B1 * m + (1.0 - ADAM_B1) * g
    v = ADAM_B2 * v + (1.0 - ADAM_B2) * (g * g)
    m_hat = m / (1.0 - ADAM_B1 ** ADAM_STEP)
    v_hat = v / (1.0 - ADAM_B2 ** ADAM_STEP)
    return -ADAM_LR * (m_hat / (jnp.sqrt(v_hat) + ADAM_EPS) + ADAM_WD * w), m, v


def _sum_adamw(recv, w, m, v, layer, others, name):
    _, rows, cols = w.shape
    tr = _row_tile(rows, cap=256)

    def body(r_ref, w_ref, m_ref, v_ref, *rest):
        g_ref, d_ref, mo_ref, vo_ref = rest[-4:]
        g = r_ref[0].astype(F32)
        for j in range(1, N_DEV):
            g = g + r_ref[j].astype(F32)
        g_ref[...] = g
        d_ref[...], mo_ref[...], vo_ref[...] = _adamw(w_ref[...], g, m_ref[...], v_ref[...])

    blk = pl.BlockSpec((None, tr, cols), lambda i: (layer, i, 0))
    carried = list(others) if others is not None else []
    return pl.pallas_call(
        body, name=name, grid=(rows // tr,),
        in_specs=[pl.BlockSpec((N_DEV, tr, cols), lambda i: (0, i, 0)), blk, blk, blk] + [HBM] * len(carried),
        out_specs=[blk] * 4, out_shape=[_sds(w.shape)] * 4,
        input_output_aliases={4 + k: k for k in range(len(carried))},
        compiler_params=pltpu.CompilerParams(dimension_semantics=("arbitrary",), vmem_limit_bytes=VMEM_LIMIT),
    )(recv, w, m, v, *carried)


def _sum_senders(recv, name):
    _, rows, cols = recv.shape
    tr = _row_tile(rows, cap=256)

    def body(r_ref, g_ref):
        g = r_ref[0].astype(F32)
        for j in range(1, N_DEV):
            g = g + r_ref[j].astype(F32)
        g_ref[...] = g

    return _call(body, name, (rows // tr,), [pl.BlockSpec((N_DEV, tr, cols), lambda i: (0, i, 0))], [_rows(tr, cols)],
                 [_sds((rows, cols))])(recv)[0]


def _adamw_blocks(g, w, m, v, name):
    n_lay, rows, cols = w.shape
    tr = _row_tile(rows, cap=256)

    def body(g_ref, w_ref, m_ref, v_ref, d_ref, mo_ref, vo_ref):
        d_ref[...], mo_ref[...], vo_ref[...] = _adamw(w_ref[...], g_ref[...], m_ref[...], v_ref[...])

    blk = pl.BlockSpec((None, tr, cols), lambda l, i: (l, i, 0))
    return pl.pallas_call(
        body, name=name, grid=(n_lay, rows // tr), in_specs=[blk] * 4, out_specs=[blk] * 3, out_shape=[_sds(w.shape)] * 3,
        compiler_params=pltpu.CompilerParams(dimension_semantics=("arbitrary", "arbitrary"), vmem_limit_bytes=VMEM_LIMIT),
    )(g, w, m, v)


def _sum_slots(recv, name):
    rows = recv.shape[1]

    def body(r_ref, g_ref):
        g = r_ref[0].astype(F32)
        for j in range(1, N_DEV):
            g = g + r_ref[j].astype(F32)
        g_ref[...] = g

    return _call(body, name, (1,), [_full(recv.shape)], [_full((rows, LANE))], [_sds((rows, LANE))])(recv)[0]


def _adamw_rows(g, w, m, v, name):
    rows, cols = w.shape
    tr = _row_tile(rows)

    def body(g_ref, w_ref, m_ref, v_ref, d_ref, mo_ref, vo_ref):
        d_ref[...], mo_ref[...], vo_ref[...] = _adamw(w_ref[...], g_ref[...], m_ref[...], v_ref[...])

    blk = _rows(tr, cols)
    return _call(body, name, (rows // tr,), [blk] * 4, [blk] * 3, [_sds((rows, cols))] * 3)(g, w, m, v)


def _row_tile(rows, cap=1024):
    if rows % SUBLANE:
        return rows
    best = SUBLANE
    for t in range(SUBLANE, cap + 1, SUBLANE):
        if rows % t == 0:
            best = t
    return best


BIG = (("w_in", (DEPTH, D_MODEL, IN_PROJ // N_DEV), 2),
       ("s5_w_glu", (DEPTH, D_MODEL // N_DEV, D_MODEL), 1),
       ("ssd_conv_w", (DEPTH, SSD_CONV, SSD_CONV_DIM // N_DEV), 2),
       ("w_out", (DEPTH, 2 * D_MODEL // N_DEV, D_MODEL), 1),
       ("w_gate", (DEPTH, D_MODEL, FFN_HIDDEN // N_DEV), 2),
       ("w_up", (DEPTH, D_MODEL, FFN_HIDDEN // N_DEV), 2),
       ("w_down", (DEPTH, FFN_HIDDEN // N_DEV, D_MODEL), 1))
SMALL = (("norm_mix", (DEPTH, D_MODEL)), ("s5_lam_re", (DEPTH, S5_GROUPS, S5_STATE)), ("s5_lam_im", (DEPTH, S5_GROUPS, S5_STATE)),
         ("s5_log_step", (DEPTH, S5_GROUPS)), ("s5_b_re", (DEPTH, S5_GROUPS, S5_STATE, S5_GROUP)),
         ("s5_b_im", (DEPTH, S5_GROUPS, S5_STATE, S5_GROUP)), ("s5_c_re", (DEPTH, S5_GROUPS, S5_GROUP, S5_STATE)),
         ("s5_c_im", (DEPTH, S5_GROUPS, S5_GROUP, S5_STATE)), ("s5_d", (DEPTH, D_MODEL)), ("s5_b_glu", (DEPTH, D_MODEL)),
         ("s5_norm", (DEPTH, D_MODEL)), ("ssd_conv_b", (DEPTH, SSD_CONV_DIM)), ("ssd_dt_bias", (DEPTH, SSD_HEADS)),
         ("ssd_a_log", (DEPTH, SSD_HEADS)), ("ssd_d", (DEPTH, SSD_HEADS)), ("ssd_norm", (DEPTH, D_MODEL)),
         ("norm_ffn", (DEPTH, D_MODEL)), ("norm_final", (D_MODEL,)))
WEIGHT_ORDER = ("norm_mix", "w_in", "s5_lam_re", "s5_lam_im", "s5_log_step", "s5_b_re", "s5_b_im", "s5_c_re", "s5_c_im", "s5_d",
                "s5_w_glu", "s5_b_glu", "s5_norm", "ssd_conv_w", "ssd_conv_b", "ssd_dt_bias", "ssd_a_log", "ssd_d", "ssd_norm",
                "w_out", "norm_ffn", "w_gate", "w_up", "w_down", "norm_final")


def _size(shape):
    n = 1
    for s in shape:
        n *= s
    return n


def _round_up(n, m):
    return -(-n // m) * m


SMALL_SIZE = sum(_size(s) for _, s in SMALL)
SMALL_ROWS = _round_up(-(-SMALL_SIZE // (N_DEV * LANE)), SUBLANE)


def _pack(parts, rows, dtype):
    flat = jnp.concatenate([p.reshape(-1).astype(dtype) for p in parts])
    return jnp.pad(flat, (0, rows * LANE - flat.shape[0])).reshape(rows, LANE)


def _unpack(flat, specs):
    out, off = {}, 0
    flat = flat.reshape(-1)
    for name, shape in specs:
        out[name] = flat[off:off + _size(shape)].reshape(shape)
        off += _size(shape)
    return out


def kernel(x, norm_mix, w_in, s5_lam_re, s5_lam_im, s5_log_step, s5_b_re, s5_b_im, s5_c_re, s5_c_im, s5_d, s5_w_glu, s5_b_glu, s5_norm, ssd_conv_w, ssd_conv_b, ssd_dt_bias, ssd_a_log, ssd_d, ssd_norm, w_out, norm_ffn, w_gate, w_up, w_down, norm_final, loss_target, m_norm_mix, m_w_in, m_s5_lam_re, m_s5_lam_im, m_s5_log_step, m_s5_b_re, m_s5_b_im, m_s5_c_re, m_s5_c_im, m_s5_d, m_s5_w_glu, m_s5_b_glu, m_s5_norm, m_ssd_conv_w, m_ssd_conv_b, m_ssd_dt_bias, m_ssd_a_log, m_ssd_d, m_ssd_norm, m_w_out, m_norm_ffn, m_w_gate, m_w_up, m_w_down, m_norm_final, v_norm_mix, v_w_in, v_s5_lam_re, v_s5_lam_im, v_s5_log_step, v_s5_b_re, v_s5_b_im, v_s5_c_re, v_s5_c_im, v_s5_d, v_s5_w_glu, v_s5_b_glu, v_s5_norm, v_ssd_conv_w, v_ssd_conv_b, v_ssd_dt_bias, v_ssd_a_log, v_ssd_d, v_ssd_norm, v_w_out, v_norm_ffn, v_w_gate, v_w_up, v_w_down, v_norm_final):
    given = dict(locals())
    w = {n: given[n] for n in WEIGHT_ORDER}
    m = {n: given["m_" + n] for n in WEIGHT_ORDER}
    v = {n: given["v_" + n] for n in WEIGHT_ORDER}
    big_names = tuple(n for n, _, _ in BIG)
    matmul_names = tuple(n for n in big_names if n != "ssd_conv_w")

    conv_hi = w["ssd_conv_w"].astype(BF16)
    conv_lo = (w["ssd_conv_w"] - conv_hi.astype(F32)).astype(BF16)
    row_pad = ((0, 0), (0, FFN_BLOCK_PAD - FFN_BLOCK), (0, 0))
    as_rows = {"w_gate": jnp.swapaxes(w["w_gate"], 1, 2), "w_up": jnp.swapaxes(w["w_up"], 1, 2), "w_down": w["w_down"]}
    to_send = [jnp.pad(as_rows[n].astype(BF16), row_pad) if n in as_rows else w[n].astype(BF16) for n in matmul_names]

    def layer_blocks(i):
        return [a[i] for a in to_send] + [jnp.stack([conv_hi[i], conv_lo[i]])]

    def conv_taps(pair):
        pair = pair.astype(F32)
        return jnp.moveaxis(pair[:, 0] + pair[:, 1], 0, 1).reshape(SSD_CONV, SSD_CONV_DIM)

    def as_layer_weights(gathered):
        blk = dict(zip(matmul_names, gathered))
        blk["ssd_conv_w"] = conv_taps(gathered[-1])
        return blk

    gathered0 = [g[0] for g in _gather_blocks(layer_blocks(0), [False] * (len(matmul_names) + 1), name="gather_weights_0")]
    blocks1 = layer_blocks(1)
    me = (4 * lax.axis_index("x") + 2 * lax.axis_index("y") + lax.axis_index("c")).astype(jnp.int32).reshape(1)
    n_first = 2
    first1, then1 = blocks1[:n_first] + blocks1[-1:], blocks1[n_first:-1]
    sems1a = _exchange_start(first1, _own_slots(first1, False, me, name="gather_own_1a"), False, name="gather_start_1a")
    sems1b = _exchange_start(then1, _own_slots(then1, False, me, name="gather_own_1b"), False, name="gather_start_1b")
    started = sems1a[-1] + sems1b[-1]
    prepared = [_prepare_layer(w, as_layer_weights(gathered0), 0, started), None]
    saved = [None, None]
    h, saved[0] = _layer_fwd(x[0], prepared[0], 0)
    arrived = _exchange_wait(*sems1a[:4], h, False, name="gather_wait_1a")
    blk1 = dict(zip(matmul_names[:n_first], arrived))
    blk1["ssd_conv_w"] = conv_taps(arrived[-1])
    prepared[1] = _prepare_layer(w, blk1, 1, started)

    def rest_of_layer1(after):
        return dict(zip(matmul_names[n_first:], _exchange_wait(*sems1b[:4], after, False, name="gather_wait_1b")))

    h, saved[1] = _layer_fwd(h, prepared[1], 1, late=rest_of_layer1)
    loss, dh, g_final = _loss_head(h, w["norm_final"].reshape(1, D_MODEL), loss_target[0], name="loss_head")

    layer_grads = [None, None]
    dh, layer_grads[1] = _layer_bwd(dh, saved[1], prepared[1], 1, started)
    slots1 = [layer_grads[1][n] for n in big_names]
    sems2 = _exchange_start(slots1, _own_slots(slots1, True, me, name="exchange_own_1"), True, name="exchange_start_1")
    early_names = ("w_out", "w_gate", "w_up", "w_down")
    late_names = tuple(n for n in big_names if n not in early_names)
    early = {}

    def send_early(g):
        slots = [g[n] for n in early_names]
        early["sems"] = _exchange_start(slots, _own_slots(slots, True, me, name="exchange_own_0"), True, name="exchange_start_0")
        return early["sems"][-1]

    grad_x, layer_grads[0] = _layer_bwd(dh, saved[0], prepared[0], 0, sems2[-1], between=send_early)

    late = [layer_grads[0][n] for n in late_names]
    sems3 = _exchange_start(late, _own_slots(late, True, me, name="exchange_own_late"), True, name="exchange_start_late")
    for i in range(DEPTH):
        _s5_param_grads(layer_grads[i], prepared[i], i, sems3[-1])
    small = jnp.concatenate([g_final.reshape(-1) if n == "norm_final"
                             else jnp.stack([layer_grads[i][n] for i in range(DEPTH)]).reshape(-1) for n, _ in SMALL])
    small_slots = [jnp.pad(small, (0, N_DEV * SMALL_ROWS * LANE - small.shape[0])).reshape(N_DEV, SMALL_ROWS, LANE)]
    sems4 = _exchange_start(small_slots, _own_slots(small_slots, True, me, name="exchange_own_small"), True,
                            name="exchange_start_small")
    received1 = _exchange_wait(*sems2[:4], sems4[-1], True, name="exchange_wait_1")
    received_early = _exchange_wait(*early["sems"][:4], sems4[-1], True, name="exchange_wait_0")
    received0 = dict(zip(early_names, received_early))

    transposed = ("w_gate", "w_up")
    layer1 = {n: _sum_adamw(received1[j], w[n], m[n], v[n], 1, None, name=f"sum_adamw_{n}_1")
              for j, n in enumerate(big_names) if n not in transposed}
    results = {}
    for n in transposed:
        recv = (received0[n], received1[big_names.index(n)])
        g = jnp.stack([jnp.swapaxes(_sum_senders(recv[i], name=f"sum_{n}_{i}"), 0, 1) for i in range(DEPTH)])
        results[n] = [g, *_adamw_blocks(g, w[n], m[n], v[n], name=f"adamw_{n}")]
    for n in early_names:
        if n not in transposed:
            results[n] = _sum_adamw(received0[n], w[n], m[n], v[n], 0, layer1[n], name=f"sum_adamw_{n}_0")
    received_late = _exchange_wait(*sems3[:4], results["w_down"][0], True, name="exchange_wait_late")
    for n, recv in zip(late_names, received_late):
        results[n] = _sum_adamw(recv, w[n], m[n], v[n], 0, layer1[n], name=f"sum_adamw_{n}_0")
    received_small = _exchange_wait(*sems4[:4], results[late_names[-1]][0], True, name="exchange_wait_small")
    g_part = _sum_slots(received_small[0], name="sum_replicated")
    g_small = _gather_blocks([g_part], [False], name="gather_replicated")[0][0]
    for n, g in _unpack(g_small, SMALL).items():
        as_rows = (-1, g.shape[-1])
        d_n, m_n, v_n = _adamw_rows(g.reshape(as_rows), w[n].reshape(as_rows), m[n].reshape(as_rows), v[n].reshape(as_rows),
                                    name=f"adamw_{n}")
        results[n] = [g, d_n.reshape(g.shape), m_n.reshape(g.shape), v_n.reshape(g.shape)]

    outs = [results[n][k] for k in range(4) for n in WEIGHT_ORDER]
    total_loss = lax.psum(loss[0, 0], ("x", "y", "c"))
    return (total_loss, grad_x[None], *outs)
```

```python
import functools

import jax
import jax.numpy as jnp
from jax import lax
from jax.experimental import pallas as pl
from jax.experimental.pallas import tpu as pltpu

F32 = jnp.float32
BF16 = jnp.bfloat16
MESH_ID = pl.DeviceIdType.MESH

N_DEV = 8
DEPTH = 2
D_MODEL = 1024
S5_GROUPS = 64
S5_GROUP = 16
S5_STATE = 64
S5_LANES = S5_GROUPS * S5_STATE
SSD_HEADS = 16
SSD_HEAD_DIM = 64
SSD_STATE = 128
SSD_CHUNK = 128
SSD_CONV = 4
SSD_CONV_DIM = 1536
FFN_HIDDEN = 2816
IN_PROJ = 3600
EPS = 1e-6
LANE = 128
SUBLANE = 8
VMEM_LIMIT = 56 * 1024 * 1024

ADAM_LR = 0.001
ADAM_B1 = 0.9
ADAM_B2 = 0.999
ADAM_EPS = 1e-08
ADAM_WD = 0.01
ADAM_STEP = 10

TOK_TILE = 256
S5_TILE = 128


def _sigmoid(x):
    return jax.nn.sigmoid(x)


def _silu(x):
    return x * _sigmoid(x)


def _gelu(x):
    return 0.5 * x * (1.0 + jnp.tanh(0.7978845608028654 * (x + 0.044715 * (x * x * x))))


def _softplus(x):
    return jnp.maximum(x, 0.0) + jnp.log(1.0 + jnp.exp(-jnp.abs(x)))


def _rms(x, g):
    r = lax.rsqrt(jnp.mean(x * x, axis=-1, keepdims=True) + EPS)
    return x * r * g


def _nn(a, b):
    return lax.dot_general(a.astype(BF16), b.astype(BF16), (((1,), (0,)), ((), ())), preferred_element_type=F32)


def _nt(a, b):
    return lax.dot_general(a.astype(BF16), b.astype(BF16), (((1,), (1,)), ((), ())), preferred_element_type=F32)


def _tn(a, b):
    return lax.dot_general(a.astype(BF16), b.astype(BF16), (((0,), (0,)), ((), ())), preferred_element_type=F32)


def _nn_f32(a, b):
    return lax.dot_general(a, b, (((1,), (0,)), ((), ())), precision=lax.Precision.HIGHEST, preferred_element_type=F32)


def _tn_f32(a, b):
    return lax.dot_general(a, b, (((0,), (0,)), ((), ())), precision=lax.Precision.HIGHEST, preferred_element_type=F32)


@jax.custom_vjp
def _nn_d(a, b):
    return _nn(a, b)


_nn_d.defvjp(lambda a, b: (_nn(a, b), (a, b)), lambda r, g: (_nt(g, r[1]), _tn(r[0], g)))


@jax.custom_vjp
def _nt_d(a, b):
    return _nt(a, b)


_nt_d.defvjp(lambda a, b: (_nt(a, b), (a, b)), lambda r, g: (_nn(g, r[1]), _tn(g, r[0])))


@jax.custom_vjp
def _tn_d(a, b):
    return _tn(a, b)


_tn_d.defvjp(lambda a, b: (_tn(a, b), (a, b)), lambda r, g: (_nt(r[1], g), _nn(r[0], g)))


@jax.custom_vjp
def _cumsum_rows(tri, x):
    return _nn_f32(tri, x)


_cumsum_rows.defvjp(lambda tri, x: (_nn_f32(tri, x), tri), lambda tri, g: (jnp.zeros_like(tri), _tn_f32(tri, g)))


def _full(shape):
    zeros = (0,) * len(shape)
    return pl.BlockSpec(shape, lambda *_: zeros)


def _const(shape):
    zeros = (0,) * len(shape)
    return pl.BlockSpec(shape, lambda *_: zeros, pipeline_mode=pl.Buffered(1))


def _rows(tile, width, n_tiles=None):
    if n_tiles is None:
        return pl.BlockSpec((tile, width), lambda i: (i, 0))
    return pl.BlockSpec((tile, width), lambda i: (n_tiles - 1 - i, 0))


def _call(body, name, grid, in_specs, out_specs, out_shape, scratch=()):
    return pl.pallas_call(
        body, name=name, grid=grid, in_specs=in_specs, out_specs=out_specs, out_shape=out_shape,
        scratch_shapes=list(scratch),
        compiler_params=pltpu.CompilerParams(dimension_semantics=("arbitrary",) * len(grid),
                                             vmem_limit_bytes=VMEM_LIMIT))


def _sds(shape, dtype=F32):
    return jax.ShapeDtypeStruct(shape, dtype)


def _tile_of(n, cap=512):
    if n <= LANE:
        return n
    best = LANE
    for t in range(LANE, cap + 1, LANE):
        if n % t == 0:
            best = t
    return best


BIG_TILE = 512


def _big_tile(n_tok):
    return TOK_TILE


def _inproj_fwd(x, nm, wu, wz, wx, wd, name):
    n_tok = x.shape[0]
    tm = _big_tile(n_tok)

    def body(x_ref, nm_ref, wu_ref, wz_ref, wx_ref, wd_ref, u_ref, z_ref, xbc_ref, dt_ref):
        h = _rms(x_ref[...], nm_ref[...]).astype(BF16)
        u_ref[...] = _nn(h, wu_ref[...])
        z_ref[...] = _nn(h, wz_ref[...])
        xbc_ref[...] = _nn(h, wx_ref[...])
        dt_ref[...] = _nn(h, wd_ref[...])

    return _call(
        body, name, (n_tok // tm,),
        [_rows(tm, D_MODEL), _const((1, D_MODEL)), _const(wu.shape), _const(wz.shape), _const(wx.shape), _const(wd.shape)],
        [_rows(tm, D_MODEL), _rows(tm, D_MODEL), _rows(tm, SSD_CONV_DIM), _rows(tm, LANE)],
        [_sds((n_tok, D_MODEL)), _sds((n_tok, D_MODEL)), _sds((n_tok, SSD_CONV_DIM)), _sds((n_tok, LANE))],
    )(x, nm, wu, wz, wx, wd)


def _inproj_bwd(x, nm, du, dz, dxbc, ddt, dres, wu, wz, wx, wd, name):
    n_tok = x.shape[0]
    tm = _big_tile(n_tok)

    def body(x_ref, nm_ref, du_ref, dz_ref, dxbc_ref, ddt_ref, dres_ref, wu_ref, wz_ref, wx_ref, wd_ref,
             dx_ref, h_ref, dnm_ref):
        dh = (_nt(du_ref[...], wu_ref[...]) + _nt(dz_ref[...], wz_ref[...])
              + _nt(dxbc_ref[...], wx_ref[...]) + _nt(ddt_ref[...], wd_ref[...]))
        h, vjp = jax.vjp(_rms, x_ref[...], nm_ref[...])
        dx, dnm = vjp(dh)
        dx_ref[...] = dres_ref[...] + dx
        h_ref[...] = h.astype(BF16)

        @pl.when(pl.program_id(0) == 0)
        def _():
            dnm_ref[...] = jnp.zeros_like(dnm_ref)

        dnm_ref[...] += dnm

    return _call(
        body, name, (n_tok // tm,),
        [_rows(tm, D_MODEL), _const((1, D_MODEL)), _rows(tm, D_MODEL), _rows(tm, D_MODEL), _rows(tm, SSD_CONV_DIM),
         _rows(tm, LANE), _rows(tm, D_MODEL), _const(wu.shape), _const(wz.shape), _const(wx.shape), _const(wd.shape)],
        [_rows(tm, D_MODEL), _rows(tm, D_MODEL), _full((1, D_MODEL))],
        [_sds((n_tok, D_MODEL)), _sds((n_tok, D_MODEL), BF16), _sds((1, D_MODEL))],
    )(x, nm, du, dz, dxbc, ddt, dres, wu, wz, wx, wd)


def _ffn_act(gt, up):
    return _silu(gt) * up


FFN_BLOCK = FFN_HIDDEN // N_DEV
FFN_BLOCK_PAD = -(-FFN_BLOCK // LANE) * LANE


FFN_PAD = N_DEV * FFN_BLOCK_PAD


def _mix_ffn_fwd(x0, ya, yb, wo, nf, wg, wu, wd, name):
    n_tok = x0.shape[0]
    tm = TOK_TILE

    def body(x0_ref, ya_ref, yb_ref, wo_ref, nf_ref, wg_ref, wu_ref, wd_ref, x1_ref, x2_ref):
        x1 = x0_ref[...] + _nn(ya_ref[...], wo_ref[:D_MODEL, :]) + _nn(yb_ref[...], wo_ref[D_MODEL:, :])
        h = _rms(x1, nf_ref[...]).astype(BF16)
        x1_ref[...] = x1
        x2_ref[...] = x1 + _nn(_ffn_act(_nt(h, wg_ref[...]), _nt(h, wu_ref[...])), wd_ref[...])

    return _call(
        body, name, (n_tok // tm,),
        [_rows(tm, D_MODEL), _rows(tm, D_MODEL), _rows(tm, D_MODEL), _const(wo.shape),
         _const((1, D_MODEL)), _const(wg.shape), _const(wu.shape), _const(wd.shape)],
        [_rows(tm, D_MODEL), _rows(tm, D_MODEL)],
        [_sds((n_tok, D_MODEL)), _sds((n_tok, D_MODEL))],
    )(x0, ya, yb, wo, nf, wg, wu, wd)


def _mix_ffn_bwd(x1, dx2, wo, nf, wg, wu, wd, after, name):
    n_tok = x1.shape[0]
    tm = TOK_TILE
    n_chunks = 3
    hc = FFN_PAD // n_chunks

    def body(x1_ref, dx2_ref, wo_ref, nf_ref, wg_ref, wu_ref, wd_ref, after_ref,
             dx1_ref, dya_ref, dyb_ref, h_ref, a_ref, dgt_ref, dup_ref, dx2b_ref, dx1b_ref, dnf_ref):
        dx2 = dx2_ref[...]
        dx2b = dx2.astype(BF16)
        dx2b_ref[...] = dx2b
        h, rms_vjp = jax.vjp(_rms, x1_ref[...], nf_ref[...])
        hb = h.astype(BF16)
        dh = jnp.zeros_like(h)
        for c in range(n_chunks):
            rows = pl.ds(c * hc, hc)
            a, act_vjp = jax.vjp(_ffn_act, _nt(hb, wg_ref[rows, :]), _nt(hb, wu_ref[rows, :]))
            dgt, dup = act_vjp(_nt(dx2b, wd_ref[rows, :]))
            a_ref[:, c * hc:(c + 1) * hc] = a.astype(BF16)
            dgt_ref[:, c * hc:(c + 1) * hc] = dgt.astype(BF16)
            dup_ref[:, c * hc:(c + 1) * hc] = dup.astype(BF16)
            dh = dh + _nn(dgt, wg_ref[rows, :]) + _nn(dup, wu_ref[rows, :])
        dx, dnf = rms_vjp(dh)
        dx1 = dx2 + dx
        dx1b = dx1.astype(BF16)
        dx1_ref[...] = dx1
        dx1b_ref[...] = dx1b
        dya_ref[...] = _nt(dx1b, wo_ref[:D_MODEL, :])
        dyb_ref[...] = _nt(dx1b, wo_ref[D_MODEL:, :])
        h_ref[...] = hb

        @pl.when(pl.program_id(0) == 0)
        def _():
            dnf_ref[...] = jnp.zeros_like(dnf_ref)

        dnf_ref[...] += dnf

    hidden = _rows(tm, FFN_PAD)
    return _call(
        body, name, (n_tok // tm,),
        [_rows(tm, D_MODEL), _rows(tm, D_MODEL), _const(wo.shape), _const((1, D_MODEL)),
         _const(wg.shape), _const(wu.shape), _const(wd.shape), HBM],
        [_rows(tm, D_MODEL), _rows(tm, D_MODEL), _rows(tm, D_MODEL), _rows(tm, D_MODEL), hidden, hidden, hidden,
         _rows(tm, D_MODEL), _rows(tm, D_MODEL), _full((1, D_MODEL))],
        [_sds((n_tok, D_MODEL)), _sds((n_tok, D_MODEL)), _sds((n_tok, D_MODEL)), _sds((n_tok, D_MODEL), BF16),
         _sds((n_tok, FFN_PAD), BF16), _sds((n_tok, FFN_PAD), BF16), _sds((n_tok, FFN_PAD), BF16),
         _sds((n_tok, D_MODEL), BF16), _sds((n_tok, D_MODEL), BF16), _sds((1, D_MODEL))],
    )(x1, dx2, wo, nf, wg, wu, wd, after)


def _loss_head(x, nf, target, name):
    n_tok = x.shape[0]
    tm = TOK_TILE

    def loss_of(xv, g, t):
        e = _rms(xv, g) - t
        return 0.5 * jnp.sum(jnp.sum(e * e, axis=-1, keepdims=True) * (1.0 / D_MODEL), axis=0, keepdims=True)

    def body(x_ref, nf_ref, t_ref, loss_ref, dx_ref, dnf_ref):
        loss, vjp = jax.vjp(functools.partial(loss_of, t=t_ref[...]), x_ref[...], nf_ref[...])
        dx, dnf = vjp(jnp.ones_like(loss))
        dx_ref[...] = dx

        @pl.when(pl.program_id(0) == 0)
        def _():
            dnf_ref[...] = jnp.zeros_like(dnf_ref)
            loss_ref[...] = jnp.zeros_like(loss_ref)

        dnf_ref[...] += dnf
        loss_ref[...] += jnp.broadcast_to(loss, loss_ref.shape)

    return _call(
        body, name, (n_tok // tm,),
        [_rows(tm, D_MODEL), _const((1, D_MODEL)), _rows(tm, D_MODEL)],
        [_full((SUBLANE, LANE)), _rows(tm, D_MODEL), _full((1, D_MODEL))],
        [_sds((SUBLANE, LANE)), _sds((n_tok, D_MODEL)), _sds((1, D_MODEL))],
    )(x, nf, target)


GRAD_WIRE = BF16


def _matmul_tn(a, b, name):
    n_tok, k1 = a.shape
    k2 = b.shape[1]
    t1 = _tile_of(k1)

    def body(a_ref, b_ref, o_ref):
        o_ref[...] = _tn(a_ref[...], b_ref[...]).astype(GRAD_WIRE)

    return _call(body, name, (k1 // t1,), [pl.BlockSpec((n_tok, t1), lambda i: (0, i)), _const((n_tok, k2))],
                 [pl.BlockSpec((t1, k2), lambda i: (i, 0))], [_sds((k1, k2), GRAD_WIRE)])(a, b)[0]


def _matmul_tn_lhs_blocks(a, b, width, keep, name):
    n_tok, k1 = a.shape
    k2 = b.shape[1]

    def body(a_ref, b_ref, o_ref):
        o_ref[...] = _tn(a_ref[...], b_ref[...])[:keep, :].astype(GRAD_WIRE)

    return _call(body, name, (k1 // width,), [pl.BlockSpec((n_tok, width), lambda d: (0, d)), _const((n_tok, k2))],
                 [pl.BlockSpec((None, keep, k2), lambda d: (d, 0, 0))], [_sds((k1 // width, keep, k2), GRAD_WIRE)])(a, b)[0]


def _matmul_tn_pair(a0, a1, b, name):
    n_tok, k1 = a0.shape
    k2 = b.shape[1]
    t1 = _tile_of(k1)
    n1 = k1 // t1

    def body(a0_ref, a1_ref, b_ref, o_ref):
        @pl.when(pl.program_id(0) < n1)
        def _():
            o_ref[...] = _tn(a0_ref[...], b_ref[...]).astype(GRAD_WIRE)

        @pl.when(pl.program_id(0) >= n1)
        def _():
            o_ref[...] = _tn(a1_ref[...], b_ref[...]).astype(GRAD_WIRE)

    return _call(
        body, name, (2 * n1,),
        [pl.BlockSpec((n_tok, t1), lambda i: (0, jnp.minimum(i, n1 - 1))),
         pl.BlockSpec((n_tok, t1), lambda i: (0, jnp.maximum(i - n1, 0))), _const((n_tok, k2))],
        [pl.BlockSpec((None, t1, k2), lambda i: (i // n1, i % n1, 0))], [_sds((2, k1, k2), GRAD_WIRE)])(a0, a1, b)[0]


W_IN_BLOCK = IN_PROJ // N_DEV
W_IN_SPLITS = (D_MODEL, 2 * D_MODEL, 2 * D_MODEL + SSD_CONV_DIM)
RELAYOUT_TILE = 256


def _w_in_split(blocks, after, name):
    tr = RELAYOUT_TILE

    def body(b_ref, after_ref, wu_ref, wz_ref, wx_ref, wd_ref):
        full = jnp.concatenate([b_ref[d] for d in range(N_DEV)], axis=1)
        wu_ref[...] = full[:, :W_IN_SPLITS[0]]
        wz_ref[...] = full[:, W_IN_SPLITS[0]:W_IN_SPLITS[1]]
        wx_ref[...] = full[:, W_IN_SPLITS[1]:W_IN_SPLITS[2]]
        wd_ref[...] = jnp.concatenate([full[:, W_IN_SPLITS[2]:], jnp.zeros((tr, LANE - SSD_HEADS), full.dtype)], axis=1)

    return _call(
        body, name, (D_MODEL // tr,), [pl.BlockSpec((N_DEV, tr, W_IN_BLOCK), lambda i: (0, i, 0)), HBM],
        [_rows(tr, D_MODEL), _rows(tr, D_MODEL), _rows(tr, SSD_CONV_DIM), _rows(tr, LANE)],
        [_sds((D_MODEL, D_MODEL), BF16), _sds((D_MODEL, D_MODEL), BF16), _sds((D_MODEL, SSD_CONV_DIM), BF16),
         _sds((D_MODEL, LANE), BF16)],
    )(blocks, after)


def _w_in_grad_blocks(gu, gz, gx, gdt, name):
    tr = RELAYOUT_TILE

    def body(gu_ref, gz_ref, gx_ref, gdt_ref, o_ref):
        full = jnp.concatenate([gu_ref[...], gz_ref[...], gx_ref[...], gdt_ref[...]], axis=1)
        for d in range(N_DEV):
            o_ref[d] = full[:, d * W_IN_BLOCK:(d + 1) * W_IN_BLOCK]

    return _call(
        body, name, (D_MODEL // tr,),
        [_rows(tr, D_MODEL), _rows(tr, D_MODEL), _rows(tr, SSD_CONV_DIM), _rows(tr, LANE)],
        [pl.BlockSpec((N_DEV, tr, W_IN_BLOCK), lambda i: (0, i, 0))], [_sds((N_DEV, D_MODEL, W_IN_BLOCK), gu.dtype)],
    )(gu, gz, gx, gdt)[0]


S5_SLICES = D_MODEL // LANE
S5_SLICE_STATES = S5_LANES // S5_SLICES
SCAN_LANES = 512


def _s5_scan(br_ref, bi_ref, a_r, a_i, car_r, car_i, ini_r, ini_i, reverse, xr_ref=None, xi_ref=None,
             acc_r=None, acc_i=None, row0=0):
    n_rows = S5_TILE
    seg = n_rows // SUBLANE
    order = range(SUBLANE - 1, -1, -1) if reverse else range(SUBLANE)

    def rows(t):
        return pl.ds(pl.multiple_of(row0 + ((seg - 1 - t) if reverse else t) * SUBLANE, SUBLANE), SUBLANE)

    tiles_per = SCAN_LANES // LANE

    def load(ref, t, lb):
        return jnp.concatenate([ref[lb * tiles_per + j, rows(t), :] for j in range(tiles_per)], axis=1)

    def store(ref, t, lb, val):
        for j in range(tiles_per):
            ref[lb * tiles_per + j, rows(t), :] = val[:, j * LANE:(j + 1) * LANE]

    for lb in range(S5_LANES // SCAN_LANES):
        lanes = pl.ds(lb * SCAN_LANES, SCAN_LANES)
        ar1, ai1 = a_r[:, lb * SCAN_LANES:(lb + 1) * SCAN_LANES], a_i[:, lb * SCAN_LANES:(lb + 1) * SCAN_LANES]
        ar8 = jnp.broadcast_to(ar1, (SUBLANE, SCAN_LANES))
        ai8 = jnp.broadcast_to(ai1, (SUBLANE, SCAN_LANES))

        def local(t, c):
            sr, si = c
            return (ar8 * sr - ai8 * si + load(br_ref, t, lb), ar8 * si + ai8 * sr + load(bi_ref, t, lb))

        zero = jnp.zeros((SUBLANE, SCAN_LANES), F32)
        er, ei = lax.fori_loop(0, seg, local, (zero, zero))
        pr, pi = ar1, ai1
        for _ in range(seg.bit_length() - 1):
            pr, pi = pr * pr - pi * pi, 2.0 * pr * pi
        cr, ci = car_r[:, lanes], car_i[:, lanes]
        for s in order:
            ini_r[s:s + 1, lanes] = cr
            ini_i[s:s + 1, lanes] = ci
            cr, ci = pr * cr - pi * ci + er[s:s + 1, :], pr * ci + pi * cr + ei[s:s + 1, :]
        car_r[:, lanes] = cr
        car_i[:, lanes] = ci

        if xr_ref is None:
            def final(t, c):
                sr, si = c
                nr = ar8 * sr - ai8 * si + load(br_ref, t, lb)
                ni = ar8 * si + ai8 * sr + load(bi_ref, t, lb)
                store(br_ref, t, lb, nr)
                store(bi_ref, t, lb, ni)
                return nr, ni

            lax.fori_loop(0, seg, final, (ini_r[:, lanes], ini_i[:, lanes]))
        else:
            def final_acc(t, c):
                sr, si, gr, gi = c
                xr, xi = load(xr_ref, t, lb), load(xi_ref, t, lb)
                gr = gr + sr * xr + si * xi
                gi = gi + si * xr - sr * xi
                nr = ar8 * sr - ai8 * si + load(br_ref, t, lb)
                ni = ar8 * si + ai8 * sr + load(bi_ref, t, lb)
                store(br_ref, t, lb, nr)
                store(bi_ref, t, lb, ni)
                return nr, ni, gr, gi

            _, _, gr, gi = lax.fori_loop(0, seg, final_acc,
                                         (ini_r[:, lanes], ini_i[:, lanes], acc_r[:, lanes], acc_i[:, lanes]))
            acc_r[:, lanes] = gr
            acc_i[:, lanes] = gi


def _s5_tail(gg, q, sn):
    return _rms(gg * _sigmoid(q), sn)


def _scan_order(n_rows):
    seg = n_rows // SUBLANE
    r = lax.broadcasted_iota(jnp.int32, (n_rows, n_rows), 0)
    c = lax.broadcasted_iota(jnp.int32, (n_rows, n_rows), 1)
    return (c == (r % SUBLANE) * seg + r // SUBLANE).astype(F32)


S5_STATE_TILES = S5_LANES // LANE
TILES_PER_SLICE = S5_SLICE_STATES // LANE


def _put_states(ref, k, val):
    for j in range(TILES_PER_SLICE):
        ref[k * TILES_PER_SLICE + j] = val[:, j * LANE:(j + 1) * LANE]


def _get_states(ref, k):
    return jnp.concatenate([ref[k * TILES_PER_SLICE + j] for j in range(TILES_PER_SLICE)], axis=1)


def _state_rows(tile, n_tiles=None):
    if n_tiles is None:
        return pl.BlockSpec((S5_STATE_TILES, tile, LANE), lambda i: (0, i, 0))
    return pl.BlockSpec((S5_STATE_TILES, tile, LANE), lambda i: (0, n_tiles - 1 - i, 0))


def _s5_fwd(u, a_r, a_i, bdb, bcr, bci, dsk, wglu, bglu, sn, name):
    n_tok = u.shape[0]
    n_sub = 2
    tc = n_sub * S5_TILE
    sw = S5_SLICE_STATES

    def body(u_ref, ar_ref, ai_ref, bdb_ref, bcr_ref, bci_ref, d_ref, wg_ref, bg_ref, sn_ref,
             ya_ref, xr_ref, xi_ref, v_ref, car_r, car_i, ini_r, ini_i):
        @pl.when(pl.program_id(0) == 0)
        def _():
            car_r[...] = jnp.zeros_like(car_r)
            car_i[...] = jnp.zeros_like(car_i)

        order = _scan_order(S5_TILE)
        u_t = jnp.concatenate([_nn_f32(order, u_ref[s * S5_TILE:(s + 1) * S5_TILE, :]) for s in range(n_sub)], axis=0)
        ub = u_t.astype(BF16)
        for k in range(S5_SLICES):
            bu = _nn(ub[:, k * LANE:(k + 1) * LANE], bdb_ref[k])
            _put_states(xr_ref, k, bu[:, :sw])
            _put_states(xi_ref, k, bu[:, sw:])
        for s in range(n_sub):
            _s5_scan(xr_ref, xi_ref, ar_ref[...], ai_ref[...], car_r, car_i, ini_r, ini_i, reverse=False, row0=s * S5_TILE)
        vs = [_nn(_get_states(xr_ref, k), bcr_ref[k]) - _nn(_get_states(xi_ref, k), bci_ref[k])
              for k in range(S5_SLICES)]
        v = jnp.concatenate(vs, axis=1) + d_ref[...] * u_t
        v_ref[...] = v
        gg = _gelu(v)
        ya = _s5_tail(gg, _nn(gg, wg_ref[...]) + bg_ref[...], sn_ref[...])
        for s in range(n_sub):
            rows = slice(s * S5_TILE, (s + 1) * S5_TILE)
            ya_ref[rows, :] = _tn_f32(order, ya[rows, :]).astype(BF16)

    return _call(
        body, name, (n_tok // tc,),
        [_rows(tc, D_MODEL), _const((1, S5_LANES)), _const((1, S5_LANES)), _const(bdb.shape), _const(bcr.shape),
         _const(bci.shape), _const((1, D_MODEL)), _const(wglu.shape), _const((1, D_MODEL)), _const((1, D_MODEL))],
        [_rows(tc, D_MODEL), _state_rows(tc), _state_rows(tc), _rows(tc, D_MODEL)],
        [_sds((n_tok, D_MODEL), BF16), _sds((S5_STATE_TILES, n_tok, LANE)), _sds((S5_STATE_TILES, n_tok, LANE)),
         _sds((n_tok, D_MODEL))],
        scratch=[pltpu.VMEM((1, S5_LANES), F32), pltpu.VMEM((1, S5_LANES), F32),
                 pltpu.VMEM((SUBLANE, S5_LANES), F32), pltpu.VMEM((SUBLANE, S5_LANES), F32)],
    )(u, a_r, a_i, bdb, bcr, bci, dsk, wglu, bglu, sn)


def _s5_bwd(dya, v, u, xr, xi, a_r, a_i, bdb, bcr, bci, dsk, wglu, bglu, sn, after, name):
    n_tok = u.shape[0]
    tc = S5_TILE
    nt = n_tok // tc
    sw = S5_SLICE_STATES

    def body(dya_ref, v_ref, u_ref, xr_ref, xi_ref, ar_ref, ai_ref, bdb_ref, bcr_ref, bci_ref, d_ref, wg_ref, bg_ref, sn_ref,
             after_ref, du_ref, gg_ref, dq_ref, gbdb_ref, gbcr_ref, gbci_ref, gar_ref, gai_ref, gd_ref, gbg_ref, gsn_ref,
             gr_ref, gi_ref, car_r, car_i, ini_r, ini_i):
        @pl.when(pl.program_id(0) == 0)
        def _():
            for r in (car_r, car_i, gbdb_ref, gbcr_ref, gbci_ref, gar_ref, gai_ref, gd_ref, gbg_ref, gsn_ref):
                r[...] = jnp.zeros_like(r)

        order = _scan_order(tc)
        u_t = _nn_f32(order, u_ref[...])
        gg, gelu_vjp = jax.vjp(_gelu, v_ref[...])
        _, tail_vjp = jax.vjp(_s5_tail, gg, _nn(gg, wg_ref[...]) + bg_ref[...], sn_ref[...])
        dgg, dq, dsn = tail_vjp(_nn_f32(order, dya_ref[...]))
        (dv,) = gelu_vjp(dgg + _nt(dq, wg_ref[...]))
        gg_ref[...] = gg.astype(BF16)
        dq_ref[...] = dq.astype(BF16)
        gd_ref[...] += jnp.sum(dv * u_t, axis=0, keepdims=True)
        gbg_ref[...] += jnp.sum(dq, axis=0, keepdims=True)
        gsn_ref[...] += dsn
        dvb = dv.astype(BF16)
        for k in range(S5_SLICES):
            dvk = dvb[:, k * LANE:(k + 1) * LANE]
            _put_states(gr_ref, k, _nt(dvk, bcr_ref[k]))
            _put_states(gi_ref, k, -_nt(dvk, bci_ref[k]))
            gbcr_ref[k] += _tn(_get_states(xr_ref, k), dvk)
            gbci_ref[k] -= _tn(_get_states(xi_ref, k), dvk)
        _s5_scan(gr_ref, gi_ref, ar_ref[...], -ai_ref[...], car_r, car_i, ini_r, ini_i, reverse=True,
                 xr_ref=xr_ref, xi_ref=xi_ref, acc_r=gar_ref, acc_i=gai_ref)
        ub = u_t.astype(BF16)
        dus = []
        for k in range(S5_SLICES):
            gk_r, gk_i = _get_states(gr_ref, k).astype(BF16), _get_states(gi_ref, k).astype(BF16)
            bk = bdb_ref[k]
            dus.append(_nt(gk_r, bk[:, :sw]) + _nt(gk_i, bk[:, sw:]))
            uk = ub[:, k * LANE:(k + 1) * LANE]
            gbdb_ref[k, :, :sw] += _tn(uk, gk_r)
            gbdb_ref[k, :, sw:] += _tn(uk, gk_i)
        du_ref[...] = _tn_f32(order, jnp.concatenate(dus, axis=1) + d_ref[...] * dv).astype(BF16)

    rev = functools.partial(_rows, n_tiles=nt)
    return _call(
        body, name, (nt,),
        [rev(tc, D_MODEL), rev(tc, D_MODEL), rev(tc, D_MODEL), _state_rows(tc, nt), _state_rows(tc, nt),
         _const((1, S5_LANES)), _const((1, S5_LANES)), _const(bdb.shape), _const(bcr.shape), _const(bci.shape),
         _const((1, D_MODEL)), _const(wglu.shape), _const((1, D_MODEL)), _const((1, D_MODEL)), HBM],
        [rev(tc, D_MODEL), rev(tc, D_MODEL), rev(tc, D_MODEL), _full(bdb.shape), _full(bcr.shape), _full(bci.shape),
         _full((SUBLANE, S5_LANES)), _full((SUBLANE, S5_LANES)), _full((1, D_MODEL)), _full((1, D_MODEL)), _full((1, D_MODEL))],
        [_sds((n_tok, D_MODEL), BF16), _sds((n_tok, D_MODEL), BF16), _sds((n_tok, D_MODEL), BF16), _sds(bdb.shape), _sds(bcr.shape),
         _sds(bci.shape), _sds((SUBLANE, S5_LANES)), _sds((SUBLANE, S5_LANES)), _sds((1, D_MODEL)), _sds((1, D_MODEL)),
         _sds((1, D_MODEL))],
        scratch=[pltpu.VMEM((S5_STATE_TILES, tc, LANE), F32), pltpu.VMEM((S5_STATE_TILES, tc, LANE), F32),
                 pltpu.VMEM((1, S5_LANES), F32), pltpu.VMEM((1, S5_LANES), F32),
                 pltpu.VMEM((SUBLANE, S5_LANES), F32), pltpu.VMEM((SUBLANE, S5_LANES), F32)],
    )(dya, v, u, xr, xi, a_r, a_i, bdb, bcr, bci, dsk, wglu, bglu, sn, after)


SSD_WIDTH = SSD_HEADS * SSD_HEAD_DIM
SSD_GROUPS = 2
HEADS_PER_GROUP = SSD_HEADS // SSD_GROUPS


def _take(x, axis, start, size):
    n = x.shape[axis]

    def sl(v):
        return lax.slice_in_dim(v, start, start + size, axis=axis)

    @jax.custom_vjp
    def f(v):
        return sl(v)

    def bwd(_, g):
        parts = []
        if start:
            parts.append(jnp.zeros(g.shape[:axis] + (start,) + g.shape[axis + 1:], g.dtype))
        parts.append(g)
        if n - start - size:
            parts.append(jnp.zeros(g.shape[:axis] + (n - start - size,) + g.shape[axis + 1:], g.dtype))
        return (jnp.concatenate(parts, axis=axis) if len(parts) > 1 else g,)

    f.defvjp(lambda v: (sl(v), None), bwd)
    return f(x)


def _lane_of(x, h):
    col = lax.broadcasted_iota(jnp.int32, x.shape, 1)
    return jnp.sum(jnp.where(col == h, x, 0.0), axis=1, keepdims=True)


def _ssd_chunk(xc, z, dt, dtb, alog, dvec, gn, st, nn, nt, tn, cumsum, take):
    t_len = xc.shape[0]
    xa = _silu(xc)
    dtp = _softplus(dt + dtb)
    d_a = dtp * (-jnp.exp(alog))
    row = lax.broadcasted_iota(jnp.int32, (t_len, t_len), 0)
    col = lax.broadcasted_iota(jnp.int32, (t_len, t_len), 1)
    causal = row >= col
    cum = cumsum(causal.astype(F32), d_a)
    eye = (row == col).astype(F32)
    group_width = HEADS_PER_GROUP * SSD_HEAD_DIM
    ys, sts = [], []
    for g in range(SSD_GROUPS):
        bg = take(xa, 1, SSD_WIDTH + g * SSD_STATE, SSD_STATE)
        cg = take(xa, 1, SSD_WIDTH + (SSD_GROUPS + g) * SSD_STATE, SSD_STATE)
        cb = nt(cg, bg)
        x_g = take(xa, 1, g * group_width, group_width)
        st_g = take(st, 0, g * group_width, group_width)
        diag, dt_l, d_l, grow_l, keep_l, last_r = [], [], [], [], [], []
        for r in range(HEADS_PER_GROUP):
            h = g * HEADS_PER_GROUP + r
            cc = _lane_of(cum, h)
            cr = jnp.sum(cc * eye, axis=0, keepdims=True)
            decay = jnp.exp(jnp.where(causal, cc - cr, -1e30))
            dt_h = _lane_of(dtp, h)
            c_last = jnp.sum(jnp.where(row[:, :1] == t_len - 1, cc, 0.0), axis=0, keepdims=True)
            lanes = (t_len, SSD_HEAD_DIM)
            diag.append(nn(cb * decay, take(x_g, 1, r * SSD_HEAD_DIM, SSD_HEAD_DIM) * dt_h))
            dt_l.append(jnp.broadcast_to(dt_h, lanes))
            d_l.append(jnp.broadcast_to(_lane_of(dvec, h), lanes))
            grow_l.append(jnp.broadcast_to(jnp.exp(cc), lanes))
            keep_l.append(jnp.broadcast_to(jnp.exp(c_last - cc), lanes))
            last_r.append(jnp.broadcast_to(jnp.exp(c_last), (SSD_HEAD_DIM, SSD_STATE)))
        side = functools.partial(jnp.concatenate, axis=1)
        xdt_g = x_g * side(dt_l)
        ys.append(side(diag) + side(grow_l) * nt(cg, st_g) + side(d_l) * x_g)
        sts.append(jnp.concatenate(last_r, axis=0) * st_g + tn(xdt_g * side(keep_l), bg))
    y = jnp.concatenate(ys, axis=1) * _silu(z)
    return _rms(y, gn), jnp.concatenate(sts, axis=0)


SSD_TILE = SSD_CHUNK


def _ssd_tile(xc, z, dt, dtb, alog, dvec, gn, st, nn, nt, tn, cumsum, take):
    ys = []
    for c in range(xc.shape[0] // SSD_CHUNK):
        rows = (c * SSD_CHUNK, SSD_CHUNK)
        y, st = _ssd_chunk(take(xc, 0, *rows), take(z, 0, *rows), take(dt, 0, *rows), dtb, alog, dvec, gn, st,
                           nn, nt, tn, cumsum, take)
        ys.append(y)
    return jnp.concatenate(ys, axis=0), st


def _shift_back(cur, prev, j):
    if j == 0:
        return cur
    row = lax.broadcasted_iota(jnp.int32, cur.shape, 0)
    return jnp.where(row < j, pltpu.roll(prev, j, 0), pltpu.roll(cur, j, 0))


def _shift_ahead(cur, nxt, j):
    if j == 0:
        return cur
    n = cur.shape[0]
    row = lax.broadcasted_iota(jnp.int32, cur.shape, 0)
    return jnp.where(row >= n - j, pltpu.roll(nxt, n - j, 0), pltpu.roll(cur, n - j, 0))


def _conv(cur, prev, w, b):
    out = b + w[SSD_CONV - 1:SSD_CONV, :] * cur
    for k in range(SSD_CONV - 1):
        out = out + w[k:k + 1, :] * _shift_back(cur, prev, SSD_CONV - 1 - k)
    return out


def _ssd_fwd(xbc, z, dt, conv_w, conv_b, dtb, alog, dvec, gn, name):
    n_tok = xbc.shape[0]
    tc = SSD_TILE
    nc = n_tok // tc
    st_rows = SSD_HEADS * SSD_HEAD_DIM

    def body(cur_ref, prev_ref, z_ref, dt_ref, w_ref, b_ref, dtb_ref, alog_ref, dvec_ref, gn_ref,
             yb_ref, stin_ref, st_ref):
        i = pl.program_id(0)

        @pl.when(i == 0)
        def _():
            st_ref[...] = jnp.zeros_like(st_ref)

        prev = jnp.where(i > 0, prev_ref[...], 0.0)
        xc = _conv(cur_ref[...], prev, w_ref[...], b_ref[...])
        st = st_ref[...]
        stin_ref[0] = st
        yb, st_new = _ssd_tile(xc, z_ref[...], dt_ref[...], dtb_ref[...], alog_ref[...], dvec_ref[...], gn_ref[...], st,
                               _nn, _nt, _tn, _nn_f32, lambda v, axis, start, size: lax.slice_in_dim(v, start, start + size, axis=axis))
        yb_ref[...] = yb.astype(BF16)
        st_ref[...] = st_new

    return _call(
        body, name, (nc,),
        [_rows(tc, SSD_CONV_DIM), pl.BlockSpec((tc, SSD_CONV_DIM), lambda i: (jnp.maximum(i - 1, 0), 0)),
         _rows(tc, D_MODEL), _rows(tc, LANE), _const((SSD_CONV, SSD_CONV_DIM)), _const((1, SSD_CONV_DIM)),
         _const((1, LANE)), _const((1, LANE)), _const((1, LANE)), _const((1, D_MODEL))],
        [_rows(tc, D_MODEL), pl.BlockSpec((1, st_rows, SSD_STATE), lambda i: (i, 0, 0))],
        [_sds((n_tok, D_MODEL), BF16), _sds((nc, st_rows, SSD_STATE))],
        scratch=[pltpu.VMEM((st_rows, SSD_STATE), F32)],
    )(xbc, xbc, z, dt, conv_w, conv_b, dtb, alog, dvec, gn)


def _ssd_bwd(dyb, xbc, z, dt, stin, conv_w, conv_b, dtb, alog, dvec, gn, name):
    n_tok = xbc.shape[0]
    tc = SSD_TILE
    nc = n_tok // tc
    st_rows = SSD_HEADS * SSD_HEAD_DIM

    def body(dyb_ref, cur_ref, prev_ref, z_ref, dt_ref, stin_ref, w_ref, b_ref, dtb_ref, alog_ref, dvec_ref, gn_ref,
             dxbc_ref, dz_ref, ddt_ref, gw_ref, gb_ref, gdtb_ref, galog_ref, gdvec_ref, ggn_ref,
             dst_ref, dxc_next_ref):
        i = pl.program_id(0)

        @pl.when(i == 0)
        def _():
            for r in (dst_ref, dxc_next_ref, gw_ref, gb_ref, gdtb_ref, galog_ref, gdvec_ref, ggn_ref):
                r[...] = jnp.zeros_like(r)

        cur = cur_ref[...]
        prev = jnp.where(i < nc - 1, prev_ref[...], 0.0)
        w = w_ref[...]
        xc = _conv(cur, prev, w, b_ref[...])
        chunk = functools.partial(_ssd_tile, nn=_nn_d, nt=_nt_d, tn=_tn_d, cumsum=_cumsum_rows, take=_take)
        _, vjp = jax.vjp(chunk, xc, z_ref[...], dt_ref[...], dtb_ref[...], alog_ref[...], dvec_ref[...], gn_ref[...],
                         stin_ref[0])
        dxc, dz, ddt, gdtb, galog, gdvec, ggn, dst = vjp((dyb_ref[...], dst_ref[...]))
        dst_ref[...] = dst
        dz_ref[...] = dz.astype(BF16)
        ddt_ref[...] = ddt.astype(BF16)
        gdtb_ref[...] += gdtb
        galog_ref[...] += galog
        gdvec_ref[...] += gdvec
        ggn_ref[...] += ggn
        dxc_next = dxc_next_ref[...]
        dxbc = w[SSD_CONV - 1:SSD_CONV, :] * dxc
        gws = []
        for k in range(SSD_CONV - 1):
            j = SSD_CONV - 1 - k
            dxbc = dxbc + w[k:k + 1, :] * _shift_ahead(dxc, dxc_next, j)
            gws.append(jnp.sum(dxc * _shift_back(cur, prev, j), axis=0, keepdims=True))
        gws.append(jnp.sum(dxc * cur, axis=0, keepdims=True))
        dxbc_ref[...] = dxbc.astype(BF16)
        gw_ref[...] += jnp.concatenate(gws, axis=0)
        gb_ref[...] += jnp.sum(dxc, axis=0, keepdims=True)
        dxc_next_ref[...] = dxc

    rev = functools.partial(_rows, n_tiles=nc)
    return _call(
        body, name, (nc,),
        [rev(tc, D_MODEL), rev(tc, SSD_CONV_DIM),
         pl.BlockSpec((tc, SSD_CONV_DIM), lambda i: (jnp.maximum(nc - 2 - i, 0), 0)),
         rev(tc, D_MODEL), rev(tc, LANE), pl.BlockSpec((1, st_rows, SSD_STATE), lambda i: (nc - 1 - i, 0, 0)),
         _const((SSD_CONV, SSD_CONV_DIM)), _const((1, SSD_CONV_DIM)), _const((1, LANE)), _const((1, LANE)),
         _const((1, LANE)), _const((1, D_MODEL))],
        [rev(tc, SSD_CONV_DIM), rev(tc, D_MODEL), rev(tc, LANE), _full((SSD_CONV, SSD_CONV_DIM)), _full((1, SSD_CONV_DIM)),
         _full((1, LANE)), _full((1, LANE)), _full((1, LANE)), _full((1, D_MODEL))],
        [_sds((n_tok, SSD_CONV_DIM), BF16), _sds((n_tok, D_MODEL), BF16), _sds((n_tok, LANE), BF16), _sds((SSD_CONV, SSD_CONV_DIM)),
         _sds((1, SSD_CONV_DIM)), _sds((1, LANE)), _sds((1, LANE)), _sds((1, LANE)), _sds((1, D_MODEL))],
        scratch=[pltpu.VMEM((st_rows, SSD_STATE), F32), pltpu.VMEM((tc, SSD_CONV_DIM), F32)],
    )(dyb, xbc, xbc, z, dt, stin, conv_w, conv_b, dtb, alog, dvec, gn)


@jax.custom_vjp
def _expand_cols(x, e):
    return _nn_f32(x, e)


_expand_cols.defvjp(
    lambda x, e: (_nn_f32(x, e), e),
    lambda e, g: (lax.dot_general(g, e, (((1,), (1,)), ((), ())), precision=lax.Precision.HIGHEST,
                                  preferred_element_type=F32), jnp.zeros_like(e)))


def _s5_discretize(lam_re, lam_im, log_step, b_re, b_im, expand):
    step = jnp.exp(log_step)
    mag = jnp.exp(lam_re * step)
    ang = lam_im * step
    a_r = mag * jnp.cos(ang)
    a_i = mag * jnp.sin(ang)
    den = lam_re * lam_re + lam_im * lam_im
    n_r = a_r - 1.0
    coef_r = _expand_cols((n_r * lam_re + a_i * lam_im) / den, expand)
    coef_i = _expand_cols((a_i * lam_re - n_r * lam_im) / den, expand)
    return a_r, a_i, coef_r * b_re - coef_i * b_im, coef_r * b_im + coef_i * b_re


def _expand_matrix():
    p = lax.broadcasted_iota(jnp.int32, (S5_STATE, S5_STATE * S5_GROUP), 0)
    c = lax.broadcasted_iota(jnp.int32, (S5_STATE, S5_STATE * S5_GROUP), 1)
    return (c // S5_GROUP == p).astype(F32)


def _s5_discretize_fwd(lam_re, lam_im, log_step, b_re, b_im, name):
    def body(lr_ref, li_ref, ls_ref, br_ref, bi_ref, ar_ref, ai_ref, bbr_ref, bbi_ref):
        outs = _s5_discretize(lr_ref[...], li_ref[...], ls_ref[...], br_ref[...], bi_ref[...], _expand_matrix())
        for r, o in zip((ar_ref, ai_ref, bbr_ref, bbi_ref), outs):
            r[...] = o

    sq, wide = (S5_GROUPS, S5_STATE), (S5_GROUPS, S5_STATE * S5_GROUP)
    return _call(body, name, (1,), [_full(sq), _full(sq), _full((S5_GROUPS, 1)), _full(wide), _full(wide)],
                 [_full(sq), _full(sq), _full(wide), _full(wide)], [_sds(sq), _sds(sq), _sds(wide), _sds(wide)],
                 )(lam_re, lam_im, log_step, b_re, b_im)


def _s5_discretize_bwd(lam_re, lam_im, log_step, b_re, b_im, g_ar8, g_ai8, g_bbr, g_bbi, after, name):
    def body(lr_ref, li_ref, ls_ref, br_ref, bi_ref, gar_ref, gai_ref, gbbr_ref, gbbi_ref, after_ref,
             glr_ref, gli_ref, gls_ref, gbr_ref, gbi_ref):
        _, vjp = jax.vjp(functools.partial(_s5_discretize, expand=_expand_matrix()),
                         lr_ref[...], li_ref[...], ls_ref[...], br_ref[...], bi_ref[...])
        grads = vjp((jnp.sum(gar_ref[...], axis=0), jnp.sum(gai_ref[...], axis=0), gbbr_ref[...], gbbi_ref[...]))
        for r, g in zip((glr_ref, gli_ref, gls_ref, gbr_ref, gbi_ref), grads):
            r[...] = g

    sq, wide, col = (S5_GROUPS, S5_STATE), (S5_GROUPS, S5_STATE * S5_GROUP), (S5_GROUPS, 1)
    part = (SUBLANE,) + sq
    return _call(body, name, (1,),
                 [_full(sq), _full(sq), _full(col), _full(wide), _full(wide), _full(part), _full(part), _full(wide), _full(wide),
                  HBM],
                 [_full(sq), _full(sq), _full(col), _full(wide), _full(wide)],
                 [_sds(sq), _sds(sq), _sds(col), _sds(wide), _sds(wide)],
                 )(lam_re, lam_im, log_step, b_re, b_im, g_ar8, g_ai8, g_bbr, g_bbi, after)


GROUPS_PER_SLICE = LANE // S5_GROUP


def _block_diag_b(bb):
    t = bb.reshape(S5_SLICES, GROUPS_PER_SLICE, S5_STATE, S5_GROUP)
    eye = jnp.eye(GROUPS_PER_SLICE, dtype=bb.dtype)
    return jnp.einsum("kgph,gf->kghfp", t, eye).reshape(S5_SLICES, LANE, S5_SLICE_STATES)


def _block_diag_b_inv(m):
    t = m.reshape(S5_SLICES, GROUPS_PER_SLICE, S5_GROUP, GROUPS_PER_SLICE, S5_STATE)
    return jnp.einsum("kghgp->kgph", t).reshape(S5_GROUPS, S5_STATE * S5_GROUP)


def _block_diag_c(c):
    t = c.reshape(S5_SLICES, GROUPS_PER_SLICE, S5_GROUP, S5_STATE)
    eye = jnp.eye(GROUPS_PER_SLICE, dtype=c.dtype)
    return jnp.einsum("kghp,gf->kgpfh", t, eye).reshape(S5_SLICES, S5_SLICE_STATES, LANE)


def _block_diag_c_inv(m):
    t = m.reshape(S5_SLICES, GROUPS_PER_SLICE, S5_STATE, GROUPS_PER_SLICE, S5_GROUP)
    return jnp.einsum("kgpgh->kghp", t).reshape(S5_GROUPS, S5_GROUP, S5_STATE)


def _pad_lanes(v):
    return jnp.pad(v.reshape(1, -1), ((0, 0), (0, LANE - v.shape[0])))


def _prepare_layer(w, blk, i, after):
    p = {}
    p["wu"], p["wz"], p["wx"], p["wd"] = _w_in_split(blk["w_in"], after, name=f"w_in_split_{i}")
    p["nm"] = w["norm_mix"][i].reshape(1, D_MODEL)
    p["lam_re"], p["lam_im"] = w["s5_lam_re"][i], w["s5_lam_im"][i]
    p["log_step"] = w["s5_log_step"][i].reshape(S5_GROUPS, 1)
    p["b_re"] = w["s5_b_re"][i].reshape(S5_GROUPS, S5_STATE * S5_GROUP)
    p["b_im"] = w["s5_b_im"][i].reshape(S5_GROUPS, S5_STATE * S5_GROUP)
    a_r, a_i, bb_r, bb_i = _s5_discretize_fwd(p["lam_re"], p["lam_im"], p["log_step"], p["b_re"], p["b_im"],
                                              name=f"s5_discretize_{i}")
    p["a_r"], p["a_i"] = a_r.reshape(1, S5_LANES), a_i.reshape(1, S5_LANES)
    p["bdb"] = jnp.concatenate([_block_diag_b(bb_r), _block_diag_b(bb_i)], axis=2).astype(BF16)
    p["bcr"] = _block_diag_c(w["s5_c_re"][i]).astype(BF16)
    p["bci"] = _block_diag_c(w["s5_c_im"][i]).astype(BF16)
    p["dsk"] = w["s5_d"][i].reshape(1, D_MODEL)
    p["wglu"] = blk["s5_w_glu"].reshape(D_MODEL, D_MODEL)
    p["bglu"] = w["s5_b_glu"][i].reshape(1, D_MODEL)
    p["sn"] = w["s5_norm"][i].reshape(1, D_MODEL)
    p["conv_w"] = blk["ssd_conv_w"]
    p["conv_b"] = w["ssd_conv_b"][i].reshape(1, SSD_CONV_DIM)
    p["dtb"] = _pad_lanes(w["ssd_dt_bias"][i])
    p["alog"] = _pad_lanes(w["ssd_a_log"][i])
    p["dvec"] = _pad_lanes(w["ssd_d"][i])
    p["gn"] = w["ssd_norm"][i].reshape(1, D_MODEL)
    if "w_out" in blk:
        p.update(_late_weights(blk))
    p["nf"] = w["norm_ffn"][i].reshape(1, D_MODEL)
    return p


def _late_weights(blk):
    p = {}
    p["wo"] = blk["w_out"].reshape(2 * D_MODEL, D_MODEL)
    p["wg"], p["wup"], p["wdn"] = (blk[n].reshape(FFN_PAD, D_MODEL) for n in ("w_gate", "w_up", "w_down"))
    return p


def _layer_fwd(x0, p, i, late=None):
    u, z, xbc, dt = _inproj_fwd(x0, p["nm"], p["wu"], p["wz"], p["wx"], p["wd"], name=f"inproj_fwd_{i}")
    ya, xr, xi, v = _s5_fwd(u, p["a_r"], p["a_i"], p["bdb"], p["bcr"], p["bci"], p["dsk"], p["wglu"], p["bglu"], p["sn"],
                            name=f"s5_fwd_{i}")
    yb, stin = _ssd_fwd(xbc, z, dt, p["conv_w"], p["conv_b"], p["dtb"], p["alog"], p["dvec"], p["gn"], name=f"ssd_fwd_{i}")
    if late is not None:
        p.update(_late_weights(late(yb)))
    x1, x2 = _mix_ffn_fwd(x0, ya, yb, p["wo"], p["nf"], p["wg"], p["wup"], p["wdn"], name=f"mix_ffn_fwd_{i}")
    return x2, dict(x0=x0, u=u, z=z, xbc=xbc, dt=dt, xr=xr, xi=xi, v=v, stin=stin, ya=ya, yb=yb, x1=x1)


def _layer_bwd(dx2, s, p, i, after, between=None):
    g = {}
    dx1, dya, dyb, h2, act, dgt, dup, dx2b, dx1b, g_nf = _mix_ffn_bwd(
        s["x1"], dx2, p["wo"], p["nf"], p["wg"], p["wup"], p["wdn"], after, name=f"mix_ffn_bwd_{i}")
    g["norm_ffn"] = g_nf.reshape(D_MODEL)
    g["w_down"] = _matmul_tn_lhs_blocks(act, dx2b, FFN_BLOCK_PAD, FFN_BLOCK, name=f"grad_w_down_{i}")
    g["w_gate"] = _matmul_tn_lhs_blocks(dgt, h2, FFN_BLOCK_PAD, FFN_BLOCK, name=f"grad_w_gate_{i}")
    g["w_up"] = _matmul_tn_lhs_blocks(dup, h2, FFN_BLOCK_PAD, FFN_BLOCK, name=f"grad_w_up_{i}")
    g["w_out"] = _matmul_tn_pair(s["ya"], s["yb"], dx1b, name=f"grad_w_out_{i}").reshape(N_DEV, 2 * D_MODEL // N_DEV, D_MODEL)
    if between is not None:
        after = between(g)

    (du, gg, dq, g_bdb, g_bcr, g_bci, g_ar8, g_ai8, g_d, g_bglu, g_sn) = _s5_bwd(
        dya, s["v"], s["u"], s["xr"], s["xi"], p["a_r"], p["a_i"], p["bdb"], p["bcr"], p["bci"], p["dsk"], p["wglu"],
        p["bglu"], p["sn"], after, name=f"s5_bwd_{i}")
    g["s5_w_glu"] = _matmul_tn(gg, dq, name=f"grad_w_glu_{i}").reshape(N_DEV, D_MODEL // N_DEV, D_MODEL)
    g["s5_d"], g["s5_b_glu"], g["s5_norm"] = g_d.reshape(D_MODEL), g_bglu.reshape(D_MODEL), g_sn.reshape(D_MODEL)
    g["s5_raw"] = (g_bcr, g_bci, g_ar8, g_ai8, g_bdb)

    dxbc, dz, ddt, g_cw, g_cb, g_dtb, g_alog, g_dvec, g_gn = _ssd_bwd(
        dyb, s["xbc"], s["z"], s["dt"], s["stin"], p["conv_w"], p["conv_b"], p["dtb"], p["alog"], p["dvec"], p["gn"],
        name=f"ssd_bwd_{i}")
    g["ssd_conv_w"] = jnp.moveaxis(g_cw.reshape(SSD_CONV, N_DEV, SSD_CONV_DIM // N_DEV), 1, 0)
    g["ssd_conv_b"] = g_cb.reshape(SSD_CONV_DIM)
    g["ssd_dt_bias"], g["ssd_a_log"], g["ssd_d"] = g_dtb[0, :SSD_HEADS], g_alog[0, :SSD_HEADS], g_dvec[0, :SSD_HEADS]
    g["ssd_norm"] = g_gn.reshape(D_MODEL)

    dx0, h, g_nm = _inproj_bwd(s["x0"], p["nm"], du, dz, dxbc, ddt, dx1, p["wu"], p["wz"], p["wx"], p["wd"],
                               name=f"inproj_bwd_{i}")
    g["norm_mix"] = g_nm.reshape(D_MODEL)
    g["w_in"] = _w_in_grad_blocks(
        _matmul_tn(h, du, name=f"grad_w_in_u_{i}"), _matmul_tn(h, dz, name=f"grad_w_in_z_{i}"),
        _matmul_tn(h, dxbc, name=f"grad_w_in_xbc_{i}"), _matmul_tn(h, ddt, name=f"grad_w_in_dt_{i}"),
        name=f"grad_w_in_blocks_{i}")
    return dx0, g


def _s5_param_grads(g, p, i, after):
    g_bcr, g_bci, g_ar8, g_ai8, g_bdb = g.pop("s5_raw")
    g["s5_c_re"], g["s5_c_im"] = _block_diag_c_inv(g_bcr), _block_diag_c_inv(g_bci)
    sq = (SUBLANE, S5_GROUPS, S5_STATE)
    g_lr, g_li, g_ls, g_br, g_bi = _s5_discretize_bwd(
        p["lam_re"], p["lam_im"], p["log_step"], p["b_re"], p["b_im"], g_ar8.reshape(sq), g_ai8.reshape(sq),
        _block_diag_b_inv(g_bdb[:, :, :S5_SLICE_STATES]), _block_diag_b_inv(g_bdb[:, :, S5_SLICE_STATES:]), after,
        name=f"s5_discretize_bwd_{i}")
    g["s5_lam_re"], g["s5_lam_im"], g["s5_log_step"] = g_lr, g_li, g_ls.reshape(S5_GROUPS)
    b_shape = (S5_GROUPS, S5_STATE, S5_GROUP)
    g["s5_b_re"], g["s5_b_im"] = g_br.reshape(b_shape), g_bi.reshape(b_shape)


def _example_step(x, target, w, blks):
    prepared = [_prepare_layer(w, blks[i], i, x) for i in range(DEPTH)]
    saved = []
    h = x
    for i in range(DEPTH):
        h, s = _layer_fwd(h, prepared[i], i)
        saved.append(s)
    loss, dh, g_final = _loss_head(h, w["norm_final"].reshape(1, D_MODEL), target, name="loss_head")
    layer_grads = [None] * DEPTH
    for i in reversed(range(DEPTH)):
        dh, layer_grads[i] = _layer_bwd(dh, saved[i], prepared[i], i, x)
        _s5_param_grads(layer_grads[i], prepared[i], i, x)
    return loss, dh, layer_grads, g_final.reshape(D_MODEL)


def _mesh_position():
    return lax.axis_index("x"), lax.axis_index("y"), lax.axis_index("c")


def _peer(pos, k):
    x, y, c = pos
    px = 1 - x if k & 4 else x
    py = 1 - y if k & 2 else y
    pc = 1 - c if k & 1 else c
    return (px, py, pc), 4 * px + 2 * py + pc


HBM = pl.BlockSpec(memory_space=pl.ANY)


def _run_copies(local, remote):
    for cp in local + remote:
        cp.start()
    for cp in remote:
        cp.wait_recv()
    for cp in remote:
        cp.wait_send()
    for cp in local:
        cp.wait()


def _comm_scratch(n_units):
    return [pltpu.SemaphoreType.DMA((n_units, N_DEV - 1)), pltpu.SemaphoreType.DMA((n_units, N_DEV - 1)),
            pltpu.SemaphoreType.DMA((n_units,))]


def _gather_blocks(arrays, layered, name):
    units, out_shapes = [], []
    for j, (a, lay) in enumerate(zip(arrays, layered)):
        for layer in (range(a.shape[0]) if lay else (None,)):
            units.append((j, layer, len(out_shapes)))
            out_shapes.append(_sds((N_DEV,) + (a.shape[1:] if lay else a.shape), a.dtype))
    n_in = len(arrays)
    other_chips = (4, 2, 6)

    def body(*refs):
        ins, outs = refs[:n_in], refs[n_in:n_in + len(out_shapes)]
        send_sems, recv_sems, local_sems = refs[n_in + len(out_shapes):]
        pos = _mesh_position()
        me = 4 * pos[0] + 2 * pos[1] + pos[2]
        sibling, _ = _peer(pos, 1)
        local, own, passed = [], [], []
        for u, (j, layer, o) in enumerate(units):
            src = ins[j] if layer is None else ins[j].at[layer]
            local.append(pltpu.make_async_copy(src, outs[o].at[me], local_sems.at[u]))

            def copy(sem, src_ref, slot, to, u=u, o=o):
                return pltpu.make_async_remote_copy(
                    src_ref=src_ref, dst_ref=outs[o].at[slot], send_sem=send_sems.at[u, sem], recv_sem=recv_sems.at[u, sem],
                    device_id=to, device_id_type=MESH_ID)

            own.append([copy(0, src, me, sibling)] + [copy(1 + i, src, me, _peer(pos, k)[0]) for i, k in enumerate(other_chips)])
            passed.append([copy(4 + i, outs[o].at[_peer(pos, k)[1]], _peer(pos, k)[1], sibling) for i, k in enumerate(other_chips)])
        for cp in local + [c for unit in own for c in unit]:
            cp.start()
        for u in range(len(units)):
            for i in range(len(other_chips)):
                own[u][1 + i].wait_recv()
                passed[u][i].start()
        for u in range(len(units)):
            own[u][0].wait_recv()
            for cp in passed[u]:
                cp.wait_recv()
        for cp in [c for unit in own + passed for c in unit]:
            cp.wait_send()
        for cp in local:
            cp.wait()

    outs = pl.pallas_call(body, name=name, in_specs=[HBM] * n_in, out_specs=[HBM] * len(out_shapes), out_shape=out_shapes,
                          scratch_shapes=_comm_scratch(len(units)))(*arrays)
    grouped = [[] for _ in arrays]
    for j, _, o in units:
        grouped[j].append(outs[o])
    return [tuple(g) for g in grouped]


def _exchange_blocks(entries, name):
    units, flat_in, out_shapes = [], [], []
    for j, entry in enumerate(entries):
        for layer, a in enumerate(entry):
            units.append((len(flat_in), layer, j))
            flat_in.append(a)
        out_shapes.append(_sds((N_DEV, len(entry)) + entry[0].shape[1:], entry[0].dtype))
    n_in = len(flat_in)

    def body(*refs):
        ins, outs = refs[:n_in], refs[n_in:n_in + len(out_shapes)]
        send_sems, recv_sems, local_sems = refs[n_in + len(out_shapes):]
        pos = _mesh_position()
        me = 4 * pos[0] + 2 * pos[1] + pos[2]
        local, remote = [], []
        for u, (i, layer, o) in enumerate(units):
            local.append(pltpu.make_async_copy(ins[i].at[me], outs[o].at[me, layer], local_sems.at[u]))
            for k in range(1, N_DEV):
                peer, peer_index = _peer(pos, k)
                remote.append(pltpu.make_async_remote_copy(
                    src_ref=ins[i].at[peer_index], dst_ref=outs[o].at[me, layer], send_sem=send_sems.at[u, k - 1],
                    recv_sem=recv_sems.at[u, k - 1], device_id=peer, device_id_type=MESH_ID))
        _run_copies(local, remote)

    return pl.pallas_call(body, name=name, in_specs=[HBM] * n_in, out_specs=[HBM] * len(out_shapes), out_shape=out_shapes,
                          scratch_shapes=_comm_scratch(len(units)))(*flat_in)


SEM = pl.BlockSpec(memory_space=pltpu.SEMAPHORE)
SIDE_EFFECT = pltpu.SideEffectType.DATAFLOW_SIDE_EFFECTING


def _own_slots(arrays, indexed, me, name):
    lands = []
    for u, a in enumerate(arrays):
        block = a.shape[1:] if indexed else a.shape
        rows, cols = _size(block[:-1]), block[-1]
        tr = _row_tile(rows, cap=512)

        def body(me_ref, src_ref, out_ref):
            out_ref[...] = src_ref[...]

        src_spec = (pl.BlockSpec((None, tr, cols), lambda i, me_ref: (me_ref[0], i, 0)) if indexed
                    else pl.BlockSpec((tr, cols), lambda i, me_ref: (i, 0)))
        land = pl.pallas_call(
            body, name=f"{name}_{u}", out_shape=_sds((N_DEV, rows, cols), a.dtype),
            grid_spec=pltpu.PrefetchScalarGridSpec(
                num_scalar_prefetch=1, grid=(rows // tr,), in_specs=[src_spec],
                out_specs=pl.BlockSpec((None, tr, cols), lambda i, me_ref: (me_ref[0], i, 0))),
        )(me, a.reshape((N_DEV, rows, cols) if indexed else (rows, cols)))
        lands.append(land.reshape((N_DEV,) + block))
    return lands


def _split_copies(srcs, lands, send_sems, recv_sems, indexed):
    pos = _mesh_position()
    me = 4 * pos[0] + 2 * pos[1] + pos[2]
    copies = []
    for u, (src, land) in enumerate(zip(srcs, lands)):
        for k in range(1, N_DEV):
            peer, peer_index = _peer(pos, k)
            copies.append(pltpu.make_async_remote_copy(
                src_ref=src.at[peer_index] if indexed else src, dst_ref=land.at[me],
                send_sem=send_sems.at[u * (N_DEV - 1) + k - 1], recv_sem=recv_sems.at[u * (N_DEV - 1) + k - 1],
                device_id=peer, device_id_type=MESH_ID))
    return copies


def _exchange_start(arrays, lands, indexed, name):
    n = len(arrays)

    def body(*refs):
        srcs, zones = refs[:n], refs[n:2 * n]
        send_sems, recv_sems = refs[2 * n], refs[2 * n + 1]
        token = refs[-1]
        for cp in _split_copies(srcs, zones, send_sems, recv_sems, indexed):
            cp.start()
        token[...] = jnp.zeros_like(token)

    sem_shape = pltpu.SemaphoreType.DMA((n * (N_DEV - 1),))
    outs = pl.pallas_call(
        body, name=name, in_specs=[HBM] * (2 * n),
        out_specs=[SEM, SEM] + [HBM] * (2 * n) + [pl.BlockSpec(memory_space=pltpu.VMEM)],
        out_shape=[sem_shape, sem_shape] + [pltpu.HBM(a.shape, a.dtype) for a in list(arrays) + list(lands)]
        + [_sds((SUBLANE, LANE))],
        input_output_aliases={i: 2 + i for i in range(2 * n)},
        compiler_params=pltpu.CompilerParams(has_side_effects=SIDE_EFFECT),
    )(*[pltpu.with_memory_space_constraint(a, pltpu.HBM) for a in list(arrays) + list(lands)])
    return outs[0], outs[1], outs[2:2 + n], outs[2 + n:2 + 2 * n], outs[-1]


def _exchange_wait(send_sems, recv_sems, arrays, lands, after, indexed, name):
    n = len(arrays)

    def body(*refs):
        srcs, zones = refs[:n], refs[n:2 * n]
        s_sems, r_sems = refs[2 * n], refs[2 * n + 1]
        for cp in _split_copies(srcs, zones, s_sems, r_sems, indexed):
            cp.wait_send()
            cp.wait_recv()

    outs = pl.pallas_call(
        body, name=name, in_specs=[HBM] * (2 * n) + [SEM, SEM, HBM],
        out_specs=[HBM] * (2 * n), out_shape=[pltpu.HBM(a.shape, a.dtype) for a in list(arrays) + list(lands)],
        input_output_aliases={i: i for i in range(2 * n)},
        compiler_params=pltpu.CompilerParams(has_side_effects=SIDE_EFFECT),
    )(*arrays, *lands, send_sems, recv_sems, after)
    return outs[n:]


SUM_TILE = 512


def _adamw(w, g, m, v):
    m = ADAM_B1 * m + (1.0 - ADAM_B1) * g
    v = ADAM_B2 * v + (1.0 - ADAM_B2) * (g * g)
    m_hat = m / (1.0 - ADAM_B1 ** ADAM_STEP)
    v_hat = v / (1.0 - ADAM_B2 ** ADAM_STEP)
    return -ADAM_LR * (m_hat / (jnp.sqrt(v_hat) + ADAM_EPS) + ADAM_WD * w), m, v


def _sum_adamw(recv, w, m, v, layer, others, name):
    _, rows, cols = w.shape
    tr = _row_tile(rows, cap=256)

    def body(r_ref, w_ref, m_ref, v_ref, *rest):
        g_ref, d_ref, mo_ref, vo_ref = rest[-4:]
        g = r_ref[0].astype(F32)
        for j in range(1, N_DEV):
            g = g + r_ref[j].astype(F32)
        g_ref[...] = g
        d_ref[...], mo_ref[...], vo_ref[...] = _adamw(w_ref[...], g, m_ref[...], v_ref[...])

    blk = pl.BlockSpec((None, tr, cols), lambda i: (layer, i, 0))
    carried = list(others) if others is not None else []
    return pl.pallas_call(
        body, name=name, grid=(rows // tr,),
        in_specs=[pl.BlockSpec((N_DEV, tr, cols), lambda i: (0, i, 0)), blk, blk, blk] + [HBM] * len(carried),
        out_specs=[blk] * 4, out_shape=[_sds(w.shape)] * 4,
        input_output_aliases={4 + k: k for k in range(len(carried))},
        compiler_params=pltpu.CompilerParams(dimension_semantics=("arbitrary",), vmem_limit_bytes=VMEM_LIMIT),
    )(recv, w, m, v, *carried)


def _sum_senders(recv, name):
    _, rows, cols = recv.shape
    tr = _row_tile(rows, cap=256)

    def body(r_ref, g_ref):
        g = r_ref[0].astype(F32)
        for j in range(1, N_DEV):
            g = g + r_ref[j].astype(F32)
        g_ref[...] = g

    return _call(body, name, (rows // tr,), [pl.BlockSpec((N_DEV, tr, cols), lambda i: (0, i, 0))], [_rows(tr, cols)],
                 [_sds((rows, cols))])(recv)[0]


def _adamw_blocks(g, w, m, v, name):
    n_lay, rows, cols = w.shape
    tr = _row_tile(rows, cap=256)

    def body(g_ref, w_ref, m_ref, v_ref, d_ref, mo_ref, vo_ref):
        d_ref[...], mo_ref[...], vo_ref[...] = _adamw(w_ref[...], g_ref[...], m_ref[...], v_ref[...])

    blk = pl.BlockSpec((None, tr, cols), lambda l, i: (l, i, 0))
    return pl.pallas_call(
        body, name=name, grid=(n_lay, rows // tr), in_specs=[blk] * 4, out_specs=[blk] * 3, out_shape=[_sds(w.shape)] * 3,
        compiler_params=pltpu.CompilerParams(dimension_semantics=("arbitrary", "arbitrary"), vmem_limit_bytes=VMEM_LIMIT),
    )(g, w, m, v)


def _sum_slots(recv, name):
    rows = recv.shape[1]

    def body(r_ref, g_ref):
        g = r_ref[0].astype(F32)
        for j in range(1, N_DEV):
            g = g + r_ref[j].astype(F32)
        g_ref[...] = g

    return _call(body, name, (1,), [_full(recv.shape)], [_full((rows, LANE))], [_sds((rows, LANE))])(recv)[0]


def _adamw_rows(g, w, m, v, name):
    rows, cols = w.shape
    tr = _row_tile(rows)

    def body(g_ref, w_ref, m_ref, v_ref, d_ref, mo_ref, vo_ref):
        d_ref[...], mo_ref[...], vo_ref[...] = _adamw(w_ref[...], g_ref[...], m_ref[...], v_ref[...])

    blk = _rows(tr, cols)
    return _call(body, name, (rows // tr,), [blk] * 4, [blk] * 3, [_sds((rows, cols))] * 3)(g, w, m, v)


def _row_tile(rows, cap=1024):
    if rows % SUBLANE:
        return rows
    best = SUBLANE
    for t in range(SUBLANE, cap + 1, SUBLANE):
        if rows % t == 0:
            best = t
    return best


BIG = (("w_in", (DEPTH, D_MODEL, IN_PROJ // N_DEV), 2),
       ("s5_w_glu", (DEPTH, D_MODEL // N_DEV, D_MODEL), 1),
       ("ssd_conv_w", (DEPTH, SSD_CONV, SSD_CONV_DIM // N_DEV), 2),
       ("w_out", (DEPTH, 2 * D_MODEL // N_DEV, D_MODEL), 1),
       ("w_gate", (DEPTH, D_MODEL, FFN_HIDDEN // N_DEV), 2),
       ("w_up", (DEPTH, D_MODEL, FFN_HIDDEN // N_DEV), 2),
       ("w_down", (DEPTH, FFN_HIDDEN // N_DEV, D_MODEL), 1))
SMALL = (("norm_mix", (DEPTH, D_MODEL)), ("s5_lam_re", (DEPTH, S5_GROUPS, S5_STATE)), ("s5_lam_im", (DEPTH, S5_GROUPS, S5_STATE)),
         ("s5_log_step", (DEPTH, S5_GROUPS)), ("s5_b_re", (DEPTH, S5_GROUPS, S5_STATE, S5_GROUP)),
         ("s5_b_im", (DEPTH, S5_GROUPS, S5_STATE, S5_GROUP)), ("s5_c_re", (DEPTH, S5_GROUPS, S5_GROUP, S5_STATE)),
         ("s5_c_im", (DEPTH, S5_GROUPS, S5_GROUP, S5_STATE)), ("s5_d", (DEPTH, D_MODEL)), ("s5_b_glu", (DEPTH, D_MODEL)),
         ("s5_norm", (DEPTH, D_MODEL)), ("ssd_conv_b", (DEPTH, SSD_CONV_DIM)), ("ssd_dt_bias", (DEPTH, SSD_HEADS)),
         ("ssd_a_log", (DEPTH, SSD_HEADS)), ("ssd_d", (DEPTH, SSD_HEADS)), ("ssd_norm", (DEPTH, D_MODEL)),
         ("norm_ffn", (DEPTH, D_MODEL)), ("norm_final", (D_MODEL,)))
WEIGHT_ORDER = ("norm_mix", "w_in", "s5_lam_re", "s5_lam_im", "s5_log_step", "s5_b_re", "s5_b_im", "s5_c_re", "s5_c_im", "s5_d",
                "s5_w_glu", "s5_b_glu", "s5_norm", "ssd_conv_w", "ssd_conv_b", "ssd_dt_bias", "ssd_a_log", "ssd_d", "ssd_norm",
                "w_out", "norm_ffn", "w_gate", "w_up", "w_down", "norm_final")


def _size(shape):
    n = 1
    for s in shape:
        n *= s
    return n


def _round_up(n, m):
    return -(-n // m) * m


SMALL_SIZE = sum(_size(s) for _, s in SMALL)
SMALL_ROWS = _round_up(-(-SMALL_SIZE // (N_DEV * LANE)), SUBLANE)


def _pack(parts, rows, dtype):
    flat = jnp.concatenate([p.reshape(-1).astype(dtype) for p in parts])
    return jnp.pad(flat, (0, rows * LANE - flat.shape[0])).reshape(rows, LANE)


def _unpack(flat, specs):
    out, off = {}, 0
    flat = flat.reshape(-1)
    for name, shape in specs:
        out[name] = flat[off:off + _size(shape)].reshape(shape)
        off += _size(shape)
    return out


def kernel(x, norm_mix, w_in, s5_lam_re, s5_lam_im, s5_log_step, s5_b_re, s5_b_im, s5_c_re, s5_c_im, s5_d, s5_w_glu, s5_b_glu, s5_norm, ssd_conv_w, ssd_conv_b, ssd_dt_bias, ssd_a_log, ssd_d, ssd_norm, w_out, norm_ffn, w_gate, w_up, w_down, norm_final, loss_target, m_norm_mix, m_w_in, m_s5_lam_re, m_s5_lam_im, m_s5_log_step, m_s5_b_re, m_s5_b_im, m_s5_c_re, m_s5_c_im, m_s5_d, m_s5_w_glu, m_s5_b_glu, m_s5_norm, m_ssd_conv_w, m_ssd_conv_b, m_ssd_dt_bias, m_ssd_a_log, m_ssd_d, m_ssd_norm, m_w_out, m_norm_ffn, m_w_gate, m_w_up, m_w_down, m_norm_final, v_norm_mix, v_w_in, v_s5_lam_re, v_s5_lam_im, v_s5_log_step, v_s5_b_re, v_s5_b_im, v_s5_c_re, v_s5_c_im, v_s5_d, v_s5_w_glu, v_s5_b_glu, v_s5_norm, v_ssd_conv_w, v_ssd_conv_b, v_ssd_dt_bias, v_ssd_a_log, v_ssd_d, v_ssd_norm, v_w_out, v_norm_ffn, v_w_gate, v_w_up, v_w_down, v_norm_final):
    given = dict(locals())
    w = {n: given[n] for n in WEIGHT_ORDER}
    m = {n: given["m_" + n] for n in WEIGHT_ORDER}
    v = {n: given["v_" + n] for n in WEIGHT_ORDER}
    big_names = tuple(n for n, _, _ in BIG)
    matmul_names = tuple(n for n in big_names if n != "ssd_conv_w")

    conv_hi = w["ssd_conv_w"].astype(BF16)
    conv_lo = (w["ssd_conv_w"] - conv_hi.astype(F32)).astype(BF16)
    row_pad = ((0, 0), (0, FFN_BLOCK_PAD - FFN_BLOCK), (0, 0))
    as_rows = {"w_gate": jnp.swapaxes(w["w_gate"], 1, 2), "w_up": jnp.swapaxes(w["w_up"], 1, 2), "w_down": w["w_down"]}
    to_send = [jnp.pad(as_rows[n].astype(BF16), row_pad) if n in as_rows else w[n].astype(BF16) for n in matmul_names]

    def layer_blocks(i):
        return [a[i] for a in to_send] + [jnp.stack([conv_hi[i], conv_lo[i]])]

    def conv_taps(pair):
        pair = pair.astype(F32)
        return jnp.moveaxis(pair[:, 0] + pair[:, 1], 0, 1).reshape(SSD_CONV, SSD_CONV_DIM)

    def as_layer_weights(gathered):
        blk = dict(zip(matmul_names, gathered))
        blk["ssd_conv_w"] = conv_taps(gathered[-1])
        return blk

    gathered0 = [g[0] for g in _gather_blocks(layer_blocks(0), [False] * (len(matmul_names) + 1), name="gather_weights_0")]
    blocks1 = layer_blocks(1)
    me = (4 * lax.axis_index("x") + 2 * lax.axis_index("y") + lax.axis_index("c")).astype(jnp.int32).reshape(1)
    n_first = 2
    first1, then1 = blocks1[:n_first] + blocks1[-1:], blocks1[n_first:-1]
    sems1a = _exchange_start(first1, _own_slots(first1, False, me, name="gather_own_1a"), False, name="gather_start_1a")
    sems1b = _exchange_start(then1, _own_slots(then1, False, me, name="gather_own_1b"), False, name="gather_start_1b")
    started = sems1a[-1] + sems1b[-1]
    prepared = [_prepare_layer(w, as_layer_weights(gathered0), 0, started), None]
    saved = [None, None]
    h, saved[0] = _layer_fwd(x[0], prepared[0], 0)
    arrived = _exchange_wait(*sems1a[:4], h, False, name="gather_wait_1a")
    blk1 = dict(zip(matmul_names[:n_first], arrived))
    blk1["ssd_conv_w"] = conv_taps(arrived[-1])
    prepared[1] = _prepare_layer(w, blk1, 1, started)

    def rest_of_layer1(after):
        return dict(zip(matmul_names[n_first:], _exchange_wait(*sems1b[:4], after, False, name="gather_wait_1b")))

    h, saved[1] = _layer_fwd(h, prepared[1], 1, late=rest_of_layer1)
    loss, dh, g_final = _loss_head(h, w["norm_final"].reshape(1, D_MODEL), loss_target[0], name="loss_head")

    layer_grads = [None, None]
    dh, layer_grads[1] = _layer_bwd(dh, saved[1], prepared[1], 1, started)
    slots1 = [layer_grads[1][n] for n in big_names]
    sems2 = _exchange_start(slots1, _own_slots(slots1, True, me, name="exchange_own_1"), True, name="exchange_start_1")
    early_names = ("w_out", "w_gate", "w_up", "w_down")
    late_names = tuple(n for n in big_names if n not in early_names)
    early = {}

    def send_early(g):
        slots = [g[n] for n in early_names]
        early["sems"] = _exchange_start(slots, _own_slots(slots, True, me, name="exchange_own_0"), True, name="exchange_start_0")
        return early["sems"][-1]

    grad_x, layer_grads[0] = _layer_bwd(dh, saved[0], prepared[0], 0, sems2[-1], between=send_early)

    late = [layer_grads[0][n] for n in late_names]
    sems3 = _exchange_start(late, _own_slots(late, True, me, name="exchange_own_late"), True, name="exchange_start_late")
    for i in range(DEPTH):
        _s5_param_grads(layer_grads[i], prepared[i], i, sems3[-1])
    small = jnp.concatenate([g_final.reshape(-1) if n == "norm_final"
                             else jnp.stack([layer_grads[i][n] for i in range(DEPTH)]).reshape(-1) for n, _ in SMALL])
    small_slots = [jnp.pad(small, (0, N_DEV * SMALL_ROWS * LANE - small.shape[0])).reshape(N_DEV, SMALL_ROWS, LANE)]
    sems4 = _exchange_start(small_slots, _own_slots(small_slots, True, me, name="exchange_own_small"), True,
                            name="exchange_start_small")
    received1 = _exchange_wait(*sems2[:4], sems4[-1], True, name="exchange_wait_1")
    received_early = _exchange_wait(*early["sems"][:4], sems4[-1], True, name="exchange_wait_0")
    received0 = dict(zip(early_names, received_early))

    transposed = ("w_gate", "w_up")
    layer1 = {n: _sum_adamw(received1[j], w[n], m[n], v[n], 1, None, name=f"sum_adamw_{n}_1")
              for j, n in enumerate(big_names) if n not in transposed}
    results = {}
    for n in transposed:
        recv = (received0[n], received1[big_names.index(n)])
        g = jnp.stack([jnp.swapaxes(_sum_senders(recv[i], name=f"sum_{n}_{i}"), 0, 1) for i in range(DEPTH)])
        results[n] = [g, *_adamw_blocks(g, w[n], m[n], v[n], name=f"adamw_{n}")]
    for n in early_names:
        if n not in transposed:
            results[n] = _sum_adamw(received0[n], w[n], m[n], v[n], 0, layer1[n], name=f"sum_adamw_{n}_0")
    received_small = _exchange_wait(*sems4[:4], results["w_down"][0], True, name="exchange_wait_small")
    g_part = [_sum_slots(received_small[0], name="sum_replicated")]
    sems5 = _exchange_start(g_part, _own_slots(g_part, False, me, name="gather_own_replicated"), False,
                            name="gather_start_replicated")
    received_late = _exchange_wait(*sems3[:4], sems5[-1], True, name="exchange_wait_late")
    for n, recv in zip(late_names, received_late):
        results[n] = _sum_adamw(recv, w[n], m[n], v[n], 0, layer1[n], name=f"sum_adamw_{n}_0")
    g_small = _exchange_wait(*sems5[:4], results[late_names[-1]][0], False, name="gather_wait_replicated")[0]
    for n, g in _unpack(g_small, SMALL).items():
        as_rows = (-1, g.shape[-1])
        d_n, m_n, v_n = _adamw_rows(g.reshape(as_rows), w[n].reshape(as_rows), m[n].reshape(as_rows), v[n].reshape(as_rows),
                                    name=f"adamw_{n}")
        results[n] = [g, d_n.reshape(g.shape), m_n.reshape(g.shape), v_n.reshape(g.shape)]

    outs = [results[n][k] for k in range(4) for n in WEIGHT_ORDER]
    total_loss = lax.psum(loss[0, 0], ("x", "y", "c"))
    return (total_loss, grad_x[None], *outs)
```
